```python
import math
import jax, jax.numpy as jnp
from jax import lax
import numpy as np

D_MODEL = 1024
BATCH = 32
SEQ = 2048
DEPTH = 2

D_MIX = D_MODEL
D_A = D_MIX // 2
D_B = D_MIX - D_A
N_HEADS_A = 8
HEAD_DIM_A = D_A // N_HEADS_A
N_GROUPS_B = 8
CHUNK = 128
CONV_WIDTH = 3
D_FF = 2816
N_MOD = 9
D_IN_PROJ = 2 * D_A + 3 * D_B
EPS = 1e-6

kernel_name = "hybrid_sgu_shortconv_macaron_adaln"


def rms_norm(x, g):
    xf = x.astype(jnp.float32)
    y = xf * lax.rsqrt(jnp.mean(xf * xf, axis=-1, keepdims=True) + EPS)
    return (y * g.astype(jnp.float32)).astype(x.dtype)


def layer_norm(x, g, b):
    xf = x.astype(jnp.float32)
    mu = jnp.mean(xf, axis=-1, keepdims=True)
    var = jnp.mean(jnp.square(xf - mu), axis=-1, keepdims=True)
    y = (xf - mu) * lax.rsqrt(var + EPS)
    return (y * g.astype(jnp.float32) + b.astype(jnp.float32)).astype(x.dtype)


def modulate(h, shift, scale):
    return h * (1 + scale[:, None, :]) + shift[:, None, :]


def swiglu_ffn(h, w_gu, w_down):
    gu = jnp.einsum('bsd,df->bsf', h, w_gu)
    g, u = jnp.split(gu, 2, axis=-1)
    return jnp.einsum('bsf,fd->bsd', jax.nn.silu(g) * u, w_down)


def chunked_sgu(u, v, ln_g, ln_b, w_s, b_s):
    bsz, s, _ = v.shape
    n_chunks = s // CHUNK
    v = layer_norm(v.reshape(bsz, s, N_HEADS_A, HEAD_DIM_A), ln_g, ln_b)
    v = v.reshape(bsz, n_chunks, CHUNK, N_HEADS_A, HEAD_DIM_A)
    causal = jnp.tril(jnp.ones((CHUNK, CHUNK), dtype=bool))
    w_masked = jnp.where(causal[None], w_s, jnp.zeros_like(w_s))
    mixed = jnp.einsum('hts,bcshd->bcthd', w_masked, v)
    mixed = mixed + jnp.transpose(b_s)[None, None, :, :, None]
    return u * mixed.reshape(bsz, s, D_A)


def short_gated_conv(b_gate, c_gate, xb, conv_w):
    s = xb.shape[1]
    z = c_gate * xb
    zp = jnp.pad(z, ((0, 0), (CONV_WIDTH - 1, 0), (0, 0)))
    conv = zp[:, 0:s] * conv_w[0] + zp[:, 1:s + 1] * conv_w[1] + zp[:, 2:s + 2] * conv_w[2]
    return b_gate * conv


def _fwd_setup_inputs(seed: int = 0) -> dict:
    key = jax.random.key(seed)
    ks = jax.random.split(key, 24)

    def nrm(k, shape, scale):
        return scale * jax.random.normal(k, shape, jnp.float32)

    def gain(k, shape):
        return 1.0 + 0.02 * jax.random.normal(k, shape, jnp.float32)

    return {
        "x": nrm(ks[0], (BATCH, SEQ, D_MODEL), 1.0),
        "c": nrm(ks[1], (BATCH, D_MODEL), 1.0),
        "ada_w": nrm(ks[2], (DEPTH, D_MODEL, N_MOD * D_MODEL), 0.5 * D_MODEL ** -0.5),
        "ada_b": nrm(ks[3], (DEPTH, N_MOD * D_MODEL), 0.01),
        "norm_ffn1_g": gain(ks[4], (DEPTH, D_MODEL)),
        "ffn1_w_gu": nrm(ks[5], (DEPTH, D_MODEL, 2 * D_FF), D_MODEL ** -0.5),
        "ffn1_w_down": nrm(ks[6], (DEPTH, D_FF, D_MODEL), D_FF ** -0.5),
        "norm_mix_g": gain(ks[7], (DEPTH, D_MODEL)),
        "mix_w_in": nrm(ks[8], (DEPTH, D_MODEL, D_IN_PROJ), D_MODEL ** -0.5),
        "sgu_ln_g": gain(ks[9], (DEPTH, HEAD_DIM_A)),
        "sgu_ln_b": nrm(ks[10], (DEPTH, HEAD_DIM_A), 0.02),
        "sgu_w_s": nrm(ks[11], (DEPTH, N_HEADS_A, CHUNK, CHUNK), CHUNK ** -0.5),
        "sgu_b": gain(ks[12], (DEPTH, N_HEADS_A, CHUNK)),
        "conv_w": nrm(ks[13], (DEPTH, CONV_WIDTH, D_B), CONV_WIDTH ** -0.5),
        "out_norm_g": gain(ks[14], (DEPTH, D_MIX)),
        "mix_w_out": nrm(ks[15], (DEPTH, D_MIX, D_MODEL), D_MIX ** -0.5),
        "norm_ffn2_g": gain(ks[16], (DEPTH, D_MODEL)),
        "ffn2_w_gu": nrm(ks[17], (DEPTH, D_MODEL, 2 * D_FF), D_MODEL ** -0.5),
        "ffn2_w_down": nrm(ks[18], (DEPTH, D_FF, D_MODEL), D_FF ** -0.5),
        "final_norm_g": gain(ks[19], (D_MODEL,)),
    }


def _fwd_reference(x, c, ada_w, ada_b, norm_ffn1_g, ffn1_w_gu, ffn1_w_down, norm_mix_g,
              mix_w_in, sgu_ln_g, sgu_ln_b, sgu_w_s, sgu_b, conv_w, out_norm_g,
              mix_w_out, norm_ffn2_g, ffn2_w_gu, ffn2_w_down, final_norm_g):
    c_act = jax.nn.silu(c)
    split_points = [D_A, 2 * D_A, 2 * D_A + D_B, 2 * D_A + 2 * D_B]
    for l in range(DEPTH):
        ada = jnp.einsum('bd,de->be', c_act, ada_w[l]) + ada_b[l]
        (sh1, sc1, g1, sh2, sc2, g2, sh3, sc3, g3) = jnp.split(ada, N_MOD, axis=-1)

        h = modulate(rms_norm(x, norm_ffn1_g[l]), sh1, sc1)
        x = x + 0.5 * g1[:, None, :] * swiglu_ffn(h, ffn1_w_gu[l], ffn1_w_down[l])

        h = modulate(rms_norm(x, norm_mix_g[l]), sh2, sc2)
        proj = jnp.einsum('bsd,de->bse', h, mix_w_in[l])
        u_a, v_a, b_gate, c_gate, xb = jnp.split(proj, split_points, axis=-1)
        y_a = chunked_sgu(jax.nn.gelu(u_a, approximate=False), jax.nn.gelu(v_a, approximate=False),
                          sgu_ln_g[l], sgu_ln_b[l], sgu_w_s[l], sgu_b[l])
        y_b = short_gated_conv(b_gate, c_gate, xb, conv_w[l])
        y = jnp.concatenate([rms_norm(y_a, out_norm_g[l, :D_A]),
                             rms_norm(y_b, out_norm_g[l, D_A:])], axis=-1)
        x = x + g2[:, None, :] * jnp.einsum('bse,ed->bsd', y, mix_w_out[l])

        h = modulate(rms_norm(x, norm_ffn2_g[l]), sh3, sc3)
        x = x + 0.5 * g3[:, None, :] * swiglu_ffn(h, ffn2_w_gu[l], ffn2_w_down[l])
    return rms_norm(x, final_norm_g)


import jax as _jax
import jax.numpy as _jnp

TWIN_FORMAT = 'train_step'
FWD_PARAMS = ['x', 'c', 'ada_w', 'ada_b', 'norm_ffn1_g', 'ffn1_w_gu', 'ffn1_w_down', 'norm_mix_g', 'mix_w_in', 'sgu_ln_g', 'sgu_ln_b', 'sgu_w_s', 'sgu_b', 'conv_w', 'out_norm_g', 'mix_w_out', 'norm_ffn2_g', 'ffn2_w_gu', 'ffn2_w_down', 'final_norm_g']
TWIN_WEIGHTS = ['ada_w', 'ada_b', 'norm_ffn1_g', 'ffn1_w_gu', 'ffn1_w_down', 'norm_mix_g', 'mix_w_in', 'sgu_ln_g', 'sgu_ln_b', 'sgu_w_s', 'sgu_b', 'conv_w', 'out_norm_g', 'mix_w_out', 'norm_ffn2_g', 'ffn2_w_gu', 'ffn2_w_down', 'final_norm_g']
TWIN_DIFF_INPUT = 'x'
TWIN_INPUTS = ['x', 'c', 'ada_w', 'ada_b', 'norm_ffn1_g', 'ffn1_w_gu', 'ffn1_w_down', 'norm_mix_g', 'mix_w_in', 'sgu_ln_g', 'sgu_ln_b', 'sgu_w_s', 'sgu_b', 'conv_w', 'out_norm_g', 'mix_w_out', 'norm_ffn2_g', 'ffn2_w_gu', 'ffn2_w_down', 'final_norm_g', 'loss_target', 'm_ada_w', 'm_ada_b', 'm_norm_ffn1_g', 'm_ffn1_w_gu', 'm_ffn1_w_down', 'm_norm_mix_g', 'm_mix_w_in', 'm_sgu_ln_g', 'm_sgu_ln_b', 'm_sgu_w_s', 'm_sgu_b', 'm_conv_w', 'm_out_norm_g', 'm_mix_w_out', 'm_norm_ffn2_g', 'm_ffn2_w_gu', 'm_ffn2_w_down', 'm_final_norm_g', 'v_ada_w', 'v_ada_b', 'v_norm_ffn1_g', 'v_ffn1_w_gu', 'v_ffn1_w_down', 'v_norm_mix_g', 'v_mix_w_in', 'v_sgu_ln_g', 'v_sgu_ln_b', 'v_sgu_w_s', 'v_sgu_b', 'v_conv_w', 'v_out_norm_g', 'v_mix_w_out', 'v_norm_ffn2_g', 'v_ffn2_w_gu', 'v_ffn2_w_down', 'v_final_norm_g']
TWIN_OUTPUTS = ['loss', 'grad_x', 'grad_ada_w', 'grad_ada_b', 'grad_norm_ffn1_g', 'grad_ffn1_w_gu', 'grad_ffn1_w_down', 'grad_norm_mix_g', 'grad_mix_w_in', 'grad_sgu_ln_g', 'grad_sgu_ln_b', 'grad_sgu_w_s', 'grad_sgu_b', 'grad_conv_w', 'grad_out_norm_g', 'grad_mix_w_out', 'grad_norm_ffn2_g', 'grad_ffn2_w_gu', 'grad_ffn2_w_down', 'grad_final_norm_g', 'delta_ada_w', 'delta_ada_b', 'delta_norm_ffn1_g', 'delta_ffn1_w_gu', 'delta_ffn1_w_down', 'delta_norm_mix_g', 'delta_mix_w_in', 'delta_sgu_ln_g', 'delta_sgu_ln_b', 'delta_sgu_w_s', 'delta_sgu_b', 'delta_conv_w', 'delta_out_norm_g', 'delta_mix_w_out', 'delta_norm_ffn2_g', 'delta_ffn2_w_gu', 'delta_ffn2_w_down', 'delta_final_norm_g', 'new_m_ada_w', 'new_m_ada_b', 'new_m_norm_ffn1_g', 'new_m_ffn1_w_gu', 'new_m_ffn1_w_down', 'new_m_norm_mix_g', 'new_m_mix_w_in', 'new_m_sgu_ln_g', 'new_m_sgu_ln_b', 'new_m_sgu_w_s', 'new_m_sgu_b', 'new_m_conv_w', 'new_m_out_norm_g', 'new_m_mix_w_out', 'new_m_norm_ffn2_g', 'new_m_ffn2_w_gu', 'new_m_ffn2_w_down', 'new_m_final_norm_g', 'new_v_ada_w', 'new_v_ada_b', 'new_v_norm_ffn1_g', 'new_v_ffn1_w_gu', 'new_v_ffn1_w_down', 'new_v_norm_mix_g', 'new_v_mix_w_in', 'new_v_sgu_ln_g', 'new_v_sgu_ln_b', 'new_v_sgu_w_s', 'new_v_sgu_b', 'new_v_conv_w', 'new_v_out_norm_g', 'new_v_mix_w_out', 'new_v_norm_ffn2_g', 'new_v_ffn2_w_gu', 'new_v_ffn2_w_down', 'new_v_final_norm_g']
TWIN_LEAF_KINDS = {'loss': 'loss', 'grad_x': 'grad_x', 'grad_ada_w': 'grad_w', 'grad_ada_b': 'grad_w', 'grad_norm_ffn1_g': 'grad_w', 'grad_ffn1_w_gu': 'grad_w', 'grad_ffn1_w_down': 'grad_w', 'grad_norm_mix_g': 'grad_w', 'grad_mix_w_in': 'grad_w', 'grad_sgu_ln_g': 'grad_w', 'grad_sgu_ln_b': 'grad_w', 'grad_sgu_w_s': 'grad_w', 'grad_sgu_b': 'grad_w', 'grad_conv_w': 'grad_w', 'grad_out_norm_g': 'grad_w', 'grad_mix_w_out': 'grad_w', 'grad_norm_ffn2_g': 'grad_w', 'grad_ffn2_w_gu': 'grad_w', 'grad_ffn2_w_down': 'grad_w', 'grad_final_norm_g': 'grad_w', 'delta_ada_w': 'delta_w', 'delta_ada_b': 'delta_w', 'delta_norm_ffn1_g': 'delta_w', 'delta_ffn1_w_gu': 'delta_w', 'delta_ffn1_w_down': 'delta_w', 'delta_norm_mix_g': 'delta_w', 'delta_mix_w_in': 'delta_w', 'delta_sgu_ln_g': 'delta_w', 'delta_sgu_ln_b': 'delta_w', 'delta_sgu_w_s': 'delta_w', 'delta_sgu_b': 'delta_w', 'delta_conv_w': 'delta_w', 'delta_out_norm_g': 'delta_w', 'delta_mix_w_out': 'delta_w', 'delta_norm_ffn2_g': 'delta_w', 'delta_ffn2_w_gu': 'delta_w', 'delta_ffn2_w_down': 'delta_w', 'delta_final_norm_g': 'delta_w', 'new_m_ada_w': 'new_m', 'new_m_ada_b': 'new_m', 'new_m_norm_ffn1_g': 'new_m', 'new_m_ffn1_w_gu': 'new_m', 'new_m_ffn1_w_down': 'new_m', 'new_m_norm_mix_g': 'new_m', 'new_m_mix_w_in': 'new_m', 'new_m_sgu_ln_g': 'new_m', 'new_m_sgu_ln_b': 'new_m', 'new_m_sgu_w_s': 'new_m', 'new_m_sgu_b': 'new_m', 'new_m_conv_w': 'new_m', 'new_m_out_norm_g': 'new_m', 'new_m_mix_w_out': 'new_m', 'new_m_norm_ffn2_g': 'new_m', 'new_m_ffn2_w_gu': 'new_m', 'new_m_ffn2_w_down': 'new_m', 'new_m_final_norm_g': 'new_m', 'new_v_ada_w': 'new_v', 'new_v_ada_b': 'new_v', 'new_v_norm_ffn1_g': 'new_v', 'new_v_ffn1_w_gu': 'new_v', 'new_v_ffn1_w_down': 'new_v', 'new_v_norm_mix_g': 'new_v', 'new_v_mix_w_in': 'new_v', 'new_v_sgu_ln_g': 'new_v', 'new_v_sgu_ln_b': 'new_v', 'new_v_sgu_w_s': 'new_v', 'new_v_sgu_b': 'new_v', 'new_v_conv_w': 'new_v', 'new_v_out_norm_g': 'new_v', 'new_v_mix_w_out': 'new_v', 'new_v_norm_ffn2_g': 'new_v', 'new_v_ffn2_w_gu': 'new_v', 'new_v_ffn2_w_down': 'new_v', 'new_v_final_norm_g': 'new_v'}


def _forward(args):
    return _fwd_reference(*[args[k] for k in FWD_PARAMS])


def _output_shape():
    out = _jax.eval_shape(lambda: _forward(_fwd_setup_inputs(0)))
    return out.shape, out.dtype

N_MICROBATCH = 1
ADAM_LR = 0.001
ADAM_B1 = 0.9
ADAM_B2 = 0.999
ADAM_EPS = 1e-08
ADAM_WD = 0.01
ADAM_STEP = 10
PER_EXAMPLE_BATCH_AXIS = {'x': 0, 'c': 0, 'loss_target': 0}
SHARED_INPUTS = []
_WEIGHT_DTYPES = {'ada_w': _jnp.float32, 'ada_b': _jnp.float32, 'norm_ffn1_g': _jnp.float32, 'ffn1_w_gu': _jnp.float32, 'ffn1_w_down': _jnp.float32, 'norm_mix_g': _jnp.float32, 'mix_w_in': _jnp.float32, 'sgu_ln_g': _jnp.float32, 'sgu_ln_b': _jnp.float32, 'sgu_w_s': _jnp.float32, 'sgu_b': _jnp.float32, 'conv_w': _jnp.float32, 'out_norm_g': _jnp.float32, 'mix_w_out': _jnp.float32, 'norm_ffn2_g': _jnp.float32, 'ffn2_w_gu': _jnp.float32, 'ffn2_w_down': _jnp.float32, 'final_norm_g': _jnp.float32}
MOMENT_SCALE = {'ada_w': 6.919208e-02, 'ada_b': 1.105591e-01, 'norm_ffn1_g': 4.080853e-02, 'ffn1_w_gu': 1.770929e-02, 'ffn1_w_down': 2.887092e-02, 'norm_mix_g': 1.059891e-01, 'mix_w_in': 7.039974e-02, 'sgu_ln_g': 1.128207e-01, 'sgu_ln_b': 1.169421e-01, 'sgu_w_s': 2.886494e-02, 'sgu_b': 4.166875e-02, 'conv_w': 7.437074e-02, 'out_norm_g': 7.905488e-02, 'mix_w_out': 7.444377e-02, 'norm_ffn2_g': 3.714771e-02, 'ffn2_w_gu': 1.626633e-02, 'ffn2_w_down': 2.646392e-02, 'final_norm_g': 6.412801e+01}


def _to_microbatches(a, axis):
    t = _jnp.moveaxis(a, axis, 0)
    t = t.reshape((N_MICROBATCH, t.shape[0] // N_MICROBATCH) + t.shape[1:])
    return _jnp.moveaxis(t, 1, axis + 1)


def setup_inputs(seed: int = 0) -> dict:
    inp = _fwd_setup_inputs(seed)
    key = _jax.random.fold_in(_jax.random.key(seed), 7919)
    shape, _ = _output_shape()
    out = dict(inp)
    out["loss_target"] = _jax.random.normal(_jax.random.fold_in(key, 0), shape, _jnp.float32)
    for i, name in enumerate(TWIN_WEIGHTS):
        w = inp[name].astype(_jnp.float32)
        if MOMENT_SCALE is None:
            s = _jnp.sqrt(_jnp.mean(_jnp.square(w)) + 1e-30)
        else:
            s = MOMENT_SCALE[name]
        km, kv = _jax.random.split(_jax.random.fold_in(key, i + 1))
        out[name] = w
        out["m_" + name] = s * _jax.random.normal(km, w.shape, _jnp.float32)
        out["v_" + name] = (s * s) * _jax.random.uniform(kv, w.shape, _jnp.float32, 0.5, 1.5)
    if N_MICROBATCH > 1:
        for name, axis in PER_EXAMPLE_BATCH_AXIS.items():
            out[name] = _to_microbatches(out[name], axis)
    return {'x': out['x'], 'c': out['c'], 'ada_w': out['ada_w'], 'ada_b': out['ada_b'], 'norm_ffn1_g': out['norm_ffn1_g'], 'ffn1_w_gu': out['ffn1_w_gu'], 'ffn1_w_down': out['ffn1_w_down'], 'norm_mix_g': out['norm_mix_g'], 'mix_w_in': out['mix_w_in'], 'sgu_ln_g': out['sgu_ln_g'], 'sgu_ln_b': out['sgu_ln_b'], 'sgu_w_s': out['sgu_w_s'], 'sgu_b': out['sgu_b'], 'conv_w': out['conv_w'], 'out_norm_g': out['out_norm_g'], 'mix_w_out': out['mix_w_out'], 'norm_ffn2_g': out['norm_ffn2_g'], 'ffn2_w_gu': out['ffn2_w_gu'], 'ffn2_w_down': out['ffn2_w_down'], 'final_norm_g': out['final_norm_g'], 'loss_target': out['loss_target'], 'm_ada_w': out['m_ada_w'], 'm_ada_b': out['m_ada_b'], 'm_norm_ffn1_g': out['m_norm_ffn1_g'], 'm_ffn1_w_gu': out['m_ffn1_w_gu'], 'm_ffn1_w_down': out['m_ffn1_w_down'], 'm_norm_mix_g': out['m_norm_mix_g'], 'm_mix_w_in': out['m_mix_w_in'], 'm_sgu_ln_g': out['m_sgu_ln_g'], 'm_sgu_ln_b': out['m_sgu_ln_b'], 'm_sgu_w_s': out['m_sgu_w_s'], 'm_sgu_b': out['m_sgu_b'], 'm_conv_w': out['m_conv_w'], 'm_out_norm_g': out['m_out_norm_g'], 'm_mix_w_out': out['m_mix_w_out'], 'm_norm_ffn2_g': out['m_norm_ffn2_g'], 'm_ffn2_w_gu': out['m_ffn2_w_gu'], 'm_ffn2_w_down': out['m_ffn2_w_down'], 'm_final_norm_g': out['m_final_norm_g'], 'v_ada_w': out['v_ada_w'], 'v_ada_b': out['v_ada_b'], 'v_norm_ffn1_g': out['v_norm_ffn1_g'], 'v_ffn1_w_gu': out['v_ffn1_w_gu'], 'v_ffn1_w_down': out['v_ffn1_w_down'], 'v_norm_mix_g': out['v_norm_mix_g'], 'v_mix_w_in': out['v_mix_w_in'], 'v_sgu_ln_g': out['v_sgu_ln_g'], 'v_sgu_ln_b': out['v_sgu_ln_b'], 'v_sgu_w_s': out['v_sgu_w_s'], 'v_sgu_b': out['v_sgu_b'], 'v_conv_w': out['v_conv_w'], 'v_out_norm_g': out['v_out_norm_g'], 'v_mix_w_out': out['v_mix_w_out'], 'v_norm_ffn2_g': out['v_norm_ffn2_g'], 'v_ffn2_w_gu': out['v_ffn2_w_gu'], 'v_ffn2_w_down': out['v_ffn2_w_down'], 'v_final_norm_g': out['v_final_norm_g']}


def _loss(weights, diff, rest, loss_target):
    with _jax.named_scope("forward"):
        args = {**rest, TWIN_DIFF_INPUT: diff, **{k: w.astype(_WEIGHT_DTYPES[k]) for k, w in weights.items()}}
        y = _forward(args)
    with _jax.named_scope("loss_head"):
        err = _jnp.square(y.astype(_jnp.float32) - loss_target)
        return 0.5 * _jnp.sum(_jnp.mean(err, axis=-1)) if err.ndim else 0.5 * err


def _adamw(w, g, m, v):
    m = ADAM_B1 * m + (1.0 - ADAM_B1) * g
    v = ADAM_B2 * v + (1.0 - ADAM_B2) * _jnp.square(g)
    m_hat = m / (1.0 - ADAM_B1 ** ADAM_STEP)
    v_hat = v / (1.0 - ADAM_B2 ** ADAM_STEP)
    delta = -ADAM_LR * (m_hat / (_jnp.sqrt(v_hat) + ADAM_EPS) + ADAM_WD * w)
    return delta, m, v


def reference(x, c, ada_w, ada_b, norm_ffn1_g, ffn1_w_gu, ffn1_w_down, norm_mix_g, mix_w_in, sgu_ln_g, sgu_ln_b, sgu_w_s, sgu_b, conv_w, out_norm_g, mix_w_out, norm_ffn2_g, ffn2_w_gu, ffn2_w_down, final_norm_g, loss_target, m_ada_w, m_ada_b, m_norm_ffn1_g, m_ffn1_w_gu, m_ffn1_w_down, m_norm_mix_g, m_mix_w_in, m_sgu_ln_g, m_sgu_ln_b, m_sgu_w_s, m_sgu_b, m_conv_w, m_out_norm_g, m_mix_w_out, m_norm_ffn2_g, m_ffn2_w_gu, m_ffn2_w_down, m_final_norm_g, v_ada_w, v_ada_b, v_norm_ffn1_g, v_ffn1_w_gu, v_ffn1_w_down, v_norm_mix_g, v_mix_w_in, v_sgu_ln_g, v_sgu_ln_b, v_sgu_w_s, v_sgu_b, v_conv_w, v_out_norm_g, v_mix_w_out, v_norm_ffn2_g, v_ffn2_w_gu, v_ffn2_w_down, v_final_norm_g):
    given = dict(x=x, c=c, ada_w=ada_w, ada_b=ada_b, norm_ffn1_g=norm_ffn1_g, ffn1_w_gu=ffn1_w_gu, ffn1_w_down=ffn1_w_down, norm_mix_g=norm_mix_g, mix_w_in=mix_w_in, sgu_ln_g=sgu_ln_g, sgu_ln_b=sgu_ln_b, sgu_w_s=sgu_w_s, sgu_b=sgu_b, conv_w=conv_w, out_norm_g=out_norm_g, mix_w_out=mix_w_out, norm_ffn2_g=norm_ffn2_g, ffn2_w_gu=ffn2_w_gu, ffn2_w_down=ffn2_w_down, final_norm_g=final_norm_g, loss_target=loss_target, m_ada_w=m_ada_w, m_ada_b=m_ada_b, m_norm_ffn1_g=m_norm_ffn1_g, m_ffn1_w_gu=m_ffn1_w_gu, m_ffn1_w_down=m_ffn1_w_down, m_norm_mix_g=m_norm_mix_g, m_mix_w_in=m_mix_w_in, m_sgu_ln_g=m_sgu_ln_g, m_sgu_ln_b=m_sgu_ln_b, m_sgu_w_s=m_sgu_w_s, m_sgu_b=m_sgu_b, m_conv_w=m_conv_w, m_out_norm_g=m_out_norm_g, m_mix_w_out=m_mix_w_out, m_norm_ffn2_g=m_norm_ffn2_g, m_ffn2_w_gu=m_ffn2_w_gu, m_ffn2_w_down=m_ffn2_w_down, m_final_norm_g=m_final_norm_g, v_ada_w=v_ada_w, v_ada_b=v_ada_b, v_norm_ffn1_g=v_norm_ffn1_g, v_ffn1_w_gu=v_ffn1_w_gu, v_ffn1_w_down=v_ffn1_w_down, v_norm_mix_g=v_norm_mix_g, v_mix_w_in=v_mix_w_in, v_sgu_ln_g=v_sgu_ln_g, v_sgu_ln_b=v_sgu_ln_b, v_sgu_w_s=v_sgu_w_s, v_sgu_b=v_sgu_b, v_conv_w=v_conv_w, v_out_norm_g=v_out_norm_g, v_mix_w_out=v_mix_w_out, v_norm_ffn2_g=v_norm_ffn2_g, v_ffn2_w_gu=v_ffn2_w_gu, v_ffn2_w_down=v_ffn2_w_down, v_final_norm_g=v_final_norm_g)
    weights = {n: given[n] for n in TWIN_WEIGHTS}
    shared = {n: given[n] for n in SHARED_INPUTS}
    per_example = {n: given[n] for n in ['x', 'c']}
    grad_fn = _jax.value_and_grad(_loss, argnums=(0, 1))

    def one_microbatch(ex, loss_target):
        ex = dict(ex)
        diff = ex.pop(TWIN_DIFF_INPUT)
        return grad_fn(weights, diff, {**shared, **ex}, loss_target)

    if N_MICROBATCH == 1:
        loss, (grad_w, grad_x) = one_microbatch(per_example, given["loss_target"])
    else:
        def body(carry, xs):
            loss_sum, grad_sum = carry
            l_k, (gw_k, gx_k) = one_microbatch(xs[0], xs[1])
            with _jax.named_scope("update"):
                return (loss_sum + l_k, _jax.tree.map(_jnp.add, grad_sum, gw_k)), gx_k

        init = (_jnp.zeros((), _jnp.float32), _jax.tree.map(_jnp.zeros_like, weights))
        (loss, grad_w), grad_x = _jax.lax.scan(body, init, (per_example, given["loss_target"]))
    with _jax.named_scope("update"):
        delta_w, new_m, new_v = {}, {}, {}
        for n in TWIN_WEIGHTS:
            delta_w[n], new_m[n], new_v[n] = _adamw(weights[n], grad_w[n], given["m_" + n], given["v_" + n])
    return (loss, grad_x, *[grad_w[n] for n in TWIN_WEIGHTS], *[delta_w[n] for n in TWIN_WEIGHTS],
            *[new_m[n] for n in TWIN_WEIGHTS], *[new_v[n] for n in TWIN_WEIGHTS])
```

```python
import functools
import math

import jax
import jax.numpy as jnp
from jax import lax
from jax.experimental import pallas as pl
from jax.experimental.pallas import tpu as pltpu

F32 = jnp.float32
BF16 = jnp.bfloat16

D_MODEL = 1024
D_FF = 2816
D_A = 512
D_PROJ = 2560
N_HEADS = 8
HEAD_DIM = 64
CHUNK = 128
N_MOD = 9
DEPTH = 2
EPS = 1e-6
N_DEV = 8
LANES = 128
MXU_N = 256
HALO = 16
VMEM_LIMIT = 56 * 1024 * 1024

ADAM_LR = 0.001
ADAM_B1 = 0.9
ADAM_B2 = 0.999
ADAM_EPS = 1e-08
ADAM_WD = 0.01
ADAM_STEP = 10

MESH = pl.DeviceIdType.MESH


def _dot(a, b):
    return jnp.dot(a, b, preferred_element_type=F32)


def _dot_nt(a, b):
    return lax.dot_general(a, b, (((1,), (1,)), ((), ())), preferred_element_type=F32)


def _dot_tn(a, b):
    return lax.dot_general(a, b, (((0,), (0,)), ((), ())), preferred_element_type=F32)


def _sigmoid(x):
    return 1.0 / (1.0 + jnp.exp(-x))


def _gelu(x):
    return 0.5 * x * (1.0 + lax.erf(x * (1.0 / math.sqrt(2.0))))


def _gelu_grad(x):
    cdf = 0.5 * (1.0 + lax.erf(x * (1.0 / math.sqrt(2.0))))
    return cdf + x * jnp.exp(-0.5 * x * x) * (1.0 / math.sqrt(2.0 * math.pi))


def _params(n_axes=1, parallel=False):
    sem = ("parallel" if parallel else "arbitrary",) * n_axes
    return pltpu.CompilerParams(dimension_semantics=sem, vmem_limit_bytes=VMEM_LIMIT)


def _resident(shape):
    nd = len(shape)
    return pl.BlockSpec(shape, lambda *_: (0,) * nd, pipeline_mode=pl.Buffered(1))


def _tile_rows(seq):
    return min(512, seq)


def _my_position():
    x, y, c = lax.axis_index("x"), lax.axis_index("y"), lax.axis_index("c")
    return x, y, c, 4 * x + 2 * y + c


def _peer(x, y, c, p):
    return (x ^ ((p >> 2) & 1), y ^ ((p >> 1) & 1), c ^ (p & 1))


def _all_gather_rows(shards, name):
    n = len(shards)

    def body(*refs):
        src, dst = refs[:n], refs[n:2 * n]
        send_sems, recv_sems, local_sems = refs[2 * n:]
        x, y, c, me = _my_position()
        copies = []
        for k in range(n):
            rows = shards[k].shape[0]
            mine = dst[k].at[pl.ds(me * rows, rows), :]
            local = pltpu.make_async_copy(src[k], mine, local_sems.at[k])
            local.start()
            copies.append(local)
            for p in range(1, N_DEV):
                cp = pltpu.make_async_remote_copy(
                    src_ref=src[k], dst_ref=mine, send_sem=send_sems.at[k, p - 1], recv_sem=recv_sems.at[k, p - 1],
                    device_id=_peer(x, y, c, p), device_id_type=MESH)
                cp.start()
                copies.append(cp)
        for cp in copies:
            cp.wait()

    return pl.pallas_call(
        body, name=name,
        out_shape=[jax.ShapeDtypeStruct((N_DEV * s.shape[0], s.shape[1]), s.dtype) for s in shards],
        in_specs=[pl.BlockSpec(memory_space=pl.ANY)] * n,
        out_specs=[pl.BlockSpec(memory_space=pl.ANY)] * n,
        scratch_shapes=[pltpu.SemaphoreType.DMA((n, N_DEV - 1)), pltpu.SemaphoreType.DMA((n, N_DEV - 1)),
                        pltpu.SemaphoreType.DMA((n,))],
    )(*shards)


def _all_to_all_rows(grads, name):
    n = len(grads)

    def body(*refs):
        src, dst = refs[:n], refs[n:2 * n]
        send_sems, recv_sems, local_sems = refs[2 * n:]
        x, y, c, me = _my_position()
        copies = []
        for k in range(n):
            rows = grads[k].shape[0] // N_DEV
            local = pltpu.make_async_copy(src[k].at[pl.ds(me * rows, rows), :], dst[k].at[me], local_sems.at[k])
            local.start()
            copies.append(local)
            for p in range(1, N_DEV):
                px, py, pc = _peer(x, y, c, p)
                peer = 4 * px + 2 * py + pc
                cp = pltpu.make_async_remote_copy(
                    src_ref=src[k].at[pl.ds(peer * rows, rows), :], dst_ref=dst[k].at[me],
                    send_sem=send_sems.at[k, p - 1], recv_sem=recv_sems.at[k, p - 1],
                    device_id=(px, py, pc), device_id_type=MESH)
                cp.start()
                copies.append(cp)
        for cp in copies:
            cp.wait()

    return pl.pallas_call(
        body, name=name,
        out_shape=[jax.ShapeDtypeStruct((N_DEV, g.shape[0] // N_DEV, g.shape[1]), g.dtype) for g in grads],
        in_specs=[pl.BlockSpec(memory_space=pl.ANY)] * n,
        out_specs=[pl.BlockSpec(memory_space=pl.ANY)] * n,
        scratch_shapes=[pltpu.SemaphoreType.DMA((n, N_DEV - 1)), pltpu.SemaphoreType.DMA((n, N_DEV - 1)),
                        pltpu.SemaphoreType.DMA((n,))],
    )(*grads)


def _all_gather_small(v, name):
    rows = v.shape[0]

    def body(v_ref, all_ref, sum_ref, send_sems, recv_sems):
        x, y, c, me = _my_position()
        all_ref[me] = v_ref[...]
        copies = []
        for p in range(1, N_DEV):
            cp = pltpu.make_async_remote_copy(
                src_ref=v_ref, dst_ref=all_ref.at[me], send_sem=send_sems.at[p - 1], recv_sem=recv_sems.at[p - 1],
                device_id=_peer(x, y, c, p), device_id_type=MESH)
            cp.start()
            copies.append(cp)
        for cp in copies:
            cp.wait()
        acc = all_ref[0]
        for d in range(1, N_DEV):
            acc = acc + all_ref[d]
        sum_ref[...] = acc

    return pl.pallas_call(
        body, name=name,
        out_shape=[jax.ShapeDtypeStruct((N_DEV, rows, LANES), F32), jax.ShapeDtypeStruct((rows, LANES), F32)],
        in_specs=[pl.BlockSpec(memory_space=pltpu.VMEM)],
        out_specs=[pl.BlockSpec(memory_space=pltpu.VMEM)] * 2,
        scratch_shapes=[pltpu.SemaphoreType.DMA((N_DEV - 1,)), pltpu.SemaphoreType.DMA((N_DEV - 1,))],
        compiler_params=pltpu.CompilerParams(vmem_limit_bytes=VMEM_LIMIT),
    )(v)


def _pack_small(parts):
    flat = jnp.concatenate([p.reshape(-1).astype(F32) for p in parts])
    total = flat.shape[0]
    padded = -(-total // (8 * LANES)) * (8 * LANES)
    flat = jnp.pad(flat, (0, padded - total))
    return flat.reshape(padded // LANES, LANES)


def _unpack_small(packed, shapes, lead=()):
    flat = packed.reshape(lead + (-1,))
    out, off = [], 0
    for shp in shapes:
        size = math.prod(shp)
        out.append(flat[..., off:off + size].reshape(lead + tuple(shp)))
        off += size
    return out


def _ada_forward(c_all, ada_w, ada_b_cols):
    nb = c_all.shape[0]
    cols = ada_w.shape[2]

    def body(c_ref, w_ref, b_ref, o_ref):
        cv = c_ref[...]
        act = (cv * _sigmoid(cv)).astype(BF16)
        o_ref[0] = _dot(act, w_ref[0].astype(BF16)) + b_ref[0]

    return pl.pallas_call(
        body, name="ada_forward", grid=(DEPTH,),
        out_shape=jax.ShapeDtypeStruct((DEPTH, nb, cols), F32),
        in_specs=[pl.BlockSpec((nb, D_MODEL), lambda l: (0, 0)),
                  pl.BlockSpec((1, D_MODEL, cols), lambda l: (l, 0, 0)),
                  pl.BlockSpec((1, 1, cols), lambda l: (l, 0, 0))],
        out_specs=pl.BlockSpec((1, nb, cols), lambda l: (l, 0, 0)),
        compiler_params=_params(),
    )(c_all, ada_w, ada_b_cols)


def _ada_backward(c_all, d_ada_cols, d_ada_all):
    nb = c_all.shape[0]
    cols = d_ada_cols.shape[2]
    full = d_ada_all.shape[2]

    def body(c_ref, dc_ref, da_ref, gw_ref, gb_ref):
        cv = c_ref[...]
        act = (cv * _sigmoid(cv)).astype(BF16)
        gw_ref[0] = _dot_tn(act, dc_ref[0].astype(BF16))
        gb_ref[0] = jnp.sum(da_ref[0], axis=0, keepdims=True)

    return pl.pallas_call(
        body, name="ada_backward", grid=(DEPTH,),
        out_shape=[jax.ShapeDtypeStruct((DEPTH, D_MODEL, cols), F32), jax.ShapeDtypeStruct((DEPTH, 1, full), F32)],
        in_specs=[pl.BlockSpec((nb, D_MODEL), lambda l: (0, 0)),
                  pl.BlockSpec((1, nb, cols), lambda l: (l, 0, 0)),
                  pl.BlockSpec((1, nb, full), lambda l: (l, 0, 0))],
        out_specs=[pl.BlockSpec((1, D_MODEL, cols), lambda l: (l, 0, 0)),
                   pl.BlockSpec((1, 1, full), lambda l: (l, 0, 0))],
        compiler_params=_params(),
    )(c_all, d_ada_cols, d_ada_all)


def _rms(xv):
    return lax.rsqrt(jnp.mean(xv * xv, axis=-1, keepdims=True) + EPS)


def _normmod_matmul(x, gnorm, scale1p, shift, w_t, seq, name):
    tokens, n_out = x.shape[0], w_t.shape[0]
    tm = _tile_rows(seq)
    per_seq = seq // tm
    n_chunks = n_out // MXU_N

    def body(x_ref, g_ref, sc_ref, sh_ref, w_ref, h_ref, o_ref):
        xv = x_ref[...]
        h = (xv * _rms(xv) * g_ref[...]) * sc_ref[0] + sh_ref[0]
        h_ref[...] = h.astype(BF16)
        for ck in range(n_chunks):
            cs = slice(ck * MXU_N, (ck + 1) * MXU_N)
            o_ref[:, cs] = _dot_nt(h_ref[...], w_ref[cs, :]).astype(BF16)

    return pl.pallas_call(
        body, name=name, grid=(tokens // tm,),
        out_shape=[jax.ShapeDtypeStruct((tokens, D_MODEL), BF16), jax.ShapeDtypeStruct((tokens, n_out), BF16)],
        in_specs=[pl.BlockSpec((tm, D_MODEL), lambda i: (i, 0)),
                  _resident((1, D_MODEL)),
                  pl.BlockSpec((1, 1, D_MODEL), lambda i: (i // per_seq, 0, 0)),
                  pl.BlockSpec((1, 1, D_MODEL), lambda i: (i // per_seq, 0, 0)),
                  _resident(w_t.shape)],
        out_specs=[pl.BlockSpec((tm, D_MODEL), lambda i: (i, 0)), pl.BlockSpec((tm, n_out), lambda i: (i, 0))],
        compiler_params=_params(parallel=True),
    )(x, gnorm, scale1p, shift, w_t)


def _matmul_residual(src, w, x, gate, scale, swiglu, seq, name):
    tokens, k_dim = x.shape[0], w.shape[0]
    tm = _tile_rows(seq)
    per_seq = seq // tm
    n_chunks = k_dim // MXU_N

    def body(*refs):
        if swiglu:
            g_ref, u_ref, w_ref, x_ref, gate_ref, xo_ref, f_ref, a_scr = refs
            for ck in range(n_chunks):
                cs = slice(ck * MXU_N, (ck + 1) * MXU_N)
                g = g_ref[:, cs].astype(F32)
                a_scr[:, cs] = (g * _sigmoid(g) * u_ref[:, cs].astype(F32)).astype(BF16)
            f = _dot(a_scr[...], w_ref[...])
        else:
            s_ref, w_ref, x_ref, gate_ref, xo_ref, f_ref = refs
            f = _dot(s_ref[...], w_ref[...])
        f_ref[...] = f.astype(BF16)
        xo_ref[...] = x_ref[...] + (scale * gate_ref[0]) * f

    src_specs = ([pl.BlockSpec((tm, k_dim), lambda i: (i, 0)), pl.BlockSpec((tm, k_dim), lambda i: (i, 1))]
                 if swiglu else [pl.BlockSpec((tm, k_dim), lambda i: (i, 0))])
    return pl.pallas_call(
        body, name=name, grid=(tokens // tm,),
        out_shape=[jax.ShapeDtypeStruct((tokens, D_MODEL), F32), jax.ShapeDtypeStruct((tokens, D_MODEL), BF16)],
        in_specs=src_specs + [_resident(w.shape),
                              pl.BlockSpec((tm, D_MODEL), lambda i: (i, 0)),
                              pl.BlockSpec((1, 1, D_MODEL), lambda i: (i // per_seq, 0, 0))],
        out_specs=[pl.BlockSpec((tm, D_MODEL), lambda i: (i, 0))] * 2,
        scratch_shapes=[pltpu.VMEM((tm, k_dim), BF16)] if swiglu else [],
        compiler_params=_params(parallel=True),
    )(*([src, src] if swiglu else [src]), w, x, gate)


def _residual_backward(dy, gate, f, src, w, scale, swiglu, seq, name):
    tokens, k_dim = dy.shape[0], w.shape[0]
    batch = tokens // seq
    tm = _tile_rows(seq)
    per_seq = seq // tm
    n_chunks = k_dim // MXU_N

    def body(*refs):
        if swiglu:
            dy_ref, gate_ref, f_ref, g_ref, u_ref, w_ref, df_ref, dgate_ref, dgu_ref, a_ref = refs
        else:
            dy_ref, gate_ref, f_ref, w_ref, df_ref, dgate_ref, dsrc_ref = refs
        i = pl.program_id(0)
        dy_v = dy_ref[...]
        df_ref[...] = ((scale * gate_ref[0]) * dy_v).astype(BF16)
        part = scale * jnp.sum(dy_v * f_ref[...].astype(F32), axis=0, keepdims=True)

        @pl.when(i % per_seq == 0)
        def _():
            dgate_ref[0] = part

        @pl.when(i % per_seq != 0)
        def _():
            dgate_ref[0] = dgate_ref[0] + part

        if swiglu:
            for ck in range(n_chunks):
                cs = slice(ck * MXU_N, (ck + 1) * MXU_N)
                cu = slice(k_dim + ck * MXU_N, k_dim + (ck + 1) * MXU_N)
                da = _dot_nt(df_ref[...], w_ref[cs, :])
                g = g_ref[:, cs].astype(F32)
                u = u_ref[:, cs].astype(F32)
                sig = _sigmoid(g)
                silu = g * sig
                a_ref[:, cs] = (silu * u).astype(BF16)
                dgu_ref[:, cs] = (da * u * (sig * (1.0 + g * (1.0 - sig)))).astype(BF16)
                dgu_ref[:, cu] = (da * silu).astype(BF16)
        else:
            dsrc_ref[...] = _dot_nt(df_ref[...], w_ref[...])

    row = lambda i: (i, 0)
    per_batch = pl.BlockSpec((1, 1, D_MODEL), lambda i: (i // per_seq, 0, 0))
    in_specs = [pl.BlockSpec((tm, D_MODEL), row), per_batch, pl.BlockSpec((tm, D_MODEL), row)]
    out_shape = [jax.ShapeDtypeStruct((tokens, D_MODEL), BF16), jax.ShapeDtypeStruct((batch, 1, D_MODEL), F32)]
    out_specs = [pl.BlockSpec((tm, D_MODEL), row), per_batch]
    if swiglu:
        in_specs += [pl.BlockSpec((tm, k_dim), lambda i: (i, 0)), pl.BlockSpec((tm, k_dim), lambda i: (i, 1))]
        operands = (dy, gate, f, src, src, w)
        out_shape += [jax.ShapeDtypeStruct((tokens, 2 * k_dim), BF16), jax.ShapeDtypeStruct((tokens, k_dim), BF16)]
        out_specs += [pl.BlockSpec((tm, 2 * k_dim), row), pl.BlockSpec((tm, k_dim), row)]
    else:
        operands = (dy, gate, f, w)
        out_shape += [jax.ShapeDtypeStruct((tokens, k_dim), F32)]
        out_specs += [pl.BlockSpec((tm, k_dim), row)]
    in_specs += [_resident(w.shape)]
    return pl.pallas_call(
        body, name=name, grid=(tokens // tm,), out_shape=out_shape, in_specs=in_specs, out_specs=out_specs,
        compiler_params=_params(),
    )(*operands)


def _matmul_normmod_backward(dsrc, w_t, x, dy, gnorm, scale1p, seq, name):
    tokens, k_dim = dsrc.shape
    batch = tokens // seq
    tm = _tile_rows(seq)
    per_seq = seq // tm

    def body(ds_ref, w_ref, x_ref, dy_ref, g_ref, sc_ref, dx_ref, dsh_ref, dsc_ref, dg_ref):
        i = pl.program_id(0)
        dh = _dot(ds_ref[...], w_ref[...])
        xv = x_ref[...]
        r = _rms(xv)
        xn = xv * r
        gn = g_ref[...]
        dsh = jnp.sum(dh, axis=0, keepdims=True)
        dsc = jnp.sum(dh * (xn * gn), axis=0, keepdims=True)
        dhn = dh * sc_ref[0]
        dg = jnp.sum(dhn * xn, axis=0, keepdims=True)
        dxn = dhn * gn
        dx_ref[...] = dy_ref[...] + r * (dxn - xn * jnp.mean(dxn * xn, axis=-1, keepdims=True))

        @pl.when(i % per_seq == 0)
        def _():
            dsh_ref[0] = dsh
            dsc_ref[0] = dsc

        @pl.when(i % per_seq != 0)
        def _():
            dsh_ref[0] = dsh_ref[0] + dsh
            dsc_ref[0] = dsc_ref[0] + dsc

        @pl.when(i == 0)
        def _():
            dg_ref[...] = dg

        @pl.when(i != 0)
        def _():
            dg_ref[...] = dg_ref[...] + dg

    row = lambda i: (i, 0)
    per_batch = pl.BlockSpec((1, 1, D_MODEL), lambda i: (i // per_seq, 0, 0))
    return pl.pallas_call(
        body, name=name, grid=(tokens // tm,),
        out_shape=[jax.ShapeDtypeStruct((tokens, D_MODEL), F32), jax.ShapeDtypeStruct((batch, 1, D_MODEL), F32),
                   jax.ShapeDtypeStruct((batch, 1, D_MODEL), F32), jax.ShapeDtypeStruct((1, D_MODEL), F32)],
        in_specs=[pl.BlockSpec((tm, k_dim), row), _resident(w_t.shape), pl.BlockSpec((tm, D_MODEL), row),
                  pl.BlockSpec((tm, D_MODEL), row), _resident((1, D_MODEL)), per_batch],
        out_specs=[pl.BlockSpec((tm, D_MODEL), row), per_batch, per_batch, pl.BlockSpec((1, D_MODEL), lambda i: (0, 0))],
        compiler_params=_params(),
    )(dsrc, w_t, x, dy, gnorm, scale1p)


def _weight_grad(a, b, seq, name):
    tokens, n_out = a.shape
    tt = _tile_rows(seq)
    steps = tokens // tt
    tn = n_out
    for cand in (1408, 1280, 1024):
        if n_out % cand == 0:
            tn = cand
            break

    def body(a_ref, b_ref, o_ref, acc_ref):
        t = pl.program_id(1)
        prod = _dot_tn(a_ref[...], b_ref[...])

        @pl.when(t == 0)
        def _():
            acc_ref[...] = prod

        @pl.when(t != 0)
        def _():
            acc_ref[...] = acc_ref[...] + prod

        @pl.when(t == steps - 1)
        def _():
            o_ref[...] = acc_ref[...].astype(BF16)

    return pl.pallas_call(
        body, name=name, grid=(n_out // tn, steps),
        out_shape=jax.ShapeDtypeStruct((n_out, D_MODEL), BF16),
        in_specs=[pl.BlockSpec((tt, tn), lambda j, t: (t, j)), pl.BlockSpec((tt, D_MODEL), lambda j, t: (t, 0))],
        out_specs=pl.BlockSpec((tn, D_MODEL), lambda j, t: (j, 0)),
        scratch_shapes=[pltpu.VMEM((tn, D_MODEL), F32)],
        compiler_params=pltpu.CompilerParams(dimension_semantics=("parallel", "arbitrary"), vmem_limit_bytes=VMEM_LIMIT),
    )(a, b)


def _final_loss(x, target, gnorm, seq):
    tokens = x.shape[0]
    tm = _tile_rows(seq)

    def body(x_ref, t_ref, g_ref, dx_ref, dg_ref, loss_ref):
        i = pl.program_id(0)
        xv = x_ref[...]
        r = _rms(xv)
        xn = xv * r
        gn = g_ref[...]
        err = xn * gn - t_ref[...]
        loss = (0.5 / D_MODEL) * jnp.sum(err * err, axis=0, keepdims=True)
        dyv = err * (1.0 / D_MODEL)
        dg = jnp.sum(dyv * xn, axis=0, keepdims=True)
        dxn = dyv * gn
        dx_ref[...] = r * (dxn - xn * jnp.mean(dxn * xn, axis=-1, keepdims=True))

        @pl.when(i == 0)
        def _():
            dg_ref[...] = dg
            loss_ref[...] = loss

        @pl.when(i != 0)
        def _():
            dg_ref[...] = dg_ref[...] + dg
            loss_ref[...] = loss_ref[...] + loss

    row = lambda i: (i, 0)
    fixed = pl.BlockSpec((1, D_MODEL), lambda i: (0, 0))
    return pl.pallas_call(
        body, name="final_loss", grid=(tokens // tm,),
        out_shape=[jax.ShapeDtypeStruct((tokens, D_MODEL), F32), jax.ShapeDtypeStruct((1, D_MODEL), F32),
                   jax.ShapeDtypeStruct((1, D_MODEL), F32)],
        in_specs=[pl.BlockSpec((tm, D_MODEL), row), pl.BlockSpec((tm, D_MODEL), row), _resident((1, D_MODEL))],
        out_specs=[pl.BlockSpec((tm, D_MODEL), row), fixed, fixed],
        compiler_params=_params(),
    )(x, target, gnorm)


def _group_mean(v, bd):
    hi = v.astype(BF16)
    lo = (v - hi.astype(F32)).astype(BF16)
    return _dot(hi, bd) + _dot(lo, bd)


def _sgu_forward(pm_ref, wm_ref, bias_ref, lng_ref, lnb_ref, bd_ref, mixed_scr, n_sub):
    ua = pm_ref[:, 0:D_A].astype(F32)
    va = pm_ref[:, D_A:2 * D_A].astype(F32)
    u_act = _gelu(ua)
    v_act = _gelu(va)
    bd = bd_ref[...]
    vc = v_act - _group_mean(v_act, bd)
    rstd = lax.rsqrt(_group_mean(vc * vc, bd) + EPS)
    vhat = vc * rstd
    vln = vhat * lng_ref[...] + lnb_ref[...]
    left = lax.broadcasted_iota(jnp.int32, (CHUNK, LANES), 1) < HEAD_DIM
    for q in range(n_sub):
        rows = slice(q * CHUNK, (q + 1) * CHUNK)
        for p in range(N_HEADS // 2):
            cols = slice(p * LANES, (p + 1) * LANES)
            vp = vln[rows, cols]
            v_l = jnp.where(left, vp, 0.0).astype(BF16)
            v_r = jnp.where(left, 0.0, vp).astype(BF16)
            mixed_scr[rows, cols] = _dot(wm_ref[2 * p], v_l) + _dot(wm_ref[2 * p + 1], v_r) + bias_ref[:, cols]
    return ua, va, u_act, vhat, rstd, vln


def _halo_specs(tm, tokens, width):
    prev = pl.BlockSpec((HALO, width), lambda i: (jnp.maximum(i * (tm // HALO) - 1, 0), 0))
    nxt = pl.BlockSpec((HALO, width), lambda i: (jnp.minimum((i + 1) * (tm // HALO), tokens // HALO - 1), 0))
    return prev, nxt


def _mixer_forward(proj, wm, bias_full, lng, lnb, convw, og, bd, seq, name):
    tokens = proj.shape[0]
    tm = _tile_rows(seq)
    per_seq = seq // tm
    n_sub = tm // CHUNK

    def body(pm_ref, pp_ref, wm_ref, bias_ref, lng_ref, lnb_ref, cw_ref, og_ref, bd_ref, y_ref, mixed_scr):
        i = pl.program_id(0)
        first = (i % per_seq) == 0
        _, _, u_act, _, _, _ = _sgu_forward(pm_ref, wm_ref, bias_ref, lng_ref, lnb_ref, bd_ref, mixed_scr, n_sub)
        ya = u_act * mixed_scr[...]
        y_ref[:, 0:D_A] = (ya * _rms(ya) * og_ref[:, 0:D_A]).astype(BF16)

        bg = pm_ref[:, 2 * D_A:3 * D_A].astype(F32)
        z = pm_ref[:, 3 * D_A:4 * D_A].astype(F32) * pm_ref[:, 4 * D_A:5 * D_A].astype(F32)
        zp = pp_ref[:, 3 * D_A:4 * D_A].astype(F32) * pp_ref[:, 4 * D_A:5 * D_A].astype(F32)
        zp = jnp.where(first, 0.0, zp)
        zext = jnp.concatenate([zp, z], axis=0)
        z1 = pltpu.roll(zext, 1, 0)[HALO:]
        z2 = pltpu.roll(zext, 2, 0)[HALO:]
        conv = cw_ref[0:1, :] * z2 + cw_ref[1:2, :] * z1 + cw_ref[2:3, :] * z
        yb = bg * conv
        y_ref[:, D_A:2 * D_A] = (yb * _rms(yb) * og_ref[:, D_A:2 * D_A]).astype(BF16)

    prev, _ = _halo_specs(tm, tokens, D_PROJ)
    return pl.pallas_call(
        body, name=name, grid=(tokens // tm,),
        out_shape=jax.ShapeDtypeStruct((tokens, D_MODEL), BF16),
        in_specs=[pl.BlockSpec((tm, D_PROJ), lambda i: (i, 0)), prev, _resident(wm.shape), _resident(bias_full.shape),
                  _resident(lng.shape), _resident(lnb.shape), _resident(convw.shape), _resident(og.shape),
                  _resident(bd.shape)],
        out_specs=pl.BlockSpec((tm, D_MODEL), lambda i: (i, 0)),
        scratch_shapes=[pltpu.VMEM((tm, D_A), F32)],
        compiler_params=_params(parallel=True),
    )(proj, proj, wm, bias_full, lng, lnb, convw, og, bd)


def _mixer_backward(proj, dy, wm, bias_full, lng, lnb, convw, og, bd, causal, seq, name):
    tokens = proj.shape[0]
    tm = _tile_rows(seq)
    per_seq = seq // tm
    n_sub = tm // CHUNK
    ext = tm + 2 * HALO

    def body(pm_ref, pp_ref, pn_ref, dy_ref, dyn_ref, wm_ref, bias_ref, lng_ref, lnb_ref, cw_ref, og_ref, bd_ref,
             causal_ref, dp_ref, dog_ref, dcw_ref, dlng_ref, dlnb_ref, dbias_ref, dwm_ref, mixed_scr, dvln_scr):
        i = pl.program_id(0)
        first = (i % per_seq) == 0
        last = (i % per_seq) == per_seq - 1

        @pl.when(i == 0)
        def _():
            dog_ref[...] = jnp.zeros_like(dog_ref)
            dcw_ref[...] = jnp.zeros_like(dcw_ref)
            dlng_ref[...] = jnp.zeros_like(dlng_ref)
            dlnb_ref[...] = jnp.zeros_like(dlnb_ref)
            dbias_ref[...] = jnp.zeros_like(dbias_ref)
            dwm_ref[...] = jnp.zeros_like(dwm_ref)

        ua, va, u_act, vhat, rstd, vln = _sgu_forward(pm_ref, wm_ref, bias_ref, lng_ref, lnb_ref, bd_ref, mixed_scr, n_sub)
        mixed = mixed_scr[...]
        ya = u_act * mixed
        ra = _rms(ya)
        yhat = ya * ra
        dya_in = dy_ref[:, 0:D_A]
        dog_ref[:, 0:D_A] = dog_ref[:, 0:D_A] + jnp.sum(dya_in * yhat, axis=0, keepdims=True)
        dyh = dya_in * og_ref[:, 0:D_A]
        dya = ra * (dyh - yhat * jnp.mean(dyh * yhat, axis=-1, keepdims=True))
        d_u = dya * mixed
        d_mixed = dya * u_act
        left = lax.broadcasted_iota(jnp.int32, (CHUNK, LANES), 1) < HEAD_DIM
        dbias = jnp.zeros((CHUNK, D_A), F32)
        for q in range(n_sub):
            rows = slice(q * CHUNK, (q + 1) * CHUNK)
            dbias = dbias + d_mixed[rows, :]
            for p in range(N_HEADS // 2):
                cols = slice(p * LANES, (p + 1) * LANES)
                dm = d_mixed[rows, cols]
                dm_l = jnp.where(left, dm, 0.0).astype(BF16)
                dm_r = jnp.where(left, 0.0, dm).astype(BF16)
                vp = vln[rows, cols].astype(BF16)
                dwm_ref[2 * p] = dwm_ref[2 * p] + causal_ref[...] * _dot_nt(dm_l, vp)
                dwm_ref[2 * p + 1] = dwm_ref[2 * p + 1] + causal_ref[...] * _dot_nt(dm_r, vp)
                dvln_scr[rows, cols] = _dot_tn(wm_ref[2 * p], dm_l) + _dot_tn(wm_ref[2 * p + 1], dm_r)
        dbias_ref[...] = dbias_ref[...] + dbias
        dvln = dvln_scr[...]
        dlng_ref[...] = dlng_ref[...] + jnp.sum(dvln * vhat, axis=0, keepdims=True)
        dlnb_ref[...] = dlnb_ref[...] + jnp.sum(dvln, axis=0, keepdims=True)
        dvh = dvln * lng_ref[...]
        bd = bd_ref[...]
        d_v = rstd * (dvh - _group_mean(dvh, bd) - vhat * _group_mean(dvh * vhat, bd))
        dp_ref[:, 0:D_A] = (d_u * _gelu_grad(ua)).astype(BF16)
        dp_ref[:, D_A:2 * D_A] = (d_v * _gelu_grad(va)).astype(BF16)

        def ext_cols(lo):
            cs = slice(lo, lo + D_A)
            return jnp.concatenate([pp_ref[:, cs], pm_ref[:, cs], pn_ref[:, cs]], axis=0).astype(F32)

        bg, cg, xb = ext_cols(2 * D_A), ext_cols(3 * D_A), ext_cols(4 * D_A)
        row = lax.broadcasted_iota(jnp.int32, (ext, D_A), 0)
        z = jnp.where(jnp.logical_and(first, row < HALO), 0.0, cg * xb)
        z1 = pltpu.roll(z, 1, 0)
        z2 = pltpu.roll(z, 2, 0)
        w0, w1, w2 = cw_ref[0:1, :], cw_ref[1:2, :], cw_ref[2:3, :]
        conv = w0 * z2 + w1 * z1 + w2 * z
        yb = bg * conv
        rb = _rms(yb)
        yhb = yb * rb
        dyn = jnp.where(last, 0.0, dyn_ref[:, D_A:2 * D_A])
        dyb_in = jnp.concatenate([jnp.zeros((HALO, D_A), F32), dy_ref[:, D_A:2 * D_A], dyn], axis=0)
        dyhb = dyb_in * og_ref[:, D_A:2 * D_A]
        dyb = rb * (dyhb - yhb * jnp.mean(dyhb * yhb, axis=-1, keepdims=True))
        d_conv = dyb * bg
        dz = w2 * d_conv + w1 * pltpu.roll(d_conv, ext - 1, 0) + w0 * pltpu.roll(d_conv, ext - 2, 0)
        main = slice(HALO, HALO + tm)
        dp_ref[:, 2 * D_A:3 * D_A] = (dyb * conv)[main].astype(BF16)
        dp_ref[:, 3 * D_A:4 * D_A] = (dz * xb)[main].astype(BF16)
        dp_ref[:, 4 * D_A:5 * D_A] = (dz * cg)[main].astype(BF16)
        dog_ref[:, D_A:2 * D_A] = dog_ref[:, D_A:2 * D_A] + jnp.sum((dyb_in * yhb)[main], axis=0, keepdims=True)
        dcm = d_conv[main]
        dcw_ref[0:1, :] = dcw_ref[0:1, :] + jnp.sum(dcm * z2[main], axis=0, keepdims=True)
        dcw_ref[1:2, :] = dcw_ref[1:2, :] + jnp.sum(dcm * z1[main], axis=0, keepdims=True)
        dcw_ref[2:3, :] = dcw_ref[2:3, :] + jnp.sum(dcm * z[main], axis=0, keepdims=True)

    prev_p, next_p = _halo_specs(tm, tokens, D_PROJ)
    _, next_d = _halo_specs(tm, tokens, D_MODEL)
    fixed2 = lambda shape: pl.BlockSpec(shape, lambda i: (0, 0))
    return pl.pallas_call(
        body, name=name, grid=(tokens // tm,),
        out_shape=[jax.ShapeDtypeStruct((tokens, D_PROJ), BF16), jax.ShapeDtypeStruct((1, D_MODEL), F32),
                   jax.ShapeDtypeStruct((8, D_A), F32), jax.ShapeDtypeStruct((1, D_A), F32),
                   jax.ShapeDtypeStruct((1, D_A), F32), jax.ShapeDtypeStruct((CHUNK, D_A), F32),
                   jax.ShapeDtypeStruct((N_HEADS, CHUNK, CHUNK), F32)],
        in_specs=[pl.BlockSpec((tm, D_PROJ), lambda i: (i, 0)), prev_p, next_p,
                  pl.BlockSpec((tm, D_MODEL), lambda i: (i, 0)), next_d,
                  _resident(wm.shape), _resident(bias_full.shape), _resident(lng.shape), _resident(lnb.shape),
                  _resident(convw.shape), _resident(og.shape), _resident(bd.shape), _resident(causal.shape)],
        out_specs=[pl.BlockSpec((tm, D_PROJ), lambda i: (i, 0)), fixed2((1, D_MODEL)), fixed2((8, D_A)),
                   fixed2((1, D_A)), fixed2((1, D_A)), fixed2((CHUNK, D_A)),
                   pl.BlockSpec((N_HEADS, CHUNK, CHUNK), lambda i: (0, 0, 0))],
        scratch_shapes=[pltpu.VMEM((tm, D_A), F32), pltpu.VMEM((tm, D_A), F32)],
        compiler_params=_params(),
    )(proj, proj, proj, dy, dy, wm, bias_full, lng, lnb, convw, og, bd, causal)


def _sum_slots(recv, name):
    _, rows, cols = recv.shape
    tr = rows // 2

    def body(r_ref, o_ref):
        acc = r_ref[0].astype(F32)
        for d in range(1, N_DEV):
            acc = acc + r_ref[d].astype(F32)
        o_ref[...] = acc

    return pl.pallas_call(
        body, name=name, grid=(2,),
        out_shape=jax.ShapeDtypeStruct((rows, cols), F32),
        in_specs=[pl.BlockSpec((N_DEV, tr, cols), lambda i: (0, i, 0))],
        out_specs=pl.BlockSpec((tr, cols), lambda i: (i, 0)),
        compiler_params=_params(parallel=True),
    )(recv)


def _adamw(w, g, m, v, name):
    rows, cols = w.shape
    tr = max(t for t in range(8, 513, 8) if rows % t == 0)

    def body(w_ref, g_ref, m_ref, v_ref, d_ref, nm_ref, nv_ref):
        gv = g_ref[...]
        nm = ADAM_B1 * m_ref[...] + (1.0 - ADAM_B1) * gv
        nv = ADAM_B2 * v_ref[...] + (1.0 - ADAM_B2) * (gv * gv)
        m_hat = nm / (1.0 - ADAM_B1 ** ADAM_STEP)
        v_hat = nv / (1.0 - ADAM_B2 ** ADAM_STEP)
        d_ref[...] = -ADAM_LR * (m_hat / (jnp.sqrt(v_hat) + ADAM_EPS) + ADAM_WD * w_ref[...])
        nm_ref[...] = nm
        nv_ref[...] = nv

    spec = pl.BlockSpec((tr, cols), lambda i: (i, 0))
    return pl.pallas_call(
        body, name=name, grid=(rows // tr,),
        out_shape=[jax.ShapeDtypeStruct((rows, cols), F32)] * 3,
        in_specs=[spec] * 4, out_specs=[spec] * 3,
        compiler_params=_params(parallel=True),
    )(w, g, m, v)


def _adamw_nd(w, g, m, v, name):
    shape = w.shape
    two_d = (-1, shape[-1])
    d, nm, nv = _adamw(w.reshape(two_d), g.reshape(two_d), m.reshape(two_d), v.reshape(two_d), name)
    return d.reshape(shape), nm.reshape(shape), nv.reshape(shape)


def kernel(x, c, ada_w, ada_b, norm_ffn1_g, ffn1_w_gu, ffn1_w_down, norm_mix_g, mix_w_in, sgu_ln_g, sgu_ln_b, sgu_w_s, sgu_b, conv_w, out_norm_g, mix_w_out, norm_ffn2_g, ffn2_w_gu, ffn2_w_down, final_norm_g, loss_target, m_ada_w, m_ada_b, m_norm_ffn1_g, m_ffn1_w_gu, m_ffn1_w_down, m_norm_mix_g, m_mix_w_in, m_sgu_ln_g, m_sgu_ln_b, m_sgu_w_s, m_sgu_b, m_conv_w, m_out_norm_g, m_mix_w_out, m_norm_ffn2_g, m_ffn2_w_gu, m_ffn2_w_down, m_final_norm_g, v_ada_w, v_ada_b, v_norm_ffn1_g, v_ffn1_w_gu, v_ffn1_w_down, v_norm_mix_g, v_mix_w_in, v_sgu_ln_g, v_sgu_ln_b, v_sgu_w_s, v_sgu_b, v_conv_w, v_out_norm_g, v_mix_w_out, v_norm_ffn2_g, v_ffn2_w_gu, v_ffn2_w_down, v_final_norm_g):
    batch, seq, _ = x.shape
    tokens = batch * seq
    me = 4 * lax.axis_index("x") + 2 * lax.axis_index("y") + lax.axis_index("c")
    weights = dict(ada_w=ada_w, ada_b=ada_b, norm_ffn1_g=norm_ffn1_g, ffn1_w_gu=ffn1_w_gu, ffn1_w_down=ffn1_w_down,
                   norm_mix_g=norm_mix_g, mix_w_in=mix_w_in, sgu_ln_g=sgu_ln_g, sgu_ln_b=sgu_ln_b, sgu_w_s=sgu_w_s,
                   sgu_b=sgu_b, conv_w=conv_w, out_norm_g=out_norm_g, mix_w_out=mix_w_out, norm_ffn2_g=norm_ffn2_g,
                   ffn2_w_gu=ffn2_w_gu, ffn2_w_down=ffn2_w_down, final_norm_g=final_norm_g)
    mom1 = dict(ada_w=m_ada_w, ada_b=m_ada_b, norm_ffn1_g=m_norm_ffn1_g, ffn1_w_gu=m_ffn1_w_gu,
                ffn1_w_down=m_ffn1_w_down, norm_mix_g=m_norm_mix_g, mix_w_in=m_mix_w_in, sgu_ln_g=m_sgu_ln_g,
                sgu_ln_b=m_sgu_ln_b, sgu_w_s=m_sgu_w_s, sgu_b=m_sgu_b, conv_w=m_conv_w, out_norm_g=m_out_norm_g,
                mix_w_out=m_mix_w_out, norm_ffn2_g=m_norm_ffn2_g, ffn2_w_gu=m_ffn2_w_gu, ffn2_w_down=m_ffn2_w_down,
                final_norm_g=m_final_norm_g)
    mom2 = dict(ada_w=v_ada_w, ada_b=v_ada_b, norm_ffn1_g=v_norm_ffn1_g, ffn1_w_gu=v_ffn1_w_gu,
                ffn1_w_down=v_ffn1_w_down, norm_mix_g=v_norm_mix_g, mix_w_in=v_mix_w_in, sgu_ln_g=v_sgu_ln_g,
                sgu_ln_b=v_sgu_ln_b, sgu_w_s=v_sgu_w_s, sgu_b=v_sgu_b, conv_w=v_conv_w, out_norm_g=v_out_norm_g,
                mix_w_out=v_mix_w_out, norm_ffn2_g=v_norm_ffn2_g, ffn2_w_gu=v_ffn2_w_gu, ffn2_w_down=v_ffn2_w_down,
                final_norm_g=v_final_norm_g)

    big = ("ffn1_w_gu", "ffn1_w_down", "mix_w_in", "mix_w_out", "ffn2_w_gu", "ffn2_w_down")
    transposed = ("ffn1_w_gu", "mix_w_in", "ffn2_w_gu")
    shards = []
    for l in range(DEPTH):
        for nm in big:
            w_l = weights[nm][l]
            shards.append((w_l.T if nm in transposed else w_l).astype(BF16))
    gathered = _all_gather_rows(shards, "gather_weights")
    full_w = [dict(zip(big, gathered[l * len(big):(l + 1) * len(big)])) for l in range(DEPTH)]

    small_in = _pack_small([c, conv_w])
    small_all, _ = _all_gather_small(small_in, "gather_c")
    c_all, convw_all = _unpack_small(small_all, [c.shape, conv_w.shape], lead=(N_DEV,))
    c_all = c_all.reshape(N_DEV * batch, D_MODEL)
    convw_full = jnp.transpose(convw_all, (1, 2, 0, 3)).reshape(DEPTH, 3, D_A)
    ada_cols = ada_w.shape[2]
    ada_b_cols = lax.dynamic_slice_in_dim(ada_b, me * ada_cols, ada_cols, axis=1).reshape(DEPTH, 1, ada_cols)
    ada_local = _ada_forward(c_all, ada_w, ada_b_cols)
    ada_all, _ = _all_gather_small(ada_local.reshape(-1, LANES), "gather_ada")
    ada_all = ada_all.reshape(N_DEV, DEPTH, N_DEV * batch, ada_cols)
    ada_full = jnp.transpose(ada_all, (1, 2, 0, 3)).reshape(DEPTH, N_DEV * batch, N_MOD * D_MODEL)
    ada_mine = lax.dynamic_slice_in_dim(ada_full, me * batch, batch, axis=1)
    mod = ada_mine.reshape(DEPTH, batch, N_MOD, 1, D_MODEL)

    causal = jnp.tril(jnp.ones((CHUNK, CHUNK), F32))
    bd = jnp.kron(jnp.eye(N_HEADS, dtype=F32), jnp.full((HEAD_DIM, HEAD_DIM), 1.0 / HEAD_DIM, F32)).astype(BF16)
    row_vec = lambda a: a.reshape(1, -1)

    xs = x.reshape(tokens, D_MODEL)
    saved = []
    for l in range(DEPTH):
        fw = full_w[l]
        sh1, sc1, g1, sh2, sc2, g2, sh3, sc3, g3 = [mod[l, :, k] for k in range(N_MOD)]
        mixer_consts = dict(
            wm=(sgu_w_s[l] * causal[None]).astype(BF16),
            bias_full=jnp.repeat(sgu_b[l].T, HEAD_DIM, axis=1),
            lng=row_vec(jnp.tile(sgu_ln_g[l], N_HEADS)), lnb=row_vec(jnp.tile(sgu_ln_b[l], N_HEADS)),
            convw=jnp.pad(convw_full[l], ((0, 5), (0, 0))), og=row_vec(out_norm_g[l]), bd=bd)
        x0 = xs
        h1, gu1 = _normmod_matmul(x0, row_vec(norm_ffn1_g[l]), 1.0 + sc1, sh1, fw["ffn1_w_gu"], seq, "ffn_up")
        x1, f1 = _matmul_residual(gu1, fw["ffn1_w_down"], x0, g1, 0.5, True, seq, "ffn_down")
        h2, proj = _normmod_matmul(x1, row_vec(norm_mix_g[l]), 1.0 + sc2, sh2, fw["mix_w_in"], seq, "mix_in")
        ymix = _mixer_forward(proj, seq=seq, name="mixer_forward", **mixer_consts)
        x2, o2 = _matmul_residual(ymix, fw["mix_w_out"], x1, g2, 1.0, False, seq, "mix_out")
        h3, gu3 = _normmod_matmul(x2, row_vec(norm_ffn2_g[l]), 1.0 + sc3, sh3, fw["ffn2_w_gu"], seq, "ffn_up")
        x3, f3 = _matmul_residual(gu3, fw["ffn2_w_down"], x2, g3, 0.5, True, seq, "ffn_down")
        saved.append(dict(x0=x0, x1=x1, x2=x2, h1=h1, h2=h2, h3=h3, gu1=gu1, gu3=gu3, f1=f1, f3=f3, o2=o2, proj=proj,
                          ymix=ymix, mixer_consts=mixer_consts, sc=(1.0 + sc1, 1.0 + sc2, 1.0 + sc3), gates=(g1, g2, g3)))
        xs = x3

    dx, d_final_g, loss_cols = _final_loss(xs, loss_target.reshape(tokens, D_MODEL), row_vec(final_norm_g), seq)

    big_grads = [None] * DEPTH
    small_grads = [None] * DEPTH
    d_mod = [None] * DEPTH
    for l in reversed(range(DEPTH)):
        fw, sv = full_w[l], saved[l]
        mc = sv["mixer_consts"]
        df3, dg3, dgu3, a3 = _residual_backward(dx, sv["gates"][2], sv["f3"], sv["gu3"], fw["ffn2_w_down"], 0.5, True, seq, "ffn_down_bwd")
        gw_down2 = _weight_grad(a3, df3, seq, "grad_w_down")
        dx2, dsh3, dsc3, dn3 = _matmul_normmod_backward(dgu3, fw["ffn2_w_gu"], sv["x2"], dx, row_vec(norm_ffn2_g[l]), sv["sc"][2], seq, "ffn_up_bwd")
        gw_gu2 = _weight_grad(dgu3, sv["h3"], seq, "grad_w_gu")
        do2, dg2, dymix = _residual_backward(dx2, sv["gates"][1], sv["o2"], None, fw["mix_w_out"], 1.0, False, seq, "mix_out_bwd")
        gw_out = _weight_grad(sv["ymix"], do2, seq, "grad_w_out")
        dproj, d_og, d_cw, d_lng, d_lnb, d_bias, d_wm = _mixer_backward(
            sv["proj"], dymix, causal=causal, seq=seq, name="mixer_backward", **mc)
        dx1, dsh2, dsc2, dn2 = _matmul_normmod_backward(dproj, fw["mix_w_in"], sv["x1"], dx2, row_vec(norm_mix_g[l]), sv["sc"][1], seq, "mix_in_bwd")
        gw_in = _weight_grad(dproj, sv["h2"], seq, "grad_w_in")
        df1, dg1, dgu1, a1 = _residual_backward(dx1, sv["gates"][0], sv["f1"], sv["gu1"], fw["ffn1_w_down"], 0.5, True, seq, "ffn_down_bwd")
        gw_down1 = _weight_grad(a1, df1, seq, "grad_w_down")
        dx0, dsh1, dsc1, dn1 = _matmul_normmod_backward(dgu1, fw["ffn1_w_gu"], sv["x0"], dx1, row_vec(norm_ffn1_g[l]), sv["sc"][0], seq, "ffn_up_bwd")
        gw_gu1 = _weight_grad(dgu1, sv["h1"], seq, "grad_w_gu")
        dx = dx0
        big_grads[l] = dict(ffn1_w_gu=gw_gu1, ffn1_w_down=gw_down1, mix_w_in=gw_in, mix_w_out=gw_out,
                            ffn2_w_gu=gw_gu2, ffn2_w_down=gw_down2)
        small_grads[l] = dict(
            norm_ffn1_g=dn1, norm_mix_g=dn2, norm_ffn2_g=dn3, out_norm_g=d_og,
            sgu_ln_g=d_lng.reshape(N_HEADS, HEAD_DIM).sum(0), sgu_ln_b=d_lnb.reshape(N_HEADS, HEAD_DIM).sum(0),
            sgu_w_s=d_wm, sgu_b=d_bias.reshape(CHUNK, N_HEADS, HEAD_DIM).sum(-1).T, conv_w=d_cw[0:3])
        d_mod[l] = jnp.concatenate([dsh1, dsc1, dg1, dsh2, dsc2, dg2, dsh3, dsc3, dg3], axis=1)
    grad_x = dx.reshape(batch, seq, D_MODEL)

    recv = _all_to_all_rows([big_grads[l][nm] for l in range(DEPTH) for nm in big], "scatter_grads")
    grad_big = {}
    for k, nm in enumerate(big):
        per_layer = []
        for l in range(DEPTH):
            g_sum = _sum_slots(recv[l * len(big) + k], "sum_" + nm)
            per_layer.append(g_sum.T if nm in transposed else g_sum)
        grad_big[nm] = jnp.stack(per_layer)

    small_names = ("norm_ffn1_g", "norm_mix_g", "norm_ffn2_g", "out_norm_g", "sgu_ln_g", "sgu_ln_b", "sgu_w_s", "sgu_b", "conv_w")
    stacked = [jnp.stack([small_grads[l][nm] for l in range(DEPTH)]) for nm in small_names]
    parts = stacked + [d_final_g, loss_cols, jnp.stack(d_mod)]
    shapes = [p.shape for p in parts]
    packed_all, packed_sum = _all_gather_small(_pack_small(parts), "reduce_small")
    summed = _unpack_small(packed_sum, shapes)
    grad_small = {nm: summed[k].reshape(weights[nm].shape) if nm != "conv_w" else summed[k] for k, nm in enumerate(small_names)}
    grad_small["conv_w"] = lax.dynamic_slice_in_dim(grad_small["conv_w"], me * conv_w.shape[2], conv_w.shape[2], axis=2)
    grad_small["final_norm_g"] = summed[len(small_names)].reshape(final_norm_g.shape)
    loss = jnp.sum(summed[len(small_names) + 1])
    d_mod_all = _unpack_small(packed_all, shapes, lead=(N_DEV,))[-1]
    d_ada_all = jnp.transpose(d_mod_all, (1, 0, 2, 3, 4)).reshape(DEPTH, N_DEV * batch, N_MOD * D_MODEL)
    d_ada_cols = lax.dynamic_slice_in_dim(d_ada_all, me * ada_cols, ada_cols, axis=2)
    g_ada_w, g_ada_b = _ada_backward(c_all, d_ada_cols, d_ada_all)

    grads = dict(grad_big)
    grads.update(grad_small)
    grads["ada_w"] = g_ada_w
    grads["ada_b"] = g_ada_b.reshape(ada_b.shape)

    names = ("ada_w", "ada_b", "norm_ffn1_g", "ffn1_w_gu", "ffn1_w_down", "norm_mix_g", "mix_w_in", "sgu_ln_g",
             "sgu_ln_b", "sgu_w_s", "sgu_b", "conv_w", "out_norm_g", "mix_w_out", "norm_ffn2_g", "ffn2_w_gu",
             "ffn2_w_down", "final_norm_g")
    large = ("ada_w",) + big
    delta, new_m, new_v = {}, {}, {}
    for nm in large:
        delta[nm], new_m[nm], new_v[nm] = _adamw_nd(weights[nm], grads[nm], mom1[nm], mom2[nm], "adamw_" + nm)
    rest = [nm for nm in names if nm not in large]
    pack = lambda src: _pack_small([src[nm] for nm in rest])
    d_p, m_p, v_p = _adamw(pack(weights), pack(grads), pack(mom1), pack(mom2), "adamw_small")
    rest_shapes = [weights[nm].shape for nm in rest]
    for nm, d_k, m_k, v_k in zip(rest, _unpack_small(d_p, rest_shapes), _unpack_small(m_p, rest_shapes), _unpack_small(v_p, rest_shapes)):
        delta[nm], new_m[nm], new_v[nm] = d_k, m_k, v_k

    return (loss, grad_x, *[grads[nm] for nm in names], *[delta[nm] for nm in names],
            *[new_m[nm] for nm in names], *[new_v[nm] for nm in names])
```

```python
import math

import jax
import jax.numpy as jnp
from jax import lax
from jax.experimental import pallas as pl
from jax.experimental.pallas import tpu as pltpu

F32 = jnp.float32
BF16 = jnp.bfloat16

D_MODEL = 1024
D_FF = 2816
D_A = 512
D_PROJ = 2560
N_HEADS = 8
HEAD_DIM = 64
CHUNK = 128
N_MOD = 9
DEPTH = 2
EPS = 1e-6
N_DEV = 8
LANES = 128
MXU_N = 256
HALO = 16
VMEM_LIMIT = 56 * 1024 * 1024

ADAM_LR = 0.001
ADAM_B1 = 0.9
ADAM_B2 = 0.999
ADAM_EPS = 1e-08
ADAM_WD = 0.01
ADAM_STEP = 10

MESH = pl.DeviceIdType.MESH


def _dot(a, b):
    return jnp.dot(a, b, preferred_element_type=F32)


def _dot_nt(a, b):
    return lax.dot_general(a, b, (((1,), (1,)), ((), ())), preferred_element_type=F32)


def _dot_tn(a, b):
    return lax.dot_general(a, b, (((0,), (0,)), ((), ())), preferred_element_type=F32)


def _sigmoid(x):
    return 1.0 / (1.0 + jnp.exp(-x))


def _gelu(x):
    return 0.5 * x * (1.0 + lax.erf(x * (1.0 / math.sqrt(2.0))))


def _gelu_grad(x):
    cdf = 0.5 * (1.0 + lax.erf(x * (1.0 / math.sqrt(2.0))))
    return cdf + x * jnp.exp(-0.5 * x * x) * (1.0 / math.sqrt(2.0 * math.pi))


def _params(n_axes=1, parallel=False):
    sem = ("parallel" if parallel else "arbitrary",) * n_axes
    return pltpu.CompilerParams(dimension_semantics=sem, vmem_limit_bytes=VMEM_LIMIT)


def _resident(shape):
    nd = len(shape)
    return pl.BlockSpec(shape, lambda *_: (0,) * nd, pipeline_mode=pl.Buffered(1))


def _tile_rows(seq):
    return min(512, seq)


def _my_position():
    x, y, c = lax.axis_index("x"), lax.axis_index("y"), lax.axis_index("c")
    return x, y, c, 4 * x + 2 * y + c


def _peer(x, y, c, p):
    return (x ^ ((p >> 2) & 1), y ^ ((p >> 1) & 1), c ^ (p & 1))


class _GatherRows:
    def __init__(self, shards):
        self.operands = list(shards)
        n = len(shards)
        self.out_shape = [jax.ShapeDtypeStruct((N_DEV * s.shape[0], s.shape[1]), s.dtype) for s in shards]
        self.scratch = [pltpu.SemaphoreType.DMA((n, N_DEV - 1)), pltpu.SemaphoreType.DMA((n, N_DEV - 1)),
                        pltpu.SemaphoreType.DMA((n,))]

    def _plan(self, src, dst, send, recv, loc):
        x, y, c, _ = _my_position()
        me, sib = (x, y, c), (x, y, 1 - c)
        chips = [(1 - x, y), (x, 1 - y), (1 - x, 1 - y)]
        plans = []
        for k, shard in enumerate(self.operands):
            rows = shard.shape[0]

            def blk(pos, k=k, rows=rows):
                return dst[k].at[pl.ds((4 * pos[0] + 2 * pos[1] + pos[2]) * rows, rows), :]

            def rc(s, block, to, source=None, k=k, blk=blk):
                return pltpu.make_async_remote_copy(
                    src_ref=blk(block) if source is None else source, dst_ref=blk(block),
                    send_sem=send.at[k, s], recv_sem=recv.at[k, s], device_id=to, device_id_type=MESH)

            plans.append(dict(
                local=pltpu.make_async_copy(src[k], blk(me), loc.at[k]),
                first=[rc(0, me, sib, src[k])] + [rc(1 + j, me, (*chip, c), src[k]) for j, chip in enumerate(chips)],
                landed=[rc(1 + j, (*chip, c), me) for j, chip in enumerate(chips)],
                passed=[rc(4 + j, (*chip, c), sib) for j, chip in enumerate(chips)],
                from_sib=[rc(0, sib, me)] + [rc(4 + j, (*chip, 1 - c), me) for j, chip in enumerate(chips)]))
        return plans

    def start(self, src, dst, send, recv, loc):
        for plan in self._plan(src, dst, send, recv, loc):
            plan["local"].start()
            for cp in plan["first"]:
                cp.start()

    def finish(self, src, dst, send, recv, loc):
        plans = self._plan(src, dst, send, recv, loc)
        for plan in plans:
            for landed, passed in zip(plan["landed"], plan["passed"]):
                landed.wait_recv()
                passed.start()
        for plan in plans:
            for cp in plan["from_sib"]:
                cp.wait_recv()
            for cp in plan["first"] + plan["passed"]:
                cp.wait_send()
            plan["local"].wait()


class _ScatterRows:
    def __init__(self, grads):
        self.operands = list(grads)
        n = len(grads)
        self.out_shape = [jax.ShapeDtypeStruct((N_DEV, g.shape[0] // N_DEV, g.shape[1]), g.dtype) for g in grads]
        self.scratch = [pltpu.SemaphoreType.DMA((n, N_DEV - 1)), pltpu.SemaphoreType.DMA((n, N_DEV - 1)),
                        pltpu.SemaphoreType.DMA((n,))]

    def _plan(self, src, dst, send, recv, loc):
        x, y, c, me = _my_position()
        copies = []
        for k, grad in enumerate(self.operands):
            rows = grad.shape[0] // N_DEV
            copies.append(pltpu.make_async_copy(src[k].at[pl.ds(me * rows, rows), :], dst[k].at[me], loc.at[k]))
            for p in range(1, N_DEV):
                px, py, pc = _peer(x, y, c, p)
                copies.append(pltpu.make_async_remote_copy(
                    src_ref=src[k].at[pl.ds((4 * px + 2 * py + pc) * rows, rows), :], dst_ref=dst[k].at[me],
                    send_sem=send.at[k, p - 1], recv_sem=recv.at[k, p - 1], device_id=(px, py, pc), device_id_type=MESH))
        return copies

    def start(self, src, dst, send, recv, loc):
        for cp in self._plan(src, dst, send, recv, loc):
            cp.start()

    def finish(self, src, dst, send, recv, loc):
        for cp in self._plan(src, dst, send, recv, loc):
            cp.wait()


_ANY = pl.BlockSpec(memory_space=pl.ANY)


def _comm_only(comm, name):
    n = len(comm.operands)

    def body(*refs):
        src, dst, sems = refs[:n], refs[n:2 * n], refs[2 * n:]
        comm.start(src, dst, *sems)
        comm.finish(src, dst, *sems)

    return pl.pallas_call(body, name=name, out_shape=comm.out_shape, in_specs=[_ANY] * n, out_specs=[_ANY] * n,
                          scratch_shapes=comm.scratch)(*comm.operands)


def _call(body, *, name, grid, in_specs, out_specs, out_shape, operands, scratch_shapes=(), parallel=False, comm=None):
    n_axes = len(grid)
    if comm is None:
        outs = pl.pallas_call(body, name=name, grid=grid, out_shape=list(out_shape), in_specs=list(in_specs),
                              out_specs=list(out_specs), scratch_shapes=list(scratch_shapes),
                              compiler_params=_params(n_axes, parallel))(*operands)
        return list(outs), None
    n_in, n_out, n_scr, n_c = len(in_specs), len(out_specs), len(scratch_shapes), len(comm.operands)
    total = math.prod(grid)

    def hosted(*refs):
        ins, c_src = refs[:n_in], refs[n_in:n_in + n_c]
        outs, c_dst = refs[n_in + n_c:n_in + n_c + n_out], refs[n_in + n_c + n_out:n_in + 2 * n_c + n_out]
        scr, sems = refs[n_in + 2 * n_c + n_out:n_in + 2 * n_c + n_out + n_scr], refs[n_in + 2 * n_c + n_out + n_scr:]
        step = pl.program_id(0)
        for axis in range(1, n_axes):
            step = step * grid[axis] + pl.program_id(axis)

        @pl.when(step == 0)
        def _():
            comm.start(c_src, c_dst, *sems)

        body(*ins, *outs, *scr)

        @pl.when(step == total - 1)
        def _():
            comm.finish(c_src, c_dst, *sems)

    res = pl.pallas_call(hosted, name=name, grid=grid, out_shape=list(out_shape) + comm.out_shape,
                         in_specs=list(in_specs) + [_ANY] * n_c, out_specs=list(out_specs) + [_ANY] * n_c,
                         scratch_shapes=list(scratch_shapes) + comm.scratch,
                         compiler_params=_params(n_axes, False))(*operands, *comm.operands)
    return list(res[:n_out]), list(res[n_out:])


def _all_gather_small(v, name):
    rows = v.shape[0]

    def body(v_ref, all_ref, sum_ref, send_sems, recv_sems):
        x, y, c, me = _my_position()
        all_ref[me] = v_ref[...]
        copies = []
        for p in range(1, N_DEV):
            cp = pltpu.make_async_remote_copy(
                src_ref=v_ref, dst_ref=all_ref.at[me], send_sem=send_sems.at[p - 1], recv_sem=recv_sems.at[p - 1],
                device_id=_peer(x, y, c, p), device_id_type=MESH)
            cp.start()
            copies.append(cp)
        for cp in copies:
            cp.wait()
        acc = all_ref[0]
        for d in range(1, N_DEV):
            acc = acc + all_ref[d]
        sum_ref[...] = acc

    return pl.pallas_call(
        body, name=name,
        out_shape=[jax.ShapeDtypeStruct((N_DEV, rows, LANES), F32), jax.ShapeDtypeStruct((rows, LANES), F32)],
        in_specs=[pl.BlockSpec(memory_space=pltpu.VMEM)],
        out_specs=[pl.BlockSpec(memory_space=pltpu.VMEM)] * 2,
        scratch_shapes=[pltpu.SemaphoreType.DMA((N_DEV - 1,)), pltpu.SemaphoreType.DMA((N_DEV - 1,))],
        compiler_params=pltpu.CompilerParams(vmem_limit_bytes=VMEM_LIMIT),
    )(v)


def _pack_small(parts):
    flat = jnp.concatenate([p.reshape(-1).astype(F32) for p in parts])
    total = flat.shape[0]
    padded = -(-total // (8 * LANES)) * (8 * LANES)
    flat = jnp.pad(flat, (0, padded - total))
    return flat.reshape(padded // LANES, LANES)


def _unpack_small(packed, shapes, lead=()):
    flat = packed.reshape(lead + (-1,))
    out, off = [], 0
    for shp in shapes:
        size = math.prod(shp)
        out.append(flat[..., off:off + size].reshape(lead + tuple(shp)))
        off += size
    return out


def _ada_forward(c_all, ada_w, ada_b_cols):
    nb = c_all.shape[0]
    cols = ada_w.shape[2]

    def body(c_ref, w_ref, b_ref, o_ref):
        cv = c_ref[...]
        act = (cv * _sigmoid(cv)).astype(BF16)
        o_ref[0] = _dot(act, w_ref[0].astype(BF16)) + b_ref[0]

    return pl.pallas_call(
        body, name="ada_forward", grid=(DEPTH,),
        out_shape=jax.ShapeDtypeStruct((DEPTH, nb, cols), F32),
        in_specs=[pl.BlockSpec((nb, D_MODEL), lambda l: (0, 0)),
                  pl.BlockSpec((1, D_MODEL, cols), lambda l: (l, 0, 0)),
                  pl.BlockSpec((1, 1, cols), lambda l: (l, 0, 0))],
        out_specs=pl.BlockSpec((1, nb, cols), lambda l: (l, 0, 0)),
        compiler_params=_params(),
    )(c_all, ada_w, ada_b_cols)


def _ada_backward(c_all, d_ada_cols, d_ada_all):
    nb = c_all.shape[0]
    cols = d_ada_cols.shape[2]
    full = d_ada_all.shape[2]

    def body(c_ref, dc_ref, da_ref, gw_ref, gb_ref):
        cv = c_ref[...]
        act = (cv * _sigmoid(cv)).astype(BF16)
        gw_ref[0] = _dot_tn(act, dc_ref[0].astype(BF16))
        gb_ref[0] = jnp.sum(da_ref[0], axis=0, keepdims=True)

    return pl.pallas_call(
        body, name="ada_backward", grid=(DEPTH,),
        out_shape=[jax.ShapeDtypeStruct((DEPTH, D_MODEL, cols), F32), jax.ShapeDtypeStruct((DEPTH, 1, full), F32)],
        in_specs=[pl.BlockSpec((nb, D_MODEL), lambda l: (0, 0)),
                  pl.BlockSpec((1, nb, cols), lambda l: (l, 0, 0)),
                  pl.BlockSpec((1, nb, full), lambda l: (l, 0, 0))],
        out_specs=[pl.BlockSpec((1, D_MODEL, cols), lambda l: (l, 0, 0)),
                   pl.BlockSpec((1, 1, full), lambda l: (l, 0, 0))],
        compiler_params=_params(),
    )(c_all, d_ada_cols, d_ada_all)


def _rms(xv):
    return lax.rsqrt(jnp.mean(xv * xv, axis=-1, keepdims=True) + EPS)


def _normmod_matmul(x, gnorm, scale1p, shift, w_t, seq, name, comm=None):
    tokens, n_out = x.shape[0], w_t.shape[0]
    tm = _tile_rows(seq)
    per_seq = seq // tm
    n_chunks = n_out // MXU_N

    def body(x_ref, g_ref, sc_ref, sh_ref, w_ref, h_ref, o_ref):
        xv = x_ref[...]
        h = (xv * _rms(xv) * g_ref[...]) * sc_ref[0] + sh_ref[0]
        h_ref[...] = h.astype(BF16)
        for ck in range(n_chunks):
            cs = slice(ck * MXU_N, (ck + 1) * MXU_N)
            o_ref[:, cs] = _dot_nt(h_ref[...], w_ref[cs, :]).astype(BF16)

    per_batch = pl.BlockSpec((1, 1, D_MODEL), lambda i: (i // per_seq, 0, 0))
    (h, out), got = _call(
        body, name=name, grid=(tokens // tm,),
        out_shape=[jax.ShapeDtypeStruct((tokens, D_MODEL), BF16), jax.ShapeDtypeStruct((tokens, n_out), BF16)],
        in_specs=[pl.BlockSpec((tm, D_MODEL), lambda i: (i, 0)), _resident((1, D_MODEL)), per_batch, per_batch,
                  _resident(w_t.shape)],
        out_specs=[pl.BlockSpec((tm, D_MODEL), lambda i: (i, 0)), pl.BlockSpec((tm, n_out), lambda i: (i, 0))],
        operands=(x, gnorm, scale1p, shift, w_t), parallel=True, comm=comm)
    return h, out, got


def _matmul_residual(src, w, x, gate, scale, swiglu, seq, name, comm=None):
    tokens, k_dim = x.shape[0], w.shape[0]
    tm = _tile_rows(seq)
    per_seq = seq // tm
    n_chunks = k_dim // MXU_N

    def body(*refs):
        if swiglu:
            g_ref, u_ref, w_ref, x_ref, gate_ref, xo_ref, f_ref, a_scr = refs
            for ck in range(n_chunks):
                cs = slice(ck * MXU_N, (ck + 1) * MXU_N)
                g = g_ref[:, cs].astype(F32)
                a_scr[:, cs] = (g * _sigmoid(g) * u_ref[:, cs].astype(F32)).astype(BF16)
            f = _dot(a_scr[...], w_ref[...])
        else:
            s_ref, w_ref, x_ref, gate_ref, xo_ref, f_ref = refs
            f = _dot(s_ref[...], w_ref[...])
        f_ref[...] = f.astype(BF16)
        xo_ref[...] = x_ref[...] + (scale * gate_ref[0]) * f

    src_specs = ([pl.BlockSpec((tm, k_dim), lambda i: (i, 0)), pl.BlockSpec((tm, k_dim), lambda i: (i, 1))]
                 if swiglu else [pl.BlockSpec((tm, k_dim), lambda i: (i, 0))])
    (x_out, f), got = _call(
        body, name=name, grid=(tokens // tm,),
        out_shape=[jax.ShapeDtypeStruct((tokens, D_MODEL), F32), jax.ShapeDtypeStruct((tokens, D_MODEL), BF16)],
        in_specs=src_specs + [_resident(w.shape), pl.BlockSpec((tm, D_MODEL), lambda i: (i, 0)),
                              pl.BlockSpec((1, 1, D_MODEL), lambda i: (i // per_seq, 0, 0))],
        out_specs=[pl.BlockSpec((tm, D_MODEL), lambda i: (i, 0))] * 2,
        scratch_shapes=[pltpu.VMEM((tm, k_dim), BF16)] if swiglu else [],
        operands=(*([src, src] if swiglu else [src]), w, x, gate), parallel=True, comm=comm)
    return x_out, f, got


def _residual_backward(dy, gate, f, src, w, scale, swiglu, seq, name, comm=None):
    tokens, k_dim = dy.shape[0], w.shape[0]
    batch = tokens // seq
    tm = _tile_rows(seq)
    per_seq = seq // tm
    n_chunks = k_dim // MXU_N

    def body(*refs):
        if swiglu:
            dy_ref, gate_ref, f_ref, g_ref, u_ref, w_ref, df_ref, dgate_ref, dgu_ref, a_ref = refs
        else:
            dy_ref, gate_ref, f_ref, w_ref, df_ref, dgate_ref, dsrc_ref = refs
        i = pl.program_id(0)
        dy_v = dy_ref[...]
        df_ref[...] = ((scale * gate_ref[0]) * dy_v).astype(BF16)
        part = scale * jnp.sum(dy_v * f_ref[...].astype(F32), axis=0, keepdims=True)

        @pl.when(i % per_seq == 0)
        def _():
            dgate_ref[0] = part

        @pl.when(i % per_seq != 0)
        def _():
            dgate_ref[0] = dgate_ref[0] + part

        if swiglu:
            for ck in range(n_chunks):
                cs = slice(ck * MXU_N, (ck + 1) * MXU_N)
                cu = slice(k_dim + ck * MXU_N, k_dim + (ck + 1) * MXU_N)
                da = _dot_nt(df_ref[...], w_ref[cs, :])
                g = g_ref[:, cs].astype(F32)
                u = u_ref[:, cs].astype(F32)
                sig = _sigmoid(g)
                silu = g * sig
                a_ref[:, cs] = (silu * u).astype(BF16)
                dgu_ref[:, cs] = (da * u * (sig * (1.0 + g * (1.0 - sig)))).astype(BF16)
                dgu_ref[:, cu] = (da * silu).astype(BF16)
        else:
            dsrc_ref[...] = _dot_nt(df_ref[...], w_ref[...])

    row = lambda i: (i, 0)
    per_batch = pl.BlockSpec((1, 1, D_MODEL), lambda i: (i // per_seq, 0, 0))
    in_specs = [pl.BlockSpec((tm, D_MODEL), row), per_batch, pl.BlockSpec((tm, D_MODEL), row)]
    out_shape = [jax.ShapeDtypeStruct((tokens, D_MODEL), BF16), jax.ShapeDtypeStruct((batch, 1, D_MODEL), F32)]
    out_specs = [pl.BlockSpec((tm, D_MODEL), row), per_batch]
    if swiglu:
        in_specs += [pl.BlockSpec((tm, k_dim), lambda i: (i, 0)), pl.BlockSpec((tm, k_dim), lambda i: (i, 1))]
        operands = (dy, gate, f, src, src, w)
        out_shape += [jax.ShapeDtypeStruct((tokens, 2 * k_dim), BF16), jax.ShapeDtypeStruct((tokens, k_dim), BF16)]
        out_specs += [pl.BlockSpec((tm, 2 * k_dim), row), pl.BlockSpec((tm, k_dim), row)]
    else:
        operands = (dy, gate, f, w)
        out_shape += [jax.ShapeDtypeStruct((tokens, k_dim), F32)]
        out_specs += [pl.BlockSpec((tm, k_dim), row)]
    in_specs += [_resident(w.shape)]
    outs, got = _call(body, name=name, grid=(tokens // tm,), out_shape=out_shape, in_specs=in_specs,
                      out_specs=out_specs, operands=operands, comm=comm)
    return (*outs, got)


def _matmul_normmod_backward(dsrc, w_t, x, dy, gnorm, scale1p, seq, name, comm=None):
    tokens, k_dim = dsrc.shape
    batch = tokens // seq
    tm = _tile_rows(seq)
    per_seq = seq // tm

    def body(ds_ref, w_ref, x_ref, dy_ref, g_ref, sc_ref, dx_ref, dsh_ref, dsc_ref, dg_ref):
        i = pl.program_id(0)
        dh = _dot(ds_ref[...], w_ref[...])
        xv = x_ref[...]
        r = _rms(xv)
        xn = xv * r
        gn = g_ref[...]
        dsh = jnp.sum(dh, axis=0, keepdims=True)
        dsc = jnp.sum(dh * (xn * gn), axis=0, keepdims=True)
        dhn = dh * sc_ref[0]
        dg = jnp.sum(dhn * xn, axis=0, keepdims=True)
        dxn = dhn * gn
        dx_ref[...] = dy_ref[...] + r * (dxn - xn * jnp.mean(dxn * xn, axis=-1, keepdims=True))

        @pl.when(i % per_seq == 0)
        def _():
            dsh_ref[0] = dsh
            dsc_ref[0] = dsc

        @pl.when(i % per_seq != 0)
        def _():
            dsh_ref[0] = dsh_ref[0] + dsh
            dsc_ref[0] = dsc_ref[0] + dsc

        @pl.when(i == 0)
        def _():
            dg_ref[...] = dg

        @pl.when(i != 0)
        def _():
            dg_ref[...] = dg_ref[...] + dg

    row = lambda i: (i, 0)
    per_batch = pl.BlockSpec((1, 1, D_MODEL), lambda i: (i // per_seq, 0, 0))
    outs, got = _call(
        body, name=name, grid=(tokens // tm,),
        out_shape=[jax.ShapeDtypeStruct((tokens, D_MODEL), F32), jax.ShapeDtypeStruct((batch, 1, D_MODEL), F32),
                   jax.ShapeDtypeStruct((batch, 1, D_MODEL), F32), jax.ShapeDtypeStruct((1, D_MODEL), F32)],
        in_specs=[pl.BlockSpec((tm, k_dim), row), _resident(w_t.shape), pl.BlockSpec((tm, D_MODEL), row),
                  pl.BlockSpec((tm, D_MODEL), row), _resident((1, D_MODEL)), per_batch],
        out_specs=[pl.BlockSpec((tm, D_MODEL), row), per_batch, per_batch, pl.BlockSpec((1, D_MODEL), lambda i: (0, 0))],
        operands=(dsrc, w_t, x, dy, gnorm, scale1p), comm=comm)
    return (*outs, got)


def _weight_grad(a, b, seq, name, comm=None):
    tokens, n_out = a.shape
    tt = _tile_rows(seq)
    steps = tokens // tt
    tn = n_out
    for cand in (1408, 1280, 1024):
        if n_out % cand == 0:
            tn = cand
            break

    def body(a_ref, b_ref, o_ref, acc_ref):
        t = pl.program_id(1)
        prod = _dot_tn(a_ref[...], b_ref[...])

        @pl.when(t == 0)
        def _():
            acc_ref[...] = prod

        @pl.when(t != 0)
        def _():
            acc_ref[...] = acc_ref[...] + prod

        @pl.when(t == steps - 1)
        def _():
            o_ref[...] = acc_ref[...].astype(BF16)

    (out,), got = _call(
        body, name=name, grid=(n_out // tn, steps),
        out_shape=[jax.ShapeDtypeStruct((n_out, D_MODEL), BF16)],
        in_specs=[pl.BlockSpec((tt, tn), lambda j, t: (t, j)), pl.BlockSpec((tt, D_MODEL), lambda j, t: (t, 0))],
        out_specs=[pl.BlockSpec((tn, D_MODEL), lambda j, t: (j, 0))],
        scratch_shapes=[pltpu.VMEM((tn, D_MODEL), F32)],
        operands=(a, b), comm=comm)
    return out, got


def _final_loss(x, target, gnorm, seq):
    tokens = x.shape[0]
    tm = _tile_rows(seq)

    def body(x_ref, t_ref, g_ref, dx_ref, dg_ref, loss_ref):
        i = pl.program_id(0)
        xv = x_ref[...]
        r = _rms(xv)
        xn = xv * r
        gn = g_ref[...]
        err = xn * gn - t_ref[...]
        loss = (0.5 / D_MODEL) * jnp.sum(err * err, axis=0, keepdims=True)
        dyv = err * (1.0 / D_MODEL)
        dg = jnp.sum(dyv * xn, axis=0, keepdims=True)
        dxn = dyv * gn
        dx_ref[...] = r * (dxn - xn * jnp.mean(dxn * xn, axis=-1, keepdims=True))

        @pl.when(i == 0)
        def _():
            dg_ref[...] = dg
            loss_ref[...] = loss

        @pl.when(i != 0)
        def _():
            dg_ref[...] = dg_ref[...] + dg
            loss_ref[...] = loss_ref[...] + loss

    row = lambda i: (i, 0)
    fixed = pl.BlockSpec((1, D_MODEL), lambda i: (0, 0))
    outs, _ = _call(
        body, name="final_loss", grid=(tokens // tm,),
        out_shape=[jax.ShapeDtypeStruct((tokens, D_MODEL), F32), jax.ShapeDtypeStruct((1, D_MODEL), F32),
                   jax.ShapeDtypeStruct((1, D_MODEL), F32)],
        in_specs=[pl.BlockSpec((tm, D_MODEL), row), pl.BlockSpec((tm, D_MODEL), row), _resident((1, D_MODEL))],
        out_specs=[pl.BlockSpec((tm, D_MODEL), row), fixed, fixed],
        operands=(x, target, gnorm))
    return outs


def _group_mean(v, bd):
    hi = v.astype(BF16)
    lo = (v - hi.astype(F32)).astype(BF16)
    return _dot(hi, bd) + _dot(lo, bd)


def _sgu_forward(pm_ref, wm_ref, bias_ref, lng_ref, lnb_ref, bd_ref, mixed_scr, n_sub):
    ua = pm_ref[:, 0:D_A].astype(F32)
    va = pm_ref[:, D_A:2 * D_A].astype(F32)
    u_act = _gelu(ua)
    v_act = _gelu(va)
    bd = bd_ref[...]
    vc = v_act - _group_mean(v_act, bd)
    rstd = lax.rsqrt(_group_mean(vc * vc, bd) + EPS)
    vhat = vc * rstd
    vln = vhat * lng_ref[...] + lnb_ref[...]
    left = lax.broadcasted_iota(jnp.int32, (CHUNK, LANES), 1) < HEAD_DIM
    for q in range(n_sub):
        rows = slice(q * CHUNK, (q + 1) * CHUNK)
        for p in range(N_HEADS // 2):
            cols = slice(p * LANES, (p + 1) * LANES)
            vp = vln[rows, cols]
            v_l = jnp.where(left, vp, 0.0).astype(BF16)
            v_r = jnp.where(left, 0.0, vp).astype(BF16)
            mixed_scr[rows, cols] = _dot(wm_ref[2 * p], v_l) + _dot(wm_ref[2 * p + 1], v_r) + bias_ref[:, cols]
    return ua, va, u_act, vhat, rstd, vln


def _halo_specs(tm, tokens, width):
    prev = pl.BlockSpec((HALO, width), lambda i: (jnp.maximum(i * (tm // HALO) - 1, 0), 0))
    nxt = pl.BlockSpec((HALO, width), lambda i: (jnp.minimum((i + 1) * (tm // HALO), tokens // HALO - 1), 0))
    return prev, nxt


def _mixer_forward(proj, wm, bias_full, lng, lnb, convw, og, bd, seq, name):
    tokens = proj.shape[0]
    tm = _tile_rows(seq)
    per_seq = seq // tm
    n_sub = tm // CHUNK

    def body(pm_ref, pp_ref, wm_ref, bias_ref, lng_ref, lnb_ref, cw_ref, og_ref, bd_ref, y_ref, mixed_scr):
        i = pl.program_id(0)
        first = (i % per_seq) == 0
        _, _, u_act, _, _, _ = _sgu_forward(pm_ref, wm_ref, bias_ref, lng_ref, lnb_ref, bd_ref, mixed_scr, n_sub)
        ya = u_act * mixed_scr[...]
        y_ref[:, 0:D_A] = (ya * _rms(ya) * og_ref[:, 0:D_A]).astype(BF16)

        bg = pm_ref[:, 2 * D_A:3 * D_A].astype(F32)
        z = pm_ref[:, 3 * D_A:4 * D_A].astype(F32) * pm_ref[:, 4 * D_A:5 * D_A].astype(F32)
        zp = pp_ref[:, 3 * D_A:4 * D_A].astype(F32) * pp_ref[:, 4 * D_A:5 * D_A].astype(F32)
        zp = jnp.where(first, 0.0, zp)
        zext = jnp.concatenate([zp, z], axis=0)
        z1 = pltpu.roll(zext, 1, 0)[HALO:]
        z2 = pltpu.roll(zext, 2, 0)[HALO:]
        conv = cw_ref[0:1, :] * z2 + cw_ref[1:2, :] * z1 + cw_ref[2:3, :] * z
        yb = bg * conv
        y_ref[:, D_A:2 * D_A] = (yb * _rms(yb) * og_ref[:, D_A:2 * D_A]).astype(BF16)

    prev, _ = _halo_specs(tm, tokens, D_PROJ)
    (y,), _ = _call(
        body, name=name, grid=(tokens // tm,),
        out_shape=[jax.ShapeDtypeStruct((tokens, D_MODEL), BF16)],
        in_specs=[pl.BlockSpec((tm, D_PROJ), lambda i: (i, 0)), prev, _resident(wm.shape), _resident(bias_full.shape),
                  _resident(lng.shape), _resident(lnb.shape), _resident(convw.shape), _resident(og.shape),
                  _resident(bd.shape)],
        out_specs=[pl.BlockSpec((tm, D_MODEL), lambda i: (i, 0))],
        scratch_shapes=[pltpu.VMEM((tm, D_A), F32)],
        operands=(proj, proj, wm, bias_full, lng, lnb, convw, og, bd), parallel=True)
    return y


def _mixer_backward(proj, dy, wm, bias_full, lng, lnb, convw, og, bd, causal, seq, name, comm=None):
    tokens = proj.shape[0]
    tm = _tile_rows(seq)
    per_seq = seq // tm
    n_sub = tm // CHUNK
    ext = tm + 2 * HALO

    def body(pm_ref, pp_ref, pn_ref, dy_ref, dyn_ref, wm_ref, bias_ref, lng_ref, lnb_ref, cw_ref, og_ref, bd_ref,
             causal_ref, dp_ref, dog_ref, dcw_ref, dlng_ref, dlnb_ref, dbias_ref, dwm_ref, mixed_scr, dvln_scr):
        i = pl.program_id(0)
        first = (i % per_seq) == 0
        last = (i % per_seq) == per_seq - 1

        @pl.when(i == 0)
        def _():
            dog_ref[...] = jnp.zeros_like(dog_ref)
            dcw_ref[...] = jnp.zeros_like(dcw_ref)
            dlng_ref[...] = jnp.zeros_like(dlng_ref)
            dlnb_ref[...] = jnp.zeros_like(dlnb_ref)
            dbias_ref[...] = jnp.zeros_like(dbias_ref)
            dwm_ref[...] = jnp.zeros_like(dwm_ref)

        ua, va, u_act, vhat, rstd, vln = _sgu_forward(pm_ref, wm_ref, bias_ref, lng_ref, lnb_ref, bd_ref, mixed_scr, n_sub)
        mixed = mixed_scr[...]
        ya = u_act * mixed
        ra = _rms(ya)
        yhat = ya * ra
        dya_in = dy_ref[:, 0:D_A]
        dog_ref[:, 0:D_A] = dog_ref[:, 0:D_A] + jnp.sum(dya_in * yhat, axis=0, keepdims=True)
        dyh = dya_in * og_ref[:, 0:D_A]
        dya = ra * (dyh - yhat * jnp.mean(dyh * yhat, axis=-1, keepdims=True))
        d_u = dya * mixed
        d_mixed = dya * u_act
        left = lax.broadcasted_iota(jnp.int32, (CHUNK, LANES), 1) < HEAD_DIM
        dbias = jnp.zeros((CHUNK, D_A), F32)
        for q in range(n_sub):
            rows = slice(q * CHUNK, (q + 1) * CHUNK)
            dbias = dbias + d_mixed[rows, :]
            for p in range(N_HEADS // 2):
                cols = slice(p * LANES, (p + 1) * LANES)
                dm = d_mixed[rows, cols]
                dm_l = jnp.where(left, dm, 0.0).astype(BF16)
                dm_r = jnp.where(left, 0.0, dm).astype(BF16)
                vp = vln[rows, cols].astype(BF16)
                dwm_ref[2 * p] = dwm_ref[2 * p] + causal_ref[...] * _dot_nt(dm_l, vp)
                dwm_ref[2 * p + 1] = dwm_ref[2 * p + 1] + causal_ref[...] * _dot_nt(dm_r, vp)
                dvln_scr[rows, cols] = _dot_tn(wm_ref[2 * p], dm_l) + _dot_tn(wm_ref[2 * p + 1], dm_r)
        dbias_ref[...] = dbias_ref[...] + dbias
        dvln = dvln_scr[...]
        dlng_ref[...] = dlng_ref[...] + jnp.sum(dvln * vhat, axis=0, keepdims=True)
        dlnb_ref[...] = dlnb_ref[...] + jnp.sum(dvln, axis=0, keepdims=True)
        dvh = dvln * lng_ref[...]
        bd = bd_ref[...]
        d_v = rstd * (dvh - _group_mean(dvh, bd) - vhat * _group_mean(dvh * vhat, bd))
        dp_ref[:, 0:D_A] = (d_u * _gelu_grad(ua)).astype(BF16)
        dp_ref[:, D_A:2 * D_A] = (d_v * _gelu_grad(va)).astype(BF16)

        def ext_cols(lo):
            cs = slice(lo, lo + D_A)
            return jnp.concatenate([pp_ref[:, cs], pm_ref[:, cs], pn_ref[:, cs]], axis=0).astype(F32)

        bg, cg, xb = ext_cols(2 * D_A), ext_cols(3 * D_A), ext_cols(4 * D_A)
        row = lax.broadcasted_iota(jnp.int32, (ext, D_A), 0)
        z = jnp.where(jnp.logical_and(first, row < HALO), 0.0, cg * xb)
        z1 = pltpu.roll(z, 1, 0)
        z2 = pltpu.roll(z, 2, 0)
        w0, w1, w2 = cw_ref[0:1, :], cw_ref[1:2, :], cw_ref[2:3, :]
        conv = w0 * z2 + w1 * z1 + w2 * z
        yb = bg * conv
        rb = _rms(yb)
        yhb = yb * rb
        dyn = jnp.where(last, 0.0, dyn_ref[:, D_A:2 * D_A])
        dyb_in = jnp.concatenate([jnp.zeros((HALO, D_A), F32), dy_ref[:, D_A:2 * D_A], dyn], axis=0)
        dyhb = dyb_in * og_ref[:, D_A:2 * D_A]
        dyb = rb * (dyhb - yhb * jnp.mean(dyhb * yhb, axis=-1, keepdims=True))
        d_conv = dyb * bg
        dz = w2 * d_conv + w1 * pltpu.roll(d_conv, ext - 1, 0) + w0 * pltpu.roll(d_conv, ext - 2, 0)
        main = slice(HALO, HALO + tm)
        dp_ref[:, 2 * D_A:3 * D_A] = (dyb * conv)[main].astype(BF16)
        dp_ref[:, 3 * D_A:4 * D_A] = (dz * xb)[main].astype(BF16)
        dp_ref[:, 4 * D_A:5 * D_A] = (dz * cg)[main].astype(BF16)
        dog_ref[:, D_A:2 * D_A] = dog_ref[:, D_A:2 * D_A] + jnp.sum((dyb_in * yhb)[main], axis=0, keepdims=True)
        dcm = d_conv[main]
        dcw_ref[0:1, :] = dcw_ref[0:1, :] + jnp.sum(dcm * z2[main], axis=0, keepdims=True)
        dcw_ref[1:2, :] = dcw_ref[1:2, :] + jnp.sum(dcm * z1[main], axis=0, keepdims=True)
        dcw_ref[2:3, :] = dcw_ref[2:3, :] + jnp.sum(dcm * z[main], axis=0, keepdims=True)

    prev_p, next_p = _halo_specs(tm, tokens, D_PROJ)
    _, next_d = _halo_specs(tm, tokens, D_MODEL)
    fixed2 = lambda shape: pl.BlockSpec(shape, lambda i: (0, 0))
    outs, got = _call(
        body, name=name, grid=(tokens // tm,),
        out_shape=[jax.ShapeDtypeStruct((tokens, D_PROJ), BF16), jax.ShapeDtypeStruct((1, D_MODEL), F32),
                   jax.ShapeDtypeStruct((8, D_A), F32), jax.ShapeDtypeStruct((1, D_A), F32),
                   jax.ShapeDtypeStruct((1, D_A), F32), jax.ShapeDtypeStruct((CHUNK, D_A), F32),
                   jax.ShapeDtypeStruct((N_HEADS, CHUNK, CHUNK), F32)],
        in_specs=[pl.BlockSpec((tm, D_PROJ), lambda i: (i, 0)), prev_p, next_p,
                  pl.BlockSpec((tm, D_MODEL), lambda i: (i, 0)), next_d,
                  _resident(wm.shape), _resident(bias_full.shape), _resident(lng.shape), _resident(lnb.shape),
                  _resident(convw.shape), _resident(og.shape), _resident(bd.shape), _resident(causal.shape)],
        out_specs=[pl.BlockSpec((tm, D_PROJ), lambda i: (i, 0)), fixed2((1, D_MODEL)), fixed2((8, D_A)),
                   fixed2((1, D_A)), fixed2((1, D_A)), fixed2((CHUNK, D_A)),
                   pl.BlockSpec((N_HEADS, CHUNK, CHUNK), lambda i: (0, 0, 0))],
        scratch_shapes=[pltpu.VMEM((tm, D_A), F32), pltpu.VMEM((tm, D_A), F32)],
        operands=(proj, proj, proj, dy, dy, wm, bias_full, lng, lnb, convw, og, bd, causal), comm=comm)
    return (*outs, got)


def _sum_slots(recv, name):
    _, rows, cols = recv.shape
    tr = rows // 2

    def body(r_ref, o_ref):
        acc = r_ref[0].astype(F32)
        for d in range(1, N_DEV):
            acc = acc + r_ref[d].astype(F32)
        o_ref[...] = acc

    return pl.pallas_call(
        body, name=name, grid=(2,),
        out_shape=jax.ShapeDtypeStruct((rows, cols), F32),
        in_specs=[pl.BlockSpec((N_DEV, tr, cols), lambda i: (0, i, 0))],
        out_specs=pl.BlockSpec((tr, cols), lambda i: (i, 0)),
        compiler_params=_params(parallel=True),
    )(recv)


def _adamw(w, g, m, v, name):
    rows, cols = w.shape
    tr = max(t for t in range(8, 513, 8) if rows % t == 0)

    def body(w_ref, g_ref, m_ref, v_ref, d_ref, nm_ref, nv_ref):
        gv = g_ref[...]
        nm = ADAM_B1 * m_ref[...] + (1.0 - ADAM_B1) * gv
        nv = ADAM_B2 * v_ref[...] + (1.0 - ADAM_B2) * (gv * gv)
        m_hat = nm / (1.0 - ADAM_B1 ** ADAM_STEP)
        v_hat = nv / (1.0 - ADAM_B2 ** ADAM_STEP)
        d_ref[...] = -ADAM_LR * (m_hat / (jnp.sqrt(v_hat) + ADAM_EPS) + ADAM_WD * w_ref[...])
        nm_ref[...] = nm
        nv_ref[...] = nv

    spec = pl.BlockSpec((tr, cols), lambda i: (i, 0))
    return pl.pallas_call(
        body, name=name, grid=(rows // tr,),
        out_shape=[jax.ShapeDtypeStruct((rows, cols), F32)] * 3,
        in_specs=[spec] * 4, out_specs=[spec] * 3,
        compiler_params=_params(parallel=True),
    )(w, g, m, v)


def _adamw_nd(w, g, m, v, name):
    shape = w.shape
    two_d = (-1, shape[-1])
    d, nm, nv = _adamw(w.reshape(two_d), g.reshape(two_d), m.reshape(two_d), v.reshape(two_d), name)
    return d.reshape(shape), nm.reshape(shape), nv.reshape(shape)


def kernel(x, c, ada_w, ada_b, norm_ffn1_g, ffn1_w_gu, ffn1_w_down, norm_mix_g, mix_w_in, sgu_ln_g, sgu_ln_b, sgu_w_s, sgu_b, conv_w, out_norm_g, mix_w_out, norm_ffn2_g, ffn2_w_gu, ffn2_w_down, final_norm_g, loss_target, m_ada_w, m_ada_b, m_norm_ffn1_g, m_ffn1_w_gu, m_ffn1_w_down, m_norm_mix_g, m_mix_w_in, m_sgu_ln_g, m_sgu_ln_b, m_sgu_w_s, m_sgu_b, m_conv_w, m_out_norm_g, m_mix_w_out, m_norm_ffn2_g, m_ffn2_w_gu, m_ffn2_w_down, m_final_norm_g, v_ada_w, v_ada_b, v_norm_ffn1_g, v_ffn1_w_gu, v_ffn1_w_down, v_norm_mix_g, v_mix_w_in, v_sgu_ln_g, v_sgu_ln_b, v_sgu_w_s, v_sgu_b, v_conv_w, v_out_norm_g, v_mix_w_out, v_norm_ffn2_g, v_ffn2_w_gu, v_ffn2_w_down, v_final_norm_g):
    batch, seq, _ = x.shape
    tokens = batch * seq
    me = 4 * lax.axis_index("x") + 2 * lax.axis_index("y") + lax.axis_index("c")
    weights = dict(ada_w=ada_w, ada_b=ada_b, norm_ffn1_g=norm_ffn1_g, ffn1_w_gu=ffn1_w_gu, ffn1_w_down=ffn1_w_down,
                   norm_mix_g=norm_mix_g, mix_w_in=mix_w_in, sgu_ln_g=sgu_ln_g, sgu_ln_b=sgu_ln_b, sgu_w_s=sgu_w_s,
                   sgu_b=sgu_b, conv_w=conv_w, out_norm_g=out_norm_g, mix_w_out=mix_w_out, norm_ffn2_g=norm_ffn2_g,
                   ffn2_w_gu=ffn2_w_gu, ffn2_w_down=ffn2_w_down, final_norm_g=final_norm_g)
    mom1 = dict(ada_w=m_ada_w, ada_b=m_ada_b, norm_ffn1_g=m_norm_ffn1_g, ffn1_w_gu=m_ffn1_w_gu,
                ffn1_w_down=m_ffn1_w_down, norm_mix_g=m_norm_mix_g, mix_w_in=m_mix_w_in, sgu_ln_g=m_sgu_ln_g,
                sgu_ln_b=m_sgu_ln_b, sgu_w_s=m_sgu_w_s, sgu_b=m_sgu_b, conv_w=m_conv_w, out_norm_g=m_out_norm_g,
                mix_w_out=m_mix_w_out, norm_ffn2_g=m_norm_ffn2_g, ffn2_w_gu=m_ffn2_w_gu, ffn2_w_down=m_ffn2_w_down,
                final_norm_g=m_final_norm_g)
    mom2 = dict(ada_w=v_ada_w, ada_b=v_ada_b, norm_ffn1_g=v_norm_ffn1_g, ffn1_w_gu=v_ffn1_w_gu,
                ffn1_w_down=v_ffn1_w_down, norm_mix_g=v_norm_mix_g, mix_w_in=v_mix_w_in, sgu_ln_g=v_sgu_ln_g,
                sgu_ln_b=v_sgu_ln_b, sgu_w_s=v_sgu_w_s, sgu_b=v_sgu_b, conv_w=v_conv_w, out_norm_g=v_out_norm_g,
                mix_w_out=v_mix_w_out, norm_ffn2_g=v_norm_ffn2_g, ffn2_w_gu=v_ffn2_w_gu, ffn2_w_down=v_ffn2_w_down,
                final_norm_g=v_final_norm_g)

    big = ("ffn1_w_gu", "ffn1_w_down", "mix_w_in", "mix_w_out", "ffn2_w_gu", "ffn2_w_down")
    transposed = ("ffn1_w_gu", "mix_w_in", "ffn2_w_gu")
    shard = {(l, nm): (weights[nm][l].T if nm in transposed else weights[nm][l]).astype(BF16)
             for l in range(DEPTH) for nm in big}
    full_w = {}

    def gather_of(keys):
        return keys, _GatherRows([shard[k] for k in keys])

    def landed(plan, got):
        full_w.update(zip(plan[0], got))

    plan = gather_of([(0, "ffn1_w_gu")])
    landed(plan, _comm_only(plan[1], "gather_first"))

    small_in = _pack_small([c, conv_w])
    small_all, _ = _all_gather_small(small_in, "gather_c")
    c_all, convw_all = _unpack_small(small_all, [c.shape, conv_w.shape], lead=(N_DEV,))
    c_all = c_all.reshape(N_DEV * batch, D_MODEL)
    convw_full = jnp.transpose(convw_all, (1, 2, 0, 3)).reshape(DEPTH, 3, D_A)
    ada_cols = ada_w.shape[2]
    ada_b_cols = lax.dynamic_slice_in_dim(ada_b, me * ada_cols, ada_cols, axis=1).reshape(DEPTH, 1, ada_cols)
    ada_local = _ada_forward(c_all, ada_w, ada_b_cols)
    ada_all, _ = _all_gather_small(ada_local.reshape(-1, LANES), "gather_ada")
    ada_all = ada_all.reshape(N_DEV, DEPTH, N_DEV * batch, ada_cols)
    ada_full = jnp.transpose(ada_all, (1, 2, 0, 3)).reshape(DEPTH, N_DEV * batch, N_MOD * D_MODEL)
    ada_mine = lax.dynamic_slice_in_dim(ada_full, me * batch, batch, axis=1)
    mod = ada_mine.reshape(DEPTH, batch, N_MOD, 1, D_MODEL)

    causal = jnp.tril(jnp.ones((CHUNK, CHUNK), F32))
    bd = jnp.kron(jnp.eye(N_HEADS, dtype=F32), jnp.full((HEAD_DIM, HEAD_DIM), 1.0 / HEAD_DIM, F32)).astype(BF16)
    row_vec = lambda a: a.reshape(1, -1)

    hosted_gathers = {
        (0, "ffn_up1"): [(0, "ffn1_w_down"), (0, "mix_w_in"), (0, "mix_w_out")],
        (0, "ffn_down1"): [(0, "ffn2_w_gu")],
        (0, "mix_in"): [(0, "ffn2_w_down")],
        (0, "ffn_up2"): [(1, "ffn1_w_gu"), (1, "ffn1_w_down")],
        (0, "ffn_down2"): [(1, "mix_w_in"), (1, "mix_w_out")],
        (1, "ffn_up1"): [(1, "ffn2_w_gu"), (1, "ffn2_w_down")],
    }

    def hosting(l, site):
        keys = hosted_gathers.get((l, site))
        return gather_of(keys) if keys else (None, None)

    xs = x.reshape(tokens, D_MODEL)
    saved = []
    for l in range(DEPTH):
        sh1, sc1, g1, sh2, sc2, g2, sh3, sc3, g3 = [mod[l, :, k] for k in range(N_MOD)]
        mixer_consts = dict(
            wm=(sgu_w_s[l] * causal[None]).astype(BF16),
            bias_full=jnp.repeat(sgu_b[l].T, HEAD_DIM, axis=1),
            lng=row_vec(jnp.tile(sgu_ln_g[l], N_HEADS)), lnb=row_vec(jnp.tile(sgu_ln_b[l], N_HEADS)),
            convw=jnp.pad(convw_full[l], ((0, 5), (0, 0))), og=row_vec(out_norm_g[l]), bd=bd)
        x0 = xs
        plan = hosting(l, "ffn_up1")
        h1, gu1, got = _normmod_matmul(x0, row_vec(norm_ffn1_g[l]), 1.0 + sc1, sh1, full_w[l, "ffn1_w_gu"], seq, "ffn_up", plan[1])
        if got:
            landed(plan, got)
        plan = hosting(l, "ffn_down1")
        x1, f1, got = _matmul_residual(gu1, full_w[l, "ffn1_w_down"], x0, g1, 0.5, True, seq, "ffn_down", plan[1])
        if got:
            landed(plan, got)
        plan = hosting(l, "mix_in")
        h2, proj, got = _normmod_matmul(x1, row_vec(norm_mix_g[l]), 1.0 + sc2, sh2, full_w[l, "mix_w_in"], seq, "mix_in", plan[1])
        if got:
            landed(plan, got)
        ymix = _mixer_forward(proj, seq=seq, name="mixer_forward", **mixer_consts)
        x2, o2, _ = _matmul_residual(ymix, full_w[l, "mix_w_out"], x1, g2, 1.0, False, seq, "mix_out")
        plan = hosting(l, "ffn_up2")
        h3, gu3, got = _normmod_matmul(x2, row_vec(norm_ffn2_g[l]), 1.0 + sc3, sh3, full_w[l, "ffn2_w_gu"], seq, "ffn_up", plan[1])
        if got:
            landed(plan, got)
        plan = hosting(l, "ffn_down2")
        x3, f3, got = _matmul_residual(gu3, full_w[l, "ffn2_w_down"], x2, g3, 0.5, True, seq, "ffn_down", plan[1])
        if got:
            landed(plan, got)
        saved.append(dict(x0=x0, x1=x1, x2=x2, h1=h1, h2=h2, h3=h3, gu1=gu1, gu3=gu3, f1=f1, f3=f3, o2=o2, proj=proj,
                          ymix=ymix, mixer_consts=mixer_consts, sc=(1.0 + sc1, 1.0 + sc2, 1.0 + sc3), gates=(g1, g2, g3)))
        xs = x3

    dx, d_final_g, loss_cols = _final_loss(xs, loss_target.reshape(tokens, D_MODEL), row_vec(final_norm_g), seq)

    recv = {}
    small_grads = [None] * DEPTH
    d_mod = [None] * DEPTH

    def scatter_of(l, nm, grad):
        return [(l, nm)], _ScatterRows([grad])

    for l in reversed(range(DEPTH)):
        sv = saved[l]
        mc = sv["mixer_consts"]
        df3, dg3, dgu3, a3, _ = _residual_backward(dx, sv["gates"][2], sv["f3"], sv["gu3"], full_w[l, "ffn2_w_down"], 0.5, True, seq, "ffn_down_bwd")
        gw_down2, _ = _weight_grad(a3, df3, seq, "grad_w_down")
        plan = scatter_of(l, "ffn2_w_down", gw_down2)
        dx2, dsh3, dsc3, dn3, got = _matmul_normmod_backward(dgu3, full_w[l, "ffn2_w_gu"], sv["x2"], dx, row_vec(norm_ffn2_g[l]), sv["sc"][2], seq, "ffn_up_bwd", plan[1])
        recv.update(zip(plan[0], got))
        gw_gu2, _ = _weight_grad(dgu3, sv["h3"], seq, "grad_w_gu")
        do2, dg2, dymix, _ = _residual_backward(dx2, sv["gates"][1], sv["o2"], None, full_w[l, "mix_w_out"], 1.0, False, seq, "mix_out_bwd")
        gw_out, _ = _weight_grad(sv["ymix"], do2, seq, "grad_w_out")
        plan = scatter_of(l, "ffn2_w_gu", gw_gu2)
        dproj, d_og, d_cw, d_lng, d_lnb, d_bias, d_wm, got = _mixer_backward(
            sv["proj"], dymix, causal=causal, seq=seq, name="mixer_backward", comm=plan[1], **mc)
        recv.update(zip(plan[0], got))
        plan = scatter_of(l, "mix_w_out", gw_out)
        dx1, dsh2, dsc2, dn2, got = _matmul_normmod_backward(dproj, full_w[l, "mix_w_in"], sv["x1"], dx2, row_vec(norm_mix_g[l]), sv["sc"][1], seq, "mix_in_bwd", plan[1])
        recv.update(zip(plan[0], got))
        gw_in, _ = _weight_grad(dproj, sv["h2"], seq, "grad_w_in")
        plan = scatter_of(l, "mix_w_in", gw_in)
        df1, dg1, dgu1, a1, got = _residual_backward(dx1, sv["gates"][0], sv["f1"], sv["gu1"], full_w[l, "ffn1_w_down"], 0.5, True, seq, "ffn_down_bwd", plan[1])
        recv.update(zip(plan[0], got))
        gw_down1, _ = _weight_grad(a1, df1, seq, "grad_w_down")
        plan = scatter_of(l, "ffn1_w_down", gw_down1)
        gw_gu1, got = _weight_grad(dgu1, sv["h1"], seq, "grad_w_gu", plan[1])
        recv.update(zip(plan[0], got))
        plan = scatter_of(l, "ffn1_w_gu", gw_gu1)
        dx0, dsh1, dsc1, dn1, got = _matmul_normmod_backward(dgu1, full_w[l, "ffn1_w_gu"], sv["x0"], dx1, row_vec(norm_ffn1_g[l]), sv["sc"][0], seq, "ffn_up_bwd", plan[1])
        recv.update(zip(plan[0], got))
        dx = dx0
        small_grads[l] = dict(
            norm_ffn1_g=dn1, norm_mix_g=dn2, norm_ffn2_g=dn3, out_norm_g=d_og,
            sgu_ln_g=d_lng.reshape(N_HEADS, HEAD_DIM).sum(0), sgu_ln_b=d_lnb.reshape(N_HEADS, HEAD_DIM).sum(0),
            sgu_w_s=d_wm, sgu_b=d_bias.reshape(CHUNK, N_HEADS, HEAD_DIM).sum(-1).T, conv_w=d_cw[0:3])
        d_mod[l] = jnp.concatenate([dsh1, dsc1, dg1, dsh2, dsc2, dg2, dsh3, dsc3, dg3], axis=1)
    grad_x = dx.reshape(batch, seq, D_MODEL)

    grad_big = {}
    for nm in big:
        per_layer = []
        for l in range(DEPTH):
            g_sum = _sum_slots(recv[l, nm], "sum_" + nm)
            per_layer.append(g_sum.T if nm in transposed else g_sum)
        grad_big[nm] = jnp.stack(per_layer)

    small_names = ("norm_ffn1_g", "norm_mix_g", "norm_ffn2_g", "out_norm_g", "sgu_ln_g", "sgu_ln_b", "sgu_w_s", "sgu_b", "conv_w")
    stacked = [jnp.stack([small_grads[l][nm] for l in range(DEPTH)]) for nm in small_names]
    parts = stacked + [d_final_g, loss_cols, jnp.stack(d_mod)]
    shapes = [p.shape for p in parts]
    packed_all, packed_sum = _all_gather_small(_pack_small(parts), "reduce_small")
    summed = _unpack_small(packed_sum, shapes)
    grad_small = {nm: summed[k].reshape(weights[nm].shape) if nm != "conv_w" else summed[k] for k, nm in enumerate(small_names)}
    grad_small["conv_w"] = lax.dynamic_slice_in_dim(grad_small["conv_w"], me * conv_w.shape[2], conv_w.shape[2], axis=2)
    grad_small["final_norm_g"] = summed[len(small_names)].reshape(final_norm_g.shape)
    loss = jnp.sum(summed[len(small_names) + 1])
    d_mod_all = _unpack_small(packed_all, shapes, lead=(N_DEV,))[-1]
    d_ada_all = jnp.transpose(d_mod_all, (1, 0, 2, 3, 4)).reshape(DEPTH, N_DEV * batch, N_MOD * D_MODEL)
    d_ada_cols = lax.dynamic_slice_in_dim(d_ada_all, me * ada_cols, ada_cols, axis=2)
    g_ada_w, g_ada_b = _ada_backward(c_all, d_ada_cols, d_ada_all)

    grads = dict(grad_big)
    grads.update(grad_small)
    grads["ada_w"] = g_ada_w
    grads["ada_b"] = g_ada_b.reshape(ada_b.shape)

    names = ("ada_w", "ada_b", "norm_ffn1_g", "ffn1_w_gu", "ffn1_w_down", "norm_mix_g", "mix_w_in", "sgu_ln_g",
             "sgu_ln_b", "sgu_w_s", "sgu_b", "conv_w", "out_norm_g", "mix_w_out", "norm_ffn2_g", "ffn2_w_gu",
             "ffn2_w_down", "final_norm_g")
    large = ("ada_w",) + big
    delta, new_m, new_v = {}, {}, {}
    for nm in large:
        delta[nm], new_m[nm], new_v[nm] = _adamw_nd(weights[nm], grads[nm], mom1[nm], mom2[nm], "adamw_" + nm)
    rest = [nm for nm in names if nm not in large]
    pack = lambda src: _pack_small([src[nm] for nm in rest])
    d_p, m_p, v_p = _adamw(pack(weights), pack(grads), pack(mom1), pack(mom2), "adamw_small")
    rest_shapes = [weights[nm].shape for nm in rest]
    for nm, d_k, m_k, v_k in zip(rest, _unpack_small(d_p, rest_shapes), _unpack_small(m_p, rest_shapes), _unpack_small(v_p, rest_shapes)):
        delta[nm], new_m[nm], new_v[nm] = d_k, m_k, v_k

    return (loss, grad_x, *[grads[nm] for nm in names], *[delta[nm] for nm in names],
            *[new_m[nm] for nm in names], *[new_v[nm] for nm in names])
```

```python
import math

import jax
import jax.numpy as jnp
from jax import lax
from jax.experimental import pallas as pl
from jax.experimental.pallas import tpu as pltpu

F32 = jnp.float32
BF16 = jnp.bfloat16

D_MODEL = 1024
D_FF = 2816
D_A = 512
D_PROJ = 2560
N_HEADS = 8
HEAD_DIM = 64
CHUNK = 128
N_MOD = 9
DEPTH = 2
EPS = 1e-6
N_DEV = 8
LANES = 128
MXU_N = 256
HALO = 16
VMEM_LIMIT = 56 * 1024 * 1024

ADAM_LR = 0.001
ADAM_B1 = 0.9
ADAM_B2 = 0.999
ADAM_EPS = 1e-08
ADAM_WD = 0.01
ADAM_STEP = 10

MESH = pl.DeviceIdType.MESH


def _dot(a, b):
    return jnp.dot(a, b, preferred_element_type=F32)


def _dot_nt(a, b):
    return lax.dot_general(a, b, (((1,), (1,)), ((), ())), preferred_element_type=F32)


def _dot_tn(a, b):
    return lax.dot_general(a, b, (((0,), (0,)), ((), ())), preferred_element_type=F32)


def _sigmoid(x):
    return 0.5 * jnp.tanh(0.5 * x) + 0.5


def _gelu(x):
    return 0.5 * x * (1.0 + lax.erf(x * (1.0 / math.sqrt(2.0))))


def _gelu_grad(x):
    cdf = 0.5 * (1.0 + lax.erf(x * (1.0 / math.sqrt(2.0))))
    return cdf + x * jnp.exp(-0.5 * x * x) * (1.0 / math.sqrt(2.0 * math.pi))


def _params(n_axes=1, parallel=False):
    sem = ("parallel" if parallel else "arbitrary",) * n_axes
    return pltpu.CompilerParams(dimension_semantics=sem, vmem_limit_bytes=VMEM_LIMIT)


def _resident(shape):
    nd = len(shape)
    return pl.BlockSpec(shape, lambda *_: (0,) * nd, pipeline_mode=pl.Buffered(1))


def _tile_rows(seq):
    return min(512, seq)


def _my_position():
    x, y, c = lax.axis_index("x"), lax.axis_index("y"), lax.axis_index("c")
    return x, y, c, 4 * x + 2 * y + c


def _peer(x, y, c, p):
    return (x ^ ((p >> 2) & 1), y ^ ((p >> 1) & 1), c ^ (p & 1))


class _GatherRows:
    def __init__(self, shards):
        self.operands = list(shards)
        n = len(shards)
        self.out_shape = [jax.ShapeDtypeStruct((N_DEV * s.shape[0], s.shape[1]), s.dtype) for s in shards]
        self.scratch = [pltpu.SemaphoreType.DMA((n, N_DEV - 1)), pltpu.SemaphoreType.DMA((n, N_DEV - 1)),
                        pltpu.SemaphoreType.DMA((n,))]

    def _plan(self, src, dst, send, recv, loc):
        x, y, c, _ = _my_position()
        me, sib = (x, y, c), (x, y, 1 - c)
        chips = [(1 - x, y), (x, 1 - y), (1 - x, 1 - y)]
        plans = []
        for k, shard in enumerate(self.operands):
            rows = shard.shape[0]

            def blk(pos, k=k, rows=rows):
                return dst[k].at[pl.ds((4 * pos[0] + 2 * pos[1] + pos[2]) * rows, rows), :]

            def rc(s, block, to, source=None, k=k, blk=blk):
                return pltpu.make_async_remote_copy(
                    src_ref=blk(block) if source is None else source, dst_ref=blk(block),
                    send_sem=send.at[k, s], recv_sem=recv.at[k, s], device_id=to, device_id_type=MESH)

            plans.append(dict(
                local=pltpu.make_async_copy(src[k], blk(me), loc.at[k]),
                first=[rc(0, me, sib, src[k])] + [rc(1 + j, me, (*chip, c), src[k]) for j, chip in enumerate(chips)],
                landed=[rc(1 + j, (*chip, c), me) for j, chip in enumerate(chips)],
                passed=[rc(4 + j, (*chip, c), sib) for j, chip in enumerate(chips)],
                from_sib=[rc(0, sib, me)] + [rc(4 + j, (*chip, 1 - c), me) for j, chip in enumerate(chips)]))
        return plans

    def start(self, src, dst, send, recv, loc):
        for plan in self._plan(src, dst, send, recv, loc):
            plan["local"].start()
            for cp in plan["first"]:
                cp.start()

    def finish(self, src, dst, send, recv, loc):
        plans = self._plan(src, dst, send, recv, loc)
        for plan in plans:
            for landed, passed in zip(plan["landed"], plan["passed"]):
                landed.wait_recv()
                passed.start()
        for plan in plans:
            for cp in plan["from_sib"]:
                cp.wait_recv()
            for cp in plan["first"] + plan["passed"]:
                cp.wait_send()
            plan["local"].wait()


class _ScatterRows:
    def __init__(self, grads):
        self.operands = list(grads)
        n = len(grads)
        self.out_shape = [jax.ShapeDtypeStruct((N_DEV, g.shape[0] // N_DEV, g.shape[1]), g.dtype) for g in grads]
        self.scratch = [pltpu.SemaphoreType.DMA((n, N_DEV - 1)), pltpu.SemaphoreType.DMA((n, N_DEV - 1)),
                        pltpu.SemaphoreType.DMA((n,))]

    def _plan(self, src, dst, send, recv, loc):
        x, y, c, me = _my_position()
        copies = []
        for k, grad in enumerate(self.operands):
            rows = grad.shape[0] // N_DEV
            copies.append(pltpu.make_async_copy(src[k].at[pl.ds(me * rows, rows), :], dst[k].at[me], loc.at[k]))
            for p in range(1, N_DEV):
                px, py, pc = _peer(x, y, c, p)
                copies.append(pltpu.make_async_remote_copy(
                    src_ref=src[k].at[pl.ds((4 * px + 2 * py + pc) * rows, rows), :], dst_ref=dst[k].at[me],
                    send_sem=send.at[k, p - 1], recv_sem=recv.at[k, p - 1], device_id=(px, py, pc), device_id_type=MESH))
        return copies

    def start(self, src, dst, send, recv, loc):
        for cp in self._plan(src, dst, send, recv, loc):
            cp.start()

    def finish(self, src, dst, send, recv, loc):
        for cp in self._plan(src, dst, send, recv, loc):
            cp.wait()


_ANY = pl.BlockSpec(memory_space=pl.ANY)


def _comm_only(comm, name):
    n = len(comm.operands)

    def body(*refs):
        src, dst, sems = refs[:n], refs[n:2 * n], refs[2 * n:]
        comm.start(src, dst, *sems)
        comm.finish(src, dst, *sems)

    return pl.pallas_call(body, name=name, out_shape=comm.out_shape, in_specs=[_ANY] * n, out_specs=[_ANY] * n,
                          scratch_shapes=comm.scratch)(*comm.operands)


def _call(body, *, name, grid, in_specs, out_specs, out_shape, operands, scratch_shapes=(), parallel=False, comm=None):
    n_axes = len(grid)
    if comm is None:
        outs = pl.pallas_call(body, name=name, grid=grid, out_shape=list(out_shape), in_specs=list(in_specs),
                              out_specs=list(out_specs), scratch_shapes=list(scratch_shapes),
                              compiler_params=_params(n_axes, parallel))(*operands)
        return list(outs), None
    n_in, n_out, n_scr, n_c = len(in_specs), len(out_specs), len(scratch_shapes), len(comm.operands)
    total = math.prod(grid)

    def hosted(*refs):
        ins, c_src = refs[:n_in], refs[n_in:n_in + n_c]
        outs, c_dst = refs[n_in + n_c:n_in + n_c + n_out], refs[n_in + n_c + n_out:n_in + 2 * n_c + n_out]
        scr, sems = refs[n_in + 2 * n_c + n_out:n_in + 2 * n_c + n_out + n_scr], refs[n_in + 2 * n_c + n_out + n_scr:]
        step = pl.program_id(0)
        for axis in range(1, n_axes):
            step = step * grid[axis] + pl.program_id(axis)

        @pl.when(step == 0)
        def _():
            comm.start(c_src, c_dst, *sems)

        body(*ins, *outs, *scr)

        @pl.when(step == total - 1)
        def _():
            comm.finish(c_src, c_dst, *sems)

    res = pl.pallas_call(hosted, name=name, grid=grid, out_shape=list(out_shape) + comm.out_shape,
                         in_specs=list(in_specs) + [_ANY] * n_c, out_specs=list(out_specs) + [_ANY] * n_c,
                         scratch_shapes=list(scratch_shapes) + comm.scratch,
                         compiler_params=_params(n_axes, False))(*operands, *comm.operands)
    return list(res[:n_out]), list(res[n_out:])


def _all_gather_small(v, name):
    rows = v.shape[0]

    def body(v_ref, all_ref, sum_ref, send_sems, recv_sems):
        x, y, c, me = _my_position()
        all_ref[me] = v_ref[...]
        copies = []
        for p in range(1, N_DEV):
            cp = pltpu.make_async_remote_copy(
                src_ref=v_ref, dst_ref=all_ref.at[me], send_sem=send_sems.at[p - 1], recv_sem=recv_sems.at[p - 1],
                device_id=_peer(x, y, c, p), device_id_type=MESH)
            cp.start()
            copies.append(cp)
        for cp in copies:
            cp.wait()
        acc = all_ref[0]
        for d in range(1, N_DEV):
            acc = acc + all_ref[d]
        sum_ref[...] = acc

    return pl.pallas_call(
        body, name=name,
        out_shape=[jax.ShapeDtypeStruct((N_DEV, rows, LANES), F32), jax.ShapeDtypeStruct((rows, LANES), F32)],
        in_specs=[pl.BlockSpec(memory_space=pltpu.VMEM)],
        out_specs=[pl.BlockSpec(memory_space=pltpu.VMEM)] * 2,
        scratch_shapes=[pltpu.SemaphoreType.DMA((N_DEV - 1,)), pltpu.SemaphoreType.DMA((N_DEV - 1,))],
        compiler_params=pltpu.CompilerParams(vmem_limit_bytes=VMEM_LIMIT),
    )(v)


def _sum_gathered(gathered, name):
    rows = gathered.shape[1]

    def body(g_ref, o_ref):
        acc = g_ref[0]
        for d in range(1, N_DEV):
            acc = acc + g_ref[d]
        o_ref[...] = acc

    return pl.pallas_call(
        body, name=name, out_shape=jax.ShapeDtypeStruct((rows, LANES), F32),
        in_specs=[pl.BlockSpec(memory_space=pltpu.VMEM)], out_specs=pl.BlockSpec(memory_space=pltpu.VMEM),
        compiler_params=pltpu.CompilerParams(vmem_limit_bytes=VMEM_LIMIT),
    )(gathered)


def _pack_small(parts):
    flat = jnp.concatenate([p.reshape(-1).astype(F32) for p in parts])
    total = flat.shape[0]
    padded = -(-total // (8 * LANES)) * (8 * LANES)
    flat = jnp.pad(flat, (0, padded - total))
    return flat.reshape(padded // LANES, LANES)


def _unpack_small(packed, shapes, lead=()):
    flat = packed.reshape(lead + (-1,))
    out, off = [], 0
    for shp in shapes:
        size = math.prod(shp)
        out.append(flat[..., off:off + size].reshape(lead + tuple(shp)))
        off += size
    return out


def _ada_forward(c_all, ada_w, ada_b_cols):
    nb = c_all.shape[0]
    cols = ada_w.shape[2]

    def body(c_ref, w_ref, b_ref, o_ref):
        cv = c_ref[...]
        act = (cv * _sigmoid(cv)).astype(BF16)
        o_ref[0] = _dot(act, w_ref[0].astype(BF16)) + b_ref[0]

    return pl.pallas_call(
        body, name="ada_forward", grid=(DEPTH,),
        out_shape=jax.ShapeDtypeStruct((DEPTH, nb, cols), F32),
        in_specs=[pl.BlockSpec((nb, D_MODEL), lambda l: (0, 0)),
                  pl.BlockSpec((1, D_MODEL, cols), lambda l: (l, 0, 0)),
                  pl.BlockSpec((1, 1, cols), lambda l: (l, 0, 0))],
        out_specs=pl.BlockSpec((1, nb, cols), lambda l: (l, 0, 0)),
        compiler_params=_params(),
    )(c_all, ada_w, ada_b_cols)


def _ada_backward(c_all, d_ada_cols, d_ada_all):
    nb = c_all.shape[0]
    cols = d_ada_cols.shape[2]
    full = d_ada_all.shape[2]

    def body(c_ref, dc_ref, da_ref, gw_ref, gb_ref):
        cv = c_ref[...]
        act = (cv * _sigmoid(cv)).astype(BF16)
        gw_ref[0] = _dot_tn(act, dc_ref[0].astype(BF16))
        gb_ref[0] = jnp.sum(da_ref[0], axis=0, keepdims=True)

    return pl.pallas_call(
        body, name="ada_backward", grid=(DEPTH,),
        out_shape=[jax.ShapeDtypeStruct((DEPTH, D_MODEL, cols), F32), jax.ShapeDtypeStruct((DEPTH, 1, full), F32)],
        in_specs=[pl.BlockSpec((nb, D_MODEL), lambda l: (0, 0)),
                  pl.BlockSpec((1, nb, cols), lambda l: (l, 0, 0)),
                  pl.BlockSpec((1, nb, full), lambda l: (l, 0, 0))],
        out_specs=[pl.BlockSpec((1, D_MODEL, cols), lambda l: (l, 0, 0)),
                   pl.BlockSpec((1, 1, full), lambda l: (l, 0, 0))],
        compiler_params=_params(),
    )(c_all, d_ada_cols, d_ada_all)


def _rms(xv):
    return lax.rsqrt(jnp.mean(xv * xv, axis=-1, keepdims=True) + EPS)


def _normmod_matmul(x, gnorm, scale1p, shift, w_t, seq, name, swiglu=False, comm=None):
    tokens, n_out = x.shape[0], w_t.shape[0]
    tm = _tile_rows(seq)
    per_seq = seq // tm
    width = n_out // 2 if swiglu else n_out
    n_chunks = width // MXU_N

    def body(x_ref, g_ref, sc_ref, sh_ref, w_ref, h_ref, *o_refs):
        xv = x_ref[...]
        h = (xv * _rms(xv) * g_ref[...]) * sc_ref[0] + sh_ref[0]
        h_ref[...] = h.astype(BF16)
        for ck in range(n_chunks):
            cs = slice(ck * MXU_N, (ck + 1) * MXU_N)
            if swiglu:
                act_ref, silu_ref, dact_ref = o_refs
                g = _dot_nt(h_ref[...], w_ref[cs, :])
                u = _dot_nt(h_ref[...], w_ref[width + ck * MXU_N:width + (ck + 1) * MXU_N, :])
                sig = _sigmoid(g)
                silu = g * sig
                act_ref[:, cs] = (silu * u).astype(BF16)
                silu_ref[:, cs] = silu.astype(BF16)
                dact_ref[:, cs] = (u * (sig + silu * (1.0 - sig))).astype(BF16)
            else:
                o_refs[0][:, cs] = _dot_nt(h_ref[...], w_ref[cs, :]).astype(BF16)

    n_res = 3 if swiglu else 1
    per_batch = pl.BlockSpec((1, 1, D_MODEL), lambda i: (i // per_seq, 0, 0))
    outs, got = _call(
        body, name=name, grid=(tokens // tm,),
        out_shape=[jax.ShapeDtypeStruct((tokens, D_MODEL), BF16)] + [jax.ShapeDtypeStruct((tokens, width), BF16)] * n_res,
        in_specs=[pl.BlockSpec((tm, D_MODEL), lambda i: (i, 0)), _resident((1, D_MODEL)), per_batch, per_batch,
                  _resident(w_t.shape)],
        out_specs=[pl.BlockSpec((tm, D_MODEL), lambda i: (i, 0))] + [pl.BlockSpec((tm, width), lambda i: (i, 0))] * n_res,
        operands=(x, gnorm, scale1p, shift, w_t), parallel=True, comm=comm)
    return (*outs, got)


def _matmul_residual(src, w, x, gate, scale, seq, name, comm=None):
    tokens, k_dim = x.shape[0], w.shape[0]
    tm = _tile_rows(seq)
    per_seq = seq // tm

    def body(s_ref, w_ref, x_ref, gate_ref, xo_ref, f_ref):
        f = _dot(s_ref[...], w_ref[...])
        f_ref[...] = f.astype(BF16)
        xo_ref[...] = x_ref[...] + (scale * gate_ref[0]) * f

    (x_out, f), got = _call(
        body, name=name, grid=(tokens // tm,),
        out_shape=[jax.ShapeDtypeStruct((tokens, D_MODEL), F32), jax.ShapeDtypeStruct((tokens, D_MODEL), BF16)],
        in_specs=[pl.BlockSpec((tm, k_dim), lambda i: (i, 0)), _resident(w.shape),
                  pl.BlockSpec((tm, D_MODEL), lambda i: (i, 0)),
                  pl.BlockSpec((1, 1, D_MODEL), lambda i: (i // per_seq, 0, 0))],
        out_specs=[pl.BlockSpec((tm, D_MODEL), lambda i: (i, 0))] * 2,
        operands=(src, w, x, gate), parallel=True, comm=comm)
    return x_out, f, got


def _residual_backward(dy, gate, f, w, scale, seq, name, silu=None, dact=None, comm=None):
    tokens, k_dim = dy.shape[0], w.shape[0]
    batch = tokens // seq
    tm = _tile_rows(seq)
    per_seq = seq // tm
    n_chunks = k_dim // MXU_N
    swiglu = silu is not None

    def body(*refs):
        if swiglu:
            dy_ref, gate_ref, f_ref, silu_ref, dact_ref, w_ref, df_ref, dgate_ref, dgu_ref = refs
        else:
            dy_ref, gate_ref, f_ref, w_ref, df_ref, dgate_ref, dsrc_ref = refs
        i = pl.program_id(0)
        dy_v = dy_ref[...]
        df_ref[...] = ((scale * gate_ref[0]) * dy_v).astype(BF16)
        part = scale * jnp.sum(dy_v * f_ref[...].astype(F32), axis=0, keepdims=True)

        @pl.when(i % per_seq == 0)
        def _():
            dgate_ref[0] = part

        @pl.when(i % per_seq != 0)
        def _():
            dgate_ref[0] = dgate_ref[0] + part

        if swiglu:
            for ck in range(n_chunks):
                cs = slice(ck * MXU_N, (ck + 1) * MXU_N)
                cu = slice(k_dim + ck * MXU_N, k_dim + (ck + 1) * MXU_N)
                da = _dot_nt(df_ref[...], w_ref[cs, :])
                dgu_ref[:, cs] = (da * dact_ref[:, cs].astype(F32)).astype(BF16)
                dgu_ref[:, cu] = (da * silu_ref[:, cs].astype(F32)).astype(BF16)
        else:
            dsrc_ref[...] = _dot_nt(df_ref[...], w_ref[...])

    row = lambda i: (i, 0)
    per_batch = pl.BlockSpec((1, 1, D_MODEL), lambda i: (i // per_seq, 0, 0))
    in_specs = [pl.BlockSpec((tm, D_MODEL), row), per_batch, pl.BlockSpec((tm, D_MODEL), row)]
    out_shape = [jax.ShapeDtypeStruct((tokens, D_MODEL), BF16), jax.ShapeDtypeStruct((batch, 1, D_MODEL), F32)]
    out_specs = [pl.BlockSpec((tm, D_MODEL), row), per_batch]
    if swiglu:
        in_specs += [pl.BlockSpec((tm, k_dim), row)] * 2
        operands = (dy, gate, f, silu, dact, w)
        out_shape += [jax.ShapeDtypeStruct((tokens, 2 * k_dim), BF16)]
        out_specs += [pl.BlockSpec((tm, 2 * k_dim), row)]
    else:
        operands = (dy, gate, f, w)
        out_shape += [jax.ShapeDtypeStruct((tokens, k_dim), F32)]
        out_specs += [pl.BlockSpec((tm, k_dim), row)]
    in_specs += [_resident(w.shape)]
    outs, got = _call(body, name=name, grid=(tokens // tm,), out_shape=out_shape, in_specs=in_specs,
                      out_specs=out_specs, operands=operands, comm=comm)
    return (*outs, got)


def _matmul_normmod_backward(dsrc, w_t, x, dy, gnorm, scale1p, seq, name, comm=None):
    tokens, k_dim = dsrc.shape
    batch = tokens // seq
    tm = _tile_rows(seq)
    per_seq = seq // tm

    def body(ds_ref, w_ref, x_ref, dy_ref, g_ref, sc_ref, dx_ref, dsh_ref, dsc_ref, dg_ref):
        i = pl.program_id(0)
        dh = _dot(ds_ref[...], w_ref[...])
        xv = x_ref[...]
        r = _rms(xv)
        xn = xv * r
        gn = g_ref[...]
        dsh = jnp.sum(dh, axis=0, keepdims=True)
        dsc = jnp.sum(dh * (xn * gn), axis=0, keepdims=True)
        dhn = dh * sc_ref[0]
        dg = jnp.sum(dhn * xn, axis=0, keepdims=True)
        dxn = dhn * gn
        dx_ref[...] = dy_ref[...] + r * (dxn - xn * jnp.mean(dxn * xn, axis=-1, keepdims=True))

        @pl.when(i % per_seq == 0)
        def _():
            dsh_ref[0] = dsh
            dsc_ref[0] = dsc

        @pl.when(i % per_seq != 0)
        def _():
            dsh_ref[0] = dsh_ref[0] + dsh
            dsc_ref[0] = dsc_ref[0] + dsc

        @pl.when(i == 0)
        def _():
            dg_ref[...] = dg

        @pl.when(i != 0)
        def _():
            dg_ref[...] = dg_ref[...] + dg

    row = lambda i: (i, 0)
    per_batch = pl.BlockSpec((1, 1, D_MODEL), lambda i: (i // per_seq, 0, 0))
    outs, got = _call(
        body, name=name, grid=(tokens // tm,),
        out_shape=[jax.ShapeDtypeStruct((tokens, D_MODEL), F32), jax.ShapeDtypeStruct((batch, 1, D_MODEL), F32),
                   jax.ShapeDtypeStruct((batch, 1, D_MODEL), F32), jax.ShapeDtypeStruct((1, D_MODEL), F32)],
        in_specs=[pl.BlockSpec((tm, k_dim), row), _resident(w_t.shape), pl.BlockSpec((tm, D_MODEL), row),
                  pl.BlockSpec((tm, D_MODEL), row), _resident((1, D_MODEL)), per_batch],
        out_specs=[pl.BlockSpec((tm, D_MODEL), row), per_batch, per_batch, pl.BlockSpec((1, D_MODEL), lambda i: (0, 0))],
        operands=(dsrc, w_t, x, dy, gnorm, scale1p), comm=comm)
    return (*outs, got)


def _weight_grad(a, b, seq, name, comm=None):
    tokens, n_out = a.shape
    tn = MXU_N

    def body(a_ref, b_ref, o_ref):
        o_ref[...] = _dot_tn(a_ref[...], b_ref[...]).astype(BF16)

    (out,), got = _call(
        body, name=name, grid=(n_out // tn,),
        out_shape=[jax.ShapeDtypeStruct((n_out, D_MODEL), BF16)],
        in_specs=[pl.BlockSpec((tokens, tn), lambda j: (0, j)), _resident((tokens, D_MODEL))],
        out_specs=[pl.BlockSpec((tn, D_MODEL), lambda j: (j, 0))],
        operands=(a, b), parallel=True, comm=comm)
    return out, got


def _final_loss(x, target, gnorm, seq):
    tokens = x.shape[0]
    tm = _tile_rows(seq)

    def body(x_ref, t_ref, g_ref, dx_ref, dg_ref, loss_ref):
        i = pl.program_id(0)
        xv = x_ref[...]
        r = _rms(xv)
        xn = xv * r
        gn = g_ref[...]
        err = xn * gn - t_ref[...]
        loss = (0.5 / D_MODEL) * jnp.sum(err * err, axis=0, keepdims=True)
        dyv = err * (1.0 / D_MODEL)
        dg = jnp.sum(dyv * xn, axis=0, keepdims=True)
        dxn = dyv * gn
        dx_ref[...] = r * (dxn - xn * jnp.mean(dxn * xn, axis=-1, keepdims=True))

        @pl.when(i == 0)
        def _():
            dg_ref[...] = dg
            loss_ref[...] = loss

        @pl.when(i != 0)
        def _():
            dg_ref[...] = dg_ref[...] + dg
            loss_ref[...] = loss_ref[...] + loss

    row = lambda i: (i, 0)
    fixed = pl.BlockSpec((1, D_MODEL), lambda i: (0, 0))
    outs, _ = _call(
        body, name="final_loss", grid=(tokens // tm,),
        out_shape=[jax.ShapeDtypeStruct((tokens, D_MODEL), F32), jax.ShapeDtypeStruct((1, D_MODEL), F32),
                   jax.ShapeDtypeStruct((1, D_MODEL), F32)],
        in_specs=[pl.BlockSpec((tm, D_MODEL), row), pl.BlockSpec((tm, D_MODEL), row), _resident((1, D_MODEL))],
        out_specs=[pl.BlockSpec((tm, D_MODEL), row), fixed, fixed],
        operands=(x, target, gnorm))
    return outs


def _group_mean(v, bd):
    hi = v.astype(BF16)
    lo = (v - hi.astype(F32)).astype(BF16)
    return _dot(hi, bd) + _dot(lo, bd)


def _sgu_forward(pm_ref, wm_ref, bias_ref, lng_ref, lnb_ref, bd_ref, mixed_scr, n_sub):
    ua = pm_ref[:, 0:D_A].astype(F32)
    va = pm_ref[:, D_A:2 * D_A].astype(F32)
    u_act = _gelu(ua)
    v_act = _gelu(va)
    bd = bd_ref[...]
    vc = v_act - _group_mean(v_act, bd)
    rstd = lax.rsqrt(_group_mean(vc * vc, bd) + EPS)
    vhat = vc * rstd
    vln = vhat * lng_ref[...] + lnb_ref[...]
    left = lax.broadcasted_iota(jnp.int32, (CHUNK, LANES), 1) < HEAD_DIM
    for q in range(n_sub):
        rows = slice(q * CHUNK, (q + 1) * CHUNK)
        for p in range(N_HEADS // 2):
            cols = slice(p * LANES, (p + 1) * LANES)
            vp = vln[rows, cols]
            v_l = jnp.where(left, vp, 0.0).astype(BF16)
            v_r = jnp.where(left, 0.0, vp).astype(BF16)
            mixed_scr[rows, cols] = _dot(wm_ref[2 * p], v_l) + _dot(wm_ref[2 * p + 1], v_r) + bias_ref[:, cols]
    return ua, va, u_act, vhat, rstd, vln


def _halo_specs(tm, tokens, width):
    prev = pl.BlockSpec((HALO, width), lambda i: (jnp.maximum(i * (tm // HALO) - 1, 0), 0))
    nxt = pl.BlockSpec((HALO, width), lambda i: (jnp.minimum((i + 1) * (tm // HALO), tokens // HALO - 1), 0))
    return prev, nxt


def _mixer_forward(proj, wm, bias_full, lng, lnb, convw, og, bd, seq, name):
    tokens = proj.shape[0]
    tm = _tile_rows(seq)
    per_seq = seq // tm
    n_sub = tm // CHUNK

    def body(pm_ref, pp_ref, wm_ref, bias_ref, lng_ref, lnb_ref, cw_ref, og_ref, bd_ref, y_ref, mixed_scr):
        i = pl.program_id(0)
        first = (i % per_seq) == 0
        _, _, u_act, _, _, _ = _sgu_forward(pm_ref, wm_ref, bias_ref, lng_ref, lnb_ref, bd_ref, mixed_scr, n_sub)
        ya = u_act * mixed_scr[...]
        y_ref[:, 0:D_A] = (ya * _rms(ya) * og_ref[:, 0:D_A]).astype(BF16)

        bg = pm_ref[:, 2 * D_A:3 * D_A].astype(F32)
        z = pm_ref[:, 3 * D_A:4 * D_A].astype(F32) * pm_ref[:, 4 * D_A:5 * D_A].astype(F32)
        zp = pp_ref[:, 3 * D_A:4 * D_A].astype(F32) * pp_ref[:, 4 * D_A:5 * D_A].astype(F32)
        zp = jnp.where(first, 0.0, zp)
        zext = jnp.concatenate([zp, z], axis=0)
        z1 = pltpu.roll(zext, 1, 0)[HALO:]
        z2 = pltpu.roll(zext, 2, 0)[HALO:]
        conv = cw_ref[0:1, :] * z2 + cw_ref[1:2, :] * z1 + cw_ref[2:3, :] * z
        yb = bg * conv
        y_ref[:, D_A:2 * D_A] = (yb * _rms(yb) * og_ref[:, D_A:2 * D_A]).astype(BF16)

    prev, _ = _halo_specs(tm, tokens, D_PROJ)
    (y,), _ = _call(
        body, name=name, grid=(tokens // tm,),
        out_shape=[jax.ShapeDtypeStruct((tokens, D_MODEL), BF16)],
        in_specs=[pl.BlockSpec((tm, D_PROJ), lambda i: (i, 0)), prev, _resident(wm.shape), _resident(bias_full.shape),
                  _resident(lng.shape), _resident(lnb.shape), _resident(convw.shape), _resident(og.shape),
                  _resident(bd.shape)],
        out_specs=[pl.BlockSpec((tm, D_MODEL), lambda i: (i, 0))],
        scratch_shapes=[pltpu.VMEM((tm, D_A), F32)],
        operands=(proj, proj, wm, bias_full, lng, lnb, convw, og, bd), parallel=True)
    return y


def _mixer_backward(proj, dy, wm, bias_full, lng, lnb, convw, og, bd, causal, seq, name, comm=None):
    tokens = proj.shape[0]
    tm = _tile_rows(seq)
    per_seq = seq // tm
    n_sub = tm // CHUNK
    ext = tm + 2 * HALO

    def body(pm_ref, pp_ref, pn_ref, dy_ref, dyn_ref, wm_ref, bias_ref, lng_ref, lnb_ref, cw_ref, og_ref, bd_ref,
             causal_ref, dp_ref, dog_ref, dcw_ref, dlng_ref, dlnb_ref, dbias_ref, dwm_ref, mixed_scr, dvln_scr):
        i = pl.program_id(0)
        first = (i % per_seq) == 0
        last = (i % per_seq) == per_seq - 1

        @pl.when(i == 0)
        def _():
            dog_ref[...] = jnp.zeros_like(dog_ref)
            dcw_ref[...] = jnp.zeros_like(dcw_ref)
            dlng_ref[...] = jnp.zeros_like(dlng_ref)
            dlnb_ref[...] = jnp.zeros_like(dlnb_ref)
            dbias_ref[...] = jnp.zeros_like(dbias_ref)
            dwm_ref[...] = jnp.zeros_like(dwm_ref)

        ua, va, u_act, vhat, rstd, vln = _sgu_forward(pm_ref, wm_ref, bias_ref, lng_ref, lnb_ref, bd_ref, mixed_scr, n_sub)
        mixed = mixed_scr[...]
        ya = u_act * mixed
        ra = _rms(ya)
        yhat = ya * ra
        dya_in = dy_ref[:, 0:D_A]
        dog_ref[:, 0:D_A] = dog_ref[:, 0:D_A] + jnp.sum(dya_in * yhat, axis=0, keepdims=True)
        dyh = dya_in * og_ref[:, 0:D_A]
        dya = ra * (dyh - yhat * jnp.mean(dyh * yhat, axis=-1, keepdims=True))
        d_u = dya * mixed
        d_mixed = dya * u_act
        left = lax.broadcasted_iota(jnp.int32, (CHUNK, LANES), 1) < HEAD_DIM
        dbias = jnp.zeros((CHUNK, D_A), F32)
        for q in range(n_sub):
            rows = slice(q * CHUNK, (q + 1) * CHUNK)
            dbias = dbias + d_mixed[rows, :]
            for p in range(N_HEADS // 2):
                cols = slice(p * LANES, (p + 1) * LANES)
                dm = d_mixed[rows, cols]
                dm_l = jnp.where(left, dm, 0.0).astype(BF16)
                dm_r = jnp.where(left, 0.0, dm).astype(BF16)
                vp = vln[rows, cols].astype(BF16)
                dwm_ref[2 * p] = dwm_ref[2 * p] + causal_ref[...] * _dot_nt(dm_l, vp)
                dwm_ref[2 * p + 1] = dwm_ref[2 * p + 1] + causal_ref[...] * _dot_nt(dm_r, vp)
                dvln_scr[rows, cols] = _dot_tn(wm_ref[2 * p], dm_l) + _dot_tn(wm_ref[2 * p + 1], dm_r)
        dbias_ref[...] = dbias_ref[...] + dbias
        dvln = dvln_scr[...]
        dlng_ref[...] = dlng_ref[...] + jnp.sum(dvln * vhat, axis=0, keepdims=True)
        dlnb_ref[...] = dlnb_ref[...] + jnp.sum(dvln, axis=0, keepdims=True)
        dvh = dvln * lng_ref[...]
        bd = bd_ref[...]
        d_v = rstd * (dvh - _group_mean(dvh, bd) - vhat * _group_mean(dvh * vhat, bd))
        dp_ref[:, 0:D_A] = (d_u * _gelu_grad(ua)).astype(BF16)
        dp_ref[:, D_A:2 * D_A] = (d_v * _gelu_grad(va)).astype(BF16)

        def ext_cols(lo):
            cs = slice(lo, lo + D_A)
            return jnp.concatenate([pp_ref[:, cs], pm_ref[:, cs], pn_ref[:, cs]], axis=0).astype(F32)

        bg, cg, xb = ext_cols(2 * D_A), ext_cols(3 * D_A), ext_cols(4 * D_A)
        row = lax.broadcasted_iota(jnp.int32, (ext, D_A), 0)
        z = jnp.where(jnp.logical_and(first, row < HALO), 0.0, cg * xb)
        z1 = pltpu.roll(z, 1, 0)
        z2 = pltpu.roll(z, 2, 0)
        w0, w1, w2 = cw_ref[0:1, :], cw_ref[1:2, :], cw_ref[2:3, :]
        conv = w0 * z2 + w1 * z1 + w2 * z
        yb = bg * conv
        rb = _rms(yb)
        yhb = yb * rb
        dyn = jnp.where(last, 0.0, dyn_ref[:, D_A:2 * D_A])
        dyb_in = jnp.concatenate([jnp.zeros((HALO, D_A), F32), dy_ref[:, D_A:2 * D_A], dyn], axis=0)
        dyhb = dyb_in * og_ref[:, D_A:2 * D_A]
        dyb = rb * (dyhb - yhb * jnp.mean(dyhb * yhb, axis=-1, keepdims=True))
        d_conv = dyb * bg
        dz = w2 * d_conv + w1 * pltpu.roll(d_conv, ext - 1, 0) + w0 * pltpu.roll(d_conv, ext - 2, 0)
        main = slice(HALO, HALO + tm)
        dp_ref[:, 2 * D_A:3 * D_A] = (dyb * conv)[main].astype(BF16)
        dp_ref[:, 3 * D_A:4 * D_A] = (dz * xb)[main].astype(BF16)
        dp_ref[:, 4 * D_A:5 * D_A] = (dz * cg)[main].astype(BF16)
        dog_ref[:, D_A:2 * D_A] = dog_ref[:, D_A:2 * D_A] + jnp.sum((dyb_in * yhb)[main], axis=0, keepdims=True)
        dcm = d_conv[main]
        dcw_ref[0:1, :] = dcw_ref[0:1, :] + jnp.sum(dcm * z2[main], axis=0, keepdims=True)
        dcw_ref[1:2, :] = dcw_ref[1:2, :] + jnp.sum(dcm * z1[main], axis=0, keepdims=True)
        dcw_ref[2:3, :] = dcw_ref[2:3, :] + jnp.sum(dcm * z[main], axis=0, keepdims=True)

    prev_p, next_p = _halo_specs(tm, tokens, D_PROJ)
    _, next_d = _halo_specs(tm, tokens, D_MODEL)
    fixed2 = lambda shape: pl.BlockSpec(shape, lambda i: (0, 0))
    outs, got = _call(
        body, name=name, grid=(tokens // tm,),
        out_shape=[jax.ShapeDtypeStruct((tokens, D_PROJ), BF16), jax.ShapeDtypeStruct((1, D_MODEL), F32),
                   jax.ShapeDtypeStruct((8, D_A), F32), jax.ShapeDtypeStruct((1, D_A), F32),
                   jax.ShapeDtypeStruct((1, D_A), F32), jax.ShapeDtypeStruct((CHUNK, D_A), F32),
                   jax.ShapeDtypeStruct((N_HEADS, CHUNK, CHUNK), F32)],
        in_specs=[pl.BlockSpec((tm, D_PROJ), lambda i: (i, 0)), prev_p, next_p,
                  pl.BlockSpec((tm, D_MODEL), lambda i: (i, 0)), next_d,
                  _resident(wm.shape), _resident(bias_full.shape), _resident(lng.shape), _resident(lnb.shape),
                  _resident(convw.shape), _resident(og.shape), _resident(bd.shape), _resident(causal.shape)],
        out_specs=[pl.BlockSpec((tm, D_PROJ), lambda i: (i, 0)), fixed2((1, D_MODEL)), fixed2((8, D_A)),
                   fixed2((1, D_A)), fixed2((1, D_A)), fixed2((CHUNK, D_A)),
                   pl.BlockSpec((N_HEADS, CHUNK, CHUNK), lambda i: (0, 0, 0))],
        scratch_shapes=[pltpu.VMEM((tm, D_A), F32), pltpu.VMEM((tm, D_A), F32)],
        operands=(proj, proj, proj, dy, dy, wm, bias_full, lng, lnb, convw, og, bd, causal), comm=comm)
    return (*outs, got)


def _sum_slots(recv, name):
    _, rows, cols = recv.shape
    tr = rows // 2

    def body(r_ref, o_ref):
        acc = r_ref[0].astype(F32)
        for d in range(1, N_DEV):
            acc = acc + r_ref[d].astype(F32)
        o_ref[...] = acc

    return pl.pallas_call(
        body, name=name, grid=(2,),
        out_shape=jax.ShapeDtypeStruct((rows, cols), F32),
        in_specs=[pl.BlockSpec((N_DEV, tr, cols), lambda i: (0, i, 0))],
        out_specs=pl.BlockSpec((tr, cols), lambda i: (i, 0)),
        compiler_params=_params(parallel=True),
    )(recv)


def _adamw(w, g, m, v, name):
    rows, cols = w.shape
    tr = max(t for t in range(8, 513, 8) if rows % t == 0)

    def body(w_ref, g_ref, m_ref, v_ref, d_ref, nm_ref, nv_ref):
        gv = g_ref[...]
        nm = ADAM_B1 * m_ref[...] + (1.0 - ADAM_B1) * gv
        nv = ADAM_B2 * v_ref[...] + (1.0 - ADAM_B2) * (gv * gv)
        m_hat = nm / (1.0 - ADAM_B1 ** ADAM_STEP)
        v_hat = nv / (1.0 - ADAM_B2 ** ADAM_STEP)
        d_ref[...] = -ADAM_LR * (m_hat / (jnp.sqrt(v_hat) + ADAM_EPS) + ADAM_WD * w_ref[...])
        nm_ref[...] = nm
        nv_ref[...] = nv

    spec = pl.BlockSpec((tr, cols), lambda i: (i, 0))
    return pl.pallas_call(
        body, name=name, grid=(rows // tr,),
        out_shape=[jax.ShapeDtypeStruct((rows, cols), F32)] * 3,
        in_specs=[spec] * 4, out_specs=[spec] * 3,
        compiler_params=_params(parallel=True),
    )(w, g, m, v)


def _adamw_nd(w, g, m, v, name):
    shape = w.shape
    two_d = (-1, shape[-1])
    d, nm, nv = _adamw(w.reshape(two_d), g.reshape(two_d), m.reshape(two_d), v.reshape(two_d), name)
    return d.reshape(shape), nm.reshape(shape), nv.reshape(shape)


def kernel(x, c, ada_w, ada_b, norm_ffn1_g, ffn1_w_gu, ffn1_w_down, norm_mix_g, mix_w_in, sgu_ln_g, sgu_ln_b, sgu_w_s, sgu_b, conv_w, out_norm_g, mix_w_out, norm_ffn2_g, ffn2_w_gu, ffn2_w_down, final_norm_g, loss_target, m_ada_w, m_ada_b, m_norm_ffn1_g, m_ffn1_w_gu, m_ffn1_w_down, m_norm_mix_g, m_mix_w_in, m_sgu_ln_g, m_sgu_ln_b, m_sgu_w_s, m_sgu_b, m_conv_w, m_out_norm_g, m_mix_w_out, m_norm_ffn2_g, m_ffn2_w_gu, m_ffn2_w_down, m_final_norm_g, v_ada_w, v_ada_b, v_norm_ffn1_g, v_ffn1_w_gu, v_ffn1_w_down, v_norm_mix_g, v_mix_w_in, v_sgu_ln_g, v_sgu_ln_b, v_sgu_w_s, v_sgu_b, v_conv_w, v_out_norm_g, v_mix_w_out, v_norm_ffn2_g, v_ffn2_w_gu, v_ffn2_w_down, v_final_norm_g):
    batch, seq, _ = x.shape
    tokens = batch * seq
    me = 4 * lax.axis_index("x") + 2 * lax.axis_index("y") + lax.axis_index("c")
    weights = dict(ada_w=ada_w, ada_b=ada_b, norm_ffn1_g=norm_ffn1_g, ffn1_w_gu=ffn1_w_gu, ffn1_w_down=ffn1_w_down,
                   norm_mix_g=norm_mix_g, mix_w_in=mix_w_in, sgu_ln_g=sgu_ln_g, sgu_ln_b=sgu_ln_b, sgu_w_s=sgu_w_s,
                   sgu_b=sgu_b, conv_w=conv_w, out_norm_g=out_norm_g, mix_w_out=mix_w_out, norm_ffn2_g=norm_ffn2_g,
                   ffn2_w_gu=ffn2_w_gu, ffn2_w_down=ffn2_w_down, final_norm_g=final_norm_g)
    mom1 = dict(ada_w=m_ada_w, ada_b=m_ada_b, norm_ffn1_g=m_norm_ffn1_g, ffn1_w_gu=m_ffn1_w_gu,
                ffn1_w_down=m_ffn1_w_down, norm_mix_g=m_norm_mix_g, mix_w_in=m_mix_w_in, sgu_ln_g=m_sgu_ln_g,
                sgu_ln_b=m_sgu_ln_b, sgu_w_s=m_sgu_w_s, sgu_b=m_sgu_b, conv_w=m_conv_w, out_norm_g=m_out_norm_g,
                mix_w_out=m_mix_w_out, norm_ffn2_g=m_norm_ffn2_g, ffn2_w_gu=m_ffn2_w_gu, ffn2_w_down=m_ffn2_w_down,
                final_norm_g=m_final_norm_g)
    mom2 = dict(ada_w=v_ada_w, ada_b=v_ada_b, norm_ffn1_g=v_norm_ffn1_g, ffn1_w_gu=v_ffn1_w_gu,
                ffn1_w_down=v_ffn1_w_down, norm_mix_g=v_norm_mix_g, mix_w_in=v_mix_w_in, sgu_ln_g=v_sgu_ln_g,
                sgu_ln_b=v_sgu_ln_b, sgu_w_s=v_sgu_w_s, sgu_b=v_sgu_b, conv_w=v_conv_w, out_norm_g=v_out_norm_g,
                mix_w_out=v_mix_w_out, norm_ffn2_g=v_norm_ffn2_g, ffn2_w_gu=v_ffn2_w_gu, ffn2_w_down=v_ffn2_w_down,
                final_norm_g=v_final_norm_g)

    big = ("ffn1_w_gu", "ffn1_w_down", "mix_w_in", "mix_w_out", "ffn2_w_gu", "ffn2_w_down")
    transposed = ("ffn1_w_gu", "mix_w_in", "ffn2_w_gu")
    shard = {(l, nm): (weights[nm][l].T if nm in transposed else weights[nm][l]).astype(BF16)
             for l in range(DEPTH) for nm in big}
    full_w = {}

    def gather_of(keys):
        return keys, _GatherRows([shard[k] for k in keys])

    def landed(plan, got):
        full_w.update(zip(plan[0], got))

    plan = gather_of([(0, "ffn1_w_gu")])
    landed(plan, _comm_only(plan[1], "gather_first"))

    small_in = _pack_small([c, conv_w])
    small_all, _ = _all_gather_small(small_in, "gather_c")
    c_all, convw_all = _unpack_small(small_all, [c.shape, conv_w.shape], lead=(N_DEV,))
    c_all = c_all.reshape(N_DEV * batch, D_MODEL)
    convw_full = jnp.transpose(convw_all, (1, 2, 0, 3)).reshape(DEPTH, 3, D_A)
    ada_cols = ada_w.shape[2]
    ada_b_cols = lax.dynamic_slice_in_dim(ada_b, me * ada_cols, ada_cols, axis=1).reshape(DEPTH, 1, ada_cols)
    ada_local = _ada_forward(c_all, ada_w, ada_b_cols)
    ada_all, _ = _all_gather_small(ada_local.reshape(-1, LANES), "gather_ada")
    ada_all = ada_all.reshape(N_DEV, DEPTH, N_DEV * batch, ada_cols)
    ada_full = jnp.transpose(ada_all, (1, 2, 0, 3)).reshape(DEPTH, N_DEV * batch, N_MOD * D_MODEL)
    ada_mine = lax.dynamic_slice_in_dim(ada_full, me * batch, batch, axis=1)
    mod = ada_mine.reshape(DEPTH, batch, N_MOD, 1, D_MODEL)

    causal = jnp.tril(jnp.ones((CHUNK, CHUNK), F32))
    bd = jnp.kron(jnp.eye(N_HEADS, dtype=F32), jnp.full((HEAD_DIM, HEAD_DIM), 1.0 / HEAD_DIM, F32)).astype(BF16)
    row_vec = lambda a: a.reshape(1, -1)

    hosted_gathers = {
        (0, "ffn_up1"): [(0, "ffn1_w_down"), (0, "mix_w_in"), (0, "mix_w_out")],
        (0, "ffn_down1"): [(0, "ffn2_w_gu")],
        (0, "mix_in"): [(0, "ffn2_w_down")],
        (0, "ffn_up2"): [(1, "ffn1_w_gu"), (1, "ffn1_w_down")],
        (0, "ffn_down2"): [(1, "mix_w_in"), (1, "mix_w_out")],
        (1, "ffn_up1"): [(1, "ffn2_w_gu"), (1, "ffn2_w_down")],
    }

    def hosting(l, site):
        keys = hosted_gathers.get((l, site))
        return gather_of(keys) if keys else (None, None)

    xs = x.reshape(tokens, D_MODEL)
    saved = []
    for l in range(DEPTH):
        sh1, sc1, g1, sh2, sc2, g2, sh3, sc3, g3 = [mod[l, :, k] for k in range(N_MOD)]
        mixer_consts = dict(
            wm=(sgu_w_s[l] * causal[None]).astype(BF16),
            bias_full=jnp.repeat(sgu_b[l].T, HEAD_DIM, axis=1),
            lng=row_vec(jnp.tile(sgu_ln_g[l], N_HEADS)), lnb=row_vec(jnp.tile(sgu_ln_b[l], N_HEADS)),
            convw=jnp.pad(convw_full[l], ((0, 5), (0, 0))), og=row_vec(out_norm_g[l]), bd=bd)
        x0 = xs
        plan = hosting(l, "ffn_up1")
        h1, a1, s1, w1, got = _normmod_matmul(x0, row_vec(norm_ffn1_g[l]), 1.0 + sc1, sh1, full_w[l, "ffn1_w_gu"], seq, "ffn_up", True, plan[1])
        if got:
            landed(plan, got)
        plan = hosting(l, "ffn_down1")
        x1, f1, got = _matmul_residual(a1, full_w[l, "ffn1_w_down"], x0, g1, 0.5, seq, "ffn_down", plan[1])
        if got:
            landed(plan, got)
        plan = hosting(l, "mix_in")
        h2, proj, got = _normmod_matmul(x1, row_vec(norm_mix_g[l]), 1.0 + sc2, sh2, full_w[l, "mix_w_in"], seq, "mix_in", False, plan[1])
        if got:
            landed(plan, got)
        ymix = _mixer_forward(proj, seq=seq, name="mixer_forward", **mixer_consts)
        x2, o2, _ = _matmul_residual(ymix, full_w[l, "mix_w_out"], x1, g2, 1.0, seq, "mix_out")
        plan = hosting(l, "ffn_up2")
        h3, a3, s3, w3, got = _normmod_matmul(x2, row_vec(norm_ffn2_g[l]), 1.0 + sc3, sh3, full_w[l, "ffn2_w_gu"], seq, "ffn_up", True, plan[1])
        if got:
            landed(plan, got)
        plan = hosting(l, "ffn_down2")
        x3, f3, got = _matmul_residual(a3, full_w[l, "ffn2_w_down"], x2, g3, 0.5, seq, "ffn_down", plan[1])
        if got:
            landed(plan, got)
        saved.append(dict(x0=x0, x1=x1, x2=x2, h1=h1, h2=h2, h3=h3, a1=a1, s1=s1, w1=w1, a3=a3, s3=s3, w3=w3, f1=f1, f3=f3, o2=o2, proj=proj,
                          ymix=ymix, mixer_consts=mixer_consts, sc=(1.0 + sc1, 1.0 + sc2, 1.0 + sc3), gates=(g1, g2, g3)))
        xs = x3

    dx, d_final_g, loss_cols = _final_loss(xs, loss_target.reshape(tokens, D_MODEL), row_vec(final_norm_g), seq)

    recv = {}
    small_grads = [None] * DEPTH
    d_mod = [None] * DEPTH

    def scatter_of(l, nm, grad):
        return [(l, nm)], _ScatterRows([grad])

    small_names = ("norm_ffn1_g", "norm_mix_g", "norm_ffn2_g", "out_norm_g", "sgu_ln_g", "sgu_ln_b", "sgu_w_s", "sgu_b", "conv_w")

    def small_parts(l):
        return [small_grads[l][nm] for nm in small_names] + [d_mod[l]]

    for l in reversed(range(DEPTH)):
        sv = saved[l]
        mc = sv["mixer_consts"]
        plan = (None, None)
        if l + 1 < DEPTH:
            plan = [("small", l + 1)], _GatherRows([_pack_small(small_parts(l + 1))])
        df3, dg3, dgu3, got = _residual_backward(dx, sv["gates"][2], sv["f3"], full_w[l, "ffn2_w_down"], 0.5, seq, "ffn_down_bwd", sv["s3"], sv["w3"], plan[1])
        if got:
            recv.update(zip(plan[0], got))
        gw_down2, _ = _weight_grad(sv["a3"], df3, seq, "grad_w_down")
        plan = scatter_of(l, "ffn2_w_down", gw_down2)
        dx2, dsh3, dsc3, dn3, got = _matmul_normmod_backward(dgu3, full_w[l, "ffn2_w_gu"], sv["x2"], dx, row_vec(norm_ffn2_g[l]), sv["sc"][2], seq, "ffn_up_bwd", plan[1])
        recv.update(zip(plan[0], got))
        gw_gu2, _ = _weight_grad(dgu3, sv["h3"], seq, "grad_w_gu")
        do2, dg2, dymix, _ = _residual_backward(dx2, sv["gates"][1], sv["o2"], full_w[l, "mix_w_out"], 1.0, seq, "mix_out_bwd")
        gw_out, _ = _weight_grad(sv["ymix"], do2, seq, "grad_w_out")
        plan = scatter_of(l, "ffn2_w_gu", gw_gu2)
        dproj, d_og, d_cw, d_lng, d_lnb, d_bias, d_wm, got = _mixer_backward(
            sv["proj"], dymix, causal=causal, seq=seq, name="mixer_backward", comm=plan[1], **mc)
        recv.update(zip(plan[0], got))
        plan = scatter_of(l, "mix_w_out", gw_out)
        dx1, dsh2, dsc2, dn2, got = _matmul_normmod_backward(dproj, full_w[l, "mix_w_in"], sv["x1"], dx2, row_vec(norm_mix_g[l]), sv["sc"][1], seq, "mix_in_bwd", plan[1])
        recv.update(zip(plan[0], got))
        gw_in, _ = _weight_grad(dproj, sv["h2"], seq, "grad_w_in")
        plan = scatter_of(l, "mix_w_in", gw_in)
        df1, dg1, dgu1, got = _residual_backward(dx1, sv["gates"][0], sv["f1"], full_w[l, "ffn1_w_down"], 0.5, seq, "ffn_down_bwd", sv["s1"], sv["w1"], plan[1])
        recv.update(zip(plan[0], got))
        gw_down1, _ = _weight_grad(sv["a1"], df1, seq, "grad_w_down")
        plan = scatter_of(l, "ffn1_w_down", gw_down1)
        gw_gu1, got = _weight_grad(dgu1, sv["h1"], seq, "grad_w_gu", plan[1])
        recv.update(zip(plan[0], got))
        plan = scatter_of(l, "ffn1_w_gu", gw_gu1)
        dx0, dsh1, dsc1, dn1, got = _matmul_normmod_backward(dgu1, full_w[l, "ffn1_w_gu"], sv["x0"], dx1, row_vec(norm_ffn1_g[l]), sv["sc"][0], seq, "ffn_up_bwd", plan[1])
        recv.update(zip(plan[0], got))
        dx = dx0
        small_grads[l] = dict(
            norm_ffn1_g=dn1, norm_mix_g=dn2, norm_ffn2_g=dn3, out_norm_g=d_og,
            sgu_ln_g=d_lng.reshape(N_HEADS, HEAD_DIM).sum(0), sgu_ln_b=d_lnb.reshape(N_HEADS, HEAD_DIM).sum(0),
            sgu_w_s=d_wm, sgu_b=d_bias.reshape(CHUNK, N_HEADS, HEAD_DIM).sum(-1).T, conv_w=d_cw[0:3])
        d_mod[l] = jnp.concatenate([dsh1, dsc1, dg1, dsh2, dsc2, dg2, dsh3, dsc3, dg3], axis=1)
    grad_x = dx.reshape(batch, seq, D_MODEL)

    grad_big = {}
    for nm in big:
        per_layer = []
        for l in range(DEPTH):
            g_sum = _sum_slots(recv[l, nm], "sum_" + nm)
            per_layer.append(g_sum.T if nm in transposed else g_sum)
        grad_big[nm] = jnp.stack(per_layer)

    last_parts = small_parts(0) + [d_final_g, loss_cols]
    last_shapes = [p.shape for p in last_parts]
    packed_all, packed_sum = _all_gather_small(_pack_small(last_parts), "reduce_small")
    summed = {0: _unpack_small(packed_sum, last_shapes)}
    d_mod_dev = {0: _unpack_small(packed_all, last_shapes, lead=(N_DEV,))[len(small_names)]}
    for l in range(1, DEPTH):
        shapes_l = [p.shape for p in small_parts(l)]
        gathered_l = recv["small", l].reshape(N_DEV, -1, LANES)
        summed[l] = _unpack_small(_sum_gathered(gathered_l, "sum_small"), shapes_l)
        d_mod_dev[l] = _unpack_small(gathered_l, shapes_l, lead=(N_DEV,))[len(small_names)]
    grad_small = {nm: jnp.stack([summed[l][k] for l in range(DEPTH)]).reshape(
        (DEPTH, 3, D_A) if nm == "conv_w" else weights[nm].shape) for k, nm in enumerate(small_names)}
    grad_small["conv_w"] = lax.dynamic_slice_in_dim(grad_small["conv_w"], me * conv_w.shape[2], conv_w.shape[2], axis=2)
    grad_small["final_norm_g"] = summed[0][len(small_names) + 1].reshape(final_norm_g.shape)
    loss = jnp.sum(summed[0][len(small_names) + 2])
    d_ada_all = jnp.stack([d_mod_dev[l] for l in range(DEPTH)]).reshape(DEPTH, N_DEV * batch, N_MOD * D_MODEL)
    d_ada_cols = lax.dynamic_slice_in_dim(d_ada_all, me * ada_cols, ada_cols, axis=2)
    g_ada_w, g_ada_b = _ada_backward(c_all, d_ada_cols, d_ada_all)

    grads = dict(grad_big)
    grads.update(grad_small)
    grads["ada_w"] = g_ada_w
    grads["ada_b"] = g_ada_b.reshape(ada_b.shape)

    names = ("ada_w", "ada_b", "norm_ffn1_g", "ffn1_w_gu", "ffn1_w_down", "norm_mix_g", "mix_w_in", "sgu_ln_g",
             "sgu_ln_b", "sgu_w_s", "sgu_b", "conv_w", "out_norm_g", "mix_w_out", "norm_ffn2_g", "ffn2_w_gu",
             "ffn2_w_down", "final_norm_g")
    large = ("ada_w",) + big
    delta, new_m, new_v = {}, {}, {}
    for nm in large:
        delta[nm], new_m[nm], new_v[nm] = _adamw_nd(weights[nm], grads[nm], mom1[nm], mom2[nm], "adamw_" + nm)
    rest = [nm for nm in names if nm not in large]
    pack = lambda src: _pack_small([src[nm] for nm in rest])
    d_p, m_p, v_p = _adamw(pack(weights), pack(grads), pack(mom1), pack(mom2), "adamw_small")
    rest_shapes = [weights[nm].shape for nm in rest]
    for nm, d_k, m_k, v_k in zip(rest, _unpack_small(d_p, rest_shapes), _unpack_small(m_p, rest_shapes), _unpack_small(v_p, rest_shapes)):
        delta[nm], new_m[nm], new_v[nm] = d_k, m_k, v_k

    return (loss, grad_x, *[grads[nm] for nm in names], *[delta[nm] for nm in names],
            *[new_m[nm] for nm in names], *[new_v[nm] for nm in names])
```

```python
import math

import jax
import jax.numpy as jnp
from jax import lax
from jax.experimental import pallas as pl
from jax.experimental.pallas import tpu as pltpu

F32 = jnp.float32
BF16 = jnp.bfloat16

D_MODEL = 1024
D_FF = 2816
D_A = 512
D_PROJ = 2560
N_HEADS = 8
HEAD_DIM = 64
CHUNK = 128
N_MOD = 9
DEPTH = 2
EPS = 1e-6
N_DEV = 8
LANES = 128
MXU_N = 256
HALO = 16
VMEM_LIMIT = 56 * 1024 * 1024

ADAM_LR = 0.001
ADAM_B1 = 0.9
ADAM_B2 = 0.999
ADAM_EPS = 1e-08
ADAM_WD = 0.01
ADAM_STEP = 10

MESH = pl.DeviceIdType.MESH


def _dot(a, b):
    return jnp.dot(a, b, preferred_element_type=F32)


def _dot_nt(a, b):
    return lax.dot_general(a, b, (((1,), (1,)), ((), ())), preferred_element_type=F32)


def _dot_tn(a, b):
    return lax.dot_general(a, b, (((0,), (0,)), ((), ())), preferred_element_type=F32)


def _sigmoid(x):
    return 0.5 * jnp.tanh(0.5 * x) + 0.5


def _gelu(x):
    return 0.5 * x * (1.0 + lax.erf(x * (1.0 / math.sqrt(2.0))))


def _gelu_grad(x):
    cdf = 0.5 * (1.0 + lax.erf(x * (1.0 / math.sqrt(2.0))))
    return cdf + x * jnp.exp(-0.5 * x * x) * (1.0 / math.sqrt(2.0 * math.pi))


def _params(n_axes=1, parallel=False):
    sem = ("parallel" if parallel else "arbitrary",) * n_axes
    return pltpu.CompilerParams(dimension_semantics=sem, vmem_limit_bytes=VMEM_LIMIT)


def _resident(shape):
    nd = len(shape)
    return pl.BlockSpec(shape, lambda *_: (0,) * nd, pipeline_mode=pl.Buffered(1))


def _tile_rows(seq):
    return min(512, seq)


def _my_position():
    x, y, c = lax.axis_index("x"), lax.axis_index("y"), lax.axis_index("c")
    return x, y, c, 4 * x + 2 * y + c


def _peer(x, y, c, p):
    return (x ^ ((p >> 2) & 1), y ^ ((p >> 1) & 1), c ^ (p & 1))


class _GatherRows:
    def __init__(self, shards):
        self.operands = list(shards)
        n = len(shards)
        self.out_shape = [jax.ShapeDtypeStruct((N_DEV * s.shape[0], s.shape[1]), s.dtype) for s in shards]
        self.scratch = [pltpu.SemaphoreType.DMA((n, N_DEV - 1)), pltpu.SemaphoreType.DMA((n, N_DEV - 1)),
                        pltpu.SemaphoreType.DMA((n,))]

    def _plan(self, src, dst, send, recv, loc):
        x, y, c, _ = _my_position()
        me, sib = (x, y, c), (x, y, 1 - c)
        chips = [(1 - x, y), (x, 1 - y), (1 - x, 1 - y)]
        plans = []
        for k, shard in enumerate(self.operands):
            rows = shard.shape[0]

            def blk(pos, k=k, rows=rows):
                return dst[k].at[pl.ds((4 * pos[0] + 2 * pos[1] + pos[2]) * rows, rows), :]

            def rc(s, block, to, source=None, k=k, blk=blk):
                return pltpu.make_async_remote_copy(
                    src_ref=blk(block) if source is None else source, dst_ref=blk(block),
                    send_sem=send.at[k, s], recv_sem=recv.at[k, s], device_id=to, device_id_type=MESH)

            plans.append(dict(
                local=pltpu.make_async_copy(src[k], blk(me), loc.at[k]),
                first=[rc(0, me, sib, src[k])] + [rc(1 + j, me, (*chip, c), src[k]) for j, chip in enumerate(chips)],
                landed=[rc(1 + j, (*chip, c), me) for j, chip in enumerate(chips)],
                passed=[rc(4 + j, (*chip, c), sib) for j, chip in enumerate(chips)],
                from_sib=[rc(0, sib, me)] + [rc(4 + j, (*chip, 1 - c), me) for j, chip in enumerate(chips)]))
        return plans

    def start(self, src, dst, send, recv, loc):
        for plan in self._plan(src, dst, send, recv, loc):
            plan["local"].start()
            for cp in plan["first"]:
                cp.start()

    def finish(self, src, dst, send, recv, loc):
        plans = self._plan(src, dst, send, recv, loc)
        for plan in plans:
            for landed, passed in zip(plan["landed"], plan["passed"]):
                landed.wait_recv()
                passed.start()
        for plan in plans:
            for cp in plan["from_sib"]:
                cp.wait_recv()
            for cp in plan["first"] + plan["passed"]:
                cp.wait_send()
            plan["local"].wait()


class _ScatterRows:
    def __init__(self, grads):
        self.operands = list(grads)
        n = len(grads)
        self.out_shape = [jax.ShapeDtypeStruct((N_DEV, g.shape[0] // N_DEV, g.shape[1]), g.dtype) for g in grads]
        self.scratch = [pltpu.SemaphoreType.DMA((n, N_DEV - 1)), pltpu.SemaphoreType.DMA((n, N_DEV - 1)),
                        pltpu.SemaphoreType.DMA((n,))]

    def _plan(self, src, dst, send, recv, loc):
        x, y, c, me = _my_position()
        copies = []
        for k, grad in enumerate(self.operands):
            rows = grad.shape[0] // N_DEV
            copies.append(pltpu.make_async_copy(src[k].at[pl.ds(me * rows, rows), :], dst[k].at[me], loc.at[k]))
            for p in range(1, N_DEV):
                px, py, pc = _peer(x, y, c, p)
                copies.append(pltpu.make_async_remote_copy(
                    src_ref=src[k].at[pl.ds((4 * px + 2 * py + pc) * rows, rows), :], dst_ref=dst[k].at[me],
                    send_sem=send.at[k, p - 1], recv_sem=recv.at[k, p - 1], device_id=(px, py, pc), device_id_type=MESH))
        return copies

    def start(self, src, dst, send, recv, loc):
        for cp in self._plan(src, dst, send, recv, loc):
            cp.start()

    def finish(self, src, dst, send, recv, loc):
        for cp in self._plan(src, dst, send, recv, loc):
            cp.wait()


class _Exchanges:
    def __init__(self, parts):
        self.parts = list(parts)
        self.operands = [op for part in self.parts for op in part.operands]
        self.out_shape = [shp for part in self.parts for shp in part.out_shape]
        self.scratch = [scr for part in self.parts for scr in part.scratch]

    def _each(self, src, dst, sems):
        at, sem_at = 0, 0
        for part in self.parts:
            n, n_sem = len(part.operands), len(part.scratch)
            yield part, src[at:at + n], dst[at:at + n], sems[sem_at:sem_at + n_sem]
            at, sem_at = at + n, sem_at + n_sem

    def start(self, src, dst, *sems):
        for part, part_src, part_dst, part_sems in self._each(src, dst, sems):
            part.start(part_src, part_dst, *part_sems)

    def finish(self, src, dst, *sems):
        for part, part_src, part_dst, part_sems in self._each(src, dst, sems):
            part.finish(part_src, part_dst, *part_sems)


_ANY = pl.BlockSpec(memory_space=pl.ANY)


def _comm_only(comm, name):
    n = len(comm.operands)

    def body(*refs):
        src, dst, sems = refs[:n], refs[n:2 * n], refs[2 * n:]
        comm.start(src, dst, *sems)
        comm.finish(src, dst, *sems)

    return pl.pallas_call(body, name=name, out_shape=comm.out_shape, in_specs=[_ANY] * n, out_specs=[_ANY] * n,
                          scratch_shapes=comm.scratch)(*comm.operands)


def _call(body, *, name, grid, in_specs, out_specs, out_shape, operands, scratch_shapes=(), parallel=False, comm=None):
    n_axes = len(grid)
    if comm is None:
        outs = pl.pallas_call(body, name=name, grid=grid, out_shape=list(out_shape), in_specs=list(in_specs),
                              out_specs=list(out_specs), scratch_shapes=list(scratch_shapes),
                              compiler_params=_params(n_axes, parallel))(*operands)
        return list(outs), None
    n_in, n_out, n_scr, n_c = len(in_specs), len(out_specs), len(scratch_shapes), len(comm.operands)
    total = math.prod(grid)

    def hosted(*refs):
        ins, c_src = refs[:n_in], refs[n_in:n_in + n_c]
        outs, c_dst = refs[n_in + n_c:n_in + n_c + n_out], refs[n_in + n_c + n_out:n_in + 2 * n_c + n_out]
        scr, sems = refs[n_in + 2 * n_c + n_out:n_in + 2 * n_c + n_out + n_scr], refs[n_in + 2 * n_c + n_out + n_scr:]
        step = pl.program_id(0)
        for axis in range(1, n_axes):
            step = step * grid[axis] + pl.program_id(axis)

        @pl.when(step == 0)
        def _():
            comm.start(c_src, c_dst, *sems)

        body(*ins, *outs, *scr)

        @pl.when(step == total - 1)
        def _():
            comm.finish(c_src, c_dst, *sems)

    res = pl.pallas_call(hosted, name=name, grid=grid, out_shape=list(out_shape) + comm.out_shape,
                         in_specs=list(in_specs) + [_ANY] * n_c, out_specs=list(out_specs) + [_ANY] * n_c,
                         scratch_shapes=list(scratch_shapes) + comm.scratch,
                         compiler_params=_params(n_axes, False))(*operands, *comm.operands)
    return list(res[:n_out]), list(res[n_out:])


def _all_gather_small(v, name):
    rows = v.shape[0]

    def body(v_ref, all_ref, sum_ref, send_sems, recv_sems):
        x, y, c, me = _my_position()
        all_ref[me] = v_ref[...]
        copies = []
        for p in range(1, N_DEV):
            cp = pltpu.make_async_remote_copy(
                src_ref=v_ref, dst_ref=all_ref.at[me], send_sem=send_sems.at[p - 1], recv_sem=recv_sems.at[p - 1],
                device_id=_peer(x, y, c, p), device_id_type=MESH)
            cp.start()
            copies.append(cp)
        for cp in copies:
            cp.wait()
        acc = all_ref[0]
        for d in range(1, N_DEV):
            acc = acc + all_ref[d]
        sum_ref[...] = acc

    return pl.pallas_call(
        body, name=name,
        out_shape=[jax.ShapeDtypeStruct((N_DEV, rows, LANES), F32), jax.ShapeDtypeStruct((rows, LANES), F32)],
        in_specs=[pl.BlockSpec(memory_space=pltpu.VMEM)],
        out_specs=[pl.BlockSpec(memory_space=pltpu.VMEM)] * 2,
        scratch_shapes=[pltpu.SemaphoreType.DMA((N_DEV - 1,)), pltpu.SemaphoreType.DMA((N_DEV - 1,))],
        compiler_params=pltpu.CompilerParams(vmem_limit_bytes=VMEM_LIMIT),
    )(v)


def _sum_gathered(gathered, name):
    rows = gathered.shape[1]

    def body(g_ref, o_ref):
        acc = g_ref[0]
        for d in range(1, N_DEV):
            acc = acc + g_ref[d]
        o_ref[...] = acc

    return pl.pallas_call(
        body, name=name, out_shape=jax.ShapeDtypeStruct((rows, LANES), F32),
        in_specs=[pl.BlockSpec(memory_space=pltpu.VMEM)], out_specs=pl.BlockSpec(memory_space=pltpu.VMEM),
        compiler_params=pltpu.CompilerParams(vmem_limit_bytes=VMEM_LIMIT),
    )(gathered)


def _pack_small(parts):
    flat = jnp.concatenate([p.reshape(-1).astype(F32) for p in parts])
    total = flat.shape[0]
    padded = -(-total // (8 * LANES)) * (8 * LANES)
    flat = jnp.pad(flat, (0, padded - total))
    return flat.reshape(padded // LANES, LANES)


def _unpack_small(packed, shapes, lead=()):
    flat = packed.reshape(lead + (-1,))
    out, off = [], 0
    for shp in shapes:
        size = math.prod(shp)
        out.append(flat[..., off:off + size].reshape(lead + tuple(shp)))
        off += size
    return out


def _ada_forward(c_all, ada_w, ada_b_cols):
    nb = c_all.shape[0]
    cols = ada_w.shape[2]

    def body(c_ref, w_ref, b_ref, o_ref):
        cv = c_ref[...]
        act = (cv * _sigmoid(cv)).astype(BF16)
        o_ref[0] = _dot(act, w_ref[0].astype(BF16)) + b_ref[0]

    return pl.pallas_call(
        body, name="ada_forward", grid=(DEPTH,),
        out_shape=jax.ShapeDtypeStruct((DEPTH, nb, cols), F32),
        in_specs=[pl.BlockSpec((nb, D_MODEL), lambda l: (0, 0)),
                  pl.BlockSpec((1, D_MODEL, cols), lambda l: (l, 0, 0)),
                  pl.BlockSpec((1, 1, cols), lambda l: (l, 0, 0))],
        out_specs=pl.BlockSpec((1, nb, cols), lambda l: (l, 0, 0)),
        compiler_params=_params(),
    )(c_all, ada_w, ada_b_cols)


def _ada_backward(c_all, d_ada_cols, d_ada_all):
    nb = c_all.shape[0]
    cols = d_ada_cols.shape[2]
    full = d_ada_all.shape[2]

    def body(c_ref, dc_ref, da_ref, gw_ref, gb_ref):
        cv = c_ref[...]
        act = (cv * _sigmoid(cv)).astype(BF16)
        gw_ref[0] = _dot_tn(act, dc_ref[0].astype(BF16))
        gb_ref[0] = jnp.sum(da_ref[0], axis=0, keepdims=True)

    return pl.pallas_call(
        body, name="ada_backward", grid=(DEPTH,),
        out_shape=[jax.ShapeDtypeStruct((DEPTH, D_MODEL, cols), F32), jax.ShapeDtypeStruct((DEPTH, 1, full), F32)],
        in_specs=[pl.BlockSpec((nb, D_MODEL), lambda l: (0, 0)),
                  pl.BlockSpec((1, nb, cols), lambda l: (l, 0, 0)),
                  pl.BlockSpec((1, nb, full), lambda l: (l, 0, 0))],
        out_specs=[pl.BlockSpec((1, D_MODEL, cols), lambda l: (l, 0, 0)),
                   pl.BlockSpec((1, 1, full), lambda l: (l, 0, 0))],
        compiler_params=_params(),
    )(c_all, d_ada_cols, d_ada_all)


def _rms(xv):
    return lax.rsqrt(jnp.mean(xv * xv, axis=-1, keepdims=True) + EPS)


def _normmod_matmul(x, gnorm, scale1p, shift, w_t, seq, name, swiglu=False, comm=None):
    tokens, n_out = x.shape[0], w_t.shape[0]
    tm = _tile_rows(seq)
    per_seq = seq // tm
    width = n_out // 2 if swiglu else n_out
    n_chunks = width // MXU_N

    def body(x_ref, g_ref, sc_ref, sh_ref, w_ref, h_ref, *o_refs):
        xv = x_ref[...]
        h = (xv * _rms(xv) * g_ref[...]) * sc_ref[0] + sh_ref[0]
        h_ref[...] = h.astype(BF16)
        for ck in range(n_chunks):
            cs = slice(ck * MXU_N, (ck + 1) * MXU_N)
            if swiglu:
                act_ref, silu_ref, dact_ref = o_refs
                g = _dot_nt(h_ref[...], w_ref[cs, :])
                u = _dot_nt(h_ref[...], w_ref[width + ck * MXU_N:width + (ck + 1) * MXU_N, :])
                sig = _sigmoid(g)
                silu = g * sig
                act_ref[:, cs] = (silu * u).astype(BF16)
                silu_ref[:, cs] = silu.astype(BF16)
                dact_ref[:, cs] = (u * (sig + silu * (1.0 - sig))).astype(BF16)
            else:
                o_refs[0][:, cs] = _dot_nt(h_ref[...], w_ref[cs, :]).astype(BF16)

    n_res = 3 if swiglu else 1
    per_batch = pl.BlockSpec((1, 1, D_MODEL), lambda i: (i // per_seq, 0, 0))
    outs, got = _call(
        body, name=name, grid=(tokens // tm,),
        out_shape=[jax.ShapeDtypeStruct((tokens, D_MODEL), BF16)] + [jax.ShapeDtypeStruct((tokens, width), BF16)] * n_res,
        in_specs=[pl.BlockSpec((tm, D_MODEL), lambda i: (i, 0)), _resident((1, D_MODEL)), per_batch, per_batch,
                  _resident(w_t.shape)],
        out_specs=[pl.BlockSpec((tm, D_MODEL), lambda i: (i, 0))] + [pl.BlockSpec((tm, width), lambda i: (i, 0))] * n_res,
        operands=(x, gnorm, scale1p, shift, w_t), parallel=True, comm=comm)
    return (*outs, got)


def _matmul_residual(src, w, x, gate, scale, seq, name, comm=None):
    tokens, k_dim = x.shape[0], w.shape[0]
    tm = _tile_rows(seq)
    per_seq = seq // tm

    def body(s_ref, w_ref, x_ref, gate_ref, xo_ref, f_ref):
        f = _dot(s_ref[...], w_ref[...])
        f_ref[...] = f.astype(BF16)
        xo_ref[...] = x_ref[...] + (scale * gate_ref[0]) * f

    (x_out, f), got = _call(
        body, name=name, grid=(tokens // tm,),
        out_shape=[jax.ShapeDtypeStruct((tokens, D_MODEL), F32), jax.ShapeDtypeStruct((tokens, D_MODEL), BF16)],
        in_specs=[pl.BlockSpec((tm, k_dim), lambda i: (i, 0)), _resident(w.shape),
                  pl.BlockSpec((tm, D_MODEL), lambda i: (i, 0)),
                  pl.BlockSpec((1, 1, D_MODEL), lambda i: (i // per_seq, 0, 0))],
        out_specs=[pl.BlockSpec((tm, D_MODEL), lambda i: (i, 0))] * 2,
        operands=(src, w, x, gate), parallel=True, comm=comm)
    return x_out, f, got


def _ffn_forward(x, gnorm, scale1p, shift, w_gu_t, w_down, gate, scale, seq, name, comm=None):
    tokens, width = x.shape[0], w_down.shape[0]
    tm = _tile_rows(seq)
    per_seq = seq // tm
    n_chunks = width // MXU_N

    def body(x_ref, g_ref, sc_ref, sh_ref, wgu_ref, wd_ref, gate_ref, h_ref, act_ref, silu_ref, dact_ref, xo_ref, f_ref):
        xv = x_ref[...]
        h = (xv * _rms(xv) * g_ref[...]) * sc_ref[0] + sh_ref[0]
        h_ref[...] = h.astype(BF16)
        for ck in range(n_chunks):
            cs = slice(ck * MXU_N, (ck + 1) * MXU_N)
            g = _dot_nt(h_ref[...], wgu_ref[cs, :])
            u = _dot_nt(h_ref[...], wgu_ref[width + ck * MXU_N:width + (ck + 1) * MXU_N, :])
            sig = _sigmoid(g)
            silu = g * sig
            act_ref[:, cs] = (silu * u).astype(BF16)
            silu_ref[:, cs] = silu.astype(BF16)
            dact_ref[:, cs] = (u * (sig + silu * (1.0 - sig))).astype(BF16)
        f = _dot(act_ref[...], wd_ref[...])
        f_ref[...] = f.astype(BF16)
        xo_ref[...] = xv + (scale * gate_ref[0]) * f

    row = lambda i: (i, 0)
    per_batch = pl.BlockSpec((1, 1, D_MODEL), lambda i: (i // per_seq, 0, 0))
    tile = lambda cols: pl.BlockSpec((tm, cols), row)
    wide = jax.ShapeDtypeStruct((tokens, width), BF16)
    outs, got = _call(
        body, name=name, grid=(tokens // tm,),
        out_shape=[jax.ShapeDtypeStruct((tokens, D_MODEL), BF16), wide, wide, wide,
                   jax.ShapeDtypeStruct((tokens, D_MODEL), F32), jax.ShapeDtypeStruct((tokens, D_MODEL), BF16)],
        in_specs=[tile(D_MODEL), _resident((1, D_MODEL)), per_batch, per_batch, _resident(w_gu_t.shape),
                  _resident(w_down.shape), per_batch],
        out_specs=[tile(D_MODEL), tile(width), tile(width), tile(width), tile(D_MODEL), tile(D_MODEL)],
        operands=(x, gnorm, scale1p, shift, w_gu_t, w_down, gate), parallel=True, comm=comm)
    return (*outs, got)


def _residual_backward(dy, gate, f, w, scale, seq, name, silu=None, dact=None, comm=None):
    tokens, k_dim = dy.shape[0], w.shape[0]
    batch = tokens // seq
    tm = _tile_rows(seq)
    per_seq = seq // tm
    n_chunks = k_dim // MXU_N
    swiglu = silu is not None

    def body(*refs):
        if swiglu:
            dy_ref, gate_ref, f_ref, silu_ref, dact_ref, w_ref, df_ref, dgate_ref, dgu_ref = refs
        else:
            dy_ref, gate_ref, f_ref, w_ref, df_ref, dgate_ref, dsrc_ref = refs
        i = pl.program_id(0)
        dy_v = dy_ref[...]
        df_ref[...] = ((scale * gate_ref[0]) * dy_v).astype(BF16)
        part = scale * jnp.sum(dy_v * f_ref[...].astype(F32), axis=0, keepdims=True)

        @pl.when(i % per_seq == 0)
        def _():
            dgate_ref[0] = part

        @pl.when(i % per_seq != 0)
        def _():
            dgate_ref[0] = dgate_ref[0] + part

        if swiglu:
            for ck in range(n_chunks):
                cs = slice(ck * MXU_N, (ck + 1) * MXU_N)
                cu = slice(k_dim + ck * MXU_N, k_dim + (ck + 1) * MXU_N)
                da = _dot_nt(df_ref[...], w_ref[cs, :])
                dgu_ref[:, cs] = (da * dact_ref[:, cs].astype(F32)).astype(BF16)
                dgu_ref[:, cu] = (da * silu_ref[:, cs].astype(F32)).astype(BF16)
        else:
            dsrc_ref[...] = _dot_nt(df_ref[...], w_ref[...])

    row = lambda i: (i, 0)
    per_batch = pl.BlockSpec((1, 1, D_MODEL), lambda i: (i // per_seq, 0, 0))
    in_specs = [pl.BlockSpec((tm, D_MODEL), row), per_batch, pl.BlockSpec((tm, D_MODEL), row)]
    out_shape = [jax.ShapeDtypeStruct((tokens, D_MODEL), BF16), jax.ShapeDtypeStruct((batch, 1, D_MODEL), F32)]
    out_specs = [pl.BlockSpec((tm, D_MODEL), row), per_batch]
    if swiglu:
        in_specs += [pl.BlockSpec((tm, k_dim), row)] * 2
        operands = (dy, gate, f, silu, dact, w)
        out_shape += [jax.ShapeDtypeStruct((tokens, 2 * k_dim), BF16)]
        out_specs += [pl.BlockSpec((tm, 2 * k_dim), row)]
    else:
        operands = (dy, gate, f, w)
        out_shape += [jax.ShapeDtypeStruct((tokens, k_dim), F32)]
        out_specs += [pl.BlockSpec((tm, k_dim), row)]
    in_specs += [_resident(w.shape)]
    outs, got = _call(body, name=name, grid=(tokens // tm,), out_shape=out_shape, in_specs=in_specs,
                      out_specs=out_specs, operands=operands, comm=comm)
    return (*outs, got)


def _matmul_normmod_backward(dsrc, w_t, x, dy, gnorm, scale1p, seq, name, comm=None):
    tokens, k_dim = dsrc.shape
    batch = tokens // seq
    tm = _tile_rows(seq)
    per_seq = seq // tm

    def body(ds_ref, w_ref, x_ref, dy_ref, g_ref, sc_ref, dx_ref, dsh_ref, dsc_ref, dg_ref):
        i = pl.program_id(0)
        dh = _dot(ds_ref[...], w_ref[...])
        xv = x_ref[...]
        r = _rms(xv)
        xn = xv * r
        gn = g_ref[...]
        dsh = jnp.sum(dh, axis=0, keepdims=True)
        dsc = jnp.sum(dh * (xn * gn), axis=0, keepdims=True)
        dhn = dh * sc_ref[0]
        dg = jnp.sum(dhn * xn, axis=0, keepdims=True)
        dxn = dhn * gn
        dx_ref[...] = dy_ref[...] + r * (dxn - xn * jnp.mean(dxn * xn, axis=-1, keepdims=True))

        @pl.when(i % per_seq == 0)
        def _():
            dsh_ref[0] = dsh
            dsc_ref[0] = dsc

        @pl.when(i % per_seq != 0)
        def _():
            dsh_ref[0] = dsh_ref[0] + dsh
            dsc_ref[0] = dsc_ref[0] + dsc

        @pl.when(i == 0)
        def _():
            dg_ref[...] = dg

        @pl.when(i != 0)
        def _():
            dg_ref[...] = dg_ref[...] + dg

    row = lambda i: (i, 0)
    per_batch = pl.BlockSpec((1, 1, D_MODEL), lambda i: (i // per_seq, 0, 0))
    outs, got = _call(
        body, name=name, grid=(tokens // tm,),
        out_shape=[jax.ShapeDtypeStruct((tokens, D_MODEL), F32), jax.ShapeDtypeStruct((batch, 1, D_MODEL), F32),
                   jax.ShapeDtypeStruct((batch, 1, D_MODEL), F32), jax.ShapeDtypeStruct((1, D_MODEL), F32)],
        in_specs=[pl.BlockSpec((tm, k_dim), row), _resident(w_t.shape), pl.BlockSpec((tm, D_MODEL), row),
                  pl.BlockSpec((tm, D_MODEL), row), _resident((1, D_MODEL)), per_batch],
        out_specs=[pl.BlockSpec((tm, D_MODEL), row), per_batch, per_batch, pl.BlockSpec((1, D_MODEL), lambda i: (0, 0))],
        operands=(dsrc, w_t, x, dy, gnorm, scale1p), comm=comm)
    return (*outs, got)


def _ffn_backward(dy, gate, f, silu, dact, w_down, w_gu_t, x, gnorm, scale1p, scale, seq, name, comm=None):
    tokens, k_dim = dy.shape[0], w_down.shape[0]
    batch = tokens // seq
    tm = min(256, seq)
    per_seq = seq // tm
    n_chunks = k_dim // MXU_N

    def body(dy_ref, gate_ref, f_ref, silu_ref, dact_ref, wd_ref, wgu_ref, x_ref, g_ref, sc_ref,
             df_ref, dgate_ref, dgu_ref, dx_ref, dsh_ref, dsc_ref, dg_ref):
        i = pl.program_id(0)
        dy_v = dy_ref[...]
        df_ref[...] = ((scale * gate_ref[0]) * dy_v).astype(BF16)
        dgate = scale * jnp.sum(dy_v * f_ref[...].astype(F32), axis=0, keepdims=True)
        for ck in range(n_chunks):
            cs = slice(ck * MXU_N, (ck + 1) * MXU_N)
            cu = slice(k_dim + ck * MXU_N, k_dim + (ck + 1) * MXU_N)
            da = _dot_nt(df_ref[...], wd_ref[cs, :])
            dgu_ref[:, cs] = (da * dact_ref[:, cs].astype(F32)).astype(BF16)
            dgu_ref[:, cu] = (da * silu_ref[:, cs].astype(F32)).astype(BF16)
        dh = _dot(dgu_ref[...], wgu_ref[...])
        xv = x_ref[...]
        r = _rms(xv)
        xn = xv * r
        gn = g_ref[...]
        dsh = jnp.sum(dh, axis=0, keepdims=True)
        dsc = jnp.sum(dh * (xn * gn), axis=0, keepdims=True)
        dhn = dh * sc_ref[0]
        dg = jnp.sum(dhn * xn, axis=0, keepdims=True)
        dxn = dhn * gn
        dx_ref[...] = dy_v + r * (dxn - xn * jnp.mean(dxn * xn, axis=-1, keepdims=True))

        @pl.when(i % per_seq == 0)
        def _():
            dgate_ref[0] = dgate
            dsh_ref[0] = dsh
            dsc_ref[0] = dsc

        @pl.when(i % per_seq != 0)
        def _():
            dgate_ref[0] = dgate_ref[0] + dgate
            dsh_ref[0] = dsh_ref[0] + dsh
            dsc_ref[0] = dsc_ref[0] + dsc

        @pl.when(i == 0)
        def _():
            dg_ref[...] = dg

        @pl.when(i != 0)
        def _():
            dg_ref[...] = dg_ref[...] + dg

    row = lambda i: (i, 0)
    per_batch = pl.BlockSpec((1, 1, D_MODEL), lambda i: (i // per_seq, 0, 0))
    tile = lambda width: pl.BlockSpec((tm, width), row)
    vec = jax.ShapeDtypeStruct((batch, 1, D_MODEL), F32)
    outs, got = _call(
        body, name=name, grid=(tokens // tm,),
        out_shape=[jax.ShapeDtypeStruct((tokens, D_MODEL), BF16), vec, jax.ShapeDtypeStruct((tokens, 2 * k_dim), BF16),
                   jax.ShapeDtypeStruct((tokens, D_MODEL), F32), vec, vec, jax.ShapeDtypeStruct((1, D_MODEL), F32)],
        in_specs=[tile(D_MODEL), per_batch, tile(D_MODEL), tile(k_dim), tile(k_dim), _resident(w_down.shape),
                  _resident(w_gu_t.shape), tile(D_MODEL), _resident((1, D_MODEL)), per_batch],
        out_specs=[tile(D_MODEL), per_batch, tile(2 * k_dim), tile(D_MODEL), per_batch, per_batch,
                   pl.BlockSpec((1, D_MODEL), lambda i: (0, 0))],
        operands=(dy, gate, f, silu, dact, w_down, w_gu_t, x, gnorm, scale1p), comm=comm)
    return (*outs, got)


def _weight_grad(a, b, seq, name, comm=None):
    tokens, n_out = a.shape
    tn = MXU_N

    def body(a_ref, b_ref, o_ref):
        o_ref[...] = _dot_tn(a_ref[...], b_ref[...]).astype(BF16)

    (out,), got = _call(
        body, name=name, grid=(n_out // tn,),
        out_shape=[jax.ShapeDtypeStruct((n_out, D_MODEL), BF16)],
        in_specs=[pl.BlockSpec((tokens, tn), lambda j: (0, j)), _resident((tokens, D_MODEL))],
        out_specs=[pl.BlockSpec((tn, D_MODEL), lambda j: (j, 0))],
        operands=(a, b), parallel=True, comm=comm)
    return out, got


def _final_loss(x, target, gnorm, seq):
    tokens = x.shape[0]
    tm = _tile_rows(seq)

    def body(x_ref, t_ref, g_ref, dx_ref, dg_ref, loss_ref):
        i = pl.program_id(0)
        xv = x_ref[...]
        r = _rms(xv)
        xn = xv * r
        gn = g_ref[...]
        err = xn * gn - t_ref[...]
        loss = (0.5 / D_MODEL) * jnp.sum(err * err, axis=0, keepdims=True)
        dyv = err * (1.0 / D_MODEL)
        dg = jnp.sum(dyv * xn, axis=0, keepdims=True)
        dxn = dyv * gn
        dx_ref[...] = r * (dxn - xn * jnp.mean(dxn * xn, axis=-1, keepdims=True))

        @pl.when(i == 0)
        def _():
            dg_ref[...] = dg
            loss_ref[...] = loss

        @pl.when(i != 0)
        def _():
            dg_ref[...] = dg_ref[...] + dg
            loss_ref[...] = loss_ref[...] + loss

    row = lambda i: (i, 0)
    fixed = pl.BlockSpec((1, D_MODEL), lambda i: (0, 0))
    outs, _ = _call(
        body, name="final_loss", grid=(tokens // tm,),
        out_shape=[jax.ShapeDtypeStruct((tokens, D_MODEL), F32), jax.ShapeDtypeStruct((1, D_MODEL), F32),
                   jax.ShapeDtypeStruct((1, D_MODEL), F32)],
        in_specs=[pl.BlockSpec((tm, D_MODEL), row), pl.BlockSpec((tm, D_MODEL), row), _resident((1, D_MODEL))],
        out_specs=[pl.BlockSpec((tm, D_MODEL), row), fixed, fixed],
        operands=(x, target, gnorm))
    return outs


def _group_mean(v, bd):
    hi = v.astype(BF16)
    lo = (v - hi.astype(F32)).astype(BF16)
    return _dot(hi, bd) + _dot(lo, bd)


def _sgu_forward(pm_ref, wm_ref, bias_ref, lng_ref, lnb_ref, bd_ref, mixed_scr, n_sub):
    ua = pm_ref[:, 0:D_A].astype(F32)
    va = pm_ref[:, D_A:2 * D_A].astype(F32)
    u_act = _gelu(ua)
    v_act = _gelu(va)
    bd = bd_ref[...]
    vc = v_act - _group_mean(v_act, bd)
    rstd = lax.rsqrt(_group_mean(vc * vc, bd) + EPS)
    vhat = vc * rstd
    vln = vhat * lng_ref[...] + lnb_ref[...]
    left = lax.broadcasted_iota(jnp.int32, (CHUNK, LANES), 1) < HEAD_DIM
    for q in range(n_sub):
        rows = slice(q * CHUNK, (q + 1) * CHUNK)
        for p in range(N_HEADS // 2):
            cols = slice(p * LANES, (p + 1) * LANES)
            vp = vln[rows, cols]
            v_l = jnp.where(left, vp, 0.0).astype(BF16)
            v_r = jnp.where(left, 0.0, vp).astype(BF16)
            mixed_scr[rows, cols] = _dot(wm_ref[2 * p], v_l) + _dot(wm_ref[2 * p + 1], v_r) + bias_ref[:, cols]
    return ua, va, u_act, vhat, rstd, vln


def _halo_specs(tm, tokens, width):
    prev = pl.BlockSpec((HALO, width), lambda i: (jnp.maximum(i * (tm // HALO) - 1, 0), 0))
    nxt = pl.BlockSpec((HALO, width), lambda i: (jnp.minimum((i + 1) * (tm // HALO), tokens // HALO - 1), 0))
    return prev, nxt


def _mixer_forward(proj, wm, bias_full, lng, lnb, convw, og, bd, seq, name):
    tokens = proj.shape[0]
    tm = _tile_rows(seq)
    per_seq = seq // tm
    n_sub = tm // CHUNK

    def body(pm_ref, pp_ref, wm_ref, bias_ref, lng_ref, lnb_ref, cw_ref, og_ref, bd_ref, y_ref, mixed_scr):
        i = pl.program_id(0)
        first = (i % per_seq) == 0
        _, _, u_act, _, _, _ = _sgu_forward(pm_ref, wm_ref, bias_ref, lng_ref, lnb_ref, bd_ref, mixed_scr, n_sub)
        ya = u_act * mixed_scr[...]
        y_ref[:, 0:D_A] = (ya * _rms(ya) * og_ref[:, 0:D_A]).astype(BF16)

        bg = pm_ref[:, 2 * D_A:3 * D_A].astype(F32)
        z = pm_ref[:, 3 * D_A:4 * D_A].astype(F32) * pm_ref[:, 4 * D_A:5 * D_A].astype(F32)
        zp = pp_ref[:, 3 * D_A:4 * D_A].astype(F32) * pp_ref[:, 4 * D_A:5 * D_A].astype(F32)
        zp = jnp.where(first, 0.0, zp)
        zext = jnp.concatenate([zp, z], axis=0)
        z1 = pltpu.roll(zext, 1, 0)[HALO:]
        z2 = pltpu.roll(zext, 2, 0)[HALO:]
        conv = cw_ref[0:1, :] * z2 + cw_ref[1:2, :] * z1 + cw_ref[2:3, :] * z
        yb = bg * conv
        y_ref[:, D_A:2 * D_A] = (yb * _rms(yb) * og_ref[:, D_A:2 * D_A]).astype(BF16)

    prev, _ = _halo_specs(tm, tokens, D_PROJ)
    (y,), _ = _call(
        body, name=name, grid=(tokens // tm,),
        out_shape=[jax.ShapeDtypeStruct((tokens, D_MODEL), BF16)],
        in_specs=[pl.BlockSpec((tm, D_PROJ), lambda i: (i, 0)), prev, _resident(wm.shape), _resident(bias_full.shape),
                  _resident(lng.shape), _resident(lnb.shape), _resident(convw.shape), _resident(og.shape),
                  _resident(bd.shape)],
        out_specs=[pl.BlockSpec((tm, D_MODEL), lambda i: (i, 0))],
        scratch_shapes=[pltpu.VMEM((tm, D_A), F32)],
        operands=(proj, proj, wm, bias_full, lng, lnb, convw, og, bd), parallel=True)
    return y


def _mixer_backward(proj, dy, wm, bias_full, lng, lnb, convw, og, bd, causal, seq, name, comm=None):
    tokens = proj.shape[0]
    tm = _tile_rows(seq)
    per_seq = seq // tm
    n_sub = tm // CHUNK
    ext = tm + 2 * HALO

    def body(pm_ref, pp_ref, pn_ref, dy_ref, dyn_ref, wm_ref, bias_ref, lng_ref, lnb_ref, cw_ref, og_ref, bd_ref,
             causal_ref, dp_ref, dog_ref, dcw_ref, dlng_ref, dlnb_ref, dbias_ref, dwm_ref, mixed_scr, dvln_scr):
        i = pl.program_id(0)
        first = (i % per_seq) == 0
        last = (i % per_seq) == per_seq - 1

        @pl.when(i == 0)
        def _():
            dog_ref[...] = jnp.zeros_like(dog_ref)
            dcw_ref[...] = jnp.zeros_like(dcw_ref)
            dlng_ref[...] = jnp.zeros_like(dlng_ref)
            dlnb_ref[...] = jnp.zeros_like(dlnb_ref)
            dbias_ref[...] = jnp.zeros_like(dbias_ref)
            dwm_ref[...] = jnp.zeros_like(dwm_ref)

        ua, va, u_act, vhat, rstd, vln = _sgu_forward(pm_ref, wm_ref, bias_ref, lng_ref, lnb_ref, bd_ref, mixed_scr, n_sub)
        mixed = mixed_scr[...]
        ya = u_act * mixed
        ra = _rms(ya)
        yhat = ya * ra
        dya_in = dy_ref[:, 0:D_A]
        dog_ref[:, 0:D_A] = dog_ref[:, 0:D_A] + jnp.sum(dya_in * yhat, axis=0, keepdims=True)
        dyh = dya_in * og_ref[:, 0:D_A]
        dya = ra * (dyh - yhat * jnp.mean(dyh * yhat, axis=-1, keepdims=True))
        d_u = dya * mixed
        d_mixed = dya * u_act
        left = lax.broadcasted_iota(jnp.int32, (CHUNK, LANES), 1) < HEAD_DIM
        dbias = jnp.zeros((CHUNK, D_A), F32)
        for q in range(n_sub):
            rows = slice(q * CHUNK, (q + 1) * CHUNK)
            dbias = dbias + d_mixed[rows, :]
            for p in range(N_HEADS // 2):
                cols = slice(p * LANES, (p + 1) * LANES)
                dm = d_mixed[rows, cols]
                dm_l = jnp.where(left, dm, 0.0).astype(BF16)
                dm_r = jnp.where(left, 0.0, dm).astype(BF16)
                vp = vln[rows, cols].astype(BF16)
                dwm_ref[2 * p] = dwm_ref[2 * p] + causal_ref[...] * _dot_nt(dm_l, vp)
                dwm_ref[2 * p + 1] = dwm_ref[2 * p + 1] + causal_ref[...] * _dot_nt(dm_r, vp)
                dvln_scr[rows, cols] = _dot_tn(wm_ref[2 * p], dm_l) + _dot_tn(wm_ref[2 * p + 1], dm_r)
        dbias_ref[...] = dbias_ref[...] + dbias
        dvln = dvln_scr[...]
        dlng_ref[...] = dlng_ref[...] + jnp.sum(dvln * vhat, axis=0, keepdims=True)
        dlnb_ref[...] = dlnb_ref[...] + jnp.sum(dvln, axis=0, keepdims=True)
        dvh = dvln * lng_ref[...]
        bd = bd_ref[...]
        d_v = rstd * (dvh - _group_mean(dvh, bd) - vhat * _group_mean(dvh * vhat, bd))
        dp_ref[:, 0:D_A] = (d_u * _gelu_grad(ua)).astype(BF16)
        dp_ref[:, D_A:2 * D_A] = (d_v * _gelu_grad(va)).astype(BF16)

        def ext_cols(lo):
            cs = slice(lo, lo + D_A)
            return jnp.concatenate([pp_ref[:, cs], pm_ref[:, cs], pn_ref[:, cs]], axis=0).astype(F32)

        bg, cg, xb = ext_cols(2 * D_A), ext_cols(3 * D_A), ext_cols(4 * D_A)
        row = lax.broadcasted_iota(jnp.int32, (ext, D_A), 0)
        z = jnp.where(jnp.logical_and(first, row < HALO), 0.0, cg * xb)
        z1 = pltpu.roll(z, 1, 0)
        z2 = pltpu.roll(z, 2, 0)
        w0, w1, w2 = cw_ref[0:1, :], cw_ref[1:2, :], cw_ref[2:3, :]
        conv = w0 * z2 + w1 * z1 + w2 * z
        yb = bg * conv
        rb = _rms(yb)
        yhb = yb * rb
        dyn = jnp.where(last, 0.0, dyn_ref[:, D_A:2 * D_A])
        dyb_in = jnp.concatenate([jnp.zeros((HALO, D_A), F32), dy_ref[:, D_A:2 * D_A], dyn], axis=0)
        dyhb = dyb_in * og_ref[:, D_A:2 * D_A]
        dyb = rb * (dyhb - yhb * jnp.mean(dyhb * yhb, axis=-1, keepdims=True))
        d_conv = dyb * bg
        dz = w2 * d_conv + w1 * pltpu.roll(d_conv, ext - 1, 0) + w0 * pltpu.roll(d_conv, ext - 2, 0)
        main = slice(HALO, HALO + tm)
        dp_ref[:, 2 * D_A:3 * D_A] = (dyb * conv)[main].astype(BF16)
        dp_ref[:, 3 * D_A:4 * D_A] = (dz * xb)[main].astype(BF16)
        dp_ref[:, 4 * D_A:5 * D_A] = (dz * cg)[main].astype(BF16)
        dog_ref[:, D_A:2 * D_A] = dog_ref[:, D_A:2 * D_A] + jnp.sum((dyb_in * yhb)[main], axis=0, keepdims=True)
        dcm = d_conv[main]
        dcw_ref[0:1, :] = dcw_ref[0:1, :] + jnp.sum(dcm * z2[main], axis=0, keepdims=True)
        dcw_ref[1:2, :] = dcw_ref[1:2, :] + jnp.sum(dcm * z1[main], axis=0, keepdims=True)
        dcw_ref[2:3, :] = dcw_ref[2:3, :] + jnp.sum(dcm * z[main], axis=0, keepdims=True)

    prev_p, next_p = _halo_specs(tm, tokens, D_PROJ)
    _, next_d = _halo_specs(tm, tokens, D_MODEL)
    fixed2 = lambda shape: pl.BlockSpec(shape, lambda i: (0, 0))
    outs, got = _call(
        body, name=name, grid=(tokens // tm,),
        out_shape=[jax.ShapeDtypeStruct((tokens, D_PROJ), BF16), jax.ShapeDtypeStruct((1, D_MODEL), F32),
                   jax.ShapeDtypeStruct((8, D_A), F32), jax.ShapeDtypeStruct((1, D_A), F32),
                   jax.ShapeDtypeStruct((1, D_A), F32), jax.ShapeDtypeStruct((CHUNK, D_A), F32),
                   jax.ShapeDtypeStruct((N_HEADS, CHUNK, CHUNK), F32)],
        in_specs=[pl.BlockSpec((tm, D_PROJ), lambda i: (i, 0)), prev_p, next_p,
                  pl.BlockSpec((tm, D_MODEL), lambda i: (i, 0)), next_d,
                  _resident(wm.shape), _resident(bias_full.shape), _resident(lng.shape), _resident(lnb.shape),
                  _resident(convw.shape), _resident(og.shape), _resident(bd.shape), _resident(causal.shape)],
        out_specs=[pl.BlockSpec((tm, D_PROJ), lambda i: (i, 0)), fixed2((1, D_MODEL)), fixed2((8, D_A)),
                   fixed2((1, D_A)), fixed2((1, D_A)), fixed2((CHUNK, D_A)),
                   pl.BlockSpec((N_HEADS, CHUNK, CHUNK), lambda i: (0, 0, 0))],
        scratch_shapes=[pltpu.VMEM((tm, D_A), F32), pltpu.VMEM((tm, D_A), F32)],
        operands=(proj, proj, proj, dy, dy, wm, bias_full, lng, lnb, convw, og, bd, causal), comm=comm)
    return (*outs, got)


def _sum_slots(recv, name):
    _, rows, cols = recv.shape
    tr = rows // 2

    def body(r_ref, o_ref):
        acc = r_ref[0].astype(F32)
        for d in range(1, N_DEV):
            acc = acc + r_ref[d].astype(F32)
        o_ref[...] = acc

    return pl.pallas_call(
        body, name=name, grid=(2,),
        out_shape=jax.ShapeDtypeStruct((rows, cols), F32),
        in_specs=[pl.BlockSpec((N_DEV, tr, cols), lambda i: (0, i, 0))],
        out_specs=pl.BlockSpec((tr, cols), lambda i: (i, 0)),
        compiler_params=_params(parallel=True),
    )(recv)


def _adamw(w, g, m, v, name):
    rows, cols = w.shape
    tr = max(t for t in range(8, 513, 8) if rows % t == 0)

    def body(w_ref, g_ref, m_ref, v_ref, d_ref, nm_ref, nv_ref):
        gv = g_ref[...]
        nm = ADAM_B1 * m_ref[...] + (1.0 - ADAM_B1) * gv
        nv = ADAM_B2 * v_ref[...] + (1.0 - ADAM_B2) * (gv * gv)
        m_hat = nm / (1.0 - ADAM_B1 ** ADAM_STEP)
        v_hat = nv / (1.0 - ADAM_B2 ** ADAM_STEP)
        d_ref[...] = -ADAM_LR * (m_hat / (jnp.sqrt(v_hat) + ADAM_EPS) + ADAM_WD * w_ref[...])
        nm_ref[...] = nm
        nv_ref[...] = nv

    spec = pl.BlockSpec((tr, cols), lambda i: (i, 0))
    return pl.pallas_call(
        body, name=name, grid=(rows // tr,),
        out_shape=[jax.ShapeDtypeStruct((rows, cols), F32)] * 3,
        in_specs=[spec] * 4, out_specs=[spec] * 3,
        compiler_params=_params(parallel=True),
    )(w, g, m, v)


def _adamw_nd(w, g, m, v, name):
    shape = w.shape
    two_d = (-1, shape[-1])
    d, nm, nv = _adamw(w.reshape(two_d), g.reshape(two_d), m.reshape(two_d), v.reshape(two_d), name)
    return d.reshape(shape), nm.reshape(shape), nv.reshape(shape)


def kernel(x, c, ada_w, ada_b, norm_ffn1_g, ffn1_w_gu, ffn1_w_down, norm_mix_g, mix_w_in, sgu_ln_g, sgu_ln_b, sgu_w_s, sgu_b, conv_w, out_norm_g, mix_w_out, norm_ffn2_g, ffn2_w_gu, ffn2_w_down, final_norm_g, loss_target, m_ada_w, m_ada_b, m_norm_ffn1_g, m_ffn1_w_gu, m_ffn1_w_down, m_norm_mix_g, m_mix_w_in, m_sgu_ln_g, m_sgu_ln_b, m_sgu_w_s, m_sgu_b, m_conv_w, m_out_norm_g, m_mix_w_out, m_norm_ffn2_g, m_ffn2_w_gu, m_ffn2_w_down, m_final_norm_g, v_ada_w, v_ada_b, v_norm_ffn1_g, v_ffn1_w_gu, v_ffn1_w_down, v_norm_mix_g, v_mix_w_in, v_sgu_ln_g, v_sgu_ln_b, v_sgu_w_s, v_sgu_b, v_conv_w, v_out_norm_g, v_mix_w_out, v_norm_ffn2_g, v_ffn2_w_gu, v_ffn2_w_down, v_final_norm_g):
    batch, seq, _ = x.shape
    tokens = batch * seq
    me = 4 * lax.axis_index("x") + 2 * lax.axis_index("y") + lax.axis_index("c")
    weights = dict(ada_w=ada_w, ada_b=ada_b, norm_ffn1_g=norm_ffn1_g, ffn1_w_gu=ffn1_w_gu, ffn1_w_down=ffn1_w_down,
                   norm_mix_g=norm_mix_g, mix_w_in=mix_w_in, sgu_ln_g=sgu_ln_g, sgu_ln_b=sgu_ln_b, sgu_w_s=sgu_w_s,
                   sgu_b=sgu_b, conv_w=conv_w, out_norm_g=out_norm_g, mix_w_out=mix_w_out, norm_ffn2_g=norm_ffn2_g,
                   ffn2_w_gu=ffn2_w_gu, ffn2_w_down=ffn2_w_down, final_norm_g=final_norm_g)
    mom1 = dict(ada_w=m_ada_w, ada_b=m_ada_b, norm_ffn1_g=m_norm_ffn1_g, ffn1_w_gu=m_ffn1_w_gu,
                ffn1_w_down=m_ffn1_w_down, norm_mix_g=m_norm_mix_g, mix_w_in=m_mix_w_in, sgu_ln_g=m_sgu_ln_g,
                sgu_ln_b=m_sgu_ln_b, sgu_w_s=m_sgu_w_s, sgu_b=m_sgu_b, conv_w=m_conv_w, out_norm_g=m_out_norm_g,
                mix_w_out=m_mix_w_out, norm_ffn2_g=m_norm_ffn2_g, ffn2_w_gu=m_ffn2_w_gu, ffn2_w_down=m_ffn2_w_down,
                final_norm_g=m_final_norm_g)
    mom2 = dict(ada_w=v_ada_w, ada_b=v_ada_b, norm_ffn1_g=v_norm_ffn1_g, ffn1_w_gu=v_ffn1_w_gu,
                ffn1_w_down=v_ffn1_w_down, norm_mix_g=v_norm_mix_g, mix_w_in=v_mix_w_in, sgu_ln_g=v_sgu_ln_g,
                sgu_ln_b=v_sgu_ln_b, sgu_w_s=v_sgu_w_s, sgu_b=v_sgu_b, conv_w=v_conv_w, out_norm_g=v_out_norm_g,
                mix_w_out=v_mix_w_out, norm_ffn2_g=v_norm_ffn2_g, ffn2_w_gu=v_ffn2_w_gu, ffn2_w_down=v_ffn2_w_down,
                final_norm_g=v_final_norm_g)

    big = ("ffn1_w_gu", "ffn1_w_down", "mix_w_in", "mix_w_out", "ffn2_w_gu", "ffn2_w_down")
    transposed = ("ffn1_w_gu", "mix_w_in", "ffn2_w_gu")
    shard = {(l, nm): (weights[nm][l].T if nm in transposed else weights[nm][l]).astype(BF16)
             for l in range(DEPTH) for nm in big}
    full_w = {}

    def gather_of(keys):
        return keys, _GatherRows([shard[k] for k in keys])

    def landed(plan, got):
        full_w.update(zip(plan[0], got))

    plan = gather_of([(0, "ffn1_w_gu")])
    landed(plan, _comm_only(plan[1], "gather_first"))

    small_in = _pack_small([c, conv_w])
    small_all, _ = _all_gather_small(small_in, "gather_c")
    c_all, convw_all = _unpack_small(small_all, [c.shape, conv_w.shape], lead=(N_DEV,))
    c_all = c_all.reshape(N_DEV * batch, D_MODEL)
    convw_full = jnp.transpose(convw_all, (1, 2, 0, 3)).reshape(DEPTH, 3, D_A)
    ada_cols = ada_w.shape[2]
    ada_b_cols = lax.dynamic_slice_in_dim(ada_b, me * ada_cols, ada_cols, axis=1).reshape(DEPTH, 1, ada_cols)
    ada_local = _ada_forward(c_all, ada_w, ada_b_cols)
    ada_all, _ = _all_gather_small(ada_local.reshape(-1, LANES), "gather_ada")
    ada_all = ada_all.reshape(N_DEV, DEPTH, N_DEV * batch, ada_cols)
    ada_full = jnp.transpose(ada_all, (1, 2, 0, 3)).reshape(DEPTH, N_DEV * batch, N_MOD * D_MODEL)
    ada_mine = lax.dynamic_slice_in_dim(ada_full, me * batch, batch, axis=1)
    mod = ada_mine.reshape(DEPTH, batch, N_MOD, 1, D_MODEL)

    causal = jnp.tril(jnp.ones((CHUNK, CHUNK), F32))
    bd = jnp.kron(jnp.eye(N_HEADS, dtype=F32), jnp.full((HEAD_DIM, HEAD_DIM), 1.0 / HEAD_DIM, F32)).astype(BF16)
    row_vec = lambda a: a.reshape(1, -1)

    hosted_gathers = {
        (0, "ffn1"): [(0, "ffn1_w_down"), (0, "mix_w_in"), (0, "mix_w_out")],
        (0, "ffn_down1"): [(0, "ffn2_w_gu")],
        (0, "mix_in"): [(0, "ffn2_w_down")],
        (0, "ffn2"): [(1, "ffn1_w_gu"), (1, "ffn1_w_down"), (1, "mix_w_in"), (1, "mix_w_out")],
        (1, "ffn1"): [(1, "ffn2_w_gu"), (1, "ffn2_w_down")],
    }

    def hosting(l, site):
        keys = hosted_gathers.get((l, site))
        return gather_of(keys) if keys else (None, None)

    xs = x.reshape(tokens, D_MODEL)
    saved = []
    for l in range(DEPTH):
        sh1, sc1, g1, sh2, sc2, g2, sh3, sc3, g3 = [mod[l, :, k] for k in range(N_MOD)]
        mixer_consts = dict(
            wm=(sgu_w_s[l] * causal[None]).astype(BF16),
            bias_full=jnp.repeat(sgu_b[l].T, HEAD_DIM, axis=1),
            lng=row_vec(jnp.tile(sgu_ln_g[l], N_HEADS)), lnb=row_vec(jnp.tile(sgu_ln_b[l], N_HEADS)),
            convw=jnp.pad(convw_full[l], ((0, 5), (0, 0))), og=row_vec(out_norm_g[l]), bd=bd)
        x0 = xs
        plan = hosting(l, "ffn1")
        if l == 0:
            h1, a1, s1, w1, got = _normmod_matmul(x0, row_vec(norm_ffn1_g[l]), 1.0 + sc1, sh1, full_w[l, "ffn1_w_gu"], seq, "ffn_up", True, plan[1])
            landed(plan, got)
            plan = hosting(l, "ffn_down1")
            x1, f1, got = _matmul_residual(a1, full_w[l, "ffn1_w_down"], x0, g1, 0.5, seq, "ffn_down", plan[1])
        else:
            h1, a1, s1, w1, x1, f1, got = _ffn_forward(
                x0, row_vec(norm_ffn1_g[l]), 1.0 + sc1, sh1, full_w[l, "ffn1_w_gu"], full_w[l, "ffn1_w_down"], g1, 0.5, seq, "ffn_fwd", plan[1])
        if got:
            landed(plan, got)
        plan = hosting(l, "mix_in")
        h2, proj, got = _normmod_matmul(x1, row_vec(norm_mix_g[l]), 1.0 + sc2, sh2, full_w[l, "mix_w_in"], seq, "mix_in", False, plan[1])
        if got:
            landed(plan, got)
        ymix = _mixer_forward(proj, seq=seq, name="mixer_forward", **mixer_consts)
        x2, o2, _ = _matmul_residual(ymix, full_w[l, "mix_w_out"], x1, g2, 1.0, seq, "mix_out")
        plan = hosting(l, "ffn2")
        h3, a3, s3, w3, x3, f3, got = _ffn_forward(
            x2, row_vec(norm_ffn2_g[l]), 1.0 + sc3, sh3, full_w[l, "ffn2_w_gu"], full_w[l, "ffn2_w_down"], g3, 0.5, seq, "ffn_fwd", plan[1])
        if got:
            landed(plan, got)
        saved.append(dict(x0=x0, x1=x1, x2=x2, h1=h1, h2=h2, h3=h3, a1=a1, s1=s1, w1=w1, a3=a3, s3=s3, w3=w3, f1=f1, f3=f3, o2=o2, proj=proj,
                          ymix=ymix, mixer_consts=mixer_consts, sc=(1.0 + sc1, 1.0 + sc2, 1.0 + sc3), gates=(g1, g2, g3)))
        xs = x3

    dx, d_final_g, loss_cols = _final_loss(xs, loss_target.reshape(tokens, D_MODEL), row_vec(final_norm_g), seq)

    recv = {}
    small_grads = [None] * DEPTH
    d_mod = [None] * DEPTH

    small_names = ("norm_ffn1_g", "norm_mix_g", "norm_ffn2_g", "out_norm_g", "sgu_ln_g", "sgu_ln_b", "sgu_w_s", "sgu_b", "conv_w")

    def small_parts(l):
        return [small_grads[l][nm] for nm in small_names] + [d_mod[l]]

    pending = []

    def scatter_later(l, nm, grad):
        pending.append(((l, nm), _ScatterRows([grad])))

    def host():
        keys, parts = [k for k, _ in pending], [p for _, p in pending]
        pending.clear()
        return keys, (_Exchanges(parts) if parts else None)

    def hosted(keys, got):
        if got:
            recv.update(zip(keys, got))

    for l in reversed(range(DEPTH)):
        sv = saved[l]
        mc = sv["mixer_consts"]
        if l + 1 < DEPTH:
            pending.append((("small", l + 1), _GatherRows([_pack_small(small_parts(l + 1))])))
        keys, comm = host()
        df3, dg3, dgu3, dx2, dsh3, dsc3, dn3, got = _ffn_backward(
            dx, sv["gates"][2], sv["f3"], sv["s3"], sv["w3"], full_w[l, "ffn2_w_down"], full_w[l, "ffn2_w_gu"], sv["x2"],
            row_vec(norm_ffn2_g[l]), sv["sc"][2], 0.5, seq, "ffn_bwd", comm)
        hosted(keys, got)
        gw_down2, _ = _weight_grad(sv["a3"], df3, seq, "grad_w_down")
        scatter_later(l, "ffn2_w_down", gw_down2)
        keys, comm = host()
        gw_gu2, got = _weight_grad(dgu3, sv["h3"], seq, "grad_w_gu", comm)
        hosted(keys, got)
        scatter_later(l, "ffn2_w_gu", gw_gu2)
        do2, dg2, dymix, _ = _residual_backward(dx2, sv["gates"][1], sv["o2"], full_w[l, "mix_w_out"], 1.0, seq, "mix_out_bwd")
        gw_out, _ = _weight_grad(sv["ymix"], do2, seq, "grad_w_out")
        keys, comm = host()
        dproj, d_og, d_cw, d_lng, d_lnb, d_bias, d_wm, got = _mixer_backward(
            sv["proj"], dymix, causal=causal, seq=seq, name="mixer_backward", comm=comm, **mc)
        hosted(keys, got)
        scatter_later(l, "mix_w_out", gw_out)
        keys, comm = host()
        dx1, dsh2, dsc2, dn2, got = _matmul_normmod_backward(dproj, full_w[l, "mix_w_in"], sv["x1"], dx2, row_vec(norm_mix_g[l]), sv["sc"][1], seq, "mix_in_bwd", comm)
        hosted(keys, got)
        gw_in, _ = _weight_grad(dproj, sv["h2"], seq, "grad_w_in")
        scatter_later(l, "mix_w_in", gw_in)
        keys, comm = host()
        if l > 0:
            df1, dg1, dgu1, dx0, dsh1, dsc1, dn1, got = _ffn_backward(
                dx1, sv["gates"][0], sv["f1"], sv["s1"], sv["w1"], full_w[l, "ffn1_w_down"], full_w[l, "ffn1_w_gu"], sv["x0"],
                row_vec(norm_ffn1_g[l]), sv["sc"][0], 0.5, seq, "ffn_bwd", comm)
        else:
            df1, dg1, dgu1, got = _residual_backward(dx1, sv["gates"][0], sv["f1"], full_w[l, "ffn1_w_down"], 0.5, seq, "ffn_down_bwd", sv["s1"], sv["w1"], comm)
        hosted(keys, got)
        gw_down1, _ = _weight_grad(sv["a1"], df1, seq, "grad_w_down")
        scatter_later(l, "ffn1_w_down", gw_down1)
        keys, comm = host()
        gw_gu1, got = _weight_grad(dgu1, sv["h1"], seq, "grad_w_gu", comm)
        hosted(keys, got)
        scatter_later(l, "ffn1_w_gu", gw_gu1)
        if l == 0:
            keys, comm = host()
            dx0, dsh1, dsc1, dn1, got = _matmul_normmod_backward(dgu1, full_w[l, "ffn1_w_gu"], sv["x0"], dx1, row_vec(norm_ffn1_g[l]), sv["sc"][0], seq, "ffn_up_bwd", comm)
            hosted(keys, got)
        dx = dx0
        small_grads[l] = dict(
            norm_ffn1_g=dn1, norm_mix_g=dn2, norm_ffn2_g=dn3, out_norm_g=d_og,
            sgu_ln_g=d_lng.reshape(N_HEADS, HEAD_DIM).sum(0), sgu_ln_b=d_lnb.reshape(N_HEADS, HEAD_DIM).sum(0),
            sgu_w_s=d_wm, sgu_b=d_bias.reshape(CHUNK, N_HEADS, HEAD_DIM).sum(-1).T, conv_w=d_cw[0:3])
        d_mod[l] = jnp.concatenate([dsh1, dsc1, dg1, dsh2, dsc2, dg2, dsh3, dsc3, dg3], axis=1)
    grad_x = dx.reshape(batch, seq, D_MODEL)

    grad_big = {}
    for nm in big:
        per_layer = []
        for l in range(DEPTH):
            g_sum = _sum_slots(recv[l, nm], "sum_" + nm)
            per_layer.append(g_sum.T if nm in transposed else g_sum)
        grad_big[nm] = jnp.stack(per_layer)

    last_parts = small_parts(0) + [d_final_g, loss_cols]
    last_shapes = [p.shape for p in last_parts]
    packed_all, packed_sum = _all_gather_small(_pack_small(last_parts), "reduce_small")
    summed = {0: _unpack_small(packed_sum, last_shapes)}
    d_mod_dev = {0: _unpack_small(packed_all, last_shapes, lead=(N_DEV,))[len(small_names)]}
    for l in range(1, DEPTH):
        shapes_l = [p.shape for p in small_parts(l)]
        gathered_l = recv["small", l].reshape(N_DEV, -1, LANES)
        summed[l] = _unpack_small(_sum_gathered(gathered_l, "sum_small"), shapes_l)
        d_mod_dev[l] = _unpack_small(gathered_l, shapes_l, lead=(N_DEV,))[len(small_names)]
    grad_small = {nm: jnp.stack([summed[l][k] for l in range(DEPTH)]).reshape(
        (DEPTH, 3, D_A) if nm == "conv_w" else weights[nm].shape) for k, nm in enumerate(small_names)}
    grad_small["conv_w"] = lax.dynamic_slice_in_dim(grad_small["conv_w"], me * conv_w.shape[2], conv_w.shape[2], axis=2)
    grad_small["final_norm_g"] = summed[0][len(small_names) + 1].reshape(final_norm_g.shape)
    loss = jnp.sum(summed[0][len(small_names) + 2])
    d_ada_all = jnp.stack([d_mod_dev[l] for l in range(DEPTH)]).reshape(DEPTH, N_DEV * batch, N_MOD * D_MODEL)
    d_ada_cols = lax.dynamic_slice_in_dim(d_ada_all, me * ada_cols, ada_cols, axis=2)
    g_ada_w, g_ada_b = _ada_backward(c_all, d_ada_cols, d_ada_all)

    grads = dict(grad_big)
    grads.update(grad_small)
    grads["ada_w"] = g_ada_w
    grads["ada_b"] = g_ada_b.reshape(ada_b.shape)

    names = ("ada_w", "ada_b", "norm_ffn1_g", "ffn1_w_gu", "ffn1_w_down", "norm_mix_g", "mix_w_in", "sgu_ln_g",
             "sgu_ln_b", "sgu_w_s", "sgu_b", "conv_w", "out_norm_g", "mix_w_out", "norm_ffn2_g", "ffn2_w_gu",
             "ffn2_w_down", "final_norm_g")
    large = ("ada_w",) + big
    delta, new_m, new_v = {}, {}, {}
    for nm in large:
        delta[nm], new_m[nm], new_v[nm] = _adamw_nd(weights[nm], grads[nm], mom1[nm], mom2[nm], "adamw_" + nm)
    rest = [nm for nm in names if nm not in large]
    pack = lambda src: _pack_small([src[nm] for nm in rest])
    d_p, m_p, v_p = _adamw(pack(weights), pack(grads), pack(mom1), pack(mom2), "adamw_small")
    rest_shapes = [weights[nm].shape for nm in rest]
    for nm, d_k, m_k, v_k in zip(rest, _unpack_small(d_p, rest_shapes), _unpack_small(m_p, rest_shapes), _unpack_small(v_p, rest_shapes)):
        delta[nm], new_m[nm], new_v[nm] = d_k, m_k, v_k

    return (loss, grad_x, *[grads[nm] for nm in names], *[delta[nm] for nm in names],
            *[new_m[nm] for nm in names], *[new_v[nm] for nm in names])
```

```python
import math

import jax
import jax.numpy as jnp
from jax import lax
from jax.experimental import pallas as pl
from jax.experimental.pallas import tpu as pltpu

F32 = jnp.float32
BF16 = jnp.bfloat16

D_MODEL = 1024
D_FF = 2816
D_A = 512
D_PROJ = 2560
N_HEADS = 8
HEAD_DIM = 64
CHUNK = 128
N_MOD = 9
DEPTH = 2
EPS = 1e-6
N_DEV = 8
LANES = 128
MXU_N = 256
HALO = 16
VMEM_LIMIT = 56 * 1024 * 1024
FORWARD_STEPS = 4

ADAM_LR = 0.001
ADAM_B1 = 0.9
ADAM_B2 = 0.999
ADAM_EPS = 1e-08
ADAM_WD = 0.01
ADAM_STEP = 10

MESH = pl.DeviceIdType.MESH


def _dot(a, b):
    return jnp.dot(a, b, preferred_element_type=F32)


def _dot_nt(a, b):
    return lax.dot_general(a, b, (((1,), (1,)), ((), ())), preferred_element_type=F32)


def _dot_tn(a, b):
    return lax.dot_general(a, b, (((0,), (0,)), ((), ())), preferred_element_type=F32)


def _sigmoid(x):
    return 0.5 * jnp.tanh(0.5 * x) + 0.5


def _gelu(x):
    return 0.5 * x * (1.0 + lax.erf(x * (1.0 / math.sqrt(2.0))))


def _gelu_grad(x):
    cdf = 0.5 * (1.0 + lax.erf(x * (1.0 / math.sqrt(2.0))))
    return cdf + x * jnp.exp(-0.5 * x * x) * (1.0 / math.sqrt(2.0 * math.pi))


def _params(n_axes=1, parallel=False):
    sem = ("parallel" if parallel else "arbitrary",) * n_axes
    return pltpu.CompilerParams(dimension_semantics=sem, vmem_limit_bytes=VMEM_LIMIT)


def _resident(shape):
    nd = len(shape)
    return pl.BlockSpec(shape, lambda *_: (0,) * nd, pipeline_mode=pl.Buffered(1))


def _tile_rows(seq):
    return min(512, seq)


def _my_position():
    x, y, c = lax.axis_index("x"), lax.axis_index("y"), lax.axis_index("c")
    return x, y, c, 4 * x + 2 * y + c


def _peer(x, y, c, p):
    return (x ^ ((p >> 2) & 1), y ^ ((p >> 1) & 1), c ^ (p & 1))


class _GatherRows:
    def __init__(self, shards):
        self.operands = list(shards)
        n = len(shards)
        self.out_shape = [jax.ShapeDtypeStruct((N_DEV * s.shape[0], s.shape[1]), s.dtype) for s in shards]
        self.scratch = [pltpu.SemaphoreType.DMA((n, N_DEV - 1)), pltpu.SemaphoreType.DMA((n, N_DEV - 1)),
                        pltpu.SemaphoreType.DMA((n,))]

    def _plan(self, src, dst, send, recv, loc):
        x, y, c, _ = _my_position()
        me, sib = (x, y, c), (x, y, 1 - c)
        chips = [(1 - x, y), (x, 1 - y), (1 - x, 1 - y)]
        plans = []
        for k, shard in enumerate(self.operands):
            rows = shard.shape[0]

            def blk(pos, k=k, rows=rows):
                return dst[k].at[pl.ds((4 * pos[0] + 2 * pos[1] + pos[2]) * rows, rows), :]

            def rc(s, block, to, source=None, k=k, blk=blk):
                return pltpu.make_async_remote_copy(
                    src_ref=blk(block) if source is None else source, dst_ref=blk(block),
                    send_sem=send.at[k, s], recv_sem=recv.at[k, s], device_id=to, device_id_type=MESH)

            plans.append(dict(
                local=pltpu.make_async_copy(src[k], blk(me), loc.at[k]),
                first=[rc(0, me, sib, src[k])] + [rc(1 + j, me, (*chip, c), src[k]) for j, chip in enumerate(chips)],
                landed=[rc(1 + j, (*chip, c), me) for j, chip in enumerate(chips)],
                passed=[rc(4 + j, (*chip, c), sib) for j, chip in enumerate(chips)],
                from_sib=[rc(0, sib, me)] + [rc(4 + j, (*chip, 1 - c), me) for j, chip in enumerate(chips)]))
        return plans

    def start(self, src, dst, send, recv, loc):
        for plan in self._plan(src, dst, send, recv, loc):
            plan["local"].start()
            for cp in plan["first"]:
                cp.start()

    def forward(self, src, dst, send, recv, loc):
        for plan in self._plan(src, dst, send, recv, loc):
            for landed, passed in zip(plan["landed"], plan["passed"]):
                landed.wait_recv()
                passed.start()

    def finish(self, src, dst, send, recv, loc):
        for plan in self._plan(src, dst, send, recv, loc):
            for cp in plan["from_sib"]:
                cp.wait_recv()
            for cp in plan["first"] + plan["passed"]:
                cp.wait_send()
            plan["local"].wait()


class _ScatterRows:
    def __init__(self, grads):
        self.operands = list(grads)
        n = len(grads)
        self.out_shape = [jax.ShapeDtypeStruct((N_DEV, g.shape[0] // N_DEV, g.shape[1]), g.dtype) for g in grads]
        self.scratch = [pltpu.SemaphoreType.DMA((n, N_DEV - 1)), pltpu.SemaphoreType.DMA((n, N_DEV - 1)),
                        pltpu.SemaphoreType.DMA((n,))]

    def _plan(self, src, dst, send, recv, loc):
        x, y, c, me = _my_position()
        copies = []
        for k, grad in enumerate(self.operands):
            rows = grad.shape[0] // N_DEV
            copies.append(pltpu.make_async_copy(src[k].at[pl.ds(me * rows, rows), :], dst[k].at[me], loc.at[k]))
            for p in range(1, N_DEV):
                px, py, pc = _peer(x, y, c, p)
                copies.append(pltpu.make_async_remote_copy(
                    src_ref=src[k].at[pl.ds((4 * px + 2 * py + pc) * rows, rows), :], dst_ref=dst[k].at[me],
                    send_sem=send.at[k, p - 1], recv_sem=recv.at[k, p - 1], device_id=(px, py, pc), device_id_type=MESH))
        return copies

    def start(self, src, dst, send, recv, loc):
        for cp in self._plan(src, dst, send, recv, loc):
            cp.start()

    def forward(self, src, dst, send, recv, loc):
        pass

    def finish(self, src, dst, send, recv, loc):
        for cp in self._plan(src, dst, send, recv, loc):
            cp.wait()


class _Exchanges:
    def __init__(self, parts):
        self.parts = list(parts)
        self.operands = [op for part in self.parts for op in part.operands]
        self.out_shape = [shp for part in self.parts for shp in part.out_shape]
        self.scratch = [scr for part in self.parts for scr in part.scratch]

    def _each(self, src, dst, sems):
        at, sem_at = 0, 0
        for part in self.parts:
            n, n_sem = len(part.operands), len(part.scratch)
            yield part, src[at:at + n], dst[at:at + n], sems[sem_at:sem_at + n_sem]
            at, sem_at = at + n, sem_at + n_sem

    def start(self, src, dst, *sems):
        for part, part_src, part_dst, part_sems in self._each(src, dst, sems):
            part.start(part_src, part_dst, *part_sems)

    def forward(self, src, dst, *sems):
        for part, part_src, part_dst, part_sems in self._each(src, dst, sems):
            part.forward(part_src, part_dst, *part_sems)

    def finish(self, src, dst, *sems):
        for part, part_src, part_dst, part_sems in self._each(src, dst, sems):
            part.finish(part_src, part_dst, *part_sems)


_ANY = pl.BlockSpec(memory_space=pl.ANY)


def _comm_only(comm, name):
    n = len(comm.operands)

    def body(*refs):
        src, dst, sems = refs[:n], refs[n:2 * n], refs[2 * n:]
        comm.start(src, dst, *sems)
        comm.forward(src, dst, *sems)
        comm.finish(src, dst, *sems)

    return pl.pallas_call(body, name=name, out_shape=comm.out_shape, in_specs=[_ANY] * n, out_specs=[_ANY] * n,
                          scratch_shapes=comm.scratch)(*comm.operands)


def _call(body, *, name, grid, in_specs, out_specs, out_shape, operands, scratch_shapes=(), parallel=False, comm=None):
    n_axes = len(grid)
    if comm is None:
        outs = pl.pallas_call(body, name=name, grid=grid, out_shape=list(out_shape), in_specs=list(in_specs),
                              out_specs=list(out_specs), scratch_shapes=list(scratch_shapes),
                              compiler_params=_params(n_axes, parallel))(*operands)
        return list(outs), None
    n_in, n_out, n_scr, n_c = len(in_specs), len(out_specs), len(scratch_shapes), len(comm.operands)
    total = math.prod(grid)

    def hosted(*refs):
        ins, c_src = refs[:n_in], refs[n_in:n_in + n_c]
        outs, c_dst = refs[n_in + n_c:n_in + n_c + n_out], refs[n_in + n_c + n_out:n_in + 2 * n_c + n_out]
        scr, sems = refs[n_in + 2 * n_c + n_out:n_in + 2 * n_c + n_out + n_scr], refs[n_in + 2 * n_c + n_out + n_scr:]
        step = pl.program_id(0)
        for axis in range(1, n_axes):
            step = step * grid[axis] + pl.program_id(axis)

        @pl.when(step == 0)
        def _():
            comm.start(c_src, c_dst, *sems)

        @pl.when(step == max(total - FORWARD_STEPS, 0))
        def _():
            comm.forward(c_src, c_dst, *sems)

        body(*ins, *outs, *scr)

        @pl.when(step == total - 1)
        def _():
            comm.finish(c_src, c_dst, *sems)

    res = pl.pallas_call(hosted, name=name, grid=grid, out_shape=list(out_shape) + comm.out_shape,
                         in_specs=list(in_specs) + [_ANY] * n_c, out_specs=list(out_specs) + [_ANY] * n_c,
                         scratch_shapes=list(scratch_shapes) + comm.scratch,
                         compiler_params=_params(n_axes, False))(*operands, *comm.operands)
    return list(res[:n_out]), list(res[n_out:])


def _all_gather_small(v, name):
    rows = v.shape[0]

    def body(v_ref, all_ref, sum_ref, send_sems, recv_sems):
        x, y, c, me = _my_position()
        all_ref[me] = v_ref[...]
        copies = []
        for p in range(1, N_DEV):
            cp = pltpu.make_async_remote_copy(
                src_ref=v_ref, dst_ref=all_ref.at[me], send_sem=send_sems.at[p - 1], recv_sem=recv_sems.at[p - 1],
                device_id=_peer(x, y, c, p), device_id_type=MESH)
            cp.start()
            copies.append(cp)
        for cp in copies:
            cp.wait()
        acc = all_ref[0]
        for d in range(1, N_DEV):
            acc = acc + all_ref[d]
        sum_ref[...] = acc

    return pl.pallas_call(
        body, name=name,
        out_shape=[jax.ShapeDtypeStruct((N_DEV, rows, LANES), F32), jax.ShapeDtypeStruct((rows, LANES), F32)],
        in_specs=[pl.BlockSpec(memory_space=pltpu.VMEM)],
        out_specs=[pl.BlockSpec(memory_space=pltpu.VMEM)] * 2,
        scratch_shapes=[pltpu.SemaphoreType.DMA((N_DEV - 1,)), pltpu.SemaphoreType.DMA((N_DEV - 1,))],
        compiler_params=pltpu.CompilerParams(vmem_limit_bytes=VMEM_LIMIT),
    )(v)


def _exchange_batch_rows(v, name):
    depth, _, cols = v.shape
    sub = 8

    def body(v_ref, o_ref, send_sems, recv_sems):
        x, y, c, me = _my_position()
        o_ref[me] = v_ref[:, pl.ds(pl.multiple_of(me * sub, sub), sub), :]
        copies = []
        for p in range(1, N_DEV):
            px, py, pc = _peer(x, y, c, p)
            rows = pl.ds(pl.multiple_of((4 * px + 2 * py + pc) * sub, sub), sub)
            cp = pltpu.make_async_remote_copy(
                src_ref=v_ref.at[:, rows, :], dst_ref=o_ref.at[me], send_sem=send_sems.at[p - 1],
                recv_sem=recv_sems.at[p - 1], device_id=(px, py, pc), device_id_type=MESH)
            cp.start()
            copies.append(cp)
        for cp in copies:
            cp.wait()

    return pl.pallas_call(
        body, name=name, out_shape=jax.ShapeDtypeStruct((N_DEV, depth, sub, cols), F32),
        in_specs=[pl.BlockSpec(memory_space=pltpu.VMEM)], out_specs=pl.BlockSpec(memory_space=pltpu.VMEM),
        scratch_shapes=[pltpu.SemaphoreType.DMA((N_DEV - 1,)), pltpu.SemaphoreType.DMA((N_DEV - 1,))],
        compiler_params=pltpu.CompilerParams(vmem_limit_bytes=VMEM_LIMIT),
    )(v)


def _sum_gathered(gathered, name):
    rows = gathered.shape[1]

    def body(g_ref, o_ref):
        acc = g_ref[0]
        for d in range(1, N_DEV):
            acc = acc + g_ref[d]
        o_ref[...] = acc

    return pl.pallas_call(
        body, name=name, out_shape=jax.ShapeDtypeStruct((rows, LANES), F32),
        in_specs=[pl.BlockSpec(memory_space=pltpu.VMEM)], out_specs=pl.BlockSpec(memory_space=pltpu.VMEM),
        compiler_params=pltpu.CompilerParams(vmem_limit_bytes=VMEM_LIMIT),
    )(gathered)


def _pack_small(parts):
    flat = jnp.concatenate([p.reshape(-1).astype(F32) for p in parts])
    total = flat.shape[0]
    padded = -(-total // (8 * LANES)) * (8 * LANES)
    flat = jnp.pad(flat, (0, padded - total))
    return flat.reshape(padded // LANES, LANES)


def _unpack_small(packed, shapes, lead=()):
    flat = packed.reshape(lead + (-1,))
    out, off = [], 0
    for shp in shapes:
        size = math.prod(shp)
        out.append(flat[..., off:off + size].reshape(lead + tuple(shp)))
        off += size
    return out


def _ada_forward(c_all, ada_w, ada_b_cols):
    nb = c_all.shape[0]
    cols = ada_w.shape[2]

    def body(c_ref, w_ref, b_ref, o_ref):
        cv = c_ref[...]
        act = (cv * _sigmoid(cv)).astype(BF16)
        o_ref[0] = _dot(act, w_ref[0].astype(BF16)) + b_ref[0]

    return pl.pallas_call(
        body, name="ada_forward", grid=(DEPTH,),
        out_shape=jax.ShapeDtypeStruct((DEPTH, nb, cols), F32),
        in_specs=[pl.BlockSpec((nb, D_MODEL), lambda l: (0, 0)),
                  pl.BlockSpec((1, D_MODEL, cols), lambda l: (l, 0, 0)),
                  pl.BlockSpec((1, 1, cols), lambda l: (l, 0, 0))],
        out_specs=pl.BlockSpec((1, nb, cols), lambda l: (l, 0, 0)),
        compiler_params=_params(),
    )(c_all, ada_w, ada_b_cols)


def _ada_backward(c_all, d_ada_cols, d_ada_all):
    nb = c_all.shape[0]
    cols = d_ada_cols.shape[2]
    full = d_ada_all.shape[2]

    def body(c_ref, dc_ref, da_ref, gw_ref, gb_ref):
        cv = c_ref[...]
        act = (cv * _sigmoid(cv)).astype(BF16)
        gw_ref[0] = _dot_tn(act, dc_ref[0].astype(BF16))
        gb_ref[0] = jnp.sum(da_ref[0], axis=0, keepdims=True)

    return pl.pallas_call(
        body, name="ada_backward", grid=(DEPTH,),
        out_shape=[jax.ShapeDtypeStruct((DEPTH, D_MODEL, cols), F32), jax.ShapeDtypeStruct((DEPTH, 1, full), F32)],
        in_specs=[pl.BlockSpec((nb, D_MODEL), lambda l: (0, 0)),
                  pl.BlockSpec((1, nb, cols), lambda l: (l, 0, 0)),
                  pl.BlockSpec((1, nb, full), lambda l: (l, 0, 0))],
        out_specs=[pl.BlockSpec((1, D_MODEL, cols), lambda l: (l, 0, 0)),
                   pl.BlockSpec((1, 1, full), lambda l: (l, 0, 0))],
        compiler_params=_params(),
    )(c_all, d_ada_cols, d_ada_all)


def _rms(xv):
    return lax.rsqrt(jnp.mean(xv * xv, axis=-1, keepdims=True) + EPS)


def _normmod_matmul(x, gnorm, scale1p, shift, w_t, seq, name, swiglu=False, comm=None):
    tokens, n_out = x.shape[0], w_t.shape[0]
    tm = _tile_rows(seq)
    per_seq = seq // tm
    width = n_out // 2 if swiglu else n_out
    n_chunks = width // MXU_N

    def body(x_ref, g_ref, sc_ref, sh_ref, w_ref, h_ref, *o_refs):
        xv = x_ref[...]
        h = (xv * _rms(xv) * g_ref[...]) * sc_ref[0] + sh_ref[0]
        h_ref[...] = h.astype(BF16)
        for ck in range(n_chunks):
            cs = slice(ck * MXU_N, (ck + 1) * MXU_N)
            if swiglu:
                act_ref, silu_ref, dact_ref = o_refs
                g = _dot_nt(h_ref[...], w_ref[cs, :])
                u = _dot_nt(h_ref[...], w_ref[width + ck * MXU_N:width + (ck + 1) * MXU_N, :])
                sig = _sigmoid(g)
                silu = g * sig
                act_ref[:, cs] = (silu * u).astype(BF16)
                silu_ref[:, cs] = silu.astype(BF16)
                dact_ref[:, cs] = (u * (sig + silu * (1.0 - sig))).astype(BF16)
            else:
                o_refs[0][:, cs] = _dot_nt(h_ref[...], w_ref[cs, :]).astype(BF16)

    n_res = 3 if swiglu else 1
    per_batch = pl.BlockSpec((1, 1, D_MODEL), lambda i: (i // per_seq, 0, 0))
    outs, got = _call(
        body, name=name, grid=(tokens // tm,),
        out_shape=[jax.ShapeDtypeStruct((tokens, D_MODEL), BF16)] + [jax.ShapeDtypeStruct((tokens, width), BF16)] * n_res,
        in_specs=[pl.BlockSpec((tm, D_MODEL), lambda i: (i, 0)), _resident((1, D_MODEL)), per_batch, per_batch,
                  _resident(w_t.shape)],
        out_specs=[pl.BlockSpec((tm, D_MODEL), lambda i: (i, 0))] + [pl.BlockSpec((tm, width), lambda i: (i, 0))] * n_res,
        operands=(x, gnorm, scale1p, shift, w_t), parallel=True, comm=comm)
    return (*outs, got)


def _matmul_residual(src, w, x, gate, scale, seq, name, comm=None):
    tokens, k_dim = x.shape[0], w.shape[0]
    tm = _tile_rows(seq)
    per_seq = seq // tm

    def body(s_ref, w_ref, x_ref, gate_ref, xo_ref, f_ref):
        f = _dot(s_ref[...], w_ref[...])
        f_ref[...] = f.astype(BF16)
        xo_ref[...] = x_ref[...] + (scale * gate_ref[0]) * f

    (x_out, f), got = _call(
        body, name=name, grid=(tokens // tm,),
        out_shape=[jax.ShapeDtypeStruct((tokens, D_MODEL), F32), jax.ShapeDtypeStruct((tokens, D_MODEL), BF16)],
        in_specs=[pl.BlockSpec((tm, k_dim), lambda i: (i, 0)), _resident(w.shape),
                  pl.BlockSpec((tm, D_MODEL), lambda i: (i, 0)),
                  pl.BlockSpec((1, 1, D_MODEL), lambda i: (i // per_seq, 0, 0))],
        out_specs=[pl.BlockSpec((tm, D_MODEL), lambda i: (i, 0))] * 2,
        operands=(src, w, x, gate), parallel=True, comm=comm)
    return x_out, f, got


def _ffn_forward(x, gnorm, scale1p, shift, w_gu_t, w_down, gate, scale, seq, name, comm=None):
    tokens, width = x.shape[0], w_down.shape[0]
    tm = _tile_rows(seq)
    per_seq = seq // tm
    n_chunks = width // MXU_N

    def body(x_ref, g_ref, sc_ref, sh_ref, wgu_ref, wd_ref, gate_ref, h_ref, act_ref, silu_ref, dact_ref, xo_ref, f_ref):
        xv = x_ref[...]
        h = (xv * _rms(xv) * g_ref[...]) * sc_ref[0] + sh_ref[0]
        h_ref[...] = h.astype(BF16)
        for ck in range(n_chunks):
            cs = slice(ck * MXU_N, (ck + 1) * MXU_N)
            g = _dot_nt(h_ref[...], wgu_ref[cs, :])
            u = _dot_nt(h_ref[...], wgu_ref[width + ck * MXU_N:width + (ck + 1) * MXU_N, :])
            sig = _sigmoid(g)
            silu = g * sig
            act_ref[:, cs] = (silu * u).astype(BF16)
            silu_ref[:, cs] = silu.astype(BF16)
            dact_ref[:, cs] = (u * (sig + silu * (1.0 - sig))).astype(BF16)
        f = _dot(act_ref[...], wd_ref[...])
        f_ref[...] = f.astype(BF16)
        xo_ref[...] = xv + (scale * gate_ref[0]) * f

    row = lambda i: (i, 0)
    per_batch = pl.BlockSpec((1, 1, D_MODEL), lambda i: (i // per_seq, 0, 0))
    tile = lambda cols: pl.BlockSpec((tm, cols), row)
    wide = jax.ShapeDtypeStruct((tokens, width), BF16)
    outs, got = _call(
        body, name=name, grid=(tokens // tm,),
        out_shape=[jax.ShapeDtypeStruct((tokens, D_MODEL), BF16), wide, wide, wide,
                   jax.ShapeDtypeStruct((tokens, D_MODEL), F32), jax.ShapeDtypeStruct((tokens, D_MODEL), BF16)],
        in_specs=[tile(D_MODEL), _resident((1, D_MODEL)), per_batch, per_batch, _resident(w_gu_t.shape),
                  _resident(w_down.shape), per_batch],
        out_specs=[tile(D_MODEL), tile(width), tile(width), tile(width), tile(D_MODEL), tile(D_MODEL)],
        operands=(x, gnorm, scale1p, shift, w_gu_t, w_down, gate), parallel=True, comm=comm)
    return (*outs, got)


def _residual_backward(dy, gate, f, w, scale, seq, name, silu=None, dact=None, comm=None):
    tokens, k_dim = dy.shape[0], w.shape[0]
    batch = tokens // seq
    tm = _tile_rows(seq)
    per_seq = seq // tm
    n_chunks = k_dim // MXU_N
    swiglu = silu is not None

    def body(*refs):
        if swiglu:
            dy_ref, gate_ref, f_ref, silu_ref, dact_ref, w_ref, df_ref, dgate_ref, dgu_ref = refs
        else:
            dy_ref, gate_ref, f_ref, w_ref, df_ref, dgate_ref, dsrc_ref = refs
        i = pl.program_id(0)
        dy_v = dy_ref[...]
        df_ref[...] = ((scale * gate_ref[0]) * dy_v).astype(BF16)
        part = scale * jnp.sum(dy_v * f_ref[...].astype(F32), axis=0, keepdims=True)

        @pl.when(i % per_seq == 0)
        def _():
            dgate_ref[0] = part

        @pl.when(i % per_seq != 0)
        def _():
            dgate_ref[0] = dgate_ref[0] + part

        if swiglu:
            for ck in range(n_chunks):
                cs = slice(ck * MXU_N, (ck + 1) * MXU_N)
                cu = slice(k_dim + ck * MXU_N, k_dim + (ck + 1) * MXU_N)
                da = _dot_nt(df_ref[...], w_ref[cs, :])
                dgu_ref[:, cs] = (da * dact_ref[:, cs].astype(F32)).astype(BF16)
                dgu_ref[:, cu] = (da * silu_ref[:, cs].astype(F32)).astype(BF16)
        else:
            dsrc_ref[...] = _dot_nt(df_ref[...], w_ref[...])

    row = lambda i: (i, 0)
    per_batch = pl.BlockSpec((1, 1, D_MODEL), lambda i: (i // per_seq, 0, 0))
    in_specs = [pl.BlockSpec((tm, D_MODEL), row), per_batch, pl.BlockSpec((tm, D_MODEL), row)]
    out_shape = [jax.ShapeDtypeStruct((tokens, D_MODEL), BF16), jax.ShapeDtypeStruct((batch, 1, D_MODEL), F32)]
    out_specs = [pl.BlockSpec((tm, D_MODEL), row), per_batch]
    if swiglu:
        in_specs += [pl.BlockSpec((tm, k_dim), row)] * 2
        operands = (dy, gate, f, silu, dact, w)
        out_shape += [jax.ShapeDtypeStruct((tokens, 2 * k_dim), BF16)]
        out_specs += [pl.BlockSpec((tm, 2 * k_dim), row)]
    else:
        operands = (dy, gate, f, w)
        out_shape += [jax.ShapeDtypeStruct((tokens, k_dim), F32)]
        out_specs += [pl.BlockSpec((tm, k_dim), row)]
    in_specs += [_resident(w.shape)]
    outs, got = _call(body, name=name, grid=(tokens // tm,), out_shape=out_shape, in_specs=in_specs,
                      out_specs=out_specs, operands=operands, comm=comm)
    return (*outs, got)


def _matmul_normmod_backward(dsrc, w_t, x, dy, gnorm, scale1p, seq, name, comm=None):
    tokens, k_dim = dsrc.shape
    batch = tokens // seq
    tm = _tile_rows(seq)
    per_seq = seq // tm

    def body(ds_ref, w_ref, x_ref, dy_ref, g_ref, sc_ref, dx_ref, dsh_ref, dsc_ref, dg_ref):
        i = pl.program_id(0)
        dh = _dot(ds_ref[...], w_ref[...])
        xv = x_ref[...]
        r = _rms(xv)
        xn = xv * r
        gn = g_ref[...]
        dsh = jnp.sum(dh, axis=0, keepdims=True)
        dsc = jnp.sum(dh * (xn * gn), axis=0, keepdims=True)
        dhn = dh * sc_ref[0]
        dg = jnp.sum(dhn * xn, axis=0, keepdims=True)
        dxn = dhn * gn
        dx_ref[...] = dy_ref[...] + r * (dxn - xn * jnp.mean(dxn * xn, axis=-1, keepdims=True))

        @pl.when(i % per_seq == 0)
        def _():
            dsh_ref[0] = dsh
            dsc_ref[0] = dsc

        @pl.when(i % per_seq != 0)
        def _():
            dsh_ref[0] = dsh_ref[0] + dsh
            dsc_ref[0] = dsc_ref[0] + dsc

        @pl.when(i == 0)
        def _():
            dg_ref[...] = dg

        @pl.when(i != 0)
        def _():
            dg_ref[...] = dg_ref[...] + dg

    row = lambda i: (i, 0)
    per_batch = pl.BlockSpec((1, 1, D_MODEL), lambda i: (i // per_seq, 0, 0))
    outs, got = _call(
        body, name=name, grid=(tokens // tm,),
        out_shape=[jax.ShapeDtypeStruct((tokens, D_MODEL), F32), jax.ShapeDtypeStruct((batch, 1, D_MODEL), F32),
                   jax.ShapeDtypeStruct((batch, 1, D_MODEL), F32), jax.ShapeDtypeStruct((1, D_MODEL), F32)],
        in_specs=[pl.BlockSpec((tm, k_dim), row), _resident(w_t.shape), pl.BlockSpec((tm, D_MODEL), row),
                  pl.BlockSpec((tm, D_MODEL), row), _resident((1, D_MODEL)), per_batch],
        out_specs=[pl.BlockSpec((tm, D_MODEL), row), per_batch, per_batch, pl.BlockSpec((1, D_MODEL), lambda i: (0, 0))],
        operands=(dsrc, w_t, x, dy, gnorm, scale1p), comm=comm)
    return (*outs, got)


def _ffn_backward(dy, gate, f, silu, dact, w_down, w_gu_t, x, gnorm, scale1p, scale, seq, name, comm=None):
    tokens, k_dim = dy.shape[0], w_down.shape[0]
    batch = tokens // seq
    tm = min(256, seq)
    per_seq = seq // tm
    n_chunks = k_dim // MXU_N

    def body(dy_ref, gate_ref, f_ref, silu_ref, dact_ref, wd_ref, wgu_ref, x_ref, g_ref, sc_ref,
             df_ref, dgate_ref, dgu_ref, dx_ref, dsh_ref, dsc_ref, dg_ref):
        i = pl.program_id(0)
        dy_v = dy_ref[...]
        df_ref[...] = ((scale * gate_ref[0]) * dy_v).astype(BF16)
        dgate = scale * jnp.sum(dy_v * f_ref[...].astype(F32), axis=0, keepdims=True)
        for ck in range(n_chunks):
            cs = slice(ck * MXU_N, (ck + 1) * MXU_N)
            cu = slice(k_dim + ck * MXU_N, k_dim + (ck + 1) * MXU_N)
            da = _dot_nt(df_ref[...], wd_ref[cs, :])
            dgu_ref[:, cs] = (da * dact_ref[:, cs].astype(F32)).astype(BF16)
            dgu_ref[:, cu] = (da * silu_ref[:, cs].astype(F32)).astype(BF16)
        dh = _dot(dgu_ref[...], wgu_ref[...])
        xv = x_ref[...]
        r = _rms(xv)
        xn = xv * r
        gn = g_ref[...]
        dsh = jnp.sum(dh, axis=0, keepdims=True)
        dsc = jnp.sum(dh * (xn * gn), axis=0, keepdims=True)
        dhn = dh * sc_ref[0]
        dg = jnp.sum(dhn * xn, axis=0, keepdims=True)
        dxn = dhn * gn
        dx_ref[...] = dy_v + r * (dxn - xn * jnp.mean(dxn * xn, axis=-1, keepdims=True))

        @pl.when(i % per_seq == 0)
        def _():
            dgate_ref[0] = dgate
            dsh_ref[0] = dsh
            dsc_ref[0] = dsc

        @pl.when(i % per_seq != 0)
        def _():
            dgate_ref[0] = dgate_ref[0] + dgate
            dsh_ref[0] = dsh_ref[0] + dsh
            dsc_ref[0] = dsc_ref[0] + dsc

        @pl.when(i == 0)
        def _():
            dg_ref[...] = dg

        @pl.when(i != 0)
        def _():
            dg_ref[...] = dg_ref[...] + dg

    row = lambda i: (i, 0)
    per_batch = pl.BlockSpec((1, 1, D_MODEL), lambda i: (i // per_seq, 0, 0))
    tile = lambda width: pl.BlockSpec((tm, width), row)
    vec = jax.ShapeDtypeStruct((batch, 1, D_MODEL), F32)
    outs, got = _call(
        body, name=name, grid=(tokens // tm,),
        out_shape=[jax.ShapeDtypeStruct((tokens, D_MODEL), BF16), vec, jax.ShapeDtypeStruct((tokens, 2 * k_dim), BF16),
                   jax.ShapeDtypeStruct((tokens, D_MODEL), F32), vec, vec, jax.ShapeDtypeStruct((1, D_MODEL), F32)],
        in_specs=[tile(D_MODEL), per_batch, tile(D_MODEL), tile(k_dim), tile(k_dim), _resident(w_down.shape),
                  _resident(w_gu_t.shape), tile(D_MODEL), _resident((1, D_MODEL)), per_batch],
        out_specs=[tile(D_MODEL), per_batch, tile(2 * k_dim), tile(D_MODEL), per_batch, per_batch,
                   pl.BlockSpec((1, D_MODEL), lambda i: (0, 0))],
        operands=(dy, gate, f, silu, dact, w_down, w_gu_t, x, gnorm, scale1p), comm=comm)
    return (*outs, got)


def _weight_grad(a, b, seq, name, comm=None):
    tokens, n_out = a.shape
    tn = MXU_N

    def body(a_ref, b_ref, o_ref):
        o_ref[...] = _dot_tn(a_ref[...], b_ref[...]).astype(BF16)

    (out,), got = _call(
        body, name=name, grid=(n_out // tn,),
        out_shape=[jax.ShapeDtypeStruct((n_out, D_MODEL), BF16)],
        in_specs=[pl.BlockSpec((tokens, tn), lambda j: (0, j)), _resident((tokens, D_MODEL))],
        out_specs=[pl.BlockSpec((tn, D_MODEL), lambda j: (j, 0))],
        operands=(a, b), parallel=True, comm=comm)
    return out, got


def _final_loss(x, target, gnorm, seq):
    tokens = x.shape[0]
    tm = _tile_rows(seq)

    def body(x_ref, t_ref, g_ref, dx_ref, dg_ref, loss_ref):
        i = pl.program_id(0)
        xv = x_ref[...]
        r = _rms(xv)
        xn = xv * r
        gn = g_ref[...]
        err = xn * gn - t_ref[...]
        loss = (0.5 / D_MODEL) * jnp.sum(err * err, axis=0, keepdims=True)
        dyv = err * (1.0 / D_MODEL)
        dg = jnp.sum(dyv * xn, axis=0, keepdims=True)
        dxn = dyv * gn
        dx_ref[...] = r * (dxn - xn * jnp.mean(dxn * xn, axis=-1, keepdims=True))

        @pl.when(i == 0)
        def _():
            dg_ref[...] = dg
            loss_ref[...] = loss

        @pl.when(i != 0)
        def _():
            dg_ref[...] = dg_ref[...] + dg
            loss_ref[...] = loss_ref[...] + loss

    row = lambda i: (i, 0)
    fixed = pl.BlockSpec((1, D_MODEL), lambda i: (0, 0))
    outs, _ = _call(
        body, name="final_loss", grid=(tokens // tm,),
        out_shape=[jax.ShapeDtypeStruct((tokens, D_MODEL), F32), jax.ShapeDtypeStruct((1, D_MODEL), F32),
                   jax.ShapeDtypeStruct((1, D_MODEL), F32)],
        in_specs=[pl.BlockSpec((tm, D_MODEL), row), pl.BlockSpec((tm, D_MODEL), row), _resident((1, D_MODEL))],
        out_specs=[pl.BlockSpec((tm, D_MODEL), row), fixed, fixed],
        operands=(x, target, gnorm))
    return outs


def _group_mean(v, bd):
    hi = v.astype(BF16)
    lo = (v - hi.astype(F32)).astype(BF16)
    return _dot(hi, bd) + _dot(lo, bd)


def _sgu_forward(pm_ref, wm_ref, bias_ref, lng_ref, lnb_ref, bd_ref, mixed_scr, n_sub):
    ua = pm_ref[:, 0:D_A].astype(F32)
    va = pm_ref[:, D_A:2 * D_A].astype(F32)
    u_act = _gelu(ua)
    v_act = _gelu(va)
    bd = bd_ref[...]
    vc = v_act - _group_mean(v_act, bd)
    rstd = lax.rsqrt(_group_mean(vc * vc, bd) + EPS)
    vhat = vc * rstd
    vln = vhat * lng_ref[...] + lnb_ref[...]
    left = lax.broadcasted_iota(jnp.int32, (CHUNK, LANES), 1) < HEAD_DIM
    for q in range(n_sub):
        rows = slice(q * CHUNK, (q + 1) * CHUNK)
        for p in range(N_HEADS // 2):
            cols = slice(p * LANES, (p + 1) * LANES)
            vp = vln[rows, cols]
            v_l = jnp.where(left, vp, 0.0).astype(BF16)
            v_r = jnp.where(left, 0.0, vp).astype(BF16)
            mixed_scr[rows, cols] = _dot(wm_ref[2 * p], v_l) + _dot(wm_ref[2 * p + 1], v_r) + bias_ref[:, cols]
    return ua, va, u_act, vhat, rstd, vln


def _halo_specs(tm, tokens, width):
    prev = pl.BlockSpec((HALO, width), lambda i: (jnp.maximum(i * (tm // HALO) - 1, 0), 0))
    nxt = pl.BlockSpec((HALO, width), lambda i: (jnp.minimum((i + 1) * (tm // HALO), tokens // HALO - 1), 0))
    return prev, nxt


def _mixer_forward(proj, wm, bias_full, lng, lnb, convw, og, bd, seq, name):
    tokens = proj.shape[0]
    tm = _tile_rows(seq)
    per_seq = seq // tm
    n_sub = tm // CHUNK

    def body(pm_ref, pp_ref, wm_ref, bias_ref, lng_ref, lnb_ref, cw_ref, og_ref, bd_ref, y_ref, mixed_scr):
        i = pl.program_id(0)
        first = (i % per_seq) == 0
        _, _, u_act, _, _, _ = _sgu_forward(pm_ref, wm_ref, bias_ref, lng_ref, lnb_ref, bd_ref, mixed_scr, n_sub)
        ya = u_act * mixed_scr[...]
        y_ref[:, 0:D_A] = (ya * _rms(ya) * og_ref[:, 0:D_A]).astype(BF16)

        bg = pm_ref[:, 2 * D_A:3 * D_A].astype(F32)
        z = pm_ref[:, 3 * D_A:4 * D_A].astype(F32) * pm_ref[:, 4 * D_A:5 * D_A].astype(F32)
        zp = pp_ref[:, 3 * D_A:4 * D_A].astype(F32) * pp_ref[:, 4 * D_A:5 * D_A].astype(F32)
        zp = jnp.where(first, 0.0, zp)
        zext = jnp.concatenate([zp, z], axis=0)
        z1 = pltpu.roll(zext, 1, 0)[HALO:]
        z2 = pltpu.roll(zext, 2, 0)[HALO:]
        conv = cw_ref[0:1, :] * z2 + cw_ref[1:2, :] * z1 + cw_ref[2:3, :] * z
        yb = bg * conv
        y_ref[:, D_A:2 * D_A] = (yb * _rms(yb) * og_ref[:, D_A:2 * D_A]).astype(BF16)

    prev, _ = _halo_specs(tm, tokens, D_PROJ)
    (y,), _ = _call(
        body, name=name, grid=(tokens // tm,),
        out_shape=[jax.ShapeDtypeStruct((tokens, D_MODEL), BF16)],
        in_specs=[pl.BlockSpec((tm, D_PROJ), lambda i: (i, 0)), prev, _resident(wm.shape), _resident(bias_full.shape),
                  _resident(lng.shape), _resident(lnb.shape), _resident(convw.shape), _resident(og.shape),
                  _resident(bd.shape)],
        out_specs=[pl.BlockSpec((tm, D_MODEL), lambda i: (i, 0))],
        scratch_shapes=[pltpu.VMEM((tm, D_A), F32)],
        operands=(proj, proj, wm, bias_full, lng, lnb, convw, og, bd), parallel=True)
    return y


def _mixer_backward(proj, dy, wm, bias_full, lng, lnb, convw, og, bd, causal, seq, name, comm=None):
    tokens = proj.shape[0]
    tm = _tile_rows(seq)
    per_seq = seq // tm
    n_sub = tm // CHUNK
    ext = tm + 2 * HALO

    def body(pm_ref, pp_ref, pn_ref, dy_ref, dyn_ref, wm_ref, bias_ref, lng_ref, lnb_ref, cw_ref, og_ref, bd_ref,
             causal_ref, dp_ref, dog_ref, dcw_ref, dlng_ref, dlnb_ref, dbias_ref, dwm_ref, mixed_scr, dvln_scr):
        i = pl.program_id(0)
        first = (i % per_seq) == 0
        last = (i % per_seq) == per_seq - 1

        @pl.when(i == 0)
        def _():
            dog_ref[...] = jnp.zeros_like(dog_ref)
            dcw_ref[...] = jnp.zeros_like(dcw_ref)
            dlng_ref[...] = jnp.zeros_like(dlng_ref)
            dlnb_ref[...] = jnp.zeros_like(dlnb_ref)
            dbias_ref[...] = jnp.zeros_like(dbias_ref)
            dwm_ref[...] = jnp.zeros_like(dwm_ref)

        ua, va, u_act, vhat, rstd, vln = _sgu_forward(pm_ref, wm_ref, bias_ref, lng_ref, lnb_ref, bd_ref, mixed_scr, n_sub)
        mixed = mixed_scr[...]
        ya = u_act * mixed
        ra = _rms(ya)
        yhat = ya * ra
        dya_in = dy_ref[:, 0:D_A]
        dog_ref[:, 0:D_A] = dog_ref[:, 0:D_A] + jnp.sum(dya_in * yhat, axis=0, keepdims=True)
        dyh = dya_in * og_ref[:, 0:D_A]
        dya = ra * (dyh - yhat * jnp.mean(dyh * yhat, axis=-1, keepdims=True))
        d_u = dya * mixed
        d_mixed = dya * u_act
        left = lax.broadcasted_iota(jnp.int32, (CHUNK, LANES), 1) < HEAD_DIM
        dbias = jnp.zeros((CHUNK, D_A), F32)
        for q in range(n_sub):
            rows = slice(q * CHUNK, (q + 1) * CHUNK)
            dbias = dbias + d_mixed[rows, :]
            for p in range(N_HEADS // 2):
                cols = slice(p * LANES, (p + 1) * LANES)
                dm = d_mixed[rows, cols]
                dm_l = jnp.where(left, dm, 0.0).astype(BF16)
                dm_r = jnp.where(left, 0.0, dm).astype(BF16)
                vp = vln[rows, cols].astype(BF16)
                dwm_ref[2 * p] = dwm_ref[2 * p] + causal_ref[...] * _dot_nt(dm_l, vp)
                dwm_ref[2 * p + 1] = dwm_ref[2 * p + 1] + causal_ref[...] * _dot_nt(dm_r, vp)
                dvln_scr[rows, cols] = _dot_tn(wm_ref[2 * p], dm_l) + _dot_tn(wm_ref[2 * p + 1], dm_r)
        dbias_ref[...] = dbias_ref[...] + dbias
        dvln = dvln_scr[...]
        dlng_ref[...] = dlng_ref[...] + jnp.sum(dvln * vhat, axis=0, keepdims=True)
        dlnb_ref[...] = dlnb_ref[...] + jnp.sum(dvln, axis=0, keepdims=True)
        dvh = dvln * lng_ref[...]
        bd = bd_ref[...]
        d_v = rstd * (dvh - _group_mean(dvh, bd) - vhat * _group_mean(dvh * vhat, bd))
        dp_ref[:, 0:D_A] = (d_u * _gelu_grad(ua)).astype(BF16)
        dp_ref[:, D_A:2 * D_A] = (d_v * _gelu_grad(va)).astype(BF16)

        def ext_cols(lo):
            cs = slice(lo, lo + D_A)
            return jnp.concatenate([pp_ref[:, cs], pm_ref[:, cs], pn_ref[:, cs]], axis=0).astype(F32)

        bg, cg, xb = ext_cols(2 * D_A), ext_cols(3 * D_A), ext_cols(4 * D_A)
        row = lax.broadcasted_iota(jnp.int32, (ext, D_A), 0)
        z = jnp.where(jnp.logical_and(first, row < HALO), 0.0, cg * xb)
        z1 = pltpu.roll(z, 1, 0)
        z2 = pltpu.roll(z, 2, 0)
        w0, w1, w2 = cw_ref[0:1, :], cw_ref[1:2, :], cw_ref[2:3, :]
        conv = w0 * z2 + w1 * z1 + w2 * z
        yb = bg * conv
        rb = _rms(yb)
        yhb = yb * rb
        dyn = jnp.where(last, 0.0, dyn_ref[:, D_A:2 * D_A])
        dyb_in = jnp.concatenate([jnp.zeros((HALO, D_A), F32), dy_ref[:, D_A:2 * D_A], dyn], axis=0)
        dyhb = dyb_in * og_ref[:, D_A:2 * D_A]
        dyb = rb * (dyhb - yhb * jnp.mean(dyhb * yhb, axis=-1, keepdims=True))
        d_conv = dyb * bg
        dz = w2 * d_conv + w1 * pltpu.roll(d_conv, ext - 1, 0) + w0 * pltpu.roll(d_conv, ext - 2, 0)
        main = slice(HALO, HALO + tm)
        dp_ref[:, 2 * D_A:3 * D_A] = (dyb * conv)[main].astype(BF16)
        dp_ref[:, 3 * D_A:4 * D_A] = (dz * xb)[main].astype(BF16)
        dp_ref[:, 4 * D_A:5 * D_A] = (dz * cg)[main].astype(BF16)
        dog_ref[:, D_A:2 * D_A] = dog_ref[:, D_A:2 * D_A] + jnp.sum((dyb_in * yhb)[main], axis=0, keepdims=True)
        dcm = d_conv[main]
        dcw_ref[0:1, :] = dcw_ref[0:1, :] + jnp.sum(dcm * z2[main], axis=0, keepdims=True)
        dcw_ref[1:2, :] = dcw_ref[1:2, :] + jnp.sum(dcm * z1[main], axis=0, keepdims=True)
        dcw_ref[2:3, :] = dcw_ref[2:3, :] + jnp.sum(dcm * z[main], axis=0, keepdims=True)

    prev_p, next_p = _halo_specs(tm, tokens, D_PROJ)
    _, next_d = _halo_specs(tm, tokens, D_MODEL)
    fixed2 = lambda shape: pl.BlockSpec(shape, lambda i: (0, 0))
    outs, got = _call(
        body, name=name, grid=(tokens // tm,),
        out_shape=[jax.ShapeDtypeStruct((tokens, D_PROJ), BF16), jax.ShapeDtypeStruct((1, D_MODEL), F32),
                   jax.ShapeDtypeStruct((8, D_A), F32), jax.ShapeDtypeStruct((1, D_A), F32),
                   jax.ShapeDtypeStruct((1, D_A), F32), jax.ShapeDtypeStruct((CHUNK, D_A), F32),
                   jax.ShapeDtypeStruct((N_HEADS, CHUNK, CHUNK), F32)],
        in_specs=[pl.BlockSpec((tm, D_PROJ), lambda i: (i, 0)), prev_p, next_p,
                  pl.BlockSpec((tm, D_MODEL), lambda i: (i, 0)), next_d,
                  _resident(wm.shape), _resident(bias_full.shape), _resident(lng.shape), _resident(lnb.shape),
                  _resident(convw.shape), _resident(og.shape), _resident(bd.shape), _resident(causal.shape)],
        out_specs=[pl.BlockSpec((tm, D_PROJ), lambda i: (i, 0)), fixed2((1, D_MODEL)), fixed2((8, D_A)),
                   fixed2((1, D_A)), fixed2((1, D_A)), fixed2((CHUNK, D_A)),
                   pl.BlockSpec((N_HEADS, CHUNK, CHUNK), lambda i: (0, 0, 0))],
        scratch_shapes=[pltpu.VMEM((tm, D_A), F32), pltpu.VMEM((tm, D_A), F32)],
        operands=(proj, proj, proj, dy, dy, wm, bias_full, lng, lnb, convw, og, bd, causal), comm=comm)
    return (*outs, got)


def _sum_slots(recv, name):
    _, rows, cols = recv.shape
    tr = rows // 2

    def body(r_ref, o_ref):
        acc = r_ref[0].astype(F32)
        for d in range(1, N_DEV):
            acc = acc + r_ref[d].astype(F32)
        o_ref[...] = acc

    return pl.pallas_call(
        body, name=name, grid=(2,),
        out_shape=jax.ShapeDtypeStruct((rows, cols), F32),
        in_specs=[pl.BlockSpec((N_DEV, tr, cols), lambda i: (0, i, 0))],
        out_specs=pl.BlockSpec((tr, cols), lambda i: (i, 0)),
        compiler_params=_params(parallel=True),
    )(recv)


def _adamw(w, g, m, v, name):
    rows, cols = w.shape
    tr = max(t for t in range(8, 513, 8) if rows % t == 0)

    def body(w_ref, g_ref, m_ref, v_ref, d_ref, nm_ref, nv_ref):
        gv = g_ref[...]
        nm = ADAM_B1 * m_ref[...] + (1.0 - ADAM_B1) * gv
        nv = ADAM_B2 * v_ref[...] + (1.0 - ADAM_B2) * (gv * gv)
        m_hat = nm / (1.0 - ADAM_B1 ** ADAM_STEP)
        v_hat = nv / (1.0 - ADAM_B2 ** ADAM_STEP)
        d_ref[...] = -ADAM_LR * (m_hat / (jnp.sqrt(v_hat) + ADAM_EPS) + ADAM_WD * w_ref[...])
        nm_ref[...] = nm
        nv_ref[...] = nv

    spec = pl.BlockSpec((tr, cols), lambda i: (i, 0))
    return pl.pallas_call(
        body, name=name, grid=(rows // tr,),
        out_shape=[jax.ShapeDtypeStruct((rows, cols), F32)] * 3,
        in_specs=[spec] * 4, out_specs=[spec] * 3,
        compiler_params=_params(parallel=True),
    )(w, g, m, v)


def _adamw_nd(w, g, m, v, name):
    shape = w.shape
    two_d = (-1, shape[-1])
    d, nm, nv = _adamw(w.reshape(two_d), g.reshape(two_d), m.reshape(two_d), v.reshape(two_d), name)
    return d.reshape(shape), nm.reshape(shape), nv.reshape(shape)


def kernel(x, c, ada_w, ada_b, norm_ffn1_g, ffn1_w_gu, ffn1_w_down, norm_mix_g, mix_w_in, sgu_ln_g, sgu_ln_b, sgu_w_s, sgu_b, conv_w, out_norm_g, mix_w_out, norm_ffn2_g, ffn2_w_gu, ffn2_w_down, final_norm_g, loss_target, m_ada_w, m_ada_b, m_norm_ffn1_g, m_ffn1_w_gu, m_ffn1_w_down, m_norm_mix_g, m_mix_w_in, m_sgu_ln_g, m_sgu_ln_b, m_sgu_w_s, m_sgu_b, m_conv_w, m_out_norm_g, m_mix_w_out, m_norm_ffn2_g, m_ffn2_w_gu, m_ffn2_w_down, m_final_norm_g, v_ada_w, v_ada_b, v_norm_ffn1_g, v_ffn1_w_gu, v_ffn1_w_down, v_norm_mix_g, v_mix_w_in, v_sgu_ln_g, v_sgu_ln_b, v_sgu_w_s, v_sgu_b, v_conv_w, v_out_norm_g, v_mix_w_out, v_norm_ffn2_g, v_ffn2_w_gu, v_ffn2_w_down, v_final_norm_g):
    batch, seq, _ = x.shape
    tokens = batch * seq
    me = 4 * lax.axis_index("x") + 2 * lax.axis_index("y") + lax.axis_index("c")
    weights = dict(ada_w=ada_w, ada_b=ada_b, norm_ffn1_g=norm_ffn1_g, ffn1_w_gu=ffn1_w_gu, ffn1_w_down=ffn1_w_down,
                   norm_mix_g=norm_mix_g, mix_w_in=mix_w_in, sgu_ln_g=sgu_ln_g, sgu_ln_b=sgu_ln_b, sgu_w_s=sgu_w_s,
                   sgu_b=sgu_b, conv_w=conv_w, out_norm_g=out_norm_g, mix_w_out=mix_w_out, norm_ffn2_g=norm_ffn2_g,
                   ffn2_w_gu=ffn2_w_gu, ffn2_w_down=ffn2_w_down, final_norm_g=final_norm_g)
    mom1 = dict(ada_w=m_ada_w, ada_b=m_ada_b, norm_ffn1_g=m_norm_ffn1_g, ffn1_w_gu=m_ffn1_w_gu,
                ffn1_w_down=m_ffn1_w_down, norm_mix_g=m_norm_mix_g, mix_w_in=m_mix_w_in, sgu_ln_g=m_sgu_ln_g,
                sgu_ln_b=m_sgu_ln_b, sgu_w_s=m_sgu_w_s, sgu_b=m_sgu_b, conv_w=m_conv_w, out_norm_g=m_out_norm_g,
                mix_w_out=m_mix_w_out, norm_ffn2_g=m_norm_ffn2_g, ffn2_w_gu=m_ffn2_w_gu, ffn2_w_down=m_ffn2_w_down,
                final_norm_g=m_final_norm_g)
    mom2 = dict(ada_w=v_ada_w, ada_b=v_ada_b, norm_ffn1_g=v_norm_ffn1_g, ffn1_w_gu=v_ffn1_w_gu,
                ffn1_w_down=v_ffn1_w_down, norm_mix_g=v_norm_mix_g, mix_w_in=v_mix_w_in, sgu_ln_g=v_sgu_ln_g,
                sgu_ln_b=v_sgu_ln_b, sgu_w_s=v_sgu_w_s, sgu_b=v_sgu_b, conv_w=v_conv_w, out_norm_g=v_out_norm_g,
                mix_w_out=v_mix_w_out, norm_ffn2_g=v_norm_ffn2_g, ffn2_w_gu=v_ffn2_w_gu, ffn2_w_down=v_ffn2_w_down,
                final_norm_g=v_final_norm_g)

    big = ("ffn1_w_gu", "ffn1_w_down", "mix_w_in", "mix_w_out", "ffn2_w_gu", "ffn2_w_down")
    transposed = ("ffn1_w_gu", "mix_w_in", "ffn2_w_gu")
    shard = {(l, nm): (weights[nm][l].T if nm in transposed else weights[nm][l]).astype(BF16)
             for l in range(DEPTH) for nm in big}
    full_w = {}

    def gather_of(keys):
        return keys, _GatherRows([shard[k] for k in keys])

    def landed(plan, got):
        full_w.update(zip(plan[0], got))

    plan = gather_of([(0, "ffn1_w_gu")])
    landed(plan, _comm_only(plan[1], "gather_first"))

    small_in = _pack_small([c, conv_w])
    small_all, _ = _all_gather_small(small_in, "gather_c")
    c_all, convw_all = _unpack_small(small_all, [c.shape, conv_w.shape], lead=(N_DEV,))
    c_all = c_all.reshape(N_DEV * batch, D_MODEL)
    convw_full = jnp.transpose(convw_all, (1, 2, 0, 3)).reshape(DEPTH, 3, D_A)
    ada_cols = ada_w.shape[2]
    ada_b_cols = lax.dynamic_slice_in_dim(ada_b, me * ada_cols, ada_cols, axis=1).reshape(DEPTH, 1, ada_cols)
    c_rows = jnp.pad(c_all.reshape(N_DEV, batch, D_MODEL), ((0, 0), (0, 8 - batch), (0, 0))).reshape(N_DEV * 8, D_MODEL)
    ada_local = _ada_forward(c_rows, ada_w, ada_b_cols)
    ada_recv = _exchange_batch_rows(ada_local, "exchange_ada")
    ada_mine = jnp.transpose(ada_recv[:, :, :batch, :], (1, 2, 0, 3)).reshape(DEPTH, batch, N_MOD * D_MODEL)
    mod = ada_mine.reshape(DEPTH, batch, N_MOD, 1, D_MODEL)

    causal = jnp.tril(jnp.ones((CHUNK, CHUNK), F32))
    bd = jnp.kron(jnp.eye(N_HEADS, dtype=F32), jnp.full((HEAD_DIM, HEAD_DIM), 1.0 / HEAD_DIM, F32)).astype(BF16)
    row_vec = lambda a: a.reshape(1, -1)

    hosted_gathers = {
        (0, "ffn1"): [(0, "ffn1_w_down"), (0, "mix_w_in"), (0, "mix_w_out")],
        (0, "ffn_down1"): [(0, "ffn2_w_gu")],
        (0, "mix_in"): [(0, "ffn2_w_down")],
        (0, "ffn2"): [(1, "ffn1_w_gu"), (1, "ffn1_w_down"), (1, "mix_w_in"), (1, "mix_w_out")],
        (1, "ffn1"): [(1, "ffn2_w_gu"), (1, "ffn2_w_down")],
    }

    def hosting(l, site):
        keys = hosted_gathers.get((l, site))
        return gather_of(keys) if keys else (None, None)

    xs = x.reshape(tokens, D_MODEL)
    saved = []
    for l in range(DEPTH):
        sh1, sc1, g1, sh2, sc2, g2, sh3, sc3, g3 = [mod[l, :, k] for k in range(N_MOD)]
        mixer_consts = dict(
            wm=(sgu_w_s[l] * causal[None]).astype(BF16),
            bias_full=jnp.repeat(sgu_b[l].T, HEAD_DIM, axis=1),
            lng=row_vec(jnp.tile(sgu_ln_g[l], N_HEADS)), lnb=row_vec(jnp.tile(sgu_ln_b[l], N_HEADS)),
            convw=jnp.pad(convw_full[l], ((0, 5), (0, 0))), og=row_vec(out_norm_g[l]), bd=bd)
        x0 = xs
        plan = hosting(l, "ffn1")
        if l == 0:
            h1, a1, s1, w1, got = _normmod_matmul(x0, row_vec(norm_ffn1_g[l]), 1.0 + sc1, sh1, full_w[l, "ffn1_w_gu"], seq, "ffn_up", True, plan[1])
            landed(plan, got)
            plan = hosting(l, "ffn_down1")
            x1, f1, got = _matmul_residual(a1, full_w[l, "ffn1_w_down"], x0, g1, 0.5, seq, "ffn_down", plan[1])
        else:
            h1, a1, s1, w1, x1, f1, got = _ffn_forward(
                x0, row_vec(norm_ffn1_g[l]), 1.0 + sc1, sh1, full_w[l, "ffn1_w_gu"], full_w[l, "ffn1_w_down"], g1, 0.5, seq, "ffn_fwd", plan[1])
        if got:
            landed(plan, got)
        plan = hosting(l, "mix_in")
        h2, proj, got = _normmod_matmul(x1, row_vec(norm_mix_g[l]), 1.0 + sc2, sh2, full_w[l, "mix_w_in"], seq, "mix_in", False, plan[1])
        if got:
            landed(plan, got)
        ymix = _mixer_forward(proj, seq=seq, name="mixer_forward", **mixer_consts)
        x2, o2, _ = _matmul_residual(ymix, full_w[l, "mix_w_out"], x1, g2, 1.0, seq, "mix_out")
        plan = hosting(l, "ffn2")
        h3, a3, s3, w3, x3, f3, got = _ffn_forward(
            x2, row_vec(norm_ffn2_g[l]), 1.0 + sc3, sh3, full_w[l, "ffn2_w_gu"], full_w[l, "ffn2_w_down"], g3, 0.5, seq, "ffn_fwd", plan[1])
        if got:
            landed(plan, got)
        saved.append(dict(x0=x0, x1=x1, x2=x2, h1=h1, h2=h2, h3=h3, a1=a1, s1=s1, w1=w1, a3=a3, s3=s3, w3=w3, f1=f1, f3=f3, o2=o2, proj=proj,
                          ymix=ymix, mixer_consts=mixer_consts, sc=(1.0 + sc1, 1.0 + sc2, 1.0 + sc3), gates=(g1, g2, g3)))
        xs = x3

    dx, d_final_g, loss_cols = _final_loss(xs, loss_target.reshape(tokens, D_MODEL), row_vec(final_norm_g), seq)

    recv = {}
    small_grads = [None] * DEPTH
    d_mod = [None] * DEPTH

    mix_names = ("out_norm_g", "sgu_ln_g", "sgu_ln_b", "sgu_w_s", "sgu_b", "conv_w")
    late_names = ("norm_ffn1_g", "norm_mix_g", "norm_ffn2_g")

    def mix_parts(l):
        return [small_grads[l][nm] for nm in mix_names]

    def late_parts(l):
        return [small_grads[l][nm] for nm in late_names] + [d_mod[l]]

    pending = []

    def scatter_later(l, nm, grad):
        pending.append(((l, nm), _ScatterRows([grad])))

    def host():
        keys, parts = [k for k, _ in pending], [p for _, p in pending]
        pending.clear()
        return keys, (_Exchanges(parts) if parts else None)

    def hosted(keys, got):
        if got:
            recv.update(zip(keys, got))

    for l in reversed(range(DEPTH)):
        sv = saved[l]
        mc = sv["mixer_consts"]
        if l + 1 < DEPTH:
            pending.append((("late", l + 1), _GatherRows([_pack_small(late_parts(l + 1))])))
        keys, comm = host()
        df3, dg3, dgu3, dx2, dsh3, dsc3, dn3, got = _ffn_backward(
            dx, sv["gates"][2], sv["f3"], sv["s3"], sv["w3"], full_w[l, "ffn2_w_down"], full_w[l, "ffn2_w_gu"], sv["x2"],
            row_vec(norm_ffn2_g[l]), sv["sc"][2], 0.5, seq, "ffn_bwd", comm)
        hosted(keys, got)
        gw_down2, _ = _weight_grad(sv["a3"], df3, seq, "grad_w_down")
        scatter_later(l, "ffn2_w_down", gw_down2)
        keys, comm = host()
        gw_gu2, got = _weight_grad(dgu3, sv["h3"], seq, "grad_w_gu", comm)
        hosted(keys, got)
        scatter_later(l, "ffn2_w_gu", gw_gu2)
        do2, dg2, dymix, _ = _residual_backward(dx2, sv["gates"][1], sv["o2"], full_w[l, "mix_w_out"], 1.0, seq, "mix_out_bwd")
        gw_out, _ = _weight_grad(sv["ymix"], do2, seq, "grad_w_out")
        keys, comm = host()
        dproj, d_og, d_cw, d_lng, d_lnb, d_bias, d_wm, got = _mixer_backward(
            sv["proj"], dymix, causal=causal, seq=seq, name="mixer_backward", comm=comm, **mc)
        hosted(keys, got)
        small_grads[l] = dict(
            out_norm_g=d_og, sgu_ln_g=d_lng.reshape(N_HEADS, HEAD_DIM).sum(0), sgu_ln_b=d_lnb.reshape(N_HEADS, HEAD_DIM).sum(0),
            sgu_w_s=d_wm, sgu_b=d_bias.reshape(CHUNK, N_HEADS, HEAD_DIM).sum(-1).T, conv_w=d_cw[0:3])
        pending.append((("mix", l), _GatherRows([_pack_small(mix_parts(l))])))
        scatter_later(l, "mix_w_out", gw_out)
        keys, comm = host()
        dx1, dsh2, dsc2, dn2, got = _matmul_normmod_backward(dproj, full_w[l, "mix_w_in"], sv["x1"], dx2, row_vec(norm_mix_g[l]), sv["sc"][1], seq, "mix_in_bwd", comm)
        hosted(keys, got)
        gw_in, _ = _weight_grad(dproj, sv["h2"], seq, "grad_w_in")
        scatter_later(l, "mix_w_in", gw_in)
        keys, comm = host()
        if l > 0:
            df1, dg1, dgu1, dx0, dsh1, dsc1, dn1, got = _ffn_backward(
                dx1, sv["gates"][0], sv["f1"], sv["s1"], sv["w1"], full_w[l, "ffn1_w_down"], full_w[l, "ffn1_w_gu"], sv["x0"],
                row_vec(norm_ffn1_g[l]), sv["sc"][0], 0.5, seq, "ffn_bwd", comm)
        else:
            df1, dg1, dgu1, got = _residual_backward(dx1, sv["gates"][0], sv["f1"], full_w[l, "ffn1_w_down"], 0.5, seq, "ffn_down_bwd", sv["s1"], sv["w1"], comm)
        hosted(keys, got)
        gw_down1, _ = _weight_grad(sv["a1"], df1, seq, "grad_w_down")
        scatter_later(l, "ffn1_w_down", gw_down1)
        keys, comm = host()
        gw_gu1, got = _weight_grad(dgu1, sv["h1"], seq, "grad_w_gu", comm)
        hosted(keys, got)
        scatter_later(l, "ffn1_w_gu", gw_gu1)
        if l == 0:
            keys, comm = host()
            dx0, dsh1, dsc1, dn1, got = _matmul_normmod_backward(dgu1, full_w[l, "ffn1_w_gu"], sv["x0"], dx1, row_vec(norm_ffn1_g[l]), sv["sc"][0], seq, "ffn_up_bwd", comm)
            hosted(keys, got)
        dx = dx0
        small_grads[l].update(norm_ffn1_g=dn1, norm_mix_g=dn2, norm_ffn2_g=dn3)
        d_mod[l] = jnp.concatenate([dsh1, dsc1, dg1, dsh2, dsc2, dg2, dsh3, dsc3, dg3], axis=1)
    grad_x = dx.reshape(batch, seq, D_MODEL)

    grad_big = {}
    for nm in big:
        per_layer = []
        for l in range(DEPTH):
            g_sum = _sum_slots(recv[l, nm], "sum_" + nm)
            per_layer.append(g_sum.T if nm in transposed else g_sum)
        grad_big[nm] = jnp.stack(per_layer)

    last_parts = late_parts(0) + [d_final_g, loss_cols]
    last_shapes = [p.shape for p in last_parts]
    packed_all, packed_sum = _all_gather_small(_pack_small(last_parts), "reduce_small")
    late_sum = {0: _unpack_small(packed_sum, last_shapes)}
    d_mod_dev = {0: _unpack_small(packed_all, last_shapes, lead=(N_DEV,))[len(late_names)]}
    mix_sum = {}
    for l in range(DEPTH):
        gathered = recv["mix", l].reshape(N_DEV, -1, LANES)
        mix_sum[l] = _unpack_small(_sum_gathered(gathered, "sum_mix"), [p.shape for p in mix_parts(l)])
        if l > 0:
            shapes_l = [p.shape for p in late_parts(l)]
            gathered = recv["late", l].reshape(N_DEV, -1, LANES)
            late_sum[l] = _unpack_small(_sum_gathered(gathered, "sum_late"), shapes_l)
            d_mod_dev[l] = _unpack_small(gathered, shapes_l, lead=(N_DEV,))[len(late_names)]
    grad_small = {}
    for group, names in ((mix_sum, mix_names), (late_sum, late_names)):
        for k, nm in enumerate(names):
            grad_small[nm] = jnp.stack([group[l][k] for l in range(DEPTH)]).reshape(
                (DEPTH, 3, D_A) if nm == "conv_w" else weights[nm].shape)
    grad_small["conv_w"] = lax.dynamic_slice_in_dim(grad_small["conv_w"], me * conv_w.shape[2], conv_w.shape[2], axis=2)
    grad_small["final_norm_g"] = late_sum[0][len(late_names) + 1].reshape(final_norm_g.shape)
    loss = jnp.sum(late_sum[0][len(late_names) + 2])
    d_ada_all = jnp.stack([d_mod_dev[l] for l in range(DEPTH)]).reshape(DEPTH, N_DEV * batch, N_MOD * D_MODEL)
    d_ada_cols = lax.dynamic_slice_in_dim(d_ada_all, me * ada_cols, ada_cols, axis=2)
    g_ada_w, g_ada_b = _ada_backward(c_all, d_ada_cols, d_ada_all)

    grads = dict(grad_big)
    grads.update(grad_small)
    grads["ada_w"] = g_ada_w
    grads["ada_b"] = g_ada_b.reshape(ada_b.shape)

    names = ("ada_w", "ada_b", "norm_ffn1_g", "ffn1_w_gu", "ffn1_w_down", "norm_mix_g", "mix_w_in", "sgu_ln_g",
             "sgu_ln_b", "sgu_w_s", "sgu_b", "conv_w", "out_norm_g", "mix_w_out", "norm_ffn2_g", "ffn2_w_gu",
             "ffn2_w_down", "final_norm_g")
    large = ("ada_w",) + big
    delta, new_m, new_v = {}, {}, {}
    for nm in large:
        delta[nm], new_m[nm], new_v[nm] = _adamw_nd(weights[nm], grads[nm], mom1[nm], mom2[nm], "adamw_" + nm)
    rest = [nm for nm in names if nm not in large]
    pack = lambda src: _pack_small([src[nm] for nm in rest])
    d_p, m_p, v_p = _adamw(pack(weights), pack(grads), pack(mom1), pack(mom2), "adamw_small")
    rest_shapes = [weights[nm].shape for nm in rest]
    for nm, d_k, m_k, v_k in zip(rest, _unpack_small(d_p, rest_shapes), _unpack_small(m_p, rest_shapes), _unpack_small(v_p, rest_shapes)):
        delta[nm], new_m[nm], new_v[nm] = d_k, m_k, v_k

    return (loss, grad_x, *[grads[nm] for nm in names], *[delta[nm] for nm in names],
            *[new_m[nm] for nm in names], *[new_v[nm] for nm in names])
```

```python
import math

import jax
import jax.numpy as jnp
from jax import lax
from jax.experimental import pallas as pl
from jax.experimental.pallas import tpu as pltpu

F32 = jnp.float32
BF16 = jnp.bfloat16

D_MODEL = 1024
D_FF = 2816
D_A = 512
D_PROJ = 2560
N_HEADS = 8
HEAD_DIM = 64
CHUNK = 128
N_MOD = 9
DEPTH = 2
EPS = 1e-6
N_DEV = 8
LANES = 128
MXU_N = 256
HALO = 16
VMEM_LIMIT = 56 * 1024 * 1024
FORWARD_STEPS = 4

ADAM_LR = 0.001
ADAM_B1 = 0.9
ADAM_B2 = 0.999
ADAM_EPS = 1e-08
ADAM_WD = 0.01
ADAM_STEP = 10

MESH = pl.DeviceIdType.MESH


def _dot(a, b):
    return jnp.dot(a, b, preferred_element_type=F32)


def _dot_nt(a, b):
    return lax.dot_general(a, b, (((1,), (1,)), ((), ())), preferred_element_type=F32)


def _dot_tn(a, b):
    return lax.dot_general(a, b, (((0,), (0,)), ((), ())), preferred_element_type=F32)


def _sigmoid(x):
    return 0.5 * jnp.tanh(0.5 * x) + 0.5


def _gelu(x):
    return 0.5 * x * (1.0 + lax.erf(x * (1.0 / math.sqrt(2.0))))


def _gelu_grad(x):
    cdf = 0.5 * (1.0 + lax.erf(x * (1.0 / math.sqrt(2.0))))
    return cdf + x * jnp.exp(-0.5 * x * x) * (1.0 / math.sqrt(2.0 * math.pi))


def _params(n_axes=1, parallel=False):
    sem = ("parallel" if parallel else "arbitrary",) * n_axes
    return pltpu.CompilerParams(dimension_semantics=sem, vmem_limit_bytes=VMEM_LIMIT)


def _resident(shape):
    nd = len(shape)
    return pl.BlockSpec(shape, lambda *_: (0,) * nd, pipeline_mode=pl.Buffered(1))


def _tile_rows(seq):
    return min(512, seq)


def _my_position():
    x, y, c = lax.axis_index("x"), lax.axis_index("y"), lax.axis_index("c")
    return x, y, c, 4 * x + 2 * y + c


def _peer(x, y, c, p):
    return (x ^ ((p >> 2) & 1), y ^ ((p >> 1) & 1), c ^ (p & 1))


class _GatherRows:
    def __init__(self, shards):
        self.operands = list(shards)
        n = len(shards)
        self.out_shape = [jax.ShapeDtypeStruct((N_DEV * s.shape[0], s.shape[1]), s.dtype) for s in shards]
        self.scratch = [pltpu.SemaphoreType.DMA((n, N_DEV - 1)), pltpu.SemaphoreType.DMA((n, N_DEV - 1)),
                        pltpu.SemaphoreType.DMA((n,))]

    def _plan(self, src, dst, send, recv, loc):
        x, y, c, _ = _my_position()
        me, sib = (x, y, c), (x, y, 1 - c)
        chips = [(1 - x, y), (x, 1 - y), (1 - x, 1 - y)]
        plans = []
        for k, shard in enumerate(self.operands):
            rows = shard.shape[0]

            def blk(pos, k=k, rows=rows):
                return dst[k].at[pl.ds((4 * pos[0] + 2 * pos[1] + pos[2]) * rows, rows), :]

            def rc(s, block, to, source=None, k=k, blk=blk):
                return pltpu.make_async_remote_copy(
                    src_ref=blk(block) if source is None else source, dst_ref=blk(block),
                    send_sem=send.at[k, s], recv_sem=recv.at[k, s], device_id=to, device_id_type=MESH)

            plans.append(dict(
                local=pltpu.make_async_copy(src[k], blk(me), loc.at[k]),
                first=[rc(0, me, sib, src[k])] + [rc(1 + j, me, (*chip, c), src[k]) for j, chip in enumerate(chips)],
                landed=[rc(1 + j, (*chip, c), me) for j, chip in enumerate(chips)],
                passed=[rc(4 + j, (*chip, c), sib) for j, chip in enumerate(chips)],
                from_sib=[rc(0, sib, me)] + [rc(4 + j, (*chip, 1 - c), me) for j, chip in enumerate(chips)]))
        return plans

    def start(self, src, dst, send, recv, loc):
        for plan in self._plan(src, dst, send, recv, loc):
            plan["local"].start()
            for cp in plan["first"]:
                cp.start()

    def forward(self, src, dst, send, recv, loc):
        for plan in self._plan(src, dst, send, recv, loc):
            for landed, passed in zip(plan["landed"], plan["passed"]):
                landed.wait_recv()
                passed.start()

    def finish(self, src, dst, send, recv, loc):
        for plan in self._plan(src, dst, send, recv, loc):
            for cp in plan["from_sib"]:
                cp.wait_recv()
            for cp in plan["first"] + plan["passed"]:
                cp.wait_send()
            plan["local"].wait()


class _ScatterRows:
    def __init__(self, grads):
        self.operands = list(grads)
        n = len(grads)
        self.out_shape = [jax.ShapeDtypeStruct((N_DEV, g.shape[0] // N_DEV, g.shape[1]), g.dtype) for g in grads]
        self.scratch = [pltpu.SemaphoreType.DMA((n, N_DEV - 1)), pltpu.SemaphoreType.DMA((n, N_DEV - 1)),
                        pltpu.SemaphoreType.DMA((n,))]

    def _plan(self, src, dst, send, recv, loc):
        x, y, c, me = _my_position()
        copies = []
        for k, grad in enumerate(self.operands):
            rows = grad.shape[0] // N_DEV
            copies.append(pltpu.make_async_copy(src[k].at[pl.ds(me * rows, rows), :], dst[k].at[me], loc.at[k]))
            for p in range(1, N_DEV):
                px, py, pc = _peer(x, y, c, p)
                copies.append(pltpu.make_async_remote_copy(
                    src_ref=src[k].at[pl.ds((4 * px + 2 * py + pc) * rows, rows), :], dst_ref=dst[k].at[me],
                    send_sem=send.at[k, p - 1], recv_sem=recv.at[k, p - 1], device_id=(px, py, pc), device_id_type=MESH))
        return copies

    def start(self, src, dst, send, recv, loc):
        for cp in self._plan(src, dst, send, recv, loc):
            cp.start()

    def forward(self, src, dst, send, recv, loc):
        pass

    def finish(self, src, dst, send, recv, loc):
        for cp in self._plan(src, dst, send, recv, loc):
            cp.wait()


class _Exchanges:
    def __init__(self, parts):
        self.parts = list(parts)
        self.operands = [op for part in self.parts for op in part.operands]
        self.out_shape = [shp for part in self.parts for shp in part.out_shape]
        self.scratch = [scr for part in self.parts for scr in part.scratch]

    def _each(self, src, dst, sems):
        at, sem_at = 0, 0
        for part in self.parts:
            n, n_sem = len(part.operands), len(part.scratch)
            yield part, src[at:at + n], dst[at:at + n], sems[sem_at:sem_at + n_sem]
            at, sem_at = at + n, sem_at + n_sem

    def start(self, src, dst, *sems):
        for part, part_src, part_dst, part_sems in self._each(src, dst, sems):
            part.start(part_src, part_dst, *part_sems)

    def forward(self, src, dst, *sems):
        for part, part_src, part_dst, part_sems in self._each(src, dst, sems):
            part.forward(part_src, part_dst, *part_sems)

    def finish(self, src, dst, *sems):
        for part, part_src, part_dst, part_sems in self._each(src, dst, sems):
            part.finish(part_src, part_dst, *part_sems)


_ANY = pl.BlockSpec(memory_space=pl.ANY)


def _comm_only(comm, name):
    n = len(comm.operands)

    def body(*refs):
        src, dst, sems = refs[:n], refs[n:2 * n], refs[2 * n:]
        comm.start(src, dst, *sems)
        comm.forward(src, dst, *sems)
        comm.finish(src, dst, *sems)

    return pl.pallas_call(body, name=name, out_shape=comm.out_shape, in_specs=[_ANY] * n, out_specs=[_ANY] * n,
                          scratch_shapes=comm.scratch)(*comm.operands)


def _call(body, *, name, grid, in_specs, out_specs, out_shape, operands, scratch_shapes=(), parallel=False, comm=None):
    n_axes = len(grid)
    if comm is None:
        outs = pl.pallas_call(body, name=name, grid=grid, out_shape=list(out_shape), in_specs=list(in_specs),
                              out_specs=list(out_specs), scratch_shapes=list(scratch_shapes),
                              compiler_params=_params(n_axes, parallel))(*operands)
        return list(outs), None
    n_in, n_out, n_scr, n_c = len(in_specs), len(out_specs), len(scratch_shapes), len(comm.operands)
    total = math.prod(grid)

    def hosted(*refs):
        ins, c_src = refs[:n_in], refs[n_in:n_in + n_c]
        outs, c_dst = refs[n_in + n_c:n_in + n_c + n_out], refs[n_in + n_c + n_out:n_in + 2 * n_c + n_out]
        scr, sems = refs[n_in + 2 * n_c + n_out:n_in + 2 * n_c + n_out + n_scr], refs[n_in + 2 * n_c + n_out + n_scr:]
        step = pl.program_id(0)
        for axis in range(1, n_axes):
            step = step * grid[axis] + pl.program_id(axis)

        @pl.when(step == 0)
        def _():
            comm.start(c_src, c_dst, *sems)

        @pl.when(step == max(total - FORWARD_STEPS, 0))
        def _():
            comm.forward(c_src, c_dst, *sems)

        body(*ins, *outs, *scr)

        @pl.when(step == total - 1)
        def _():
            comm.finish(c_src, c_dst, *sems)

    res = pl.pallas_call(hosted, name=name, grid=grid, out_shape=list(out_shape) + comm.out_shape,
                         in_specs=list(in_specs) + [_ANY] * n_c, out_specs=list(out_specs) + [_ANY] * n_c,
                         scratch_shapes=list(scratch_shapes) + comm.scratch,
                         compiler_params=_params(n_axes, False))(*operands, *comm.operands)
    return list(res[:n_out]), list(res[n_out:])


def _all_gather_small(v, name):
    rows = v.shape[0]

    def body(v_ref, all_ref, sum_ref, send_sems, recv_sems):
        x, y, c, me = _my_position()
        all_ref[me] = v_ref[...]
        copies = []
        for p in range(1, N_DEV):
            cp = pltpu.make_async_remote_copy(
                src_ref=v_ref, dst_ref=all_ref.at[me], send_sem=send_sems.at[p - 1], recv_sem=recv_sems.at[p - 1],
                device_id=_peer(x, y, c, p), device_id_type=MESH)
            cp.start()
            copies.append(cp)
        for cp in copies:
            cp.wait()
        acc = all_ref[0]
        for d in range(1, N_DEV):
            acc = acc + all_ref[d]
        sum_ref[...] = acc

    return pl.pallas_call(
        body, name=name,
        out_shape=[jax.ShapeDtypeStruct((N_DEV, rows, LANES), F32), jax.ShapeDtypeStruct((rows, LANES), F32)],
        in_specs=[pl.BlockSpec(memory_space=pltpu.VMEM)],
        out_specs=[pl.BlockSpec(memory_space=pltpu.VMEM)] * 2,
        scratch_shapes=[pltpu.SemaphoreType.DMA((N_DEV - 1,)), pltpu.SemaphoreType.DMA((N_DEV - 1,))],
        compiler_params=pltpu.CompilerParams(vmem_limit_bytes=VMEM_LIMIT),
    )(v)


def _exchange_batch_rows(v, name):
    depth, _, cols = v.shape
    sub = 8

    def body(v_ref, o_ref, send_sems, recv_sems):
        x, y, c, me = _my_position()
        o_ref[me] = v_ref[:, pl.ds(pl.multiple_of(me * sub, sub), sub), :]
        copies = []
        for p in range(1, N_DEV):
            px, py, pc = _peer(x, y, c, p)
            rows = pl.ds(pl.multiple_of((4 * px + 2 * py + pc) * sub, sub), sub)
            cp = pltpu.make_async_remote_copy(
                src_ref=v_ref.at[:, rows, :], dst_ref=o_ref.at[me], send_sem=send_sems.at[p - 1],
                recv_sem=recv_sems.at[p - 1], device_id=(px, py, pc), device_id_type=MESH)
            cp.start()
            copies.append(cp)
        for cp in copies:
            cp.wait()

    return pl.pallas_call(
        body, name=name, out_shape=jax.ShapeDtypeStruct((N_DEV, depth, sub, cols), F32),
        in_specs=[pl.BlockSpec(memory_space=pltpu.VMEM)], out_specs=pl.BlockSpec(memory_space=pltpu.VMEM),
        scratch_shapes=[pltpu.SemaphoreType.DMA((N_DEV - 1,)), pltpu.SemaphoreType.DMA((N_DEV - 1,))],
        compiler_params=pltpu.CompilerParams(vmem_limit_bytes=VMEM_LIMIT),
    )(v)


def _sum_gathered(gathered, name):
    rows = gathered.shape[1]

    def body(g_ref, o_ref):
        acc = g_ref[0]
        for d in range(1, N_DEV):
            acc = acc + g_ref[d]
        o_ref[...] = acc

    return pl.pallas_call(
        body, name=name, out_shape=jax.ShapeDtypeStruct((rows, LANES), F32),
        in_specs=[pl.BlockSpec(memory_space=pltpu.VMEM)], out_specs=pl.BlockSpec(memory_space=pltpu.VMEM),
        compiler_params=pltpu.CompilerParams(vmem_limit_bytes=VMEM_LIMIT),
    )(gathered)


def _pack_small(parts):
    flat = jnp.concatenate([p.reshape(-1).astype(F32) for p in parts])
    total = flat.shape[0]
    padded = -(-total // (8 * LANES)) * (8 * LANES)
    flat = jnp.pad(flat, (0, padded - total))
    return flat.reshape(padded // LANES, LANES)


def _unpack_small(packed, shapes, lead=()):
    flat = packed.reshape(lead + (-1,))
    out, off = [], 0
    for shp in shapes:
        size = math.prod(shp)
        out.append(flat[..., off:off + size].reshape(lead + tuple(shp)))
        off += size
    return out


def _ada_forward(c_all, ada_w, ada_b_cols):
    nb = c_all.shape[0]
    cols = ada_w.shape[2]

    def body(c_ref, w_ref, b_ref, o_ref):
        cv = c_ref[...]
        act = (cv * _sigmoid(cv)).astype(BF16)
        o_ref[0] = _dot(act, w_ref[0].astype(BF16)) + b_ref[0]

    return pl.pallas_call(
        body, name="ada_forward", grid=(DEPTH,),
        out_shape=jax.ShapeDtypeStruct((DEPTH, nb, cols), F32),
        in_specs=[pl.BlockSpec((nb, D_MODEL), lambda l: (0, 0)),
                  pl.BlockSpec((1, D_MODEL, cols), lambda l: (l, 0, 0)),
                  pl.BlockSpec((1, 1, cols), lambda l: (l, 0, 0))],
        out_specs=pl.BlockSpec((1, nb, cols), lambda l: (l, 0, 0)),
        compiler_params=_params(),
    )(c_all, ada_w, ada_b_cols)


def _ada_backward(c_all, d_ada_cols, d_ada_all):
    nb = c_all.shape[0]
    cols = d_ada_cols.shape[2]
    full = d_ada_all.shape[2]

    def body(c_ref, dc_ref, da_ref, gw_ref, gb_ref):
        cv = c_ref[...]
        act = (cv * _sigmoid(cv)).astype(BF16)
        gw_ref[0] = _dot_tn(act, dc_ref[0].astype(BF16))
        gb_ref[0] = jnp.sum(da_ref[0], axis=0, keepdims=True)

    return pl.pallas_call(
        body, name="ada_backward", grid=(DEPTH,),
        out_shape=[jax.ShapeDtypeStruct((DEPTH, D_MODEL, cols), F32), jax.ShapeDtypeStruct((DEPTH, 1, full), F32)],
        in_specs=[pl.BlockSpec((nb, D_MODEL), lambda l: (0, 0)),
                  pl.BlockSpec((1, nb, cols), lambda l: (l, 0, 0)),
                  pl.BlockSpec((1, nb, full), lambda l: (l, 0, 0))],
        out_specs=[pl.BlockSpec((1, D_MODEL, cols), lambda l: (l, 0, 0)),
                   pl.BlockSpec((1, 1, full), lambda l: (l, 0, 0))],
        compiler_params=_params(),
    )(c_all, d_ada_cols, d_ada_all)


def _rms(xv):
    return lax.rsqrt(jnp.mean(xv * xv, axis=-1, keepdims=True) + EPS)


def _normmod_matmul(x, gnorm, scale1p, shift, w_t, seq, name, swiglu=False, comm=None):
    tokens, n_out = x.shape[0], w_t.shape[0]
    tm = _tile_rows(seq)
    per_seq = seq // tm
    width = n_out // 2 if swiglu else n_out
    n_chunks = width // MXU_N

    def body(x_ref, g_ref, sc_ref, sh_ref, w_ref, h_ref, *o_refs):
        xv = x_ref[...]
        h = (xv * _rms(xv) * g_ref[...]) * sc_ref[0] + sh_ref[0]
        h_ref[...] = h.astype(BF16)
        for ck in range(n_chunks):
            cs = slice(ck * MXU_N, (ck + 1) * MXU_N)
            if swiglu:
                act_ref, silu_ref, dact_ref = o_refs
                g = _dot_nt(h_ref[...], w_ref[cs, :])
                u = _dot_nt(h_ref[...], w_ref[width + ck * MXU_N:width + (ck + 1) * MXU_N, :])
                sig = _sigmoid(g)
                silu = g * sig
                act_ref[:, cs] = (silu * u).astype(BF16)
                silu_ref[:, cs] = silu.astype(BF16)
                dact_ref[:, cs] = (u * (sig + silu * (1.0 - sig))).astype(BF16)
            else:
                o_refs[0][:, cs] = _dot_nt(h_ref[...], w_ref[cs, :]).astype(BF16)

    n_res = 3 if swiglu else 1
    per_batch = pl.BlockSpec((1, 1, D_MODEL), lambda i: (i // per_seq, 0, 0))
    outs, got = _call(
        body, name=name, grid=(tokens // tm,),
        out_shape=[jax.ShapeDtypeStruct((tokens, D_MODEL), BF16)] + [jax.ShapeDtypeStruct((tokens, width), BF16)] * n_res,
        in_specs=[pl.BlockSpec((tm, D_MODEL), lambda i: (i, 0)), _resident((1, D_MODEL)), per_batch, per_batch,
                  _resident(w_t.shape)],
        out_specs=[pl.BlockSpec((tm, D_MODEL), lambda i: (i, 0))] + [pl.BlockSpec((tm, width), lambda i: (i, 0))] * n_res,
        operands=(x, gnorm, scale1p, shift, w_t), parallel=True, comm=comm)
    return (*outs, got)


def _matmul_residual(src, w, x, gate, scale, seq, name, comm=None):
    tokens, k_dim = x.shape[0], w.shape[0]
    tm = _tile_rows(seq)
    per_seq = seq // tm

    def body(s_ref, w_ref, x_ref, gate_ref, xo_ref, f_ref):
        f = _dot(s_ref[...], w_ref[...])
        f_ref[...] = f.astype(BF16)
        xo_ref[...] = x_ref[...] + (scale * gate_ref[0]) * f

    (x_out, f), got = _call(
        body, name=name, grid=(tokens // tm,),
        out_shape=[jax.ShapeDtypeStruct((tokens, D_MODEL), F32), jax.ShapeDtypeStruct((tokens, D_MODEL), BF16)],
        in_specs=[pl.BlockSpec((tm, k_dim), lambda i: (i, 0)), _resident(w.shape),
                  pl.BlockSpec((tm, D_MODEL), lambda i: (i, 0)),
                  pl.BlockSpec((1, 1, D_MODEL), lambda i: (i // per_seq, 0, 0))],
        out_specs=[pl.BlockSpec((tm, D_MODEL), lambda i: (i, 0))] * 2,
        operands=(src, w, x, gate), parallel=True, comm=comm)
    return x_out, f, got


def _ffn_forward(x, gnorm, scale1p, shift, w_gu_t, w_down, gate, scale, seq, name, comm=None):
    tokens, width = x.shape[0], w_down.shape[0]
    tm = _tile_rows(seq)
    per_seq = seq // tm
    n_chunks = width // MXU_N

    def body(x_ref, g_ref, sc_ref, sh_ref, wgu_ref, wd_ref, gate_ref, h_ref, act_ref, silu_ref, dact_ref, xo_ref, f_ref):
        xv = x_ref[...]
        h = (xv * _rms(xv) * g_ref[...]) * sc_ref[0] + sh_ref[0]
        h_ref[...] = h.astype(BF16)
        for ck in range(n_chunks):
            cs = slice(ck * MXU_N, (ck + 1) * MXU_N)
            g = _dot_nt(h_ref[...], wgu_ref[cs, :])
            u = _dot_nt(h_ref[...], wgu_ref[width + ck * MXU_N:width + (ck + 1) * MXU_N, :])
            sig = _sigmoid(g)
            silu = g * sig
            act_ref[:, cs] = (silu * u).astype(BF16)
            silu_ref[:, cs] = silu.astype(BF16)
            dact_ref[:, cs] = (u * (sig + silu * (1.0 - sig))).astype(BF16)
        f = _dot(act_ref[...], wd_ref[...])
        f_ref[...] = f.astype(BF16)
        xo_ref[...] = xv + (scale * gate_ref[0]) * f

    row = lambda i: (i, 0)
    per_batch = pl.BlockSpec((1, 1, D_MODEL), lambda i: (i // per_seq, 0, 0))
    tile = lambda cols: pl.BlockSpec((tm, cols), row)
    wide = jax.ShapeDtypeStruct((tokens, width), BF16)
    outs, got = _call(
        body, name=name, grid=(tokens // tm,),
        out_shape=[jax.ShapeDtypeStruct((tokens, D_MODEL), BF16), wide, wide, wide,
                   jax.ShapeDtypeStruct((tokens, D_MODEL), F32), jax.ShapeDtypeStruct((tokens, D_MODEL), BF16)],
        in_specs=[tile(D_MODEL), _resident((1, D_MODEL)), per_batch, per_batch, _resident(w_gu_t.shape),
                  _resident(w_down.shape), per_batch],
        out_specs=[tile(D_MODEL), tile(width), tile(width), tile(width), tile(D_MODEL), tile(D_MODEL)],
        operands=(x, gnorm, scale1p, shift, w_gu_t, w_down, gate), parallel=True, comm=comm)
    return (*outs, got)


def _residual_backward(dy, gate, f, w, scale, seq, name, silu=None, dact=None, comm=None):
    tokens, k_dim = dy.shape[0], w.shape[0]
    batch = tokens // seq
    tm = _tile_rows(seq)
    per_seq = seq // tm
    n_chunks = k_dim // MXU_N
    swiglu = silu is not None

    def body(*refs):
        if swiglu:
            dy_ref, gate_ref, f_ref, silu_ref, dact_ref, w_ref, df_ref, dgate_ref, dgu_ref = refs
        else:
            dy_ref, gate_ref, f_ref, w_ref, df_ref, dgate_ref, dsrc_ref = refs
        i = pl.program_id(0)
        dy_v = dy_ref[...]
        df_ref[...] = ((scale * gate_ref[0]) * dy_v).astype(BF16)
        part = scale * jnp.sum(dy_v * f_ref[...].astype(F32), axis=0, keepdims=True)

        @pl.when(i % per_seq == 0)
        def _():
            dgate_ref[0] = part

        @pl.when(i % per_seq != 0)
        def _():
            dgate_ref[0] = dgate_ref[0] + part

        if swiglu:
            for ck in range(n_chunks):
                cs = slice(ck * MXU_N, (ck + 1) * MXU_N)
                cu = slice(k_dim + ck * MXU_N, k_dim + (ck + 1) * MXU_N)
                da = _dot_nt(df_ref[...], w_ref[cs, :])
                dgu_ref[:, cs] = (da * dact_ref[:, cs].astype(F32)).astype(BF16)
                dgu_ref[:, cu] = (da * silu_ref[:, cs].astype(F32)).astype(BF16)
        else:
            dsrc_ref[...] = _dot_nt(df_ref[...], w_ref[...])

    row = lambda i: (i, 0)
    per_batch = pl.BlockSpec((1, 1, D_MODEL), lambda i: (i // per_seq, 0, 0))
    in_specs = [pl.BlockSpec((tm, D_MODEL), row), per_batch, pl.BlockSpec((tm, D_MODEL), row)]
    out_shape = [jax.ShapeDtypeStruct((tokens, D_MODEL), BF16), jax.ShapeDtypeStruct((batch, 1, D_MODEL), F32)]
    out_specs = [pl.BlockSpec((tm, D_MODEL), row), per_batch]
    if swiglu:
        in_specs += [pl.BlockSpec((tm, k_dim), row)] * 2
        operands = (dy, gate, f, silu, dact, w)
        out_shape += [jax.ShapeDtypeStruct((tokens, 2 * k_dim), BF16)]
        out_specs += [pl.BlockSpec((tm, 2 * k_dim), row)]
    else:
        operands = (dy, gate, f, w)
        out_shape += [jax.ShapeDtypeStruct((tokens, k_dim), F32)]
        out_specs += [pl.BlockSpec((tm, k_dim), row)]
    in_specs += [_resident(w.shape)]
    outs, got = _call(body, name=name, grid=(tokens // tm,), out_shape=out_shape, in_specs=in_specs,
                      out_specs=out_specs, operands=operands, comm=comm)
    return (*outs, got)


def _matmul_normmod_backward(dsrc, w_t, x, dy, gnorm, scale1p, seq, name, comm=None):
    tokens, k_dim = dsrc.shape
    batch = tokens // seq
    tm = _tile_rows(seq)
    per_seq = seq // tm

    def body(ds_ref, w_ref, x_ref, dy_ref, g_ref, sc_ref, dx_ref, dsh_ref, dsc_ref, dg_ref):
        i = pl.program_id(0)
        dh = _dot(ds_ref[...], w_ref[...])
        xv = x_ref[...]
        r = _rms(xv)
        xn = xv * r
        gn = g_ref[...]
        dsh = jnp.sum(dh, axis=0, keepdims=True)
        dsc = jnp.sum(dh * (xn * gn), axis=0, keepdims=True)
        dhn = dh * sc_ref[0]
        dg = jnp.sum(dhn * xn, axis=0, keepdims=True)
        dxn = dhn * gn
        dx_ref[...] = dy_ref[...] + r * (dxn - xn * jnp.mean(dxn * xn, axis=-1, keepdims=True))

        @pl.when(i % per_seq == 0)
        def _():
            dsh_ref[0] = dsh
            dsc_ref[0] = dsc

        @pl.when(i % per_seq != 0)
        def _():
            dsh_ref[0] = dsh_ref[0] + dsh
            dsc_ref[0] = dsc_ref[0] + dsc

        @pl.when(i == 0)
        def _():
            dg_ref[...] = dg

        @pl.when(i != 0)
        def _():
            dg_ref[...] = dg_ref[...] + dg

    row = lambda i: (i, 0)
    per_batch = pl.BlockSpec((1, 1, D_MODEL), lambda i: (i // per_seq, 0, 0))
    outs, got = _call(
        body, name=name, grid=(tokens // tm,),
        out_shape=[jax.ShapeDtypeStruct((tokens, D_MODEL), F32), jax.ShapeDtypeStruct((batch, 1, D_MODEL), F32),
                   jax.ShapeDtypeStruct((batch, 1, D_MODEL), F32), jax.ShapeDtypeStruct((1, D_MODEL), F32)],
        in_specs=[pl.BlockSpec((tm, k_dim), row), _resident(w_t.shape), pl.BlockSpec((tm, D_MODEL), row),
                  pl.BlockSpec((tm, D_MODEL), row), _resident((1, D_MODEL)), per_batch],
        out_specs=[pl.BlockSpec((tm, D_MODEL), row), per_batch, per_batch, pl.BlockSpec((1, D_MODEL), lambda i: (0, 0))],
        operands=(dsrc, w_t, x, dy, gnorm, scale1p), comm=comm)
    return (*outs, got)


def _ffn_backward(dy, gate, f, silu, dact, w_down, w_gu_t, x, gnorm, scale1p, scale, seq, name, comm=None):
    tokens, k_dim = dy.shape[0], w_down.shape[0]
    batch = tokens // seq
    tm = min(256, seq)
    per_seq = seq // tm
    n_chunks = k_dim // MXU_N

    def body(dy_ref, gate_ref, f_ref, silu_ref, dact_ref, wd_ref, wgu_ref, x_ref, g_ref, sc_ref,
             df_ref, dgate_ref, dgu_ref, dx_ref, dsh_ref, dsc_ref, dg_ref):
        i = pl.program_id(0)
        dy_v = dy_ref[...]
        df_ref[...] = ((scale * gate_ref[0]) * dy_v).astype(BF16)
        dgate = scale * jnp.sum(dy_v * f_ref[...].astype(F32), axis=0, keepdims=True)
        for ck in range(n_chunks):
            cs = slice(ck * MXU_N, (ck + 1) * MXU_N)
            cu = slice(k_dim + ck * MXU_N, k_dim + (ck + 1) * MXU_N)
            da = _dot_nt(df_ref[...], wd_ref[cs, :])
            dgu_ref[:, cs] = (da * dact_ref[:, cs].astype(F32)).astype(BF16)
            dgu_ref[:, cu] = (da * silu_ref[:, cs].astype(F32)).astype(BF16)
        dh = _dot(dgu_ref[...], wgu_ref[...])
        xv = x_ref[...]
        r = _rms(xv)
        xn = xv * r
        gn = g_ref[...]
        dsh = jnp.sum(dh, axis=0, keepdims=True)
        dsc = jnp.sum(dh * (xn * gn), axis=0, keepdims=True)
        dhn = dh * sc_ref[0]
        dg = jnp.sum(dhn * xn, axis=0, keepdims=True)
        dxn = dhn * gn
        dx_ref[...] = dy_v + r * (dxn - xn * jnp.mean(dxn * xn, axis=-1, keepdims=True))

        @pl.when(i % per_seq == 0)
        def _():
            dgate_ref[0] = dgate
            dsh_ref[0] = dsh
            dsc_ref[0] = dsc

        @pl.when(i % per_seq != 0)
        def _():
            dgate_ref[0] = dgate_ref[0] + dgate
            dsh_ref[0] = dsh_ref[0] + dsh
            dsc_ref[0] = dsc_ref[0] + dsc

        @pl.when(i == 0)
        def _():
            dg_ref[...] = dg

        @pl.when(i != 0)
        def _():
            dg_ref[...] = dg_ref[...] + dg

    row = lambda i: (i, 0)
    per_batch = pl.BlockSpec((1, 1, D_MODEL), lambda i: (i // per_seq, 0, 0))
    tile = lambda width: pl.BlockSpec((tm, width), row)
    vec = jax.ShapeDtypeStruct((batch, 1, D_MODEL), F32)
    outs, got = _call(
        body, name=name, grid=(tokens // tm,),
        out_shape=[jax.ShapeDtypeStruct((tokens, D_MODEL), BF16), vec, jax.ShapeDtypeStruct((tokens, 2 * k_dim), BF16),
                   jax.ShapeDtypeStruct((tokens, D_MODEL), F32), vec, vec, jax.ShapeDtypeStruct((1, D_MODEL), F32)],
        in_specs=[tile(D_MODEL), per_batch, tile(D_MODEL), tile(k_dim), tile(k_dim), _resident(w_down.shape),
                  _resident(w_gu_t.shape), tile(D_MODEL), _resident((1, D_MODEL)), per_batch],
        out_specs=[tile(D_MODEL), per_batch, tile(2 * k_dim), tile(D_MODEL), per_batch, per_batch,
                   pl.BlockSpec((1, D_MODEL), lambda i: (0, 0))],
        operands=(dy, gate, f, silu, dact, w_down, w_gu_t, x, gnorm, scale1p), comm=comm)
    return (*outs, got)


def _weight_grad(a, b, seq, name, comm=None):
    tokens, n_out = a.shape
    tn = MXU_N

    def body(a_ref, b_ref, o_ref):
        o_ref[...] = _dot_tn(a_ref[...], b_ref[...]).astype(BF16)

    (out,), got = _call(
        body, name=name, grid=(n_out // tn,),
        out_shape=[jax.ShapeDtypeStruct((n_out, D_MODEL), BF16)],
        in_specs=[pl.BlockSpec((tokens, tn), lambda j: (0, j)), _resident((tokens, D_MODEL))],
        out_specs=[pl.BlockSpec((tn, D_MODEL), lambda j: (j, 0))],
        operands=(a, b), parallel=True, comm=comm)
    return out, got


def _final_loss(x, target, gnorm, seq):
    tokens = x.shape[0]
    tm = _tile_rows(seq)

    def body(x_ref, t_ref, g_ref, dx_ref, dg_ref, loss_ref):
        i = pl.program_id(0)
        xv = x_ref[...]
        r = _rms(xv)
        xn = xv * r
        gn = g_ref[...]
        err = xn * gn - t_ref[...]
        loss = (0.5 / D_MODEL) * jnp.sum(err * err, axis=0, keepdims=True)
        dyv = err * (1.0 / D_MODEL)
        dg = jnp.sum(dyv * xn, axis=0, keepdims=True)
        dxn = dyv * gn
        dx_ref[...] = r * (dxn - xn * jnp.mean(dxn * xn, axis=-1, keepdims=True))

        @pl.when(i == 0)
        def _():
            dg_ref[...] = dg
            loss_ref[...] = loss

        @pl.when(i != 0)
        def _():
            dg_ref[...] = dg_ref[...] + dg
            loss_ref[...] = loss_ref[...] + loss

    row = lambda i: (i, 0)
    fixed = pl.BlockSpec((1, D_MODEL), lambda i: (0, 0))
    outs, _ = _call(
        body, name="final_loss", grid=(tokens // tm,),
        out_shape=[jax.ShapeDtypeStruct((tokens, D_MODEL), F32), jax.ShapeDtypeStruct((1, D_MODEL), F32),
                   jax.ShapeDtypeStruct((1, D_MODEL), F32)],
        in_specs=[pl.BlockSpec((tm, D_MODEL), row), pl.BlockSpec((tm, D_MODEL), row), _resident((1, D_MODEL))],
        out_specs=[pl.BlockSpec((tm, D_MODEL), row), fixed, fixed],
        operands=(x, target, gnorm))
    return outs


def _group_mean(v, bd):
    hi = v.astype(BF16)
    lo = (v - hi.astype(F32)).astype(BF16)
    return _dot(hi, bd) + _dot(lo, bd)


def _sgu_forward(pm_ref, wm_ref, bias_ref, lng_ref, lnb_ref, bd_ref, mixed_scr, n_sub):
    ua = pm_ref[:, 0:D_A].astype(F32)
    va = pm_ref[:, D_A:2 * D_A].astype(F32)
    u_act = _gelu(ua)
    v_act = _gelu(va)
    bd = bd_ref[...]
    vc = v_act - _group_mean(v_act, bd)
    rstd = lax.rsqrt(_group_mean(vc * vc, bd) + EPS)
    vhat = vc * rstd
    vln = vhat * lng_ref[...] + lnb_ref[...]
    left = lax.broadcasted_iota(jnp.int32, (CHUNK, LANES), 1) < HEAD_DIM
    for q in range(n_sub):
        rows = slice(q * CHUNK, (q + 1) * CHUNK)
        for p in range(N_HEADS // 2):
            cols = slice(p * LANES, (p + 1) * LANES)
            vp = vln[rows, cols]
            v_l = jnp.where(left, vp, 0.0).astype(BF16)
            v_r = jnp.where(left, 0.0, vp).astype(BF16)
            mixed_scr[rows, cols] = _dot(wm_ref[2 * p], v_l) + _dot(wm_ref[2 * p + 1], v_r) + bias_ref[:, cols]
    return ua, va, u_act, vhat, rstd, vln


def _halo_specs(tm, tokens, width):
    prev = pl.BlockSpec((HALO, width), lambda i: (jnp.maximum(i * (tm // HALO) - 1, 0), 0))
    nxt = pl.BlockSpec((HALO, width), lambda i: (jnp.minimum((i + 1) * (tm // HALO), tokens // HALO - 1), 0))
    return prev, nxt


def _mixer_forward(proj, wm, bias_full, lng, lnb, convw, og, bd, seq, name):
    tokens = proj.shape[0]
    tm = _tile_rows(seq)
    per_seq = seq // tm
    n_sub = tm // CHUNK

    def body(pm_ref, pp_ref, wm_ref, bias_ref, lng_ref, lnb_ref, cw_ref, og_ref, bd_ref, y_ref, mixed_scr):
        i = pl.program_id(0)
        first = (i % per_seq) == 0
        _, _, u_act, _, _, _ = _sgu_forward(pm_ref, wm_ref, bias_ref, lng_ref, lnb_ref, bd_ref, mixed_scr, n_sub)
        ya = u_act * mixed_scr[...]
        y_ref[:, 0:D_A] = (ya * _rms(ya) * og_ref[:, 0:D_A]).astype(BF16)

        bg = pm_ref[:, 2 * D_A:3 * D_A].astype(F32)
        z = pm_ref[:, 3 * D_A:4 * D_A].astype(F32) * pm_ref[:, 4 * D_A:5 * D_A].astype(F32)
        zp = pp_ref[:, 3 * D_A:4 * D_A].astype(F32) * pp_ref[:, 4 * D_A:5 * D_A].astype(F32)
        zp = jnp.where(first, 0.0, zp)
        zext = jnp.concatenate([zp, z], axis=0)
        z1 = pltpu.roll(zext, 1, 0)[HALO:]
        z2 = pltpu.roll(zext, 2, 0)[HALO:]
        conv = cw_ref[0:1, :] * z2 + cw_ref[1:2, :] * z1 + cw_ref[2:3, :] * z
        yb = bg * conv
        y_ref[:, D_A:2 * D_A] = (yb * _rms(yb) * og_ref[:, D_A:2 * D_A]).astype(BF16)

    prev, _ = _halo_specs(tm, tokens, D_PROJ)
    (y,), _ = _call(
        body, name=name, grid=(tokens // tm,),
        out_shape=[jax.ShapeDtypeStruct((tokens, D_MODEL), BF16)],
        in_specs=[pl.BlockSpec((tm, D_PROJ), lambda i: (i, 0)), prev, _resident(wm.shape), _resident(bias_full.shape),
                  _resident(lng.shape), _resident(lnb.shape), _resident(convw.shape), _resident(og.shape),
                  _resident(bd.shape)],
        out_specs=[pl.BlockSpec((tm, D_MODEL), lambda i: (i, 0))],
        scratch_shapes=[pltpu.VMEM((tm, D_A), F32)],
        operands=(proj, proj, wm, bias_full, lng, lnb, convw, og, bd), parallel=True)
    return y


def _mixer_backward(proj, dy, wm, bias_full, lng, lnb, convw, og, bd, causal, seq, name, comm=None):
    tokens = proj.shape[0]
    tm = _tile_rows(seq)
    per_seq = seq // tm
    n_sub = tm // CHUNK
    ext = tm + 2 * HALO

    def body(pm_ref, pp_ref, pn_ref, dy_ref, dyn_ref, wm_ref, bias_ref, lng_ref, lnb_ref, cw_ref, og_ref, bd_ref,
             causal_ref, dp_ref, dog_ref, dcw_ref, dlng_ref, dlnb_ref, dbias_ref, dwm_ref, mixed_scr, dvln_scr):
        i = pl.program_id(0)
        first = (i % per_seq) == 0
        last = (i % per_seq) == per_seq - 1

        @pl.when(i == 0)
        def _():
            dog_ref[...] = jnp.zeros_like(dog_ref)
            dcw_ref[...] = jnp.zeros_like(dcw_ref)
            dlng_ref[...] = jnp.zeros_like(dlng_ref)
            dlnb_ref[...] = jnp.zeros_like(dlnb_ref)
            dbias_ref[...] = jnp.zeros_like(dbias_ref)
            dwm_ref[...] = jnp.zeros_like(dwm_ref)

        ua, va, u_act, vhat, rstd, vln = _sgu_forward(pm_ref, wm_ref, bias_ref, lng_ref, lnb_ref, bd_ref, mixed_scr, n_sub)
        mixed = mixed_scr[...]
        ya = u_act * mixed
        ra = _rms(ya)
        yhat = ya * ra
        dya_in = dy_ref[:, 0:D_A]
        dog_ref[:, 0:D_A] = dog_ref[:, 0:D_A] + jnp.sum(dya_in * yhat, axis=0, keepdims=True)
        dyh = dya_in * og_ref[:, 0:D_A]
        dya = ra * (dyh - yhat * jnp.mean(dyh * yhat, axis=-1, keepdims=True))
        d_u = dya * mixed
        d_mixed = dya * u_act
        left = lax.broadcasted_iota(jnp.int32, (CHUNK, LANES), 1) < HEAD_DIM
        dbias = jnp.zeros((CHUNK, D_A), F32)
        for q in range(n_sub):
            rows = slice(q * CHUNK, (q + 1) * CHUNK)
            dbias = dbias + d_mixed[rows, :]
            for p in range(N_HEADS // 2):
                cols = slice(p * LANES, (p + 1) * LANES)
                dm = d_mixed[rows, cols]
                dm_l = jnp.where(left, dm, 0.0).astype(BF16)
                dm_r = jnp.where(left, 0.0, dm).astype(BF16)
                vp = vln[rows, cols].astype(BF16)
                dwm_ref[2 * p] = dwm_ref[2 * p] + causal_ref[...] * _dot_nt(dm_l, vp)
                dwm_ref[2 * p + 1] = dwm_ref[2 * p + 1] + causal_ref[...] * _dot_nt(dm_r, vp)
                dvln_scr[rows, cols] = _dot_tn(wm_ref[2 * p], dm_l) + _dot_tn(wm_ref[2 * p + 1], dm_r)
        dbias_ref[...] = dbias_ref[...] + dbias
        dvln = dvln_scr[...]
        dlng_ref[...] = dlng_ref[...] + jnp.sum(dvln * vhat, axis=0, keepdims=True)
        dlnb_ref[...] = dlnb_ref[...] + jnp.sum(dvln, axis=0, keepdims=True)
        dvh = dvln * lng_ref[...]
        bd = bd_ref[...]
        d_v = rstd * (dvh - _group_mean(dvh, bd) - vhat * _group_mean(dvh * vhat, bd))
        dp_ref[:, 0:D_A] = (d_u * _gelu_grad(ua)).astype(BF16)
        dp_ref[:, D_A:2 * D_A] = (d_v * _gelu_grad(va)).astype(BF16)

        def ext_cols(lo):
            cs = slice(lo, lo + D_A)
            return jnp.concatenate([pp_ref[:, cs], pm_ref[:, cs], pn_ref[:, cs]], axis=0).astype(F32)

        bg, cg, xb = ext_cols(2 * D_A), ext_cols(3 * D_A), ext_cols(4 * D_A)
        row = lax.broadcasted_iota(jnp.int32, (ext, D_A), 0)
        z = jnp.where(jnp.logical_and(first, row < HALO), 0.0, cg * xb)
        z1 = pltpu.roll(z, 1, 0)
        z2 = pltpu.roll(z, 2, 0)
        w0, w1, w2 = cw_ref[0:1, :], cw_ref[1:2, :], cw_ref[2:3, :]
        conv = w0 * z2 + w1 * z1 + w2 * z
        yb = bg * conv
        rb = _rms(yb)
        yhb = yb * rb
        dyn = jnp.where(last, 0.0, dyn_ref[:, D_A:2 * D_A])
        dyb_in = jnp.concatenate([jnp.zeros((HALO, D_A), F32), dy_ref[:, D_A:2 * D_A], dyn], axis=0)
        dyhb = dyb_in * og_ref[:, D_A:2 * D_A]
        dyb = rb * (dyhb - yhb * jnp.mean(dyhb * yhb, axis=-1, keepdims=True))
        d_conv = dyb * bg
        dz = w2 * d_conv + w1 * pltpu.roll(d_conv, ext - 1, 0) + w0 * pltpu.roll(d_conv, ext - 2, 0)
        main = slice(HALO, HALO + tm)
        dp_ref[:, 2 * D_A:3 * D_A] = (dyb * conv)[main].astype(BF16)
        dp_ref[:, 3 * D_A:4 * D_A] = (dz * xb)[main].astype(BF16)
        dp_ref[:, 4 * D_A:5 * D_A] = (dz * cg)[main].astype(BF16)
        dog_ref[:, D_A:2 * D_A] = dog_ref[:, D_A:2 * D_A] + jnp.sum((dyb_in * yhb)[main], axis=0, keepdims=True)
        dcm = d_conv[main]
        dcw_ref[0:1, :] = dcw_ref[0:1, :] + jnp.sum(dcm * z2[main], axis=0, keepdims=True)
        dcw_ref[1:2, :] = dcw_ref[1:2, :] + jnp.sum(dcm * z1[main], axis=0, keepdims=True)
        dcw_ref[2:3, :] = dcw_ref[2:3, :] + jnp.sum(dcm * z[main], axis=0, keepdims=True)

    prev_p, next_p = _halo_specs(tm, tokens, D_PROJ)
    _, next_d = _halo_specs(tm, tokens, D_MODEL)
    fixed2 = lambda shape: pl.BlockSpec(shape, lambda i: (0, 0))
    outs, got = _call(
        body, name=name, grid=(tokens // tm,),
        out_shape=[jax.ShapeDtypeStruct((tokens, D_PROJ), BF16), jax.ShapeDtypeStruct((1, D_MODEL), F32),
                   jax.ShapeDtypeStruct((8, D_A), F32), jax.ShapeDtypeStruct((1, D_A), F32),
                   jax.ShapeDtypeStruct((1, D_A), F32), jax.ShapeDtypeStruct((CHUNK, D_A), F32),
                   jax.ShapeDtypeStruct((N_HEADS, CHUNK, CHUNK), F32)],
        in_specs=[pl.BlockSpec((tm, D_PROJ), lambda i: (i, 0)), prev_p, next_p,
                  pl.BlockSpec((tm, D_MODEL), lambda i: (i, 0)), next_d,
                  _resident(wm.shape), _resident(bias_full.shape), _resident(lng.shape), _resident(lnb.shape),
                  _resident(convw.shape), _resident(og.shape), _resident(bd.shape), _resident(causal.shape)],
        out_specs=[pl.BlockSpec((tm, D_PROJ), lambda i: (i, 0)), fixed2((1, D_MODEL)), fixed2((8, D_A)),
                   fixed2((1, D_A)), fixed2((1, D_A)), fixed2((CHUNK, D_A)),
                   pl.BlockSpec((N_HEADS, CHUNK, CHUNK), lambda i: (0, 0, 0))],
        scratch_shapes=[pltpu.VMEM((tm, D_A), F32), pltpu.VMEM((tm, D_A), F32)],
        operands=(proj, proj, proj, dy, dy, wm, bias_full, lng, lnb, convw, og, bd, causal), comm=comm)
    return (*outs, got)


def _adamw_update(wv, gv, mv, vv):
    nm = ADAM_B1 * mv + (1.0 - ADAM_B1) * gv
    nv = ADAM_B2 * vv + (1.0 - ADAM_B2) * (gv * gv)
    m_hat = nm / (1.0 - ADAM_B1 ** ADAM_STEP)
    v_hat = nv / (1.0 - ADAM_B2 ** ADAM_STEP)
    return -ADAM_LR * (m_hat / (jnp.sqrt(v_hat) + ADAM_EPS) + ADAM_WD * wv), nm, nv


def _adamw_rows(recv, w, m, v, name):
    depth, rows, cols = w.shape
    tr = rows // 2
    last = rows // tr - 1

    def body(*refs):
        r_refs, (w_ref, m_ref, v_ref, g_ref, d_ref, nm_ref, nv_ref) = refs[:depth], refs[depth:]
        for l in range(depth):
            @pl.when(pl.program_id(0) == l)
            def _(r_ref=r_refs[l]):
                acc = r_ref[0].astype(F32)
                for d in range(1, N_DEV):
                    acc = acc + r_ref[d].astype(F32)
                g_ref[0] = acc
                d_ref[0], nm_ref[0], nv_ref[0] = _adamw_update(w_ref[0], acc, m_ref[0], v_ref[0])

    def slots(l):
        return pl.BlockSpec((N_DEV, tr, cols), lambda ll, i: (0, jnp.where(ll == l, i, jnp.where(ll < l, 0, last)), 0))

    spec = pl.BlockSpec((1, tr, cols), lambda ll, i: (ll, i, 0))
    return pl.pallas_call(
        body, name=name, grid=(depth, rows // tr),
        out_shape=[jax.ShapeDtypeStruct((depth, rows, cols), F32)] * 4,
        in_specs=[slots(l) for l in range(depth)] + [spec] * 3, out_specs=[spec] * 4,
        compiler_params=_params(2),
    )(*recv, w, m, v)


def _adamw(w, g, m, v, name):
    rows, cols = w.shape
    tr = max(t for t in range(8, 513, 8) if rows % t == 0)

    def body(w_ref, g_ref, m_ref, v_ref, d_ref, nm_ref, nv_ref):
        d_ref[...], nm_ref[...], nv_ref[...] = _adamw_update(w_ref[...], g_ref[...], m_ref[...], v_ref[...])

    spec = pl.BlockSpec((tr, cols), lambda i: (i, 0))
    return pl.pallas_call(
        body, name=name, grid=(rows // tr,),
        out_shape=[jax.ShapeDtypeStruct((rows, cols), F32)] * 3,
        in_specs=[spec] * 4, out_specs=[spec] * 3,
        compiler_params=_params(parallel=True),
    )(w, g, m, v)


def _adamw_nd(w, g, m, v, name):
    shape = w.shape
    two_d = (-1, shape[-1])
    d, nm, nv = _adamw(w.reshape(two_d), g.reshape(two_d), m.reshape(two_d), v.reshape(two_d), name)
    return d.reshape(shape), nm.reshape(shape), nv.reshape(shape)


def kernel(x, c, ada_w, ada_b, norm_ffn1_g, ffn1_w_gu, ffn1_w_down, norm_mix_g, mix_w_in, sgu_ln_g, sgu_ln_b, sgu_w_s, sgu_b, conv_w, out_norm_g, mix_w_out, norm_ffn2_g, ffn2_w_gu, ffn2_w_down, final_norm_g, loss_target, m_ada_w, m_ada_b, m_norm_ffn1_g, m_ffn1_w_gu, m_ffn1_w_down, m_norm_mix_g, m_mix_w_in, m_sgu_ln_g, m_sgu_ln_b, m_sgu_w_s, m_sgu_b, m_conv_w, m_out_norm_g, m_mix_w_out, m_norm_ffn2_g, m_ffn2_w_gu, m_ffn2_w_down, m_final_norm_g, v_ada_w, v_ada_b, v_norm_ffn1_g, v_ffn1_w_gu, v_ffn1_w_down, v_norm_mix_g, v_mix_w_in, v_sgu_ln_g, v_sgu_ln_b, v_sgu_w_s, v_sgu_b, v_conv_w, v_out_norm_g, v_mix_w_out, v_norm_ffn2_g, v_ffn2_w_gu, v_ffn2_w_down, v_final_norm_g):
    batch, seq, _ = x.shape
    tokens = batch * seq
    me = 4 * lax.axis_index("x") + 2 * lax.axis_index("y") + lax.axis_index("c")
    weights = dict(ada_w=ada_w, ada_b=ada_b, norm_ffn1_g=norm_ffn1_g, ffn1_w_gu=ffn1_w_gu, ffn1_w_down=ffn1_w_down,
                   norm_mix_g=norm_mix_g, mix_w_in=mix_w_in, sgu_ln_g=sgu_ln_g, sgu_ln_b=sgu_ln_b, sgu_w_s=sgu_w_s,
                   sgu_b=sgu_b, conv_w=conv_w, out_norm_g=out_norm_g, mix_w_out=mix_w_out, norm_ffn2_g=norm_ffn2_g,
                   ffn2_w_gu=ffn2_w_gu, ffn2_w_down=ffn2_w_down, final_norm_g=final_norm_g)
    mom1 = dict(ada_w=m_ada_w, ada_b=m_ada_b, norm_ffn1_g=m_norm_ffn1_g, ffn1_w_gu=m_ffn1_w_gu,
                ffn1_w_down=m_ffn1_w_down, norm_mix_g=m_norm_mix_g, mix_w_in=m_mix_w_in, sgu_ln_g=m_sgu_ln_g,
                sgu_ln_b=m_sgu_ln_b, sgu_w_s=m_sgu_w_s, sgu_b=m_sgu_b, conv_w=m_conv_w, out_norm_g=m_out_norm_g,
                mix_w_out=m_mix_w_out, norm_ffn2_g=m_norm_ffn2_g, ffn2_w_gu=m_ffn2_w_gu, ffn2_w_down=m_ffn2_w_down,
                final_norm_g=m_final_norm_g)
    mom2 = dict(ada_w=v_ada_w, ada_b=v_ada_b, norm_ffn1_g=v_norm_ffn1_g, ffn1_w_gu=v_ffn1_w_gu,
                ffn1_w_down=v_ffn1_w_down, norm_mix_g=v_norm_mix_g, mix_w_in=v_mix_w_in, sgu_ln_g=v_sgu_ln_g,
                sgu_ln_b=v_sgu_ln_b, sgu_w_s=v_sgu_w_s, sgu_b=v_sgu_b, conv_w=v_conv_w, out_norm_g=v_out_norm_g,
                mix_w_out=v_mix_w_out, norm_ffn2_g=v_norm_ffn2_g, ffn2_w_gu=v_ffn2_w_gu, ffn2_w_down=v_ffn2_w_down,
                final_norm_g=v_final_norm_g)

    big = ("ffn1_w_gu", "ffn1_w_down", "mix_w_in", "mix_w_out", "ffn2_w_gu", "ffn2_w_down")
    transposed = ("ffn1_w_gu", "mix_w_in", "ffn2_w_gu")
    as_rows = lambda nm, a: jnp.swapaxes(a, 1, 2) if nm in transposed else a
    shard = {(l, nm): as_rows(nm, weights[nm])[l].astype(BF16) for l in range(DEPTH) for nm in big}
    full_w = {}

    def gather_of(keys):
        return keys, _GatherRows([shard[k] for k in keys])

    def landed(plan, got):
        full_w.update(zip(plan[0], got))

    plan = gather_of([(0, "ffn1_w_gu")])
    landed(plan, _comm_only(plan[1], "gather_first"))

    small_in = _pack_small([c, conv_w])
    small_all, _ = _all_gather_small(small_in, "gather_c")
    c_all, convw_all = _unpack_small(small_all, [c.shape, conv_w.shape], lead=(N_DEV,))
    c_all = c_all.reshape(N_DEV * batch, D_MODEL)
    convw_full = jnp.transpose(convw_all, (1, 2, 0, 3)).reshape(DEPTH, 3, D_A)
    ada_cols = ada_w.shape[2]
    ada_b_cols = lax.dynamic_slice_in_dim(ada_b, me * ada_cols, ada_cols, axis=1).reshape(DEPTH, 1, ada_cols)
    c_rows = jnp.pad(c_all.reshape(N_DEV, batch, D_MODEL), ((0, 0), (0, 8 - batch), (0, 0))).reshape(N_DEV * 8, D_MODEL)
    ada_local = _ada_forward(c_rows, ada_w, ada_b_cols)
    ada_recv = _exchange_batch_rows(ada_local, "exchange_ada")
    ada_mine = jnp.transpose(ada_recv[:, :, :batch, :], (1, 2, 0, 3)).reshape(DEPTH, batch, N_MOD * D_MODEL)
    mod = ada_mine.reshape(DEPTH, batch, N_MOD, 1, D_MODEL)

    causal = jnp.tril(jnp.ones((CHUNK, CHUNK), F32))
    bd = jnp.kron(jnp.eye(N_HEADS, dtype=F32), jnp.full((HEAD_DIM, HEAD_DIM), 1.0 / HEAD_DIM, F32)).astype(BF16)
    row_vec = lambda a: a.reshape(1, -1)

    hosted_gathers = {
        (0, "ffn1"): [(0, "ffn1_w_down"), (0, "mix_w_in"), (0, "mix_w_out")],
        (0, "ffn_down1"): [(0, "ffn2_w_gu")],
        (0, "mix_in"): [(0, "ffn2_w_down")],
        (0, "ffn2"): [(1, "ffn1_w_gu"), (1, "ffn1_w_down"), (1, "mix_w_in"), (1, "mix_w_out")],
        (1, "ffn1"): [(1, "ffn2_w_gu"), (1, "ffn2_w_down")],
    }

    def hosting(l, site):
        keys = hosted_gathers.get((l, site))
        return gather_of(keys) if keys else (None, None)

    xs = x.reshape(tokens, D_MODEL)
    saved = []
    for l in range(DEPTH):
        sh1, sc1, g1, sh2, sc2, g2, sh3, sc3, g3 = [mod[l, :, k] for k in range(N_MOD)]
        mixer_consts = dict(
            wm=(sgu_w_s[l] * causal[None]).astype(BF16),
            bias_full=jnp.repeat(sgu_b[l].T, HEAD_DIM, axis=1),
            lng=row_vec(jnp.tile(sgu_ln_g[l], N_HEADS)), lnb=row_vec(jnp.tile(sgu_ln_b[l], N_HEADS)),
            convw=jnp.pad(convw_full[l], ((0, 5), (0, 0))), og=row_vec(out_norm_g[l]), bd=bd)
        x0 = xs
        plan = hosting(l, "ffn1")
        if l == 0:
            h1, a1, s1, w1, got = _normmod_matmul(x0, row_vec(norm_ffn1_g[l]), 1.0 + sc1, sh1, full_w[l, "ffn1_w_gu"], seq, "ffn_up", True, plan[1])
            landed(plan, got)
            plan = hosting(l, "ffn_down1")
            x1, f1, got = _matmul_residual(a1, full_w[l, "ffn1_w_down"], x0, g1, 0.5, seq, "ffn_down", plan[1])
        else:
            h1, a1, s1, w1, x1, f1, got = _ffn_forward(
                x0, row_vec(norm_ffn1_g[l]), 1.0 + sc1, sh1, full_w[l, "ffn1_w_gu"], full_w[l, "ffn1_w_down"], g1, 0.5, seq, "ffn_fwd", plan[1])
        if got:
            landed(plan, got)
        plan = hosting(l, "mix_in")
        h2, proj, got = _normmod_matmul(x1, row_vec(norm_mix_g[l]), 1.0 + sc2, sh2, full_w[l, "mix_w_in"], seq, "mix_in", False, plan[1])
        if got:
            landed(plan, got)
        ymix = _mixer_forward(proj, seq=seq, name="mixer_forward", **mixer_consts)
        x2, o2, _ = _matmul_residual(ymix, full_w[l, "mix_w_out"], x1, g2, 1.0, seq, "mix_out")
        plan = hosting(l, "ffn2")
        h3, a3, s3, w3, x3, f3, got = _ffn_forward(
            x2, row_vec(norm_ffn2_g[l]), 1.0 + sc3, sh3, full_w[l, "ffn2_w_gu"], full_w[l, "ffn2_w_down"], g3, 0.5, seq, "ffn_fwd", plan[1])
        if got:
            landed(plan, got)
        saved.append(dict(x0=x0, x1=x1, x2=x2, h1=h1, h2=h2, h3=h3, a1=a1, s1=s1, w1=w1, a3=a3, s3=s3, w3=w3, f1=f1, f3=f3, o2=o2, proj=proj,
                          ymix=ymix, mixer_consts=mixer_consts, sc=(1.0 + sc1, 1.0 + sc2, 1.0 + sc3), gates=(g1, g2, g3)))
        xs = x3

    dx, d_final_g, loss_cols = _final_loss(xs, loss_target.reshape(tokens, D_MODEL), row_vec(final_norm_g), seq)

    recv = {}
    small_grads = [None] * DEPTH
    d_mod = [None] * DEPTH

    mix_names = ("out_norm_g", "sgu_ln_g", "sgu_ln_b", "sgu_w_s", "sgu_b", "conv_w")
    late_names = ("norm_ffn1_g", "norm_mix_g", "norm_ffn2_g")

    def mix_parts(l):
        return [small_grads[l][nm] for nm in mix_names]

    def late_parts(l):
        return [small_grads[l][nm] for nm in late_names] + [d_mod[l]]

    pending = []

    def scatter_later(l, nm, grad):
        pending.append(((l, nm), _ScatterRows([grad])))

    def host():
        keys, parts = [k for k, _ in pending], [p for _, p in pending]
        pending.clear()
        return keys, (_Exchanges(parts) if parts else None)

    def hosted(keys, got):
        if got:
            recv.update(zip(keys, got))

    for l in reversed(range(DEPTH)):
        sv = saved[l]
        mc = sv["mixer_consts"]
        if l + 1 < DEPTH:
            pending.append((("late", l + 1), _GatherRows([_pack_small(late_parts(l + 1))])))
        keys, comm = host()
        df3, dg3, dgu3, dx2, dsh3, dsc3, dn3, got = _ffn_backward(
            dx, sv["gates"][2], sv["f3"], sv["s3"], sv["w3"], full_w[l, "ffn2_w_down"], full_w[l, "ffn2_w_gu"], sv["x2"],
            row_vec(norm_ffn2_g[l]), sv["sc"][2], 0.5, seq, "ffn_bwd", comm)
        hosted(keys, got)
        gw_down2, _ = _weight_grad(sv["a3"], df3, seq, "grad_w_down")
        scatter_later(l, "ffn2_w_down", gw_down2)
        keys, comm = host()
        gw_gu2, got = _weight_grad(dgu3, sv["h3"], seq, "grad_w_gu", comm)
        hosted(keys, got)
        scatter_later(l, "ffn2_w_gu", gw_gu2)
        do2, dg2, dymix, _ = _residual_backward(dx2, sv["gates"][1], sv["o2"], full_w[l, "mix_w_out"], 1.0, seq, "mix_out_bwd")
        gw_out, _ = _weight_grad(sv["ymix"], do2, seq, "grad_w_out")
        keys, comm = host()
        dproj, d_og, d_cw, d_lng, d_lnb, d_bias, d_wm, got = _mixer_backward(
            sv["proj"], dymix, causal=causal, seq=seq, name="mixer_backward", comm=comm, **mc)
        hosted(keys, got)
        small_grads[l] = dict(
            out_norm_g=d_og, sgu_ln_g=d_lng.reshape(N_HEADS, HEAD_DIM).sum(0), sgu_ln_b=d_lnb.reshape(N_HEADS, HEAD_DIM).sum(0),
            sgu_w_s=d_wm, sgu_b=d_bias.reshape(CHUNK, N_HEADS, HEAD_DIM).sum(-1).T, conv_w=d_cw[0:3])
        pending.append((("mix", l), _GatherRows([_pack_small(mix_parts(l))])))
        scatter_later(l, "mix_w_out", gw_out)
        keys, comm = host()
        dx1, dsh2, dsc2, dn2, got = _matmul_normmod_backward(dproj, full_w[l, "mix_w_in"], sv["x1"], dx2, row_vec(norm_mix_g[l]), sv["sc"][1], seq, "mix_in_bwd", comm)
        hosted(keys, got)
        gw_in, _ = _weight_grad(dproj, sv["h2"], seq, "grad_w_in")
        scatter_later(l, "mix_w_in", gw_in)
        keys, comm = host()
        if l > 0:
            df1, dg1, dgu1, dx0, dsh1, dsc1, dn1, got = _ffn_backward(
                dx1, sv["gates"][0], sv["f1"], sv["s1"], sv["w1"], full_w[l, "ffn1_w_down"], full_w[l, "ffn1_w_gu"], sv["x0"],
                row_vec(norm_ffn1_g[l]), sv["sc"][0], 0.5, seq, "ffn_bwd", comm)
        else:
            df1, dg1, dgu1, got = _residual_backward(dx1, sv["gates"][0], sv["f1"], full_w[l, "ffn1_w_down"], 0.5, seq, "ffn_down_bwd", sv["s1"], sv["w1"], comm)
        hosted(keys, got)
        gw_down1, _ = _weight_grad(sv["a1"], df1, seq, "grad_w_down")
        scatter_later(l, "ffn1_w_down", gw_down1)
        keys, comm = host()
        gw_gu1, got = _weight_grad(dgu1, sv["h1"], seq, "grad_w_gu", comm)
        hosted(keys, got)
        scatter_later(l, "ffn1_w_gu", gw_gu1)
        if l == 0:
            keys, comm = host()
            dx0, dsh1, dsc1, dn1, got = _matmul_normmod_backward(dgu1, full_w[l, "ffn1_w_gu"], sv["x0"], dx1, row_vec(norm_ffn1_g[l]), sv["sc"][0], seq, "ffn_up_bwd", comm)
            hosted(keys, got)
        dx = dx0
        small_grads[l].update(norm_ffn1_g=dn1, norm_mix_g=dn2, norm_ffn2_g=dn3)
        d_mod[l] = jnp.concatenate([dsh1, dsc1, dg1, dsh2, dsc2, dg2, dsh3, dsc3, dg3], axis=1)
    grad_x = dx.reshape(batch, seq, D_MODEL)

    grad_big, delta, new_m, new_v = {}, {}, {}, {}
    for nm in big:
        results = _adamw_rows([recv[l, nm] for l in range(DEPTH)], as_rows(nm, weights[nm]), as_rows(nm, mom1[nm]),
                              as_rows(nm, mom2[nm]), "adamw_" + nm)
        grad_big[nm], delta[nm], new_m[nm], new_v[nm] = [as_rows(nm, r) for r in results]

    last_parts = late_parts(0) + [d_final_g, loss_cols]
    last_shapes = [p.shape for p in last_parts]
    packed_all, packed_sum = _all_gather_small(_pack_small(last_parts), "reduce_small")
    late_sum = {0: _unpack_small(packed_sum, last_shapes)}
    d_mod_dev = {0: _unpack_small(packed_all, last_shapes, lead=(N_DEV,))[len(late_names)]}
    mix_sum = {}
    for l in range(DEPTH):
        gathered = recv["mix", l].reshape(N_DEV, -1, LANES)
        mix_sum[l] = _unpack_small(_sum_gathered(gathered, "sum_mix"), [p.shape for p in mix_parts(l)])
        if l > 0:
            shapes_l = [p.shape for p in late_parts(l)]
            gathered = recv["late", l].reshape(N_DEV, -1, LANES)
            late_sum[l] = _unpack_small(_sum_gathered(gathered, "sum_late"), shapes_l)
            d_mod_dev[l] = _unpack_small(gathered, shapes_l, lead=(N_DEV,))[len(late_names)]
    grad_small = {}
    for group, names in ((mix_sum, mix_names), (late_sum, late_names)):
        for k, nm in enumerate(names):
            grad_small[nm] = jnp.stack([group[l][k] for l in range(DEPTH)]).reshape(
                (DEPTH, 3, D_A) if nm == "conv_w" else weights[nm].shape)
    grad_small["conv_w"] = lax.dynamic_slice_in_dim(grad_small["conv_w"], me * conv_w.shape[2], conv_w.shape[2], axis=2)
    grad_small["final_norm_g"] = late_sum[0][len(late_names) + 1].reshape(final_norm_g.shape)
    loss = jnp.sum(late_sum[0][len(late_names) + 2])
    d_ada_all = jnp.stack([d_mod_dev[l] for l in range(DEPTH)]).reshape(DEPTH, N_DEV * batch, N_MOD * D_MODEL)
    d_ada_cols = lax.dynamic_slice_in_dim(d_ada_all, me * ada_cols, ada_cols, axis=2)
    g_ada_w, g_ada_b = _ada_backward(c_all, d_ada_cols, d_ada_all)

    grads = dict(grad_big)
    grads.update(grad_small)
    grads["ada_w"] = g_ada_w
    grads["ada_b"] = g_ada_b.reshape(ada_b.shape)

    names = ("ada_w", "ada_b", "norm_ffn1_g", "ffn1_w_gu", "ffn1_w_down", "norm_mix_g", "mix_w_in", "sgu_ln_g",
             "sgu_ln_b", "sgu_w_s", "sgu_b", "conv_w", "out_norm_g", "mix_w_out", "norm_ffn2_g", "ffn2_w_gu",
             "ffn2_w_down", "final_norm_g")
    delta["ada_w"], new_m["ada_w"], new_v["ada_w"] = _adamw_nd(ada_w, grads["ada_w"], m_ada_w, v_ada_w, "adamw_ada_w")
    rest = [nm for nm in names if nm not in big and nm != "ada_w"]
    pack = lambda src: _pack_small([src[nm] for nm in rest])
    d_p, m_p, v_p = _adamw(pack(weights), pack(grads), pack(mom1), pack(mom2), "adamw_small")
    rest_shapes = [weights[nm].shape for nm in rest]
    for nm, d_k, m_k, v_k in zip(rest, _unpack_small(d_p, rest_shapes), _unpack_small(m_p, rest_shapes), _unpack_small(v_p, rest_shapes)):
        delta[nm], new_m[nm], new_v[nm] = d_k, m_k, v_k

    return (loss, grad_x, *[grads[nm] for nm in names], *[delta[nm] for nm in names],
            *[new_m[nm] for nm in names], *[new_v[nm] for nm in names])
```

```python
import math

import jax
import jax.numpy as jnp
from jax import lax
from jax.experimental import pallas as pl
from jax.experimental.pallas import tpu as pltpu

F32 = jnp.float32
BF16 = jnp.bfloat16

D_MODEL = 1024
D_FF = 2816
D_A = 512
D_PROJ = 2560
N_HEADS = 8
HEAD_DIM = 64
CHUNK = 128
N_MOD = 9
DEPTH = 2
EPS = 1e-6
N_DEV = 8
LANES = 128
MXU_N = 256
HALO = 16
VMEM_LIMIT = 56 * 1024 * 1024
FORWARD_STEPS = 4

ADAM_LR = 0.001
ADAM_B1 = 0.9
ADAM_B2 = 0.999
ADAM_EPS = 1e-08
ADAM_WD = 0.01
ADAM_STEP = 10

MESH = pl.DeviceIdType.MESH


def _dot(a, b):
    return jnp.dot(a, b, preferred_element_type=F32)


def _dot_nt(a, b):
    return lax.dot_general(a, b, (((1,), (1,)), ((), ())), preferred_element_type=F32)


def _dot_tn(a, b):
    return lax.dot_general(a, b, (((0,), (0,)), ((), ())), preferred_element_type=F32)


def _sigmoid(x):
    return 0.5 * jnp.tanh(0.5 * x) + 0.5


def _gelu(x):
    return 0.5 * x * (1.0 + lax.erf(x * (1.0 / math.sqrt(2.0))))


def _gelu_grad(x):
    cdf = 0.5 * (1.0 + lax.erf(x * (1.0 / math.sqrt(2.0))))
    return cdf + x * jnp.exp(-0.5 * x * x) * (1.0 / math.sqrt(2.0 * math.pi))


def _params(n_axes=1, parallel=False):
    sem = ("parallel" if parallel else "arbitrary",) * n_axes
    return pltpu.CompilerParams(dimension_semantics=sem, vmem_limit_bytes=VMEM_LIMIT)


def _resident(shape):
    nd = len(shape)
    return pl.BlockSpec(shape, lambda *_: (0,) * nd, pipeline_mode=pl.Buffered(1))


def _tile_rows(seq):
    return min(512, seq)


def _my_position():
    x, y, c = lax.axis_index("x"), lax.axis_index("y"), lax.axis_index("c")
    return x, y, c, 4 * x + 2 * y + c


def _peer(x, y, c, p):
    return (x ^ ((p >> 2) & 1), y ^ ((p >> 1) & 1), c ^ (p & 1))


class _GatherRows:
    def __init__(self, shards):
        self.operands = list(shards)
        n = len(shards)
        self.out_shape = [jax.ShapeDtypeStruct((N_DEV * s.shape[0], s.shape[1]), s.dtype) for s in shards]
        self.scratch = [pltpu.SemaphoreType.DMA((n, N_DEV - 1)), pltpu.SemaphoreType.DMA((n, N_DEV - 1)),
                        pltpu.SemaphoreType.DMA((n,))]

    def _plan(self, src, dst, send, recv, loc):
        x, y, c, _ = _my_position()
        me, sib = (x, y, c), (x, y, 1 - c)
        chips = [(1 - x, y), (x, 1 - y), (1 - x, 1 - y)]
        plans = []
        for k, shard in enumerate(self.operands):
            rows = shard.shape[0]

            def blk(pos, k=k, rows=rows):
                return dst[k].at[pl.ds((4 * pos[0] + 2 * pos[1] + pos[2]) * rows, rows), :]

            def rc(s, block, to, source=None, k=k, blk=blk):
                return pltpu.make_async_remote_copy(
                    src_ref=blk(block) if source is None else source, dst_ref=blk(block),
                    send_sem=send.at[k, s], recv_sem=recv.at[k, s], device_id=to, device_id_type=MESH)

            plans.append(dict(
                local=pltpu.make_async_copy(src[k], blk(me), loc.at[k]),
                first=[rc(0, me, sib, src[k])] + [rc(1 + j, me, (*chip, c), src[k]) for j, chip in enumerate(chips)],
                landed=[rc(1 + j, (*chip, c), me) for j, chip in enumerate(chips)],
                passed=[rc(4 + j, (*chip, c), sib) for j, chip in enumerate(chips)],
                from_sib=[rc(0, sib, me)] + [rc(4 + j, (*chip, 1 - c), me) for j, chip in enumerate(chips)]))
        return plans

    def start(self, src, dst, send, recv, loc):
        for plan in self._plan(src, dst, send, recv, loc):
            plan["local"].start()
            for cp in plan["first"]:
                cp.start()

    def forward(self, src, dst, send, recv, loc):
        for plan in self._plan(src, dst, send, recv, loc):
            for landed, passed in zip(plan["landed"], plan["passed"]):
                landed.wait_recv()
                passed.start()

    def finish(self, src, dst, send, recv, loc):
        for plan in self._plan(src, dst, send, recv, loc):
            for cp in plan["from_sib"]:
                cp.wait_recv()
            for cp in plan["first"] + plan["passed"]:
                cp.wait_send()
            plan["local"].wait()


class _ScatterRows:
    def __init__(self, grads):
        self.operands = list(grads)
        n = len(grads)
        self.out_shape = [jax.ShapeDtypeStruct((N_DEV, g.shape[0] // N_DEV, g.shape[1]), g.dtype) for g in grads]
        self.scratch = [pltpu.SemaphoreType.DMA((n, N_DEV - 1)), pltpu.SemaphoreType.DMA((n, N_DEV - 1)),
                        pltpu.SemaphoreType.DMA((n,))]

    def _plan(self, src, dst, send, recv, loc):
        x, y, c, me = _my_position()
        copies = []
        for k, grad in enumerate(self.operands):
            rows = grad.shape[0] // N_DEV
            copies.append(pltpu.make_async_copy(src[k].at[pl.ds(me * rows, rows), :], dst[k].at[me], loc.at[k]))
            for p in range(1, N_DEV):
                px, py, pc = _peer(x, y, c, p)
                copies.append(pltpu.make_async_remote_copy(
                    src_ref=src[k].at[pl.ds((4 * px + 2 * py + pc) * rows, rows), :], dst_ref=dst[k].at[me],
                    send_sem=send.at[k, p - 1], recv_sem=recv.at[k, p - 1], device_id=(px, py, pc), device_id_type=MESH))
        return copies

    def start(self, src, dst, send, recv, loc):
        for cp in self._plan(src, dst, send, recv, loc):
            cp.start()

    def forward(self, src, dst, send, recv, loc):
        pass

    def finish(self, src, dst, send, recv, loc):
        for cp in self._plan(src, dst, send, recv, loc):
            cp.wait()


class _Exchanges:
    def __init__(self, parts):
        self.parts = list(parts)
        self.operands = [op for part in self.parts for op in part.operands]
        self.out_shape = [shp for part in self.parts for shp in part.out_shape]
        self.scratch = [scr for part in self.parts for scr in part.scratch]

    def _each(self, src, dst, sems):
        at, sem_at = 0, 0
        for part in self.parts:
            n, n_sem = len(part.operands), len(part.scratch)
            yield part, src[at:at + n], dst[at:at + n], sems[sem_at:sem_at + n_sem]
            at, sem_at = at + n, sem_at + n_sem

    def start(self, src, dst, *sems):
        for part, part_src, part_dst, part_sems in self._each(src, dst, sems):
            part.start(part_src, part_dst, *part_sems)

    def forward(self, src, dst, *sems):
        for part, part_src, part_dst, part_sems in self._each(src, dst, sems):
            part.forward(part_src, part_dst, *part_sems)

    def finish(self, src, dst, *sems):
        for part, part_src, part_dst, part_sems in self._each(src, dst, sems):
            part.finish(part_src, part_dst, *part_sems)


_ANY = pl.BlockSpec(memory_space=pl.ANY)


def _call(body, *, name, grid, in_specs, out_specs, out_shape, operands, scratch_shapes=(), parallel=False, comm=None):
    n_axes = len(grid)
    if comm is None:
        outs = pl.pallas_call(body, name=name, grid=grid, out_shape=list(out_shape), in_specs=list(in_specs),
                              out_specs=list(out_specs), scratch_shapes=list(scratch_shapes),
                              compiler_params=_params(n_axes, parallel))(*operands)
        return list(outs), None
    n_in, n_out, n_scr, n_c = len(in_specs), len(out_specs), len(scratch_shapes), len(comm.operands)
    total = math.prod(grid)

    def hosted(*refs):
        ins, c_src = refs[:n_in], refs[n_in:n_in + n_c]
        outs, c_dst = refs[n_in + n_c:n_in + n_c + n_out], refs[n_in + n_c + n_out:n_in + 2 * n_c + n_out]
        scr, sems = refs[n_in + 2 * n_c + n_out:n_in + 2 * n_c + n_out + n_scr], refs[n_in + 2 * n_c + n_out + n_scr:]
        step = pl.program_id(0)
        for axis in range(1, n_axes):
            step = step * grid[axis] + pl.program_id(axis)

        @pl.when(step == 0)
        def _():
            comm.start(c_src, c_dst, *sems)

        @pl.when(step == max(total - FORWARD_STEPS, 0))
        def _():
            comm.forward(c_src, c_dst, *sems)

        body(*ins, *outs, *scr)

        @pl.when(step == total - 1)
        def _():
            comm.finish(c_src, c_dst, *sems)

    res = pl.pallas_call(hosted, name=name, grid=grid, out_shape=list(out_shape) + comm.out_shape,
                         in_specs=list(in_specs) + [_ANY] * n_c, out_specs=list(out_specs) + [_ANY] * n_c,
                         scratch_shapes=list(scratch_shapes) + comm.scratch,
                         compiler_params=_params(n_axes, False))(*operands, *comm.operands)
    return list(res[:n_out]), list(res[n_out:])


def _all_gather_small(v, name):
    rows = v.shape[0]

    def body(v_ref, all_ref, sum_ref, send_sems, recv_sems):
        x, y, c, me = _my_position()
        all_ref[me] = v_ref[...]
        copies = []
        for p in range(1, N_DEV):
            cp = pltpu.make_async_remote_copy(
                src_ref=v_ref, dst_ref=all_ref.at[me], send_sem=send_sems.at[p - 1], recv_sem=recv_sems.at[p - 1],
                device_id=_peer(x, y, c, p), device_id_type=MESH)
            cp.start()
            copies.append(cp)
        for cp in copies:
            cp.wait()
        acc = all_ref[0]
        for d in range(1, N_DEV):
            acc = acc + all_ref[d]
        sum_ref[...] = acc

    return pl.pallas_call(
        body, name=name,
        out_shape=[jax.ShapeDtypeStruct((N_DEV, rows, LANES), F32), jax.ShapeDtypeStruct((rows, LANES), F32)],
        in_specs=[pl.BlockSpec(memory_space=pltpu.VMEM)],
        out_specs=[pl.BlockSpec(memory_space=pltpu.VMEM)] * 2,
        scratch_shapes=[pltpu.SemaphoreType.DMA((N_DEV - 1,)), pltpu.SemaphoreType.DMA((N_DEV - 1,))],
        compiler_params=pltpu.CompilerParams(vmem_limit_bytes=VMEM_LIMIT),
    )(v)


def _sum_gathered(gathered, name):
    rows = gathered.shape[1]

    def body(g_ref, o_ref):
        acc = g_ref[0]
        for d in range(1, N_DEV):
            acc = acc + g_ref[d]
        o_ref[...] = acc

    return pl.pallas_call(
        body, name=name, out_shape=jax.ShapeDtypeStruct((rows, LANES), F32),
        in_specs=[pl.BlockSpec(memory_space=pltpu.VMEM)], out_specs=pl.BlockSpec(memory_space=pltpu.VMEM),
        compiler_params=pltpu.CompilerParams(vmem_limit_bytes=VMEM_LIMIT),
    )(gathered)


def _pack_small(parts):
    flat = jnp.concatenate([p.reshape(-1).astype(F32) for p in parts])
    total = flat.shape[0]
    padded = -(-total // (8 * LANES)) * (8 * LANES)
    flat = jnp.pad(flat, (0, padded - total))
    return flat.reshape(padded // LANES, LANES)


def _unpack_small(packed, shapes, lead=()):
    flat = packed.reshape(lead + (-1,))
    out, off = [], 0
    for shp in shapes:
        size = math.prod(shp)
        out.append(flat[..., off:off + size].reshape(lead + tuple(shp)))
        off += size
    return out


def _prologue(c_rows, convw_rows, ada_w, ada_b_cols, gather):
    depth, _, cols = ada_w.shape
    n_c = len(gather.operands)
    sub = 8

    def body(c_ref, cw_ref, w_ref, b_ref, *rest):
        g_src, (c_all_ref, cw_all_ref, ada_ref), g_dst = rest[:n_c], rest[n_c:n_c + 3], rest[n_c + 3:2 * n_c + 3]
        ada_local, send_sems, recv_sems = rest[2 * n_c + 3:2 * n_c + 6]
        g_sems = rest[2 * n_c + 6:]
        x, y, c, me = _my_position()
        gather.start(g_src, g_dst, *g_sems)

        def to_all(k, src_ref, dst_ref):
            copies = []
            for p in range(1, N_DEV):
                copies.append(pltpu.make_async_remote_copy(
                    src_ref=src_ref, dst_ref=dst_ref.at[me], send_sem=send_sems.at[k, p - 1], recv_sem=recv_sems.at[k, p - 1],
                    device_id=_peer(x, y, c, p), device_id_type=MESH))
            return copies

        first = to_all(0, c_ref, c_all_ref) + to_all(1, cw_ref, cw_all_ref)
        c_all_ref[me] = c_ref[...]
        cw_all_ref[me] = cw_ref[...]
        for cp in first:
            cp.start()
        for cp in first:
            cp.wait()
        cv = c_all_ref[...].reshape(N_DEV * sub, D_MODEL)
        act = (cv * _sigmoid(cv)).astype(BF16)
        for l in range(depth):
            ada_local[l] = _dot(act, w_ref[l].astype(BF16)) + b_ref[l]
        ada_ref[me] = ada_local[:, pl.ds(pl.multiple_of(me * sub, sub), sub), :]
        rows_out = []
        for p in range(1, N_DEV):
            px, py, pc = _peer(x, y, c, p)
            rows = pl.ds(pl.multiple_of((4 * px + 2 * py + pc) * sub, sub), sub)
            rows_out.append(pltpu.make_async_remote_copy(
                src_ref=ada_local.at[:, rows, :], dst_ref=ada_ref.at[me], send_sem=send_sems.at[2, p - 1],
                recv_sem=recv_sems.at[2, p - 1], device_id=(px, py, pc), device_id_type=MESH))
        for cp in rows_out:
            cp.start()
        for cp in rows_out:
            cp.wait()
        gather.forward(g_src, g_dst, *g_sems)
        gather.finish(g_src, g_dst, *g_sems)

    vmem = pl.BlockSpec(memory_space=pltpu.VMEM)
    outs = pl.pallas_call(
        body, name="prologue",
        out_shape=[jax.ShapeDtypeStruct((N_DEV, sub, D_MODEL), F32), jax.ShapeDtypeStruct((N_DEV, sub, LANES), F32),
                   jax.ShapeDtypeStruct((N_DEV, depth, sub, cols), F32)] + gather.out_shape,
        in_specs=[vmem] * 4 + [_ANY] * n_c, out_specs=[vmem] * 3 + [_ANY] * n_c,
        scratch_shapes=[pltpu.VMEM((depth, N_DEV * sub, cols), F32), pltpu.SemaphoreType.DMA((3, N_DEV - 1)),
                        pltpu.SemaphoreType.DMA((3, N_DEV - 1))] + gather.scratch,
        compiler_params=pltpu.CompilerParams(vmem_limit_bytes=VMEM_LIMIT),
    )(c_rows, convw_rows, ada_w, ada_b_cols, *gather.operands)
    return outs[0], outs[1], outs[2], list(outs[3:])


def _ada_backward(c_all, d_ada_cols, d_ada_all):
    nb = c_all.shape[0]
    cols = d_ada_cols.shape[2]
    full = d_ada_all.shape[2]

    def body(c_ref, dc_ref, da_ref, gw_ref, gb_ref):
        cv = c_ref[...]
        act = (cv * _sigmoid(cv)).astype(BF16)
        gw_ref[0] = _dot_tn(act, dc_ref[0].astype(BF16))
        gb_ref[0] = jnp.sum(da_ref[0], axis=0, keepdims=True)

    return pl.pallas_call(
        body, name="ada_backward", grid=(DEPTH,),
        out_shape=[jax.ShapeDtypeStruct((DEPTH, D_MODEL, cols), F32), jax.ShapeDtypeStruct((DEPTH, 1, full), F32)],
        in_specs=[pl.BlockSpec((nb, D_MODEL), lambda l: (0, 0)),
                  pl.BlockSpec((1, nb, cols), lambda l: (l, 0, 0)),
                  pl.BlockSpec((1, nb, full), lambda l: (l, 0, 0))],
        out_specs=[pl.BlockSpec((1, D_MODEL, cols), lambda l: (l, 0, 0)),
                   pl.BlockSpec((1, 1, full), lambda l: (l, 0, 0))],
        compiler_params=_params(),
    )(c_all, d_ada_cols, d_ada_all)


def _rms(xv):
    return lax.rsqrt(jnp.mean(xv * xv, axis=-1, keepdims=True) + EPS)


def _normmod_matmul(x, gnorm, scale1p, shift, w_t, seq, name, swiglu=False, comm=None):
    tokens, n_out = x.shape[0], w_t.shape[0]
    tm = _tile_rows(seq)
    per_seq = seq // tm
    width = n_out // 2 if swiglu else n_out
    n_chunks = width // MXU_N

    def body(x_ref, g_ref, sc_ref, sh_ref, w_ref, h_ref, *o_refs):
        xv = x_ref[...]
        h = (xv * _rms(xv) * g_ref[...]) * sc_ref[0] + sh_ref[0]
        h_ref[...] = h.astype(BF16)
        for ck in range(n_chunks):
            cs = slice(ck * MXU_N, (ck + 1) * MXU_N)
            if swiglu:
                act_ref, silu_ref, dact_ref = o_refs
                g = _dot_nt(h_ref[...], w_ref[cs, :])
                u = _dot_nt(h_ref[...], w_ref[width + ck * MXU_N:width + (ck + 1) * MXU_N, :])
                sig = _sigmoid(g)
                silu = g * sig
                act_ref[:, cs] = (silu * u).astype(BF16)
                silu_ref[:, cs] = silu.astype(BF16)
                dact_ref[:, cs] = (u * (sig + silu * (1.0 - sig))).astype(BF16)
            else:
                o_refs[0][:, cs] = _dot_nt(h_ref[...], w_ref[cs, :]).astype(BF16)

    n_res = 3 if swiglu else 1
    per_batch = pl.BlockSpec((1, 1, D_MODEL), lambda i: (i // per_seq, 0, 0))
    outs, got = _call(
        body, name=name, grid=(tokens // tm,),
        out_shape=[jax.ShapeDtypeStruct((tokens, D_MODEL), BF16)] + [jax.ShapeDtypeStruct((tokens, width), BF16)] * n_res,
        in_specs=[pl.BlockSpec((tm, D_MODEL), lambda i: (i, 0)), _resident((1, D_MODEL)), per_batch, per_batch,
                  _resident(w_t.shape)],
        out_specs=[pl.BlockSpec((tm, D_MODEL), lambda i: (i, 0))] + [pl.BlockSpec((tm, width), lambda i: (i, 0))] * n_res,
        operands=(x, gnorm, scale1p, shift, w_t), parallel=True, comm=comm)
    return (*outs, got)


def _matmul_residual(src, w, x, gate, scale, seq, name, comm=None):
    tokens, k_dim = x.shape[0], w.shape[0]
    tm = _tile_rows(seq)
    per_seq = seq // tm

    def body(s_ref, w_ref, x_ref, gate_ref, xo_ref, f_ref):
        f = _dot(s_ref[...], w_ref[...])
        f_ref[...] = f.astype(BF16)
        xo_ref[...] = x_ref[...] + (scale * gate_ref[0]) * f

    (x_out, f), got = _call(
        body, name=name, grid=(tokens // tm,),
        out_shape=[jax.ShapeDtypeStruct((tokens, D_MODEL), F32), jax.ShapeDtypeStruct((tokens, D_MODEL), BF16)],
        in_specs=[pl.BlockSpec((tm, k_dim), lambda i: (i, 0)), _resident(w.shape),
                  pl.BlockSpec((tm, D_MODEL), lambda i: (i, 0)),
                  pl.BlockSpec((1, 1, D_MODEL), lambda i: (i // per_seq, 0, 0))],
        out_specs=[pl.BlockSpec((tm, D_MODEL), lambda i: (i, 0))] * 2,
        operands=(src, w, x, gate), parallel=True, comm=comm)
    return x_out, f, got


def _loss_tile(xv, target, gn):
    r = _rms(xv)
    xn = xv * r
    err = xn * gn - target
    loss = (0.5 / D_MODEL) * jnp.sum(err * err, axis=0, keepdims=True)
    dyv = err * (1.0 / D_MODEL)
    dg = jnp.sum(dyv * xn, axis=0, keepdims=True)
    dxn = dyv * gn
    dx = r * (dxn - xn * jnp.mean(dxn * xn, axis=-1, keepdims=True))
    return loss, dx, dg


def _ffn_forward(x, gnorm, scale1p, shift, w_gu_t, w_down, gate, scale, seq, name, loss_head=None, comm=None):
    tokens, width = x.shape[0], w_down.shape[0]
    tm = _tile_rows(seq)
    per_seq = seq // tm
    n_chunks = width // MXU_N

    def body(x_ref, g_ref, sc_ref, sh_ref, wgu_ref, wd_ref, gate_ref, *rest):
        if loss_head:
            t_ref, gf_ref, h_ref, act_ref, silu_ref, dact_ref, xo_ref, f_ref, dgf_ref, loss_ref = rest
        else:
            h_ref, act_ref, silu_ref, dact_ref, xo_ref, f_ref = rest
        xv = x_ref[...]
        h = (xv * _rms(xv) * g_ref[...]) * sc_ref[0] + sh_ref[0]
        h_ref[...] = h.astype(BF16)
        for ck in range(n_chunks):
            cs = slice(ck * MXU_N, (ck + 1) * MXU_N)
            g = _dot_nt(h_ref[...], wgu_ref[cs, :])
            u = _dot_nt(h_ref[...], wgu_ref[width + ck * MXU_N:width + (ck + 1) * MXU_N, :])
            sig = _sigmoid(g)
            silu = g * sig
            act_ref[:, cs] = (silu * u).astype(BF16)
            silu_ref[:, cs] = silu.astype(BF16)
            dact_ref[:, cs] = (u * (sig + silu * (1.0 - sig))).astype(BF16)
        f = _dot(act_ref[...], wd_ref[...])
        f_ref[...] = f.astype(BF16)
        x_out = xv + (scale * gate_ref[0]) * f
        if loss_head:
            i = pl.program_id(0)
            loss, dx, dg = _loss_tile(x_out, t_ref[...], gf_ref[...])
            xo_ref[...] = dx

            @pl.when(i == 0)
            def _():
                dgf_ref[...] = dg
                loss_ref[...] = loss

            @pl.when(i != 0)
            def _():
                dgf_ref[...] = dgf_ref[...] + dg
                loss_ref[...] = loss_ref[...] + loss
        else:
            xo_ref[...] = x_out

    row = lambda i: (i, 0)
    per_batch = pl.BlockSpec((1, 1, D_MODEL), lambda i: (i // per_seq, 0, 0))
    tile = lambda cols: pl.BlockSpec((tm, cols), row)
    wide = jax.ShapeDtypeStruct((tokens, width), BF16)
    fixed = pl.BlockSpec((1, D_MODEL), lambda i: (0, 0))
    vec = jax.ShapeDtypeStruct((1, D_MODEL), F32)
    outs, got = _call(
        body, name=name, grid=(tokens // tm,),
        out_shape=[jax.ShapeDtypeStruct((tokens, D_MODEL), BF16), wide, wide, wide,
                   jax.ShapeDtypeStruct((tokens, D_MODEL), F32), jax.ShapeDtypeStruct((tokens, D_MODEL), BF16)]
        + ([vec, vec] if loss_head else []),
        in_specs=[tile(D_MODEL), _resident((1, D_MODEL)), per_batch, per_batch, _resident(w_gu_t.shape),
                  _resident(w_down.shape), per_batch] + ([tile(D_MODEL), _resident((1, D_MODEL))] if loss_head else []),
        out_specs=[tile(D_MODEL), tile(width), tile(width), tile(width), tile(D_MODEL), tile(D_MODEL)]
        + ([fixed, fixed] if loss_head else []),
        operands=(x, gnorm, scale1p, shift, w_gu_t, w_down, gate) + (tuple(loss_head) if loss_head else ()),
        parallel=not loss_head, comm=comm)
    return (*outs, got)


def _residual_backward(dy, gate, f, w, scale, seq, name, silu=None, dact=None, comm=None):
    tokens, k_dim = dy.shape[0], w.shape[0]
    batch = tokens // seq
    tm = _tile_rows(seq)
    per_seq = seq // tm
    n_chunks = k_dim // MXU_N
    swiglu = silu is not None

    def body(*refs):
        if swiglu:
            dy_ref, gate_ref, f_ref, silu_ref, dact_ref, w_ref, df_ref, dgate_ref, dgu_ref = refs
        else:
            dy_ref, gate_ref, f_ref, w_ref, df_ref, dgate_ref, dsrc_ref = refs
        i = pl.program_id(0)
        dy_v = dy_ref[...]
        df_ref[...] = ((scale * gate_ref[0]) * dy_v).astype(BF16)
        part = scale * jnp.sum(dy_v * f_ref[...].astype(F32), axis=0, keepdims=True)

        @pl.when(i % per_seq == 0)
        def _():
            dgate_ref[0] = part

        @pl.when(i % per_seq != 0)
        def _():
            dgate_ref[0] = dgate_ref[0] + part

        if swiglu:
            for ck in range(n_chunks):
                cs = slice(ck * MXU_N, (ck + 1) * MXU_N)
                cu = slice(k_dim + ck * MXU_N, k_dim + (ck + 1) * MXU_N)
                da = _dot_nt(df_ref[...], w_ref[cs, :])
                dgu_ref[:, cs] = (da * dact_ref[:, cs].astype(F32)).astype(BF16)
                dgu_ref[:, cu] = (da * silu_ref[:, cs].astype(F32)).astype(BF16)
        else:
            dsrc_ref[...] = _dot_nt(df_ref[...], w_ref[...])

    row = lambda i: (i, 0)
    per_batch = pl.BlockSpec((1, 1, D_MODEL), lambda i: (i // per_seq, 0, 0))
    in_specs = [pl.BlockSpec((tm, D_MODEL), row), per_batch, pl.BlockSpec((tm, D_MODEL), row)]
    out_shape = [jax.ShapeDtypeStruct((tokens, D_MODEL), BF16), jax.ShapeDtypeStruct((batch, 1, D_MODEL), F32)]
    out_specs = [pl.BlockSpec((tm, D_MODEL), row), per_batch]
    if swiglu:
        in_specs += [pl.BlockSpec((tm, k_dim), row)] * 2
        operands = (dy, gate, f, silu, dact, w)
        out_shape += [jax.ShapeDtypeStruct((tokens, 2 * k_dim), BF16)]
        out_specs += [pl.BlockSpec((tm, 2 * k_dim), row)]
    else:
        operands = (dy, gate, f, w)
        out_shape += [jax.ShapeDtypeStruct((tokens, k_dim), F32)]
        out_specs += [pl.BlockSpec((tm, k_dim), row)]
    in_specs += [_resident(w.shape)]
    outs, got = _call(body, name=name, grid=(tokens // tm,), out_shape=out_shape, in_specs=in_specs,
                      out_specs=out_specs, operands=operands, comm=comm)
    return (*outs, got)


def _matmul_normmod_backward(dsrc, w_t, x, dy, gnorm, scale1p, seq, name, comm=None):
    tokens, k_dim = dsrc.shape
    batch = tokens // seq
    tm = _tile_rows(seq)
    per_seq = seq // tm

    def body(ds_ref, w_ref, x_ref, dy_ref, g_ref, sc_ref, dx_ref, dsh_ref, dsc_ref, dg_ref):
        i = pl.program_id(0)
        dh = _dot(ds_ref[...], w_ref[...])
        xv = x_ref[...]
        r = _rms(xv)
        xn = xv * r
        gn = g_ref[...]
        dsh = jnp.sum(dh, axis=0, keepdims=True)
        dsc = jnp.sum(dh * (xn * gn), axis=0, keepdims=True)
        dhn = dh * sc_ref[0]
        dg = jnp.sum(dhn * xn, axis=0, keepdims=True)
        dxn = dhn * gn
        dx_ref[...] = dy_ref[...] + r * (dxn - xn * jnp.mean(dxn * xn, axis=-1, keepdims=True))

        @pl.when(i % per_seq == 0)
        def _():
            dsh_ref[0] = dsh
            dsc_ref[0] = dsc

        @pl.when(i % per_seq != 0)
        def _():
            dsh_ref[0] = dsh_ref[0] + dsh
            dsc_ref[0] = dsc_ref[0] + dsc

        @pl.when(i == 0)
        def _():
            dg_ref[...] = dg

        @pl.when(i != 0)
        def _():
            dg_ref[...] = dg_ref[...] + dg

    row = lambda i: (i, 0)
    per_batch = pl.BlockSpec((1, 1, D_MODEL), lambda i: (i // per_seq, 0, 0))
    outs, got = _call(
        body, name=name, grid=(tokens // tm,),
        out_shape=[jax.ShapeDtypeStruct((tokens, D_MODEL), F32), jax.ShapeDtypeStruct((batch, 1, D_MODEL), F32),
                   jax.ShapeDtypeStruct((batch, 1, D_MODEL), F32), jax.ShapeDtypeStruct((1, D_MODEL), F32)],
        in_specs=[pl.BlockSpec((tm, k_dim), row), _resident(w_t.shape), pl.BlockSpec((tm, D_MODEL), row),
                  pl.BlockSpec((tm, D_MODEL), row), _resident((1, D_MODEL)), per_batch],
        out_specs=[pl.BlockSpec((tm, D_MODEL), row), per_batch, per_batch, pl.BlockSpec((1, D_MODEL), lambda i: (0, 0))],
        operands=(dsrc, w_t, x, dy, gnorm, scale1p), comm=comm)
    return (*outs, got)


def _ffn_backward(dy, gate, f, silu, dact, w_down, w_gu_t, x, gnorm, scale1p, scale, seq, name, comm=None):
    tokens, k_dim = dy.shape[0], w_down.shape[0]
    batch = tokens // seq
    tm = min(256, seq)
    per_seq = seq // tm
    n_chunks = k_dim // MXU_N

    def body(dy_ref, gate_ref, f_ref, silu_ref, dact_ref, wd_ref, wgu_ref, x_ref, g_ref, sc_ref,
             df_ref, dgate_ref, dgu_ref, dx_ref, dsh_ref, dsc_ref, dg_ref):
        i = pl.program_id(0)
        dy_v = dy_ref[...]
        df_ref[...] = ((scale * gate_ref[0]) * dy_v).astype(BF16)
        dgate = scale * jnp.sum(dy_v * f_ref[...].astype(F32), axis=0, keepdims=True)
        for ck in range(n_chunks):
            cs = slice(ck * MXU_N, (ck + 1) * MXU_N)
            cu = slice(k_dim + ck * MXU_N, k_dim + (ck + 1) * MXU_N)
            da = _dot_nt(df_ref[...], wd_ref[cs, :])
            dgu_ref[:, cs] = (da * dact_ref[:, cs].astype(F32)).astype(BF16)
            dgu_ref[:, cu] = (da * silu_ref[:, cs].astype(F32)).astype(BF16)
        dh = _dot(dgu_ref[...], wgu_ref[...])
        xv = x_ref[...]
        r = _rms(xv)
        xn = xv * r
        gn = g_ref[...]
        dsh = jnp.sum(dh, axis=0, keepdims=True)
        dsc = jnp.sum(dh * (xn * gn), axis=0, keepdims=True)
        dhn = dh * sc_ref[0]
        dg = jnp.sum(dhn * xn, axis=0, keepdims=True)
        dxn = dhn * gn
        dx_ref[...] = dy_v + r * (dxn - xn * jnp.mean(dxn * xn, axis=-1, keepdims=True))

        @pl.when(i % per_seq == 0)
        def _():
            dgate_ref[0] = dgate
            dsh_ref[0] = dsh
            dsc_ref[0] = dsc

        @pl.when(i % per_seq != 0)
        def _():
            dgate_ref[0] = dgate_ref[0] + dgate
            dsh_ref[0] = dsh_ref[0] + dsh
            dsc_ref[0] = dsc_ref[0] + dsc

        @pl.when(i == 0)
        def _():
            dg_ref[...] = dg

        @pl.when(i != 0)
        def _():
            dg_ref[...] = dg_ref[...] + dg

    row = lambda i: (i, 0)
    per_batch = pl.BlockSpec((1, 1, D_MODEL), lambda i: (i // per_seq, 0, 0))
    tile = lambda width: pl.BlockSpec((tm, width), row)
    vec = jax.ShapeDtypeStruct((batch, 1, D_MODEL), F32)
    outs, got = _call(
        body, name=name, grid=(tokens // tm,),
        out_shape=[jax.ShapeDtypeStruct((tokens, D_MODEL), BF16), vec, jax.ShapeDtypeStruct((tokens, 2 * k_dim), BF16),
                   jax.ShapeDtypeStruct((tokens, D_MODEL), F32), vec, vec, jax.ShapeDtypeStruct((1, D_MODEL), F32)],
        in_specs=[tile(D_MODEL), per_batch, tile(D_MODEL), tile(k_dim), tile(k_dim), _resident(w_down.shape),
                  _resident(w_gu_t.shape), tile(D_MODEL), _resident((1, D_MODEL)), per_batch],
        out_specs=[tile(D_MODEL), per_batch, tile(2 * k_dim), tile(D_MODEL), per_batch, per_batch,
                   pl.BlockSpec((1, D_MODEL), lambda i: (0, 0))],
        operands=(dy, gate, f, silu, dact, w_down, w_gu_t, x, gnorm, scale1p), comm=comm)
    return (*outs, got)


def _weight_grad(a, b, seq, name, comm=None):
    tokens, n_out = a.shape
    tn = MXU_N

    def body(a_ref, b_ref, o_ref):
        o_ref[...] = _dot_tn(a_ref[...], b_ref[...]).astype(BF16)

    (out,), got = _call(
        body, name=name, grid=(n_out // tn,),
        out_shape=[jax.ShapeDtypeStruct((n_out, D_MODEL), BF16)],
        in_specs=[pl.BlockSpec((tokens, tn), lambda j: (0, j)), _resident((tokens, D_MODEL))],
        out_specs=[pl.BlockSpec((tn, D_MODEL), lambda j: (j, 0))],
        operands=(a, b), parallel=True, comm=comm)
    return out, got


def _group_mean(v, bd):
    hi = v.astype(BF16)
    lo = (v - hi.astype(F32)).astype(BF16)
    return _dot(hi, bd) + _dot(lo, bd)


def _sgu_forward(pm_ref, wm_ref, bias_ref, lng_ref, lnb_ref, bd_ref, mixed_scr, n_sub):
    ua = pm_ref[:, 0:D_A].astype(F32)
    va = pm_ref[:, D_A:2 * D_A].astype(F32)
    u_act = _gelu(ua)
    v_act = _gelu(va)
    bd = bd_ref[...]
    vc = v_act - _group_mean(v_act, bd)
    rstd = lax.rsqrt(_group_mean(vc * vc, bd) + EPS)
    vhat = vc * rstd
    vln = vhat * lng_ref[...] + lnb_ref[...]
    left = lax.broadcasted_iota(jnp.int32, (CHUNK, LANES), 1) < HEAD_DIM
    for q in range(n_sub):
        rows = slice(q * CHUNK, (q + 1) * CHUNK)
        for p in range(N_HEADS // 2):
            cols = slice(p * LANES, (p + 1) * LANES)
            vp = vln[rows, cols]
            v_l = jnp.where(left, vp, 0.0).astype(BF16)
            v_r = jnp.where(left, 0.0, vp).astype(BF16)
            mixed_scr[rows, cols] = _dot(wm_ref[2 * p], v_l) + _dot(wm_ref[2 * p + 1], v_r) + bias_ref[:, cols]
    return ua, va, u_act, vhat, rstd, vln


def _halo_specs(tm, tokens, width):
    prev = pl.BlockSpec((HALO, width), lambda i: (jnp.maximum(i * (tm // HALO) - 1, 0), 0))
    nxt = pl.BlockSpec((HALO, width), lambda i: (jnp.minimum((i + 1) * (tm // HALO), tokens // HALO - 1), 0))
    return prev, nxt


def _mixer_forward(proj, wm, bias_full, lng, lnb, convw, og, bd, seq, name):
    tokens = proj.shape[0]
    tm = _tile_rows(seq)
    per_seq = seq // tm
    n_sub = tm // CHUNK

    def body(pm_ref, pp_ref, wm_ref, bias_ref, lng_ref, lnb_ref, cw_ref, og_ref, bd_ref, y_ref, mixed_scr):
        i = pl.program_id(0)
        first = (i % per_seq) == 0
        _, _, u_act, _, _, _ = _sgu_forward(pm_ref, wm_ref, bias_ref, lng_ref, lnb_ref, bd_ref, mixed_scr, n_sub)
        ya = u_act * mixed_scr[...]
        y_ref[:, 0:D_A] = (ya * _rms(ya) * og_ref[:, 0:D_A]).astype(BF16)

        bg = pm_ref[:, 2 * D_A:3 * D_A].astype(F32)
        z = pm_ref[:, 3 * D_A:4 * D_A].astype(F32) * pm_ref[:, 4 * D_A:5 * D_A].astype(F32)
        zp = pp_ref[:, 3 * D_A:4 * D_A].astype(F32) * pp_ref[:, 4 * D_A:5 * D_A].astype(F32)
        zp = jnp.where(first, 0.0, zp)
        zext = jnp.concatenate([zp, z], axis=0)
        z1 = pltpu.roll(zext, 1, 0)[HALO:]
        z2 = pltpu.roll(zext, 2, 0)[HALO:]
        conv = cw_ref[0:1, :] * z2 + cw_ref[1:2, :] * z1 + cw_ref[2:3, :] * z
        yb = bg * conv
        y_ref[:, D_A:2 * D_A] = (yb * _rms(yb) * og_ref[:, D_A:2 * D_A]).astype(BF16)

    prev, _ = _halo_specs(tm, tokens, D_PROJ)
    (y,), _ = _call(
        body, name=name, grid=(tokens // tm,),
        out_shape=[jax.ShapeDtypeStruct((tokens, D_MODEL), BF16)],
        in_specs=[pl.BlockSpec((tm, D_PROJ), lambda i: (i, 0)), prev, _resident(wm.shape), _resident(bias_full.shape),
                  _resident(lng.shape), _resident(lnb.shape), _resident(convw.shape), _resident(og.shape),
                  _resident(bd.shape)],
        out_specs=[pl.BlockSpec((tm, D_MODEL), lambda i: (i, 0))],
        scratch_shapes=[pltpu.VMEM((tm, D_A), F32)],
        operands=(proj, proj, wm, bias_full, lng, lnb, convw, og, bd), parallel=True)
    return y


def _mixer_backward(proj, dy, wm, bias_full, lng, lnb, convw, og, bd, causal, seq, name, comm=None):
    tokens = proj.shape[0]
    tm = _tile_rows(seq)
    per_seq = seq // tm
    n_sub = tm // CHUNK
    ext = tm + 2 * HALO

    def body(pm_ref, pp_ref, pn_ref, dy_ref, dyn_ref, wm_ref, bias_ref, lng_ref, lnb_ref, cw_ref, og_ref, bd_ref,
             causal_ref, dp_ref, dog_ref, dcw_ref, dlng_ref, dlnb_ref, dbias_ref, dwm_ref, mixed_scr, dvln_scr):
        i = pl.program_id(0)
        first = (i % per_seq) == 0
        last = (i % per_seq) == per_seq - 1

        @pl.when(i == 0)
        def _():
            dog_ref[...] = jnp.zeros_like(dog_ref)
            dcw_ref[...] = jnp.zeros_like(dcw_ref)
            dlng_ref[...] = jnp.zeros_like(dlng_ref)
            dlnb_ref[...] = jnp.zeros_like(dlnb_ref)
            dbias_ref[...] = jnp.zeros_like(dbias_ref)
            dwm_ref[...] = jnp.zeros_like(dwm_ref)

        ua, va, u_act, vhat, rstd, vln = _sgu_forward(pm_ref, wm_ref, bias_ref, lng_ref, lnb_ref, bd_ref, mixed_scr, n_sub)
        mixed = mixed_scr[...]
        ya = u_act * mixed
        ra = _rms(ya)
        yhat = ya * ra
        dya_in = dy_ref[:, 0:D_A]
        dog_ref[:, 0:D_A] = dog_ref[:, 0:D_A] + jnp.sum(dya_in * yhat, axis=0, keepdims=True)
        dyh = dya_in * og_ref[:, 0:D_A]
        dya = ra * (dyh - yhat * jnp.mean(dyh * yhat, axis=-1, keepdims=True))
        d_u = dya * mixed
        d_mixed = dya * u_act
        left = lax.broadcasted_iota(jnp.int32, (CHUNK, LANES), 1) < HEAD_DIM
        dbias = jnp.zeros((CHUNK, D_A), F32)
        for q in range(n_sub):
            rows = slice(q * CHUNK, (q + 1) * CHUNK)
            dbias = dbias + d_mixed[rows, :]
            for p in range(N_HEADS // 2):
                cols = slice(p * LANES, (p + 1) * LANES)
                dm = d_mixed[rows, cols]
                dm_l = jnp.where(left, dm, 0.0).astype(BF16)
                dm_r = jnp.where(left, 0.0, dm).astype(BF16)
                vp = vln[rows, cols].astype(BF16)
                dwm_ref[2 * p] = dwm_ref[2 * p] + causal_ref[...] * _dot_nt(dm_l, vp)
                dwm_ref[2 * p + 1] = dwm_ref[2 * p + 1] + causal_ref[...] * _dot_nt(dm_r, vp)
                dvln_scr[rows, cols] = _dot_tn(wm_ref[2 * p], dm_l) + _dot_tn(wm_ref[2 * p + 1], dm_r)
        dbias_ref[...] = dbias_ref[...] + dbias
        dvln = dvln_scr[...]
        dlng_ref[...] = dlng_ref[...] + jnp.sum(dvln * vhat, axis=0, keepdims=True)
        dlnb_ref[...] = dlnb_ref[...] + jnp.sum(dvln, axis=0, keepdims=True)
        dvh = dvln * lng_ref[...]
        bd = bd_ref[...]
        d_v = rstd * (dvh - _group_mean(dvh, bd) - vhat * _group_mean(dvh * vhat, bd))
        dp_ref[:, 0:D_A] = (d_u * _gelu_grad(ua)).astype(BF16)
        dp_ref[:, D_A:2 * D_A] = (d_v * _gelu_grad(va)).astype(BF16)

        def ext_cols(lo):
            cs = slice(lo, lo + D_A)
            return jnp.concatenate([pp_ref[:, cs], pm_ref[:, cs], pn_ref[:, cs]], axis=0).astype(F32)

        bg, cg, xb = ext_cols(2 * D_A), ext_cols(3 * D_A), ext_cols(4 * D_A)
        row = lax.broadcasted_iota(jnp.int32, (ext, D_A), 0)
        z = jnp.where(jnp.logical_and(first, row < HALO), 0.0, cg * xb)
        z1 = pltpu.roll(z, 1, 0)
        z2 = pltpu.roll(z, 2, 0)
        w0, w1, w2 = cw_ref[0:1, :], cw_ref[1:2, :], cw_ref[2:3, :]
        conv = w0 * z2 + w1 * z1 + w2 * z
        yb = bg * conv
        rb = _rms(yb)
        yhb = yb * rb
        dyn = jnp.where(last, 0.0, dyn_ref[:, D_A:2 * D_A])
        dyb_in = jnp.concatenate([jnp.zeros((HALO, D_A), F32), dy_ref[:, D_A:2 * D_A], dyn], axis=0)
        dyhb = dyb_in * og_ref[:, D_A:2 * D_A]
        dyb = rb * (dyhb - yhb * jnp.mean(dyhb * yhb, axis=-1, keepdims=True))
        d_conv = dyb * bg
        dz = w2 * d_conv + w1 * pltpu.roll(d_conv, ext - 1, 0) + w0 * pltpu.roll(d_conv, ext - 2, 0)
        main = slice(HALO, HALO + tm)
        dp_ref[:, 2 * D_A:3 * D_A] = (dyb * conv)[main].astype(BF16)
        dp_ref[:, 3 * D_A:4 * D_A] = (dz * xb)[main].astype(BF16)
        dp_ref[:, 4 * D_A:5 * D_A] = (dz * cg)[main].astype(BF16)
        dog_ref[:, D_A:2 * D_A] = dog_ref[:, D_A:2 * D_A] + jnp.sum((dyb_in * yhb)[main], axis=0, keepdims=True)
        dcm = d_conv[main]
        dcw_ref[0:1, :] = dcw_ref[0:1, :] + jnp.sum(dcm * z2[main], axis=0, keepdims=True)
        dcw_ref[1:2, :] = dcw_ref[1:2, :] + jnp.sum(dcm * z1[main], axis=0, keepdims=True)
        dcw_ref[2:3, :] = dcw_ref[2:3, :] + jnp.sum(dcm * z[main], axis=0, keepdims=True)

    prev_p, next_p = _halo_specs(tm, tokens, D_PROJ)
    _, next_d = _halo_specs(tm, tokens, D_MODEL)
    fixed2 = lambda shape: pl.BlockSpec(shape, lambda i: (0, 0))
    outs, got = _call(
        body, name=name, grid=(tokens // tm,),
        out_shape=[jax.ShapeDtypeStruct((tokens, D_PROJ), BF16), jax.ShapeDtypeStruct((1, D_MODEL), F32),
                   jax.ShapeDtypeStruct((8, D_A), F32), jax.ShapeDtypeStruct((1, D_A), F32),
                   jax.ShapeDtypeStruct((1, D_A), F32), jax.ShapeDtypeStruct((CHUNK, D_A), F32),
                   jax.ShapeDtypeStruct((N_HEADS, CHUNK, CHUNK), F32)],
        in_specs=[pl.BlockSpec((tm, D_PROJ), lambda i: (i, 0)), prev_p, next_p,
                  pl.BlockSpec((tm, D_MODEL), lambda i: (i, 0)), next_d,
                  _resident(wm.shape), _resident(bias_full.shape), _resident(lng.shape), _resident(lnb.shape),
                  _resident(convw.shape), _resident(og.shape), _resident(bd.shape), _resident(causal.shape)],
        out_specs=[pl.BlockSpec((tm, D_PROJ), lambda i: (i, 0)), fixed2((1, D_MODEL)), fixed2((8, D_A)),
                   fixed2((1, D_A)), fixed2((1, D_A)), fixed2((CHUNK, D_A)),
                   pl.BlockSpec((N_HEADS, CHUNK, CHUNK), lambda i: (0, 0, 0))],
        scratch_shapes=[pltpu.VMEM((tm, D_A), F32), pltpu.VMEM((tm, D_A), F32)],
        operands=(proj, proj, proj, dy, dy, wm, bias_full, lng, lnb, convw, og, bd, causal), comm=comm)
    return (*outs, got)


def _adamw_update(wv, gv, mv, vv):
    nm = ADAM_B1 * mv + (1.0 - ADAM_B1) * gv
    nv = ADAM_B2 * vv + (1.0 - ADAM_B2) * (gv * gv)
    m_hat = nm / (1.0 - ADAM_B1 ** ADAM_STEP)
    v_hat = nv / (1.0 - ADAM_B2 ** ADAM_STEP)
    return -ADAM_LR * (m_hat / (jnp.sqrt(v_hat) + ADAM_EPS) + ADAM_WD * wv), nm, nv


def _adamw_rows(recv, w, m, v, name):
    depth, rows, cols = w.shape
    tr = rows // 2
    last = rows // tr - 1

    def body(*refs):
        r_refs, (w_ref, m_ref, v_ref, g_ref, d_ref, nm_ref, nv_ref) = refs[:depth], refs[depth:]
        for l in range(depth):
            @pl.when(pl.program_id(0) == l)
            def _(r_ref=r_refs[l]):
                acc = r_ref[0].astype(F32)
                for d in range(1, N_DEV):
                    acc = acc + r_ref[d].astype(F32)
                g_ref[0] = acc
                d_ref[0], nm_ref[0], nv_ref[0] = _adamw_update(w_ref[0], acc, m_ref[0], v_ref[0])

    def slots(l):
        return pl.BlockSpec((N_DEV, tr, cols), lambda ll, i: (0, jnp.where(ll == l, i, jnp.where(ll < l, 0, last)), 0))

    spec = pl.BlockSpec((1, tr, cols), lambda ll, i: (ll, i, 0))
    return pl.pallas_call(
        body, name=name, grid=(depth, rows // tr),
        out_shape=[jax.ShapeDtypeStruct((depth, rows, cols), F32)] * 4,
        in_specs=[slots(l) for l in range(depth)] + [spec] * 3, out_specs=[spec] * 4,
        compiler_params=_params(2),
    )(*recv, w, m, v)


def _adamw(w, g, m, v, name):
    rows, cols = w.shape
    tr = max(t for t in range(8, 513, 8) if rows % t == 0)

    def body(w_ref, g_ref, m_ref, v_ref, d_ref, nm_ref, nv_ref):
        d_ref[...], nm_ref[...], nv_ref[...] = _adamw_update(w_ref[...], g_ref[...], m_ref[...], v_ref[...])

    spec = pl.BlockSpec((tr, cols), lambda i: (i, 0))
    return pl.pallas_call(
        body, name=name, grid=(rows // tr,),
        out_shape=[jax.ShapeDtypeStruct((rows, cols), F32)] * 3,
        in_specs=[spec] * 4, out_specs=[spec] * 3,
        compiler_params=_params(parallel=True),
    )(w, g, m, v)


def _adamw_many(ws, gs, ms, vs, name):
    n = len(ws)
    two_d = lambda a: a.reshape(-1, a.shape[-1])

    def body(*refs):
        w_refs, g_refs, m_refs, v_refs = refs[:n], refs[n:2 * n], refs[2 * n:3 * n], refs[3 * n:4 * n]
        d_refs, nm_refs, nv_refs = refs[4 * n:5 * n], refs[5 * n:6 * n], refs[6 * n:]
        for k in range(n):
            d_refs[k][...], nm_refs[k][...], nv_refs[k][...] = _adamw_update(
                w_refs[k][...], g_refs[k][...], m_refs[k][...], v_refs[k][...])

    flat = [two_d(a) for a in ws]
    outs = pl.pallas_call(
        body, name=name, out_shape=[jax.ShapeDtypeStruct(a.shape, F32) for a in flat] * 3,
        in_specs=[pl.BlockSpec(memory_space=pltpu.VMEM)] * (4 * n),
        out_specs=[pl.BlockSpec(memory_space=pltpu.VMEM)] * (3 * n),
        compiler_params=pltpu.CompilerParams(vmem_limit_bytes=VMEM_LIMIT),
    )(*flat, *[two_d(a) for a in gs], *[two_d(a) for a in ms], *[two_d(a) for a in vs])
    shaped = [o.reshape(ws[k % n].shape) for k, o in enumerate(outs)]
    return shaped[:n], shaped[n:2 * n], shaped[2 * n:]


def _adamw_nd(w, g, m, v, name):
    shape = w.shape
    two_d = (-1, shape[-1])
    d, nm, nv = _adamw(w.reshape(two_d), g.reshape(two_d), m.reshape(two_d), v.reshape(two_d), name)
    return d.reshape(shape), nm.reshape(shape), nv.reshape(shape)


def kernel(x, c, ada_w, ada_b, norm_ffn1_g, ffn1_w_gu, ffn1_w_down, norm_mix_g, mix_w_in, sgu_ln_g, sgu_ln_b, sgu_w_s, sgu_b, conv_w, out_norm_g, mix_w_out, norm_ffn2_g, ffn2_w_gu, ffn2_w_down, final_norm_g, loss_target, m_ada_w, m_ada_b, m_norm_ffn1_g, m_ffn1_w_gu, m_ffn1_w_down, m_norm_mix_g, m_mix_w_in, m_sgu_ln_g, m_sgu_ln_b, m_sgu_w_s, m_sgu_b, m_conv_w, m_out_norm_g, m_mix_w_out, m_norm_ffn2_g, m_ffn2_w_gu, m_ffn2_w_down, m_final_norm_g, v_ada_w, v_ada_b, v_norm_ffn1_g, v_ffn1_w_gu, v_ffn1_w_down, v_norm_mix_g, v_mix_w_in, v_sgu_ln_g, v_sgu_ln_b, v_sgu_w_s, v_sgu_b, v_conv_w, v_out_norm_g, v_mix_w_out, v_norm_ffn2_g, v_ffn2_w_gu, v_ffn2_w_down, v_final_norm_g):
    batch, seq, _ = x.shape
    tokens = batch * seq
    me = 4 * lax.axis_index("x") + 2 * lax.axis_index("y") + lax.axis_index("c")
    weights = dict(ada_w=ada_w, ada_b=ada_b, norm_ffn1_g=norm_ffn1_g, ffn1_w_gu=ffn1_w_gu, ffn1_w_down=ffn1_w_down,
                   norm_mix_g=norm_mix_g, mix_w_in=mix_w_in, sgu_ln_g=sgu_ln_g, sgu_ln_b=sgu_ln_b, sgu_w_s=sgu_w_s,
                   sgu_b=sgu_b, conv_w=conv_w, out_norm_g=out_norm_g, mix_w_out=mix_w_out, norm_ffn2_g=norm_ffn2_g,
                   ffn2_w_gu=ffn2_w_gu, ffn2_w_down=ffn2_w_down, final_norm_g=final_norm_g)
    mom1 = dict(ada_w=m_ada_w, ada_b=m_ada_b, norm_ffn1_g=m_norm_ffn1_g, ffn1_w_gu=m_ffn1_w_gu,
                ffn1_w_down=m_ffn1_w_down, norm_mix_g=m_norm_mix_g, mix_w_in=m_mix_w_in, sgu_ln_g=m_sgu_ln_g,
                sgu_ln_b=m_sgu_ln_b, sgu_w_s=m_sgu_w_s, sgu_b=m_sgu_b, conv_w=m_conv_w, out_norm_g=m_out_norm_g,
                mix_w_out=m_mix_w_out, norm_ffn2_g=m_norm_ffn2_g, ffn2_w_gu=m_ffn2_w_gu, ffn2_w_down=m_ffn2_w_down,
                final_norm_g=m_final_norm_g)
    mom2 = dict(ada_w=v_ada_w, ada_b=v_ada_b, norm_ffn1_g=v_norm_ffn1_g, ffn1_w_gu=v_ffn1_w_gu,
                ffn1_w_down=v_ffn1_w_down, norm_mix_g=v_norm_mix_g, mix_w_in=v_mix_w_in, sgu_ln_g=v_sgu_ln_g,
                sgu_ln_b=v_sgu_ln_b, sgu_w_s=v_sgu_w_s, sgu_b=v_sgu_b, conv_w=v_conv_w, out_norm_g=v_out_norm_g,
                mix_w_out=v_mix_w_out, norm_ffn2_g=v_norm_ffn2_g, ffn2_w_gu=v_ffn2_w_gu, ffn2_w_down=v_ffn2_w_down,
                final_norm_g=v_final_norm_g)

    big = ("ffn1_w_gu", "ffn1_w_down", "mix_w_in", "mix_w_out", "ffn2_w_gu", "ffn2_w_down")
    transposed = ("ffn1_w_gu", "mix_w_in", "ffn2_w_gu")
    as_rows = lambda nm, a: jnp.swapaxes(a, 1, 2) if nm in transposed else a
    shard = {(l, nm): as_rows(nm, weights[nm])[l].astype(BF16) for l in range(DEPTH) for nm in big}
    full_w = {}

    def gather_of(keys):
        return keys, _GatherRows([shard[k] for k in keys])

    def landed(plan, got):
        full_w.update(zip(plan[0], got))

    ada_cols = ada_w.shape[2]
    ada_b_cols = lax.dynamic_slice_in_dim(ada_b, me * ada_cols, ada_cols, axis=1).reshape(DEPTH, 1, ada_cols)
    plan = gather_of([(0, "ffn1_w_gu")])
    c_dev, convw_dev, ada_recv, got = _prologue(
        jnp.pad(c, ((0, 8 - batch), (0, 0))), jnp.pad(conv_w.reshape(-1), (0, 8 * LANES - conv_w.size)).reshape(8, LANES),
        ada_w, ada_b_cols, plan[1])
    landed(plan, got)
    c_all = c_dev[:, :batch].reshape(N_DEV * batch, D_MODEL)
    convw_all = convw_dev.reshape(N_DEV, -1)[:, :conv_w.size].reshape((N_DEV,) + conv_w.shape)
    convw_full = jnp.transpose(convw_all, (1, 2, 0, 3)).reshape(DEPTH, 3, D_A)
    ada_mine = jnp.transpose(ada_recv[:, :, :batch, :], (1, 2, 0, 3)).reshape(DEPTH, batch, N_MOD * D_MODEL)
    mod = ada_mine.reshape(DEPTH, batch, N_MOD, 1, D_MODEL)

    causal = jnp.tril(jnp.ones((CHUNK, CHUNK), F32))
    bd = jnp.kron(jnp.eye(N_HEADS, dtype=F32), jnp.full((HEAD_DIM, HEAD_DIM), 1.0 / HEAD_DIM, F32)).astype(BF16)
    row_vec = lambda a: a.reshape(1, -1)

    hosted_gathers = {
        (0, "ffn1"): [(0, "ffn1_w_down"), (0, "mix_w_in"), (0, "mix_w_out")],
        (0, "ffn_down1"): [(0, "ffn2_w_gu")],
        (0, "mix_in"): [(0, "ffn2_w_down")],
        (0, "ffn2"): [(1, "ffn1_w_gu"), (1, "ffn1_w_down"), (1, "mix_w_in"), (1, "mix_w_out")],
        (1, "ffn1"): [(1, "ffn2_w_gu"), (1, "ffn2_w_down")],
    }

    def hosting(l, site):
        keys = hosted_gathers.get((l, site))
        return gather_of(keys) if keys else (None, None)

    xs = x.reshape(tokens, D_MODEL)
    saved = []
    for l in range(DEPTH):
        sh1, sc1, g1, sh2, sc2, g2, sh3, sc3, g3 = [mod[l, :, k] for k in range(N_MOD)]
        mixer_consts = dict(
            wm=(sgu_w_s[l] * causal[None]).astype(BF16),
            bias_full=jnp.repeat(sgu_b[l].T, HEAD_DIM, axis=1),
            lng=row_vec(jnp.tile(sgu_ln_g[l], N_HEADS)), lnb=row_vec(jnp.tile(sgu_ln_b[l], N_HEADS)),
            convw=jnp.pad(convw_full[l], ((0, 5), (0, 0))), og=row_vec(out_norm_g[l]), bd=bd)
        x0 = xs
        plan = hosting(l, "ffn1")
        if l == 0:
            h1, a1, s1, w1, got = _normmod_matmul(x0, row_vec(norm_ffn1_g[l]), 1.0 + sc1, sh1, full_w[l, "ffn1_w_gu"], seq, "ffn_up", True, plan[1])
            landed(plan, got)
            plan = hosting(l, "ffn_down1")
            x1, f1, got = _matmul_residual(a1, full_w[l, "ffn1_w_down"], x0, g1, 0.5, seq, "ffn_down", plan[1])
        else:
            h1, a1, s1, w1, x1, f1, got = _ffn_forward(
                x0, row_vec(norm_ffn1_g[l]), 1.0 + sc1, sh1, full_w[l, "ffn1_w_gu"], full_w[l, "ffn1_w_down"], g1, 0.5, seq, "ffn_fwd",
                comm=plan[1])
        if got:
            landed(plan, got)
        plan = hosting(l, "mix_in")
        h2, proj, got = _normmod_matmul(x1, row_vec(norm_mix_g[l]), 1.0 + sc2, sh2, full_w[l, "mix_w_in"], seq, "mix_in", False, plan[1])
        if got:
            landed(plan, got)
        ymix = _mixer_forward(proj, seq=seq, name="mixer_forward", **mixer_consts)
        x2, o2, _ = _matmul_residual(ymix, full_w[l, "mix_w_out"], x1, g2, 1.0, seq, "mix_out")
        plan = hosting(l, "ffn2")
        if l + 1 < DEPTH:
            h3, a3, s3, w3, x3, f3, got = _ffn_forward(
                x2, row_vec(norm_ffn2_g[l]), 1.0 + sc3, sh3, full_w[l, "ffn2_w_gu"], full_w[l, "ffn2_w_down"], g3, 0.5, seq, "ffn_fwd",
                comm=plan[1])
        else:
            head = (loss_target.reshape(tokens, D_MODEL), row_vec(final_norm_g))
            h3, a3, s3, w3, x3, f3, d_final_g, loss_cols, got = _ffn_forward(
                x2, row_vec(norm_ffn2_g[l]), 1.0 + sc3, sh3, full_w[l, "ffn2_w_gu"], full_w[l, "ffn2_w_down"], g3, 0.5, seq, "ffn_fwd_loss",
                loss_head=head, comm=plan[1])
        if got:
            landed(plan, got)
        saved.append(dict(x0=x0, x1=x1, x2=x2, h1=h1, h2=h2, h3=h3, a1=a1, s1=s1, w1=w1, a3=a3, s3=s3, w3=w3, f1=f1, f3=f3, o2=o2, proj=proj,
                          ymix=ymix, mixer_consts=mixer_consts, sc=(1.0 + sc1, 1.0 + sc2, 1.0 + sc3), gates=(g1, g2, g3)))
        xs = x3

    dx = xs

    recv = {}
    small_grads = [None] * DEPTH
    d_mod = [None] * DEPTH

    mix_names = ("out_norm_g", "sgu_ln_g", "sgu_ln_b", "sgu_w_s", "sgu_b", "conv_w")
    late_names = ("norm_ffn1_g", "norm_mix_g", "norm_ffn2_g")

    def mix_parts(l):
        return [small_grads[l][nm] for nm in mix_names]

    def late_parts(l):
        return [small_grads[l][nm] for nm in late_names] + [d_mod[l]]

    pending = []

    def scatter_later(l, nm, grad):
        pending.append(((l, nm), _ScatterRows([grad])))

    def host():
        keys, parts = [k for k, _ in pending], [p for _, p in pending]
        pending.clear()
        return keys, (_Exchanges(parts) if parts else None)

    def hosted(keys, got):
        if got:
            recv.update(zip(keys, got))

    for l in reversed(range(DEPTH)):
        sv = saved[l]
        mc = sv["mixer_consts"]
        if l + 1 < DEPTH:
            pending.append((("late", l + 1), _GatherRows([_pack_small(late_parts(l + 1))])))
        keys, comm = host()
        df3, dg3, dgu3, dx2, dsh3, dsc3, dn3, got = _ffn_backward(
            dx, sv["gates"][2], sv["f3"], sv["s3"], sv["w3"], full_w[l, "ffn2_w_down"], full_w[l, "ffn2_w_gu"], sv["x2"],
            row_vec(norm_ffn2_g[l]), sv["sc"][2], 0.5, seq, "ffn_bwd", comm)
        hosted(keys, got)
        gw_down2, _ = _weight_grad(sv["a3"], df3, seq, "grad_w_down")
        scatter_later(l, "ffn2_w_down", gw_down2)
        keys, comm = host()
        gw_gu2, got = _weight_grad(dgu3, sv["h3"], seq, "grad_w_gu", comm)
        hosted(keys, got)
        scatter_later(l, "ffn2_w_gu", gw_gu2)
        do2, dg2, dymix, _ = _residual_backward(dx2, sv["gates"][1], sv["o2"], full_w[l, "mix_w_out"], 1.0, seq, "mix_out_bwd")
        gw_out, _ = _weight_grad(sv["ymix"], do2, seq, "grad_w_out")
        keys, comm = host()
        dproj, d_og, d_cw, d_lng, d_lnb, d_bias, d_wm, got = _mixer_backward(
            sv["proj"], dymix, causal=causal, seq=seq, name="mixer_backward", comm=comm, **mc)
        hosted(keys, got)
        small_grads[l] = dict(
            out_norm_g=d_og, sgu_ln_g=d_lng.reshape(N_HEADS, HEAD_DIM).sum(0), sgu_ln_b=d_lnb.reshape(N_HEADS, HEAD_DIM).sum(0),
            sgu_w_s=d_wm, sgu_b=d_bias.reshape(CHUNK, N_HEADS, HEAD_DIM).sum(-1).T, conv_w=d_cw[0:3])
        pending.append((("mix", l), _GatherRows([_pack_small(mix_parts(l))])))
        scatter_later(l, "mix_w_out", gw_out)
        keys, comm = host()
        dx1, dsh2, dsc2, dn2, got = _matmul_normmod_backward(dproj, full_w[l, "mix_w_in"], sv["x1"], dx2, row_vec(norm_mix_g[l]), sv["sc"][1], seq, "mix_in_bwd", comm)
        hosted(keys, got)
        gw_in, _ = _weight_grad(dproj, sv["h2"], seq, "grad_w_in")
        scatter_later(l, "mix_w_in", gw_in)
        keys, comm = host()
        if l > 0:
            df1, dg1, dgu1, dx0, dsh1, dsc1, dn1, got = _ffn_backward(
                dx1, sv["gates"][0], sv["f1"], sv["s1"], sv["w1"], full_w[l, "ffn1_w_down"], full_w[l, "ffn1_w_gu"], sv["x0"],
                row_vec(norm_ffn1_g[l]), sv["sc"][0], 0.5, seq, "ffn_bwd", comm)
        else:
            df1, dg1, dgu1, got = _residual_backward(dx1, sv["gates"][0], sv["f1"], full_w[l, "ffn1_w_down"], 0.5, seq, "ffn_down_bwd", sv["s1"], sv["w1"], comm)
        hosted(keys, got)
        gw_down1, _ = _weight_grad(sv["a1"], df1, seq, "grad_w_down")
        scatter_later(l, "ffn1_w_down", gw_down1)
        keys, comm = host()
        gw_gu1, got = _weight_grad(dgu1, sv["h1"], seq, "grad_w_gu", comm)
        hosted(keys, got)
        scatter_later(l, "ffn1_w_gu", gw_gu1)
        if l == 0:
            keys, comm = host()
            dx0, dsh1, dsc1, dn1, got = _matmul_normmod_backward(dgu1, full_w[l, "ffn1_w_gu"], sv["x0"], dx1, row_vec(norm_ffn1_g[l]), sv["sc"][0], seq, "ffn_up_bwd", comm)
            hosted(keys, got)
        dx = dx0
        small_grads[l].update(norm_ffn1_g=dn1, norm_mix_g=dn2, norm_ffn2_g=dn3)
        d_mod[l] = jnp.concatenate([dsh1, dsc1, dg1, dsh2, dsc2, dg2, dsh3, dsc3, dg3], axis=1)
    grad_x = dx.reshape(batch, seq, D_MODEL)

    grad_big, delta, new_m, new_v = {}, {}, {}, {}
    for nm in big:
        results = _adamw_rows([recv[l, nm] for l in range(DEPTH)], as_rows(nm, weights[nm]), as_rows(nm, mom1[nm]),
                              as_rows(nm, mom2[nm]), "adamw_" + nm)
        grad_big[nm], delta[nm], new_m[nm], new_v[nm] = [as_rows(nm, r) for r in results]

    last_parts = late_parts(0) + [d_final_g, loss_cols]
    last_shapes = [p.shape for p in last_parts]
    packed_all, packed_sum = _all_gather_small(_pack_small(last_parts), "reduce_small")
    late_sum = {0: _unpack_small(packed_sum, last_shapes)}
    d_mod_dev = {0: _unpack_small(packed_all, last_shapes, lead=(N_DEV,))[len(late_names)]}
    mix_sum = {}
    for l in range(DEPTH):
        gathered = recv["mix", l].reshape(N_DEV, -1, LANES)
        mix_sum[l] = _unpack_small(_sum_gathered(gathered, "sum_mix"), [p.shape for p in mix_parts(l)])
        if l > 0:
            shapes_l = [p.shape for p in late_parts(l)]
            gathered = recv["late", l].reshape(N_DEV, -1, LANES)
            late_sum[l] = _unpack_small(_sum_gathered(gathered, "sum_late"), shapes_l)
            d_mod_dev[l] = _unpack_small(gathered, shapes_l, lead=(N_DEV,))[len(late_names)]
    grad_small = {}
    for group, names in ((mix_sum, mix_names), (late_sum, late_names)):
        for k, nm in enumerate(names):
            grad_small[nm] = jnp.stack([group[l][k] for l in range(DEPTH)]).reshape(
                (DEPTH, 3, D_A) if nm == "conv_w" else weights[nm].shape)
    grad_small["conv_w"] = lax.dynamic_slice_in_dim(grad_small["conv_w"], me * conv_w.shape[2], conv_w.shape[2], axis=2)
    grad_small["final_norm_g"] = late_sum[0][len(late_names) + 1].reshape(final_norm_g.shape)
    loss = jnp.sum(late_sum[0][len(late_names) + 2])
    d_ada_all = jnp.stack([d_mod_dev[l] for l in range(DEPTH)]).reshape(DEPTH, N_DEV * batch, N_MOD * D_MODEL)
    d_ada_cols = lax.dynamic_slice_in_dim(d_ada_all, me * ada_cols, ada_cols, axis=2)
    g_ada_w, g_ada_b = _ada_backward(c_all, d_ada_cols, d_ada_all)

    grads = dict(grad_big)
    grads.update(grad_small)
    grads["ada_w"] = g_ada_w
    grads["ada_b"] = g_ada_b.reshape(ada_b.shape)

    names = ("ada_w", "ada_b", "norm_ffn1_g", "ffn1_w_gu", "ffn1_w_down", "norm_mix_g", "mix_w_in", "sgu_ln_g",
             "sgu_ln_b", "sgu_w_s", "sgu_b", "conv_w", "out_norm_g", "mix_w_out", "norm_ffn2_g", "ffn2_w_gu",
             "ffn2_w_down", "final_norm_g")
    delta["ada_w"], new_m["ada_w"], new_v["ada_w"] = _adamw_nd(ada_w, grads["ada_w"], m_ada_w, v_ada_w, "adamw_ada_w")
    rest = [nm for nm in names if nm not in big and nm != "ada_w"]
    pick = lambda src: [src[nm] for nm in rest]
    for nm, d_k, m_k, v_k in zip(rest, *_adamw_many(pick(weights), pick(grads), pick(mom1), pick(mom2), "adamw_small")):
        delta[nm], new_m[nm], new_v[nm] = d_k, m_k, v_k

    return (loss, grad_x, *[grads[nm] for nm in names], *[delta[nm] for nm in names],
            *[new_m[nm] for nm in names], *[new_v[nm] for nm in names])
```

```python
import math

import jax
import jax.numpy as jnp
from jax import lax
from jax.experimental import pallas as pl
from jax.experimental.pallas import tpu as pltpu

F32 = jnp.float32
BF16 = jnp.bfloat16

D_MODEL = 1024
D_FF = 2816
D_A = 512
D_PROJ = 2560
N_HEADS = 8
HEAD_DIM = 64
CHUNK = 128
N_MOD = 9
DEPTH = 2
EPS = 1e-6
N_DEV = 8
LANES = 128
MXU_N = 256
HALO = 16
VMEM_LIMIT = 56 * 1024 * 1024
FORWARD_STEPS = 4

ADAM_LR = 0.001
ADAM_B1 = 0.9
ADAM_B2 = 0.999
ADAM_EPS = 1e-08
ADAM_WD = 0.01
ADAM_STEP = 10

MESH = pl.DeviceIdType.MESH


def _dot(a, b):
    return jnp.dot(a, b, preferred_element_type=F32)


def _dot_nt(a, b):
    return lax.dot_general(a, b, (((1,), (1,)), ((), ())), preferred_element_type=F32)


def _dot_tn(a, b):
    return lax.dot_general(a, b, (((0,), (0,)), ((), ())), preferred_element_type=F32)


def _sigmoid(x):
    return 0.5 * jnp.tanh(0.5 * x) + 0.5


def _gelu(x):
    return 0.5 * x * (1.0 + lax.erf(x * (1.0 / math.sqrt(2.0))))


def _gelu_grad(x):
    cdf = 0.5 * (1.0 + lax.erf(x * (1.0 / math.sqrt(2.0))))
    return cdf + x * jnp.exp(-0.5 * x * x) * (1.0 / math.sqrt(2.0 * math.pi))


def _params(n_axes=1, parallel=False):
    sem = ("parallel" if parallel else "arbitrary",) * n_axes
    return pltpu.CompilerParams(dimension_semantics=sem, vmem_limit_bytes=VMEM_LIMIT)


def _resident(shape):
    nd = len(shape)
    return pl.BlockSpec(shape, lambda *_: (0,) * nd, pipeline_mode=pl.Buffered(1))


def _tile_rows(seq):
    return min(512, seq)


def _my_position():
    x, y, c = lax.axis_index("x"), lax.axis_index("y"), lax.axis_index("c")
    return x, y, c, 4 * x + 2 * y + c


def _peer(x, y, c, p):
    return (x ^ ((p >> 2) & 1), y ^ ((p >> 1) & 1), c ^ (p & 1))


class _GatherRows:
    def __init__(self, shards):
        self.operands = list(shards)
        n = len(shards)
        self.out_shape = [jax.ShapeDtypeStruct((N_DEV * s.shape[0], s.shape[1]), s.dtype) for s in shards]
        self.scratch = [pltpu.SemaphoreType.DMA((n, N_DEV - 1)), pltpu.SemaphoreType.DMA((n, N_DEV - 1)),
                        pltpu.SemaphoreType.DMA((n,))]

    def _plan(self, src, dst, send, recv, loc):
        x, y, c, _ = _my_position()
        me, sib = (x, y, c), (x, y, 1 - c)
        chips = [(1 - x, y), (x, 1 - y), (1 - x, 1 - y)]
        plans = []
        for k, shard in enumerate(self.operands):
            rows = shard.shape[0]

            def blk(pos, k=k, rows=rows):
                return dst[k].at[pl.ds((4 * pos[0] + 2 * pos[1] + pos[2]) * rows, rows), :]

            def rc(s, block, to, source=None, k=k, blk=blk):
                return pltpu.make_async_remote_copy(
                    src_ref=blk(block) if source is None else source, dst_ref=blk(block),
                    send_sem=send.at[k, s], recv_sem=recv.at[k, s], device_id=to, device_id_type=MESH)

            plans.append(dict(
                local=pltpu.make_async_copy(src[k], blk(me), loc.at[k]),
                first=[rc(0, me, sib, src[k])] + [rc(1 + j, me, (*chip, c), src[k]) for j, chip in enumerate(chips)],
                landed=[rc(1 + j, (*chip, c), me) for j, chip in enumerate(chips)],
                passed=[rc(4 + j, (*chip, c), sib) for j, chip in enumerate(chips)],
                from_sib=[rc(0, sib, me)] + [rc(4 + j, (*chip, 1 - c), me) for j, chip in enumerate(chips)]))
        return plans

    def start(self, src, dst, send, recv, loc):
        for plan in self._plan(src, dst, send, recv, loc):
            plan["local"].start()
            for cp in plan["first"]:
                cp.start()

    def forward(self, src, dst, send, recv, loc):
        for plan in self._plan(src, dst, send, recv, loc):
            for landed, passed in zip(plan["landed"], plan["passed"]):
                landed.wait_recv()
                passed.start()

    def finish(self, src, dst, send, recv, loc):
        for plan in self._plan(src, dst, send, recv, loc):
            for cp in plan["from_sib"]:
                cp.wait_recv()
            for cp in plan["first"] + plan["passed"]:
                cp.wait_send()
            plan["local"].wait()


class _ScatterRows:
    def __init__(self, grads):
        self.operands = list(grads)
        n = len(grads)
        self.out_shape = [jax.ShapeDtypeStruct((N_DEV, g.shape[0] // N_DEV, g.shape[1]), g.dtype) for g in grads]
        self.scratch = [pltpu.SemaphoreType.DMA((n, N_DEV - 1)), pltpu.SemaphoreType.DMA((n, N_DEV - 1)),
                        pltpu.SemaphoreType.DMA((n,))]

    def _plan(self, src, dst, send, recv, loc):
        x, y, c, me = _my_position()
        copies = []
        for k, grad in enumerate(self.operands):
            rows = grad.shape[0] // N_DEV
            copies.append(pltpu.make_async_copy(src[k].at[pl.ds(me * rows, rows), :], dst[k].at[me], loc.at[k]))
            for p in range(1, N_DEV):
                px, py, pc = _peer(x, y, c, p)
                copies.append(pltpu.make_async_remote_copy(
                    src_ref=src[k].at[pl.ds((4 * px + 2 * py + pc) * rows, rows), :], dst_ref=dst[k].at[me],
                    send_sem=send.at[k, p - 1], recv_sem=recv.at[k, p - 1], device_id=(px, py, pc), device_id_type=MESH))
        return copies

    def start(self, src, dst, send, recv, loc):
        for cp in self._plan(src, dst, send, recv, loc):
            cp.start()

    def forward(self, src, dst, send, recv, loc):
        pass

    def finish(self, src, dst, send, recv, loc):
        for cp in self._plan(src, dst, send, recv, loc):
            cp.wait()


class _Exchanges:
    def __init__(self, parts):
        self.parts = list(parts)
        self.operands = [op for part in self.parts for op in part.operands]
        self.out_shape = [shp for part in self.parts for shp in part.out_shape]
        self.scratch = [scr for part in self.parts for scr in part.scratch]

    def _each(self, src, dst, sems):
        at, sem_at = 0, 0
        for part in self.parts:
            n, n_sem = len(part.operands), len(part.scratch)
            yield part, src[at:at + n], dst[at:at + n], sems[sem_at:sem_at + n_sem]
            at, sem_at = at + n, sem_at + n_sem

    def start(self, src, dst, *sems):
        for part, part_src, part_dst, part_sems in self._each(src, dst, sems):
            part.start(part_src, part_dst, *part_sems)

    def forward(self, src, dst, *sems):
        for part, part_src, part_dst, part_sems in self._each(src, dst, sems):
            part.forward(part_src, part_dst, *part_sems)

    def finish(self, src, dst, *sems):
        for part, part_src, part_dst, part_sems in self._each(src, dst, sems):
            part.finish(part_src, part_dst, *part_sems)


_ANY = pl.BlockSpec(memory_space=pl.ANY)


def _call(body, *, name, grid, in_specs, out_specs, out_shape, operands, scratch_shapes=(), parallel=False, comm=None):
    n_axes = len(grid)
    if comm is None:
        outs = pl.pallas_call(body, name=name, grid=grid, out_shape=list(out_shape), in_specs=list(in_specs),
                              out_specs=list(out_specs), scratch_shapes=list(scratch_shapes),
                              compiler_params=_params(n_axes, parallel))(*operands)
        return list(outs), None
    n_in, n_out, n_scr, n_c = len(in_specs), len(out_specs), len(scratch_shapes), len(comm.operands)
    total = math.prod(grid)

    def hosted(*refs):
        ins, c_src = refs[:n_in], refs[n_in:n_in + n_c]
        outs, c_dst = refs[n_in + n_c:n_in + n_c + n_out], refs[n_in + n_c + n_out:n_in + 2 * n_c + n_out]
        scr, sems = refs[n_in + 2 * n_c + n_out:n_in + 2 * n_c + n_out + n_scr], refs[n_in + 2 * n_c + n_out + n_scr:]
        step = pl.program_id(0)
        for axis in range(1, n_axes):
            step = step * grid[axis] + pl.program_id(axis)

        @pl.when(step == 0)
        def _():
            comm.start(c_src, c_dst, *sems)

        @pl.when(step == max(total - FORWARD_STEPS, 0))
        def _():
            comm.forward(c_src, c_dst, *sems)

        body(*ins, *outs, *scr)

        @pl.when(step == total - 1)
        def _():
            comm.finish(c_src, c_dst, *sems)

    res = pl.pallas_call(hosted, name=name, grid=grid, out_shape=list(out_shape) + comm.out_shape,
                         in_specs=list(in_specs) + [_ANY] * n_c, out_specs=list(out_specs) + [_ANY] * n_c,
                         scratch_shapes=list(scratch_shapes) + comm.scratch,
                         compiler_params=_params(n_axes, False))(*operands, *comm.operands)
    return list(res[:n_out]), list(res[n_out:])


def _all_gather_small(v, name):
    rows = v.shape[0]

    def body(v_ref, all_ref, sum_ref, send_sems, recv_sems):
        x, y, c, me = _my_position()
        all_ref[me] = v_ref[...]
        copies = []
        for p in range(1, N_DEV):
            cp = pltpu.make_async_remote_copy(
                src_ref=v_ref, dst_ref=all_ref.at[me], send_sem=send_sems.at[p - 1], recv_sem=recv_sems.at[p - 1],
                device_id=_peer(x, y, c, p), device_id_type=MESH)
            cp.start()
            copies.append(cp)
        for cp in copies:
            cp.wait()
        acc = all_ref[0]
        for d in range(1, N_DEV):
            acc = acc + all_ref[d]
        sum_ref[...] = acc

    return pl.pallas_call(
        body, name=name,
        out_shape=[jax.ShapeDtypeStruct((N_DEV, rows, LANES), F32), jax.ShapeDtypeStruct((rows, LANES), F32)],
        in_specs=[pl.BlockSpec(memory_space=pltpu.VMEM)],
        out_specs=[pl.BlockSpec(memory_space=pltpu.VMEM)] * 2,
        scratch_shapes=[pltpu.SemaphoreType.DMA((N_DEV - 1,)), pltpu.SemaphoreType.DMA((N_DEV - 1,))],
        compiler_params=pltpu.CompilerParams(vmem_limit_bytes=VMEM_LIMIT),
    )(v)


def _sum_gathered(gathered, name):
    rows = gathered.shape[1]

    def body(g_ref, o_ref):
        acc = g_ref[0]
        for d in range(1, N_DEV):
            acc = acc + g_ref[d]
        o_ref[...] = acc

    return pl.pallas_call(
        body, name=name, out_shape=jax.ShapeDtypeStruct((rows, LANES), F32),
        in_specs=[pl.BlockSpec(memory_space=pltpu.VMEM)], out_specs=pl.BlockSpec(memory_space=pltpu.VMEM),
        compiler_params=pltpu.CompilerParams(vmem_limit_bytes=VMEM_LIMIT),
    )(gathered)


def _pack_small(parts):
    flat = jnp.concatenate([p.reshape(-1).astype(F32) for p in parts])
    total = flat.shape[0]
    padded = -(-total // (8 * LANES)) * (8 * LANES)
    flat = jnp.pad(flat, (0, padded - total))
    return flat.reshape(padded // LANES, LANES)


def _unpack_small(packed, shapes, lead=()):
    flat = packed.reshape(lead + (-1,))
    out, off = [], 0
    for shp in shapes:
        size = math.prod(shp)
        out.append(flat[..., off:off + size].reshape(lead + tuple(shp)))
        off += size
    return out


def _prologue(c_rows, convw_rows, ada_w, ada_b_cols, gather):
    depth, _, cols = ada_w.shape
    n_c = len(gather.operands)
    sub = 8

    def body(c_ref, cw_ref, w_ref, b_ref, *rest):
        g_src, (c_all_ref, cw_all_ref, ada_ref), g_dst = rest[:n_c], rest[n_c:n_c + 3], rest[n_c + 3:2 * n_c + 3]
        ada_local, send_sems, recv_sems = rest[2 * n_c + 3:2 * n_c + 6]
        g_sems = rest[2 * n_c + 6:]
        x, y, c, me = _my_position()
        gather.start(g_src, g_dst, *g_sems)

        def to_all(k, src_ref, dst_ref):
            copies = []
            for p in range(1, N_DEV):
                copies.append(pltpu.make_async_remote_copy(
                    src_ref=src_ref, dst_ref=dst_ref.at[me], send_sem=send_sems.at[k, p - 1], recv_sem=recv_sems.at[k, p - 1],
                    device_id=_peer(x, y, c, p), device_id_type=MESH))
            return copies

        first = to_all(0, c_ref, c_all_ref) + to_all(1, cw_ref, cw_all_ref)
        c_all_ref[me] = c_ref[...]
        cw_all_ref[me] = cw_ref[...]
        for cp in first:
            cp.start()
        for cp in first:
            cp.wait()
        cv = c_all_ref[...].reshape(N_DEV * sub, D_MODEL)
        act = (cv * _sigmoid(cv)).astype(BF16)
        for l in range(depth):
            ada_local[l] = _dot(act, w_ref[l].astype(BF16)) + b_ref[l]
        ada_ref[me] = ada_local[:, pl.ds(pl.multiple_of(me * sub, sub), sub), :]
        rows_out = []
        for p in range(1, N_DEV):
            px, py, pc = _peer(x, y, c, p)
            rows = pl.ds(pl.multiple_of((4 * px + 2 * py + pc) * sub, sub), sub)
            rows_out.append(pltpu.make_async_remote_copy(
                src_ref=ada_local.at[:, rows, :], dst_ref=ada_ref.at[me], send_sem=send_sems.at[2, p - 1],
                recv_sem=recv_sems.at[2, p - 1], device_id=(px, py, pc), device_id_type=MESH))
        for cp in rows_out:
            cp.start()
        for cp in rows_out:
            cp.wait()
        gather.forward(g_src, g_dst, *g_sems)
        gather.finish(g_src, g_dst, *g_sems)

    vmem = pl.BlockSpec(memory_space=pltpu.VMEM)
    outs = pl.pallas_call(
        body, name="prologue",
        out_shape=[jax.ShapeDtypeStruct((N_DEV, sub, D_MODEL), F32), jax.ShapeDtypeStruct((N_DEV, sub, LANES), F32),
                   jax.ShapeDtypeStruct((N_DEV, depth, sub, cols), F32)] + gather.out_shape,
        in_specs=[vmem] * 4 + [_ANY] * n_c, out_specs=[vmem] * 3 + [_ANY] * n_c,
        scratch_shapes=[pltpu.VMEM((depth, N_DEV * sub, cols), F32), pltpu.SemaphoreType.DMA((3, N_DEV - 1)),
                        pltpu.SemaphoreType.DMA((3, N_DEV - 1))] + gather.scratch,
        compiler_params=pltpu.CompilerParams(vmem_limit_bytes=VMEM_LIMIT),
    )(c_rows, convw_rows, ada_w, ada_b_cols, *gather.operands)
    return outs[0], outs[1], outs[2], list(outs[3:])


def _ada_backward(c_all, d_ada_cols, d_ada_all):
    nb = c_all.shape[0]
    cols = d_ada_cols.shape[2]
    full = d_ada_all.shape[2]

    def body(c_ref, dc_ref, da_ref, gw_ref, gb_ref):
        cv = c_ref[...]
        act = (cv * _sigmoid(cv)).astype(BF16)
        gw_ref[0] = _dot_tn(act, dc_ref[0].astype(BF16))
        gb_ref[0] = jnp.sum(da_ref[0], axis=0, keepdims=True)

    return pl.pallas_call(
        body, name="ada_backward", grid=(DEPTH,),
        out_shape=[jax.ShapeDtypeStruct((DEPTH, D_MODEL, cols), F32), jax.ShapeDtypeStruct((DEPTH, 1, full), F32)],
        in_specs=[pl.BlockSpec((nb, D_MODEL), lambda l: (0, 0)),
                  pl.BlockSpec((1, nb, cols), lambda l: (l, 0, 0)),
                  pl.BlockSpec((1, nb, full), lambda l: (l, 0, 0))],
        out_specs=[pl.BlockSpec((1, D_MODEL, cols), lambda l: (l, 0, 0)),
                   pl.BlockSpec((1, 1, full), lambda l: (l, 0, 0))],
        compiler_params=_params(),
    )(c_all, d_ada_cols, d_ada_all)


def _rms(xv):
    return lax.rsqrt(jnp.mean(xv * xv, axis=-1, keepdims=True) + EPS)


def _normmod_matmul(x, gnorm, scale1p, shift, w_t, seq, name, swiglu=False, comm=None):
    tokens, n_out = x.shape[0], w_t.shape[0]
    tm = _tile_rows(seq)
    per_seq = seq // tm
    width = n_out // 2 if swiglu else n_out
    n_chunks = width // MXU_N

    def body(x_ref, g_ref, sc_ref, sh_ref, w_ref, h_ref, *o_refs):
        xv = x_ref[...]
        h = (xv * _rms(xv) * g_ref[...]) * sc_ref[0] + sh_ref[0]
        h_ref[...] = h.astype(BF16)
        for ck in range(n_chunks):
            cs = slice(ck * MXU_N, (ck + 1) * MXU_N)
            if swiglu:
                act_ref, silu_ref, dact_ref = o_refs
                g = _dot_nt(h_ref[...], w_ref[cs, :])
                u = _dot_nt(h_ref[...], w_ref[width + ck * MXU_N:width + (ck + 1) * MXU_N, :])
                sig = _sigmoid(g)
                silu = g * sig
                act_ref[:, cs] = (silu * u).astype(BF16)
                silu_ref[:, cs] = silu.astype(BF16)
                dact_ref[:, cs] = (u * (sig + silu * (1.0 - sig))).astype(BF16)
            else:
                o_refs[0][:, cs] = _dot_nt(h_ref[...], w_ref[cs, :]).astype(BF16)

    n_res = 3 if swiglu else 1
    per_batch = pl.BlockSpec((1, 1, D_MODEL), lambda i: (i // per_seq, 0, 0))
    outs, got = _call(
        body, name=name, grid=(tokens // tm,),
        out_shape=[jax.ShapeDtypeStruct((tokens, D_MODEL), BF16)] + [jax.ShapeDtypeStruct((tokens, width), BF16)] * n_res,
        in_specs=[pl.BlockSpec((tm, D_MODEL), lambda i: (i, 0)), _resident((1, D_MODEL)), per_batch, per_batch,
                  _resident(w_t.shape)],
        out_specs=[pl.BlockSpec((tm, D_MODEL), lambda i: (i, 0))] + [pl.BlockSpec((tm, width), lambda i: (i, 0))] * n_res,
        operands=(x, gnorm, scale1p, shift, w_t), parallel=True, comm=comm)
    return (*outs, got)


def _matmul_residual(src, w, x, gate, scale, seq, name, comm=None):
    tokens, k_dim = x.shape[0], w.shape[0]
    tm = _tile_rows(seq)
    per_seq = seq // tm

    def body(s_ref, w_ref, x_ref, gate_ref, xo_ref, f_ref):
        f = _dot(s_ref[...], w_ref[...])
        f_ref[...] = f.astype(BF16)
        xo_ref[...] = x_ref[...] + (scale * gate_ref[0]) * f

    (x_out, f), got = _call(
        body, name=name, grid=(tokens // tm,),
        out_shape=[jax.ShapeDtypeStruct((tokens, D_MODEL), F32), jax.ShapeDtypeStruct((tokens, D_MODEL), BF16)],
        in_specs=[pl.BlockSpec((tm, k_dim), lambda i: (i, 0)), _resident(w.shape),
                  pl.BlockSpec((tm, D_MODEL), lambda i: (i, 0)),
                  pl.BlockSpec((1, 1, D_MODEL), lambda i: (i // per_seq, 0, 0))],
        out_specs=[pl.BlockSpec((tm, D_MODEL), lambda i: (i, 0))] * 2,
        operands=(src, w, x, gate), parallel=True, comm=comm)
    return x_out, f, got


def _loss_tile(xv, target, gn):
    r = _rms(xv)
    xn = xv * r
    err = xn * gn - target
    loss = (0.5 / D_MODEL) * jnp.sum(err * err, axis=0, keepdims=True)
    dyv = err * (1.0 / D_MODEL)
    dg = jnp.sum(dyv * xn, axis=0, keepdims=True)
    dxn = dyv * gn
    dx = r * (dxn - xn * jnp.mean(dxn * xn, axis=-1, keepdims=True))
    return loss, dx, dg


def _ffn_forward(x, gnorm, scale1p, shift, w_gu_t, w_down, gate, scale, seq, name, loss_head=None, comm=None):
    tokens, width = x.shape[0], w_down.shape[0]
    tm = _tile_rows(seq)
    per_seq = seq // tm
    n_chunks = width // MXU_N

    def body(x_ref, g_ref, sc_ref, sh_ref, wgu_ref, wd_ref, gate_ref, *rest):
        if loss_head:
            t_ref, gf_ref, h_ref, act_ref, silu_ref, dact_ref, xo_ref, f_ref, dgf_ref, loss_ref = rest
        else:
            h_ref, act_ref, silu_ref, dact_ref, xo_ref, f_ref = rest
        xv = x_ref[...]
        h = (xv * _rms(xv) * g_ref[...]) * sc_ref[0] + sh_ref[0]
        h_ref[...] = h.astype(BF16)
        for ck in range(n_chunks):
            cs = slice(ck * MXU_N, (ck + 1) * MXU_N)
            g = _dot_nt(h_ref[...], wgu_ref[cs, :])
            u = _dot_nt(h_ref[...], wgu_ref[width + ck * MXU_N:width + (ck + 1) * MXU_N, :])
            sig = _sigmoid(g)
            silu = g * sig
            act_ref[:, cs] = (silu * u).astype(BF16)
            silu_ref[:, cs] = silu.astype(BF16)
            dact_ref[:, cs] = (u * (sig + silu * (1.0 - sig))).astype(BF16)
        f = _dot(act_ref[...], wd_ref[...])
        f_ref[...] = f.astype(BF16)
        x_out = xv + (scale * gate_ref[0]) * f
        if loss_head:
            i = pl.program_id(0)
            loss, dx, dg = _loss_tile(x_out, t_ref[...], gf_ref[...])
            xo_ref[...] = dx

            @pl.when(i == 0)
            def _():
                dgf_ref[...] = dg
                loss_ref[...] = loss

            @pl.when(i != 0)
            def _():
                dgf_ref[...] = dgf_ref[...] + dg
                loss_ref[...] = loss_ref[...] + loss
        else:
            xo_ref[...] = x_out

    row = lambda i: (i, 0)
    per_batch = pl.BlockSpec((1, 1, D_MODEL), lambda i: (i // per_seq, 0, 0))
    tile = lambda cols: pl.BlockSpec((tm, cols), row)
    wide = jax.ShapeDtypeStruct((tokens, width), BF16)
    fixed = pl.BlockSpec((1, D_MODEL), lambda i: (0, 0))
    vec = jax.ShapeDtypeStruct((1, D_MODEL), F32)
    outs, got = _call(
        body, name=name, grid=(tokens // tm,),
        out_shape=[jax.ShapeDtypeStruct((tokens, D_MODEL), BF16), wide, wide, wide,
                   jax.ShapeDtypeStruct((tokens, D_MODEL), F32), jax.ShapeDtypeStruct((tokens, D_MODEL), BF16)]
        + ([vec, vec] if loss_head else []),
        in_specs=[tile(D_MODEL), _resident((1, D_MODEL)), per_batch, per_batch, _resident(w_gu_t.shape),
                  _resident(w_down.shape), per_batch] + ([tile(D_MODEL), _resident((1, D_MODEL))] if loss_head else []),
        out_specs=[tile(D_MODEL), tile(width), tile(width), tile(width), tile(D_MODEL), tile(D_MODEL)]
        + ([fixed, fixed] if loss_head else []),
        operands=(x, gnorm, scale1p, shift, w_gu_t, w_down, gate) + (tuple(loss_head) if loss_head else ()),
        parallel=not loss_head, comm=comm)
    return (*outs, got)


def _residual_backward(dy, gate, f, w, scale, silu, dact, seq, name, comm=None):
    tokens, k_dim = dy.shape[0], w.shape[0]
    batch = tokens // seq
    tm = _tile_rows(seq)
    per_seq = seq // tm
    n_chunks = k_dim // MXU_N

    def body(dy_ref, gate_ref, f_ref, silu_ref, dact_ref, w_ref, df_ref, dgate_ref, dgu_ref):
        i = pl.program_id(0)
        dy_v = dy_ref[...]
        df_ref[...] = ((scale * gate_ref[0]) * dy_v).astype(BF16)
        part = scale * jnp.sum(dy_v * f_ref[...].astype(F32), axis=0, keepdims=True)

        @pl.when(i % per_seq == 0)
        def _():
            dgate_ref[0] = part

        @pl.when(i % per_seq != 0)
        def _():
            dgate_ref[0] = dgate_ref[0] + part

        for ck in range(n_chunks):
            cs = slice(ck * MXU_N, (ck + 1) * MXU_N)
            cu = slice(k_dim + ck * MXU_N, k_dim + (ck + 1) * MXU_N)
            da = _dot_nt(df_ref[...], w_ref[cs, :])
            dgu_ref[:, cs] = (da * dact_ref[:, cs].astype(F32)).astype(BF16)
            dgu_ref[:, cu] = (da * silu_ref[:, cs].astype(F32)).astype(BF16)

    row = lambda i: (i, 0)
    per_batch = pl.BlockSpec((1, 1, D_MODEL), lambda i: (i // per_seq, 0, 0))
    tile = lambda cols: pl.BlockSpec((tm, cols), row)
    outs, got = _call(
        body, name=name, grid=(tokens // tm,),
        out_shape=[jax.ShapeDtypeStruct((tokens, D_MODEL), BF16), jax.ShapeDtypeStruct((batch, 1, D_MODEL), F32),
                   jax.ShapeDtypeStruct((tokens, 2 * k_dim), BF16)],
        in_specs=[tile(D_MODEL), per_batch, tile(D_MODEL), tile(k_dim), tile(k_dim), _resident(w.shape)],
        out_specs=[tile(D_MODEL), per_batch, tile(2 * k_dim)],
        operands=(dy, gate, f, silu, dact, w), comm=comm)
    return (*outs, got)


def _matmul_normmod_backward(dsrc, w_t, x, dy, gnorm, scale1p, seq, name, comm=None):
    tokens, k_dim = dsrc.shape
    batch = tokens // seq
    tm = _tile_rows(seq)
    per_seq = seq // tm

    def body(ds_ref, w_ref, x_ref, dy_ref, g_ref, sc_ref, dx_ref, dsh_ref, dsc_ref, dg_ref):
        i = pl.program_id(0)
        dh = _dot(ds_ref[...], w_ref[...])
        xv = x_ref[...]
        r = _rms(xv)
        xn = xv * r
        gn = g_ref[...]
        dsh = jnp.sum(dh, axis=0, keepdims=True)
        dsc = jnp.sum(dh * (xn * gn), axis=0, keepdims=True)
        dhn = dh * sc_ref[0]
        dg = jnp.sum(dhn * xn, axis=0, keepdims=True)
        dxn = dhn * gn
        dx_ref[...] = dy_ref[...] + r * (dxn - xn * jnp.mean(dxn * xn, axis=-1, keepdims=True))

        @pl.when(i % per_seq == 0)
        def _():
            dsh_ref[0] = dsh
            dsc_ref[0] = dsc

        @pl.when(i % per_seq != 0)
        def _():
            dsh_ref[0] = dsh_ref[0] + dsh
            dsc_ref[0] = dsc_ref[0] + dsc

        @pl.when(i == 0)
        def _():
            dg_ref[...] = dg

        @pl.when(i != 0)
        def _():
            dg_ref[...] = dg_ref[...] + dg

    row = lambda i: (i, 0)
    per_batch = pl.BlockSpec((1, 1, D_MODEL), lambda i: (i // per_seq, 0, 0))
    outs, got = _call(
        body, name=name, grid=(tokens // tm,),
        out_shape=[jax.ShapeDtypeStruct((tokens, D_MODEL), F32), jax.ShapeDtypeStruct((batch, 1, D_MODEL), F32),
                   jax.ShapeDtypeStruct((batch, 1, D_MODEL), F32), jax.ShapeDtypeStruct((1, D_MODEL), F32)],
        in_specs=[pl.BlockSpec((tm, k_dim), row), _resident(w_t.shape), pl.BlockSpec((tm, D_MODEL), row),
                  pl.BlockSpec((tm, D_MODEL), row), _resident((1, D_MODEL)), per_batch],
        out_specs=[pl.BlockSpec((tm, D_MODEL), row), per_batch, per_batch, pl.BlockSpec((1, D_MODEL), lambda i: (0, 0))],
        operands=(dsrc, w_t, x, dy, gnorm, scale1p), comm=comm)
    return (*outs, got)


def _ffn_backward(dy, gate, f, silu, dact, w_down, w_gu_t, x, gnorm, scale1p, scale, seq, name, comm=None):
    tokens, k_dim = dy.shape[0], w_down.shape[0]
    batch = tokens // seq
    tm = min(256, seq)
    per_seq = seq // tm
    n_chunks = k_dim // MXU_N

    def body(dy_ref, gate_ref, f_ref, silu_ref, dact_ref, wd_ref, wgu_ref, x_ref, g_ref, sc_ref,
             df_ref, dgate_ref, dgu_ref, dx_ref, dsh_ref, dsc_ref, dg_ref):
        i = pl.program_id(0)
        dy_v = dy_ref[...]
        df_ref[...] = ((scale * gate_ref[0]) * dy_v).astype(BF16)
        dgate = scale * jnp.sum(dy_v * f_ref[...].astype(F32), axis=0, keepdims=True)
        for ck in range(n_chunks):
            cs = slice(ck * MXU_N, (ck + 1) * MXU_N)
            cu = slice(k_dim + ck * MXU_N, k_dim + (ck + 1) * MXU_N)
            da = _dot_nt(df_ref[...], wd_ref[cs, :])
            dgu_ref[:, cs] = (da * dact_ref[:, cs].astype(F32)).astype(BF16)
            dgu_ref[:, cu] = (da * silu_ref[:, cs].astype(F32)).astype(BF16)
        dh = _dot(dgu_ref[...], wgu_ref[...])
        xv = x_ref[...]
        r = _rms(xv)
        xn = xv * r
        gn = g_ref[...]
        dsh = jnp.sum(dh, axis=0, keepdims=True)
        dsc = jnp.sum(dh * (xn * gn), axis=0, keepdims=True)
        dhn = dh * sc_ref[0]
        dg = jnp.sum(dhn * xn, axis=0, keepdims=True)
        dxn = dhn * gn
        dx_ref[...] = dy_v + r * (dxn - xn * jnp.mean(dxn * xn, axis=-1, keepdims=True))

        @pl.when(i % per_seq == 0)
        def _():
            dgate_ref[0] = dgate
            dsh_ref[0] = dsh
            dsc_ref[0] = dsc

        @pl.when(i % per_seq != 0)
        def _():
            dgate_ref[0] = dgate_ref[0] + dgate
            dsh_ref[0] = dsh_ref[0] + dsh
            dsc_ref[0] = dsc_ref[0] + dsc

        @pl.when(i == 0)
        def _():
            dg_ref[...] = dg

        @pl.when(i != 0)
        def _():
            dg_ref[...] = dg_ref[...] + dg

    row = lambda i: (i, 0)
    per_batch = pl.BlockSpec((1, 1, D_MODEL), lambda i: (i // per_seq, 0, 0))
    tile = lambda width: pl.BlockSpec((tm, width), row)
    vec = jax.ShapeDtypeStruct((batch, 1, D_MODEL), F32)
    outs, got = _call(
        body, name=name, grid=(tokens // tm,),
        out_shape=[jax.ShapeDtypeStruct((tokens, D_MODEL), BF16), vec, jax.ShapeDtypeStruct((tokens, 2 * k_dim), BF16),
                   jax.ShapeDtypeStruct((tokens, D_MODEL), F32), vec, vec, jax.ShapeDtypeStruct((1, D_MODEL), F32)],
        in_specs=[tile(D_MODEL), per_batch, tile(D_MODEL), tile(k_dim), tile(k_dim), _resident(w_down.shape),
                  _resident(w_gu_t.shape), tile(D_MODEL), _resident((1, D_MODEL)), per_batch],
        out_specs=[tile(D_MODEL), per_batch, tile(2 * k_dim), tile(D_MODEL), per_batch, per_batch,
                   pl.BlockSpec((1, D_MODEL), lambda i: (0, 0))],
        operands=(dy, gate, f, silu, dact, w_down, w_gu_t, x, gnorm, scale1p), comm=comm)
    return (*outs, got)


def _weight_grad(a, b, seq, name, comm=None):
    tokens, n_out = a.shape
    tn = MXU_N

    def body(a_ref, b_ref, o_ref):
        o_ref[...] = _dot_tn(a_ref[...], b_ref[...]).astype(BF16)

    (out,), got = _call(
        body, name=name, grid=(n_out // tn,),
        out_shape=[jax.ShapeDtypeStruct((n_out, D_MODEL), BF16)],
        in_specs=[pl.BlockSpec((tokens, tn), lambda j: (0, j)), _resident((tokens, D_MODEL))],
        out_specs=[pl.BlockSpec((tn, D_MODEL), lambda j: (j, 0))],
        operands=(a, b), parallel=True, comm=comm)
    return out, got


def _group_mean(v, bd):
    hi = v.astype(BF16)
    lo = (v - hi.astype(F32)).astype(BF16)
    return _dot(hi, bd) + _dot(lo, bd)


def _sgu_forward(pm_ref, wm_ref, bias_ref, lng_ref, lnb_ref, bd_ref, mixed_scr, n_sub):
    ua = pm_ref[:, 0:D_A].astype(F32)
    va = pm_ref[:, D_A:2 * D_A].astype(F32)
    u_act = _gelu(ua)
    v_act = _gelu(va)
    bd = bd_ref[...]
    vc = v_act - _group_mean(v_act, bd)
    rstd = lax.rsqrt(_group_mean(vc * vc, bd) + EPS)
    vhat = vc * rstd
    vln = vhat * lng_ref[...] + lnb_ref[...]
    left = lax.broadcasted_iota(jnp.int32, (CHUNK, LANES), 1) < HEAD_DIM
    for q in range(n_sub):
        rows = slice(q * CHUNK, (q + 1) * CHUNK)
        for p in range(N_HEADS // 2):
            cols = slice(p * LANES, (p + 1) * LANES)
            vp = vln[rows, cols]
            v_l = jnp.where(left, vp, 0.0).astype(BF16)
            v_r = jnp.where(left, 0.0, vp).astype(BF16)
            mixed_scr[rows, cols] = _dot(wm_ref[2 * p], v_l) + _dot(wm_ref[2 * p + 1], v_r) + bias_ref[:, cols]
    return ua, va, u_act, vhat, rstd, vln


def _halo_specs(tm, tokens, width):
    prev = pl.BlockSpec((HALO, width), lambda i: (jnp.maximum(i * (tm // HALO) - 1, 0), 0))
    nxt = pl.BlockSpec((HALO, width), lambda i: (jnp.minimum((i + 1) * (tm // HALO), tokens // HALO - 1), 0))
    return prev, nxt


def _mixer_forward(proj, x, gate, w_out, wm, bias_full, lng, lnb, convw, og, bd, seq, name):
    tokens = proj.shape[0]
    tm = _tile_rows(seq)
    per_seq = seq // tm
    n_sub = tm // CHUNK

    def body(pm_ref, pp_ref, x_ref, gate_ref, wo_ref, wm_ref, bias_ref, lng_ref, lnb_ref, cw_ref, og_ref, bd_ref,
             y_ref, xo_ref, o_ref, mixed_scr):
        i = pl.program_id(0)
        first = (i % per_seq) == 0
        _, _, u_act, _, _, _ = _sgu_forward(pm_ref, wm_ref, bias_ref, lng_ref, lnb_ref, bd_ref, mixed_scr, n_sub)
        ya = u_act * mixed_scr[...]
        y_ref[:, 0:D_A] = (ya * _rms(ya) * og_ref[:, 0:D_A]).astype(BF16)

        bg = pm_ref[:, 2 * D_A:3 * D_A].astype(F32)
        z = pm_ref[:, 3 * D_A:4 * D_A].astype(F32) * pm_ref[:, 4 * D_A:5 * D_A].astype(F32)
        zp = pp_ref[:, 3 * D_A:4 * D_A].astype(F32) * pp_ref[:, 4 * D_A:5 * D_A].astype(F32)
        zp = jnp.where(first, 0.0, zp)
        zext = jnp.concatenate([zp, z], axis=0)
        z1 = pltpu.roll(zext, 1, 0)[HALO:]
        z2 = pltpu.roll(zext, 2, 0)[HALO:]
        conv = cw_ref[0:1, :] * z2 + cw_ref[1:2, :] * z1 + cw_ref[2:3, :] * z
        yb = bg * conv
        y_ref[:, D_A:2 * D_A] = (yb * _rms(yb) * og_ref[:, D_A:2 * D_A]).astype(BF16)

        f = _dot(y_ref[...], wo_ref[...])
        o_ref[...] = f.astype(BF16)
        xo_ref[...] = x_ref[...] + gate_ref[0] * f

    prev, _ = _halo_specs(tm, tokens, D_PROJ)
    tile = pl.BlockSpec((tm, D_MODEL), lambda i: (i, 0))
    (y, x_out, o), _ = _call(
        body, name=name, grid=(tokens // tm,),
        out_shape=[jax.ShapeDtypeStruct((tokens, D_MODEL), BF16), jax.ShapeDtypeStruct((tokens, D_MODEL), F32),
                   jax.ShapeDtypeStruct((tokens, D_MODEL), BF16)],
        in_specs=[pl.BlockSpec((tm, D_PROJ), lambda i: (i, 0)), prev, tile,
                  pl.BlockSpec((1, 1, D_MODEL), lambda i: (i // per_seq, 0, 0)), _resident(w_out.shape),
                  _resident(wm.shape), _resident(bias_full.shape), _resident(lng.shape), _resident(lnb.shape),
                  _resident(convw.shape), _resident(og.shape), _resident(bd.shape)],
        out_specs=[tile, tile, tile],
        scratch_shapes=[pltpu.VMEM((tm, D_A), F32)],
        operands=(proj, proj, x, gate, w_out, wm, bias_full, lng, lnb, convw, og, bd), parallel=True)
    return y, x_out, o


def _mixer_backward(proj, dx, gate, o, w_out, wm, bias_full, lng, lnb, convw, og, bd, causal, seq, name, comm=None):
    tokens = proj.shape[0]
    batch = tokens // seq
    tm = _tile_rows(seq)
    per_seq = seq // tm
    n_sub = tm // CHUNK
    ext = tm + 2 * HALO

    def body(pm_ref, pp_ref, pn_ref, dx_ref, dxn_ref, gate_ref, o_ref, wo_ref, wm_ref, bias_ref, lng_ref, lnb_ref, cw_ref,
             og_ref, bd_ref, causal_ref, do_ref, dgate_ref, dp_ref, dog_ref, dcw_ref, dlng_ref, dlnb_ref, dbias_ref, dwm_ref,
             mixed_scr, dvln_scr, dy_scr):
        i = pl.program_id(0)
        first = (i % per_seq) == 0
        last = (i % per_seq) == per_seq - 1

        dx_v = dx_ref[...]
        do_ref[...] = (gate_ref[0] * dx_v).astype(BF16)
        dgate = jnp.sum(dx_v * o_ref[...].astype(F32), axis=0, keepdims=True)
        dy_scr[...] = _dot_nt(do_ref[...], wo_ref[...])
        dyn_conv = _dot_nt((gate_ref[0] * dxn_ref[...]).astype(BF16), wo_ref[D_A:2 * D_A, :])

        @pl.when(first)
        def _():
            dgate_ref[0] = dgate

        @pl.when(jnp.logical_not(first))
        def _():
            dgate_ref[0] = dgate_ref[0] + dgate

        @pl.when(i == 0)
        def _():
            dog_ref[...] = jnp.zeros_like(dog_ref)
            dcw_ref[...] = jnp.zeros_like(dcw_ref)
            dlng_ref[...] = jnp.zeros_like(dlng_ref)
            dlnb_ref[...] = jnp.zeros_like(dlnb_ref)
            dbias_ref[...] = jnp.zeros_like(dbias_ref)
            dwm_ref[...] = jnp.zeros_like(dwm_ref)

        ua, va, u_act, vhat, rstd, vln = _sgu_forward(pm_ref, wm_ref, bias_ref, lng_ref, lnb_ref, bd_ref, mixed_scr, n_sub)
        mixed = mixed_scr[...]
        ya = u_act * mixed
        ra = _rms(ya)
        yhat = ya * ra
        dya_in = dy_scr[:, 0:D_A]
        dog_ref[:, 0:D_A] = dog_ref[:, 0:D_A] + jnp.sum(dya_in * yhat, axis=0, keepdims=True)
        dyh = dya_in * og_ref[:, 0:D_A]
        dya = ra * (dyh - yhat * jnp.mean(dyh * yhat, axis=-1, keepdims=True))
        d_u = dya * mixed
        d_mixed = dya * u_act
        left = lax.broadcasted_iota(jnp.int32, (CHUNK, LANES), 1) < HEAD_DIM
        dbias = jnp.zeros((CHUNK, D_A), F32)
        for q in range(n_sub):
            rows = slice(q * CHUNK, (q + 1) * CHUNK)
            dbias = dbias + d_mixed[rows, :]
            for p in range(N_HEADS // 2):
                cols = slice(p * LANES, (p + 1) * LANES)
                dm = d_mixed[rows, cols]
                dm_l = jnp.where(left, dm, 0.0).astype(BF16)
                dm_r = jnp.where(left, 0.0, dm).astype(BF16)
                vp = vln[rows, cols].astype(BF16)
                dwm_ref[2 * p] = dwm_ref[2 * p] + causal_ref[...] * _dot_nt(dm_l, vp)
                dwm_ref[2 * p + 1] = dwm_ref[2 * p + 1] + causal_ref[...] * _dot_nt(dm_r, vp)
                dvln_scr[rows, cols] = _dot_tn(wm_ref[2 * p], dm_l) + _dot_tn(wm_ref[2 * p + 1], dm_r)
        dbias_ref[...] = dbias_ref[...] + dbias
        dvln = dvln_scr[...]
        dlng_ref[...] = dlng_ref[...] + jnp.sum(dvln * vhat, axis=0, keepdims=True)
        dlnb_ref[...] = dlnb_ref[...] + jnp.sum(dvln, axis=0, keepdims=True)
        dvh = dvln * lng_ref[...]
        bd = bd_ref[...]
        d_v = rstd * (dvh - _group_mean(dvh, bd) - vhat * _group_mean(dvh * vhat, bd))
        dp_ref[:, 0:D_A] = (d_u * _gelu_grad(ua)).astype(BF16)
        dp_ref[:, D_A:2 * D_A] = (d_v * _gelu_grad(va)).astype(BF16)

        def ext_cols(lo):
            cs = slice(lo, lo + D_A)
            return jnp.concatenate([pp_ref[:, cs], pm_ref[:, cs], pn_ref[:, cs]], axis=0).astype(F32)

        bg, cg, xb = ext_cols(2 * D_A), ext_cols(3 * D_A), ext_cols(4 * D_A)
        row = lax.broadcasted_iota(jnp.int32, (ext, D_A), 0)
        z = jnp.where(jnp.logical_and(first, row < HALO), 0.0, cg * xb)
        z1 = pltpu.roll(z, 1, 0)
        z2 = pltpu.roll(z, 2, 0)
        w0, w1, w2 = cw_ref[0:1, :], cw_ref[1:2, :], cw_ref[2:3, :]
        conv = w0 * z2 + w1 * z1 + w2 * z
        yb = bg * conv
        rb = _rms(yb)
        yhb = yb * rb
        dyn = jnp.where(last, 0.0, dyn_conv)
        dyb_in = jnp.concatenate([jnp.zeros((HALO, D_A), F32), dy_scr[:, D_A:2 * D_A], dyn], axis=0)
        dyhb = dyb_in * og_ref[:, D_A:2 * D_A]
        dyb = rb * (dyhb - yhb * jnp.mean(dyhb * yhb, axis=-1, keepdims=True))
        d_conv = dyb * bg
        dz = w2 * d_conv + w1 * pltpu.roll(d_conv, ext - 1, 0) + w0 * pltpu.roll(d_conv, ext - 2, 0)
        main = slice(HALO, HALO + tm)
        dp_ref[:, 2 * D_A:3 * D_A] = (dyb * conv)[main].astype(BF16)
        dp_ref[:, 3 * D_A:4 * D_A] = (dz * xb)[main].astype(BF16)
        dp_ref[:, 4 * D_A:5 * D_A] = (dz * cg)[main].astype(BF16)
        dog_ref[:, D_A:2 * D_A] = dog_ref[:, D_A:2 * D_A] + jnp.sum((dyb_in * yhb)[main], axis=0, keepdims=True)
        dcm = d_conv[main]
        dcw_ref[0:1, :] = dcw_ref[0:1, :] + jnp.sum(dcm * z2[main], axis=0, keepdims=True)
        dcw_ref[1:2, :] = dcw_ref[1:2, :] + jnp.sum(dcm * z1[main], axis=0, keepdims=True)
        dcw_ref[2:3, :] = dcw_ref[2:3, :] + jnp.sum(dcm * z[main], axis=0, keepdims=True)

    prev_p, next_p = _halo_specs(tm, tokens, D_PROJ)
    _, next_d = _halo_specs(tm, tokens, D_MODEL)
    fixed2 = lambda shape: pl.BlockSpec(shape, lambda i: (0, 0))
    tile = pl.BlockSpec((tm, D_MODEL), lambda i: (i, 0))
    per_batch = pl.BlockSpec((1, 1, D_MODEL), lambda i: (i // per_seq, 0, 0))
    outs, got = _call(
        body, name=name, grid=(tokens // tm,),
        out_shape=[jax.ShapeDtypeStruct((tokens, D_MODEL), BF16), jax.ShapeDtypeStruct((batch, 1, D_MODEL), F32),
                   jax.ShapeDtypeStruct((tokens, D_PROJ), BF16), jax.ShapeDtypeStruct((1, D_MODEL), F32),
                   jax.ShapeDtypeStruct((8, D_A), F32), jax.ShapeDtypeStruct((1, D_A), F32),
                   jax.ShapeDtypeStruct((1, D_A), F32), jax.ShapeDtypeStruct((CHUNK, D_A), F32),
                   jax.ShapeDtypeStruct((N_HEADS, CHUNK, CHUNK), F32)],
        in_specs=[pl.BlockSpec((tm, D_PROJ), lambda i: (i, 0)), prev_p, next_p, tile, next_d, per_batch, tile,
                  _resident(w_out.shape), _resident(wm.shape), _resident(bias_full.shape), _resident(lng.shape),
                  _resident(lnb.shape), _resident(convw.shape), _resident(og.shape), _resident(bd.shape),
                  _resident(causal.shape)],
        out_specs=[tile, per_batch, pl.BlockSpec((tm, D_PROJ), lambda i: (i, 0)), fixed2((1, D_MODEL)), fixed2((8, D_A)),
                   fixed2((1, D_A)), fixed2((1, D_A)), fixed2((CHUNK, D_A)),
                   pl.BlockSpec((N_HEADS, CHUNK, CHUNK), lambda i: (0, 0, 0))],
        scratch_shapes=[pltpu.VMEM((tm, D_A), F32), pltpu.VMEM((tm, D_A), F32), pltpu.VMEM((tm, D_MODEL), F32)],
        operands=(proj, proj, proj, dx, dx, gate, o, w_out, wm, bias_full, lng, lnb, convw, og, bd, causal), comm=comm)
    return (*outs, got)


def _adamw_update(wv, gv, mv, vv):
    nm = ADAM_B1 * mv + (1.0 - ADAM_B1) * gv
    nv = ADAM_B2 * vv + (1.0 - ADAM_B2) * (gv * gv)
    m_hat = nm / (1.0 - ADAM_B1 ** ADAM_STEP)
    v_hat = nv / (1.0 - ADAM_B2 ** ADAM_STEP)
    return -ADAM_LR * (m_hat / (jnp.sqrt(v_hat) + ADAM_EPS) + ADAM_WD * wv), nm, nv


def _adamw_rows(recv, w, m, v, name):
    depth, rows, cols = w.shape
    tr = rows // 2
    last = rows // tr - 1

    def body(*refs):
        r_refs, (w_ref, m_ref, v_ref, g_ref, d_ref, nm_ref, nv_ref) = refs[:depth], refs[depth:]
        for l in range(depth):
            @pl.when(pl.program_id(0) == l)
            def _(r_ref=r_refs[l]):
                acc = r_ref[0].astype(F32)
                for d in range(1, N_DEV):
                    acc = acc + r_ref[d].astype(F32)
                g_ref[0] = acc
                d_ref[0], nm_ref[0], nv_ref[0] = _adamw_update(w_ref[0], acc, m_ref[0], v_ref[0])

    def slots(l):
        return pl.BlockSpec((N_DEV, tr, cols), lambda ll, i: (0, jnp.where(ll == l, i, jnp.where(ll < l, 0, last)), 0))

    spec = pl.BlockSpec((1, tr, cols), lambda ll, i: (ll, i, 0))
    return pl.pallas_call(
        body, name=name, grid=(depth, rows // tr),
        out_shape=[jax.ShapeDtypeStruct((depth, rows, cols), F32)] * 4,
        in_specs=[slots(l) for l in range(depth)] + [spec] * 3, out_specs=[spec] * 4,
        compiler_params=_params(2),
    )(*recv, w, m, v)


def _adamw(w, g, m, v, name):
    rows, cols = w.shape
    tr = max(t for t in range(8, 513, 8) if rows % t == 0)

    def body(w_ref, g_ref, m_ref, v_ref, d_ref, nm_ref, nv_ref):
        d_ref[...], nm_ref[...], nv_ref[...] = _adamw_update(w_ref[...], g_ref[...], m_ref[...], v_ref[...])

    spec = pl.BlockSpec((tr, cols), lambda i: (i, 0))
    return pl.pallas_call(
        body, name=name, grid=(rows // tr,),
        out_shape=[jax.ShapeDtypeStruct((rows, cols), F32)] * 3,
        in_specs=[spec] * 4, out_specs=[spec] * 3,
        compiler_params=_params(parallel=True),
    )(w, g, m, v)


def _adamw_many(ws, gs, ms, vs, name):
    n = len(ws)
    two_d = lambda a: a.reshape(-1, a.shape[-1])

    def body(*refs):
        w_refs, g_refs, m_refs, v_refs = refs[:n], refs[n:2 * n], refs[2 * n:3 * n], refs[3 * n:4 * n]
        d_refs, nm_refs, nv_refs = refs[4 * n:5 * n], refs[5 * n:6 * n], refs[6 * n:]
        for k in range(n):
            d_refs[k][...], nm_refs[k][...], nv_refs[k][...] = _adamw_update(
                w_refs[k][...], g_refs[k][...], m_refs[k][...], v_refs[k][...])

    flat = [two_d(a) for a in ws]
    outs = pl.pallas_call(
        body, name=name, out_shape=[jax.ShapeDtypeStruct(a.shape, F32) for a in flat] * 3,
        in_specs=[pl.BlockSpec(memory_space=pltpu.VMEM)] * (4 * n),
        out_specs=[pl.BlockSpec(memory_space=pltpu.VMEM)] * (3 * n),
        compiler_params=pltpu.CompilerParams(vmem_limit_bytes=VMEM_LIMIT),
    )(*flat, *[two_d(a) for a in gs], *[two_d(a) for a in ms], *[two_d(a) for a in vs])
    shaped = [o.reshape(ws[k % n].shape) for k, o in enumerate(outs)]
    return shaped[:n], shaped[n:2 * n], shaped[2 * n:]


def _adamw_nd(w, g, m, v, name):
    shape = w.shape
    two_d = (-1, shape[-1])
    d, nm, nv = _adamw(w.reshape(two_d), g.reshape(two_d), m.reshape(two_d), v.reshape(two_d), name)
    return d.reshape(shape), nm.reshape(shape), nv.reshape(shape)


def kernel(x, c, ada_w, ada_b, norm_ffn1_g, ffn1_w_gu, ffn1_w_down, norm_mix_g, mix_w_in, sgu_ln_g, sgu_ln_b, sgu_w_s, sgu_b, conv_w, out_norm_g, mix_w_out, norm_ffn2_g, ffn2_w_gu, ffn2_w_down, final_norm_g, loss_target, m_ada_w, m_ada_b, m_norm_ffn1_g, m_ffn1_w_gu, m_ffn1_w_down, m_norm_mix_g, m_mix_w_in, m_sgu_ln_g, m_sgu_ln_b, m_sgu_w_s, m_sgu_b, m_conv_w, m_out_norm_g, m_mix_w_out, m_norm_ffn2_g, m_ffn2_w_gu, m_ffn2_w_down, m_final_norm_g, v_ada_w, v_ada_b, v_norm_ffn1_g, v_ffn1_w_gu, v_ffn1_w_down, v_norm_mix_g, v_mix_w_in, v_sgu_ln_g, v_sgu_ln_b, v_sgu_w_s, v_sgu_b, v_conv_w, v_out_norm_g, v_mix_w_out, v_norm_ffn2_g, v_ffn2_w_gu, v_ffn2_w_down, v_final_norm_g):
    batch, seq, _ = x.shape
    tokens = batch * seq
    me = 4 * lax.axis_index("x") + 2 * lax.axis_index("y") + lax.axis_index("c")
    weights = dict(ada_w=ada_w, ada_b=ada_b, norm_ffn1_g=norm_ffn1_g, ffn1_w_gu=ffn1_w_gu, ffn1_w_down=ffn1_w_down,
                   norm_mix_g=norm_mix_g, mix_w_in=mix_w_in, sgu_ln_g=sgu_ln_g, sgu_ln_b=sgu_ln_b, sgu_w_s=sgu_w_s,
                   sgu_b=sgu_b, conv_w=conv_w, out_norm_g=out_norm_g, mix_w_out=mix_w_out, norm_ffn2_g=norm_ffn2_g,
                   ffn2_w_gu=ffn2_w_gu, ffn2_w_down=ffn2_w_down, final_norm_g=final_norm_g)
    mom1 = dict(ada_w=m_ada_w, ada_b=m_ada_b, norm_ffn1_g=m_norm_ffn1_g, ffn1_w_gu=m_ffn1_w_gu,
                ffn1_w_down=m_ffn1_w_down, norm_mix_g=m_norm_mix_g, mix_w_in=m_mix_w_in, sgu_ln_g=m_sgu_ln_g,
                sgu_ln_b=m_sgu_ln_b, sgu_w_s=m_sgu_w_s, sgu_b=m_sgu_b, conv_w=m_conv_w, out_norm_g=m_out_norm_g,
                mix_w_out=m_mix_w_out, norm_ffn2_g=m_norm_ffn2_g, ffn2_w_gu=m_ffn2_w_gu, ffn2_w_down=m_ffn2_w_down,
                final_norm_g=m_final_norm_g)
    mom2 = dict(ada_w=v_ada_w, ada_b=v_ada_b, norm_ffn1_g=v_norm_ffn1_g, ffn1_w_gu=v_ffn1_w_gu,
                ffn1_w_down=v_ffn1_w_down, norm_mix_g=v_norm_mix_g, mix_w_in=v_mix_w_in, sgu_ln_g=v_sgu_ln_g,
                sgu_ln_b=v_sgu_ln_b, sgu_w_s=v_sgu_w_s, sgu_b=v_sgu_b, conv_w=v_conv_w, out_norm_g=v_out_norm_g,
                mix_w_out=v_mix_w_out, norm_ffn2_g=v_norm_ffn2_g, ffn2_w_gu=v_ffn2_w_gu, ffn2_w_down=v_ffn2_w_down,
                final_norm_g=v_final_norm_g)

    big = ("ffn1_w_gu", "ffn1_w_down", "mix_w_in", "mix_w_out", "ffn2_w_gu", "ffn2_w_down")
    transposed = ("ffn1_w_gu", "mix_w_in", "ffn2_w_gu")
    as_rows = lambda nm, a: jnp.swapaxes(a, 1, 2) if nm in transposed else a
    shard = {(l, nm): as_rows(nm, weights[nm])[l].astype(BF16) for l in range(DEPTH) for nm in big}
    full_w = {}

    def gather_of(keys):
        return keys, _GatherRows([shard[k] for k in keys])

    def landed(plan, got):
        full_w.update(zip(plan[0], got))

    ada_cols = ada_w.shape[2]
    ada_b_cols = lax.dynamic_slice_in_dim(ada_b, me * ada_cols, ada_cols, axis=1).reshape(DEPTH, 1, ada_cols)
    plan = gather_of([(0, "ffn1_w_gu")])
    c_dev, convw_dev, ada_recv, got = _prologue(
        jnp.pad(c, ((0, 8 - batch), (0, 0))), jnp.pad(conv_w.reshape(-1), (0, 8 * LANES - conv_w.size)).reshape(8, LANES),
        ada_w, ada_b_cols, plan[1])
    landed(plan, got)
    c_all = c_dev[:, :batch].reshape(N_DEV * batch, D_MODEL)
    convw_all = convw_dev.reshape(N_DEV, -1)[:, :conv_w.size].reshape((N_DEV,) + conv_w.shape)
    convw_full = jnp.transpose(convw_all, (1, 2, 0, 3)).reshape(DEPTH, 3, D_A)
    ada_mine = jnp.transpose(ada_recv[:, :, :batch, :], (1, 2, 0, 3)).reshape(DEPTH, batch, N_MOD * D_MODEL)
    mod = ada_mine.reshape(DEPTH, batch, N_MOD, 1, D_MODEL)

    causal = jnp.tril(jnp.ones((CHUNK, CHUNK), F32))
    bd = jnp.kron(jnp.eye(N_HEADS, dtype=F32), jnp.full((HEAD_DIM, HEAD_DIM), 1.0 / HEAD_DIM, F32)).astype(BF16)
    row_vec = lambda a: a.reshape(1, -1)

    hosted_gathers = {
        (0, "ffn1"): [(0, "ffn1_w_down"), (0, "mix_w_in"), (0, "mix_w_out")],
        (0, "ffn_down1"): [(0, "ffn2_w_gu")],
        (0, "mix_in"): [(0, "ffn2_w_down")],
        (0, "ffn2"): [(1, "ffn1_w_gu"), (1, "ffn1_w_down"), (1, "mix_w_in"), (1, "mix_w_out")],
        (1, "ffn1"): [(1, "ffn2_w_gu"), (1, "ffn2_w_down")],
    }

    def hosting(l, site):
        keys = hosted_gathers.get((l, site))
        return gather_of(keys) if keys else (None, None)

    xs = x.reshape(tokens, D_MODEL)
    saved = []
    for l in range(DEPTH):
        sh1, sc1, g1, sh2, sc2, g2, sh3, sc3, g3 = [mod[l, :, k] for k in range(N_MOD)]
        mixer_consts = dict(
            wm=(sgu_w_s[l] * causal[None]).astype(BF16),
            bias_full=jnp.repeat(sgu_b[l].T, HEAD_DIM, axis=1),
            lng=row_vec(jnp.tile(sgu_ln_g[l], N_HEADS)), lnb=row_vec(jnp.tile(sgu_ln_b[l], N_HEADS)),
            convw=jnp.pad(convw_full[l], ((0, 5), (0, 0))), og=row_vec(out_norm_g[l]), bd=bd)
        x0 = xs
        plan = hosting(l, "ffn1")
        if l == 0:
            h1, a1, s1, w1, got = _normmod_matmul(x0, row_vec(norm_ffn1_g[l]), 1.0 + sc1, sh1, full_w[l, "ffn1_w_gu"], seq, "ffn_up", True, plan[1])
            landed(plan, got)
            plan = hosting(l, "ffn_down1")
            x1, f1, got = _matmul_residual(a1, full_w[l, "ffn1_w_down"], x0, g1, 0.5, seq, "ffn_down", plan[1])
        else:
            h1, a1, s1, w1, x1, f1, got = _ffn_forward(
                x0, row_vec(norm_ffn1_g[l]), 1.0 + sc1, sh1, full_w[l, "ffn1_w_gu"], full_w[l, "ffn1_w_down"], g1, 0.5, seq, "ffn_fwd",
                comm=plan[1])
        if got:
            landed(plan, got)
        plan = hosting(l, "mix_in")
        h2, proj, got = _normmod_matmul(x1, row_vec(norm_mix_g[l]), 1.0 + sc2, sh2, full_w[l, "mix_w_in"], seq, "mix_in", False, plan[1])
        if got:
            landed(plan, got)
        ymix, x2, o2 = _mixer_forward(proj, x1, g2, full_w[l, "mix_w_out"], seq=seq, name="mixer_forward", **mixer_consts)
        plan = hosting(l, "ffn2")
        if l + 1 < DEPTH:
            h3, a3, s3, w3, x3, f3, got = _ffn_forward(
                x2, row_vec(norm_ffn2_g[l]), 1.0 + sc3, sh3, full_w[l, "ffn2_w_gu"], full_w[l, "ffn2_w_down"], g3, 0.5, seq, "ffn_fwd",
                comm=plan[1])
        else:
            head = (loss_target.reshape(tokens, D_MODEL), row_vec(final_norm_g))
            h3, a3, s3, w3, x3, f3, d_final_g, loss_cols, got = _ffn_forward(
                x2, row_vec(norm_ffn2_g[l]), 1.0 + sc3, sh3, full_w[l, "ffn2_w_gu"], full_w[l, "ffn2_w_down"], g3, 0.5, seq, "ffn_fwd_loss",
                loss_head=head, comm=plan[1])
        if got:
            landed(plan, got)
        saved.append(dict(x0=x0, x1=x1, x2=x2, h1=h1, h2=h2, h3=h3, a1=a1, s1=s1, w1=w1, a3=a3, s3=s3, w3=w3, f1=f1, f3=f3, o2=o2, proj=proj,
                          ymix=ymix, mixer_consts=mixer_consts, sc=(1.0 + sc1, 1.0 + sc2, 1.0 + sc3), gates=(g1, g2, g3)))
        xs = x3

    dx = xs

    recv = {}
    small_grads = [None] * DEPTH
    d_mod = [None] * DEPTH

    mix_names = ("out_norm_g", "sgu_ln_g", "sgu_ln_b", "sgu_w_s", "sgu_b", "conv_w")
    late_names = ("norm_ffn1_g", "norm_mix_g", "norm_ffn2_g")

    def mix_parts(l):
        return [small_grads[l][nm] for nm in mix_names]

    def late_parts(l):
        return [small_grads[l][nm] for nm in late_names] + [d_mod[l]]

    pending = []

    def scatter_later(l, nm, grad):
        pending.append(((l, nm), _ScatterRows([grad])))

    def host():
        keys, parts = [k for k, _ in pending], [p for _, p in pending]
        pending.clear()
        return keys, (_Exchanges(parts) if parts else None)

    def hosted(keys, got):
        if got:
            recv.update(zip(keys, got))

    for l in reversed(range(DEPTH)):
        sv = saved[l]
        mc = sv["mixer_consts"]
        if l + 1 < DEPTH:
            pending.append((("late", l + 1), _GatherRows([_pack_small(late_parts(l + 1))])))
        keys, comm = host()
        df3, dg3, dgu3, dx2, dsh3, dsc3, dn3, got = _ffn_backward(
            dx, sv["gates"][2], sv["f3"], sv["s3"], sv["w3"], full_w[l, "ffn2_w_down"], full_w[l, "ffn2_w_gu"], sv["x2"],
            row_vec(norm_ffn2_g[l]), sv["sc"][2], 0.5, seq, "ffn_bwd", comm)
        hosted(keys, got)
        gw_down2, _ = _weight_grad(sv["a3"], df3, seq, "grad_w_down")
        scatter_later(l, "ffn2_w_down", gw_down2)
        keys, comm = host()
        gw_gu2, got = _weight_grad(dgu3, sv["h3"], seq, "grad_w_gu", comm)
        hosted(keys, got)
        scatter_later(l, "ffn2_w_gu", gw_gu2)
        keys, comm = host()
        do2, dg2, dproj, d_og, d_cw, d_lng, d_lnb, d_bias, d_wm, got = _mixer_backward(
            sv["proj"], dx2, sv["gates"][1], sv["o2"], full_w[l, "mix_w_out"], causal=causal, seq=seq, name="mixer_backward",
            comm=comm, **mc)
        hosted(keys, got)
        gw_out, _ = _weight_grad(sv["ymix"], do2, seq, "grad_w_out")
        small_grads[l] = dict(
            out_norm_g=d_og, sgu_ln_g=d_lng.reshape(N_HEADS, HEAD_DIM).sum(0), sgu_ln_b=d_lnb.reshape(N_HEADS, HEAD_DIM).sum(0),
            sgu_w_s=d_wm, sgu_b=d_bias.reshape(CHUNK, N_HEADS, HEAD_DIM).sum(-1).T, conv_w=d_cw[0:3])
        pending.append((("mix", l), _GatherRows([_pack_small(mix_parts(l))])))
        scatter_later(l, "mix_w_out", gw_out)
        keys, comm = host()
        dx1, dsh2, dsc2, dn2, got = _matmul_normmod_backward(dproj, full_w[l, "mix_w_in"], sv["x1"], dx2, row_vec(norm_mix_g[l]), sv["sc"][1], seq, "mix_in_bwd", comm)
        hosted(keys, got)
        gw_in, _ = _weight_grad(dproj, sv["h2"], seq, "grad_w_in")
        scatter_later(l, "mix_w_in", gw_in)
        keys, comm = host()
        if l > 0:
            df1, dg1, dgu1, dx0, dsh1, dsc1, dn1, got = _ffn_backward(
                dx1, sv["gates"][0], sv["f1"], sv["s1"], sv["w1"], full_w[l, "ffn1_w_down"], full_w[l, "ffn1_w_gu"], sv["x0"],
                row_vec(norm_ffn1_g[l]), sv["sc"][0], 0.5, seq, "ffn_bwd", comm)
        else:
            df1, dg1, dgu1, got = _residual_backward(dx1, sv["gates"][0], sv["f1"], full_w[l, "ffn1_w_down"], 0.5, sv["s1"], sv["w1"], seq, "ffn_down_bwd", comm)
        hosted(keys, got)
        gw_down1, _ = _weight_grad(sv["a1"], df1, seq, "grad_w_down")
        scatter_later(l, "ffn1_w_down", gw_down1)
        keys, comm = host()
        gw_gu1, got = _weight_grad(dgu1, sv["h1"], seq, "grad_w_gu", comm)
        hosted(keys, got)
        scatter_later(l, "ffn1_w_gu", gw_gu1)
        if l == 0:
            keys, comm = host()
            dx0, dsh1, dsc1, dn1, got = _matmul_normmod_backward(dgu1, full_w[l, "ffn1_w_gu"], sv["x0"], dx1, row_vec(norm_ffn1_g[l]), sv["sc"][0], seq, "ffn_up_bwd", comm)
            hosted(keys, got)
        dx = dx0
        small_grads[l].update(norm_ffn1_g=dn1, norm_mix_g=dn2, norm_ffn2_g=dn3)
        d_mod[l] = jnp.concatenate([dsh1, dsc1, dg1, dsh2, dsc2, dg2, dsh3, dsc3, dg3], axis=1)
    grad_x = dx.reshape(batch, seq, D_MODEL)

    grad_big, delta, new_m, new_v = {}, {}, {}, {}
    for nm in big:
        results = _adamw_rows([recv[l, nm] for l in range(DEPTH)], as_rows(nm, weights[nm]), as_rows(nm, mom1[nm]),
                              as_rows(nm, mom2[nm]), "adamw_" + nm)
        grad_big[nm], delta[nm], new_m[nm], new_v[nm] = [as_rows(nm, r) for r in results]

    last_parts = late_parts(0) + [d_final_g, loss_cols]
    last_shapes = [p.shape for p in last_parts]
    packed_all, packed_sum = _all_gather_small(_pack_small(last_parts), "reduce_small")
    late_sum = {0: _unpack_small(packed_sum, last_shapes)}
    d_mod_dev = {0: _unpack_small(packed_all, last_shapes, lead=(N_DEV,))[len(late_names)]}
    mix_sum = {}
    for l in range(DEPTH):
        gathered = recv["mix", l].reshape(N_DEV, -1, LANES)
        mix_sum[l] = _unpack_small(_sum_gathered(gathered, "sum_mix"), [p.shape for p in mix_parts(l)])
        if l > 0:
            shapes_l = [p.shape for p in late_parts(l)]
            gathered = recv["late", l].reshape(N_DEV, -1, LANES)
            late_sum[l] = _unpack_small(_sum_gathered(gathered, "sum_late"), shapes_l)
            d_mod_dev[l] = _unpack_small(gathered, shapes_l, lead=(N_DEV,))[len(late_names)]
    grad_small = {}
    for group, names in ((mix_sum, mix_names), (late_sum, late_names)):
        for k, nm in enumerate(names):
            grad_small[nm] = jnp.stack([group[l][k] for l in range(DEPTH)]).reshape(
                (DEPTH, 3, D_A) if nm == "conv_w" else weights[nm].shape)
    grad_small["conv_w"] = lax.dynamic_slice_in_dim(grad_small["conv_w"], me * conv_w.shape[2], conv_w.shape[2], axis=2)
    grad_small["final_norm_g"] = late_sum[0][len(late_names) + 1].reshape(final_norm_g.shape)
    loss = jnp.sum(late_sum[0][len(late_names) + 2])
    d_ada_all = jnp.stack([d_mod_dev[l] for l in range(DEPTH)]).reshape(DEPTH, N_DEV * batch, N_MOD * D_MODEL)
    d_ada_cols = lax.dynamic_slice_in_dim(d_ada_all, me * ada_cols, ada_cols, axis=2)
    g_ada_w, g_ada_b = _ada_backward(c_all, d_ada_cols, d_ada_all)

    grads = dict(grad_big)
    grads.update(grad_small)
    grads["ada_w"] = g_ada_w
    grads["ada_b"] = g_ada_b.reshape(ada_b.shape)

    names = ("ada_w", "ada_b", "norm_ffn1_g", "ffn1_w_gu", "ffn1_w_down", "norm_mix_g", "mix_w_in", "sgu_ln_g",
             "sgu_ln_b", "sgu_w_s", "sgu_b", "conv_w", "out_norm_g", "mix_w_out", "norm_ffn2_g", "ffn2_w_gu",
             "ffn2_w_down", "final_norm_g")
    delta["ada_w"], new_m["ada_w"], new_v["ada_w"] = _adamw_nd(ada_w, grads["ada_w"], m_ada_w, v_ada_w, "adamw_ada_w")
    rest = [nm for nm in names if nm not in big and nm != "ada_w"]
    pick = lambda src: [src[nm] for nm in rest]
    for nm, d_k, m_k, v_k in zip(rest, *_adamw_many(pick(weights), pick(grads), pick(mom1), pick(mom2), "adamw_small")):
        delta[nm], new_m[nm], new_v[nm] = d_k, m_k, v_k

    return (loss, grad_x, *[grads[nm] for nm in names], *[delta[nm] for nm in names],
            *[new_m[nm] for nm in names], *[new_v[nm] for nm in names])
```

```python
import math

import jax
import jax.numpy as jnp
from jax import lax
from jax.experimental import pallas as pl
from jax.experimental.pallas import tpu as pltpu

F32 = jnp.float32
BF16 = jnp.bfloat16

D_MODEL = 1024
D_FF = 2816
D_A = 512
D_PROJ = 2560
N_HEADS = 8
HEAD_DIM = 64
CHUNK = 128
N_MOD = 9
DEPTH = 2
EPS = 1e-6
N_DEV = 8
LANES = 128
MXU_N = 256
HALO = 16
VMEM_LIMIT = 56 * 1024 * 1024
FORWARD_STEPS = 4

ADAM_LR = 0.001
ADAM_B1 = 0.9
ADAM_B2 = 0.999
ADAM_EPS = 1e-08
ADAM_WD = 0.01
ADAM_STEP = 10

MESH = pl.DeviceIdType.MESH


def _dot(a, b):
    return jnp.dot(a, b, preferred_element_type=F32)


def _dot_nt(a, b):
    return lax.dot_general(a, b, (((1,), (1,)), ((), ())), preferred_element_type=F32)


def _dot_tn(a, b):
    return lax.dot_general(a, b, (((0,), (0,)), ((), ())), preferred_element_type=F32)


def _sigmoid(x):
    return 0.5 * jnp.tanh(0.5 * x) + 0.5


def _gelu(x):
    return 0.5 * x * (1.0 + lax.erf(x * (1.0 / math.sqrt(2.0))))


def _gelu_grad(x):
    cdf = 0.5 * (1.0 + lax.erf(x * (1.0 / math.sqrt(2.0))))
    return cdf + x * jnp.exp(-0.5 * x * x) * (1.0 / math.sqrt(2.0 * math.pi))


def _params(n_axes=1, parallel=False):
    sem = ("parallel" if parallel else "arbitrary",) * n_axes
    return pltpu.CompilerParams(dimension_semantics=sem, vmem_limit_bytes=VMEM_LIMIT)


def _resident(shape):
    nd = len(shape)
    return pl.BlockSpec(shape, lambda *_: (0,) * nd, pipeline_mode=pl.Buffered(1))


def _tile_rows(seq):
    return min(512, seq)


def _my_position():
    x, y, c = lax.axis_index("x"), lax.axis_index("y"), lax.axis_index("c")
    return x, y, c, 4 * x + 2 * y + c


def _peer(x, y, c, p):
    return (x ^ ((p >> 2) & 1), y ^ ((p >> 1) & 1), c ^ (p & 1))


class _GatherRows:
    def __init__(self, shards):
        self.operands = list(shards)
        n = len(shards)
        self.out_shape = [jax.ShapeDtypeStruct((N_DEV * s.shape[0], s.shape[1]), s.dtype) for s in shards]
        self.scratch = [pltpu.SemaphoreType.DMA((n, N_DEV - 1)), pltpu.SemaphoreType.DMA((n, N_DEV - 1)),
                        pltpu.SemaphoreType.DMA((n,))]

    def _plan(self, src, dst, send, recv, loc):
        x, y, c, _ = _my_position()
        me, sib = (x, y, c), (x, y, 1 - c)
        chips = [(1 - x, y), (x, 1 - y), (1 - x, 1 - y)]
        plans = []
        for k, shard in enumerate(self.operands):
            rows = shard.shape[0]

            def blk(pos, k=k, rows=rows):
                return dst[k].at[pl.ds((4 * pos[0] + 2 * pos[1] + pos[2]) * rows, rows), :]

            def rc(s, block, to, source=None, k=k, blk=blk):
                return pltpu.make_async_remote_copy(
                    src_ref=blk(block) if source is None else source, dst_ref=blk(block),
                    send_sem=send.at[k, s], recv_sem=recv.at[k, s], device_id=to, device_id_type=MESH)

            plans.append(dict(
                local=pltpu.make_async_copy(src[k], blk(me), loc.at[k]),
                first=[rc(0, me, sib, src[k])] + [rc(1 + j, me, (*chip, c), src[k]) for j, chip in enumerate(chips)],
                landed=[rc(1 + j, (*chip, c), me) for j, chip in enumerate(chips)],
                passed=[rc(4 + j, (*chip, c), sib) for j, chip in enumerate(chips)],
                from_sib=[rc(0, sib, me)] + [rc(4 + j, (*chip, 1 - c), me) for j, chip in enumerate(chips)]))
        return plans

    def start(self, src, dst, send, recv, loc):
        for plan in self._plan(src, dst, send, recv, loc):
            plan["local"].start()
            for cp in plan["first"]:
                cp.start()

    def forward(self, src, dst, send, recv, loc):
        for plan in self._plan(src, dst, send, recv, loc):
            for landed, passed in zip(plan["landed"], plan["passed"]):
                landed.wait_recv()
                passed.start()

    def finish(self, src, dst, send, recv, loc):
        for plan in self._plan(src, dst, send, recv, loc):
            for cp in plan["from_sib"]:
                cp.wait_recv()
            for cp in plan["first"] + plan["passed"]:
                cp.wait_send()
            plan["local"].wait()


class _ScatterRows:
    def __init__(self, grads):
        self.operands = list(grads)
        n = len(grads)
        self.out_shape = [jax.ShapeDtypeStruct((N_DEV, g.shape[0] // N_DEV, g.shape[1]), g.dtype) for g in grads]
        self.scratch = [pltpu.SemaphoreType.DMA((n, N_DEV - 1)), pltpu.SemaphoreType.DMA((n, N_DEV - 1)),
                        pltpu.SemaphoreType.DMA((n,))]

    def _plan(self, src, dst, send, recv, loc):
        x, y, c, me = _my_position()
        copies = []
        for k, grad in enumerate(self.operands):
            rows = grad.shape[0] // N_DEV
            copies.append(pltpu.make_async_copy(src[k].at[pl.ds(me * rows, rows), :], dst[k].at[me], loc.at[k]))
            for p in range(1, N_DEV):
                px, py, pc = _peer(x, y, c, p)
                copies.append(pltpu.make_async_remote_copy(
                    src_ref=src[k].at[pl.ds((4 * px + 2 * py + pc) * rows, rows), :], dst_ref=dst[k].at[me],
                    send_sem=send.at[k, p - 1], recv_sem=recv.at[k, p - 1], device_id=(px, py, pc), device_id_type=MESH))
        return copies

    def start(self, src, dst, send, recv, loc):
        for cp in self._plan(src, dst, send, recv, loc):
            cp.start()

    def forward(self, src, dst, send, recv, loc):
        pass

    def finish(self, src, dst, send, recv, loc):
        for cp in self._plan(src, dst, send, recv, loc):
            cp.wait()


class _Exchanges:
    def __init__(self, parts):
        self.parts = list(parts)
        self.operands = [op for part in self.parts for op in part.operands]
        self.out_shape = [shp for part in self.parts for shp in part.out_shape]
        self.scratch = [scr for part in self.parts for scr in part.scratch]

    def _each(self, src, dst, sems):
        at, sem_at = 0, 0
        for part in self.parts:
            n, n_sem = len(part.operands), len(part.scratch)
            yield part, src[at:at + n], dst[at:at + n], sems[sem_at:sem_at + n_sem]
            at, sem_at = at + n, sem_at + n_sem

    def start(self, src, dst, *sems):
        for part, part_src, part_dst, part_sems in self._each(src, dst, sems):
            part.start(part_src, part_dst, *part_sems)

    def forward(self, src, dst, *sems):
        for part, part_src, part_dst, part_sems in self._each(src, dst, sems):
            part.forward(part_src, part_dst, *part_sems)

    def finish(self, src, dst, *sems):
        for part, part_src, part_dst, part_sems in self._each(src, dst, sems):
            part.finish(part_src, part_dst, *part_sems)


_ANY = pl.BlockSpec(memory_space=pl.ANY)


def _call(body, *, name, grid, in_specs, out_specs, out_shape, operands, scratch_shapes=(), parallel=False, comm=None):
    n_axes = len(grid)
    if comm is None:
        outs = pl.pallas_call(body, name=name, grid=grid, out_shape=list(out_shape), in_specs=list(in_specs),
                              out_specs=list(out_specs), scratch_shapes=list(scratch_shapes),
                              compiler_params=_params(n_axes, parallel))(*operands)
        return list(outs), None
    n_in, n_out, n_scr, n_c = len(in_specs), len(out_specs), len(scratch_shapes), len(comm.operands)
    total = math.prod(grid)

    def hosted(*refs):
        ins, c_src = refs[:n_in], refs[n_in:n_in + n_c]
        outs, c_dst = refs[n_in + n_c:n_in + n_c + n_out], refs[n_in + n_c + n_out:n_in + 2 * n_c + n_out]
        scr, sems = refs[n_in + 2 * n_c + n_out:n_in + 2 * n_c + n_out + n_scr], refs[n_in + 2 * n_c + n_out + n_scr:]
        step = pl.program_id(0)
        for axis in range(1, n_axes):
            step = step * grid[axis] + pl.program_id(axis)

        @pl.when(step == 0)
        def _():
            comm.start(c_src, c_dst, *sems)

        @pl.when(step == max(total - FORWARD_STEPS, 0))
        def _():
            comm.forward(c_src, c_dst, *sems)

        body(*ins, *outs, *scr)

        @pl.when(step == total - 1)
        def _():
            comm.finish(c_src, c_dst, *sems)

    res = pl.pallas_call(hosted, name=name, grid=grid, out_shape=list(out_shape) + comm.out_shape,
                         in_specs=list(in_specs) + [_ANY] * n_c, out_specs=list(out_specs) + [_ANY] * n_c,
                         scratch_shapes=list(scratch_shapes) + comm.scratch,
                         compiler_params=_params(n_axes, False))(*operands, *comm.operands)
    return list(res[:n_out]), list(res[n_out:])


def _all_gather_small(v, name):
    rows = v.shape[0]

    def body(v_ref, all_ref, sum_ref, send_sems, recv_sems):
        x, y, c, me = _my_position()
        all_ref[me] = v_ref[...]
        copies = []
        for p in range(1, N_DEV):
            cp = pltpu.make_async_remote_copy(
                src_ref=v_ref, dst_ref=all_ref.at[me], send_sem=send_sems.at[p - 1], recv_sem=recv_sems.at[p - 1],
                device_id=_peer(x, y, c, p), device_id_type=MESH)
            cp.start()
            copies.append(cp)
        for cp in copies:
            cp.wait()
        acc = all_ref[0]
        for d in range(1, N_DEV):
            acc = acc + all_ref[d]
        sum_ref[...] = acc

    return pl.pallas_call(
        body, name=name,
        out_shape=[jax.ShapeDtypeStruct((N_DEV, rows, LANES), F32), jax.ShapeDtypeStruct((rows, LANES), F32)],
        in_specs=[pl.BlockSpec(memory_space=pltpu.VMEM)],
        out_specs=[pl.BlockSpec(memory_space=pltpu.VMEM)] * 2,
        scratch_shapes=[pltpu.SemaphoreType.DMA((N_DEV - 1,)), pltpu.SemaphoreType.DMA((N_DEV - 1,))],
        compiler_params=pltpu.CompilerParams(vmem_limit_bytes=VMEM_LIMIT),
    )(v)


def _sum_gathered(gathered, name):
    rows = gathered.shape[1]

    def body(g_ref, o_ref):
        acc = g_ref[0]
        for d in range(1, N_DEV):
            acc = acc + g_ref[d]
        o_ref[...] = acc

    return pl.pallas_call(
        body, name=name, out_shape=jax.ShapeDtypeStruct((rows, LANES), F32),
        in_specs=[pl.BlockSpec(memory_space=pltpu.VMEM)], out_specs=pl.BlockSpec(memory_space=pltpu.VMEM),
        compiler_params=pltpu.CompilerParams(vmem_limit_bytes=VMEM_LIMIT),
    )(gathered)


def _pack_small(parts):
    flat = jnp.concatenate([p.reshape(-1).astype(F32) for p in parts])
    total = flat.shape[0]
    padded = -(-total // (8 * LANES)) * (8 * LANES)
    flat = jnp.pad(flat, (0, padded - total))
    return flat.reshape(padded // LANES, LANES)


def _unpack_small(packed, shapes, lead=()):
    flat = packed.reshape(lead + (-1,))
    out, off = [], 0
    for shp in shapes:
        size = math.prod(shp)
        out.append(flat[..., off:off + size].reshape(lead + tuple(shp)))
        off += size
    return out


def _prologue(c_rows, convw_rows, ada_w, ada_b_cols, gather):
    depth, _, cols = ada_w.shape
    n_c = len(gather.operands)
    sub = 8

    def body(c_ref, cw_ref, w_ref, b_ref, *rest):
        g_src, (c_all_ref, cw_all_ref, ada_ref), g_dst = rest[:n_c], rest[n_c:n_c + 3], rest[n_c + 3:2 * n_c + 3]
        ada_local, send_sems, recv_sems = rest[2 * n_c + 3:2 * n_c + 6]
        g_sems = rest[2 * n_c + 6:]
        x, y, c, me = _my_position()
        gather.start(g_src, g_dst, *g_sems)

        def to_all(k, src_ref, dst_ref):
            copies = []
            for p in range(1, N_DEV):
                copies.append(pltpu.make_async_remote_copy(
                    src_ref=src_ref, dst_ref=dst_ref.at[me], send_sem=send_sems.at[k, p - 1], recv_sem=recv_sems.at[k, p - 1],
                    device_id=_peer(x, y, c, p), device_id_type=MESH))
            return copies

        first = to_all(0, c_ref, c_all_ref) + to_all(1, cw_ref, cw_all_ref)
        c_all_ref[me] = c_ref[...]
        cw_all_ref[me] = cw_ref[...]
        for cp in first:
            cp.start()
        for cp in first:
            cp.wait()
        cv = c_all_ref[...].reshape(N_DEV * sub, D_MODEL)
        act = (cv * _sigmoid(cv)).astype(BF16)
        for l in range(depth):
            ada_local[l] = _dot(act, w_ref[l].astype(BF16)) + b_ref[l]
        ada_ref[me] = ada_local[:, pl.ds(pl.multiple_of(me * sub, sub), sub), :]
        rows_out = []
        for p in range(1, N_DEV):
            px, py, pc = _peer(x, y, c, p)
            rows = pl.ds(pl.multiple_of((4 * px + 2 * py + pc) * sub, sub), sub)
            rows_out.append(pltpu.make_async_remote_copy(
                src_ref=ada_local.at[:, rows, :], dst_ref=ada_ref.at[me], send_sem=send_sems.at[2, p - 1],
                recv_sem=recv_sems.at[2, p - 1], device_id=(px, py, pc), device_id_type=MESH))
        for cp in rows_out:
            cp.start()
        for cp in rows_out:
            cp.wait()
        gather.forward(g_src, g_dst, *g_sems)
        gather.finish(g_src, g_dst, *g_sems)

    vmem = pl.BlockSpec(memory_space=pltpu.VMEM)
    outs = pl.pallas_call(
        body, name="prologue",
        out_shape=[jax.ShapeDtypeStruct((N_DEV, sub, D_MODEL), F32), jax.ShapeDtypeStruct((N_DEV, sub, LANES), F32),
                   jax.ShapeDtypeStruct((N_DEV, depth, sub, cols), F32)] + gather.out_shape,
        in_specs=[vmem] * 4 + [_ANY] * n_c, out_specs=[vmem] * 3 + [_ANY] * n_c,
        scratch_shapes=[pltpu.VMEM((depth, N_DEV * sub, cols), F32), pltpu.SemaphoreType.DMA((3, N_DEV - 1)),
                        pltpu.SemaphoreType.DMA((3, N_DEV - 1))] + gather.scratch,
        compiler_params=pltpu.CompilerParams(vmem_limit_bytes=VMEM_LIMIT),
    )(c_rows, convw_rows, ada_w, ada_b_cols, *gather.operands)
    return outs[0], outs[1], outs[2], list(outs[3:])


def _ada_backward(c_all, d_ada_cols, d_ada_all):
    nb = c_all.shape[0]
    cols = d_ada_cols.shape[2]
    full = d_ada_all.shape[2]

    def body(c_ref, dc_ref, da_ref, gw_ref, gb_ref):
        cv = c_ref[...]
        act = (cv * _sigmoid(cv)).astype(BF16)
        gw_ref[0] = _dot_tn(act, dc_ref[0].astype(BF16))
        gb_ref[0] = jnp.sum(da_ref[0], axis=0, keepdims=True)

    return pl.pallas_call(
        body, name="ada_backward", grid=(DEPTH,),
        out_shape=[jax.ShapeDtypeStruct((DEPTH, D_MODEL, cols), F32), jax.ShapeDtypeStruct((DEPTH, 1, full), F32)],
        in_specs=[pl.BlockSpec((nb, D_MODEL), lambda l: (0, 0)),
                  pl.BlockSpec((1, nb, cols), lambda l: (l, 0, 0)),
                  pl.BlockSpec((1, nb, full), lambda l: (l, 0, 0))],
        out_specs=[pl.BlockSpec((1, D_MODEL, cols), lambda l: (l, 0, 0)),
                   pl.BlockSpec((1, 1, full), lambda l: (l, 0, 0))],
        compiler_params=_params(),
    )(c_all, d_ada_cols, d_ada_all)


def _rms(xv):
    return lax.rsqrt(jnp.mean(xv * xv, axis=-1, keepdims=True) + EPS)


def _normmod_matmul(x, gnorm, scale1p, shift, w_t, seq, name, swiglu=False, comm=None):
    tokens, n_out = x.shape[0], w_t.shape[0]
    tm = _tile_rows(seq)
    per_seq = seq // tm
    width = n_out // 2 if swiglu else n_out
    n_chunks = width // MXU_N

    def body(x_ref, g_ref, sc_ref, sh_ref, w_ref, h_ref, *o_refs):
        xv = x_ref[...]
        h = (xv * _rms(xv) * g_ref[...]) * sc_ref[0] + sh_ref[0]
        h_ref[...] = h.astype(BF16)
        for ck in range(n_chunks):
            cs = slice(ck * MXU_N, (ck + 1) * MXU_N)
            if swiglu:
                act_ref, silu_ref, dact_ref = o_refs
                g = _dot_nt(h_ref[...], w_ref[cs, :])
                u = _dot_nt(h_ref[...], w_ref[width + ck * MXU_N:width + (ck + 1) * MXU_N, :])
                sig = _sigmoid(g)
                silu = g * sig
                act_ref[:, cs] = (silu * u).astype(BF16)
                silu_ref[:, cs] = silu.astype(BF16)
                dact_ref[:, cs] = (u * (sig + silu * (1.0 - sig))).astype(BF16)
            else:
                o_refs[0][:, cs] = _dot_nt(h_ref[...], w_ref[cs, :]).astype(BF16)

    n_res = 3 if swiglu else 1
    per_batch = pl.BlockSpec((1, 1, D_MODEL), lambda i: (i // per_seq, 0, 0))
    outs, got = _call(
        body, name=name, grid=(tokens // tm,),
        out_shape=[jax.ShapeDtypeStruct((tokens, D_MODEL), BF16)] + [jax.ShapeDtypeStruct((tokens, width), BF16)] * n_res,
        in_specs=[pl.BlockSpec((tm, D_MODEL), lambda i: (i, 0)), _resident((1, D_MODEL)), per_batch, per_batch,
                  _resident(w_t.shape)],
        out_specs=[pl.BlockSpec((tm, D_MODEL), lambda i: (i, 0))] + [pl.BlockSpec((tm, width), lambda i: (i, 0))] * n_res,
        operands=(x, gnorm, scale1p, shift, w_t), parallel=True, comm=comm)
    return (*outs, got)


def _matmul_residual(src, w, x, gate, scale, seq, name, comm=None):
    tokens, k_dim = x.shape[0], w.shape[0]
    tm = _tile_rows(seq)
    per_seq = seq // tm

    def body(s_ref, w_ref, x_ref, gate_ref, xo_ref, f_ref):
        f = _dot(s_ref[...], w_ref[...])
        f_ref[...] = f.astype(BF16)
        xo_ref[...] = x_ref[...] + (scale * gate_ref[0]) * f

    (x_out, f), got = _call(
        body, name=name, grid=(tokens // tm,),
        out_shape=[jax.ShapeDtypeStruct((tokens, D_MODEL), F32), jax.ShapeDtypeStruct((tokens, D_MODEL), BF16)],
        in_specs=[pl.BlockSpec((tm, k_dim), lambda i: (i, 0)), _resident(w.shape),
                  pl.BlockSpec((tm, D_MODEL), lambda i: (i, 0)),
                  pl.BlockSpec((1, 1, D_MODEL), lambda i: (i // per_seq, 0, 0))],
        out_specs=[pl.BlockSpec((tm, D_MODEL), lambda i: (i, 0))] * 2,
        operands=(src, w, x, gate), parallel=True, comm=comm)
    return x_out, f, got


def _loss_tile(xv, target, gn):
    r = _rms(xv)
    xn = xv * r
    err = xn * gn - target
    loss = (0.5 / D_MODEL) * jnp.sum(err * err, axis=0, keepdims=True)
    dyv = err * (1.0 / D_MODEL)
    dg = jnp.sum(dyv * xn, axis=0, keepdims=True)
    dxn = dyv * gn
    dx = r * (dxn - xn * jnp.mean(dxn * xn, axis=-1, keepdims=True))
    return loss, dx, dg


def _ffn_forward(x, gnorm, scale1p, shift, w_gu_t, w_down, gate, scale, seq, name, loss_head=None, comm=None):
    tokens, width = x.shape[0], w_down.shape[0]
    tm = _tile_rows(seq)
    per_seq = seq // tm
    n_chunks = width // MXU_N

    def body(x_ref, g_ref, sc_ref, sh_ref, wgu_ref, wd_ref, gate_ref, *rest):
        if loss_head:
            t_ref, gf_ref, h_ref, act_ref, silu_ref, dact_ref, xo_ref, f_ref, dgf_ref, loss_ref = rest
        else:
            h_ref, act_ref, silu_ref, dact_ref, xo_ref, f_ref = rest
        xv = x_ref[...]
        h = (xv * _rms(xv) * g_ref[...]) * sc_ref[0] + sh_ref[0]
        h_ref[...] = h.astype(BF16)
        for ck in range(n_chunks):
            cs = slice(ck * MXU_N, (ck + 1) * MXU_N)
            g = _dot_nt(h_ref[...], wgu_ref[cs, :])
            u = _dot_nt(h_ref[...], wgu_ref[width + ck * MXU_N:width + (ck + 1) * MXU_N, :])
            sig = _sigmoid(g)
            silu = g * sig
            act_ref[:, cs] = (silu * u).astype(BF16)
            silu_ref[:, cs] = silu.astype(BF16)
            dact_ref[:, cs] = (u * (sig + silu * (1.0 - sig))).astype(BF16)
        f = _dot(act_ref[...], wd_ref[...])
        f_ref[...] = f.astype(BF16)
        x_out = xv + (scale * gate_ref[0]) * f
        if loss_head:
            i = pl.program_id(0)
            loss, dx, dg = _loss_tile(x_out, t_ref[...], gf_ref[...])
            xo_ref[...] = dx

            @pl.when(i == 0)
            def _():
                dgf_ref[...] = dg
                loss_ref[...] = loss

            @pl.when(i != 0)
            def _():
                dgf_ref[...] = dgf_ref[...] + dg
                loss_ref[...] = loss_ref[...] + loss
        else:
            xo_ref[...] = x_out

    row = lambda i: (i, 0)
    per_batch = pl.BlockSpec((1, 1, D_MODEL), lambda i: (i // per_seq, 0, 0))
    tile = lambda cols: pl.BlockSpec((tm, cols), row)
    wide = jax.ShapeDtypeStruct((tokens, width), BF16)
    fixed = pl.BlockSpec((1, D_MODEL), lambda i: (0, 0))
    vec = jax.ShapeDtypeStruct((1, D_MODEL), F32)
    outs, got = _call(
        body, name=name, grid=(tokens // tm,),
        out_shape=[jax.ShapeDtypeStruct((tokens, D_MODEL), BF16), wide, wide, wide,
                   jax.ShapeDtypeStruct((tokens, D_MODEL), F32), jax.ShapeDtypeStruct((tokens, D_MODEL), BF16)]
        + ([vec, vec] if loss_head else []),
        in_specs=[tile(D_MODEL), _resident((1, D_MODEL)), per_batch, per_batch, _resident(w_gu_t.shape),
                  _resident(w_down.shape), per_batch] + ([tile(D_MODEL), _resident((1, D_MODEL))] if loss_head else []),
        out_specs=[tile(D_MODEL), tile(width), tile(width), tile(width), tile(D_MODEL), tile(D_MODEL)]
        + ([fixed, fixed] if loss_head else []),
        operands=(x, gnorm, scale1p, shift, w_gu_t, w_down, gate) + (tuple(loss_head) if loss_head else ()),
        parallel=not loss_head, comm=comm)
    return (*outs, got)


def _residual_backward(dy, gate, f, w, scale, silu, dact, seq, name, comm=None):
    tokens, k_dim = dy.shape[0], w.shape[0]
    batch = tokens // seq
    tm = _tile_rows(seq)
    per_seq = seq // tm
    n_chunks = k_dim // MXU_N

    def body(dy_ref, gate_ref, f_ref, silu_ref, dact_ref, w_ref, df_ref, dgate_ref, dgu_ref):
        i = pl.program_id(0)
        dy_v = dy_ref[...]
        df_ref[...] = ((scale * gate_ref[0]) * dy_v).astype(BF16)
        part = scale * jnp.sum(dy_v * f_ref[...].astype(F32), axis=0, keepdims=True)

        @pl.when(i % per_seq == 0)
        def _():
            dgate_ref[0] = part

        @pl.when(i % per_seq != 0)
        def _():
            dgate_ref[0] = dgate_ref[0] + part

        for ck in range(n_chunks):
            cs = slice(ck * MXU_N, (ck + 1) * MXU_N)
            cu = slice(k_dim + ck * MXU_N, k_dim + (ck + 1) * MXU_N)
            da = _dot_nt(df_ref[...], w_ref[cs, :])
            dgu_ref[:, cs] = (da * dact_ref[:, cs].astype(F32)).astype(BF16)
            dgu_ref[:, cu] = (da * silu_ref[:, cs].astype(F32)).astype(BF16)

    row = lambda i: (i, 0)
    per_batch = pl.BlockSpec((1, 1, D_MODEL), lambda i: (i // per_seq, 0, 0))
    tile = lambda cols: pl.BlockSpec((tm, cols), row)
    outs, got = _call(
        body, name=name, grid=(tokens // tm,),
        out_shape=[jax.ShapeDtypeStruct((tokens, D_MODEL), BF16), jax.ShapeDtypeStruct((batch, 1, D_MODEL), F32),
                   jax.ShapeDtypeStruct((tokens, 2 * k_dim), BF16)],
        in_specs=[tile(D_MODEL), per_batch, tile(D_MODEL), tile(k_dim), tile(k_dim), _resident(w.shape)],
        out_specs=[tile(D_MODEL), per_batch, tile(2 * k_dim)],
        operands=(dy, gate, f, silu, dact, w), comm=comm)
    return (*outs, got)


def _matmul_normmod_backward(dsrc, w_t, x, dy, gnorm, scale1p, seq, name, comm=None):
    tokens, k_dim = dsrc.shape
    batch = tokens // seq
    tm = _tile_rows(seq)
    per_seq = seq // tm

    def body(ds_ref, w_ref, x_ref, dy_ref, g_ref, sc_ref, dx_ref, dsh_ref, dsc_ref, dg_ref):
        i = pl.program_id(0)
        dh = _dot(ds_ref[...], w_ref[...])
        xv = x_ref[...]
        r = _rms(xv)
        xn = xv * r
        gn = g_ref[...]
        dsh = jnp.sum(dh, axis=0, keepdims=True)
        dsc = jnp.sum(dh * (xn * gn), axis=0, keepdims=True)
        dhn = dh * sc_ref[0]
        dg = jnp.sum(dhn * xn, axis=0, keepdims=True)
        dxn = dhn * gn
        dx_ref[...] = dy_ref[...] + r * (dxn - xn * jnp.mean(dxn * xn, axis=-1, keepdims=True))

        @pl.when(i % per_seq == 0)
        def _():
            dsh_ref[0] = dsh
            dsc_ref[0] = dsc

        @pl.when(i % per_seq != 0)
        def _():
            dsh_ref[0] = dsh_ref[0] + dsh
            dsc_ref[0] = dsc_ref[0] + dsc

        @pl.when(i == 0)
        def _():
            dg_ref[...] = dg

        @pl.when(i != 0)
        def _():
            dg_ref[...] = dg_ref[...] + dg

    row = lambda i: (i, 0)
    per_batch = pl.BlockSpec((1, 1, D_MODEL), lambda i: (i // per_seq, 0, 0))
    outs, got = _call(
        body, name=name, grid=(tokens // tm,),
        out_shape=[jax.ShapeDtypeStruct((tokens, D_MODEL), F32), jax.ShapeDtypeStruct((batch, 1, D_MODEL), F32),
                   jax.ShapeDtypeStruct((batch, 1, D_MODEL), F32), jax.ShapeDtypeStruct((1, D_MODEL), F32)],
        in_specs=[pl.BlockSpec((tm, k_dim), row), _resident(w_t.shape), pl.BlockSpec((tm, D_MODEL), row),
                  pl.BlockSpec((tm, D_MODEL), row), _resident((1, D_MODEL)), per_batch],
        out_specs=[pl.BlockSpec((tm, D_MODEL), row), per_batch, per_batch, pl.BlockSpec((1, D_MODEL), lambda i: (0, 0))],
        operands=(dsrc, w_t, x, dy, gnorm, scale1p), comm=comm)
    return (*outs, got)


def _ffn_backward(dy, gate, f, silu, dact, w_down, w_gu_t, x, gnorm, scale1p, scale, seq, name, comm=None):
    tokens, k_dim = dy.shape[0], w_down.shape[0]
    batch = tokens // seq
    tm = min(256, seq)
    per_seq = seq // tm
    n_chunks = k_dim // MXU_N

    def body(dy_ref, gate_ref, f_ref, silu_ref, dact_ref, wd_ref, wgu_ref, x_ref, g_ref, sc_ref,
             df_ref, dgate_ref, dgu_ref, dx_ref, dsh_ref, dsc_ref, dg_ref):
        i = pl.program_id(0)
        dy_v = dy_ref[...]
        df_ref[...] = ((scale * gate_ref[0]) * dy_v).astype(BF16)
        dgate = scale * jnp.sum(dy_v * f_ref[...].astype(F32), axis=0, keepdims=True)
        for ck in range(n_chunks):
            cs = slice(ck * MXU_N, (ck + 1) * MXU_N)
            cu = slice(k_dim + ck * MXU_N, k_dim + (ck + 1) * MXU_N)
            da = _dot_nt(df_ref[...], wd_ref[cs, :])
            dgu_ref[:, cs] = (da * dact_ref[:, cs].astype(F32)).astype(BF16)
            dgu_ref[:, cu] = (da * silu_ref[:, cs].astype(F32)).astype(BF16)
        dh = _dot(dgu_ref[...], wgu_ref[...])
        xv = x_ref[...]
        r = _rms(xv)
        xn = xv * r
        gn = g_ref[...]
        dsh = jnp.sum(dh, axis=0, keepdims=True)
        dsc = jnp.sum(dh * (xn * gn), axis=0, keepdims=True)
        dhn = dh * sc_ref[0]
        dg = jnp.sum(dhn * xn, axis=0, keepdims=True)
        dxn = dhn * gn
        dx_ref[...] = dy_v + r * (dxn - xn * jnp.mean(dxn * xn, axis=-1, keepdims=True))

        @pl.when(i % per_seq == 0)
        def _():
            dgate_ref[0] = dgate
            dsh_ref[0] = dsh
            dsc_ref[0] = dsc

        @pl.when(i % per_seq != 0)
        def _():
            dgate_ref[0] = dgate_ref[0] + dgate
            dsh_ref[0] = dsh_ref[0] + dsh
            dsc_ref[0] = dsc_ref[0] + dsc

        @pl.when(i == 0)
        def _():
            dg_ref[...] = dg

        @pl.when(i != 0)
        def _():
            dg_ref[...] = dg_ref[...] + dg

    row = lambda i: (i, 0)
    per_batch = pl.BlockSpec((1, 1, D_MODEL), lambda i: (i // per_seq, 0, 0))
    tile = lambda width: pl.BlockSpec((tm, width), row)
    vec = jax.ShapeDtypeStruct((batch, 1, D_MODEL), F32)
    outs, got = _call(
        body, name=name, grid=(tokens // tm,),
        out_shape=[jax.ShapeDtypeStruct((tokens, D_MODEL), BF16), vec, jax.ShapeDtypeStruct((tokens, 2 * k_dim), BF16),
                   jax.ShapeDtypeStruct((tokens, D_MODEL), F32), vec, vec, jax.ShapeDtypeStruct((1, D_MODEL), F32)],
        in_specs=[tile(D_MODEL), per_batch, tile(D_MODEL), tile(k_dim), tile(k_dim), _resident(w_down.shape),
                  _resident(w_gu_t.shape), tile(D_MODEL), _resident((1, D_MODEL)), per_batch],
        out_specs=[tile(D_MODEL), per_batch, tile(2 * k_dim), tile(D_MODEL), per_batch, per_batch,
                   pl.BlockSpec((1, D_MODEL), lambda i: (0, 0))],
        operands=(dy, gate, f, silu, dact, w_down, w_gu_t, x, gnorm, scale1p), comm=comm)
    return (*outs, got)


def _weight_grad(a, b, seq, name, comm=None):
    tokens, n_out = a.shape
    tn = MXU_N

    def body(a_ref, b_ref, o_ref):
        o_ref[...] = _dot_tn(a_ref[...], b_ref[...]).astype(BF16)

    (out,), got = _call(
        body, name=name, grid=(n_out // tn,),
        out_shape=[jax.ShapeDtypeStruct((n_out, D_MODEL), BF16)],
        in_specs=[pl.BlockSpec((tokens, tn), lambda j: (0, j)), _resident((tokens, D_MODEL))],
        out_specs=[pl.BlockSpec((tn, D_MODEL), lambda j: (j, 0))],
        operands=(a, b), parallel=True, comm=comm)
    return out, got


def _group_mean(v, bd):
    hi = v.astype(BF16)
    lo = (v - hi.astype(F32)).astype(BF16)
    return _dot(hi, bd) + _dot(lo, bd)


def _sgu_forward(pm_ref, wm_ref, bias_ref, lng_ref, lnb_ref, bd_ref, mixed_scr, n_sub):
    ua = pm_ref[:, 0:D_A].astype(F32)
    va = pm_ref[:, D_A:2 * D_A].astype(F32)
    u_act = _gelu(ua)
    v_act = _gelu(va)
    bd = bd_ref[...]
    vc = v_act - _group_mean(v_act, bd)
    rstd = lax.rsqrt(_group_mean(vc * vc, bd) + EPS)
    vhat = vc * rstd
    vln = vhat * lng_ref[...] + lnb_ref[...]
    left = lax.broadcasted_iota(jnp.int32, (CHUNK, LANES), 1) < HEAD_DIM
    for q in range(n_sub):
        rows = slice(q * CHUNK, (q + 1) * CHUNK)
        for p in range(N_HEADS // 2):
            cols = slice(p * LANES, (p + 1) * LANES)
            vp = vln[rows, cols]
            v_l = jnp.where(left, vp, 0.0).astype(BF16)
            v_r = jnp.where(left, 0.0, vp).astype(BF16)
            mixed_scr[rows, cols] = _dot(wm_ref[2 * p], v_l) + _dot(wm_ref[2 * p + 1], v_r) + bias_ref[:, cols]
    return ua, va, u_act, vhat, rstd, vln


def _halo_specs(tm, tokens, width):
    prev = pl.BlockSpec((HALO, width), lambda i: (jnp.maximum(i * (tm // HALO) - 1, 0), 0))
    nxt = pl.BlockSpec((HALO, width), lambda i: (jnp.minimum((i + 1) * (tm // HALO), tokens // HALO - 1), 0))
    return prev, nxt


def _mixer_forward(x, gnorm, scale1p, shift, w_in_t, gate, w_out, wm, bias_full, lng, lnb, convw, og, bd, seq, name, comm=None):
    tokens = x.shape[0]
    tm = _tile_rows(seq)
    per_seq = seq // tm
    n_sub = tm // CHUNK

    def body(x_ref, xp_ref, g_ref, sc_ref, sh_ref, win_ref, gate_ref, wo_ref, wm_ref, bias_ref, lng_ref, lnb_ref, cw_ref,
             og_ref, bd_ref, h_ref, pm_ref, y_ref, xo_ref, o_ref, mixed_scr):
        i = pl.program_id(0)
        first = (i % per_seq) == 0
        xv = x_ref[...]
        h_ref[...] = ((xv * _rms(xv) * g_ref[...]) * sc_ref[0] + sh_ref[0]).astype(BF16)
        for ck in range(D_PROJ // MXU_N):
            cs = slice(ck * MXU_N, (ck + 1) * MXU_N)
            pm_ref[:, cs] = _dot_nt(h_ref[...], win_ref[cs, :]).astype(BF16)
        xp = xp_ref[...]
        hp = ((xp * _rms(xp) * g_ref[...]) * sc_ref[0] + sh_ref[0]).astype(BF16)
        gates_prev = _dot_nt(hp, win_ref[3 * D_A:5 * D_A, :]).astype(BF16).astype(F32)

        _, _, u_act, _, _, _ = _sgu_forward(pm_ref, wm_ref, bias_ref, lng_ref, lnb_ref, bd_ref, mixed_scr, n_sub)
        ya = u_act * mixed_scr[...]
        y_ref[:, 0:D_A] = (ya * _rms(ya) * og_ref[:, 0:D_A]).astype(BF16)

        bg = pm_ref[:, 2 * D_A:3 * D_A].astype(F32)
        z = pm_ref[:, 3 * D_A:4 * D_A].astype(F32) * pm_ref[:, 4 * D_A:5 * D_A].astype(F32)
        zp = jnp.where(first, 0.0, gates_prev[:, 0:D_A] * gates_prev[:, D_A:2 * D_A])
        zext = jnp.concatenate([zp, z], axis=0)
        z1 = pltpu.roll(zext, 1, 0)[HALO:]
        z2 = pltpu.roll(zext, 2, 0)[HALO:]
        conv = cw_ref[0:1, :] * z2 + cw_ref[1:2, :] * z1 + cw_ref[2:3, :] * z
        yb = bg * conv
        y_ref[:, D_A:2 * D_A] = (yb * _rms(yb) * og_ref[:, D_A:2 * D_A]).astype(BF16)

        f = _dot(y_ref[...], wo_ref[...])
        o_ref[...] = f.astype(BF16)
        xo_ref[...] = xv + gate_ref[0] * f

    prev, _ = _halo_specs(tm, tokens, D_MODEL)
    tile = pl.BlockSpec((tm, D_MODEL), lambda i: (i, 0))
    per_batch = pl.BlockSpec((1, 1, D_MODEL), lambda i: (i // per_seq, 0, 0))
    bf = lambda cols: jax.ShapeDtypeStruct((tokens, cols), BF16)
    outs, got = _call(
        body, name=name, grid=(tokens // tm,),
        out_shape=[bf(D_MODEL), bf(D_PROJ), bf(D_MODEL), jax.ShapeDtypeStruct((tokens, D_MODEL), F32), bf(D_MODEL)],
        in_specs=[tile, prev, _resident((1, D_MODEL)), per_batch, per_batch, _resident(w_in_t.shape), per_batch,
                  _resident(w_out.shape), _resident(wm.shape), _resident(bias_full.shape), _resident(lng.shape),
                  _resident(lnb.shape), _resident(convw.shape), _resident(og.shape), _resident(bd.shape)],
        out_specs=[tile, pl.BlockSpec((tm, D_PROJ), lambda i: (i, 0)), tile, tile, tile],
        scratch_shapes=[pltpu.VMEM((tm, D_A), F32)],
        operands=(x, x, gnorm, scale1p, shift, w_in_t, gate, w_out, wm, bias_full, lng, lnb, convw, og, bd),
        parallel=True, comm=comm)
    return (*outs, got)


def _mixer_backward(proj, dx, gate, o, w_out, x, gnorm, scale1p, w_in_t, wm, bias_full, lng, lnb, convw, og, bd, causal,
                    seq, name, comm=None):
    tokens = proj.shape[0]
    batch = tokens // seq
    tm = _tile_rows(seq)
    per_seq = seq // tm
    n_sub = tm // CHUNK
    ext = tm + 2 * HALO

    def body(pm_ref, pp_ref, pn_ref, dx_ref, dxn_ref, gate_ref, o_ref, wo_ref, x_ref, g_ref, sc_ref, win_ref, wm_ref, bias_ref,
             lng_ref, lnb_ref, cw_ref, og_ref, bd_ref, causal_ref, do_ref, dgate_ref, dp_ref, dxo_ref, dsh_ref, dsc_ref, dgn_ref,
             dog_ref, dcw_ref, dlng_ref, dlnb_ref, dbias_ref, dwm_ref, mixed_scr, dvln_scr, dy_scr):
        i = pl.program_id(0)
        first = (i % per_seq) == 0
        last = (i % per_seq) == per_seq - 1

        dx_v = dx_ref[...]
        do_ref[...] = (gate_ref[0] * dx_v).astype(BF16)
        dgate = jnp.sum(dx_v * o_ref[...].astype(F32), axis=0, keepdims=True)
        dy_scr[...] = _dot_nt(do_ref[...], wo_ref[...])
        dyn_conv = _dot_nt((gate_ref[0] * dxn_ref[...]).astype(BF16), wo_ref[D_A:2 * D_A, :])

        @pl.when(i == 0)
        def _():
            dog_ref[...] = jnp.zeros_like(dog_ref)
            dcw_ref[...] = jnp.zeros_like(dcw_ref)
            dlng_ref[...] = jnp.zeros_like(dlng_ref)
            dlnb_ref[...] = jnp.zeros_like(dlnb_ref)
            dbias_ref[...] = jnp.zeros_like(dbias_ref)
            dwm_ref[...] = jnp.zeros_like(dwm_ref)

        ua, va, u_act, vhat, rstd, vln = _sgu_forward(pm_ref, wm_ref, bias_ref, lng_ref, lnb_ref, bd_ref, mixed_scr, n_sub)
        mixed = mixed_scr[...]
        ya = u_act * mixed
        ra = _rms(ya)
        yhat = ya * ra
        dya_in = dy_scr[:, 0:D_A]
        dog_ref[:, 0:D_A] = dog_ref[:, 0:D_A] + jnp.sum(dya_in * yhat, axis=0, keepdims=True)
        dyh = dya_in * og_ref[:, 0:D_A]
        dya = ra * (dyh - yhat * jnp.mean(dyh * yhat, axis=-1, keepdims=True))
        d_u = dya * mixed
        d_mixed = dya * u_act
        left = lax.broadcasted_iota(jnp.int32, (CHUNK, LANES), 1) < HEAD_DIM
        dbias = jnp.zeros((CHUNK, D_A), F32)
        for q in range(n_sub):
            rows = slice(q * CHUNK, (q + 1) * CHUNK)
            dbias = dbias + d_mixed[rows, :]
            for p in range(N_HEADS // 2):
                cols = slice(p * LANES, (p + 1) * LANES)
                dm = d_mixed[rows, cols]
                dm_l = jnp.where(left, dm, 0.0).astype(BF16)
                dm_r = jnp.where(left, 0.0, dm).astype(BF16)
                vp = vln[rows, cols].astype(BF16)
                dwm_ref[2 * p] = dwm_ref[2 * p] + causal_ref[...] * _dot_nt(dm_l, vp)
                dwm_ref[2 * p + 1] = dwm_ref[2 * p + 1] + causal_ref[...] * _dot_nt(dm_r, vp)
                dvln_scr[rows, cols] = _dot_tn(wm_ref[2 * p], dm_l) + _dot_tn(wm_ref[2 * p + 1], dm_r)
        dbias_ref[...] = dbias_ref[...] + dbias
        dvln = dvln_scr[...]
        dlng_ref[...] = dlng_ref[...] + jnp.sum(dvln * vhat, axis=0, keepdims=True)
        dlnb_ref[...] = dlnb_ref[...] + jnp.sum(dvln, axis=0, keepdims=True)
        dvh = dvln * lng_ref[...]
        bd = bd_ref[...]
        d_v = rstd * (dvh - _group_mean(dvh, bd) - vhat * _group_mean(dvh * vhat, bd))
        dp_ref[:, 0:D_A] = (d_u * _gelu_grad(ua)).astype(BF16)
        dp_ref[:, D_A:2 * D_A] = (d_v * _gelu_grad(va)).astype(BF16)
        dh_a = _dot(dp_ref[:, 0:2 * D_A], win_ref[0:2 * D_A, :])

        def ext_cols(lo):
            cs = slice(lo, lo + D_A)
            return jnp.concatenate([pp_ref[:, cs], pm_ref[:, cs], pn_ref[:, cs]], axis=0).astype(F32)

        bg, cg, xb = ext_cols(2 * D_A), ext_cols(3 * D_A), ext_cols(4 * D_A)
        row = lax.broadcasted_iota(jnp.int32, (ext, D_A), 0)
        z = jnp.where(jnp.logical_and(first, row < HALO), 0.0, cg * xb)
        z1 = pltpu.roll(z, 1, 0)
        z2 = pltpu.roll(z, 2, 0)
        w0, w1, w2 = cw_ref[0:1, :], cw_ref[1:2, :], cw_ref[2:3, :]
        conv = w0 * z2 + w1 * z1 + w2 * z
        yb = bg * conv
        rb = _rms(yb)
        yhb = yb * rb
        dyn = jnp.where(last, 0.0, dyn_conv)
        dyb_in = jnp.concatenate([jnp.zeros((HALO, D_A), F32), dy_scr[:, D_A:2 * D_A], dyn], axis=0)
        dyhb = dyb_in * og_ref[:, D_A:2 * D_A]
        dyb = rb * (dyhb - yhb * jnp.mean(dyhb * yhb, axis=-1, keepdims=True))
        d_conv = dyb * bg
        dz = w2 * d_conv + w1 * pltpu.roll(d_conv, ext - 1, 0) + w0 * pltpu.roll(d_conv, ext - 2, 0)
        main = slice(HALO, HALO + tm)
        dp_ref[:, 2 * D_A:3 * D_A] = (dyb * conv)[main].astype(BF16)
        dp_ref[:, 3 * D_A:4 * D_A] = (dz * xb)[main].astype(BF16)
        dp_ref[:, 4 * D_A:5 * D_A] = (dz * cg)[main].astype(BF16)
        dog_ref[:, D_A:2 * D_A] = dog_ref[:, D_A:2 * D_A] + jnp.sum((dyb_in * yhb)[main], axis=0, keepdims=True)
        dcm = d_conv[main]
        dcw_ref[0:1, :] = dcw_ref[0:1, :] + jnp.sum(dcm * z2[main], axis=0, keepdims=True)
        dcw_ref[1:2, :] = dcw_ref[1:2, :] + jnp.sum(dcm * z1[main], axis=0, keepdims=True)
        dcw_ref[2:3, :] = dcw_ref[2:3, :] + jnp.sum(dcm * z[main], axis=0, keepdims=True)

        dh = dh_a + _dot(dp_ref[:, 2 * D_A:5 * D_A], win_ref[2 * D_A:5 * D_A, :])
        xv = x_ref[...]
        r = _rms(xv)
        xn = xv * r
        gn = g_ref[...]
        dsh = jnp.sum(dh, axis=0, keepdims=True)
        dsc = jnp.sum(dh * (xn * gn), axis=0, keepdims=True)
        dhn = dh * sc_ref[0]
        dgn = jnp.sum(dhn * xn, axis=0, keepdims=True)
        dxn = dhn * gn
        dxo_ref[...] = dx_v + r * (dxn - xn * jnp.mean(dxn * xn, axis=-1, keepdims=True))

        @pl.when(first)
        def _():
            dgate_ref[0] = dgate
            dsh_ref[0] = dsh
            dsc_ref[0] = dsc

        @pl.when(jnp.logical_not(first))
        def _():
            dgate_ref[0] = dgate_ref[0] + dgate
            dsh_ref[0] = dsh_ref[0] + dsh
            dsc_ref[0] = dsc_ref[0] + dsc

        @pl.when(i == 0)
        def _():
            dgn_ref[...] = dgn

        @pl.when(i != 0)
        def _():
            dgn_ref[...] = dgn_ref[...] + dgn

    prev_p, next_p = _halo_specs(tm, tokens, D_PROJ)
    _, next_d = _halo_specs(tm, tokens, D_MODEL)
    fixed2 = lambda shape: pl.BlockSpec(shape, lambda i: (0, 0))
    tile = pl.BlockSpec((tm, D_MODEL), lambda i: (i, 0))
    per_batch = pl.BlockSpec((1, 1, D_MODEL), lambda i: (i // per_seq, 0, 0))
    vec = jax.ShapeDtypeStruct((batch, 1, D_MODEL), F32)
    outs, got = _call(
        body, name=name, grid=(tokens // tm,),
        out_shape=[jax.ShapeDtypeStruct((tokens, D_MODEL), BF16), vec, jax.ShapeDtypeStruct((tokens, D_PROJ), BF16),
                   jax.ShapeDtypeStruct((tokens, D_MODEL), F32), vec, vec, jax.ShapeDtypeStruct((1, D_MODEL), F32),
                   jax.ShapeDtypeStruct((1, D_MODEL), F32), jax.ShapeDtypeStruct((8, D_A), F32),
                   jax.ShapeDtypeStruct((1, D_A), F32), jax.ShapeDtypeStruct((1, D_A), F32),
                   jax.ShapeDtypeStruct((CHUNK, D_A), F32), jax.ShapeDtypeStruct((N_HEADS, CHUNK, CHUNK), F32)],
        in_specs=[pl.BlockSpec((tm, D_PROJ), lambda i: (i, 0)), prev_p, next_p, tile, next_d, per_batch, tile,
                  _resident(w_out.shape), tile, _resident((1, D_MODEL)), per_batch, _resident(w_in_t.shape),
                  _resident(wm.shape), _resident(bias_full.shape), _resident(lng.shape), _resident(lnb.shape),
                  _resident(convw.shape), _resident(og.shape), _resident(bd.shape), _resident(causal.shape)],
        out_specs=[tile, per_batch, pl.BlockSpec((tm, D_PROJ), lambda i: (i, 0)), tile, per_batch, per_batch,
                   fixed2((1, D_MODEL)), fixed2((1, D_MODEL)), fixed2((8, D_A)), fixed2((1, D_A)), fixed2((1, D_A)),
                   fixed2((CHUNK, D_A)), pl.BlockSpec((N_HEADS, CHUNK, CHUNK), lambda i: (0, 0, 0))],
        scratch_shapes=[pltpu.VMEM((tm, D_A), F32), pltpu.VMEM((tm, D_A), F32), pltpu.VMEM((tm, D_MODEL), F32)],
        operands=(proj, proj, proj, dx, dx, gate, o, w_out, x, gnorm, scale1p, w_in_t, wm, bias_full, lng, lnb, convw, og, bd,
                  causal), comm=comm)
    return (*outs, got)


def _adamw_update(wv, gv, mv, vv):
    nm = ADAM_B1 * mv + (1.0 - ADAM_B1) * gv
    nv = ADAM_B2 * vv + (1.0 - ADAM_B2) * (gv * gv)
    m_hat = nm / (1.0 - ADAM_B1 ** ADAM_STEP)
    v_hat = nv / (1.0 - ADAM_B2 ** ADAM_STEP)
    return -ADAM_LR * (m_hat / (jnp.sqrt(v_hat) + ADAM_EPS) + ADAM_WD * wv), nm, nv


def _adamw_rows(recv, w, m, v, name):
    depth, rows, cols = w.shape
    tr = rows // 2
    last = rows // tr - 1

    def body(*refs):
        r_refs, (w_ref, m_ref, v_ref, g_ref, d_ref, nm_ref, nv_ref) = refs[:depth], refs[depth:]
        for l in range(depth):
            @pl.when(pl.program_id(0) == l)
            def _(r_ref=r_refs[l]):
                acc = r_ref[0].astype(F32)
                for d in range(1, N_DEV):
                    acc = acc + r_ref[d].astype(F32)
                g_ref[0] = acc
                d_ref[0], nm_ref[0], nv_ref[0] = _adamw_update(w_ref[0], acc, m_ref[0], v_ref[0])

    def slots(l):
        return pl.BlockSpec((N_DEV, tr, cols), lambda ll, i: (0, jnp.where(ll == l, i, jnp.where(ll < l, 0, last)), 0))

    spec = pl.BlockSpec((1, tr, cols), lambda ll, i: (ll, i, 0))
    return pl.pallas_call(
        body, name=name, grid=(depth, rows // tr),
        out_shape=[jax.ShapeDtypeStruct((depth, rows, cols), F32)] * 4,
        in_specs=[slots(l) for l in range(depth)] + [spec] * 3, out_specs=[spec] * 4,
        compiler_params=_params(2),
    )(*recv, w, m, v)


def _adamw(w, g, m, v, name):
    rows, cols = w.shape
    tr = max(t for t in range(8, 513, 8) if rows % t == 0)

    def body(w_ref, g_ref, m_ref, v_ref, d_ref, nm_ref, nv_ref):
        d_ref[...], nm_ref[...], nv_ref[...] = _adamw_update(w_ref[...], g_ref[...], m_ref[...], v_ref[...])

    spec = pl.BlockSpec((tr, cols), lambda i: (i, 0))
    return pl.pallas_call(
        body, name=name, grid=(rows // tr,),
        out_shape=[jax.ShapeDtypeStruct((rows, cols), F32)] * 3,
        in_specs=[spec] * 4, out_specs=[spec] * 3,
        compiler_params=_params(parallel=True),
    )(w, g, m, v)


def _adamw_many(ws, gs, ms, vs, name):
    n = len(ws)
    two_d = lambda a: a.reshape(-1, a.shape[-1])

    def body(*refs):
        w_refs, g_refs, m_refs, v_refs = refs[:n], refs[n:2 * n], refs[2 * n:3 * n], refs[3 * n:4 * n]
        d_refs, nm_refs, nv_refs = refs[4 * n:5 * n], refs[5 * n:6 * n], refs[6 * n:]
        for k in range(n):
            d_refs[k][...], nm_refs[k][...], nv_refs[k][...] = _adamw_update(
                w_refs[k][...], g_refs[k][...], m_refs[k][...], v_refs[k][...])

    flat = [two_d(a) for a in ws]
    outs = pl.pallas_call(
        body, name=name, out_shape=[jax.ShapeDtypeStruct(a.shape, F32) for a in flat] * 3,
        in_specs=[pl.BlockSpec(memory_space=pltpu.VMEM)] * (4 * n),
        out_specs=[pl.BlockSpec(memory_space=pltpu.VMEM)] * (3 * n),
        compiler_params=pltpu.CompilerParams(vmem_limit_bytes=VMEM_LIMIT),
    )(*flat, *[two_d(a) for a in gs], *[two_d(a) for a in ms], *[two_d(a) for a in vs])
    shaped = [o.reshape(ws[k % n].shape) for k, o in enumerate(outs)]
    return shaped[:n], shaped[n:2 * n], shaped[2 * n:]


def _adamw_nd(w, g, m, v, name):
    shape = w.shape
    two_d = (-1, shape[-1])
    d, nm, nv = _adamw(w.reshape(two_d), g.reshape(two_d), m.reshape(two_d), v.reshape(two_d), name)
    return d.reshape(shape), nm.reshape(shape), nv.reshape(shape)


def kernel(x, c, ada_w, ada_b, norm_ffn1_g, ffn1_w_gu, ffn1_w_down, norm_mix_g, mix_w_in, sgu_ln_g, sgu_ln_b, sgu_w_s, sgu_b, conv_w, out_norm_g, mix_w_out, norm_ffn2_g, ffn2_w_gu, ffn2_w_down, final_norm_g, loss_target, m_ada_w, m_ada_b, m_norm_ffn1_g, m_ffn1_w_gu, m_ffn1_w_down, m_norm_mix_g, m_mix_w_in, m_sgu_ln_g, m_sgu_ln_b, m_sgu_w_s, m_sgu_b, m_conv_w, m_out_norm_g, m_mix_w_out, m_norm_ffn2_g, m_ffn2_w_gu, m_ffn2_w_down, m_final_norm_g, v_ada_w, v_ada_b, v_norm_ffn1_g, v_ffn1_w_gu, v_ffn1_w_down, v_norm_mix_g, v_mix_w_in, v_sgu_ln_g, v_sgu_ln_b, v_sgu_w_s, v_sgu_b, v_conv_w, v_out_norm_g, v_mix_w_out, v_norm_ffn2_g, v_ffn2_w_gu, v_ffn2_w_down, v_final_norm_g):
    batch, seq, _ = x.shape
    tokens = batch * seq
    me = 4 * lax.axis_index("x") + 2 * lax.axis_index("y") + lax.axis_index("c")
    weights = dict(ada_w=ada_w, ada_b=ada_b, norm_ffn1_g=norm_ffn1_g, ffn1_w_gu=ffn1_w_gu, ffn1_w_down=ffn1_w_down,
                   norm_mix_g=norm_mix_g, mix_w_in=mix_w_in, sgu_ln_g=sgu_ln_g, sgu_ln_b=sgu_ln_b, sgu_w_s=sgu_w_s,
                   sgu_b=sgu_b, conv_w=conv_w, out_norm_g=out_norm_g, mix_w_out=mix_w_out, norm_ffn2_g=norm_ffn2_g,
                   ffn2_w_gu=ffn2_w_gu, ffn2_w_down=ffn2_w_down, final_norm_g=final_norm_g)
    mom1 = dict(ada_w=m_ada_w, ada_b=m_ada_b, norm_ffn1_g=m_norm_ffn1_g, ffn1_w_gu=m_ffn1_w_gu,
                ffn1_w_down=m_ffn1_w_down, norm_mix_g=m_norm_mix_g, mix_w_in=m_mix_w_in, sgu_ln_g=m_sgu_ln_g,
                sgu_ln_b=m_sgu_ln_b, sgu_w_s=m_sgu_w_s, sgu_b=m_sgu_b, conv_w=m_conv_w, out_norm_g=m_out_norm_g,
                mix_w_out=m_mix_w_out, norm_ffn2_g=m_norm_ffn2_g, ffn2_w_gu=m_ffn2_w_gu, ffn2_w_down=m_ffn2_w_down,
                final_norm_g=m_final_norm_g)
    mom2 = dict(ada_w=v_ada_w, ada_b=v_ada_b, norm_ffn1_g=v_norm_ffn1_g, ffn1_w_gu=v_ffn1_w_gu,
                ffn1_w_down=v_ffn1_w_down, norm_mix_g=v_norm_mix_g, mix_w_in=v_mix_w_in, sgu_ln_g=v_sgu_ln_g,
                sgu_ln_b=v_sgu_ln_b, sgu_w_s=v_sgu_w_s, sgu_b=v_sgu_b, conv_w=v_conv_w, out_norm_g=v_out_norm_g,
                mix_w_out=v_mix_w_out, norm_ffn2_g=v_norm_ffn2_g, ffn2_w_gu=v_ffn2_w_gu, ffn2_w_down=v_ffn2_w_down,
                final_norm_g=v_final_norm_g)

    big = ("ffn1_w_gu", "ffn1_w_down", "mix_w_in", "mix_w_out", "ffn2_w_gu", "ffn2_w_down")
    transposed = ("ffn1_w_gu", "mix_w_in", "ffn2_w_gu")
    as_rows = lambda nm, a: jnp.swapaxes(a, 1, 2) if nm in transposed else a
    shard = {(l, nm): as_rows(nm, weights[nm])[l].astype(BF16) for l in range(DEPTH) for nm in big}
    full_w = {}

    def gather_of(keys):
        return keys, _GatherRows([shard[k] for k in keys])

    def landed(plan, got):
        full_w.update(zip(plan[0], got))

    ada_cols = ada_w.shape[2]
    ada_b_cols = lax.dynamic_slice_in_dim(ada_b, me * ada_cols, ada_cols, axis=1).reshape(DEPTH, 1, ada_cols)
    plan = gather_of([(0, "ffn1_w_gu")])
    c_dev, convw_dev, ada_recv, got = _prologue(
        jnp.pad(c, ((0, 8 - batch), (0, 0))), jnp.pad(conv_w.reshape(-1), (0, 8 * LANES - conv_w.size)).reshape(8, LANES),
        ada_w, ada_b_cols, plan[1])
    landed(plan, got)
    c_all = c_dev[:, :batch].reshape(N_DEV * batch, D_MODEL)
    convw_all = convw_dev.reshape(N_DEV, -1)[:, :conv_w.size].reshape((N_DEV,) + conv_w.shape)
    convw_full = jnp.transpose(convw_all, (1, 2, 0, 3)).reshape(DEPTH, 3, D_A)
    ada_mine = jnp.transpose(ada_recv[:, :, :batch, :], (1, 2, 0, 3)).reshape(DEPTH, batch, N_MOD * D_MODEL)
    mod = ada_mine.reshape(DEPTH, batch, N_MOD, 1, D_MODEL)

    causal = jnp.tril(jnp.ones((CHUNK, CHUNK), F32))
    bd = jnp.kron(jnp.eye(N_HEADS, dtype=F32), jnp.full((HEAD_DIM, HEAD_DIM), 1.0 / HEAD_DIM, F32)).astype(BF16)
    row_vec = lambda a: a.reshape(1, -1)

    hosted_gathers = {
        (0, "ffn1"): [(0, "ffn1_w_down"), (0, "mix_w_in"), (0, "mix_w_out")],
        (0, "ffn_down1"): [(0, "ffn2_w_gu")],
        (0, "mix_in"): [(0, "ffn2_w_down")],
        (0, "ffn2"): [(1, "ffn1_w_gu"), (1, "ffn1_w_down"), (1, "mix_w_in"), (1, "mix_w_out")],
        (1, "ffn1"): [(1, "ffn2_w_gu"), (1, "ffn2_w_down")],
    }

    def hosting(l, site):
        keys = hosted_gathers.get((l, site))
        return gather_of(keys) if keys else (None, None)

    xs = x.reshape(tokens, D_MODEL)
    saved = []
    for l in range(DEPTH):
        sh1, sc1, g1, sh2, sc2, g2, sh3, sc3, g3 = [mod[l, :, k] for k in range(N_MOD)]
        mixer_consts = dict(
            wm=(sgu_w_s[l] * causal[None]).astype(BF16),
            bias_full=jnp.repeat(sgu_b[l].T, HEAD_DIM, axis=1),
            lng=row_vec(jnp.tile(sgu_ln_g[l], N_HEADS)), lnb=row_vec(jnp.tile(sgu_ln_b[l], N_HEADS)),
            convw=jnp.pad(convw_full[l], ((0, 5), (0, 0))), og=row_vec(out_norm_g[l]), bd=bd)
        x0 = xs
        plan = hosting(l, "ffn1")
        if l == 0:
            h1, a1, s1, w1, got = _normmod_matmul(x0, row_vec(norm_ffn1_g[l]), 1.0 + sc1, sh1, full_w[l, "ffn1_w_gu"], seq, "ffn_up", True, plan[1])
            landed(plan, got)
            plan = hosting(l, "ffn_down1")
            x1, f1, got = _matmul_residual(a1, full_w[l, "ffn1_w_down"], x0, g1, 0.5, seq, "ffn_down", plan[1])
        else:
            h1, a1, s1, w1, x1, f1, got = _ffn_forward(
                x0, row_vec(norm_ffn1_g[l]), 1.0 + sc1, sh1, full_w[l, "ffn1_w_gu"], full_w[l, "ffn1_w_down"], g1, 0.5, seq, "ffn_fwd",
                comm=plan[1])
        if got:
            landed(plan, got)
        plan = hosting(l, "mix_in")
        h2, proj, ymix, x2, o2, got = _mixer_forward(
            x1, row_vec(norm_mix_g[l]), 1.0 + sc2, sh2, full_w[l, "mix_w_in"], g2, full_w[l, "mix_w_out"], seq=seq,
            name="mixer_forward", comm=plan[1], **mixer_consts)
        if got:
            landed(plan, got)
        plan = hosting(l, "ffn2")
        if l + 1 < DEPTH:
            h3, a3, s3, w3, x3, f3, got = _ffn_forward(
                x2, row_vec(norm_ffn2_g[l]), 1.0 + sc3, sh3, full_w[l, "ffn2_w_gu"], full_w[l, "ffn2_w_down"], g3, 0.5, seq, "ffn_fwd",
                comm=plan[1])
        else:
            head = (loss_target.reshape(tokens, D_MODEL), row_vec(final_norm_g))
            h3, a3, s3, w3, x3, f3, d_final_g, loss_cols, got = _ffn_forward(
                x2, row_vec(norm_ffn2_g[l]), 1.0 + sc3, sh3, full_w[l, "ffn2_w_gu"], full_w[l, "ffn2_w_down"], g3, 0.5, seq, "ffn_fwd_loss",
                loss_head=head, comm=plan[1])
        if got:
            landed(plan, got)
        saved.append(dict(x0=x0, x1=x1, x2=x2, h1=h1, h2=h2, h3=h3, a1=a1, s1=s1, w1=w1, a3=a3, s3=s3, w3=w3, f1=f1, f3=f3, o2=o2, proj=proj,
                          ymix=ymix, mixer_consts=mixer_consts, sc=(1.0 + sc1, 1.0 + sc2, 1.0 + sc3), gates=(g1, g2, g3)))
        xs = x3

    dx = xs

    recv = {}
    small_grads = [None] * DEPTH
    d_mod = [None] * DEPTH

    mix_names = ("out_norm_g", "sgu_ln_g", "sgu_ln_b", "sgu_w_s", "sgu_b", "conv_w")
    late_names = ("norm_ffn1_g", "norm_mix_g", "norm_ffn2_g")

    def mix_parts(l):
        return [small_grads[l][nm] for nm in mix_names]

    def late_parts(l):
        return [small_grads[l][nm] for nm in late_names] + [d_mod[l]]

    pending = []

    def scatter_later(l, nm, grad):
        pending.append(((l, nm), _ScatterRows([grad])))

    def host():
        keys, parts = [k for k, _ in pending], [p for _, p in pending]
        pending.clear()
        return keys, (_Exchanges(parts) if parts else None)

    def hosted(keys, got):
        if got:
            recv.update(zip(keys, got))

    for l in reversed(range(DEPTH)):
        sv = saved[l]
        mc = sv["mixer_consts"]
        if l + 1 < DEPTH:
            pending.append((("late", l + 1), _GatherRows([_pack_small(late_parts(l + 1))])))
        keys, comm = host()
        df3, dg3, dgu3, dx2, dsh3, dsc3, dn3, got = _ffn_backward(
            dx, sv["gates"][2], sv["f3"], sv["s3"], sv["w3"], full_w[l, "ffn2_w_down"], full_w[l, "ffn2_w_gu"], sv["x2"],
            row_vec(norm_ffn2_g[l]), sv["sc"][2], 0.5, seq, "ffn_bwd", comm)
        hosted(keys, got)
        gw_down2, _ = _weight_grad(sv["a3"], df3, seq, "grad_w_down")
        scatter_later(l, "ffn2_w_down", gw_down2)
        keys, comm = host()
        gw_gu2, got = _weight_grad(dgu3, sv["h3"], seq, "grad_w_gu", comm)
        hosted(keys, got)
        scatter_later(l, "ffn2_w_gu", gw_gu2)
        keys, comm = host()
        do2, dg2, dproj, dx1, dsh2, dsc2, dn2, d_og, d_cw, d_lng, d_lnb, d_bias, d_wm, got = _mixer_backward(
            sv["proj"], dx2, sv["gates"][1], sv["o2"], full_w[l, "mix_w_out"], sv["x1"], row_vec(norm_mix_g[l]), sv["sc"][1],
            full_w[l, "mix_w_in"], causal=causal, seq=seq, name="mixer_backward", comm=comm, **mc)
        hosted(keys, got)
        small_grads[l] = dict(
            out_norm_g=d_og, sgu_ln_g=d_lng.reshape(N_HEADS, HEAD_DIM).sum(0), sgu_ln_b=d_lnb.reshape(N_HEADS, HEAD_DIM).sum(0),
            sgu_w_s=d_wm, sgu_b=d_bias.reshape(CHUNK, N_HEADS, HEAD_DIM).sum(-1).T, conv_w=d_cw[0:3])
        pending.append((("mix", l), _GatherRows([_pack_small(mix_parts(l))])))
        keys, comm = host()
        gw_out, got = _weight_grad(sv["ymix"], do2, seq, "grad_w_out", comm)
        hosted(keys, got)
        scatter_later(l, "mix_w_out", gw_out)
        keys, comm = host()
        gw_in, got = _weight_grad(dproj, sv["h2"], seq, "grad_w_in", comm)
        hosted(keys, got)
        scatter_later(l, "mix_w_in", gw_in)
        keys, comm = host()
        if l > 0:
            df1, dg1, dgu1, dx0, dsh1, dsc1, dn1, got = _ffn_backward(
                dx1, sv["gates"][0], sv["f1"], sv["s1"], sv["w1"], full_w[l, "ffn1_w_down"], full_w[l, "ffn1_w_gu"], sv["x0"],
                row_vec(norm_ffn1_g[l]), sv["sc"][0], 0.5, seq, "ffn_bwd", comm)
        else:
            df1, dg1, dgu1, got = _residual_backward(dx1, sv["gates"][0], sv["f1"], full_w[l, "ffn1_w_down"], 0.5, sv["s1"], sv["w1"], seq, "ffn_down_bwd", comm)
        hosted(keys, got)
        gw_down1, _ = _weight_grad(sv["a1"], df1, seq, "grad_w_down")
        scatter_later(l, "ffn1_w_down", gw_down1)
        keys, comm = host()
        gw_gu1, got = _weight_grad(dgu1, sv["h1"], seq, "grad_w_gu", comm)
        hosted(keys, got)
        scatter_later(l, "ffn1_w_gu", gw_gu1)
        if l == 0:
            keys, comm = host()
            dx0, dsh1, dsc1, dn1, got = _matmul_normmod_backward(dgu1, full_w[l, "ffn1_w_gu"], sv["x0"], dx1, row_vec(norm_ffn1_g[l]), sv["sc"][0], seq, "ffn_up_bwd", comm)
            hosted(keys, got)
        dx = dx0
        small_grads[l].update(norm_ffn1_g=dn1, norm_mix_g=dn2, norm_ffn2_g=dn3)
        d_mod[l] = jnp.concatenate([dsh1, dsc1, dg1, dsh2, dsc2, dg2, dsh3, dsc3, dg3], axis=1)
    grad_x = dx.reshape(batch, seq, D_MODEL)

    grad_big, delta, new_m, new_v = {}, {}, {}, {}
    for nm in big:
        results = _adamw_rows([recv[l, nm] for l in range(DEPTH)], as_rows(nm, weights[nm]), as_rows(nm, mom1[nm]),
                              as_rows(nm, mom2[nm]), "adamw_" + nm)
        grad_big[nm], delta[nm], new_m[nm], new_v[nm] = [as_rows(nm, r) for r in results]

    last_parts = late_parts(0) + [d_final_g, loss_cols]
    last_shapes = [p.shape for p in last_parts]
    packed_all, packed_sum = _all_gather_small(_pack_small(last_parts), "reduce_small")
    late_sum = {0: _unpack_small(packed_sum, last_shapes)}
    d_mod_dev = {0: _unpack_small(packed_all, last_shapes, lead=(N_DEV,))[len(late_names)]}
    mix_sum = {}
    for l in range(DEPTH):
        gathered = recv["mix", l].reshape(N_DEV, -1, LANES)
        mix_sum[l] = _unpack_small(_sum_gathered(gathered, "sum_mix"), [p.shape for p in mix_parts(l)])
        if l > 0:
            shapes_l = [p.shape for p in late_parts(l)]
            gathered = recv["late", l].reshape(N_DEV, -1, LANES)
            late_sum[l] = _unpack_small(_sum_gathered(gathered, "sum_late"), shapes_l)
            d_mod_dev[l] = _unpack_small(gathered, shapes_l, lead=(N_DEV,))[len(late_names)]
    grad_small = {}
    for group, names in ((mix_sum, mix_names), (late_sum, late_names)):
        for k, nm in enumerate(names):
            grad_small[nm] = jnp.stack([group[l][k] for l in range(DEPTH)]).reshape(
                (DEPTH, 3, D_A) if nm == "conv_w" else weights[nm].shape)
    grad_small["conv_w"] = lax.dynamic_slice_in_dim(grad_small["conv_w"], me * conv_w.shape[2], conv_w.shape[2], axis=2)
    grad_small["final_norm_g"] = late_sum[0][len(late_names) + 1].reshape(final_norm_g.shape)
    loss = jnp.sum(late_sum[0][len(late_names) + 2])
    d_ada_all = jnp.stack([d_mod_dev[l] for l in range(DEPTH)]).reshape(DEPTH, N_DEV * batch, N_MOD * D_MODEL)
    d_ada_cols = lax.dynamic_slice_in_dim(d_ada_all, me * ada_cols, ada_cols, axis=2)
    g_ada_w, g_ada_b = _ada_backward(c_all, d_ada_cols, d_ada_all)

    grads = dict(grad_big)
    grads.update(grad_small)
    grads["ada_w"] = g_ada_w
    grads["ada_b"] = g_ada_b.reshape(ada_b.shape)

    names = ("ada_w", "ada_b", "norm_ffn1_g", "ffn1_w_gu", "ffn1_w_down", "norm_mix_g", "mix_w_in", "sgu_ln_g",
             "sgu_ln_b", "sgu_w_s", "sgu_b", "conv_w", "out_norm_g", "mix_w_out", "norm_ffn2_g", "ffn2_w_gu",
             "ffn2_w_down", "final_norm_g")
    delta["ada_w"], new_m["ada_w"], new_v["ada_w"] = _adamw_nd(ada_w, grads["ada_w"], m_ada_w, v_ada_w, "adamw_ada_w")
    rest = [nm for nm in names if nm not in big and nm != "ada_w"]
    pick = lambda src: [src[nm] for nm in rest]
    for nm, d_k, m_k, v_k in zip(rest, *_adamw_many(pick(weights), pick(grads), pick(mom1), pick(mom2), "adamw_small")):
        delta[nm], new_m[nm], new_v[nm] = d_k, m_k, v_k

    return (loss, grad_x, *[grads[nm] for nm in names], *[delta[nm] for nm in names],
            *[new_m[nm] for nm in names], *[new_v[nm] for nm in names])
```

```python
import math

import jax
import jax.numpy as jnp
from jax import lax
from jax.experimental import pallas as pl
from jax.experimental.pallas import tpu as pltpu

F32 = jnp.float32
BF16 = jnp.bfloat16

D_MODEL = 1024
D_FF = 2816
D_A = 512
D_PROJ = 2560
N_HEADS = 8
HEAD_DIM = 64
CHUNK = 128
N_MOD = 9
DEPTH = 2
EPS = 1e-6
N_DEV = 8
LANES = 128
MXU_N = 256
HALO = 16
VMEM_LIMIT = 56 * 1024 * 1024
FORWARD_STEPS = 4

ADAM_LR = 0.001
ADAM_B1 = 0.9
ADAM_B2 = 0.999
ADAM_EPS = 1e-08
ADAM_WD = 0.01
ADAM_STEP = 10

MESH = pl.DeviceIdType.MESH


def _dot(a, b):
    return jnp.dot(a, b, preferred_element_type=F32)


def _dot_nt(a, b):
    return lax.dot_general(a, b, (((1,), (1,)), ((), ())), preferred_element_type=F32)


def _dot_tn(a, b):
    return lax.dot_general(a, b, (((0,), (0,)), ((), ())), preferred_element_type=F32)


def _sigmoid(x):
    return 0.5 * jnp.tanh(0.5 * x) + 0.5


def _gelu(x):
    return 0.5 * x * (1.0 + lax.erf(x * (1.0 / math.sqrt(2.0))))


def _gelu_grad(x):
    cdf = 0.5 * (1.0 + lax.erf(x * (1.0 / math.sqrt(2.0))))
    return cdf + x * jnp.exp(-0.5 * x * x) * (1.0 / math.sqrt(2.0 * math.pi))


def _params(n_axes=1, parallel=False):
    sem = ("parallel" if parallel else "arbitrary",) * n_axes
    return pltpu.CompilerParams(dimension_semantics=sem, vmem_limit_bytes=VMEM_LIMIT)


def _resident(shape):
    nd = len(shape)
    return pl.BlockSpec(shape, lambda *_: (0,) * nd, pipeline_mode=pl.Buffered(1))


def _tile_rows(seq):
    return min(512, seq)


def _my_position():
    x, y, c = lax.axis_index("x"), lax.axis_index("y"), lax.axis_index("c")
    return x, y, c, 4 * x + 2 * y + c


def _peer(x, y, c, p):
    return (x ^ ((p >> 2) & 1), y ^ ((p >> 1) & 1), c ^ (p & 1))


class _GatherRows:
    def __init__(self, shards):
        self.operands = list(shards)
        n = len(shards)
        self.out_shape = [jax.ShapeDtypeStruct((N_DEV * s.shape[0], s.shape[1]), s.dtype) for s in shards]
        self.scratch = [pltpu.SemaphoreType.DMA((n, N_DEV - 1)), pltpu.SemaphoreType.DMA((n, N_DEV - 1)),
                        pltpu.SemaphoreType.DMA((n,))]

    def _plan(self, src, dst, send, recv, loc):
        x, y, c, _ = _my_position()
        me, sib = (x, y, c), (x, y, 1 - c)
        chips = [(1 - x, y), (x, 1 - y), (1 - x, 1 - y)]
        plans = []
        for k, shard in enumerate(self.operands):
            rows = shard.shape[0]

            def blk(pos, k=k, rows=rows):
                return dst[k].at[pl.ds((4 * pos[0] + 2 * pos[1] + pos[2]) * rows, rows), :]

            def rc(s, block, to, source=None, k=k, blk=blk):
                return pltpu.make_async_remote_copy(
                    src_ref=blk(block) if source is None else source, dst_ref=blk(block),
                    send_sem=send.at[k, s], recv_sem=recv.at[k, s], device_id=to, device_id_type=MESH)

            plans.append(dict(
                local=pltpu.make_async_copy(src[k], blk(me), loc.at[k]),
                first=[rc(0, me, sib, src[k])] + [rc(1 + j, me, (*chip, c), src[k]) for j, chip in enumerate(chips)],
                landed=[rc(1 + j, (*chip, c), me) for j, chip in enumerate(chips)],
                passed=[rc(4 + j, (*chip, c), sib) for j, chip in enumerate(chips)],
                from_sib=[rc(0, sib, me)] + [rc(4 + j, (*chip, 1 - c), me) for j, chip in enumerate(chips)]))
        return plans

    def start(self, src, dst, send, recv, loc):
        for plan in self._plan(src, dst, send, recv, loc):
            plan["local"].start()
            for cp in plan["first"]:
                cp.start()

    def forward(self, src, dst, send, recv, loc):
        for plan in self._plan(src, dst, send, recv, loc):
            for landed, passed in zip(plan["landed"], plan["passed"]):
                landed.wait_recv()
                passed.start()

    def finish(self, src, dst, send, recv, loc):
        for plan in self._plan(src, dst, send, recv, loc):
            for cp in plan["from_sib"]:
                cp.wait_recv()
            for cp in plan["first"] + plan["passed"]:
                cp.wait_send()
            plan["local"].wait()


class _ScatterRows:
    def __init__(self, grads):
        self.operands = list(grads)
        n = len(grads)
        self.out_shape = [jax.ShapeDtypeStruct((N_DEV, g.shape[0] // N_DEV, g.shape[1]), g.dtype) for g in grads]
        self.scratch = [pltpu.SemaphoreType.DMA((n, N_DEV - 1)), pltpu.SemaphoreType.DMA((n, N_DEV - 1)),
                        pltpu.SemaphoreType.DMA((n,))]

    def _plan(self, src, dst, send, recv, loc):
        x, y, c, me = _my_position()
        copies = []
        for k, grad in enumerate(self.operands):
            rows = grad.shape[0] // N_DEV
            copies.append(pltpu.make_async_copy(src[k].at[pl.ds(me * rows, rows), :], dst[k].at[me], loc.at[k]))
            for p in range(1, N_DEV):
                px, py, pc = _peer(x, y, c, p)
                copies.append(pltpu.make_async_remote_copy(
                    src_ref=src[k].at[pl.ds((4 * px + 2 * py + pc) * rows, rows), :], dst_ref=dst[k].at[me],
                    send_sem=send.at[k, p - 1], recv_sem=recv.at[k, p - 1], device_id=(px, py, pc), device_id_type=MESH))
        return copies

    def start(self, src, dst, send, recv, loc):
        for cp in self._plan(src, dst, send, recv, loc):
            cp.start()

    def forward(self, src, dst, send, recv, loc):
        pass

    def finish(self, src, dst, send, recv, loc):
        for cp in self._plan(src, dst, send, recv, loc):
            cp.wait()


class _Exchanges:
    def __init__(self, parts):
        self.parts = list(parts)
        self.operands = [op for part in self.parts for op in part.operands]
        self.out_shape = [shp for part in self.parts for shp in part.out_shape]
        self.scratch = [scr for part in self.parts for scr in part.scratch]

    def _each(self, src, dst, sems):
        at, sem_at = 0, 0
        for part in self.parts:
            n, n_sem = len(part.operands), len(part.scratch)
            yield part, src[at:at + n], dst[at:at + n], sems[sem_at:sem_at + n_sem]
            at, sem_at = at + n, sem_at + n_sem

    def start(self, src, dst, *sems):
        for part, part_src, part_dst, part_sems in self._each(src, dst, sems):
            part.start(part_src, part_dst, *part_sems)

    def forward(self, src, dst, *sems):
        for part, part_src, part_dst, part_sems in self._each(src, dst, sems):
            part.forward(part_src, part_dst, *part_sems)

    def finish(self, src, dst, *sems):
        for part, part_src, part_dst, part_sems in self._each(src, dst, sems):
            part.finish(part_src, part_dst, *part_sems)


_ANY = pl.BlockSpec(memory_space=pl.ANY)


def _call(body, *, name, grid, in_specs, out_specs, out_shape, operands, scratch_shapes=(), parallel=False, comm=None):
    n_axes = len(grid)
    if comm is None:
        outs = pl.pallas_call(body, name=name, grid=grid, out_shape=list(out_shape), in_specs=list(in_specs),
                              out_specs=list(out_specs), scratch_shapes=list(scratch_shapes),
                              compiler_params=_params(n_axes, parallel))(*operands)
        return list(outs), None
    n_in, n_out, n_scr, n_c = len(in_specs), len(out_specs), len(scratch_shapes), len(comm.operands)
    total = math.prod(grid)

    def hosted(*refs):
        ins, c_src = refs[:n_in], refs[n_in:n_in + n_c]
        outs, c_dst = refs[n_in + n_c:n_in + n_c + n_out], refs[n_in + n_c + n_out:n_in + 2 * n_c + n_out]
        scr, sems = refs[n_in + 2 * n_c + n_out:n_in + 2 * n_c + n_out + n_scr], refs[n_in + 2 * n_c + n_out + n_scr:]
        step = pl.program_id(0)
        for axis in range(1, n_axes):
            step = step * grid[axis] + pl.program_id(axis)

        @pl.when(step == 0)
        def _():
            comm.start(c_src, c_dst, *sems)

        @pl.when(step == max(total - FORWARD_STEPS, 0))
        def _():
            comm.forward(c_src, c_dst, *sems)

        body(*ins, *outs, *scr)

        @pl.when(step == total - 1)
        def _():
            comm.finish(c_src, c_dst, *sems)

    res = pl.pallas_call(hosted, name=name, grid=grid, out_shape=list(out_shape) + comm.out_shape,
                         in_specs=list(in_specs) + [_ANY] * n_c, out_specs=list(out_specs) + [_ANY] * n_c,
                         scratch_shapes=list(scratch_shapes) + comm.scratch,
                         compiler_params=_params(n_axes, False))(*operands, *comm.operands)
    return list(res[:n_out]), list(res[n_out:])


def _all_gather_small(v, name):
    rows = v.shape[0]

    def body(v_ref, all_ref, sum_ref, send_sems, recv_sems):
        x, y, c, me = _my_position()
        all_ref[me] = v_ref[...]
        copies = []
        for p in range(1, N_DEV):
            cp = pltpu.make_async_remote_copy(
                src_ref=v_ref, dst_ref=all_ref.at[me], send_sem=send_sems.at[p - 1], recv_sem=recv_sems.at[p - 1],
                device_id=_peer(x, y, c, p), device_id_type=MESH)
            cp.start()
            copies.append(cp)
        for cp in copies:
            cp.wait()
        acc = all_ref[0]
        for d in range(1, N_DEV):
            acc = acc + all_ref[d]
        sum_ref[...] = acc

    return pl.pallas_call(
        body, name=name,
        out_shape=[jax.ShapeDtypeStruct((N_DEV, rows, LANES), F32), jax.ShapeDtypeStruct((rows, LANES), F32)],
        in_specs=[pl.BlockSpec(memory_space=pltpu.VMEM)],
        out_specs=[pl.BlockSpec(memory_space=pltpu.VMEM)] * 2,
        scratch_shapes=[pltpu.SemaphoreType.DMA((N_DEV - 1,)), pltpu.SemaphoreType.DMA((N_DEV - 1,))],
        compiler_params=pltpu.CompilerParams(vmem_limit_bytes=VMEM_LIMIT),
    )(v)


def _sum_gathered(gathered, name):
    rows = gathered.shape[1]

    def body(g_ref, o_ref):
        acc = g_ref[0]
        for d in range(1, N_DEV):
            acc = acc + g_ref[d]
        o_ref[...] = acc

    return pl.pallas_call(
        body, name=name, out_shape=jax.ShapeDtypeStruct((rows, LANES), F32),
        in_specs=[pl.BlockSpec(memory_space=pltpu.VMEM)], out_specs=pl.BlockSpec(memory_space=pltpu.VMEM),
        compiler_params=pltpu.CompilerParams(vmem_limit_bytes=VMEM_LIMIT),
    )(gathered)


def _pack_small(parts):
    flat = jnp.concatenate([p.reshape(-1).astype(F32) for p in parts])
    total = flat.shape[0]
    padded = -(-total // (8 * LANES)) * (8 * LANES)
    flat = jnp.pad(flat, (0, padded - total))
    return flat.reshape(padded // LANES, LANES)


def _unpack_small(packed, shapes, lead=()):
    flat = packed.reshape(lead + (-1,))
    out, off = [], 0
    for shp in shapes:
        size = math.prod(shp)
        out.append(flat[..., off:off + size].reshape(lead + tuple(shp)))
        off += size
    return out


def _prologue(c_rows, convw_rows, ada_w, ada_b_cols, gather):
    depth, _, cols = ada_w.shape
    n_c = len(gather.operands)
    sub = 8

    def body(c_ref, cw_ref, b_ref, w_hbm, *rest):
        g_src, (c_all_ref, cw_all_ref, ada_ref), g_dst = rest[:n_c], rest[n_c:n_c + 3], rest[n_c + 3:2 * n_c + 3]
        ada_local, w_ref, w_sem, send_sems, recv_sems = rest[2 * n_c + 3:2 * n_c + 8]
        g_sems = rest[2 * n_c + 8:]
        x, y, c, me = _my_position()
        gather.start(g_src, g_dst, *g_sems)
        load_w = pltpu.make_async_copy(w_hbm, w_ref, w_sem)
        load_w.start()

        def to_all(k, src_ref, dst_ref):
            copies = []
            for p in range(1, N_DEV):
                copies.append(pltpu.make_async_remote_copy(
                    src_ref=src_ref, dst_ref=dst_ref.at[me], send_sem=send_sems.at[k, p - 1], recv_sem=recv_sems.at[k, p - 1],
                    device_id=_peer(x, y, c, p), device_id_type=MESH))
            return copies

        first = to_all(0, c_ref, c_all_ref) + to_all(1, cw_ref, cw_all_ref)
        c_all_ref[me] = c_ref[...]
        cw_all_ref[me] = cw_ref[...]
        for cp in first:
            cp.start()
        for cp in first:
            cp.wait()
        cv = c_all_ref[...].reshape(N_DEV * sub, D_MODEL)
        act = (cv * _sigmoid(cv)).astype(BF16)
        load_w.wait()
        for l in range(depth):
            ada_local[l] = _dot(act, w_ref[l].astype(BF16)) + b_ref[l]
        ada_ref[me] = ada_local[:, pl.ds(pl.multiple_of(me * sub, sub), sub), :]
        rows_out = []
        for p in range(1, N_DEV):
            px, py, pc = _peer(x, y, c, p)
            rows = pl.ds(pl.multiple_of((4 * px + 2 * py + pc) * sub, sub), sub)
            rows_out.append(pltpu.make_async_remote_copy(
                src_ref=ada_local.at[:, rows, :], dst_ref=ada_ref.at[me], send_sem=send_sems.at[2, p - 1],
                recv_sem=recv_sems.at[2, p - 1], device_id=(px, py, pc), device_id_type=MESH))
        for cp in rows_out:
            cp.start()
        for cp in rows_out:
            cp.wait()
        gather.forward(g_src, g_dst, *g_sems)
        gather.finish(g_src, g_dst, *g_sems)

    vmem = pl.BlockSpec(memory_space=pltpu.VMEM)
    outs = pl.pallas_call(
        body, name="prologue",
        out_shape=[jax.ShapeDtypeStruct((N_DEV, sub, D_MODEL), F32), jax.ShapeDtypeStruct((N_DEV, sub, LANES), F32),
                   jax.ShapeDtypeStruct((N_DEV, depth, sub, cols), F32)] + gather.out_shape,
        in_specs=[vmem] * 3 + [_ANY] * (1 + n_c), out_specs=[vmem] * 3 + [_ANY] * n_c,
        scratch_shapes=[pltpu.VMEM((depth, N_DEV * sub, cols), F32), pltpu.VMEM(ada_w.shape, F32), pltpu.SemaphoreType.DMA,
                        pltpu.SemaphoreType.DMA((3, N_DEV - 1)), pltpu.SemaphoreType.DMA((3, N_DEV - 1))] + gather.scratch,
        compiler_params=pltpu.CompilerParams(vmem_limit_bytes=VMEM_LIMIT),
    )(c_rows, convw_rows, ada_b_cols, ada_w, *gather.operands)
    return outs[0], outs[1], outs[2], list(outs[3:])


def _ada_backward(c_all, d_ada_cols, d_ada_all):
    nb = c_all.shape[0]
    cols = d_ada_cols.shape[2]
    full = d_ada_all.shape[2]

    def body(c_ref, dc_ref, da_ref, gw_ref, gb_ref):
        cv = c_ref[...]
        act = (cv * _sigmoid(cv)).astype(BF16)
        gw_ref[0] = _dot_tn(act, dc_ref[0].astype(BF16))
        gb_ref[0] = jnp.sum(da_ref[0], axis=0, keepdims=True)

    return pl.pallas_call(
        body, name="ada_backward", grid=(DEPTH,),
        out_shape=[jax.ShapeDtypeStruct((DEPTH, D_MODEL, cols), F32), jax.ShapeDtypeStruct((DEPTH, 1, full), F32)],
        in_specs=[pl.BlockSpec((nb, D_MODEL), lambda l: (0, 0)),
                  pl.BlockSpec((1, nb, cols), lambda l: (l, 0, 0)),
                  pl.BlockSpec((1, nb, full), lambda l: (l, 0, 0))],
        out_specs=[pl.BlockSpec((1, D_MODEL, cols), lambda l: (l, 0, 0)),
                   pl.BlockSpec((1, 1, full), lambda l: (l, 0, 0))],
        compiler_params=_params(),
    )(c_all, d_ada_cols, d_ada_all)


def _rms(xv):
    return lax.rsqrt(jnp.mean(xv * xv, axis=-1, keepdims=True) + EPS)


def _normmod_matmul(x, gnorm, scale1p, shift, w_t, seq, name, swiglu=False, comm=None):
    tokens, n_out = x.shape[0], w_t.shape[0]
    tm = _tile_rows(seq)
    per_seq = seq // tm
    width = n_out // 2 if swiglu else n_out
    n_chunks = width // MXU_N

    def body(x_ref, g_ref, sc_ref, sh_ref, w_ref, h_ref, *o_refs):
        xv = x_ref[...]
        h = (xv * _rms(xv) * g_ref[...]) * sc_ref[0] + sh_ref[0]
        h_ref[...] = h.astype(BF16)
        for ck in range(n_chunks):
            cs = slice(ck * MXU_N, (ck + 1) * MXU_N)
            if swiglu:
                act_ref, silu_ref, dact_ref = o_refs
                g = _dot_nt(h_ref[...], w_ref[cs, :])
                u = _dot_nt(h_ref[...], w_ref[width + ck * MXU_N:width + (ck + 1) * MXU_N, :])
                sig = _sigmoid(g)
                silu = g * sig
                act_ref[:, cs] = (silu * u).astype(BF16)
                silu_ref[:, cs] = silu.astype(BF16)
                dact_ref[:, cs] = (u * (sig + silu * (1.0 - sig))).astype(BF16)
            else:
                o_refs[0][:, cs] = _dot_nt(h_ref[...], w_ref[cs, :]).astype(BF16)

    n_res = 3 if swiglu else 1
    per_batch = pl.BlockSpec((1, 1, D_MODEL), lambda i: (i // per_seq, 0, 0))
    outs, got = _call(
        body, name=name, grid=(tokens // tm,),
        out_shape=[jax.ShapeDtypeStruct((tokens, D_MODEL), BF16)] + [jax.ShapeDtypeStruct((tokens, width), BF16)] * n_res,
        in_specs=[pl.BlockSpec((tm, D_MODEL), lambda i: (i, 0)), _resident((1, D_MODEL)), per_batch, per_batch,
                  _resident(w_t.shape)],
        out_specs=[pl.BlockSpec((tm, D_MODEL), lambda i: (i, 0))] + [pl.BlockSpec((tm, width), lambda i: (i, 0))] * n_res,
        operands=(x, gnorm, scale1p, shift, w_t), parallel=True, comm=comm)
    return (*outs, got)


def _matmul_residual(src, w, x, gate, scale, seq, name, comm=None):
    tokens, k_dim = x.shape[0], w.shape[0]
    tm = _tile_rows(seq)
    per_seq = seq // tm

    def body(s_ref, w_ref, x_ref, gate_ref, xo_ref, f_ref):
        f = _dot(s_ref[...], w_ref[...])
        f_ref[...] = f.astype(BF16)
        xo_ref[...] = x_ref[...] + (scale * gate_ref[0]) * f

    (x_out, f), got = _call(
        body, name=name, grid=(tokens // tm,),
        out_shape=[jax.ShapeDtypeStruct((tokens, D_MODEL), F32), jax.ShapeDtypeStruct((tokens, D_MODEL), BF16)],
        in_specs=[pl.BlockSpec((tm, k_dim), lambda i: (i, 0)), _resident(w.shape),
                  pl.BlockSpec((tm, D_MODEL), lambda i: (i, 0)),
                  pl.BlockSpec((1, 1, D_MODEL), lambda i: (i // per_seq, 0, 0))],
        out_specs=[pl.BlockSpec((tm, D_MODEL), lambda i: (i, 0))] * 2,
        operands=(src, w, x, gate), parallel=True, comm=comm)
    return x_out, f, got


def _loss_tile(xv, target, gn):
    r = _rms(xv)
    xn = xv * r
    err = xn * gn - target
    loss = (0.5 / D_MODEL) * jnp.sum(err * err, axis=0, keepdims=True)
    dyv = err * (1.0 / D_MODEL)
    dg = jnp.sum(dyv * xn, axis=0, keepdims=True)
    dxn = dyv * gn
    dx = r * (dxn - xn * jnp.mean(dxn * xn, axis=-1, keepdims=True))
    return loss, dx, dg


def _ffn_forward(x, gnorm, scale1p, shift, w_gu_t, w_down, gate, scale, seq, name, loss_head=None, comm=None):
    tokens, width = x.shape[0], w_down.shape[0]
    tm = _tile_rows(seq)
    per_seq = seq // tm
    n_chunks = width // MXU_N

    def body(x_ref, g_ref, sc_ref, sh_ref, wgu_ref, wd_ref, gate_ref, *rest):
        if loss_head:
            t_ref, gf_ref, h_ref, act_ref, silu_ref, dact_ref, xo_ref, f_ref, dgf_ref, loss_ref = rest
        else:
            h_ref, act_ref, silu_ref, dact_ref, xo_ref, f_ref = rest
        xv = x_ref[...]
        h = (xv * _rms(xv) * g_ref[...]) * sc_ref[0] + sh_ref[0]
        h_ref[...] = h.astype(BF16)
        for ck in range(n_chunks):
            cs = slice(ck * MXU_N, (ck + 1) * MXU_N)
            g = _dot_nt(h_ref[...], wgu_ref[cs, :])
            u = _dot_nt(h_ref[...], wgu_ref[width + ck * MXU_N:width + (ck + 1) * MXU_N, :])
            sig = _sigmoid(g)
            silu = g * sig
            act_ref[:, cs] = (silu * u).astype(BF16)
            silu_ref[:, cs] = silu.astype(BF16)
            dact_ref[:, cs] = (u * (sig + silu * (1.0 - sig))).astype(BF16)
        f = _dot(act_ref[...], wd_ref[...])
        f_ref[...] = f.astype(BF16)
        x_out = xv + (scale * gate_ref[0]) * f
        if loss_head:
            i = pl.program_id(0)
            loss, dx, dg = _loss_tile(x_out, t_ref[...], gf_ref[...])
            xo_ref[...] = dx

            @pl.when(i == 0)
            def _():
                dgf_ref[...] = dg
                loss_ref[...] = loss

            @pl.when(i != 0)
            def _():
                dgf_ref[...] = dgf_ref[...] + dg
                loss_ref[...] = loss_ref[...] + loss
        else:
            xo_ref[...] = x_out

    row = lambda i: (i, 0)
    per_batch = pl.BlockSpec((1, 1, D_MODEL), lambda i: (i // per_seq, 0, 0))
    tile = lambda cols: pl.BlockSpec((tm, cols), row)
    wide = jax.ShapeDtypeStruct((tokens, width), BF16)
    fixed = pl.BlockSpec((1, D_MODEL), lambda i: (0, 0))
    vec = jax.ShapeDtypeStruct((1, D_MODEL), F32)
    outs, got = _call(
        body, name=name, grid=(tokens // tm,),
        out_shape=[jax.ShapeDtypeStruct((tokens, D_MODEL), BF16), wide, wide, wide,
                   jax.ShapeDtypeStruct((tokens, D_MODEL), F32), jax.ShapeDtypeStruct((tokens, D_MODEL), BF16)]
        + ([vec, vec] if loss_head else []),
        in_specs=[tile(D_MODEL), _resident((1, D_MODEL)), per_batch, per_batch, _resident(w_gu_t.shape),
                  _resident(w_down.shape), per_batch] + ([tile(D_MODEL), _resident((1, D_MODEL))] if loss_head else []),
        out_specs=[tile(D_MODEL), tile(width), tile(width), tile(width), tile(D_MODEL), tile(D_MODEL)]
        + ([fixed, fixed] if loss_head else []),
        operands=(x, gnorm, scale1p, shift, w_gu_t, w_down, gate) + (tuple(loss_head) if loss_head else ()),
        parallel=not loss_head, comm=comm)
    return (*outs, got)


def _residual_backward(dy, gate, f, w, scale, silu, dact, seq, name, comm=None):
    tokens, k_dim = dy.shape[0], w.shape[0]
    batch = tokens // seq
    tm = _tile_rows(seq)
    per_seq = seq // tm
    n_chunks = k_dim // MXU_N

    def body(dy_ref, gate_ref, f_ref, silu_ref, dact_ref, w_ref, df_ref, dgate_ref, dgu_ref):
        i = pl.program_id(0)
        dy_v = dy_ref[...]
        df_ref[...] = ((scale * gate_ref[0]) * dy_v).astype(BF16)
        part = scale * jnp.sum(dy_v * f_ref[...].astype(F32), axis=0, keepdims=True)

        @pl.when(i % per_seq == 0)
        def _():
            dgate_ref[0] = part

        @pl.when(i % per_seq != 0)
        def _():
            dgate_ref[0] = dgate_ref[0] + part

        for ck in range(n_chunks):
            cs = slice(ck * MXU_N, (ck + 1) * MXU_N)
            cu = slice(k_dim + ck * MXU_N, k_dim + (ck + 1) * MXU_N)
            da = _dot_nt(df_ref[...], w_ref[cs, :])
            dgu_ref[:, cs] = (da * dact_ref[:, cs].astype(F32)).astype(BF16)
            dgu_ref[:, cu] = (da * silu_ref[:, cs].astype(F32)).astype(BF16)

    row = lambda i: (i, 0)
    per_batch = pl.BlockSpec((1, 1, D_MODEL), lambda i: (i // per_seq, 0, 0))
    tile = lambda cols: pl.BlockSpec((tm, cols), row)
    outs, got = _call(
        body, name=name, grid=(tokens // tm,),
        out_shape=[jax.ShapeDtypeStruct((tokens, D_MODEL), BF16), jax.ShapeDtypeStruct((batch, 1, D_MODEL), F32),
                   jax.ShapeDtypeStruct((tokens, 2 * k_dim), BF16)],
        in_specs=[tile(D_MODEL), per_batch, tile(D_MODEL), tile(k_dim), tile(k_dim), _resident(w.shape)],
        out_specs=[tile(D_MODEL), per_batch, tile(2 * k_dim)],
        operands=(dy, gate, f, silu, dact, w), comm=comm)
    return (*outs, got)


def _matmul_normmod_backward(dsrc, w_t, x, dy, gnorm, scale1p, seq, name, comm=None):
    tokens, k_dim = dsrc.shape
    batch = tokens // seq
    tm = _tile_rows(seq)
    per_seq = seq // tm

    def body(ds_ref, w_ref, x_ref, dy_ref, g_ref, sc_ref, dx_ref, dsh_ref, dsc_ref, dg_ref):
        i = pl.program_id(0)
        dh = _dot(ds_ref[...], w_ref[...])
        xv = x_ref[...]
        r = _rms(xv)
        xn = xv * r
        gn = g_ref[...]
        dsh = jnp.sum(dh, axis=0, keepdims=True)
        dsc = jnp.sum(dh * (xn * gn), axis=0, keepdims=True)
        dhn = dh * sc_ref[0]
        dg = jnp.sum(dhn * xn, axis=0, keepdims=True)
        dxn = dhn * gn
        dx_ref[...] = dy_ref[...] + r * (dxn - xn * jnp.mean(dxn * xn, axis=-1, keepdims=True))

        @pl.when(i % per_seq == 0)
        def _():
            dsh_ref[0] = dsh
            dsc_ref[0] = dsc

        @pl.when(i % per_seq != 0)
        def _():
            dsh_ref[0] = dsh_ref[0] + dsh
            dsc_ref[0] = dsc_ref[0] + dsc

        @pl.when(i == 0)
        def _():
            dg_ref[...] = dg

        @pl.when(i != 0)
        def _():
            dg_ref[...] = dg_ref[...] + dg

    row = lambda i: (i, 0)
    per_batch = pl.BlockSpec((1, 1, D_MODEL), lambda i: (i // per_seq, 0, 0))
    outs, got = _call(
        body, name=name, grid=(tokens // tm,),
        out_shape=[jax.ShapeDtypeStruct((tokens, D_MODEL), F32), jax.ShapeDtypeStruct((batch, 1, D_MODEL), F32),
                   jax.ShapeDtypeStruct((batch, 1, D_MODEL), F32), jax.ShapeDtypeStruct((1, D_MODEL), F32)],
        in_specs=[pl.BlockSpec((tm, k_dim), row), _resident(w_t.shape), pl.BlockSpec((tm, D_MODEL), row),
                  pl.BlockSpec((tm, D_MODEL), row), _resident((1, D_MODEL)), per_batch],
        out_specs=[pl.BlockSpec((tm, D_MODEL), row), per_batch, per_batch, pl.BlockSpec((1, D_MODEL), lambda i: (0, 0))],
        operands=(dsrc, w_t, x, dy, gnorm, scale1p), comm=comm)
    return (*outs, got)


def _ffn_backward(dy, gate, f, silu, dact, w_down, w_gu_t, x, gnorm, scale1p, scale, seq, name, comm=None):
    tokens, k_dim = dy.shape[0], w_down.shape[0]
    batch = tokens // seq
    tm = min(256, seq)
    per_seq = seq // tm
    n_chunks = k_dim // MXU_N

    def body(dy_ref, gate_ref, f_ref, silu_ref, dact_ref, wd_ref, wgu_ref, x_ref, g_ref, sc_ref,
             df_ref, dgate_ref, dgu_ref, dx_ref, dsh_ref, dsc_ref, dg_ref):
        i = pl.program_id(0)
        dy_v = dy_ref[...]
        df_ref[...] = ((scale * gate_ref[0]) * dy_v).astype(BF16)
        dgate = scale * jnp.sum(dy_v * f_ref[...].astype(F32), axis=0, keepdims=True)
        for ck in range(n_chunks):
            cs = slice(ck * MXU_N, (ck + 1) * MXU_N)
            cu = slice(k_dim + ck * MXU_N, k_dim + (ck + 1) * MXU_N)
            da = _dot_nt(df_ref[...], wd_ref[cs, :])
            dgu_ref[:, cs] = (da * dact_ref[:, cs].astype(F32)).astype(BF16)
            dgu_ref[:, cu] = (da * silu_ref[:, cs].astype(F32)).astype(BF16)
        dh = _dot(dgu_ref[...], wgu_ref[...])
        xv = x_ref[...]
        r = _rms(xv)
        xn = xv * r
        gn = g_ref[...]
        dsh = jnp.sum(dh, axis=0, keepdims=True)
        dsc = jnp.sum(dh * (xn * gn), axis=0, keepdims=True)
        dhn = dh * sc_ref[0]
        dg = jnp.sum(dhn * xn, axis=0, keepdims=True)
        dxn = dhn * gn
        dx_ref[...] = dy_v + r * (dxn - xn * jnp.mean(dxn * xn, axis=-1, keepdims=True))

        @pl.when(i % per_seq == 0)
        def _():
            dgate_ref[0] = dgate
            dsh_ref[0] = dsh
            dsc_ref[0] = dsc

        @pl.when(i % per_seq != 0)
        def _():
            dgate_ref[0] = dgate_ref[0] + dgate
            dsh_ref[0] = dsh_ref[0] + dsh
            dsc_ref[0] = dsc_ref[0] + dsc

        @pl.when(i == 0)
        def _():
            dg_ref[...] = dg

        @pl.when(i != 0)
        def _():
            dg_ref[...] = dg_ref[...] + dg

    row = lambda i: (i, 0)
    per_batch = pl.BlockSpec((1, 1, D_MODEL), lambda i: (i // per_seq, 0, 0))
    tile = lambda width: pl.BlockSpec((tm, width), row)
    vec = jax.ShapeDtypeStruct((batch, 1, D_MODEL), F32)
    outs, got = _call(
        body, name=name, grid=(tokens // tm,),
        out_shape=[jax.ShapeDtypeStruct((tokens, D_MODEL), BF16), vec, jax.ShapeDtypeStruct((tokens, 2 * k_dim), BF16),
                   jax.ShapeDtypeStruct((tokens, D_MODEL), F32), vec, vec, jax.ShapeDtypeStruct((1, D_MODEL), F32)],
        in_specs=[tile(D_MODEL), per_batch, tile(D_MODEL), tile(k_dim), tile(k_dim), _resident(w_down.shape),
                  _resident(w_gu_t.shape), tile(D_MODEL), _resident((1, D_MODEL)), per_batch],
        out_specs=[tile(D_MODEL), per_batch, tile(2 * k_dim), tile(D_MODEL), per_batch, per_batch,
                   pl.BlockSpec((1, D_MODEL), lambda i: (0, 0))],
        operands=(dy, gate, f, silu, dact, w_down, w_gu_t, x, gnorm, scale1p), comm=comm)
    return (*outs, got)


def _weight_grad(a, b, seq, name, comm=None):
    tokens, n_out = a.shape
    tn = MXU_N

    def body(a_ref, b_ref, o_ref):
        o_ref[...] = _dot_tn(a_ref[...], b_ref[...]).astype(BF16)

    (out,), got = _call(
        body, name=name, grid=(n_out // tn,),
        out_shape=[jax.ShapeDtypeStruct((n_out, D_MODEL), BF16)],
        in_specs=[pl.BlockSpec((tokens, tn), lambda j: (0, j)), _resident((tokens, D_MODEL))],
        out_specs=[pl.BlockSpec((tn, D_MODEL), lambda j: (j, 0))],
        operands=(a, b), parallel=True, comm=comm)
    return out, got


def _group_mean(v, bd):
    hi = v.astype(BF16)
    lo = (v - hi.astype(F32)).astype(BF16)
    return _dot(hi, bd) + _dot(lo, bd)


def _sgu_forward(pm_ref, wm_ref, bias_ref, lng_ref, lnb_ref, bd_ref, mixed_scr, n_sub):
    ua = pm_ref[:, 0:D_A].astype(F32)
    va = pm_ref[:, D_A:2 * D_A].astype(F32)
    u_act = _gelu(ua)
    v_act = _gelu(va)
    bd = bd_ref[...]
    vc = v_act - _group_mean(v_act, bd)
    rstd = lax.rsqrt(_group_mean(vc * vc, bd) + EPS)
    vhat = vc * rstd
    vln = vhat * lng_ref[...] + lnb_ref[...]
    left = lax.broadcasted_iota(jnp.int32, (CHUNK, LANES), 1) < HEAD_DIM
    for q in range(n_sub):
        rows = slice(q * CHUNK, (q + 1) * CHUNK)
        for p in range(N_HEADS // 2):
            cols = slice(p * LANES, (p + 1) * LANES)
            vp = vln[rows, cols]
            v_l = jnp.where(left, vp, 0.0).astype(BF16)
            v_r = jnp.where(left, 0.0, vp).astype(BF16)
            mixed_scr[rows, cols] = _dot(wm_ref[2 * p], v_l) + _dot(wm_ref[2 * p + 1], v_r) + bias_ref[:, cols]
    return ua, va, u_act, vhat, rstd, vln


def _halo_specs(tm, tokens, width):
    prev = pl.BlockSpec((HALO, width), lambda i: (jnp.maximum(i * (tm // HALO) - 1, 0), 0))
    nxt = pl.BlockSpec((HALO, width), lambda i: (jnp.minimum((i + 1) * (tm // HALO), tokens // HALO - 1), 0))
    return prev, nxt


def _mixer_forward(x, gnorm, scale1p, shift, w_in_t, gate, w_out, wm, bias_full, lng, lnb, convw, og, bd, seq, name, comm=None):
    tokens = x.shape[0]
    tm = _tile_rows(seq)
    per_seq = seq // tm
    n_sub = tm // CHUNK

    def body(x_ref, xp_ref, g_ref, sc_ref, sh_ref, win_ref, gate_ref, wo_ref, wm_ref, bias_ref, lng_ref, lnb_ref, cw_ref,
             og_ref, bd_ref, h_ref, pm_ref, y_ref, xo_ref, o_ref, mixed_scr):
        i = pl.program_id(0)
        first = (i % per_seq) == 0
        xv = x_ref[...]
        h_ref[...] = ((xv * _rms(xv) * g_ref[...]) * sc_ref[0] + sh_ref[0]).astype(BF16)
        for ck in range(D_PROJ // MXU_N):
            cs = slice(ck * MXU_N, (ck + 1) * MXU_N)
            pm_ref[:, cs] = _dot_nt(h_ref[...], win_ref[cs, :]).astype(BF16)
        xp = xp_ref[...]
        hp = ((xp * _rms(xp) * g_ref[...]) * sc_ref[0] + sh_ref[0]).astype(BF16)
        gates_prev = _dot_nt(hp, win_ref[3 * D_A:5 * D_A, :]).astype(BF16).astype(F32)

        _, _, u_act, _, _, _ = _sgu_forward(pm_ref, wm_ref, bias_ref, lng_ref, lnb_ref, bd_ref, mixed_scr, n_sub)
        ya = u_act * mixed_scr[...]
        y_ref[:, 0:D_A] = (ya * _rms(ya) * og_ref[:, 0:D_A]).astype(BF16)

        bg = pm_ref[:, 2 * D_A:3 * D_A].astype(F32)
        z = pm_ref[:, 3 * D_A:4 * D_A].astype(F32) * pm_ref[:, 4 * D_A:5 * D_A].astype(F32)
        zp = jnp.where(first, 0.0, gates_prev[:, 0:D_A] * gates_prev[:, D_A:2 * D_A])
        zext = jnp.concatenate([zp, z], axis=0)
        z1 = pltpu.roll(zext, 1, 0)[HALO:]
        z2 = pltpu.roll(zext, 2, 0)[HALO:]
        conv = cw_ref[0:1, :] * z2 + cw_ref[1:2, :] * z1 + cw_ref[2:3, :] * z
        yb = bg * conv
        y_ref[:, D_A:2 * D_A] = (yb * _rms(yb) * og_ref[:, D_A:2 * D_A]).astype(BF16)

        f = _dot(y_ref[...], wo_ref[...])
        o_ref[...] = f.astype(BF16)
        xo_ref[...] = xv + gate_ref[0] * f

    prev, _ = _halo_specs(tm, tokens, D_MODEL)
    tile = pl.BlockSpec((tm, D_MODEL), lambda i: (i, 0))
    per_batch = pl.BlockSpec((1, 1, D_MODEL), lambda i: (i // per_seq, 0, 0))
    bf = lambda cols: jax.ShapeDtypeStruct((tokens, cols), BF16)
    outs, got = _call(
        body, name=name, grid=(tokens // tm,),
        out_shape=[bf(D_MODEL), bf(D_PROJ), bf(D_MODEL), jax.ShapeDtypeStruct((tokens, D_MODEL), F32), bf(D_MODEL)],
        in_specs=[tile, prev, _resident((1, D_MODEL)), per_batch, per_batch, _resident(w_in_t.shape), per_batch,
                  _resident(w_out.shape), _resident(wm.shape), _resident(bias_full.shape), _resident(lng.shape),
                  _resident(lnb.shape), _resident(convw.shape), _resident(og.shape), _resident(bd.shape)],
        out_specs=[tile, pl.BlockSpec((tm, D_PROJ), lambda i: (i, 0)), tile, tile, tile],
        scratch_shapes=[pltpu.VMEM((tm, D_A), F32)],
        operands=(x, x, gnorm, scale1p, shift, w_in_t, gate, w_out, wm, bias_full, lng, lnb, convw, og, bd),
        parallel=True, comm=comm)
    return (*outs, got)


def _mixer_backward(proj, dx, gate, o, w_out, x, gnorm, scale1p, w_in_t, wm, bias_full, lng, lnb, convw, og, bd, causal,
                    seq, name, comm=None):
    tokens = proj.shape[0]
    batch = tokens // seq
    tm = _tile_rows(seq)
    per_seq = seq // tm
    n_sub = tm // CHUNK
    ext = tm + 2 * HALO

    def body(pm_ref, pp_ref, pn_ref, dx_ref, dxn_ref, gate_ref, o_ref, wo_ref, x_ref, g_ref, sc_ref, win_ref, wm_ref, bias_ref,
             lng_ref, lnb_ref, cw_ref, og_ref, bd_ref, causal_ref, do_ref, dgate_ref, dp_ref, dxo_ref, dsh_ref, dsc_ref, dgn_ref,
             dog_ref, dcw_ref, dlng_ref, dlnb_ref, dbias_ref, dwm_ref, mixed_scr, dvln_scr, dy_scr):
        i = pl.program_id(0)
        first = (i % per_seq) == 0
        last = (i % per_seq) == per_seq - 1

        dx_v = dx_ref[...]
        do_ref[...] = (gate_ref[0] * dx_v).astype(BF16)
        dgate = jnp.sum(dx_v * o_ref[...].astype(F32), axis=0, keepdims=True)
        dy_scr[...] = _dot_nt(do_ref[...], wo_ref[...])
        dyn_conv = _dot_nt((gate_ref[0] * dxn_ref[...]).astype(BF16), wo_ref[D_A:2 * D_A, :])

        @pl.when(i == 0)
        def _():
            dog_ref[...] = jnp.zeros_like(dog_ref)
            dcw_ref[...] = jnp.zeros_like(dcw_ref)
            dlng_ref[...] = jnp.zeros_like(dlng_ref)
            dlnb_ref[...] = jnp.zeros_like(dlnb_ref)
            dbias_ref[...] = jnp.zeros_like(dbias_ref)
            dwm_ref[...] = jnp.zeros_like(dwm_ref)

        ua, va, u_act, vhat, rstd, vln = _sgu_forward(pm_ref, wm_ref, bias_ref, lng_ref, lnb_ref, bd_ref, mixed_scr, n_sub)
        mixed = mixed_scr[...]
        ya = u_act * mixed
        ra = _rms(ya)
        yhat = ya * ra
        dya_in = dy_scr[:, 0:D_A]
        dog_ref[:, 0:D_A] = dog_ref[:, 0:D_A] + jnp.sum(dya_in * yhat, axis=0, keepdims=True)
        dyh = dya_in * og_ref[:, 0:D_A]
        dya = ra * (dyh - yhat * jnp.mean(dyh * yhat, axis=-1, keepdims=True))
        d_u = dya * mixed
        d_mixed = dya * u_act
        left = lax.broadcasted_iota(jnp.int32, (CHUNK, LANES), 1) < HEAD_DIM
        dbias = jnp.zeros((CHUNK, D_A), F32)
        for q in range(n_sub):
            rows = slice(q * CHUNK, (q + 1) * CHUNK)
            dbias = dbias + d_mixed[rows, :]
            for p in range(N_HEADS // 2):
                cols = slice(p * LANES, (p + 1) * LANES)
                dm = d_mixed[rows, cols]
                dm_l = jnp.where(left, dm, 0.0).astype(BF16)
                dm_r = jnp.where(left, 0.0, dm).astype(BF16)
                vp = vln[rows, cols].astype(BF16)
                dwm_ref[2 * p] = dwm_ref[2 * p] + causal_ref[...] * _dot_nt(dm_l, vp)
                dwm_ref[2 * p + 1] = dwm_ref[2 * p + 1] + causal_ref[...] * _dot_nt(dm_r, vp)
                dvln_scr[rows, cols] = _dot_tn(wm_ref[2 * p], dm_l) + _dot_tn(wm_ref[2 * p + 1], dm_r)
        dbias_ref[...] = dbias_ref[...] + dbias
        dvln = dvln_scr[...]
        dlng_ref[...] = dlng_ref[...] + jnp.sum(dvln * vhat, axis=0, keepdims=True)
        dlnb_ref[...] = dlnb_ref[...] + jnp.sum(dvln, axis=0, keepdims=True)
        dvh = dvln * lng_ref[...]
        bd = bd_ref[...]
        d_v = rstd * (dvh - _group_mean(dvh, bd) - vhat * _group_mean(dvh * vhat, bd))
        dp_ref[:, 0:D_A] = (d_u * _gelu_grad(ua)).astype(BF16)
        dp_ref[:, D_A:2 * D_A] = (d_v * _gelu_grad(va)).astype(BF16)
        dh_a = _dot(dp_ref[:, 0:2 * D_A], win_ref[0:2 * D_A, :])

        def ext_cols(lo):
            cs = slice(lo, lo + D_A)
            return jnp.concatenate([pp_ref[:, cs], pm_ref[:, cs], pn_ref[:, cs]], axis=0).astype(F32)

        bg, cg, xb = ext_cols(2 * D_A), ext_cols(3 * D_A), ext_cols(4 * D_A)
        row = lax.broadcasted_iota(jnp.int32, (ext, D_A), 0)
        z = jnp.where(jnp.logical_and(first, row < HALO), 0.0, cg * xb)
        z1 = pltpu.roll(z, 1, 0)
        z2 = pltpu.roll(z, 2, 0)
        w0, w1, w2 = cw_ref[0:1, :], cw_ref[1:2, :], cw_ref[2:3, :]
        conv = w0 * z2 + w1 * z1 + w2 * z
        yb = bg * conv
        rb = _rms(yb)
        yhb = yb * rb
        dyn = jnp.where(last, 0.0, dyn_conv)
        dyb_in = jnp.concatenate([jnp.zeros((HALO, D_A), F32), dy_scr[:, D_A:2 * D_A], dyn], axis=0)
        dyhb = dyb_in * og_ref[:, D_A:2 * D_A]
        dyb = rb * (dyhb - yhb * jnp.mean(dyhb * yhb, axis=-1, keepdims=True))
        d_conv = dyb * bg
        dz = w2 * d_conv + w1 * pltpu.roll(d_conv, ext - 1, 0) + w0 * pltpu.roll(d_conv, ext - 2, 0)
        main = slice(HALO, HALO + tm)
        dp_ref[:, 2 * D_A:3 * D_A] = (dyb * conv)[main].astype(BF16)
        dp_ref[:, 3 * D_A:4 * D_A] = (dz * xb)[main].astype(BF16)
        dp_ref[:, 4 * D_A:5 * D_A] = (dz * cg)[main].astype(BF16)
        dog_ref[:, D_A:2 * D_A] = dog_ref[:, D_A:2 * D_A] + jnp.sum((dyb_in * yhb)[main], axis=0, keepdims=True)
        dcm = d_conv[main]
        dcw_ref[0:1, :] = dcw_ref[0:1, :] + jnp.sum(dcm * z2[main], axis=0, keepdims=True)
        dcw_ref[1:2, :] = dcw_ref[1:2, :] + jnp.sum(dcm * z1[main], axis=0, keepdims=True)
        dcw_ref[2:3, :] = dcw_ref[2:3, :] + jnp.sum(dcm * z[main], axis=0, keepdims=True)

        dh = dh_a + _dot(dp_ref[:, 2 * D_A:5 * D_A], win_ref[2 * D_A:5 * D_A, :])
        xv = x_ref[...]
        r = _rms(xv)
        xn = xv * r
        gn = g_ref[...]
        dsh = jnp.sum(dh, axis=0, keepdims=True)
        dsc = jnp.sum(dh * (xn * gn), axis=0, keepdims=True)
        dhn = dh * sc_ref[0]
        dgn = jnp.sum(dhn * xn, axis=0, keepdims=True)
        dxn = dhn * gn
        dxo_ref[...] = dx_v + r * (dxn - xn * jnp.mean(dxn * xn, axis=-1, keepdims=True))

        @pl.when(first)
        def _():
            dgate_ref[0] = dgate
            dsh_ref[0] = dsh
            dsc_ref[0] = dsc

        @pl.when(jnp.logical_not(first))
        def _():
            dgate_ref[0] = dgate_ref[0] + dgate
            dsh_ref[0] = dsh_ref[0] + dsh
            dsc_ref[0] = dsc_ref[0] + dsc

        @pl.when(i == 0)
        def _():
            dgn_ref[...] = dgn

        @pl.when(i != 0)
        def _():
            dgn_ref[...] = dgn_ref[...] + dgn

    prev_p, next_p = _halo_specs(tm, tokens, D_PROJ)
    _, next_d = _halo_specs(tm, tokens, D_MODEL)
    fixed2 = lambda shape: pl.BlockSpec(shape, lambda i: (0, 0))
    tile = pl.BlockSpec((tm, D_MODEL), lambda i: (i, 0))
    per_batch = pl.BlockSpec((1, 1, D_MODEL), lambda i: (i // per_seq, 0, 0))
    vec = jax.ShapeDtypeStruct((batch, 1, D_MODEL), F32)
    outs, got = _call(
        body, name=name, grid=(tokens // tm,),
        out_shape=[jax.ShapeDtypeStruct((tokens, D_MODEL), BF16), vec, jax.ShapeDtypeStruct((tokens, D_PROJ), BF16),
                   jax.ShapeDtypeStruct((tokens, D_MODEL), F32), vec, vec, jax.ShapeDtypeStruct((1, D_MODEL), F32),
                   jax.ShapeDtypeStruct((1, D_MODEL), F32), jax.ShapeDtypeStruct((8, D_A), F32),
                   jax.ShapeDtypeStruct((1, D_A), F32), jax.ShapeDtypeStruct((1, D_A), F32),
                   jax.ShapeDtypeStruct((CHUNK, D_A), F32), jax.ShapeDtypeStruct((N_HEADS, CHUNK, CHUNK), F32)],
        in_specs=[pl.BlockSpec((tm, D_PROJ), lambda i: (i, 0)), prev_p, next_p, tile, next_d, per_batch, tile,
                  _resident(w_out.shape), tile, _resident((1, D_MODEL)), per_batch, _resident(w_in_t.shape),
                  _resident(wm.shape), _resident(bias_full.shape), _resident(lng.shape), _resident(lnb.shape),
                  _resident(convw.shape), _resident(og.shape), _resident(bd.shape), _resident(causal.shape)],
        out_specs=[tile, per_batch, pl.BlockSpec((tm, D_PROJ), lambda i: (i, 0)), tile, per_batch, per_batch,
                   fixed2((1, D_MODEL)), fixed2((1, D_MODEL)), fixed2((8, D_A)), fixed2((1, D_A)), fixed2((1, D_A)),
                   fixed2((CHUNK, D_A)), pl.BlockSpec((N_HEADS, CHUNK, CHUNK), lambda i: (0, 0, 0))],
        scratch_shapes=[pltpu.VMEM((tm, D_A), F32), pltpu.VMEM((tm, D_A), F32), pltpu.VMEM((tm, D_MODEL), F32)],
        operands=(proj, proj, proj, dx, dx, gate, o, w_out, x, gnorm, scale1p, w_in_t, wm, bias_full, lng, lnb, convw, og, bd,
                  causal), comm=comm)
    return (*outs, got)


def _adamw_update(wv, gv, mv, vv):
    nm = ADAM_B1 * mv + (1.0 - ADAM_B1) * gv
    nv = ADAM_B2 * vv + (1.0 - ADAM_B2) * (gv * gv)
    m_hat = nm / (1.0 - ADAM_B1 ** ADAM_STEP)
    v_hat = nv / (1.0 - ADAM_B2 ** ADAM_STEP)
    return -ADAM_LR * (m_hat / (jnp.sqrt(v_hat) + ADAM_EPS) + ADAM_WD * wv), nm, nv


def _adamw_rows(recv, w, m, v, name):
    depth, rows, cols = w.shape
    tr = rows // 2
    last = rows // tr - 1

    def body(*refs):
        r_refs, (w_ref, m_ref, v_ref, g_ref, d_ref, nm_ref, nv_ref) = refs[:depth], refs[depth:]
        for l in range(depth):
            @pl.when(pl.program_id(0) == l)
            def _(r_ref=r_refs[l]):
                acc = r_ref[0].astype(F32)
                for d in range(1, N_DEV):
                    acc = acc + r_ref[d].astype(F32)
                g_ref[0] = acc
                d_ref[0], nm_ref[0], nv_ref[0] = _adamw_update(w_ref[0], acc, m_ref[0], v_ref[0])

    def slots(l):
        return pl.BlockSpec((N_DEV, tr, cols), lambda ll, i: (0, jnp.where(ll == l, i, jnp.where(ll < l, 0, last)), 0))

    spec = pl.BlockSpec((1, tr, cols), lambda ll, i: (ll, i, 0))
    return pl.pallas_call(
        body, name=name, grid=(depth, rows // tr),
        out_shape=[jax.ShapeDtypeStruct((depth, rows, cols), F32)] * 4,
        in_specs=[slots(l) for l in range(depth)] + [spec] * 3, out_specs=[spec] * 4,
        compiler_params=_params(2),
    )(*recv, w, m, v)


def _adamw(w, g, m, v, name):
    rows, cols = w.shape
    tr = max(t for t in range(8, 513, 8) if rows % t == 0)

    def body(w_ref, g_ref, m_ref, v_ref, d_ref, nm_ref, nv_ref):
        d_ref[...], nm_ref[...], nv_ref[...] = _adamw_update(w_ref[...], g_ref[...], m_ref[...], v_ref[...])

    spec = pl.BlockSpec((tr, cols), lambda i: (i, 0))
    return pl.pallas_call(
        body, name=name, grid=(rows // tr,),
        out_shape=[jax.ShapeDtypeStruct((rows, cols), F32)] * 3,
        in_specs=[spec] * 4, out_specs=[spec] * 3,
        compiler_params=_params(parallel=True),
    )(w, g, m, v)


def _adamw_many(ws, gs, ms, vs, name):
    n = len(ws)
    two_d = lambda a: a.reshape(-1, a.shape[-1])

    def body(*refs):
        w_refs, g_refs, m_refs, v_refs = refs[:n], refs[n:2 * n], refs[2 * n:3 * n], refs[3 * n:4 * n]
        d_refs, nm_refs, nv_refs = refs[4 * n:5 * n], refs[5 * n:6 * n], refs[6 * n:]
        for k in range(n):
            d_refs[k][...], nm_refs[k][...], nv_refs[k][...] = _adamw_update(
                w_refs[k][...], g_refs[k][...], m_refs[k][...], v_refs[k][...])

    flat = [two_d(a) for a in ws]
    outs = pl.pallas_call(
        body, name=name, out_shape=[jax.ShapeDtypeStruct(a.shape, F32) for a in flat] * 3,
        in_specs=[pl.BlockSpec(memory_space=pltpu.VMEM)] * (4 * n),
        out_specs=[pl.BlockSpec(memory_space=pltpu.VMEM)] * (3 * n),
        compiler_params=pltpu.CompilerParams(vmem_limit_bytes=VMEM_LIMIT),
    )(*flat, *[two_d(a) for a in gs], *[two_d(a) for a in ms], *[two_d(a) for a in vs])
    shaped = [o.reshape(ws[k % n].shape) for k, o in enumerate(outs)]
    return shaped[:n], shaped[n:2 * n], shaped[2 * n:]


def _adamw_nd(w, g, m, v, name):
    shape = w.shape
    two_d = (-1, shape[-1])
    d, nm, nv = _adamw(w.reshape(two_d), g.reshape(two_d), m.reshape(two_d), v.reshape(two_d), name)
    return d.reshape(shape), nm.reshape(shape), nv.reshape(shape)


def kernel(x, c, ada_w, ada_b, norm_ffn1_g, ffn1_w_gu, ffn1_w_down, norm_mix_g, mix_w_in, sgu_ln_g, sgu_ln_b, sgu_w_s, sgu_b, conv_w, out_norm_g, mix_w_out, norm_ffn2_g, ffn2_w_gu, ffn2_w_down, final_norm_g, loss_target, m_ada_w, m_ada_b, m_norm_ffn1_g, m_ffn1_w_gu, m_ffn1_w_down, m_norm_mix_g, m_mix_w_in, m_sgu_ln_g, m_sgu_ln_b, m_sgu_w_s, m_sgu_b, m_conv_w, m_out_norm_g, m_mix_w_out, m_norm_ffn2_g, m_ffn2_w_gu, m_ffn2_w_down, m_final_norm_g, v_ada_w, v_ada_b, v_norm_ffn1_g, v_ffn1_w_gu, v_ffn1_w_down, v_norm_mix_g, v_mix_w_in, v_sgu_ln_g, v_sgu_ln_b, v_sgu_w_s, v_sgu_b, v_conv_w, v_out_norm_g, v_mix_w_out, v_norm_ffn2_g, v_ffn2_w_gu, v_ffn2_w_down, v_final_norm_g):
    batch, seq, _ = x.shape
    tokens = batch * seq
    me = 4 * lax.axis_index("x") + 2 * lax.axis_index("y") + lax.axis_index("c")
    weights = dict(ada_w=ada_w, ada_b=ada_b, norm_ffn1_g=norm_ffn1_g, ffn1_w_gu=ffn1_w_gu, ffn1_w_down=ffn1_w_down,
                   norm_mix_g=norm_mix_g, mix_w_in=mix_w_in, sgu_ln_g=sgu_ln_g, sgu_ln_b=sgu_ln_b, sgu_w_s=sgu_w_s,
                   sgu_b=sgu_b, conv_w=conv_w, out_norm_g=out_norm_g, mix_w_out=mix_w_out, norm_ffn2_g=norm_ffn2_g,
                   ffn2_w_gu=ffn2_w_gu, ffn2_w_down=ffn2_w_down, final_norm_g=final_norm_g)
    mom1 = dict(ada_w=m_ada_w, ada_b=m_ada_b, norm_ffn1_g=m_norm_ffn1_g, ffn1_w_gu=m_ffn1_w_gu,
                ffn1_w_down=m_ffn1_w_down, norm_mix_g=m_norm_mix_g, mix_w_in=m_mix_w_in, sgu_ln_g=m_sgu_ln_g,
                sgu_ln_b=m_sgu_ln_b, sgu_w_s=m_sgu_w_s, sgu_b=m_sgu_b, conv_w=m_conv_w, out_norm_g=m_out_norm_g,
                mix_w_out=m_mix_w_out, norm_ffn2_g=m_norm_ffn2_g, ffn2_w_gu=m_ffn2_w_gu, ffn2_w_down=m_ffn2_w_down,
                final_norm_g=m_final_norm_g)
    mom2 = dict(ada_w=v_ada_w, ada_b=v_ada_b, norm_ffn1_g=v_norm_ffn1_g, ffn1_w_gu=v_ffn1_w_gu,
                ffn1_w_down=v_ffn1_w_down, norm_mix_g=v_norm_mix_g, mix_w_in=v_mix_w_in, sgu_ln_g=v_sgu_ln_g,
                sgu_ln_b=v_sgu_ln_b, sgu_w_s=v_sgu_w_s, sgu_b=v_sgu_b, conv_w=v_conv_w, out_norm_g=v_out_norm_g,
                mix_w_out=v_mix_w_out, norm_ffn2_g=v_norm_ffn2_g, ffn2_w_gu=v_ffn2_w_gu, ffn2_w_down=v_ffn2_w_down,
                final_norm_g=v_final_norm_g)

    big = ("ffn1_w_gu", "ffn1_w_down", "mix_w_in", "mix_w_out", "ffn2_w_gu", "ffn2_w_down")
    transposed = ("ffn1_w_gu", "mix_w_in", "ffn2_w_gu")
    as_rows = lambda nm, a: jnp.swapaxes(a, 1, 2) if nm in transposed else a
    shard = {(l, nm): as_rows(nm, weights[nm])[l].astype(BF16) for l in range(DEPTH) for nm in big}
    full_w = {}

    def gather_of(keys):
        return keys, _GatherRows([shard[k] for k in keys])

    def landed(plan, got):
        full_w.update(zip(plan[0], got))

    ada_cols = ada_w.shape[2]
    ada_b_cols = lax.dynamic_slice_in_dim(ada_b, me * ada_cols, ada_cols, axis=1).reshape(DEPTH, 1, ada_cols)
    plan = gather_of([(0, "ffn1_w_gu")])
    c_dev, convw_dev, ada_recv, got = _prologue(
        jnp.pad(c, ((0, 8 - batch), (0, 0))), jnp.pad(conv_w.reshape(-1), (0, 8 * LANES - conv_w.size)).reshape(8, LANES),
        ada_w, ada_b_cols, plan[1])
    landed(plan, got)
    c_all = c_dev[:, :batch].reshape(N_DEV * batch, D_MODEL)
    convw_all = convw_dev.reshape(N_DEV, -1)[:, :conv_w.size].reshape((N_DEV,) + conv_w.shape)
    convw_full = jnp.transpose(convw_all, (1, 2, 0, 3)).reshape(DEPTH, 3, D_A)
    ada_mine = jnp.transpose(ada_recv[:, :, :batch, :], (1, 2, 0, 3)).reshape(DEPTH, batch, N_MOD * D_MODEL)
    mod = ada_mine.reshape(DEPTH, batch, N_MOD, 1, D_MODEL)

    causal = jnp.tril(jnp.ones((CHUNK, CHUNK), F32))
    bd = jnp.kron(jnp.eye(N_HEADS, dtype=F32), jnp.full((HEAD_DIM, HEAD_DIM), 1.0 / HEAD_DIM, F32)).astype(BF16)
    row_vec = lambda a: a.reshape(1, -1)

    hosted_gathers = {
        (0, "ffn1"): [(0, "ffn1_w_down"), (0, "mix_w_in"), (0, "mix_w_out")],
        (0, "ffn_down1"): [(0, "ffn2_w_gu")],
        (0, "mix_in"): [(0, "ffn2_w_down")],
        (0, "ffn2"): [(1, "ffn1_w_gu"), (1, "ffn1_w_down"), (1, "mix_w_in"), (1, "mix_w_out")],
        (1, "ffn1"): [(1, "ffn2_w_gu"), (1, "ffn2_w_down")],
    }

    def hosting(l, site):
        keys = hosted_gathers.get((l, site))
        return gather_of(keys) if keys else (None, None)

    xs = x.reshape(tokens, D_MODEL)
    saved = []
    for l in range(DEPTH):
        sh1, sc1, g1, sh2, sc2, g2, sh3, sc3, g3 = [mod[l, :, k] for k in range(N_MOD)]
        mixer_consts = dict(
            wm=(sgu_w_s[l] * causal[None]).astype(BF16),
            bias_full=jnp.repeat(sgu_b[l].T, HEAD_DIM, axis=1),
            lng=row_vec(jnp.tile(sgu_ln_g[l], N_HEADS)), lnb=row_vec(jnp.tile(sgu_ln_b[l], N_HEADS)),
            convw=jnp.pad(convw_full[l], ((0, 5), (0, 0))), og=row_vec(out_norm_g[l]), bd=bd)
        x0 = xs
        plan = hosting(l, "ffn1")
        if l == 0:
            h1, a1, s1, w1, got = _normmod_matmul(x0, row_vec(norm_ffn1_g[l]), 1.0 + sc1, sh1, full_w[l, "ffn1_w_gu"], seq, "ffn_up", True, plan[1])
            landed(plan, got)
            plan = hosting(l, "ffn_down1")
            x1, f1, got = _matmul_residual(a1, full_w[l, "ffn1_w_down"], x0, g1, 0.5, seq, "ffn_down", plan[1])
        else:
            h1, a1, s1, w1, x1, f1, got = _ffn_forward(
                x0, row_vec(norm_ffn1_g[l]), 1.0 + sc1, sh1, full_w[l, "ffn1_w_gu"], full_w[l, "ffn1_w_down"], g1, 0.5, seq, "ffn_fwd",
                comm=plan[1])
        if got:
            landed(plan, got)
        plan = hosting(l, "mix_in")
        h2, proj, ymix, x2, o2, got = _mixer_forward(
            x1, row_vec(norm_mix_g[l]), 1.0 + sc2, sh2, full_w[l, "mix_w_in"], g2, full_w[l, "mix_w_out"], seq=seq,
            name="mixer_forward", comm=plan[1], **mixer_consts)
        if got:
            landed(plan, got)
        plan = hosting(l, "ffn2")
        if l + 1 < DEPTH:
            h3, a3, s3, w3, x3, f3, got = _ffn_forward(
                x2, row_vec(norm_ffn2_g[l]), 1.0 + sc3, sh3, full_w[l, "ffn2_w_gu"], full_w[l, "ffn2_w_down"], g3, 0.5, seq, "ffn_fwd",
                comm=plan[1])
        else:
            head = (loss_target.reshape(tokens, D_MODEL), row_vec(final_norm_g))
            h3, a3, s3, w3, x3, f3, d_final_g, loss_cols, got = _ffn_forward(
                x2, row_vec(norm_ffn2_g[l]), 1.0 + sc3, sh3, full_w[l, "ffn2_w_gu"], full_w[l, "ffn2_w_down"], g3, 0.5, seq, "ffn_fwd_loss",
                loss_head=head, comm=plan[1])
        if got:
            landed(plan, got)
        saved.append(dict(x0=x0, x1=x1, x2=x2, h1=h1, h2=h2, h3=h3, a1=a1, s1=s1, w1=w1, a3=a3, s3=s3, w3=w3, f1=f1, f3=f3, o2=o2, proj=proj,
                          ymix=ymix, mixer_consts=mixer_consts, sc=(1.0 + sc1, 1.0 + sc2, 1.0 + sc3), gates=(g1, g2, g3)))
        xs = x3

    dx = xs

    recv = {}
    small_grads = [None] * DEPTH
    d_mod = [None] * DEPTH

    mix_names = ("out_norm_g", "sgu_ln_g", "sgu_ln_b", "sgu_w_s", "sgu_b", "conv_w")
    late_names = ("norm_ffn1_g", "norm_mix_g", "norm_ffn2_g")

    def mix_parts(l):
        return [small_grads[l][nm] for nm in mix_names]

    def late_parts(l):
        return [small_grads[l][nm] for nm in late_names] + [d_mod[l]]

    pending = []

    def scatter_later(l, nm, grad):
        pending.append(((l, nm), _ScatterRows([grad])))

    def host():
        keys, parts = [k for k, _ in pending], [p for _, p in pending]
        pending.clear()
        return keys, (_Exchanges(parts) if parts else None)

    def hosted(keys, got):
        if got:
            recv.update(zip(keys, got))

    for l in reversed(range(DEPTH)):
        sv = saved[l]
        mc = sv["mixer_consts"]
        if l + 1 < DEPTH:
            pending.append((("late", l + 1), _GatherRows([_pack_small(late_parts(l + 1))])))
        keys, comm = host()
        df3, dg3, dgu3, dx2, dsh3, dsc3, dn3, got = _ffn_backward(
            dx, sv["gates"][2], sv["f3"], sv["s3"], sv["w3"], full_w[l, "ffn2_w_down"], full_w[l, "ffn2_w_gu"], sv["x2"],
            row_vec(norm_ffn2_g[l]), sv["sc"][2], 0.5, seq, "ffn_bwd", comm)
        hosted(keys, got)
        gw_down2, _ = _weight_grad(sv["a3"], df3, seq, "grad_w_down")
        scatter_later(l, "ffn2_w_down", gw_down2)
        keys, comm = host()
        gw_gu2, got = _weight_grad(dgu3, sv["h3"], seq, "grad_w_gu", comm)
        hosted(keys, got)
        scatter_later(l, "ffn2_w_gu", gw_gu2)
        keys, comm = host()
        do2, dg2, dproj, dx1, dsh2, dsc2, dn2, d_og, d_cw, d_lng, d_lnb, d_bias, d_wm, got = _mixer_backward(
            sv["proj"], dx2, sv["gates"][1], sv["o2"], full_w[l, "mix_w_out"], sv["x1"], row_vec(norm_mix_g[l]), sv["sc"][1],
            full_w[l, "mix_w_in"], causal=causal, seq=seq, name="mixer_backward", comm=comm, **mc)
        hosted(keys, got)
        small_grads[l] = dict(
            out_norm_g=d_og, sgu_ln_g=d_lng.reshape(N_HEADS, HEAD_DIM).sum(0), sgu_ln_b=d_lnb.reshape(N_HEADS, HEAD_DIM).sum(0),
            sgu_w_s=d_wm, sgu_b=d_bias.reshape(CHUNK, N_HEADS, HEAD_DIM).sum(-1).T, conv_w=d_cw[0:3])
        gw_out, _ = _weight_grad(sv["ymix"], do2, seq, "grad_w_out")
        scatter_later(l, "mix_w_out", gw_out)
        keys, comm = host()
        gw_in, got = _weight_grad(dproj, sv["h2"], seq, "grad_w_in", comm)
        hosted(keys, got)
        scatter_later(l, "mix_w_in", gw_in)
        pending.append((("mix", l), _GatherRows([_pack_small(mix_parts(l))])))
        keys, comm = host()
        if l > 0:
            df1, dg1, dgu1, dx0, dsh1, dsc1, dn1, got = _ffn_backward(
                dx1, sv["gates"][0], sv["f1"], sv["s1"], sv["w1"], full_w[l, "ffn1_w_down"], full_w[l, "ffn1_w_gu"], sv["x0"],
                row_vec(norm_ffn1_g[l]), sv["sc"][0], 0.5, seq, "ffn_bwd", comm)
        else:
            df1, dg1, dgu1, got = _residual_backward(dx1, sv["gates"][0], sv["f1"], full_w[l, "ffn1_w_down"], 0.5, sv["s1"], sv["w1"], seq, "ffn_down_bwd", comm)
        hosted(keys, got)
        gw_down1, _ = _weight_grad(sv["a1"], df1, seq, "grad_w_down")
        scatter_later(l, "ffn1_w_down", gw_down1)
        keys, comm = host()
        gw_gu1, got = _weight_grad(dgu1, sv["h1"], seq, "grad_w_gu", comm)
        hosted(keys, got)
        scatter_later(l, "ffn1_w_gu", gw_gu1)
        if l == 0:
            keys, comm = host()
            dx0, dsh1, dsc1, dn1, got = _matmul_normmod_backward(dgu1, full_w[l, "ffn1_w_gu"], sv["x0"], dx1, row_vec(norm_ffn1_g[l]), sv["sc"][0], seq, "ffn_up_bwd", comm)
            hosted(keys, got)
        dx = dx0
        small_grads[l].update(norm_ffn1_g=dn1, norm_mix_g=dn2, norm_ffn2_g=dn3)
        d_mod[l] = jnp.concatenate([dsh1, dsc1, dg1, dsh2, dsc2, dg2, dsh3, dsc3, dg3], axis=1)
    grad_x = dx.reshape(batch, seq, D_MODEL)

    grad_big, delta, new_m, new_v = {}, {}, {}, {}
    for nm in big:
        results = _adamw_rows([recv[l, nm] for l in range(DEPTH)], as_rows(nm, weights[nm]), as_rows(nm, mom1[nm]),
                              as_rows(nm, mom2[nm]), "adamw_" + nm)
        grad_big[nm], delta[nm], new_m[nm], new_v[nm] = [as_rows(nm, r) for r in results]

    last_parts = late_parts(0) + [d_final_g, loss_cols]
    last_shapes = [p.shape for p in last_parts]
    packed_all, packed_sum = _all_gather_small(_pack_small(last_parts), "reduce_small")
    late_sum = {0: _unpack_small(packed_sum, last_shapes)}
    d_mod_dev = {0: _unpack_small(packed_all, last_shapes, lead=(N_DEV,))[len(late_names)]}
    mix_sum = {}
    for l in range(DEPTH):
        gathered = recv["mix", l].reshape(N_DEV, -1, LANES)
        mix_sum[l] = _unpack_small(_sum_gathered(gathered, "sum_mix"), [p.shape for p in mix_parts(l)])
        if l > 0:
            shapes_l = [p.shape for p in late_parts(l)]
            gathered = recv["late", l].reshape(N_DEV, -1, LANES)
            late_sum[l] = _unpack_small(_sum_gathered(gathered, "sum_late"), shapes_l)
            d_mod_dev[l] = _unpack_small(gathered, shapes_l, lead=(N_DEV,))[len(late_names)]
    grad_small = {}
    for group, names in ((mix_sum, mix_names), (late_sum, late_names)):
        for k, nm in enumerate(names):
            grad_small[nm] = jnp.stack([group[l][k] for l in range(DEPTH)]).reshape(
                (DEPTH, 3, D_A) if nm == "conv_w" else weights[nm].shape)
    grad_small["conv_w"] = lax.dynamic_slice_in_dim(grad_small["conv_w"], me * conv_w.shape[2], conv_w.shape[2], axis=2)
    grad_small["final_norm_g"] = late_sum[0][len(late_names) + 1].reshape(final_norm_g.shape)
    loss = jnp.sum(late_sum[0][len(late_names) + 2])
    d_ada_all = jnp.stack([d_mod_dev[l] for l in range(DEPTH)]).reshape(DEPTH, N_DEV * batch, N_MOD * D_MODEL)
    d_ada_cols = lax.dynamic_slice_in_dim(d_ada_all, me * ada_cols, ada_cols, axis=2)
    g_ada_w, g_ada_b = _ada_backward(c_all, d_ada_cols, d_ada_all)

    grads = dict(grad_big)
    grads.update(grad_small)
    grads["ada_w"] = g_ada_w
    grads["ada_b"] = g_ada_b.reshape(ada_b.shape)

    names = ("ada_w", "ada_b", "norm_ffn1_g", "ffn1_w_gu", "ffn1_w_down", "norm_mix_g", "mix_w_in", "sgu_ln_g",
             "sgu_ln_b", "sgu_w_s", "sgu_b", "conv_w", "out_norm_g", "mix_w_out", "norm_ffn2_g", "ffn2_w_gu",
             "ffn2_w_down", "final_norm_g")
    delta["ada_w"], new_m["ada_w"], new_v["ada_w"] = _adamw_nd(ada_w, grads["ada_w"], m_ada_w, v_ada_w, "adamw_ada_w")
    rest = [nm for nm in names if nm not in big and nm != "ada_w"]
    pick = lambda src: [src[nm] for nm in rest]
    for nm, d_k, m_k, v_k in zip(rest, *_adamw_many(pick(weights), pick(grads), pick(mom1), pick(mom2), "adamw_small")):
        delta[nm], new_m[nm], new_v[nm] = d_k, m_k, v_k

    return (loss, grad_x, *[grads[nm] for nm in names], *[delta[nm] for nm in names],
            *[new_m[nm] for nm in names], *[new_v[nm] for nm in names])
```

```python
import math

import jax
import jax.numpy as jnp
from jax import lax
from jax.experimental import pallas as pl
from jax.experimental.pallas import tpu as pltpu

F32 = jnp.float32
BF16 = jnp.bfloat16

D_MODEL = 1024
D_A = 512
D_PROJ = 2560
N_HEADS = 8
HEAD_DIM = 64
CHUNK = 128
N_MOD = 9
DEPTH = 2
EPS = 1e-6
N_DEV = 8
LANES = 128
MXU_N = 256
HALO = 16
VMEM_LIMIT = 56 * 1024 * 1024
FORWARD_STEPS = 4

ADAM_LR = 0.001
ADAM_B1 = 0.9
ADAM_B2 = 0.999
ADAM_EPS = 1e-08
ADAM_WD = 0.01
ADAM_STEP = 10

MESH = pl.DeviceIdType.MESH


def _dot(a, b):
    return jnp.dot(a, b, preferred_element_type=F32)


def _dot_nt(a, b):
    return lax.dot_general(a, b, (((1,), (1,)), ((), ())), preferred_element_type=F32)


def _dot_tn(a, b):
    return lax.dot_general(a, b, (((0,), (0,)), ((), ())), preferred_element_type=F32)


def _sigmoid(x):
    return 0.5 * jnp.tanh(0.5 * x) + 0.5


def _gelu(x):
    return 0.5 * x * (1.0 + lax.erf(x * (1.0 / math.sqrt(2.0))))


def _gelu_grad(x):
    cdf = 0.5 * (1.0 + lax.erf(x * (1.0 / math.sqrt(2.0))))
    return cdf + x * jnp.exp(-0.5 * x * x) * (1.0 / math.sqrt(2.0 * math.pi))


def _params(n_axes=1, parallel=False):
    sem = ("parallel" if parallel else "arbitrary",) * n_axes
    return pltpu.CompilerParams(dimension_semantics=sem, vmem_limit_bytes=VMEM_LIMIT)


def _resident(shape):
    nd = len(shape)
    return pl.BlockSpec(shape, lambda *_: (0,) * nd, pipeline_mode=pl.Buffered(1))


def _tile_rows(seq):
    return min(512, seq)


def _my_position():
    x, y, c = lax.axis_index("x"), lax.axis_index("y"), lax.axis_index("c")
    return x, y, c, 4 * x + 2 * y + c


def _peer(x, y, c, p):
    return (x ^ ((p >> 2) & 1), y ^ ((p >> 1) & 1), c ^ (p & 1))


class _GatherRows:
    def __init__(self, shards):
        self.operands = list(shards)
        n = len(shards)
        self.out_shape = [jax.ShapeDtypeStruct((N_DEV * s.shape[0], s.shape[1]), s.dtype) for s in shards]
        self.scratch = [pltpu.SemaphoreType.DMA((n, N_DEV - 1)), pltpu.SemaphoreType.DMA((n, N_DEV - 1)),
                        pltpu.SemaphoreType.DMA((n,))]

    def _plan(self, src, dst, send, recv, loc):
        x, y, c, _ = _my_position()
        me, sib = (x, y, c), (x, y, 1 - c)
        chips = [(1 - x, y), (x, 1 - y), (1 - x, 1 - y)]
        plans = []
        for k, shard in enumerate(self.operands):
            rows = shard.shape[0]

            def blk(pos, k=k, rows=rows):
                return dst[k].at[pl.ds((4 * pos[0] + 2 * pos[1] + pos[2]) * rows, rows), :]

            def rc(s, block, to, source=None, k=k, blk=blk):
                return pltpu.make_async_remote_copy(
                    src_ref=blk(block) if source is None else source, dst_ref=blk(block),
                    send_sem=send.at[k, s], recv_sem=recv.at[k, s], device_id=to, device_id_type=MESH)

            plans.append(dict(
                local=pltpu.make_async_copy(src[k], blk(me), loc.at[k]),
                first=[rc(0, me, sib, src[k])] + [rc(1 + j, me, (*chip, c), src[k]) for j, chip in enumerate(chips)],
                landed=[rc(1 + j, (*chip, c), me) for j, chip in enumerate(chips)],
                passed=[rc(4 + j, (*chip, c), sib) for j, chip in enumerate(chips)],
                from_sib=[rc(0, sib, me)] + [rc(4 + j, (*chip, 1 - c), me) for j, chip in enumerate(chips)]))
        return plans

    def start(self, src, dst, send, recv, loc):
        for plan in self._plan(src, dst, send, recv, loc):
            plan["local"].start()
            for cp in plan["first"]:
                cp.start()

    def forward(self, src, dst, send, recv, loc):
        for plan in self._plan(src, dst, send, recv, loc):
            for landed, passed in zip(plan["landed"], plan["passed"]):
                landed.wait_recv()
                passed.start()

    def finish(self, src, dst, send, recv, loc):
        for plan in self._plan(src, dst, send, recv, loc):
            for cp in plan["from_sib"]:
                cp.wait_recv()
            for cp in plan["first"] + plan["passed"]:
                cp.wait_send()
            plan["local"].wait()


class _ScatterRows:
    def __init__(self, grads):
        self.operands = list(grads)
        n = len(grads)
        self.out_shape = [jax.ShapeDtypeStruct((N_DEV, g.shape[0] // N_DEV, g.shape[1]), g.dtype) for g in grads]
        self.scratch = [pltpu.SemaphoreType.DMA((n, N_DEV - 1)), pltpu.SemaphoreType.DMA((n, N_DEV - 1)),
                        pltpu.SemaphoreType.DMA((n,))]

    def _plan(self, src, dst, send, recv, loc):
        x, y, c, me = _my_position()
        copies = []
        for k, grad in enumerate(self.operands):
            rows = grad.shape[0] // N_DEV
            copies.append(pltpu.make_async_copy(src[k].at[pl.ds(me * rows, rows), :], dst[k].at[me], loc.at[k]))
            for p in range(1, N_DEV):
                px, py, pc = _peer(x, y, c, p)
                copies.append(pltpu.make_async_remote_copy(
                    src_ref=src[k].at[pl.ds((4 * px + 2 * py + pc) * rows, rows), :], dst_ref=dst[k].at[me],
                    send_sem=send.at[k, p - 1], recv_sem=recv.at[k, p - 1], device_id=(px, py, pc), device_id_type=MESH))
        return copies

    def start(self, src, dst, send, recv, loc):
        for cp in self._plan(src, dst, send, recv, loc):
            cp.start()

    def forward(self, src, dst, send, recv, loc):
        pass

    def finish(self, src, dst, send, recv, loc):
        for cp in self._plan(src, dst, send, recv, loc):
            cp.wait()


class _Exchanges:
    def __init__(self, parts):
        self.parts = list(parts)
        self.operands = [op for part in self.parts for op in part.operands]
        self.out_shape = [shp for part in self.parts for shp in part.out_shape]
        self.scratch = [scr for part in self.parts for scr in part.scratch]

    def _each(self, src, dst, sems):
        at, sem_at = 0, 0
        for part in self.parts:
            n, n_sem = len(part.operands), len(part.scratch)
            yield part, src[at:at + n], dst[at:at + n], sems[sem_at:sem_at + n_sem]
            at, sem_at = at + n, sem_at + n_sem

    def start(self, src, dst, *sems):
        for part, part_src, part_dst, part_sems in self._each(src, dst, sems):
            part.start(part_src, part_dst, *part_sems)

    def forward(self, src, dst, *sems):
        for part, part_src, part_dst, part_sems in self._each(src, dst, sems):
            part.forward(part_src, part_dst, *part_sems)

    def finish(self, src, dst, *sems):
        for part, part_src, part_dst, part_sems in self._each(src, dst, sems):
            part.finish(part_src, part_dst, *part_sems)


_ANY = pl.BlockSpec(memory_space=pl.ANY)


def _call(body, *, name, grid, in_specs, out_specs, out_shape, operands, scratch_shapes=(), parallel=False, comm=None):
    n_axes = len(grid)
    if comm is None:
        outs = pl.pallas_call(body, name=name, grid=grid, out_shape=list(out_shape), in_specs=list(in_specs),
                              out_specs=list(out_specs), scratch_shapes=list(scratch_shapes),
                              compiler_params=_params(n_axes, parallel))(*operands)
        return list(outs), None
    n_in, n_out, n_scr, n_c = len(in_specs), len(out_specs), len(scratch_shapes), len(comm.operands)
    total = math.prod(grid)

    def hosted(*refs):
        ins, c_src = refs[:n_in], refs[n_in:n_in + n_c]
        outs, c_dst = refs[n_in + n_c:n_in + n_c + n_out], refs[n_in + n_c + n_out:n_in + 2 * n_c + n_out]
        scr, sems = refs[n_in + 2 * n_c + n_out:n_in + 2 * n_c + n_out + n_scr], refs[n_in + 2 * n_c + n_out + n_scr:]
        step = pl.program_id(0)
        for axis in range(1, n_axes):
            step = step * grid[axis] + pl.program_id(axis)

        @pl.when(step == 0)
        def _():
            comm.start(c_src, c_dst, *sems)

        @pl.when(step == max(total - FORWARD_STEPS, 0))
        def _():
            comm.forward(c_src, c_dst, *sems)

        body(*ins, *outs, *scr)

        @pl.when(step == total - 1)
        def _():
            comm.finish(c_src, c_dst, *sems)

    res = pl.pallas_call(hosted, name=name, grid=grid, out_shape=list(out_shape) + comm.out_shape,
                         in_specs=list(in_specs) + [_ANY] * n_c, out_specs=list(out_specs) + [_ANY] * n_c,
                         scratch_shapes=list(scratch_shapes) + comm.scratch,
                         compiler_params=_params(n_axes, False))(*operands, *comm.operands)
    return list(res[:n_out]), list(res[n_out:])


def _all_gather_small(v, name):
    rows = v.shape[0]

    def body(v_ref, all_ref, sum_ref, send_sems, recv_sems):
        x, y, c, me = _my_position()
        all_ref[me] = v_ref[...]
        copies = []
        for p in range(1, N_DEV):
            cp = pltpu.make_async_remote_copy(
                src_ref=v_ref, dst_ref=all_ref.at[me], send_sem=send_sems.at[p - 1], recv_sem=recv_sems.at[p - 1],
                device_id=_peer(x, y, c, p), device_id_type=MESH)
            cp.start()
            copies.append(cp)
        for cp in copies:
            cp.wait()
        acc = all_ref[0]
        for d in range(1, N_DEV):
            acc = acc + all_ref[d]
        sum_ref[...] = acc

    return pl.pallas_call(
        body, name=name,
        out_shape=[jax.ShapeDtypeStruct((N_DEV, rows, LANES), F32), jax.ShapeDtypeStruct((rows, LANES), F32)],
        in_specs=[pl.BlockSpec(memory_space=pltpu.VMEM)],
        out_specs=[pl.BlockSpec(memory_space=pltpu.VMEM)] * 2,
        scratch_shapes=[pltpu.SemaphoreType.DMA((N_DEV - 1,)), pltpu.SemaphoreType.DMA((N_DEV - 1,))],
        compiler_params=pltpu.CompilerParams(vmem_limit_bytes=VMEM_LIMIT),
    )(v)


def _sum_gathered(gathered, name):
    rows = gathered.shape[1]

    def body(g_ref, o_ref):
        acc = g_ref[0]
        for d in range(1, N_DEV):
            acc = acc + g_ref[d]
        o_ref[...] = acc

    return pl.pallas_call(
        body, name=name, out_shape=jax.ShapeDtypeStruct((rows, LANES), F32),
        in_specs=[pl.BlockSpec(memory_space=pltpu.VMEM)], out_specs=pl.BlockSpec(memory_space=pltpu.VMEM),
        compiler_params=pltpu.CompilerParams(vmem_limit_bytes=VMEM_LIMIT),
    )(gathered)


def _pack_small(parts):
    flat = jnp.concatenate([p.reshape(-1).astype(F32) for p in parts])
    total = flat.shape[0]
    padded = -(-total // (8 * LANES)) * (8 * LANES)
    flat = jnp.pad(flat, (0, padded - total))
    return flat.reshape(padded // LANES, LANES)


def _unpack_small(packed, shapes, lead=()):
    flat = packed.reshape(lead + (-1,))
    out, off = [], 0
    for shp in shapes:
        size = math.prod(shp)
        out.append(flat[..., off:off + size].reshape(lead + tuple(shp)))
        off += size
    return out


def _prologue(c_rows, convw_rows, ada_w, ada_b_cols, gather):
    depth, _, cols = ada_w.shape
    n_c = len(gather.operands)
    sub = 8

    def body(c_ref, cw_ref, b_ref, w_hbm, *rest):
        g_src, (c_all_ref, cw_all_ref, ada_ref), g_dst = rest[:n_c], rest[n_c:n_c + 3], rest[n_c + 3:2 * n_c + 3]
        ada_local, w_ref, w_sem, send_sems, recv_sems = rest[2 * n_c + 3:2 * n_c + 8]
        g_sems = rest[2 * n_c + 8:]
        x, y, c, me = _my_position()
        gather.start(g_src, g_dst, *g_sems)
        load_w = pltpu.make_async_copy(w_hbm, w_ref, w_sem)
        load_w.start()

        def to_all(k, src_ref, dst_ref):
            copies = []
            for p in range(1, N_DEV):
                copies.append(pltpu.make_async_remote_copy(
                    src_ref=src_ref, dst_ref=dst_ref.at[me], send_sem=send_sems.at[k, p - 1], recv_sem=recv_sems.at[k, p - 1],
                    device_id=_peer(x, y, c, p), device_id_type=MESH))
            return copies

        first = to_all(0, c_ref, c_all_ref) + to_all(1, cw_ref, cw_all_ref)
        c_all_ref[me] = c_ref[...]
        cw_all_ref[me] = cw_ref[...]
        for cp in first:
            cp.start()
        for cp in first:
            cp.wait()
        cv = c_all_ref[...].reshape(N_DEV * sub, D_MODEL)
        act = (cv * _sigmoid(cv)).astype(BF16)
        load_w.wait()
        for l in range(depth):
            ada_local[l] = _dot(act, w_ref[l].astype(BF16)) + b_ref[l]
        ada_ref[me] = ada_local[:, pl.ds(pl.multiple_of(me * sub, sub), sub), :]
        rows_out = []
        for p in range(1, N_DEV):
            px, py, pc = _peer(x, y, c, p)
            rows = pl.ds(pl.multiple_of((4 * px + 2 * py + pc) * sub, sub), sub)
            rows_out.append(pltpu.make_async_remote_copy(
                src_ref=ada_local.at[:, rows, :], dst_ref=ada_ref.at[me], send_sem=send_sems.at[2, p - 1],
                recv_sem=recv_sems.at[2, p - 1], device_id=(px, py, pc), device_id_type=MESH))
        for cp in rows_out:
            cp.start()
        for cp in rows_out:
            cp.wait()
        gather.forward(g_src, g_dst, *g_sems)
        gather.finish(g_src, g_dst, *g_sems)

    vmem = pl.BlockSpec(memory_space=pltpu.VMEM)
    outs = pl.pallas_call(
        body, name="prologue",
        out_shape=[jax.ShapeDtypeStruct((N_DEV, sub, D_MODEL), F32), jax.ShapeDtypeStruct((N_DEV, sub, LANES), F32),
                   jax.ShapeDtypeStruct((N_DEV, depth, sub, cols), F32)] + gather.out_shape,
        in_specs=[vmem] * 3 + [_ANY] * (1 + n_c), out_specs=[vmem] * 3 + [_ANY] * n_c,
        scratch_shapes=[pltpu.VMEM((depth, N_DEV * sub, cols), F32), pltpu.VMEM(ada_w.shape, F32), pltpu.SemaphoreType.DMA,
                        pltpu.SemaphoreType.DMA((3, N_DEV - 1)), pltpu.SemaphoreType.DMA((3, N_DEV - 1))] + gather.scratch,
        compiler_params=pltpu.CompilerParams(vmem_limit_bytes=VMEM_LIMIT),
    )(c_rows, convw_rows, ada_b_cols, ada_w, *gather.operands)
    return outs[0], outs[1], outs[2], list(outs[3:])


def _ada_backward(c_all, d_ada_cols, d_ada_all):
    nb = c_all.shape[0]
    cols = d_ada_cols.shape[2]
    full = d_ada_all.shape[2]

    def body(c_ref, dc_ref, da_ref, gw_ref, gb_ref):
        cv = c_ref[...]
        act = (cv * _sigmoid(cv)).astype(BF16)
        gw_ref[0] = _dot_tn(act, dc_ref[0].astype(BF16))
        gb_ref[0] = jnp.sum(da_ref[0], axis=0, keepdims=True)

    return pl.pallas_call(
        body, name="ada_backward", grid=(DEPTH,),
        out_shape=[jax.ShapeDtypeStruct((DEPTH, D_MODEL, cols), F32), jax.ShapeDtypeStruct((DEPTH, 1, full), F32)],
        in_specs=[pl.BlockSpec((nb, D_MODEL), lambda l: (0, 0)),
                  pl.BlockSpec((1, nb, cols), lambda l: (l, 0, 0)),
                  pl.BlockSpec((1, nb, full), lambda l: (l, 0, 0))],
        out_specs=[pl.BlockSpec((1, D_MODEL, cols), lambda l: (l, 0, 0)),
                   pl.BlockSpec((1, 1, full), lambda l: (l, 0, 0))],
        compiler_params=_params(),
    )(c_all, d_ada_cols, d_ada_all)


def _rms(xv):
    return lax.rsqrt(jnp.mean(xv * xv, axis=-1, keepdims=True) + EPS)


def _normmod_matmul(x, gnorm, scale1p, shift, w_t, seq, name, comm=None):
    tokens, width = x.shape[0], w_t.shape[0] // 2
    tm = _tile_rows(seq)
    per_seq = seq // tm
    n_chunks = width // MXU_N

    def body(x_ref, g_ref, sc_ref, sh_ref, w_ref, h_ref, act_ref, silu_ref, dact_ref):
        xv = x_ref[...]
        h = (xv * _rms(xv) * g_ref[...]) * sc_ref[0] + sh_ref[0]
        h_ref[...] = h.astype(BF16)
        for ck in range(n_chunks):
            cs = slice(ck * MXU_N, (ck + 1) * MXU_N)
            g = _dot_nt(h_ref[...], w_ref[cs, :])
            u = _dot_nt(h_ref[...], w_ref[width + ck * MXU_N:width + (ck + 1) * MXU_N, :])
            sig = _sigmoid(g)
            silu = g * sig
            act_ref[:, cs] = (silu * u).astype(BF16)
            silu_ref[:, cs] = silu.astype(BF16)
            dact_ref[:, cs] = (u * (sig + silu * (1.0 - sig))).astype(BF16)

    per_batch = pl.BlockSpec((1, 1, D_MODEL), lambda i: (i // per_seq, 0, 0))
    outs, got = _call(
        body, name=name, grid=(tokens // tm,),
        out_shape=[jax.ShapeDtypeStruct((tokens, D_MODEL), BF16)] + [jax.ShapeDtypeStruct((tokens, width), BF16)] * 3,
        in_specs=[pl.BlockSpec((tm, D_MODEL), lambda i: (i, 0)), _resident((1, D_MODEL)), per_batch, per_batch,
                  _resident(w_t.shape)],
        out_specs=[pl.BlockSpec((tm, D_MODEL), lambda i: (i, 0))] + [pl.BlockSpec((tm, width), lambda i: (i, 0))] * 3,
        operands=(x, gnorm, scale1p, shift, w_t), parallel=True, comm=comm)
    return (*outs, got)


def _matmul_residual(src, w, x, gate, scale, seq, name, comm=None):
    tokens, k_dim = x.shape[0], w.shape[0]
    tm = _tile_rows(seq)
    per_seq = seq // tm

    def body(s_ref, w_ref, x_ref, gate_ref, xo_ref, f_ref):
        f = _dot(s_ref[...], w_ref[...])
        f_ref[...] = f.astype(BF16)
        xo_ref[...] = x_ref[...] + (scale * gate_ref[0]) * f

    (x_out, f), got = _call(
        body, name=name, grid=(tokens // tm,),
        out_shape=[jax.ShapeDtypeStruct((tokens, D_MODEL), F32), jax.ShapeDtypeStruct((tokens, D_MODEL), BF16)],
        in_specs=[pl.BlockSpec((tm, k_dim), lambda i: (i, 0)), _resident(w.shape),
                  pl.BlockSpec((tm, D_MODEL), lambda i: (i, 0)),
                  pl.BlockSpec((1, 1, D_MODEL), lambda i: (i // per_seq, 0, 0))],
        out_specs=[pl.BlockSpec((tm, D_MODEL), lambda i: (i, 0))] * 2,
        operands=(src, w, x, gate), parallel=True, comm=comm)
    return x_out, f, got


def _loss_tile(xv, target, gn):
    r = _rms(xv)
    xn = xv * r
    err = xn * gn - target
    loss = (0.5 / D_MODEL) * jnp.sum(err * err, axis=0, keepdims=True)
    dyv = err * (1.0 / D_MODEL)
    dg = jnp.sum(dyv * xn, axis=0, keepdims=True)
    dxn = dyv * gn
    dx = r * (dxn - xn * jnp.mean(dxn * xn, axis=-1, keepdims=True))
    return loss, dx, dg


def _ffn_forward(x, gnorm, scale1p, shift, w_gu_t, w_down, gate, scale, seq, name, loss_head=None, comm=None):
    tokens, width = x.shape[0], w_down.shape[0]
    tm = _tile_rows(seq)
    per_seq = seq // tm
    n_chunks = width // MXU_N

    def body(x_ref, g_ref, sc_ref, sh_ref, wgu_ref, wd_ref, gate_ref, *rest):
        if loss_head:
            t_ref, gf_ref, h_ref, act_ref, silu_ref, dact_ref, xo_ref, f_ref, dgf_ref, loss_ref = rest
        else:
            h_ref, act_ref, silu_ref, dact_ref, xo_ref, f_ref = rest
        xv = x_ref[...]
        h = (xv * _rms(xv) * g_ref[...]) * sc_ref[0] + sh_ref[0]
        h_ref[...] = h.astype(BF16)
        for ck in range(n_chunks):
            cs = slice(ck * MXU_N, (ck + 1) * MXU_N)
            g = _dot_nt(h_ref[...], wgu_ref[cs, :])
            u = _dot_nt(h_ref[...], wgu_ref[width + ck * MXU_N:width + (ck + 1) * MXU_N, :])
            sig = _sigmoid(g)
            silu = g * sig
            act_ref[:, cs] = (silu * u).astype(BF16)
            silu_ref[:, cs] = silu.astype(BF16)
            dact_ref[:, cs] = (u * (sig + silu * (1.0 - sig))).astype(BF16)
        f = _dot(act_ref[...], wd_ref[...])
        f_ref[...] = f.astype(BF16)
        x_out = xv + (scale * gate_ref[0]) * f
        if loss_head:
            i = pl.program_id(0)
            loss, dx, dg = _loss_tile(x_out, t_ref[...], gf_ref[...])
            xo_ref[...] = dx

            @pl.when(i == 0)
            def _():
                dgf_ref[...] = dg
                loss_ref[...] = loss

            @pl.when(i != 0)
            def _():
                dgf_ref[...] = dgf_ref[...] + dg
                loss_ref[...] = loss_ref[...] + loss
        else:
            xo_ref[...] = x_out

    row = lambda i: (i, 0)
    per_batch = pl.BlockSpec((1, 1, D_MODEL), lambda i: (i // per_seq, 0, 0))
    tile = lambda cols: pl.BlockSpec((tm, cols), row)
    wide = jax.ShapeDtypeStruct((tokens, width), BF16)
    fixed = pl.BlockSpec((1, D_MODEL), lambda i: (0, 0))
    vec = jax.ShapeDtypeStruct((1, D_MODEL), F32)
    outs, got = _call(
        body, name=name, grid=(tokens // tm,),
        out_shape=[jax.ShapeDtypeStruct((tokens, D_MODEL), BF16), wide, wide, wide,
                   jax.ShapeDtypeStruct((tokens, D_MODEL), F32), jax.ShapeDtypeStruct((tokens, D_MODEL), BF16)]
        + ([vec, vec] if loss_head else []),
        in_specs=[tile(D_MODEL), _resident((1, D_MODEL)), per_batch, per_batch, _resident(w_gu_t.shape),
                  _resident(w_down.shape), per_batch] + ([tile(D_MODEL), _resident((1, D_MODEL))] if loss_head else []),
        out_specs=[tile(D_MODEL), tile(width), tile(width), tile(width), tile(D_MODEL), tile(D_MODEL)]
        + ([fixed, fixed] if loss_head else []),
        operands=(x, gnorm, scale1p, shift, w_gu_t, w_down, gate) + (tuple(loss_head) if loss_head else ()),
        parallel=not loss_head, comm=comm)
    return (*outs, got)


def _residual_backward(dy, gate, f, w, scale, silu, dact, seq, name, comm=None):
    tokens, k_dim = dy.shape[0], w.shape[0]
    batch = tokens // seq
    tm = _tile_rows(seq)
    per_seq = seq // tm
    n_chunks = k_dim // MXU_N

    def body(dy_ref, gate_ref, f_ref, silu_ref, dact_ref, w_ref, df_ref, dgate_ref, dgu_ref):
        i = pl.program_id(0)
        dy_v = dy_ref[...]
        df_ref[...] = ((scale * gate_ref[0]) * dy_v).astype(BF16)
        part = scale * jnp.sum(dy_v * f_ref[...].astype(F32), axis=0, keepdims=True)

        @pl.when(i % per_seq == 0)
        def _():
            dgate_ref[0] = part

        @pl.when(i % per_seq != 0)
        def _():
            dgate_ref[0] = dgate_ref[0] + part

        for ck in range(n_chunks):
            cs = slice(ck * MXU_N, (ck + 1) * MXU_N)
            cu = slice(k_dim + ck * MXU_N, k_dim + (ck + 1) * MXU_N)
            da = _dot_nt(df_ref[...], w_ref[cs, :])
            dgu_ref[:, cs] = (da * dact_ref[:, cs].astype(F32)).astype(BF16)
            dgu_ref[:, cu] = (da * silu_ref[:, cs].astype(F32)).astype(BF16)

    row = lambda i: (i, 0)
    per_batch = pl.BlockSpec((1, 1, D_MODEL), lambda i: (i // per_seq, 0, 0))
    tile = lambda cols: pl.BlockSpec((tm, cols), row)
    outs, got = _call(
        body, name=name, grid=(tokens // tm,),
        out_shape=[jax.ShapeDtypeStruct((tokens, D_MODEL), BF16), jax.ShapeDtypeStruct((batch, 1, D_MODEL), F32),
                   jax.ShapeDtypeStruct((tokens, 2 * k_dim), BF16)],
        in_specs=[tile(D_MODEL), per_batch, tile(D_MODEL), tile(k_dim), tile(k_dim), _resident(w.shape)],
        out_specs=[tile(D_MODEL), per_batch, tile(2 * k_dim)],
        operands=(dy, gate, f, silu, dact, w), comm=comm)
    return (*outs, got)


def _matmul_normmod_backward(dsrc, w_t, x, dy, gnorm, scale1p, seq, name, comm=None):
    tokens, k_dim = dsrc.shape
    batch = tokens // seq
    tm = _tile_rows(seq)
    per_seq = seq // tm

    def body(ds_ref, w_ref, x_ref, dy_ref, g_ref, sc_ref, dx_ref, dsh_ref, dsc_ref, dg_ref):
        i = pl.program_id(0)
        dh = _dot(ds_ref[...], w_ref[...])
        xv = x_ref[...]
        r = _rms(xv)
        xn = xv * r
        gn = g_ref[...]
        dsh = jnp.sum(dh, axis=0, keepdims=True)
        dsc = jnp.sum(dh * (xn * gn), axis=0, keepdims=True)
        dhn = dh * sc_ref[0]
        dg = jnp.sum(dhn * xn, axis=0, keepdims=True)
        dxn = dhn * gn
        dx_ref[...] = dy_ref[...] + r * (dxn - xn * jnp.mean(dxn * xn, axis=-1, keepdims=True))

        @pl.when(i % per_seq == 0)
        def _():
            dsh_ref[0] = dsh
            dsc_ref[0] = dsc

        @pl.when(i % per_seq != 0)
        def _():
            dsh_ref[0] = dsh_ref[0] + dsh
            dsc_ref[0] = dsc_ref[0] + dsc

        @pl.when(i == 0)
        def _():
            dg_ref[...] = dg

        @pl.when(i != 0)
        def _():
            dg_ref[...] = dg_ref[...] + dg

    row = lambda i: (i, 0)
    per_batch = pl.BlockSpec((1, 1, D_MODEL), lambda i: (i // per_seq, 0, 0))
    outs, got = _call(
        body, name=name, grid=(tokens // tm,),
        out_shape=[jax.ShapeDtypeStruct((tokens, D_MODEL), F32), jax.ShapeDtypeStruct((batch, 1, D_MODEL), F32),
                   jax.ShapeDtypeStruct((batch, 1, D_MODEL), F32), jax.ShapeDtypeStruct((1, D_MODEL), F32)],
        in_specs=[pl.BlockSpec((tm, k_dim), row), _resident(w_t.shape), pl.BlockSpec((tm, D_MODEL), row),
                  pl.BlockSpec((tm, D_MODEL), row), _resident((1, D_MODEL)), per_batch],
        out_specs=[pl.BlockSpec((tm, D_MODEL), row), per_batch, per_batch, pl.BlockSpec((1, D_MODEL), lambda i: (0, 0))],
        operands=(dsrc, w_t, x, dy, gnorm, scale1p), comm=comm)
    return (*outs, got)


def _ffn_backward(dy, gate, f, silu, dact, w_down, w_gu_t, x, gnorm, scale1p, scale, seq, name, comm=None):
    tokens, k_dim = dy.shape[0], w_down.shape[0]
    batch = tokens // seq
    tm = min(256, seq)
    per_seq = seq // tm
    n_chunks = k_dim // MXU_N

    def body(dy_ref, gate_ref, f_ref, silu_ref, dact_ref, wd_ref, wgu_ref, x_ref, g_ref, sc_ref,
             df_ref, dgate_ref, dgu_ref, dx_ref, dsh_ref, dsc_ref, dg_ref):
        i = pl.program_id(0)
        dy_v = dy_ref[...]
        df_ref[...] = ((scale * gate_ref[0]) * dy_v).astype(BF16)
        dgate = scale * jnp.sum(dy_v * f_ref[...].astype(F32), axis=0, keepdims=True)
        for ck in range(n_chunks):
            cs = slice(ck * MXU_N, (ck + 1) * MXU_N)
            cu = slice(k_dim + ck * MXU_N, k_dim + (ck + 1) * MXU_N)
            da = _dot_nt(df_ref[...], wd_ref[cs, :])
            dgu_ref[:, cs] = (da * dact_ref[:, cs].astype(F32)).astype(BF16)
            dgu_ref[:, cu] = (da * silu_ref[:, cs].astype(F32)).astype(BF16)
        dh = _dot(dgu_ref[...], wgu_ref[...])
        xv = x_ref[...]
        r = _rms(xv)
        xn = xv * r
        gn = g_ref[...]
        dsh = jnp.sum(dh, axis=0, keepdims=True)
        dsc = jnp.sum(dh * (xn * gn), axis=0, keepdims=True)
        dhn = dh * sc_ref[0]
        dg = jnp.sum(dhn * xn, axis=0, keepdims=True)
        dxn = dhn * gn
        dx_ref[...] = dy_v + r * (dxn - xn * jnp.mean(dxn * xn, axis=-1, keepdims=True))

        @pl.when(i % per_seq == 0)
        def _():
            dgate_ref[0] = dgate
            dsh_ref[0] = dsh
            dsc_ref[0] = dsc

        @pl.when(i % per_seq != 0)
        def _():
            dgate_ref[0] = dgate_ref[0] + dgate
            dsh_ref[0] = dsh_ref[0] + dsh
            dsc_ref[0] = dsc_ref[0] + dsc

        @pl.when(i == 0)
        def _():
            dg_ref[...] = dg

        @pl.when(i != 0)
        def _():
            dg_ref[...] = dg_ref[...] + dg

    row = lambda i: (i, 0)
    per_batch = pl.BlockSpec((1, 1, D_MODEL), lambda i: (i // per_seq, 0, 0))
    tile = lambda width: pl.BlockSpec((tm, width), row)
    vec = jax.ShapeDtypeStruct((batch, 1, D_MODEL), F32)
    outs, got = _call(
        body, name=name, grid=(tokens // tm,),
        out_shape=[jax.ShapeDtypeStruct((tokens, D_MODEL), BF16), vec, jax.ShapeDtypeStruct((tokens, 2 * k_dim), BF16),
                   jax.ShapeDtypeStruct((tokens, D_MODEL), F32), vec, vec, jax.ShapeDtypeStruct((1, D_MODEL), F32)],
        in_specs=[tile(D_MODEL), per_batch, tile(D_MODEL), tile(k_dim), tile(k_dim), _resident(w_down.shape),
                  _resident(w_gu_t.shape), tile(D_MODEL), _resident((1, D_MODEL)), per_batch],
        out_specs=[tile(D_MODEL), per_batch, tile(2 * k_dim), tile(D_MODEL), per_batch, per_batch,
                   pl.BlockSpec((1, D_MODEL), lambda i: (0, 0))],
        operands=(dy, gate, f, silu, dact, w_down, w_gu_t, x, gnorm, scale1p), comm=comm)
    return (*outs, got)


def _weight_grad(a, b, seq, name, comm=None):
    tokens, n_out = a.shape
    tn = MXU_N

    def body(a_ref, b_ref, o_ref):
        o_ref[...] = _dot_tn(a_ref[...], b_ref[...]).astype(BF16)

    (out,), got = _call(
        body, name=name, grid=(n_out // tn,),
        out_shape=[jax.ShapeDtypeStruct((n_out, D_MODEL), BF16)],
        in_specs=[pl.BlockSpec((tokens, tn), lambda j: (0, j)), _resident((tokens, D_MODEL))],
        out_specs=[pl.BlockSpec((tn, D_MODEL), lambda j: (j, 0))],
        operands=(a, b), parallel=True, comm=comm)
    return out, got


def _group_mean(v, bd):
    hi = v.astype(BF16)
    lo = (v - hi.astype(F32)).astype(BF16)
    return _dot(hi, bd) + _dot(lo, bd)


def _sgu_forward(pm_ref, wm_ref, bias_ref, lng_ref, lnb_ref, bd_ref, mixed_scr, n_sub):
    ua = pm_ref[:, 0:D_A].astype(F32)
    va = pm_ref[:, D_A:2 * D_A].astype(F32)
    u_act = _gelu(ua)
    v_act = _gelu(va)
    bd = bd_ref[...]
    vc = v_act - _group_mean(v_act, bd)
    rstd = lax.rsqrt(_group_mean(vc * vc, bd) + EPS)
    vhat = vc * rstd
    vln = vhat * lng_ref[...] + lnb_ref[...]
    left = lax.broadcasted_iota(jnp.int32, (CHUNK, LANES), 1) < HEAD_DIM
    for q in range(n_sub):
        rows = slice(q * CHUNK, (q + 1) * CHUNK)
        for p in range(N_HEADS // 2):
            cols = slice(p * LANES, (p + 1) * LANES)
            vp = vln[rows, cols]
            v_l = jnp.where(left, vp, 0.0).astype(BF16)
            v_r = jnp.where(left, 0.0, vp).astype(BF16)
            mixed_scr[rows, cols] = _dot(wm_ref[2 * p], v_l) + _dot(wm_ref[2 * p + 1], v_r) + bias_ref[:, cols]
    return ua, va, u_act, vhat, rstd, vln


def _halo_specs(tm, tokens, width):
    prev = pl.BlockSpec((HALO, width), lambda i: (jnp.maximum(i * (tm // HALO) - 1, 0), 0))
    nxt = pl.BlockSpec((HALO, width), lambda i: (jnp.minimum((i + 1) * (tm // HALO), tokens // HALO - 1), 0))
    return prev, nxt


def _mixer_forward(x, gnorm, scale1p, shift, w_in_t, gate, w_out, wm, bias_full, lng, lnb, convw, og, bd, seq, name, comm=None):
    tokens = x.shape[0]
    tm = _tile_rows(seq)
    per_seq = seq // tm
    n_sub = tm // CHUNK

    def body(x_ref, xp_ref, g_ref, sc_ref, sh_ref, win_ref, gate_ref, wo_ref, wm_ref, bias_ref, lng_ref, lnb_ref, cw_ref,
             og_ref, bd_ref, h_ref, pm_ref, y_ref, xo_ref, o_ref, mixed_scr):
        i = pl.program_id(0)
        first = (i % per_seq) == 0
        xv = x_ref[...]
        h_ref[...] = ((xv * _rms(xv) * g_ref[...]) * sc_ref[0] + sh_ref[0]).astype(BF16)
        for ck in range(D_PROJ // MXU_N):
            cs = slice(ck * MXU_N, (ck + 1) * MXU_N)
            pm_ref[:, cs] = _dot_nt(h_ref[...], win_ref[cs, :]).astype(BF16)
        xp = xp_ref[...]
        hp = ((xp * _rms(xp) * g_ref[...]) * sc_ref[0] + sh_ref[0]).astype(BF16)
        gates_prev = _dot_nt(hp, win_ref[3 * D_A:5 * D_A, :]).astype(BF16).astype(F32)

        _, _, u_act, _, _, _ = _sgu_forward(pm_ref, wm_ref, bias_ref, lng_ref, lnb_ref, bd_ref, mixed_scr, n_sub)
        ya = u_act * mixed_scr[...]
        y_ref[:, 0:D_A] = (ya * _rms(ya) * og_ref[:, 0:D_A]).astype(BF16)

        bg = pm_ref[:, 2 * D_A:3 * D_A].astype(F32)
        z = pm_ref[:, 3 * D_A:4 * D_A].astype(F32) * pm_ref[:, 4 * D_A:5 * D_A].astype(F32)
        zp = jnp.where(first, 0.0, gates_prev[:, 0:D_A] * gates_prev[:, D_A:2 * D_A])
        zext = jnp.concatenate([zp, z], axis=0)
        z1 = pltpu.roll(zext, 1, 0)[HALO:]
        z2 = pltpu.roll(zext, 2, 0)[HALO:]
        conv = cw_ref[0:1, :] * z2 + cw_ref[1:2, :] * z1 + cw_ref[2:3, :] * z
        yb = bg * conv
        y_ref[:, D_A:2 * D_A] = (yb * _rms(yb) * og_ref[:, D_A:2 * D_A]).astype(BF16)

        f = _dot(y_ref[...], wo_ref[...])
        o_ref[...] = f.astype(BF16)
        xo_ref[...] = xv + gate_ref[0] * f

    prev, _ = _halo_specs(tm, tokens, D_MODEL)
    tile = pl.BlockSpec((tm, D_MODEL), lambda i: (i, 0))
    per_batch = pl.BlockSpec((1, 1, D_MODEL), lambda i: (i // per_seq, 0, 0))
    bf = lambda cols: jax.ShapeDtypeStruct((tokens, cols), BF16)
    outs, got = _call(
        body, name=name, grid=(tokens // tm,),
        out_shape=[bf(D_MODEL), bf(D_PROJ), bf(D_MODEL), jax.ShapeDtypeStruct((tokens, D_MODEL), F32), bf(D_MODEL)],
        in_specs=[tile, prev, _resident((1, D_MODEL)), per_batch, per_batch, _resident(w_in_t.shape), per_batch,
                  _resident(w_out.shape), _resident(wm.shape), _resident(bias_full.shape), _resident(lng.shape),
                  _resident(lnb.shape), _resident(convw.shape), _resident(og.shape), _resident(bd.shape)],
        out_specs=[tile, pl.BlockSpec((tm, D_PROJ), lambda i: (i, 0)), tile, tile, tile],
        scratch_shapes=[pltpu.VMEM((tm, D_A), F32)],
        operands=(x, x, gnorm, scale1p, shift, w_in_t, gate, w_out, wm, bias_full, lng, lnb, convw, og, bd),
        parallel=True, comm=comm)
    return (*outs, got)


def _mixer_backward(proj, dx, gate, o, w_out, x, gnorm, scale1p, w_in_t, wm, bias_full, lng, lnb, convw, og, bd, causal,
                    seq, name, comm=None):
    tokens = proj.shape[0]
    batch = tokens // seq
    tm = _tile_rows(seq)
    per_seq = seq // tm
    n_sub = tm // CHUNK
    ext = tm + 2 * HALO

    def body(pm_ref, pp_ref, pn_ref, dx_ref, dxn_ref, gate_ref, o_ref, wo_ref, x_ref, g_ref, sc_ref, win_ref, wm_ref, bias_ref,
             lng_ref, lnb_ref, cw_ref, og_ref, bd_ref, causal_ref, do_ref, dgate_ref, dp_ref, dxo_ref, dsh_ref, dsc_ref, dgn_ref,
             dog_ref, dcw_ref, dlng_ref, dlnb_ref, dbias_ref, dwm_ref, mixed_scr, dvln_scr, dy_scr):
        i = pl.program_id(0)
        first = (i % per_seq) == 0
        last = (i % per_seq) == per_seq - 1

        dx_v = dx_ref[...]
        do_ref[...] = (gate_ref[0] * dx_v).astype(BF16)
        dgate = jnp.sum(dx_v * o_ref[...].astype(F32), axis=0, keepdims=True)
        dy_scr[...] = _dot_nt(do_ref[...], wo_ref[...])
        dyn_conv = _dot_nt((gate_ref[0] * dxn_ref[...]).astype(BF16), wo_ref[D_A:2 * D_A, :])

        @pl.when(i == 0)
        def _():
            dog_ref[...] = jnp.zeros_like(dog_ref)
            dcw_ref[...] = jnp.zeros_like(dcw_ref)
            dlng_ref[...] = jnp.zeros_like(dlng_ref)
            dlnb_ref[...] = jnp.zeros_like(dlnb_ref)
            dbias_ref[...] = jnp.zeros_like(dbias_ref)
            dwm_ref[...] = jnp.zeros_like(dwm_ref)

        ua, va, u_act, vhat, rstd, vln = _sgu_forward(pm_ref, wm_ref, bias_ref, lng_ref, lnb_ref, bd_ref, mixed_scr, n_sub)
        mixed = mixed_scr[...]
        ya = u_act * mixed
        ra = _rms(ya)
        yhat = ya * ra
        dya_in = dy_scr[:, 0:D_A]
        dog_ref[:, 0:D_A] = dog_ref[:, 0:D_A] + jnp.sum(dya_in * yhat, axis=0, keepdims=True)
        dyh = dya_in * og_ref[:, 0:D_A]
        dya = ra * (dyh - yhat * jnp.mean(dyh * yhat, axis=-1, keepdims=True))
        d_u = dya * mixed
        d_mixed = dya * u_act
        left = lax.broadcasted_iota(jnp.int32, (CHUNK, LANES), 1) < HEAD_DIM
        dbias = jnp.zeros((CHUNK, D_A), F32)
        for q in range(n_sub):
            rows = slice(q * CHUNK, (q + 1) * CHUNK)
            dbias = dbias + d_mixed[rows, :]
            for p in range(N_HEADS // 2):
                cols = slice(p * LANES, (p + 1) * LANES)
                dm = d_mixed[rows, cols]
                dm_l = jnp.where(left, dm, 0.0).astype(BF16)
                dm_r = jnp.where(left, 0.0, dm).astype(BF16)
                vp = vln[rows, cols].astype(BF16)
                dwm_ref[2 * p] = dwm_ref[2 * p] + causal_ref[...] * _dot_nt(dm_l, vp)
                dwm_ref[2 * p + 1] = dwm_ref[2 * p + 1] + causal_ref[...] * _dot_nt(dm_r, vp)
                dvln_scr[rows, cols] = _dot_tn(wm_ref[2 * p], dm_l) + _dot_tn(wm_ref[2 * p + 1], dm_r)
        dbias_ref[...] = dbias_ref[...] + dbias
        dvln = dvln_scr[...]
        dlng_ref[...] = dlng_ref[...] + jnp.sum(dvln * vhat, axis=0, keepdims=True)
        dlnb_ref[...] = dlnb_ref[...] + jnp.sum(dvln, axis=0, keepdims=True)
        dvh = dvln * lng_ref[...]
        bd = bd_ref[...]
        d_v = rstd * (dvh - _group_mean(dvh, bd) - vhat * _group_mean(dvh * vhat, bd))
        dp_ref[:, 0:D_A] = (d_u * _gelu_grad(ua)).astype(BF16)
        dp_ref[:, D_A:2 * D_A] = (d_v * _gelu_grad(va)).astype(BF16)
        dh_a = _dot(dp_ref[:, 0:2 * D_A], win_ref[0:2 * D_A, :])

        def ext_cols(lo):
            cs = slice(lo, lo + D_A)
            return jnp.concatenate([pp_ref[:, cs], pm_ref[:, cs], pn_ref[:, cs]], axis=0).astype(F32)

        bg, cg, xb = ext_cols(2 * D_A), ext_cols(3 * D_A), ext_cols(4 * D_A)
        row = lax.broadcasted_iota(jnp.int32, (ext, D_A), 0)
        z = jnp.where(jnp.logical_and(first, row < HALO), 0.0, cg * xb)
        z1 = pltpu.roll(z, 1, 0)
        z2 = pltpu.roll(z, 2, 0)
        w0, w1, w2 = cw_ref[0:1, :], cw_ref[1:2, :], cw_ref[2:3, :]
        conv = w0 * z2 + w1 * z1 + w2 * z
        yb = bg * conv
        rb = _rms(yb)
        yhb = yb * rb
        dyn = jnp.where(last, 0.0, dyn_conv)
        dyb_in = jnp.concatenate([jnp.zeros((HALO, D_A), F32), dy_scr[:, D_A:2 * D_A], dyn], axis=0)
        dyhb = dyb_in * og_ref[:, D_A:2 * D_A]
        dyb = rb * (dyhb - yhb * jnp.mean(dyhb * yhb, axis=-1, keepdims=True))
        d_conv = dyb * bg
        dz = w2 * d_conv + w1 * pltpu.roll(d_conv, ext - 1, 0) + w0 * pltpu.roll(d_conv, ext - 2, 0)
        main = slice(HALO, HALO + tm)
        dp_ref[:, 2 * D_A:3 * D_A] = (dyb * conv)[main].astype(BF16)
        dp_ref[:, 3 * D_A:4 * D_A] = (dz * xb)[main].astype(BF16)
        dp_ref[:, 4 * D_A:5 * D_A] = (dz * cg)[main].astype(BF16)
        dog_ref[:, D_A:2 * D_A] = dog_ref[:, D_A:2 * D_A] + jnp.sum((dyb_in * yhb)[main], axis=0, keepdims=True)
        dcm = d_conv[main]
        dcw_ref[0:1, :] = dcw_ref[0:1, :] + jnp.sum(dcm * z2[main], axis=0, keepdims=True)
        dcw_ref[1:2, :] = dcw_ref[1:2, :] + jnp.sum(dcm * z1[main], axis=0, keepdims=True)
        dcw_ref[2:3, :] = dcw_ref[2:3, :] + jnp.sum(dcm * z[main], axis=0, keepdims=True)

        dh = dh_a + _dot(dp_ref[:, 2 * D_A:5 * D_A], win_ref[2 * D_A:5 * D_A, :])
        xv = x_ref[...]
        r = _rms(xv)
        xn = xv * r
        gn = g_ref[...]
        dsh = jnp.sum(dh, axis=0, keepdims=True)
        dsc = jnp.sum(dh * (xn * gn), axis=0, keepdims=True)
        dhn = dh * sc_ref[0]
        dgn = jnp.sum(dhn * xn, axis=0, keepdims=True)
        dxn = dhn * gn
        dxo_ref[...] = dx_v + r * (dxn - xn * jnp.mean(dxn * xn, axis=-1, keepdims=True))

        @pl.when(first)
        def _():
            dgate_ref[0] = dgate
            dsh_ref[0] = dsh
            dsc_ref[0] = dsc

        @pl.when(jnp.logical_not(first))
        def _():
            dgate_ref[0] = dgate_ref[0] + dgate
            dsh_ref[0] = dsh_ref[0] + dsh
            dsc_ref[0] = dsc_ref[0] + dsc

        @pl.when(i == 0)
        def _():
            dgn_ref[...] = dgn

        @pl.when(i != 0)
        def _():
            dgn_ref[...] = dgn_ref[...] + dgn

    prev_p, next_p = _halo_specs(tm, tokens, D_PROJ)
    _, next_d = _halo_specs(tm, tokens, D_MODEL)
    fixed2 = lambda shape: pl.BlockSpec(shape, lambda i: (0, 0))
    tile = pl.BlockSpec((tm, D_MODEL), lambda i: (i, 0))
    per_batch = pl.BlockSpec((1, 1, D_MODEL), lambda i: (i // per_seq, 0, 0))
    vec = jax.ShapeDtypeStruct((batch, 1, D_MODEL), F32)
    outs, got = _call(
        body, name=name, grid=(tokens // tm,),
        out_shape=[jax.ShapeDtypeStruct((tokens, D_MODEL), BF16), vec, jax.ShapeDtypeStruct((tokens, D_PROJ), BF16),
                   jax.ShapeDtypeStruct((tokens, D_MODEL), F32), vec, vec, jax.ShapeDtypeStruct((1, D_MODEL), F32),
                   jax.ShapeDtypeStruct((1, D_MODEL), F32), jax.ShapeDtypeStruct((8, D_A), F32),
                   jax.ShapeDtypeStruct((1, D_A), F32), jax.ShapeDtypeStruct((1, D_A), F32),
                   jax.ShapeDtypeStruct((CHUNK, D_A), F32), jax.ShapeDtypeStruct((N_HEADS, CHUNK, CHUNK), F32)],
        in_specs=[pl.BlockSpec((tm, D_PROJ), lambda i: (i, 0)), prev_p, next_p, tile, next_d, per_batch, tile,
                  _resident(w_out.shape), tile, _resident((1, D_MODEL)), per_batch, _resident(w_in_t.shape),
                  _resident(wm.shape), _resident(bias_full.shape), _resident(lng.shape), _resident(lnb.shape),
                  _resident(convw.shape), _resident(og.shape), _resident(bd.shape), _resident(causal.shape)],
        out_specs=[tile, per_batch, pl.BlockSpec((tm, D_PROJ), lambda i: (i, 0)), tile, per_batch, per_batch,
                   fixed2((1, D_MODEL)), fixed2((1, D_MODEL)), fixed2((8, D_A)), fixed2((1, D_A)), fixed2((1, D_A)),
                   fixed2((CHUNK, D_A)), pl.BlockSpec((N_HEADS, CHUNK, CHUNK), lambda i: (0, 0, 0))],
        scratch_shapes=[pltpu.VMEM((tm, D_A), F32), pltpu.VMEM((tm, D_A), F32), pltpu.VMEM((tm, D_MODEL), F32)],
        operands=(proj, proj, proj, dx, dx, gate, o, w_out, x, gnorm, scale1p, w_in_t, wm, bias_full, lng, lnb, convw, og, bd,
                  causal), comm=comm)
    return (*outs, got)


def _adamw_update(wv, gv, mv, vv):
    nm = ADAM_B1 * mv + (1.0 - ADAM_B1) * gv
    nv = ADAM_B2 * vv + (1.0 - ADAM_B2) * (gv * gv)
    m_hat = nm / (1.0 - ADAM_B1 ** ADAM_STEP)
    v_hat = nv / (1.0 - ADAM_B2 ** ADAM_STEP)
    return -ADAM_LR * (m_hat / (jnp.sqrt(v_hat) + ADAM_EPS) + ADAM_WD * wv), nm, nv


def _adamw_rows(recv, w, m, v, name):
    depth, rows, cols = w.shape
    tr = rows // 2
    last = rows // tr - 1

    def body(*refs):
        r_refs, (w_ref, m_ref, v_ref, g_ref, d_ref, nm_ref, nv_ref) = refs[:depth], refs[depth:]
        for l in range(depth):
            @pl.when(pl.program_id(0) == l)
            def _(r_ref=r_refs[l]):
                acc = r_ref[0].astype(F32)
                for d in range(1, N_DEV):
                    acc = acc + r_ref[d].astype(F32)
                g_ref[0] = acc
                d_ref[0], nm_ref[0], nv_ref[0] = _adamw_update(w_ref[0], acc, m_ref[0], v_ref[0])

    def slots(l):
        return pl.BlockSpec((N_DEV, tr, cols), lambda ll, i: (0, jnp.where(ll == l, i, jnp.where(ll < l, 0, last)), 0))

    spec = pl.BlockSpec((1, tr, cols), lambda ll, i: (ll, i, 0))
    return pl.pallas_call(
        body, name=name, grid=(depth, rows // tr),
        out_shape=[jax.ShapeDtypeStruct((depth, rows, cols), F32)] * 4,
        in_specs=[slots(l) for l in range(depth)] + [spec] * 3, out_specs=[spec] * 4,
        compiler_params=_params(2),
    )(*recv, w, m, v)


def _adamw(w, g, m, v, name):
    rows, cols = w.shape
    tr = max(t for t in range(8, 513, 8) if rows % t == 0)

    def body(w_ref, g_ref, m_ref, v_ref, d_ref, nm_ref, nv_ref):
        d_ref[...], nm_ref[...], nv_ref[...] = _adamw_update(w_ref[...], g_ref[...], m_ref[...], v_ref[...])

    spec = pl.BlockSpec((tr, cols), lambda i: (i, 0))
    return pl.pallas_call(
        body, name=name, grid=(rows // tr,),
        out_shape=[jax.ShapeDtypeStruct((rows, cols), F32)] * 3,
        in_specs=[spec] * 4, out_specs=[spec] * 3,
        compiler_params=_params(parallel=True),
    )(w, g, m, v)


def _adamw_many(ws, gs, ms, vs, name):
    n = len(ws)
    two_d = lambda a: a.reshape(-1, a.shape[-1])

    def body(*refs):
        w_refs, g_refs, m_refs, v_refs = refs[:n], refs[n:2 * n], refs[2 * n:3 * n], refs[3 * n:4 * n]
        d_refs, nm_refs, nv_refs = refs[4 * n:5 * n], refs[5 * n:6 * n], refs[6 * n:]
        for k in range(n):
            d_refs[k][...], nm_refs[k][...], nv_refs[k][...] = _adamw_update(
                w_refs[k][...], g_refs[k][...], m_refs[k][...], v_refs[k][...])

    flat = [two_d(a) for a in ws]
    outs = pl.pallas_call(
        body, name=name, out_shape=[jax.ShapeDtypeStruct(a.shape, F32) for a in flat] * 3,
        in_specs=[pl.BlockSpec(memory_space=pltpu.VMEM)] * (4 * n),
        out_specs=[pl.BlockSpec(memory_space=pltpu.VMEM)] * (3 * n),
        compiler_params=pltpu.CompilerParams(vmem_limit_bytes=VMEM_LIMIT),
    )(*flat, *[two_d(a) for a in gs], *[two_d(a) for a in ms], *[two_d(a) for a in vs])
    shaped = [o.reshape(ws[k % n].shape) for k, o in enumerate(outs)]
    return shaped[:n], shaped[n:2 * n], shaped[2 * n:]


def _adamw_nd(w, g, m, v, name):
    shape = w.shape
    two_d = (-1, shape[-1])
    d, nm, nv = _adamw(w.reshape(two_d), g.reshape(two_d), m.reshape(two_d), v.reshape(two_d), name)
    return d.reshape(shape), nm.reshape(shape), nv.reshape(shape)


def kernel(x, c, ada_w, ada_b, norm_ffn1_g, ffn1_w_gu, ffn1_w_down, norm_mix_g, mix_w_in, sgu_ln_g, sgu_ln_b, sgu_w_s, sgu_b, conv_w, out_norm_g, mix_w_out, norm_ffn2_g, ffn2_w_gu, ffn2_w_down, final_norm_g, loss_target, m_ada_w, m_ada_b, m_norm_ffn1_g, m_ffn1_w_gu, m_ffn1_w_down, m_norm_mix_g, m_mix_w_in, m_sgu_ln_g, m_sgu_ln_b, m_sgu_w_s, m_sgu_b, m_conv_w, m_out_norm_g, m_mix_w_out, m_norm_ffn2_g, m_ffn2_w_gu, m_ffn2_w_down, m_final_norm_g, v_ada_w, v_ada_b, v_norm_ffn1_g, v_ffn1_w_gu, v_ffn1_w_down, v_norm_mix_g, v_mix_w_in, v_sgu_ln_g, v_sgu_ln_b, v_sgu_w_s, v_sgu_b, v_conv_w, v_out_norm_g, v_mix_w_out, v_norm_ffn2_g, v_ffn2_w_gu, v_ffn2_w_down, v_final_norm_g):
    batch, seq, _ = x.shape
    tokens = batch * seq
    me = 4 * lax.axis_index("x") + 2 * lax.axis_index("y") + lax.axis_index("c")
    weights = dict(ada_w=ada_w, ada_b=ada_b, norm_ffn1_g=norm_ffn1_g, ffn1_w_gu=ffn1_w_gu, ffn1_w_down=ffn1_w_down,
                   norm_mix_g=norm_mix_g, mix_w_in=mix_w_in, sgu_ln_g=sgu_ln_g, sgu_ln_b=sgu_ln_b, sgu_w_s=sgu_w_s,
                   sgu_b=sgu_b, conv_w=conv_w, out_norm_g=out_norm_g, mix_w_out=mix_w_out, norm_ffn2_g=norm_ffn2_g,
                   ffn2_w_gu=ffn2_w_gu, ffn2_w_down=ffn2_w_down, final_norm_g=final_norm_g)
    mom1 = dict(ada_w=m_ada_w, ada_b=m_ada_b, norm_ffn1_g=m_norm_ffn1_g, ffn1_w_gu=m_ffn1_w_gu,
                ffn1_w_down=m_ffn1_w_down, norm_mix_g=m_norm_mix_g, mix_w_in=m_mix_w_in, sgu_ln_g=m_sgu_ln_g,
                sgu_ln_b=m_sgu_ln_b, sgu_w_s=m_sgu_w_s, sgu_b=m_sgu_b, conv_w=m_conv_w, out_norm_g=m_out_norm_g,
                mix_w_out=m_mix_w_out, norm_ffn2_g=m_norm_ffn2_g, ffn2_w_gu=m_ffn2_w_gu, ffn2_w_down=m_ffn2_w_down,
                final_norm_g=m_final_norm_g)
    mom2 = dict(ada_w=v_ada_w, ada_b=v_ada_b, norm_ffn1_g=v_norm_ffn1_g, ffn1_w_gu=v_ffn1_w_gu,
                ffn1_w_down=v_ffn1_w_down, norm_mix_g=v_norm_mix_g, mix_w_in=v_mix_w_in, sgu_ln_g=v_sgu_ln_g,
                sgu_ln_b=v_sgu_ln_b, sgu_w_s=v_sgu_w_s, sgu_b=v_sgu_b, conv_w=v_conv_w, out_norm_g=v_out_norm_g,
                mix_w_out=v_mix_w_out, norm_ffn2_g=v_norm_ffn2_g, ffn2_w_gu=v_ffn2_w_gu, ffn2_w_down=v_ffn2_w_down,
                final_norm_g=v_final_norm_g)

    big = ("ffn1_w_gu", "ffn1_w_down", "mix_w_in", "mix_w_out", "ffn2_w_gu", "ffn2_w_down")
    transposed = ("ffn1_w_gu", "mix_w_in", "ffn2_w_gu")
    as_rows = lambda nm, a: jnp.swapaxes(a, 1, 2) if nm in transposed else a
    shard = {(l, nm): as_rows(nm, weights[nm])[l].astype(BF16) for l in range(DEPTH) for nm in big}
    full_w = {}

    def gather_of(keys):
        return keys, _GatherRows([shard[k] for k in keys])

    def landed(plan, got):
        full_w.update(zip(plan[0], got))

    ada_cols = ada_w.shape[2]
    ada_b_cols = lax.dynamic_slice_in_dim(ada_b, me * ada_cols, ada_cols, axis=1).reshape(DEPTH, 1, ada_cols)
    plan = gather_of([(0, "ffn1_w_gu")])
    c_dev, convw_dev, ada_recv, got = _prologue(
        jnp.pad(c, ((0, 8 - batch), (0, 0))), jnp.pad(conv_w.reshape(-1), (0, 8 * LANES - conv_w.size)).reshape(8, LANES),
        ada_w, ada_b_cols, plan[1])
    landed(plan, got)
    c_all = c_dev[:, :batch].reshape(N_DEV * batch, D_MODEL)
    convw_all = convw_dev.reshape(N_DEV, -1)[:, :conv_w.size].reshape((N_DEV,) + conv_w.shape)
    convw_full = jnp.transpose(convw_all, (1, 2, 0, 3)).reshape(DEPTH, 3, D_A)
    ada_mine = jnp.transpose(ada_recv[:, :, :batch, :], (1, 2, 0, 3)).reshape(DEPTH, batch, N_MOD * D_MODEL)
    mod = ada_mine.reshape(DEPTH, batch, N_MOD, 1, D_MODEL)

    causal = jnp.tril(jnp.ones((CHUNK, CHUNK), F32))
    bd = jnp.kron(jnp.eye(N_HEADS, dtype=F32), jnp.full((HEAD_DIM, HEAD_DIM), 1.0 / HEAD_DIM, F32)).astype(BF16)
    row_vec = lambda a: a.reshape(1, -1)

    hosted_gathers = {
        (0, "ffn1"): [(0, "ffn1_w_down"), (0, "mix_w_in"), (0, "mix_w_out")],
        (0, "ffn_down1"): [(0, "ffn2_w_gu")],
        (0, "mix_in"): [(0, "ffn2_w_down")],
        (0, "ffn2"): [(1, "ffn1_w_gu"), (1, "ffn1_w_down"), (1, "mix_w_in"), (1, "mix_w_out")],
        (1, "ffn1"): [(1, "ffn2_w_gu"), (1, "ffn2_w_down")],
    }

    def hosting(l, site):
        keys = hosted_gathers.get((l, site))
        return gather_of(keys) if keys else (None, None)

    xs = x.reshape(tokens, D_MODEL)
    saved = []
    for l in range(DEPTH):
        sh1, sc1, g1, sh2, sc2, g2, sh3, sc3, g3 = [mod[l, :, k] for k in range(N_MOD)]
        mixer_consts = dict(
            wm=(sgu_w_s[l] * causal[None]).astype(BF16),
            bias_full=jnp.repeat(sgu_b[l].T, HEAD_DIM, axis=1),
            lng=row_vec(jnp.tile(sgu_ln_g[l], N_HEADS)), lnb=row_vec(jnp.tile(sgu_ln_b[l], N_HEADS)),
            convw=jnp.pad(convw_full[l], ((0, 5), (0, 0))), og=row_vec(out_norm_g[l]), bd=bd)
        x0 = xs
        plan = hosting(l, "ffn1")
        if l == 0:
            h1, a1, s1, w1, got = _normmod_matmul(x0, row_vec(norm_ffn1_g[l]), 1.0 + sc1, sh1, full_w[l, "ffn1_w_gu"], seq, "ffn_up", plan[1])
            landed(plan, got)
            plan = hosting(l, "ffn_down1")
            x1, f1, got = _matmul_residual(a1, full_w[l, "ffn1_w_down"], x0, g1, 0.5, seq, "ffn_down", plan[1])
        else:
            h1, a1, s1, w1, x1, f1, got = _ffn_forward(
                x0, row_vec(norm_ffn1_g[l]), 1.0 + sc1, sh1, full_w[l, "ffn1_w_gu"], full_w[l, "ffn1_w_down"], g1, 0.5, seq, "ffn_fwd",
                comm=plan[1])
        if got:
            landed(plan, got)
        plan = hosting(l, "mix_in")
        h2, proj, ymix, x2, o2, got = _mixer_forward(
            x1, row_vec(norm_mix_g[l]), 1.0 + sc2, sh2, full_w[l, "mix_w_in"], g2, full_w[l, "mix_w_out"], seq=seq,
            name="mixer_forward", comm=plan[1], **mixer_consts)
        if got:
            landed(plan, got)
        plan = hosting(l, "ffn2")
        if l + 1 < DEPTH:
            h3, a3, s3, w3, x3, f3, got = _ffn_forward(
                x2, row_vec(norm_ffn2_g[l]), 1.0 + sc3, sh3, full_w[l, "ffn2_w_gu"], full_w[l, "ffn2_w_down"], g3, 0.5, seq, "ffn_fwd",
                comm=plan[1])
        else:
            head = (loss_target.reshape(tokens, D_MODEL), row_vec(final_norm_g))
            h3, a3, s3, w3, x3, f3, d_final_g, loss_cols, got = _ffn_forward(
                x2, row_vec(norm_ffn2_g[l]), 1.0 + sc3, sh3, full_w[l, "ffn2_w_gu"], full_w[l, "ffn2_w_down"], g3, 0.5, seq, "ffn_fwd_loss",
                loss_head=head, comm=plan[1])
        if got:
            landed(plan, got)
        saved.append(dict(x0=x0, x1=x1, x2=x2, h1=h1, h2=h2, h3=h3, a1=a1, s1=s1, w1=w1, a3=a3, s3=s3, w3=w3, f1=f1, f3=f3, o2=o2, proj=proj,
                          ymix=ymix, mixer_consts=mixer_consts, sc=(1.0 + sc1, 1.0 + sc2, 1.0 + sc3), gates=(g1, g2, g3)))
        xs = x3

    dx = xs

    recv = {}
    small_grads = [None] * DEPTH
    d_mod = [None] * DEPTH

    mix_names = ("out_norm_g", "sgu_ln_g", "sgu_ln_b", "sgu_w_s", "sgu_b", "conv_w")
    late_names = ("norm_ffn1_g", "norm_mix_g", "norm_ffn2_g")

    def mix_parts(l):
        return [small_grads[l][nm] for nm in mix_names]

    def late_parts(l):
        return [small_grads[l][nm] for nm in late_names] + [d_mod[l]]

    pending = []

    def scatter_later(l, nm, grad):
        pending.append(((l, nm), _ScatterRows([grad])))

    def host():
        keys, parts = [k for k, _ in pending], [p for _, p in pending]
        pending.clear()
        return keys, (_Exchanges(parts) if parts else None)

    def hosted(keys, got):
        if got:
            recv.update(zip(keys, got))

    for l in reversed(range(DEPTH)):
        sv = saved[l]
        mc = sv["mixer_consts"]
        if l + 1 < DEPTH:
            pending.append((("late", l + 1), _GatherRows([_pack_small(late_parts(l + 1))])))
        keys, comm = host()
        df3, dg3, dgu3, dx2, dsh3, dsc3, dn3, got = _ffn_backward(
            dx, sv["gates"][2], sv["f3"], sv["s3"], sv["w3"], full_w[l, "ffn2_w_down"], full_w[l, "ffn2_w_gu"], sv["x2"],
            row_vec(norm_ffn2_g[l]), sv["sc"][2], 0.5, seq, "ffn_bwd", comm)
        hosted(keys, got)
        gw_down2, _ = _weight_grad(sv["a3"], df3, seq, "grad_w_down")
        scatter_later(l, "ffn2_w_down", gw_down2)
        keys, comm = host()
        gw_gu2, got = _weight_grad(dgu3, sv["h3"], seq, "grad_w_gu", comm)
        hosted(keys, got)
        scatter_later(l, "ffn2_w_gu", gw_gu2)
        keys, comm = host()
        do2, dg2, dproj, dx1, dsh2, dsc2, dn2, d_og, d_cw, d_lng, d_lnb, d_bias, d_wm, got = _mixer_backward(
            sv["proj"], dx2, sv["gates"][1], sv["o2"], full_w[l, "mix_w_out"], sv["x1"], row_vec(norm_mix_g[l]), sv["sc"][1],
            full_w[l, "mix_w_in"], causal=causal, seq=seq, name="mixer_backward", comm=comm, **mc)
        hosted(keys, got)
        small_grads[l] = dict(
            out_norm_g=d_og, sgu_ln_g=d_lng.reshape(N_HEADS, HEAD_DIM).sum(0), sgu_ln_b=d_lnb.reshape(N_HEADS, HEAD_DIM).sum(0),
            sgu_w_s=d_wm, sgu_b=d_bias.reshape(CHUNK, N_HEADS, HEAD_DIM).sum(-1).T, conv_w=d_cw[0:3])
        gw_out, _ = _weight_grad(sv["ymix"], do2, seq, "grad_w_out")
        scatter_later(l, "mix_w_out", gw_out)
        keys, comm = host()
        gw_in, got = _weight_grad(dproj, sv["h2"], seq, "grad_w_in", comm)
        hosted(keys, got)
        scatter_later(l, "mix_w_in", gw_in)
        keys, comm = host()
        pending.append((("mix", l), _GatherRows([_pack_small(mix_parts(l))])))
        if l > 0:
            df1, dg1, dgu1, dx0, dsh1, dsc1, dn1, got = _ffn_backward(
                dx1, sv["gates"][0], sv["f1"], sv["s1"], sv["w1"], full_w[l, "ffn1_w_down"], full_w[l, "ffn1_w_gu"], sv["x0"],
                row_vec(norm_ffn1_g[l]), sv["sc"][0], 0.5, seq, "ffn_bwd", comm)
        else:
            df1, dg1, dgu1, got = _residual_backward(dx1, sv["gates"][0], sv["f1"], full_w[l, "ffn1_w_down"], 0.5, sv["s1"], sv["w1"], seq, "ffn_down_bwd", comm)
        hosted(keys, got)
        gw_down1, _ = _weight_grad(sv["a1"], df1, seq, "grad_w_down")
        scatter_later(l, "ffn1_w_down", gw_down1)
        keys, comm = host()
        gw_gu1, got = _weight_grad(dgu1, sv["h1"], seq, "grad_w_gu", comm)
        hosted(keys, got)
        scatter_later(l, "ffn1_w_gu", gw_gu1)
        if l == 0:
            keys, comm = host()
            dx0, dsh1, dsc1, dn1, got = _matmul_normmod_backward(dgu1, full_w[l, "ffn1_w_gu"], sv["x0"], dx1, row_vec(norm_ffn1_g[l]), sv["sc"][0], seq, "ffn_up_bwd", comm)
            hosted(keys, got)
        dx = dx0
        small_grads[l].update(norm_ffn1_g=dn1, norm_mix_g=dn2, norm_ffn2_g=dn3)
        d_mod[l] = jnp.concatenate([dsh1, dsc1, dg1, dsh2, dsc2, dg2, dsh3, dsc3, dg3], axis=1)
    grad_x = dx.reshape(batch, seq, D_MODEL)

    grad_big, delta, new_m, new_v = {}, {}, {}, {}
    for nm in big:
        results = _adamw_rows([recv[l, nm] for l in range(DEPTH)], as_rows(nm, weights[nm]), as_rows(nm, mom1[nm]),
                              as_rows(nm, mom2[nm]), "adamw_" + nm)
        grad_big[nm], delta[nm], new_m[nm], new_v[nm] = [as_rows(nm, r) for r in results]

    last_parts = late_parts(0) + [d_final_g, loss_cols]
    last_shapes = [p.shape for p in last_parts]
    packed_all, packed_sum = _all_gather_small(_pack_small(last_parts), "reduce_small")
    late_sum = {0: _unpack_small(packed_sum, last_shapes)}
    d_mod_dev = {0: _unpack_small(packed_all, last_shapes, lead=(N_DEV,))[len(late_names)]}
    mix_sum = {}
    for l in range(DEPTH):
        gathered = recv["mix", l].reshape(N_DEV, -1, LANES)
        mix_sum[l] = _unpack_small(_sum_gathered(gathered, "sum_mix"), [p.shape for p in mix_parts(l)])
        if l > 0:
            shapes_l = [p.shape for p in late_parts(l)]
            gathered = recv["late", l].reshape(N_DEV, -1, LANES)
            late_sum[l] = _unpack_small(_sum_gathered(gathered, "sum_late"), shapes_l)
            d_mod_dev[l] = _unpack_small(gathered, shapes_l, lead=(N_DEV,))[len(late_names)]
    grad_small = {}
    for group, names in ((mix_sum, mix_names), (late_sum, late_names)):
        for k, nm in enumerate(names):
            grad_small[nm] = jnp.stack([group[l][k] for l in range(DEPTH)]).reshape(
                (DEPTH, 3, D_A) if nm == "conv_w" else weights[nm].shape)
    grad_small["conv_w"] = lax.dynamic_slice_in_dim(grad_small["conv_w"], me * conv_w.shape[2], conv_w.shape[2], axis=2)
    grad_small["final_norm_g"] = late_sum[0][len(late_names) + 1].reshape(final_norm_g.shape)
    loss = jnp.sum(late_sum[0][len(late_names) + 2])
    d_ada_all = jnp.stack([d_mod_dev[l] for l in range(DEPTH)]).reshape(DEPTH, N_DEV * batch, N_MOD * D_MODEL)
    d_ada_cols = lax.dynamic_slice_in_dim(d_ada_all, me * ada_cols, ada_cols, axis=2)
    g_ada_w, g_ada_b = _ada_backward(c_all, d_ada_cols, d_ada_all)

    grads = dict(grad_big)
    grads.update(grad_small)
    grads["ada_w"] = g_ada_w
    grads["ada_b"] = g_ada_b.reshape(ada_b.shape)

    names = ("ada_w", "ada_b", "norm_ffn1_g", "ffn1_w_gu", "ffn1_w_down", "norm_mix_g", "mix_w_in", "sgu_ln_g",
             "sgu_ln_b", "sgu_w_s", "sgu_b", "conv_w", "out_norm_g", "mix_w_out", "norm_ffn2_g", "ffn2_w_gu",
             "ffn2_w_down", "final_norm_g")
    delta["ada_w"], new_m["ada_w"], new_v["ada_w"] = _adamw_nd(ada_w, grads["ada_w"], m_ada_w, v_ada_w, "adamw_ada_w")
    rest = [nm for nm in names if nm not in big and nm != "ada_w"]
    pick = lambda src: [src[nm] for nm in rest]
    for nm, d_k, m_k, v_k in zip(rest, *_adamw_many(pick(weights), pick(grads), pick(mom1), pick(mom2), "adamw_small")):
        delta[nm], new_m[nm], new_v[nm] = d_k, m_k, v_k

    return (loss, grad_x, *[grads[nm] for nm in names], *[delta[nm] for nm in names],
            *[new_m[nm] for nm in names], *[new_v[nm] for nm in names])
```

```python
import math

import jax
import jax.numpy as jnp
from jax import lax
from jax.experimental import pallas as pl
from jax.experimental.pallas import tpu as pltpu

F32 = jnp.float32
BF16 = jnp.bfloat16

D_MODEL = 1024
D_A = 512
D_PROJ = 2560
N_HEADS = 8
HEAD_DIM = 64
CHUNK = 128
N_MOD = 9
DEPTH = 2
EPS = 1e-6
N_DEV = 8
LANES = 128
MXU_N = 256
HALO = 16
VMEM_LIMIT = 56 * 1024 * 1024
FORWARD_STEPS = 4

ADAM_LR = 0.001
ADAM_B1 = 0.9
ADAM_B2 = 0.999
ADAM_EPS = 1e-08
ADAM_WD = 0.01
ADAM_STEP = 10

MESH = pl.DeviceIdType.MESH


def _dot(a, b):
    return jnp.dot(a, b, preferred_element_type=F32)


def _dot_nt(a, b):
    return lax.dot_general(a, b, (((1,), (1,)), ((), ())), preferred_element_type=F32)


def _dot_tn(a, b):
    return lax.dot_general(a, b, (((0,), (0,)), ((), ())), preferred_element_type=F32)


def _sigmoid(x):
    return 0.5 * jnp.tanh(0.5 * x) + 0.5


def _gelu(x):
    return 0.5 * x * (1.0 + lax.erf(x * (1.0 / math.sqrt(2.0))))


def _gelu_grad(x):
    cdf = 0.5 * (1.0 + lax.erf(x * (1.0 / math.sqrt(2.0))))
    return cdf + x * jnp.exp(-0.5 * x * x) * (1.0 / math.sqrt(2.0 * math.pi))


def _params(n_axes=1, parallel=False):
    sem = ("parallel" if parallel else "arbitrary",) * n_axes
    return pltpu.CompilerParams(dimension_semantics=sem, vmem_limit_bytes=VMEM_LIMIT)


def _resident(shape):
    nd = len(shape)
    return pl.BlockSpec(shape, lambda *_: (0,) * nd, pipeline_mode=pl.Buffered(1))


def _tile_rows(seq):
    return min(512, seq)


def _my_position():
    x, y, c = lax.axis_index("x"), lax.axis_index("y"), lax.axis_index("c")
    return x, y, c, 4 * x + 2 * y + c


def _peer(x, y, c, p):
    return (x ^ ((p >> 2) & 1), y ^ ((p >> 1) & 1), c ^ (p & 1))


class _GatherRows:
    def __init__(self, shards):
        self.operands = list(shards)
        n = len(shards)
        self.out_shape = [jax.ShapeDtypeStruct((N_DEV * s.shape[0], s.shape[1]), s.dtype) for s in shards]
        self.scratch = [pltpu.SemaphoreType.DMA((n, N_DEV - 1)), pltpu.SemaphoreType.DMA((n, N_DEV - 1)),
                        pltpu.SemaphoreType.DMA((n,))]

    def _plan(self, src, dst, send, recv, loc):
        x, y, c, _ = _my_position()
        me, sib = (x, y, c), (x, y, 1 - c)
        chips = [(1 - x, y), (x, 1 - y), (1 - x, 1 - y)]
        plans = []
        for k, shard in enumerate(self.operands):
            rows = shard.shape[0]

            def blk(pos, k=k, rows=rows):
                return dst[k].at[pl.ds((4 * pos[0] + 2 * pos[1] + pos[2]) * rows, rows), :]

            def rc(s, block, to, source=None, k=k, blk=blk):
                return pltpu.make_async_remote_copy(
                    src_ref=blk(block) if source is None else source, dst_ref=blk(block),
                    send_sem=send.at[k, s], recv_sem=recv.at[k, s], device_id=to, device_id_type=MESH)

            plans.append(dict(
                local=pltpu.make_async_copy(src[k], blk(me), loc.at[k]),
                first=[rc(0, me, sib, src[k])] + [rc(1 + j, me, (*chip, c), src[k]) for j, chip in enumerate(chips)],
                landed=[rc(1 + j, (*chip, c), me) for j, chip in enumerate(chips)],
                passed=[rc(4 + j, (*chip, c), sib) for j, chip in enumerate(chips)],
                from_sib=[rc(0, sib, me)] + [rc(4 + j, (*chip, 1 - c), me) for j, chip in enumerate(chips)]))
        return plans

    def start(self, src, dst, send, recv, loc):
        for plan in self._plan(src, dst, send, recv, loc):
            plan["local"].start()
            for cp in plan["first"]:
                cp.start()

    def forward(self, src, dst, send, recv, loc):
        for plan in self._plan(src, dst, send, recv, loc):
            for landed, passed in zip(plan["landed"], plan["passed"]):
                landed.wait_recv()
                passed.start()

    def finish(self, src, dst, send, recv, loc):
        for plan in self._plan(src, dst, send, recv, loc):
            for cp in plan["from_sib"]:
                cp.wait_recv()
            for cp in plan["first"] + plan["passed"]:
                cp.wait_send()
            plan["local"].wait()


class _ScatterRows:
    def __init__(self, grads):
        self.operands = list(grads)
        n = len(grads)
        self.out_shape = [jax.ShapeDtypeStruct((N_DEV, g.shape[0] // N_DEV, g.shape[1]), g.dtype) for g in grads]
        self.scratch = [pltpu.SemaphoreType.DMA((n, N_DEV - 1)), pltpu.SemaphoreType.DMA((n, N_DEV - 1)),
                        pltpu.SemaphoreType.DMA((n,))]

    def _plan(self, src, dst, send, recv, loc):
        x, y, c, me = _my_position()
        copies = []
        for k, grad in enumerate(self.operands):
            rows = grad.shape[0] // N_DEV
            copies.append(pltpu.make_async_copy(src[k].at[pl.ds(me * rows, rows), :], dst[k].at[me], loc.at[k]))
            for p in range(1, N_DEV):
                px, py, pc = _peer(x, y, c, p)
                copies.append(pltpu.make_async_remote_copy(
                    src_ref=src[k].at[pl.ds((4 * px + 2 * py + pc) * rows, rows), :], dst_ref=dst[k].at[me],
                    send_sem=send.at[k, p - 1], recv_sem=recv.at[k, p - 1], device_id=(px, py, pc), device_id_type=MESH))
        return copies

    def start(self, src, dst, send, recv, loc):
        for cp in self._plan(src, dst, send, recv, loc):
            cp.start()

    def forward(self, src, dst, send, recv, loc):
        pass

    def finish(self, src, dst, send, recv, loc):
        for cp in self._plan(src, dst, send, recv, loc):
            cp.wait()


class _Exchanges:
    def __init__(self, parts):
        self.parts = list(parts)
        self.operands = [op for part in self.parts for op in part.operands]
        self.out_shape = [shp for part in self.parts for shp in part.out_shape]
        self.scratch = [scr for part in self.parts for scr in part.scratch]

    def _each(self, src, dst, sems):
        at, sem_at = 0, 0
        for part in self.parts:
            n, n_sem = len(part.operands), len(part.scratch)
            yield part, src[at:at + n], dst[at:at + n], sems[sem_at:sem_at + n_sem]
            at, sem_at = at + n, sem_at + n_sem

    def start(self, src, dst, *sems):
        for part, part_src, part_dst, part_sems in self._each(src, dst, sems):
            part.start(part_src, part_dst, *part_sems)

    def forward(self, src, dst, *sems):
        for part, part_src, part_dst, part_sems in self._each(src, dst, sems):
            part.forward(part_src, part_dst, *part_sems)

    def finish(self, src, dst, *sems):
        for part, part_src, part_dst, part_sems in self._each(src, dst, sems):
            part.finish(part_src, part_dst, *part_sems)


_ANY = pl.BlockSpec(memory_space=pl.ANY)


def _call(body, *, name, grid, in_specs, out_specs, out_shape, operands, scratch_shapes=(), parallel=False, comm=None):
    n_axes = len(grid)
    if comm is None:
        outs = pl.pallas_call(body, name=name, grid=grid, out_shape=list(out_shape), in_specs=list(in_specs),
                              out_specs=list(out_specs), scratch_shapes=list(scratch_shapes),
                              compiler_params=_params(n_axes, parallel))(*operands)
        return list(outs), None
    n_in, n_out, n_scr, n_c = len(in_specs), len(out_specs), len(scratch_shapes), len(comm.operands)
    total = math.prod(grid)

    def hosted(*refs):
        ins, c_src = refs[:n_in], refs[n_in:n_in + n_c]
        outs, c_dst = refs[n_in + n_c:n_in + n_c + n_out], refs[n_in + n_c + n_out:n_in + 2 * n_c + n_out]
        scr, sems = refs[n_in + 2 * n_c + n_out:n_in + 2 * n_c + n_out + n_scr], refs[n_in + 2 * n_c + n_out + n_scr:]
        step = pl.program_id(0)
        for axis in range(1, n_axes):
            step = step * grid[axis] + pl.program_id(axis)

        @pl.when(step == 0)
        def _():
            comm.start(c_src, c_dst, *sems)

        @pl.when(step == max(total - FORWARD_STEPS, 0))
        def _():
            comm.forward(c_src, c_dst, *sems)

        body(*ins, *outs, *scr)

        @pl.when(step == total - 1)
        def _():
            comm.finish(c_src, c_dst, *sems)

    res = pl.pallas_call(hosted, name=name, grid=grid, out_shape=list(out_shape) + comm.out_shape,
                         in_specs=list(in_specs) + [_ANY] * n_c, out_specs=list(out_specs) + [_ANY] * n_c,
                         scratch_shapes=list(scratch_shapes) + comm.scratch,
                         compiler_params=_params(n_axes, False))(*operands, *comm.operands)
    return list(res[:n_out]), list(res[n_out:])


def _all_gather_small(v, name):
    rows = v.shape[0]

    def body(v_ref, all_ref, sum_ref, send_sems, recv_sems):
        x, y, c, me = _my_position()
        all_ref[me] = v_ref[...]
        copies = []
        for p in range(1, N_DEV):
            cp = pltpu.make_async_remote_copy(
                src_ref=v_ref, dst_ref=all_ref.at[me], send_sem=send_sems.at[p - 1], recv_sem=recv_sems.at[p - 1],
                device_id=_peer(x, y, c, p), device_id_type=MESH)
            cp.start()
            copies.append(cp)
        for cp in copies:
            cp.wait()
        acc = all_ref[0]
        for d in range(1, N_DEV):
            acc = acc + all_ref[d]
        sum_ref[...] = acc

    return pl.pallas_call(
        body, name=name,
        out_shape=[jax.ShapeDtypeStruct((N_DEV, rows, LANES), F32), jax.ShapeDtypeStruct((rows, LANES), F32)],
        in_specs=[pl.BlockSpec(memory_space=pltpu.VMEM)],
        out_specs=[pl.BlockSpec(memory_space=pltpu.VMEM)] * 2,
        scratch_shapes=[pltpu.SemaphoreType.DMA((N_DEV - 1,)), pltpu.SemaphoreType.DMA((N_DEV - 1,))],
        compiler_params=pltpu.CompilerParams(vmem_limit_bytes=VMEM_LIMIT),
    )(v)


def _sum_gathered(gathered, name):
    rows = gathered.shape[1]

    def body(g_ref, o_ref):
        acc = g_ref[0]
        for d in range(1, N_DEV):
            acc = acc + g_ref[d]
        o_ref[...] = acc

    return pl.pallas_call(
        body, name=name, out_shape=jax.ShapeDtypeStruct((rows, LANES), F32),
        in_specs=[pl.BlockSpec(memory_space=pltpu.VMEM)], out_specs=pl.BlockSpec(memory_space=pltpu.VMEM),
        compiler_params=pltpu.CompilerParams(vmem_limit_bytes=VMEM_LIMIT),
    )(gathered)


def _pack_small(parts):
    flat = jnp.concatenate([p.reshape(-1).astype(F32) for p in parts])
    total = flat.shape[0]
    padded = -(-total // (8 * LANES)) * (8 * LANES)
    flat = jnp.pad(flat, (0, padded - total))
    return flat.reshape(padded // LANES, LANES)


def _unpack_small(packed, shapes, lead=()):
    flat = packed.reshape(lead + (-1,))
    out, off = [], 0
    for shp in shapes:
        size = math.prod(shp)
        out.append(flat[..., off:off + size].reshape(lead + tuple(shp)))
        off += size
    return out


def _prologue(c_rows, convw_rows, ada_w, ada_b_cols, gather):
    depth, _, cols = ada_w.shape
    n_c = len(gather.operands)
    sub = 8

    def body(c_ref, cw_ref, b_ref, w_hbm, *rest):
        g_src, (c_all_ref, cw_all_ref, ada_ref), g_dst = rest[:n_c], rest[n_c:n_c + 3], rest[n_c + 3:2 * n_c + 3]
        ada_local, w_ref, w_sem, send_sems, recv_sems = rest[2 * n_c + 3:2 * n_c + 8]
        g_sems = rest[2 * n_c + 8:]
        x, y, c, me = _my_position()
        gather.start(g_src, g_dst, *g_sems)
        load_w = pltpu.make_async_copy(w_hbm, w_ref, w_sem)
        load_w.start()

        def to_all(k, src_ref, dst_ref):
            copies = []
            for p in range(1, N_DEV):
                copies.append(pltpu.make_async_remote_copy(
                    src_ref=src_ref, dst_ref=dst_ref.at[me], send_sem=send_sems.at[k, p - 1], recv_sem=recv_sems.at[k, p - 1],
                    device_id=_peer(x, y, c, p), device_id_type=MESH))
            return copies

        first = to_all(0, c_ref, c_all_ref) + to_all(1, cw_ref, cw_all_ref)
        c_all_ref[me] = c_ref[...]
        cw_all_ref[me] = cw_ref[...]
        for cp in first:
            cp.start()
        for cp in first:
            cp.wait()
        cv = c_all_ref[...].reshape(N_DEV * sub, D_MODEL)
        act = (cv * _sigmoid(cv)).astype(BF16)
        load_w.wait()
        for l in range(depth):
            ada_local[l] = _dot(act, w_ref[l].astype(BF16)) + b_ref[l]
        ada_ref[me] = ada_local[:, pl.ds(pl.multiple_of(me * sub, sub), sub), :]
        rows_out = []
        for p in range(1, N_DEV):
            px, py, pc = _peer(x, y, c, p)
            rows = pl.ds(pl.multiple_of((4 * px + 2 * py + pc) * sub, sub), sub)
            rows_out.append(pltpu.make_async_remote_copy(
                src_ref=ada_local.at[:, rows, :], dst_ref=ada_ref.at[me], send_sem=send_sems.at[2, p - 1],
                recv_sem=recv_sems.at[2, p - 1], device_id=(px, py, pc), device_id_type=MESH))
        for cp in rows_out:
            cp.start()
        for cp in rows_out:
            cp.wait()
        gather.forward(g_src, g_dst, *g_sems)
        gather.finish(g_src, g_dst, *g_sems)

    vmem = pl.BlockSpec(memory_space=pltpu.VMEM)
    outs = pl.pallas_call(
        body, name="prologue",
        out_shape=[jax.ShapeDtypeStruct((N_DEV, sub, D_MODEL), F32), jax.ShapeDtypeStruct((N_DEV, sub, LANES), F32),
                   jax.ShapeDtypeStruct((N_DEV, depth, sub, cols), F32)] + gather.out_shape,
        in_specs=[vmem] * 3 + [_ANY] * (1 + n_c), out_specs=[vmem] * 3 + [_ANY] * n_c,
        scratch_shapes=[pltpu.VMEM((depth, N_DEV * sub, cols), F32), pltpu.VMEM(ada_w.shape, F32), pltpu.SemaphoreType.DMA,
                        pltpu.SemaphoreType.DMA((3, N_DEV - 1)), pltpu.SemaphoreType.DMA((3, N_DEV - 1))] + gather.scratch,
        compiler_params=pltpu.CompilerParams(vmem_limit_bytes=VMEM_LIMIT),
    )(c_rows, convw_rows, ada_b_cols, ada_w, *gather.operands)
    return outs[0], outs[1], outs[2], list(outs[3:])


def _ada_backward(c_all, d_ada_cols, d_ada_all):
    nb = c_all.shape[0]
    cols = d_ada_cols.shape[2]
    full = d_ada_all.shape[2]

    def body(c_ref, dc_ref, da_ref, gw_ref, gb_ref):
        cv = c_ref[...]
        act = (cv * _sigmoid(cv)).astype(BF16)
        gw_ref[0] = _dot_tn(act, dc_ref[0].astype(BF16))
        gb_ref[0] = jnp.sum(da_ref[0], axis=0, keepdims=True)

    return pl.pallas_call(
        body, name="ada_backward", grid=(DEPTH,),
        out_shape=[jax.ShapeDtypeStruct((DEPTH, D_MODEL, cols), F32), jax.ShapeDtypeStruct((DEPTH, 1, full), F32)],
        in_specs=[pl.BlockSpec((nb, D_MODEL), lambda l: (0, 0)),
                  pl.BlockSpec((1, nb, cols), lambda l: (l, 0, 0)),
                  pl.BlockSpec((1, nb, full), lambda l: (l, 0, 0))],
        out_specs=[pl.BlockSpec((1, D_MODEL, cols), lambda l: (l, 0, 0)),
                   pl.BlockSpec((1, 1, full), lambda l: (l, 0, 0))],
        compiler_params=_params(),
    )(c_all, d_ada_cols, d_ada_all)


def _rms(xv):
    return lax.rsqrt(jnp.mean(xv * xv, axis=-1, keepdims=True) + EPS)


def _normmod_matmul(x, gnorm, scale1p, shift, w_t, seq, name, comm=None):
    tokens, width = x.shape[0], w_t.shape[0] // 2
    tm = _tile_rows(seq)
    per_seq = seq // tm
    n_chunks = width // MXU_N

    def body(x_ref, g_ref, sc_ref, sh_ref, w_ref, h_ref, act_ref, silu_ref, dact_ref):
        xv = x_ref[...]
        h = (xv * _rms(xv) * g_ref[...]) * sc_ref[0] + sh_ref[0]
        h_ref[...] = h.astype(BF16)
        for ck in range(n_chunks):
            cs = slice(ck * MXU_N, (ck + 1) * MXU_N)
            g = _dot_nt(h_ref[...], w_ref[cs, :])
            u = _dot_nt(h_ref[...], w_ref[width + ck * MXU_N:width + (ck + 1) * MXU_N, :])
            sig = _sigmoid(g)
            silu = g * sig
            act_ref[:, cs] = (silu * u).astype(BF16)
            silu_ref[:, cs] = silu.astype(BF16)
            dact_ref[:, cs] = (u * (sig + silu * (1.0 - sig))).astype(BF16)

    per_batch = pl.BlockSpec((1, 1, D_MODEL), lambda i: (i // per_seq, 0, 0))
    outs, got = _call(
        body, name=name, grid=(tokens // tm,),
        out_shape=[jax.ShapeDtypeStruct((tokens, D_MODEL), BF16)] + [jax.ShapeDtypeStruct((tokens, width), BF16)] * 3,
        in_specs=[pl.BlockSpec((tm, D_MODEL), lambda i: (i, 0)), _resident((1, D_MODEL)), per_batch, per_batch,
                  _resident(w_t.shape)],
        out_specs=[pl.BlockSpec((tm, D_MODEL), lambda i: (i, 0))] + [pl.BlockSpec((tm, width), lambda i: (i, 0))] * 3,
        operands=(x, gnorm, scale1p, shift, w_t), parallel=True, comm=comm)
    return (*outs, got)


def _matmul_residual(src, w, x, gate, scale, seq, name, comm=None):
    tokens, k_dim = x.shape[0], w.shape[0]
    tm = _tile_rows(seq)
    per_seq = seq // tm

    def body(s_ref, w_ref, x_ref, gate_ref, xo_ref, f_ref):
        f = _dot(s_ref[...], w_ref[...])
        f_ref[...] = f.astype(BF16)
        xo_ref[...] = x_ref[...] + (scale * gate_ref[0]) * f

    (x_out, f), got = _call(
        body, name=name, grid=(tokens // tm,),
        out_shape=[jax.ShapeDtypeStruct((tokens, D_MODEL), F32), jax.ShapeDtypeStruct((tokens, D_MODEL), BF16)],
        in_specs=[pl.BlockSpec((tm, k_dim), lambda i: (i, 0)), _resident(w.shape),
                  pl.BlockSpec((tm, D_MODEL), lambda i: (i, 0)),
                  pl.BlockSpec((1, 1, D_MODEL), lambda i: (i // per_seq, 0, 0))],
        out_specs=[pl.BlockSpec((tm, D_MODEL), lambda i: (i, 0))] * 2,
        operands=(src, w, x, gate), parallel=True, comm=comm)
    return x_out, f, got


def _loss_tile(xv, target, gn):
    r = _rms(xv)
    xn = xv * r
    err = xn * gn - target
    loss = (0.5 / D_MODEL) * jnp.sum(err * err, axis=0, keepdims=True)
    dyv = err * (1.0 / D_MODEL)
    dg = jnp.sum(dyv * xn, axis=0, keepdims=True)
    dxn = dyv * gn
    dx = r * (dxn - xn * jnp.mean(dxn * xn, axis=-1, keepdims=True))
    return loss, dx, dg


def _ffn_forward(x, gnorm, scale1p, shift, w_gu_t, w_down, gate, scale, seq, name, loss_head=None, comm=None):
    tokens, width = x.shape[0], w_down.shape[0]
    tm = _tile_rows(seq)
    per_seq = seq // tm
    n_chunks = width // MXU_N

    def body(x_ref, g_ref, sc_ref, sh_ref, wgu_ref, wd_ref, gate_ref, *rest):
        if loss_head:
            t_ref, gf_ref, h_ref, act_ref, silu_ref, dact_ref, xo_ref, f_ref, dgf_ref, loss_ref = rest
        else:
            h_ref, act_ref, silu_ref, dact_ref, xo_ref, f_ref = rest
        xv = x_ref[...]
        h = (xv * _rms(xv) * g_ref[...]) * sc_ref[0] + sh_ref[0]
        h_ref[...] = h.astype(BF16)
        for ck in range(n_chunks):
            cs = slice(ck * MXU_N, (ck + 1) * MXU_N)
            g = _dot_nt(h_ref[...], wgu_ref[cs, :])
            u = _dot_nt(h_ref[...], wgu_ref[width + ck * MXU_N:width + (ck + 1) * MXU_N, :])
            sig = _sigmoid(g)
            silu = g * sig
            act_ref[:, cs] = (silu * u).astype(BF16)
            silu_ref[:, cs] = silu.astype(BF16)
            dact_ref[:, cs] = (u * (sig + silu * (1.0 - sig))).astype(BF16)
        f = _dot(act_ref[...], wd_ref[...])
        f_ref[...] = f.astype(BF16)
        x_out = xv + (scale * gate_ref[0]) * f
        if loss_head:
            i = pl.program_id(0)
            loss, dx, dg = _loss_tile(x_out, t_ref[...], gf_ref[...])
            xo_ref[...] = dx

            @pl.when(i == 0)
            def _():
                dgf_ref[...] = dg
                loss_ref[...] = loss

            @pl.when(i != 0)
            def _():
                dgf_ref[...] = dgf_ref[...] + dg
                loss_ref[...] = loss_ref[...] + loss
        else:
            xo_ref[...] = x_out

    row = lambda i: (i, 0)
    per_batch = pl.BlockSpec((1, 1, D_MODEL), lambda i: (i // per_seq, 0, 0))
    tile = lambda cols: pl.BlockSpec((tm, cols), row)
    wide = jax.ShapeDtypeStruct((tokens, width), BF16)
    fixed = pl.BlockSpec((1, D_MODEL), lambda i: (0, 0))
    vec = jax.ShapeDtypeStruct((1, D_MODEL), F32)
    outs, got = _call(
        body, name=name, grid=(tokens // tm,),
        out_shape=[jax.ShapeDtypeStruct((tokens, D_MODEL), BF16), wide, wide, wide,
                   jax.ShapeDtypeStruct((tokens, D_MODEL), F32), jax.ShapeDtypeStruct((tokens, D_MODEL), BF16)]
        + ([vec, vec] if loss_head else []),
        in_specs=[tile(D_MODEL), _resident((1, D_MODEL)), per_batch, per_batch, _resident(w_gu_t.shape),
                  _resident(w_down.shape), per_batch] + ([tile(D_MODEL), _resident((1, D_MODEL))] if loss_head else []),
        out_specs=[tile(D_MODEL), tile(width), tile(width), tile(width), tile(D_MODEL), tile(D_MODEL)]
        + ([fixed, fixed] if loss_head else []),
        operands=(x, gnorm, scale1p, shift, w_gu_t, w_down, gate) + (tuple(loss_head) if loss_head else ()),
        parallel=not loss_head, comm=comm)
    return (*outs, got)


def _residual_backward(dy, gate, f, w, scale, silu, dact, seq, name, comm=None):
    tokens, k_dim = dy.shape[0], w.shape[0]
    batch = tokens // seq
    tm = _tile_rows(seq)
    per_seq = seq // tm
    n_chunks = k_dim // MXU_N

    def body(dy_ref, gate_ref, f_ref, silu_ref, dact_ref, w_ref, df_ref, dgate_ref, dgu_ref):
        i = pl.program_id(0)
        dy_v = dy_ref[...]
        df_ref[...] = ((scale * gate_ref[0]) * dy_v).astype(BF16)
        part = scale * jnp.sum(dy_v * f_ref[...].astype(F32), axis=0, keepdims=True)

        @pl.when(i % per_seq == 0)
        def _():
            dgate_ref[0] = part

        @pl.when(i % per_seq != 0)
        def _():
            dgate_ref[0] = dgate_ref[0] + part

        for ck in range(n_chunks):
            cs = slice(ck * MXU_N, (ck + 1) * MXU_N)
            cu = slice(k_dim + ck * MXU_N, k_dim + (ck + 1) * MXU_N)
            da = _dot_nt(df_ref[...], w_ref[cs, :])
            dgu_ref[:, cs] = (da * dact_ref[:, cs].astype(F32)).astype(BF16)
            dgu_ref[:, cu] = (da * silu_ref[:, cs].astype(F32)).astype(BF16)

    row = lambda i: (i, 0)
    per_batch = pl.BlockSpec((1, 1, D_MODEL), lambda i: (i // per_seq, 0, 0))
    tile = lambda cols: pl.BlockSpec((tm, cols), row)
    outs, got = _call(
        body, name=name, grid=(tokens // tm,),
        out_shape=[jax.ShapeDtypeStruct((tokens, D_MODEL), BF16), jax.ShapeDtypeStruct((batch, 1, D_MODEL), F32),
                   jax.ShapeDtypeStruct((tokens, 2 * k_dim), BF16)],
        in_specs=[tile(D_MODEL), per_batch, tile(D_MODEL), tile(k_dim), tile(k_dim), _resident(w.shape)],
        out_specs=[tile(D_MODEL), per_batch, tile(2 * k_dim)],
        operands=(dy, gate, f, silu, dact, w), comm=comm)
    return (*outs, got)


def _matmul_normmod_backward(dsrc, w_t, x, dy, gnorm, scale1p, seq, name, comm=None):
    tokens, k_dim = dsrc.shape
    batch = tokens // seq
    tm = _tile_rows(seq)
    per_seq = seq // tm

    def body(ds_ref, w_ref, x_ref, dy_ref, g_ref, sc_ref, dx_ref, dsh_ref, dsc_ref, dg_ref):
        i = pl.program_id(0)
        dh = _dot(ds_ref[...], w_ref[...])
        xv = x_ref[...]
        r = _rms(xv)
        xn = xv * r
        gn = g_ref[...]
        dsh = jnp.sum(dh, axis=0, keepdims=True)
        dsc = jnp.sum(dh * (xn * gn), axis=0, keepdims=True)
        dhn = dh * sc_ref[0]
        dg = jnp.sum(dhn * xn, axis=0, keepdims=True)
        dxn = dhn * gn
        dx_ref[...] = dy_ref[...] + r * (dxn - xn * jnp.mean(dxn * xn, axis=-1, keepdims=True))

        @pl.when(i % per_seq == 0)
        def _():
            dsh_ref[0] = dsh
            dsc_ref[0] = dsc

        @pl.when(i % per_seq != 0)
        def _():
            dsh_ref[0] = dsh_ref[0] + dsh
            dsc_ref[0] = dsc_ref[0] + dsc

        @pl.when(i == 0)
        def _():
            dg_ref[...] = dg

        @pl.when(i != 0)
        def _():
            dg_ref[...] = dg_ref[...] + dg

    row = lambda i: (i, 0)
    per_batch = pl.BlockSpec((1, 1, D_MODEL), lambda i: (i // per_seq, 0, 0))
    outs, got = _call(
        body, name=name, grid=(tokens // tm,),
        out_shape=[jax.ShapeDtypeStruct((tokens, D_MODEL), F32), jax.ShapeDtypeStruct((batch, 1, D_MODEL), F32),
                   jax.ShapeDtypeStruct((batch, 1, D_MODEL), F32), jax.ShapeDtypeStruct((1, D_MODEL), F32)],
        in_specs=[pl.BlockSpec((tm, k_dim), row), _resident(w_t.shape), pl.BlockSpec((tm, D_MODEL), row),
                  pl.BlockSpec((tm, D_MODEL), row), _resident((1, D_MODEL)), per_batch],
        out_specs=[pl.BlockSpec((tm, D_MODEL), row), per_batch, per_batch, pl.BlockSpec((1, D_MODEL), lambda i: (0, 0))],
        operands=(dsrc, w_t, x, dy, gnorm, scale1p), comm=comm)
    return (*outs, got)


def _ffn_backward(dy, gate, f, silu, dact, w_down, w_gu_t, x, gnorm, scale1p, scale, seq, name, comm=None):
    tokens, k_dim = dy.shape[0], w_down.shape[0]
    batch = tokens // seq
    tm = min(256, seq)
    per_seq = seq // tm
    n_chunks = k_dim // MXU_N

    def body(dy_ref, gate_ref, f_ref, silu_ref, dact_ref, wd_ref, wgu_ref, x_ref, g_ref, sc_ref,
             df_ref, dgate_ref, dgu_ref, dx_ref, dsh_ref, dsc_ref, dg_ref):
        i = pl.program_id(0)
        dy_v = dy_ref[...]
        df_ref[...] = ((scale * gate_ref[0]) * dy_v).astype(BF16)
        dgate = scale * jnp.sum(dy_v * f_ref[...].astype(F32), axis=0, keepdims=True)
        for ck in range(n_chunks):
            cs = slice(ck * MXU_N, (ck + 1) * MXU_N)
            cu = slice(k_dim + ck * MXU_N, k_dim + (ck + 1) * MXU_N)
            da = _dot_nt(df_ref[...], wd_ref[cs, :])
            dgu_ref[:, cs] = (da * dact_ref[:, cs].astype(F32)).astype(BF16)
            dgu_ref[:, cu] = (da * silu_ref[:, cs].astype(F32)).astype(BF16)
        dh = _dot(dgu_ref[...], wgu_ref[...])
        xv = x_ref[...]
        r = _rms(xv)
        xn = xv * r
        gn = g_ref[...]
        dsh = jnp.sum(dh, axis=0, keepdims=True)
        dsc = jnp.sum(dh * (xn * gn), axis=0, keepdims=True)
        dhn = dh * sc_ref[0]
        dg = jnp.sum(dhn * xn, axis=0, keepdims=True)
        dxn = dhn * gn
        dx_ref[...] = dy_v + r * (dxn - xn * jnp.mean(dxn * xn, axis=-1, keepdims=True))

        @pl.when(i % per_seq == 0)
        def _():
            dgate_ref[0] = dgate
            dsh_ref[0] = dsh
            dsc_ref[0] = dsc

        @pl.when(i % per_seq != 0)
        def _():
            dgate_ref[0] = dgate_ref[0] + dgate
            dsh_ref[0] = dsh_ref[0] + dsh
            dsc_ref[0] = dsc_ref[0] + dsc

        @pl.when(i == 0)
        def _():
            dg_ref[...] = dg

        @pl.when(i != 0)
        def _():
            dg_ref[...] = dg_ref[...] + dg

    row = lambda i: (i, 0)
    per_batch = pl.BlockSpec((1, 1, D_MODEL), lambda i: (i // per_seq, 0, 0))
    tile = lambda width: pl.BlockSpec((tm, width), row)
    vec = jax.ShapeDtypeStruct((batch, 1, D_MODEL), F32)
    outs, got = _call(
        body, name=name, grid=(tokens // tm,),
        out_shape=[jax.ShapeDtypeStruct((tokens, D_MODEL), BF16), vec, jax.ShapeDtypeStruct((tokens, 2 * k_dim), BF16),
                   jax.ShapeDtypeStruct((tokens, D_MODEL), F32), vec, vec, jax.ShapeDtypeStruct((1, D_MODEL), F32)],
        in_specs=[tile(D_MODEL), per_batch, tile(D_MODEL), tile(k_dim), tile(k_dim), _resident(w_down.shape),
                  _resident(w_gu_t.shape), tile(D_MODEL), _resident((1, D_MODEL)), per_batch],
        out_specs=[tile(D_MODEL), per_batch, tile(2 * k_dim), tile(D_MODEL), per_batch, per_batch,
                   pl.BlockSpec((1, D_MODEL), lambda i: (0, 0))],
        operands=(dy, gate, f, silu, dact, w_down, w_gu_t, x, gnorm, scale1p), comm=comm)
    return (*outs, got)


def _weight_grad(a, b, seq, name, comm=None):
    tokens, n_out = a.shape
    tn = MXU_N

    def body(a_ref, b_ref, o_ref):
        o_ref[...] = _dot_tn(a_ref[...], b_ref[...]).astype(BF16)

    (out,), got = _call(
        body, name=name, grid=(n_out // tn,),
        out_shape=[jax.ShapeDtypeStruct((n_out, D_MODEL), BF16)],
        in_specs=[pl.BlockSpec((tokens, tn), lambda j: (0, j)), _resident((tokens, D_MODEL))],
        out_specs=[pl.BlockSpec((tn, D_MODEL), lambda j: (j, 0))],
        operands=(a, b), parallel=True, comm=comm)
    return out, got


def _group_mean(v, bd):
    hi = v.astype(BF16)
    lo = (v - hi.astype(F32)).astype(BF16)
    blocks = [slice(k * MXU_N, (k + 1) * MXU_N) for k in range(v.shape[1] // MXU_N)]
    return jnp.concatenate([_dot(hi[:, b], bd) + _dot(lo[:, b], bd) for b in blocks], axis=1)


def _sgu_forward(pm_ref, wm_ref, bias_ref, lng_ref, lnb_ref, bd_ref, mixed_scr, n_sub):
    ua = pm_ref[:, 0:D_A].astype(F32)
    va = pm_ref[:, D_A:2 * D_A].astype(F32)
    u_act = _gelu(ua)
    v_act = _gelu(va)
    bd = bd_ref[...]
    vc = v_act - _group_mean(v_act, bd)
    rstd = lax.rsqrt(_group_mean(vc * vc, bd) + EPS)
    vhat = vc * rstd
    vln = vhat * lng_ref[...] + lnb_ref[...]
    left = lax.broadcasted_iota(jnp.int32, (CHUNK, LANES), 1) < HEAD_DIM
    for q in range(n_sub):
        rows = slice(q * CHUNK, (q + 1) * CHUNK)
        for p in range(N_HEADS // 2):
            cols = slice(p * LANES, (p + 1) * LANES)
            vp = vln[rows, cols]
            stacked = jnp.concatenate([jnp.where(left, vp, 0.0), jnp.where(left, 0.0, vp)], axis=0).astype(BF16)
            mixed_scr[rows, cols] = _dot(wm_ref[p], stacked) + bias_ref[:, cols]
    return ua, va, u_act, vhat, rstd, vln


def _halo_specs(tm, tokens, width):
    prev = pl.BlockSpec((HALO, width), lambda i: (jnp.maximum(i * (tm // HALO) - 1, 0), 0))
    nxt = pl.BlockSpec((HALO, width), lambda i: (jnp.minimum((i + 1) * (tm // HALO), tokens // HALO - 1), 0))
    return prev, nxt


def _mixer_forward(x, gnorm, scale1p, shift, w_in_t, gate, w_out, wm, bias_full, lng, lnb, convw, og, bd, seq, name, comm=None):
    tokens = x.shape[0]
    tm = _tile_rows(seq)
    per_seq = seq // tm
    n_sub = tm // CHUNK

    def body(x_ref, xp_ref, g_ref, sc_ref, sh_ref, win_ref, gate_ref, wo_ref, wm_ref, bias_ref, lng_ref, lnb_ref, cw_ref,
             og_ref, bd_ref, h_ref, pm_ref, y_ref, xo_ref, o_ref, mixed_scr):
        i = pl.program_id(0)
        first = (i % per_seq) == 0
        xv = x_ref[...]
        h_ref[...] = ((xv * _rms(xv) * g_ref[...]) * sc_ref[0] + sh_ref[0]).astype(BF16)
        for ck in range(D_PROJ // MXU_N):
            cs = slice(ck * MXU_N, (ck + 1) * MXU_N)
            pm_ref[:, cs] = _dot_nt(h_ref[...], win_ref[cs, :]).astype(BF16)
        xp = xp_ref[...]
        hp = ((xp * _rms(xp) * g_ref[...]) * sc_ref[0] + sh_ref[0]).astype(BF16)
        gates_prev = _dot_nt(hp, win_ref[3 * D_A:5 * D_A, :]).astype(BF16).astype(F32)

        _, _, u_act, _, _, _ = _sgu_forward(pm_ref, wm_ref, bias_ref, lng_ref, lnb_ref, bd_ref, mixed_scr, n_sub)
        ya = u_act * mixed_scr[...]
        y_ref[:, 0:D_A] = (ya * _rms(ya) * og_ref[:, 0:D_A]).astype(BF16)

        bg = pm_ref[:, 2 * D_A:3 * D_A].astype(F32)
        z = pm_ref[:, 3 * D_A:4 * D_A].astype(F32) * pm_ref[:, 4 * D_A:5 * D_A].astype(F32)
        zp = jnp.where(first, 0.0, gates_prev[:, 0:D_A] * gates_prev[:, D_A:2 * D_A])
        zext = jnp.concatenate([zp, z], axis=0)
        z1 = pltpu.roll(zext, 1, 0)[HALO:]
        z2 = pltpu.roll(zext, 2, 0)[HALO:]
        conv = cw_ref[0:1, :] * z2 + cw_ref[1:2, :] * z1 + cw_ref[2:3, :] * z
        yb = bg * conv
        y_ref[:, D_A:2 * D_A] = (yb * _rms(yb) * og_ref[:, D_A:2 * D_A]).astype(BF16)

        f = _dot(y_ref[...], wo_ref[...])
        o_ref[...] = f.astype(BF16)
        xo_ref[...] = xv + gate_ref[0] * f

    prev, _ = _halo_specs(tm, tokens, D_MODEL)
    tile = pl.BlockSpec((tm, D_MODEL), lambda i: (i, 0))
    per_batch = pl.BlockSpec((1, 1, D_MODEL), lambda i: (i // per_seq, 0, 0))
    bf = lambda cols: jax.ShapeDtypeStruct((tokens, cols), BF16)
    outs, got = _call(
        body, name=name, grid=(tokens // tm,),
        out_shape=[bf(D_MODEL), bf(D_PROJ), bf(D_MODEL), jax.ShapeDtypeStruct((tokens, D_MODEL), F32), bf(D_MODEL)],
        in_specs=[tile, prev, _resident((1, D_MODEL)), per_batch, per_batch, _resident(w_in_t.shape), per_batch,
                  _resident(w_out.shape), _resident(wm.shape), _resident(bias_full.shape), _resident(lng.shape),
                  _resident(lnb.shape), _resident(convw.shape), _resident(og.shape), _resident(bd.shape)],
        out_specs=[tile, pl.BlockSpec((tm, D_PROJ), lambda i: (i, 0)), tile, tile, tile],
        scratch_shapes=[pltpu.VMEM((tm, D_A), F32)],
        operands=(x, x, gnorm, scale1p, shift, w_in_t, gate, w_out, wm, bias_full, lng, lnb, convw, og, bd),
        parallel=True, comm=comm)
    return (*outs, got)


def _mixer_backward(proj, dx, gate, o, w_out, x, gnorm, scale1p, w_in_t, wm, bias_full, lng, lnb, convw, og, bd, wm_rows,
                    causal, seq, name, comm=None):
    tokens = proj.shape[0]
    batch = tokens // seq
    tm = _tile_rows(seq)
    per_seq = seq // tm
    n_sub = tm // CHUNK
    ext = tm + 2 * HALO

    def body(pm_ref, pp_ref, pn_ref, dx_ref, dxn_ref, gate_ref, o_ref, wo_ref, x_ref, g_ref, sc_ref, win_ref, wm_ref, bias_ref,
             lng_ref, lnb_ref, cw_ref, og_ref, bd_ref, wmr_ref, causal_ref, do_ref, dgate_ref, dp_ref, dxo_ref, dsh_ref, dsc_ref,
             dgn_ref,
             dog_ref, dcw_ref, dlng_ref, dlnb_ref, dbias_ref, dwm_ref, mixed_scr, dvln_scr, dy_scr):
        i = pl.program_id(0)
        first = (i % per_seq) == 0
        last = (i % per_seq) == per_seq - 1

        dx_v = dx_ref[...]
        do_ref[...] = (gate_ref[0] * dx_v).astype(BF16)
        dgate = jnp.sum(dx_v * o_ref[...].astype(F32), axis=0, keepdims=True)
        dy_scr[...] = _dot_nt(do_ref[...], wo_ref[...])
        dyn_conv = _dot_nt((gate_ref[0] * dxn_ref[...]).astype(BF16), wo_ref[D_A:2 * D_A, :])

        @pl.when(i == 0)
        def _():
            dog_ref[...] = jnp.zeros_like(dog_ref)
            dcw_ref[...] = jnp.zeros_like(dcw_ref)
            dlng_ref[...] = jnp.zeros_like(dlng_ref)
            dlnb_ref[...] = jnp.zeros_like(dlnb_ref)
            dbias_ref[...] = jnp.zeros_like(dbias_ref)
            dwm_ref[...] = jnp.zeros_like(dwm_ref)

        ua, va, u_act, vhat, rstd, vln = _sgu_forward(pm_ref, wm_ref, bias_ref, lng_ref, lnb_ref, bd_ref, mixed_scr, n_sub)
        mixed = mixed_scr[...]
        ya = u_act * mixed
        ra = _rms(ya)
        yhat = ya * ra
        dya_in = dy_scr[:, 0:D_A]
        dog_ref[:, 0:D_A] = dog_ref[:, 0:D_A] + jnp.sum(dya_in * yhat, axis=0, keepdims=True)
        dyh = dya_in * og_ref[:, 0:D_A]
        dya = ra * (dyh - yhat * jnp.mean(dyh * yhat, axis=-1, keepdims=True))
        d_u = dya * mixed
        d_mixed = dya * u_act
        left = lax.broadcasted_iota(jnp.int32, (CHUNK, LANES), 1) < HEAD_DIM
        dbias = jnp.zeros((CHUNK, D_A), F32)
        for q in range(n_sub):
            rows = slice(q * CHUNK, (q + 1) * CHUNK)
            dbias = dbias + d_mixed[rows, :]
            for p in range(N_HEADS // 2):
                cols = slice(p * LANES, (p + 1) * LANES)
                dm = d_mixed[rows, cols]
                stacked = jnp.concatenate([jnp.where(left, dm, 0.0), jnp.where(left, 0.0, dm)], axis=0).astype(BF16)
                dw = _dot_nt(stacked, vln[rows, cols].astype(BF16))
                dwm_ref[2 * p] = dwm_ref[2 * p] + causal_ref[...] * dw[0:CHUNK]
                dwm_ref[2 * p + 1] = dwm_ref[2 * p + 1] + causal_ref[...] * dw[CHUNK:2 * CHUNK]
                dvln_scr[rows, cols] = _dot_tn(wmr_ref[p], stacked)
        dbias_ref[...] = dbias_ref[...] + dbias
        dvln = dvln_scr[...]
        dlng_ref[...] = dlng_ref[...] + jnp.sum(dvln * vhat, axis=0, keepdims=True)
        dlnb_ref[...] = dlnb_ref[...] + jnp.sum(dvln, axis=0, keepdims=True)
        dvh = dvln * lng_ref[...]
        bd = bd_ref[...]
        d_v = rstd * (dvh - _group_mean(dvh, bd) - vhat * _group_mean(dvh * vhat, bd))
        dp_ref[:, 0:D_A] = (d_u * _gelu_grad(ua)).astype(BF16)
        dp_ref[:, D_A:2 * D_A] = (d_v * _gelu_grad(va)).astype(BF16)
        dh_a = _dot(dp_ref[:, 0:2 * D_A], win_ref[0:2 * D_A, :])

        def ext_cols(lo):
            cs = slice(lo, lo + D_A)
            return jnp.concatenate([pp_ref[:, cs], pm_ref[:, cs], pn_ref[:, cs]], axis=0).astype(F32)

        bg, cg, xb = ext_cols(2 * D_A), ext_cols(3 * D_A), ext_cols(4 * D_A)
        row = lax.broadcasted_iota(jnp.int32, (ext, D_A), 0)
        z = jnp.where(jnp.logical_and(first, row < HALO), 0.0, cg * xb)
        z1 = pltpu.roll(z, 1, 0)
        z2 = pltpu.roll(z, 2, 0)
        w0, w1, w2 = cw_ref[0:1, :], cw_ref[1:2, :], cw_ref[2:3, :]
        conv = w0 * z2 + w1 * z1 + w2 * z
        yb = bg * conv
        rb = _rms(yb)
        yhb = yb * rb
        dyn = jnp.where(last, 0.0, dyn_conv)
        dyb_in = jnp.concatenate([jnp.zeros((HALO, D_A), F32), dy_scr[:, D_A:2 * D_A], dyn], axis=0)
        dyhb = dyb_in * og_ref[:, D_A:2 * D_A]
        dyb = rb * (dyhb - yhb * jnp.mean(dyhb * yhb, axis=-1, keepdims=True))
        d_conv = dyb * bg
        dz = w2 * d_conv + w1 * pltpu.roll(d_conv, ext - 1, 0) + w0 * pltpu.roll(d_conv, ext - 2, 0)
        main = slice(HALO, HALO + tm)
        dp_ref[:, 2 * D_A:3 * D_A] = (dyb * conv)[main].astype(BF16)
        dp_ref[:, 3 * D_A:4 * D_A] = (dz * xb)[main].astype(BF16)
        dp_ref[:, 4 * D_A:5 * D_A] = (dz * cg)[main].astype(BF16)
        dog_ref[:, D_A:2 * D_A] = dog_ref[:, D_A:2 * D_A] + jnp.sum((dyb_in * yhb)[main], axis=0, keepdims=True)
        dcm = d_conv[main]
        dcw_ref[0:1, :] = dcw_ref[0:1, :] + jnp.sum(dcm * z2[main], axis=0, keepdims=True)
        dcw_ref[1:2, :] = dcw_ref[1:2, :] + jnp.sum(dcm * z1[main], axis=0, keepdims=True)
        dcw_ref[2:3, :] = dcw_ref[2:3, :] + jnp.sum(dcm * z[main], axis=0, keepdims=True)

        dh = dh_a + _dot(dp_ref[:, 2 * D_A:5 * D_A], win_ref[2 * D_A:5 * D_A, :])
        xv = x_ref[...]
        r = _rms(xv)
        xn = xv * r
        gn = g_ref[...]
        dsh = jnp.sum(dh, axis=0, keepdims=True)
        dsc = jnp.sum(dh * (xn * gn), axis=0, keepdims=True)
        dhn = dh * sc_ref[0]
        dgn = jnp.sum(dhn * xn, axis=0, keepdims=True)
        dxn = dhn * gn
        dxo_ref[...] = dx_v + r * (dxn - xn * jnp.mean(dxn * xn, axis=-1, keepdims=True))

        @pl.when(first)
        def _():
            dgate_ref[0] = dgate
            dsh_ref[0] = dsh
            dsc_ref[0] = dsc

        @pl.when(jnp.logical_not(first))
        def _():
            dgate_ref[0] = dgate_ref[0] + dgate
            dsh_ref[0] = dsh_ref[0] + dsh
            dsc_ref[0] = dsc_ref[0] + dsc

        @pl.when(i == 0)
        def _():
            dgn_ref[...] = dgn

        @pl.when(i != 0)
        def _():
            dgn_ref[...] = dgn_ref[...] + dgn

    prev_p, next_p = _halo_specs(tm, tokens, D_PROJ)
    _, next_d = _halo_specs(tm, tokens, D_MODEL)
    fixed2 = lambda shape: pl.BlockSpec(shape, lambda i: (0, 0))
    tile = pl.BlockSpec((tm, D_MODEL), lambda i: (i, 0))
    per_batch = pl.BlockSpec((1, 1, D_MODEL), lambda i: (i // per_seq, 0, 0))
    vec = jax.ShapeDtypeStruct((batch, 1, D_MODEL), F32)
    outs, got = _call(
        body, name=name, grid=(tokens // tm,),
        out_shape=[jax.ShapeDtypeStruct((tokens, D_MODEL), BF16), vec, jax.ShapeDtypeStruct((tokens, D_PROJ), BF16),
                   jax.ShapeDtypeStruct((tokens, D_MODEL), F32), vec, vec, jax.ShapeDtypeStruct((1, D_MODEL), F32),
                   jax.ShapeDtypeStruct((1, D_MODEL), F32), jax.ShapeDtypeStruct((8, D_A), F32),
                   jax.ShapeDtypeStruct((1, D_A), F32), jax.ShapeDtypeStruct((1, D_A), F32),
                   jax.ShapeDtypeStruct((CHUNK, D_A), F32), jax.ShapeDtypeStruct((N_HEADS, CHUNK, CHUNK), F32)],
        in_specs=[pl.BlockSpec((tm, D_PROJ), lambda i: (i, 0)), prev_p, next_p, tile, next_d, per_batch, tile,
                  _resident(w_out.shape), tile, _resident((1, D_MODEL)), per_batch, _resident(w_in_t.shape),
                  _resident(wm.shape), _resident(bias_full.shape), _resident(lng.shape), _resident(lnb.shape),
                  _resident(convw.shape), _resident(og.shape), _resident(bd.shape), _resident(wm_rows.shape),
                  _resident(causal.shape)],
        out_specs=[tile, per_batch, pl.BlockSpec((tm, D_PROJ), lambda i: (i, 0)), tile, per_batch, per_batch,
                   fixed2((1, D_MODEL)), fixed2((1, D_MODEL)), fixed2((8, D_A)), fixed2((1, D_A)), fixed2((1, D_A)),
                   fixed2((CHUNK, D_A)), pl.BlockSpec((N_HEADS, CHUNK, CHUNK), lambda i: (0, 0, 0))],
        scratch_shapes=[pltpu.VMEM((tm, D_A), F32), pltpu.VMEM((tm, D_A), F32), pltpu.VMEM((tm, D_MODEL), F32)],
        operands=(proj, proj, proj, dx, dx, gate, o, w_out, x, gnorm, scale1p, w_in_t, wm, bias_full, lng, lnb, convw, og, bd,
                  wm_rows, causal), comm=comm)
    return (*outs, got)


def _adamw_update(wv, gv, mv, vv):
    nm = ADAM_B1 * mv + (1.0 - ADAM_B1) * gv
    nv = ADAM_B2 * vv + (1.0 - ADAM_B2) * (gv * gv)
    m_hat = nm / (1.0 - ADAM_B1 ** ADAM_STEP)
    v_hat = nv / (1.0 - ADAM_B2 ** ADAM_STEP)
    return -ADAM_LR * (m_hat / (jnp.sqrt(v_hat) + ADAM_EPS) + ADAM_WD * wv), nm, nv


def _adamw_rows(recv, w, m, v, name):
    depth, rows, cols = w.shape
    tr = rows // 2
    last = rows // tr - 1

    def body(*refs):
        r_refs, (w_ref, m_ref, v_ref, g_ref, d_ref, nm_ref, nv_ref) = refs[:depth], refs[depth:]
        for l in range(depth):
            @pl.when(pl.program_id(0) == l)
            def _(r_ref=r_refs[l]):
                acc = r_ref[0].astype(F32)
                for d in range(1, N_DEV):
                    acc = acc + r_ref[d].astype(F32)
                g_ref[0] = acc
                d_ref[0], nm_ref[0], nv_ref[0] = _adamw_update(w_ref[0], acc, m_ref[0], v_ref[0])

    def slots(l):
        return pl.BlockSpec((N_DEV, tr, cols), lambda ll, i: (0, jnp.where(ll == l, i, jnp.where(ll < l, 0, last)), 0))

    spec = pl.BlockSpec((1, tr, cols), lambda ll, i: (ll, i, 0))
    return pl.pallas_call(
        body, name=name, grid=(depth, rows // tr),
        out_shape=[jax.ShapeDtypeStruct((depth, rows, cols), F32)] * 4,
        in_specs=[slots(l) for l in range(depth)] + [spec] * 3, out_specs=[spec] * 4,
        compiler_params=_params(2),
    )(*recv, w, m, v)


def _adamw(w, g, m, v, name):
    rows, cols = w.shape
    tr = max(t for t in range(8, 513, 8) if rows % t == 0)

    def body(w_ref, g_ref, m_ref, v_ref, d_ref, nm_ref, nv_ref):
        d_ref[...], nm_ref[...], nv_ref[...] = _adamw_update(w_ref[...], g_ref[...], m_ref[...], v_ref[...])

    spec = pl.BlockSpec((tr, cols), lambda i: (i, 0))
    return pl.pallas_call(
        body, name=name, grid=(rows // tr,),
        out_shape=[jax.ShapeDtypeStruct((rows, cols), F32)] * 3,
        in_specs=[spec] * 4, out_specs=[spec] * 3,
        compiler_params=_params(parallel=True),
    )(w, g, m, v)


def _adamw_many(ws, gs, ms, vs, name):
    n = len(ws)
    two_d = lambda a: a.reshape(-1, a.shape[-1])

    def body(*refs):
        w_refs, g_refs, m_refs, v_refs = refs[:n], refs[n:2 * n], refs[2 * n:3 * n], refs[3 * n:4 * n]
        d_refs, nm_refs, nv_refs = refs[4 * n:5 * n], refs[5 * n:6 * n], refs[6 * n:]
        for k in range(n):
            d_refs[k][...], nm_refs[k][...], nv_refs[k][...] = _adamw_update(
                w_refs[k][...], g_refs[k][...], m_refs[k][...], v_refs[k][...])

    flat = [two_d(a) for a in ws]
    outs = pl.pallas_call(
        body, name=name, out_shape=[jax.ShapeDtypeStruct(a.shape, F32) for a in flat] * 3,
        in_specs=[pl.BlockSpec(memory_space=pltpu.VMEM)] * (4 * n),
        out_specs=[pl.BlockSpec(memory_space=pltpu.VMEM)] * (3 * n),
        compiler_params=pltpu.CompilerParams(vmem_limit_bytes=VMEM_LIMIT),
    )(*flat, *[two_d(a) for a in gs], *[two_d(a) for a in ms], *[two_d(a) for a in vs])
    shaped = [o.reshape(ws[k % n].shape) for k, o in enumerate(outs)]
    return shaped[:n], shaped[n:2 * n], shaped[2 * n:]


def _adamw_nd(w, g, m, v, name):
    shape = w.shape
    two_d = (-1, shape[-1])
    d, nm, nv = _adamw(w.reshape(two_d), g.reshape(two_d), m.reshape(two_d), v.reshape(two_d), name)
    return d.reshape(shape), nm.reshape(shape), nv.reshape(shape)


def kernel(x, c, ada_w, ada_b, norm_ffn1_g, ffn1_w_gu, ffn1_w_down, norm_mix_g, mix_w_in, sgu_ln_g, sgu_ln_b, sgu_w_s, sgu_b, conv_w, out_norm_g, mix_w_out, norm_ffn2_g, ffn2_w_gu, ffn2_w_down, final_norm_g, loss_target, m_ada_w, m_ada_b, m_norm_ffn1_g, m_ffn1_w_gu, m_ffn1_w_down, m_norm_mix_g, m_mix_w_in, m_sgu_ln_g, m_sgu_ln_b, m_sgu_w_s, m_sgu_b, m_conv_w, m_out_norm_g, m_mix_w_out, m_norm_ffn2_g, m_ffn2_w_gu, m_ffn2_w_down, m_final_norm_g, v_ada_w, v_ada_b, v_norm_ffn1_g, v_ffn1_w_gu, v_ffn1_w_down, v_norm_mix_g, v_mix_w_in, v_sgu_ln_g, v_sgu_ln_b, v_sgu_w_s, v_sgu_b, v_conv_w, v_out_norm_g, v_mix_w_out, v_norm_ffn2_g, v_ffn2_w_gu, v_ffn2_w_down, v_final_norm_g):
    batch, seq, _ = x.shape
    tokens = batch * seq
    me = 4 * lax.axis_index("x") + 2 * lax.axis_index("y") + lax.axis_index("c")
    weights = dict(ada_w=ada_w, ada_b=ada_b, norm_ffn1_g=norm_ffn1_g, ffn1_w_gu=ffn1_w_gu, ffn1_w_down=ffn1_w_down,
                   norm_mix_g=norm_mix_g, mix_w_in=mix_w_in, sgu_ln_g=sgu_ln_g, sgu_ln_b=sgu_ln_b, sgu_w_s=sgu_w_s,
                   sgu_b=sgu_b, conv_w=conv_w, out_norm_g=out_norm_g, mix_w_out=mix_w_out, norm_ffn2_g=norm_ffn2_g,
                   ffn2_w_gu=ffn2_w_gu, ffn2_w_down=ffn2_w_down, final_norm_g=final_norm_g)
    mom1 = dict(ada_w=m_ada_w, ada_b=m_ada_b, norm_ffn1_g=m_norm_ffn1_g, ffn1_w_gu=m_ffn1_w_gu,
                ffn1_w_down=m_ffn1_w_down, norm_mix_g=m_norm_mix_g, mix_w_in=m_mix_w_in, sgu_ln_g=m_sgu_ln_g,
                sgu_ln_b=m_sgu_ln_b, sgu_w_s=m_sgu_w_s, sgu_b=m_sgu_b, conv_w=m_conv_w, out_norm_g=m_out_norm_g,
                mix_w_out=m_mix_w_out, norm_ffn2_g=m_norm_ffn2_g, ffn2_w_gu=m_ffn2_w_gu, ffn2_w_down=m_ffn2_w_down,
                final_norm_g=m_final_norm_g)
    mom2 = dict(ada_w=v_ada_w, ada_b=v_ada_b, norm_ffn1_g=v_norm_ffn1_g, ffn1_w_gu=v_ffn1_w_gu,
                ffn1_w_down=v_ffn1_w_down, norm_mix_g=v_norm_mix_g, mix_w_in=v_mix_w_in, sgu_ln_g=v_sgu_ln_g,
                sgu_ln_b=v_sgu_ln_b, sgu_w_s=v_sgu_w_s, sgu_b=v_sgu_b, conv_w=v_conv_w, out_norm_g=v_out_norm_g,
                mix_w_out=v_mix_w_out, norm_ffn2_g=v_norm_ffn2_g, ffn2_w_gu=v_ffn2_w_gu, ffn2_w_down=v_ffn2_w_down,
                final_norm_g=v_final_norm_g)

    big = ("ffn1_w_gu", "ffn1_w_down", "mix_w_in", "mix_w_out", "ffn2_w_gu", "ffn2_w_down")
    transposed = ("ffn1_w_gu", "mix_w_in", "ffn2_w_gu")
    as_rows = lambda nm, a: jnp.swapaxes(a, 1, 2) if nm in transposed else a
    shard = {(l, nm): as_rows(nm, weights[nm])[l].astype(BF16) for l in range(DEPTH) for nm in big}
    full_w = {}

    def gather_of(keys):
        return keys, _GatherRows([shard[k] for k in keys])

    def landed(plan, got):
        full_w.update(zip(plan[0], got))

    ada_cols = ada_w.shape[2]
    ada_b_cols = lax.dynamic_slice_in_dim(ada_b, me * ada_cols, ada_cols, axis=1).reshape(DEPTH, 1, ada_cols)
    plan = gather_of([(0, "ffn1_w_gu")])
    c_dev, convw_dev, ada_recv, got = _prologue(
        jnp.pad(c, ((0, 8 - batch), (0, 0))), jnp.pad(conv_w.reshape(-1), (0, 8 * LANES - conv_w.size)).reshape(8, LANES),
        ada_w, ada_b_cols, plan[1])
    landed(plan, got)
    c_all = c_dev[:, :batch].reshape(N_DEV * batch, D_MODEL)
    convw_all = convw_dev.reshape(N_DEV, -1)[:, :conv_w.size].reshape((N_DEV,) + conv_w.shape)
    convw_full = jnp.transpose(convw_all, (1, 2, 0, 3)).reshape(DEPTH, 3, D_A)
    ada_mine = jnp.transpose(ada_recv[:, :, :batch, :], (1, 2, 0, 3)).reshape(DEPTH, batch, N_MOD * D_MODEL)
    mod = ada_mine.reshape(DEPTH, batch, N_MOD, 1, D_MODEL)

    causal = jnp.tril(jnp.ones((CHUNK, CHUNK), F32))
    bd = jnp.kron(jnp.eye(MXU_N // HEAD_DIM, dtype=F32), jnp.full((HEAD_DIM, HEAD_DIM), 1.0 / HEAD_DIM, F32)).astype(BF16)
    row_vec = lambda a: a.reshape(1, -1)

    hosted_gathers = {
        (0, "ffn1"): [(0, "ffn1_w_down"), (0, "mix_w_in"), (0, "mix_w_out")],
        (0, "ffn_down1"): [(0, "ffn2_w_gu")],
        (0, "mix_in"): [(0, "ffn2_w_down")],
        (0, "ffn2"): [(1, "ffn1_w_gu"), (1, "ffn1_w_down"), (1, "mix_w_in"), (1, "mix_w_out")],
        (1, "ffn1"): [(1, "ffn2_w_gu"), (1, "ffn2_w_down")],
    }

    def hosting(l, site):
        keys = hosted_gathers.get((l, site))
        return gather_of(keys) if keys else (None, None)

    xs = x.reshape(tokens, D_MODEL)
    saved = []
    for l in range(DEPTH):
        sh1, sc1, g1, sh2, sc2, g2, sh3, sc3, g3 = [mod[l, :, k] for k in range(N_MOD)]
        wm_masked = (sgu_w_s[l] * causal[None]).astype(BF16)
        mixer_consts = dict(
            wm=jnp.concatenate([wm_masked[0::2], wm_masked[1::2]], axis=2),
            bias_full=jnp.repeat(sgu_b[l].T, HEAD_DIM, axis=1),
            lng=row_vec(jnp.tile(sgu_ln_g[l], N_HEADS)), lnb=row_vec(jnp.tile(sgu_ln_b[l], N_HEADS)),
            convw=jnp.pad(convw_full[l], ((0, 5), (0, 0))), og=row_vec(out_norm_g[l]), bd=bd)
        x0 = xs
        plan = hosting(l, "ffn1")
        if l == 0:
            h1, a1, s1, w1, got = _normmod_matmul(x0, row_vec(norm_ffn1_g[l]), 1.0 + sc1, sh1, full_w[l, "ffn1_w_gu"], seq, "ffn_up", plan[1])
            landed(plan, got)
            plan = hosting(l, "ffn_down1")
            x1, f1, got = _matmul_residual(a1, full_w[l, "ffn1_w_down"], x0, g1, 0.5, seq, "ffn_down", plan[1])
        else:
            h1, a1, s1, w1, x1, f1, got = _ffn_forward(
                x0, row_vec(norm_ffn1_g[l]), 1.0 + sc1, sh1, full_w[l, "ffn1_w_gu"], full_w[l, "ffn1_w_down"], g1, 0.5, seq, "ffn_fwd",
                comm=plan[1])
        if got:
            landed(plan, got)
        plan = hosting(l, "mix_in")
        h2, proj, ymix, x2, o2, got = _mixer_forward(
            x1, row_vec(norm_mix_g[l]), 1.0 + sc2, sh2, full_w[l, "mix_w_in"], g2, full_w[l, "mix_w_out"], seq=seq,
            name="mixer_forward", comm=plan[1], **mixer_consts)
        if got:
            landed(plan, got)
        plan = hosting(l, "ffn2")
        if l + 1 < DEPTH:
            h3, a3, s3, w3, x3, f3, got = _ffn_forward(
                x2, row_vec(norm_ffn2_g[l]), 1.0 + sc3, sh3, full_w[l, "ffn2_w_gu"], full_w[l, "ffn2_w_down"], g3, 0.5, seq, "ffn_fwd",
                comm=plan[1])
        else:
            head = (loss_target.reshape(tokens, D_MODEL), row_vec(final_norm_g))
            h3, a3, s3, w3, x3, f3, d_final_g, loss_cols, got = _ffn_forward(
                x2, row_vec(norm_ffn2_g[l]), 1.0 + sc3, sh3, full_w[l, "ffn2_w_gu"], full_w[l, "ffn2_w_down"], g3, 0.5, seq, "ffn_fwd_loss",
                loss_head=head, comm=plan[1])
        if got:
            landed(plan, got)
        saved.append(dict(x0=x0, x1=x1, x2=x2, h1=h1, h2=h2, h3=h3, a1=a1, s1=s1, w1=w1, a3=a3, s3=s3, w3=w3, f1=f1, f3=f3, o2=o2, proj=proj,
                          ymix=ymix, mixer_consts=mixer_consts, wm_rows=wm_masked.reshape(N_HEADS // 2, 2 * CHUNK, CHUNK), sc=(1.0 + sc1, 1.0 + sc2, 1.0 + sc3), gates=(g1, g2, g3)))
        xs = x3

    dx = xs

    recv = {}
    small_grads = [None] * DEPTH
    d_mod = [None] * DEPTH

    mix_names = ("out_norm_g", "sgu_ln_g", "sgu_ln_b", "sgu_w_s", "sgu_b", "conv_w")
    late_names = ("norm_ffn1_g", "norm_mix_g", "norm_ffn2_g")

    def mix_parts(l):
        return [small_grads[l][nm] for nm in mix_names]

    def late_parts(l):
        return [small_grads[l][nm] for nm in late_names] + [d_mod[l]]

    pending = []

    def scatter_later(l, nm, grad):
        pending.append(((l, nm), _ScatterRows([grad])))

    def host():
        keys, parts = [k for k, _ in pending], [p for _, p in pending]
        pending.clear()
        return keys, (_Exchanges(parts) if parts else None)

    def hosted(keys, got):
        if got:
            recv.update(zip(keys, got))

    for l in reversed(range(DEPTH)):
        sv = saved[l]
        mc = sv["mixer_consts"]
        if l + 1 < DEPTH:
            pending.append((("late", l + 1), _GatherRows([_pack_small(late_parts(l + 1))])))
        keys, comm = host()
        df3, dg3, dgu3, dx2, dsh3, dsc3, dn3, got = _ffn_backward(
            dx, sv["gates"][2], sv["f3"], sv["s3"], sv["w3"], full_w[l, "ffn2_w_down"], full_w[l, "ffn2_w_gu"], sv["x2"],
            row_vec(norm_ffn2_g[l]), sv["sc"][2], 0.5, seq, "ffn_bwd", comm)
        hosted(keys, got)
        gw_down2, _ = _weight_grad(sv["a3"], df3, seq, "grad_w_down")
        scatter_later(l, "ffn2_w_down", gw_down2)
        keys, comm = host()
        gw_gu2, got = _weight_grad(dgu3, sv["h3"], seq, "grad_w_gu", comm)
        hosted(keys, got)
        scatter_later(l, "ffn2_w_gu", gw_gu2)
        keys, comm = host()
        do2, dg2, dproj, dx1, dsh2, dsc2, dn2, d_og, d_cw, d_lng, d_lnb, d_bias, d_wm, got = _mixer_backward(
            sv["proj"], dx2, sv["gates"][1], sv["o2"], full_w[l, "mix_w_out"], sv["x1"], row_vec(norm_mix_g[l]), sv["sc"][1],
            full_w[l, "mix_w_in"], wm_rows=sv["wm_rows"], causal=causal, seq=seq, name="mixer_backward", comm=comm, **mc)
        hosted(keys, got)
        small_grads[l] = dict(
            out_norm_g=d_og, sgu_ln_g=d_lng.reshape(N_HEADS, HEAD_DIM).sum(0), sgu_ln_b=d_lnb.reshape(N_HEADS, HEAD_DIM).sum(0),
            sgu_w_s=d_wm, sgu_b=d_bias.reshape(CHUNK, N_HEADS, HEAD_DIM).sum(-1).T, conv_w=d_cw[0:3])
        gw_out, _ = _weight_grad(sv["ymix"], do2, seq, "grad_w_out")
        scatter_later(l, "mix_w_out", gw_out)
        keys, comm = host()
        gw_in, got = _weight_grad(dproj, sv["h2"], seq, "grad_w_in", comm)
        hosted(keys, got)
        scatter_later(l, "mix_w_in", gw_in)
        keys, comm = host()
        pending.append((("mix", l), _GatherRows([_pack_small(mix_parts(l))])))
        if l > 0:
            df1, dg1, dgu1, dx0, dsh1, dsc1, dn1, got = _ffn_backward(
                dx1, sv["gates"][0], sv["f1"], sv["s1"], sv["w1"], full_w[l, "ffn1_w_down"], full_w[l, "ffn1_w_gu"], sv["x0"],
                row_vec(norm_ffn1_g[l]), sv["sc"][0], 0.5, seq, "ffn_bwd", comm)
        else:
            df1, dg1, dgu1, got = _residual_backward(dx1, sv["gates"][0], sv["f1"], full_w[l, "ffn1_w_down"], 0.5, sv["s1"], sv["w1"], seq, "ffn_down_bwd", comm)
        hosted(keys, got)
        gw_down1, _ = _weight_grad(sv["a1"], df1, seq, "grad_w_down")
        scatter_later(l, "ffn1_w_down", gw_down1)
        keys, comm = host()
        gw_gu1, got = _weight_grad(dgu1, sv["h1"], seq, "grad_w_gu", comm)
        hosted(keys, got)
        scatter_later(l, "ffn1_w_gu", gw_gu1)
        if l == 0:
            keys, comm = host()
            dx0, dsh1, dsc1, dn1, got = _matmul_normmod_backward(dgu1, full_w[l, "ffn1_w_gu"], sv["x0"], dx1, row_vec(norm_ffn1_g[l]), sv["sc"][0], seq, "ffn_up_bwd", comm)
            hosted(keys, got)
        dx = dx0
        small_grads[l].update(norm_ffn1_g=dn1, norm_mix_g=dn2, norm_ffn2_g=dn3)
        d_mod[l] = jnp.concatenate([dsh1, dsc1, dg1, dsh2, dsc2, dg2, dsh3, dsc3, dg3], axis=1)
    grad_x = dx.reshape(batch, seq, D_MODEL)

    grad_big, delta, new_m, new_v = {}, {}, {}, {}
    for nm in big:
        results = _adamw_rows([recv[l, nm] for l in range(DEPTH)], as_rows(nm, weights[nm]), as_rows(nm, mom1[nm]),
                              as_rows(nm, mom2[nm]), "adamw_" + nm)
        grad_big[nm], delta[nm], new_m[nm], new_v[nm] = [as_rows(nm, r) for r in results]

    last_parts = late_parts(0) + [d_final_g, loss_cols]
    last_shapes = [p.shape for p in last_parts]
    packed_all, packed_sum = _all_gather_small(_pack_small(last_parts), "reduce_small")
    late_sum = {0: _unpack_small(packed_sum, last_shapes)}
    d_mod_dev = {0: _unpack_small(packed_all, last_shapes, lead=(N_DEV,))[len(late_names)]}
    mix_sum = {}
    for l in range(DEPTH):
        gathered = recv["mix", l].reshape(N_DEV, -1, LANES)
        mix_sum[l] = _unpack_small(_sum_gathered(gathered, "sum_mix"), [p.shape for p in mix_parts(l)])
        if l > 0:
            shapes_l = [p.shape for p in late_parts(l)]
            gathered = recv["late", l].reshape(N_DEV, -1, LANES)
            late_sum[l] = _unpack_small(_sum_gathered(gathered, "sum_late"), shapes_l)
            d_mod_dev[l] = _unpack_small(gathered, shapes_l, lead=(N_DEV,))[len(late_names)]
    grad_small = {}
    for group, names in ((mix_sum, mix_names), (late_sum, late_names)):
        for k, nm in enumerate(names):
            grad_small[nm] = jnp.stack([group[l][k] for l in range(DEPTH)]).reshape(
                (DEPTH, 3, D_A) if nm == "conv_w" else weights[nm].shape)
    grad_small["conv_w"] = lax.dynamic_slice_in_dim(grad_small["conv_w"], me * conv_w.shape[2], conv_w.shape[2], axis=2)
    grad_small["final_norm_g"] = late_sum[0][len(late_names) + 1].reshape(final_norm_g.shape)
    loss = jnp.sum(late_sum[0][len(late_names) + 2])
    d_ada_all = jnp.stack([d_mod_dev[l] for l in range(DEPTH)]).reshape(DEPTH, N_DEV * batch, N_MOD * D_MODEL)
    d_ada_cols = lax.dynamic_slice_in_dim(d_ada_all, me * ada_cols, ada_cols, axis=2)
    g_ada_w, g_ada_b = _ada_backward(c_all, d_ada_cols, d_ada_all)

    grads = dict(grad_big)
    grads.update(grad_small)
    grads["ada_w"] = g_ada_w
    grads["ada_b"] = g_ada_b.reshape(ada_b.shape)

    names = ("ada_w", "ada_b", "norm_ffn1_g", "ffn1_w_gu", "ffn1_w_down", "norm_mix_g", "mix_w_in", "sgu_ln_g",
             "sgu_ln_b", "sgu_w_s", "sgu_b", "conv_w", "out_norm_g", "mix_w_out", "norm_ffn2_g", "ffn2_w_gu",
             "ffn2_w_down", "final_norm_g")
    delta["ada_w"], new_m["ada_w"], new_v["ada_w"] = _adamw_nd(ada_w, grads["ada_w"], m_ada_w, v_ada_w, "adamw_ada_w")
    rest = [nm for nm in names if nm not in big and nm != "ada_w"]
    pick = lambda src: [src[nm] for nm in rest]
    for nm, d_k, m_k, v_k in zip(rest, *_adamw_many(pick(weights), pick(grads), pick(mom1), pick(mom2), "adamw_small")):
        delta[nm], new_m[nm], new_v[nm] = d_k, m_k, v_k

    return (loss, grad_x, *[grads[nm] for nm in names], *[delta[nm] for nm in names],
            *[new_m[nm] for nm in names], *[new_v[nm] for nm in names])
```

```python
import math

import jax
import jax.numpy as jnp
from jax import lax
from jax.experimental import pallas as pl
from jax.experimental.pallas import tpu as pltpu

F32 = jnp.float32
BF16 = jnp.bfloat16

D_MODEL = 1024
D_A = 512
D_PROJ = 2560
N_HEADS = 8
HEAD_DIM = 64
CHUNK = 128
N_MOD = 9
DEPTH = 2
EPS = 1e-6
N_DEV = 8
LANES = 128
MXU_N = 256
HALO = 16
VMEM_LIMIT = 56 * 1024 * 1024
FORWARD_STEPS = 4

ADAM_LR = 0.001
ADAM_B1 = 0.9
ADAM_B2 = 0.999
ADAM_EPS = 1e-08
ADAM_WD = 0.01
ADAM_STEP = 10

MESH = pl.DeviceIdType.MESH


def _dot(a, b):
    return jnp.dot(a, b, preferred_element_type=F32)


def _dot_nt(a, b):
    return lax.dot_general(a, b, (((1,), (1,)), ((), ())), preferred_element_type=F32)


def _dot_tn(a, b):
    return lax.dot_general(a, b, (((0,), (0,)), ((), ())), preferred_element_type=F32)


def _sigmoid(x):
    return 0.5 * jnp.tanh(0.5 * x) + 0.5


def _gelu(x):
    return 0.5 * x * (1.0 + lax.erf(x * (1.0 / math.sqrt(2.0))))


def _gelu_grad(x):
    cdf = 0.5 * (1.0 + lax.erf(x * (1.0 / math.sqrt(2.0))))
    return cdf + x * jnp.exp(-0.5 * x * x) * (1.0 / math.sqrt(2.0 * math.pi))


def _params(n_axes=1, parallel=False):
    sem = ("parallel" if parallel else "arbitrary",) * n_axes
    return pltpu.CompilerParams(dimension_semantics=sem, vmem_limit_bytes=VMEM_LIMIT)


def _resident(shape):
    nd = len(shape)
    return pl.BlockSpec(shape, lambda *_: (0,) * nd, pipeline_mode=pl.Buffered(1))


def _tile_rows(seq):
    return min(512, seq)


def _my_position():
    x, y, c = lax.axis_index("x"), lax.axis_index("y"), lax.axis_index("c")
    return x, y, c, 4 * x + 2 * y + c


def _peer(x, y, c, p):
    return (x ^ ((p >> 2) & 1), y ^ ((p >> 1) & 1), c ^ (p & 1))


class _GatherRows:
    def __init__(self, shards):
        self.operands = list(shards)
        n = len(shards)
        self.out_shape = [jax.ShapeDtypeStruct((N_DEV * s.shape[0], s.shape[1]), s.dtype) for s in shards]
        self.scratch = [pltpu.SemaphoreType.DMA((n, N_DEV - 1)), pltpu.SemaphoreType.DMA((n, N_DEV - 1)),
                        pltpu.SemaphoreType.DMA((n,))]

    def _plan(self, src, dst, send, recv, loc):
        x, y, c, _ = _my_position()
        me, sib = (x, y, c), (x, y, 1 - c)
        chips = [(1 - x, y), (x, 1 - y), (1 - x, 1 - y)]
        plans = []
        for k, shard in enumerate(self.operands):
            rows = shard.shape[0]

            def blk(pos, k=k, rows=rows):
                return dst[k].at[pl.ds((4 * pos[0] + 2 * pos[1] + pos[2]) * rows, rows), :]

            def rc(s, block, to, source=None, k=k, blk=blk):
                return pltpu.make_async_remote_copy(
                    src_ref=blk(block) if source is None else source, dst_ref=blk(block),
                    send_sem=send.at[k, s], recv_sem=recv.at[k, s], device_id=to, device_id_type=MESH)

            plans.append(dict(
                local=pltpu.make_async_copy(src[k], blk(me), loc.at[k]),
                first=[rc(0, me, sib, src[k])] + [rc(1 + j, me, (*chip, c), src[k]) for j, chip in enumerate(chips)],
                landed=[rc(1 + j, (*chip, c), me) for j, chip in enumerate(chips)],
                passed=[rc(4 + j, (*chip, c), sib) for j, chip in enumerate(chips)],
                from_sib=[rc(0, sib, me)] + [rc(4 + j, (*chip, 1 - c), me) for j, chip in enumerate(chips)]))
        return plans

    def start(self, src, dst, send, recv, loc):
        for plan in self._plan(src, dst, send, recv, loc):
            plan["local"].start()
            for cp in plan["first"]:
                cp.start()

    def forward(self, src, dst, send, recv, loc):
        for plan in self._plan(src, dst, send, recv, loc):
            for landed, passed in zip(plan["landed"], plan["passed"]):
                landed.wait_recv()
                passed.start()

    def finish(self, src, dst, send, recv, loc):
        for plan in self._plan(src, dst, send, recv, loc):
            for cp in plan["from_sib"]:
                cp.wait_recv()
            for cp in plan["first"] + plan["passed"]:
                cp.wait_send()
            plan["local"].wait()


class _ScatterRows:
    def __init__(self, grads):
        self.operands = list(grads)
        n = len(grads)
        self.out_shape = [jax.ShapeDtypeStruct((N_DEV, g.shape[0] // N_DEV, g.shape[1]), g.dtype) for g in grads]
        self.scratch = [pltpu.SemaphoreType.DMA((n, N_DEV - 1)), pltpu.SemaphoreType.DMA((n, N_DEV - 1)),
                        pltpu.SemaphoreType.DMA((n,))]

    def _plan(self, src, dst, send, recv, loc):
        x, y, c, me = _my_position()
        copies = []
        for k, grad in enumerate(self.operands):
            rows = grad.shape[0] // N_DEV
            copies.append(pltpu.make_async_copy(src[k].at[pl.ds(me * rows, rows), :], dst[k].at[me], loc.at[k]))
            for p in range(1, N_DEV):
                px, py, pc = _peer(x, y, c, p)
                copies.append(pltpu.make_async_remote_copy(
                    src_ref=src[k].at[pl.ds((4 * px + 2 * py + pc) * rows, rows), :], dst_ref=dst[k].at[me],
                    send_sem=send.at[k, p - 1], recv_sem=recv.at[k, p - 1], device_id=(px, py, pc), device_id_type=MESH))
        return copies

    def start(self, src, dst, send, recv, loc):
        for cp in self._plan(src, dst, send, recv, loc):
            cp.start()

    def forward(self, src, dst, send, recv, loc):
        pass

    def finish(self, src, dst, send, recv, loc):
        for cp in self._plan(src, dst, send, recv, loc):
            cp.wait()


class _Exchanges:
    def __init__(self, parts):
        self.parts = list(parts)
        self.operands = [op for part in self.parts for op in part.operands]
        self.out_shape = [shp for part in self.parts for shp in part.out_shape]
        self.scratch = [scr for part in self.parts for scr in part.scratch]

    def _each(self, src, dst, sems):
        at, sem_at = 0, 0
        for part in self.parts:
            n, n_sem = len(part.operands), len(part.scratch)
            yield part, src[at:at + n], dst[at:at + n], sems[sem_at:sem_at + n_sem]
            at, sem_at = at + n, sem_at + n_sem

    def start(self, src, dst, *sems):
        for part, part_src, part_dst, part_sems in self._each(src, dst, sems):
            part.start(part_src, part_dst, *part_sems)

    def forward(self, src, dst, *sems):
        for part, part_src, part_dst, part_sems in self._each(src, dst, sems):
            part.forward(part_src, part_dst, *part_sems)

    def finish(self, src, dst, *sems):
        for part, part_src, part_dst, part_sems in self._each(src, dst, sems):
            part.finish(part_src, part_dst, *part_sems)


_ANY = pl.BlockSpec(memory_space=pl.ANY)


def _call(body, *, name, grid, in_specs, out_specs, out_shape, operands, scratch_shapes=(), parallel=False, comm=None):
    n_axes = len(grid)
    if comm is None:
        outs = pl.pallas_call(body, name=name, grid=grid, out_shape=list(out_shape), in_specs=list(in_specs),
                              out_specs=list(out_specs), scratch_shapes=list(scratch_shapes),
                              compiler_params=_params(n_axes, parallel))(*operands)
        return list(outs), None
    n_in, n_out, n_scr, n_c = len(in_specs), len(out_specs), len(scratch_shapes), len(comm.operands)
    total = math.prod(grid)

    def hosted(*refs):
        ins, c_src = refs[:n_in], refs[n_in:n_in + n_c]
        outs, c_dst = refs[n_in + n_c:n_in + n_c + n_out], refs[n_in + n_c + n_out:n_in + 2 * n_c + n_out]
        scr, sems = refs[n_in + 2 * n_c + n_out:n_in + 2 * n_c + n_out + n_scr], refs[n_in + 2 * n_c + n_out + n_scr:]
        step = pl.program_id(0)
        for axis in range(1, n_axes):
            step = step * grid[axis] + pl.program_id(axis)

        @pl.when(step == 0)
        def _():
            comm.start(c_src, c_dst, *sems)

        @pl.when(step == max(total - FORWARD_STEPS, 0))
        def _():
            comm.forward(c_src, c_dst, *sems)

        body(*ins, *outs, *scr)

        @pl.when(step == total - 1)
        def _():
            comm.finish(c_src, c_dst, *sems)

    res = pl.pallas_call(hosted, name=name, grid=grid, out_shape=list(out_shape) + comm.out_shape,
                         in_specs=list(in_specs) + [_ANY] * n_c, out_specs=list(out_specs) + [_ANY] * n_c,
                         scratch_shapes=list(scratch_shapes) + comm.scratch,
                         compiler_params=_params(n_axes, False))(*operands, *comm.operands)
    return list(res[:n_out]), list(res[n_out:])


def _all_gather_small(v, name):
    rows = v.shape[0]

    def body(v_ref, all_ref, sum_ref, send_sems, recv_sems):
        x, y, c, me = _my_position()
        all_ref[me] = v_ref[...]
        copies = []
        for p in range(1, N_DEV):
            cp = pltpu.make_async_remote_copy(
                src_ref=v_ref, dst_ref=all_ref.at[me], send_sem=send_sems.at[p - 1], recv_sem=recv_sems.at[p - 1],
                device_id=_peer(x, y, c, p), device_id_type=MESH)
            cp.start()
            copies.append(cp)
        for cp in copies:
            cp.wait()
        acc = all_ref[0]
        for d in range(1, N_DEV):
            acc = acc + all_ref[d]
        sum_ref[...] = acc

    return pl.pallas_call(
        body, name=name,
        out_shape=[jax.ShapeDtypeStruct((N_DEV, rows, LANES), F32), jax.ShapeDtypeStruct((rows, LANES), F32)],
        in_specs=[pl.BlockSpec(memory_space=pltpu.VMEM)],
        out_specs=[pl.BlockSpec(memory_space=pltpu.VMEM)] * 2,
        scratch_shapes=[pltpu.SemaphoreType.DMA((N_DEV - 1,)), pltpu.SemaphoreType.DMA((N_DEV - 1,))],
        compiler_params=pltpu.CompilerParams(vmem_limit_bytes=VMEM_LIMIT),
    )(v)


def _sum_gathered(gathered, name):
    rows = gathered.shape[1]

    def body(g_ref, o_ref):
        acc = g_ref[0]
        for d in range(1, N_DEV):
            acc = acc + g_ref[d]
        o_ref[...] = acc

    return pl.pallas_call(
        body, name=name, out_shape=jax.ShapeDtypeStruct((rows, LANES), F32),
        in_specs=[pl.BlockSpec(memory_space=pltpu.VMEM)], out_specs=pl.BlockSpec(memory_space=pltpu.VMEM),
        compiler_params=pltpu.CompilerParams(vmem_limit_bytes=VMEM_LIMIT),
    )(gathered)


def _pack_small(parts):
    flat = jnp.concatenate([p.reshape(-1).astype(F32) for p in parts])
    total = flat.shape[0]
    padded = -(-total // (8 * LANES)) * (8 * LANES)
    flat = jnp.pad(flat, (0, padded - total))
    return flat.reshape(padded // LANES, LANES)


def _unpack_small(packed, shapes, lead=()):
    flat = packed.reshape(lead + (-1,))
    out, off = [], 0
    for shp in shapes:
        size = math.prod(shp)
        out.append(flat[..., off:off + size].reshape(lead + tuple(shp)))
        off += size
    return out


def _prologue(c_rows, convw_rows, ada_w, ada_b_cols, gather):
    depth, _, cols = ada_w.shape
    n_c = len(gather.operands)
    sub = 8

    def body(c_ref, cw_ref, b_ref, w_hbm, *rest):
        g_src, (c_all_ref, cw_all_ref, ada_ref), g_dst = rest[:n_c], rest[n_c:n_c + 3], rest[n_c + 3:2 * n_c + 3]
        ada_local, w_ref, w_sem, send_sems, recv_sems = rest[2 * n_c + 3:2 * n_c + 8]
        g_sems = rest[2 * n_c + 8:]
        x, y, c, me = _my_position()
        gather.start(g_src, g_dst, *g_sems)
        load_w = pltpu.make_async_copy(w_hbm, w_ref, w_sem)
        load_w.start()

        def to_all(k, src_ref, dst_ref):
            copies = []
            for p in range(1, N_DEV):
                copies.append(pltpu.make_async_remote_copy(
                    src_ref=src_ref, dst_ref=dst_ref.at[me], send_sem=send_sems.at[k, p - 1], recv_sem=recv_sems.at[k, p - 1],
                    device_id=_peer(x, y, c, p), device_id_type=MESH))
            return copies

        first = to_all(0, c_ref, c_all_ref) + to_all(1, cw_ref, cw_all_ref)
        c_all_ref[me] = c_ref[...]
        cw_all_ref[me] = cw_ref[...]
        for cp in first:
            cp.start()
        for cp in first:
            cp.wait()
        cv = c_all_ref[...].reshape(N_DEV * sub, D_MODEL)
        act = (cv * _sigmoid(cv)).astype(BF16)
        load_w.wait()
        for l in range(depth):
            ada_local[l] = _dot(act, w_ref[l].astype(BF16)) + b_ref[l]
        ada_ref[me] = ada_local[:, pl.ds(pl.multiple_of(me * sub, sub), sub), :]
        rows_out = []
        for p in range(1, N_DEV):
            px, py, pc = _peer(x, y, c, p)
            rows = pl.ds(pl.multiple_of((4 * px + 2 * py + pc) * sub, sub), sub)
            rows_out.append(pltpu.make_async_remote_copy(
                src_ref=ada_local.at[:, rows, :], dst_ref=ada_ref.at[me], send_sem=send_sems.at[2, p - 1],
                recv_sem=recv_sems.at[2, p - 1], device_id=(px, py, pc), device_id_type=MESH))
        for cp in rows_out:
            cp.start()
        for cp in rows_out:
            cp.wait()
        gather.forward(g_src, g_dst, *g_sems)
        gather.finish(g_src, g_dst, *g_sems)

    vmem = pl.BlockSpec(memory_space=pltpu.VMEM)
    outs = pl.pallas_call(
        body, name="prologue",
        out_shape=[jax.ShapeDtypeStruct((N_DEV, sub, D_MODEL), F32), jax.ShapeDtypeStruct((N_DEV, sub, LANES), F32),
                   jax.ShapeDtypeStruct((N_DEV, depth, sub, cols), F32)] + gather.out_shape,
        in_specs=[vmem] * 3 + [_ANY] * (1 + n_c), out_specs=[vmem] * 3 + [_ANY] * n_c,
        scratch_shapes=[pltpu.VMEM((depth, N_DEV * sub, cols), F32), pltpu.VMEM(ada_w.shape, F32), pltpu.SemaphoreType.DMA,
                        pltpu.SemaphoreType.DMA((3, N_DEV - 1)), pltpu.SemaphoreType.DMA((3, N_DEV - 1))] + gather.scratch,
        compiler_params=pltpu.CompilerParams(vmem_limit_bytes=VMEM_LIMIT),
    )(c_rows, convw_rows, ada_b_cols, ada_w, *gather.operands)
    return outs[0], outs[1], outs[2], list(outs[3:])


def _ada_backward(c_all, d_ada_cols, d_ada_all):
    nb = c_all.shape[0]
    cols = d_ada_cols.shape[2]
    full = d_ada_all.shape[2]

    def body(c_ref, dc_ref, da_ref, gw_ref, gb_ref):
        cv = c_ref[...]
        act = (cv * _sigmoid(cv)).astype(BF16)
        gw_ref[0] = _dot_tn(act, dc_ref[0].astype(BF16))
        gb_ref[0] = jnp.sum(da_ref[0], axis=0, keepdims=True)

    return pl.pallas_call(
        body, name="ada_backward", grid=(DEPTH,),
        out_shape=[jax.ShapeDtypeStruct((DEPTH, D_MODEL, cols), F32), jax.ShapeDtypeStruct((DEPTH, 1, full), F32)],
        in_specs=[pl.BlockSpec((nb, D_MODEL), lambda l: (0, 0)),
                  pl.BlockSpec((1, nb, cols), lambda l: (l, 0, 0)),
                  pl.BlockSpec((1, nb, full), lambda l: (l, 0, 0))],
        out_specs=[pl.BlockSpec((1, D_MODEL, cols), lambda l: (l, 0, 0)),
                   pl.BlockSpec((1, 1, full), lambda l: (l, 0, 0))],
        compiler_params=_params(),
    )(c_all, d_ada_cols, d_ada_all)


def _rms(xv):
    return lax.rsqrt(jnp.mean(xv * xv, axis=-1, keepdims=True) + EPS)


def _normmod_matmul(x, gnorm, scale1p, shift, w_t, seq, name, comm=None):
    tokens, width = x.shape[0], w_t.shape[0] // 2
    tm = _tile_rows(seq)
    per_seq = seq // tm
    n_chunks = width // MXU_N

    def body(x_ref, g_ref, sc_ref, sh_ref, w_ref, h_ref, act_ref, silu_ref, dact_ref):
        xv = x_ref[...]
        h = (xv * _rms(xv) * g_ref[...]) * sc_ref[0] + sh_ref[0]
        h_ref[...] = h.astype(BF16)
        for ck in range(n_chunks):
            cs = slice(ck * MXU_N, (ck + 1) * MXU_N)
            g = _dot_nt(h_ref[...], w_ref[cs, :])
            u = _dot_nt(h_ref[...], w_ref[width + ck * MXU_N:width + (ck + 1) * MXU_N, :])
            sig = _sigmoid(g)
            silu = g * sig
            act_ref[:, cs] = (silu * u).astype(BF16)
            silu_ref[:, cs] = silu.astype(BF16)
            dact_ref[:, cs] = (u * (sig + silu * (1.0 - sig))).astype(BF16)

    per_batch = pl.BlockSpec((1, 1, D_MODEL), lambda i: (i // per_seq, 0, 0))
    outs, got = _call(
        body, name=name, grid=(tokens // tm,),
        out_shape=[jax.ShapeDtypeStruct((tokens, D_MODEL), BF16)] + [jax.ShapeDtypeStruct((tokens, width), BF16)] * 3,
        in_specs=[pl.BlockSpec((tm, D_MODEL), lambda i: (i, 0)), _resident((1, D_MODEL)), per_batch, per_batch,
                  _resident(w_t.shape)],
        out_specs=[pl.BlockSpec((tm, D_MODEL), lambda i: (i, 0))] + [pl.BlockSpec((tm, width), lambda i: (i, 0))] * 3,
        operands=(x, gnorm, scale1p, shift, w_t), parallel=True, comm=comm)
    return (*outs, got)


def _matmul_residual(src, w, x, gate, scale, seq, name, comm=None):
    tokens, k_dim = x.shape[0], w.shape[0]
    tm = _tile_rows(seq)
    per_seq = seq // tm

    def body(s_ref, w_ref, x_ref, gate_ref, xo_ref, f_ref):
        f = _dot(s_ref[...], w_ref[...])
        f_ref[...] = f.astype(BF16)
        xo_ref[...] = x_ref[...] + (scale * gate_ref[0]) * f

    (x_out, f), got = _call(
        body, name=name, grid=(tokens // tm,),
        out_shape=[jax.ShapeDtypeStruct((tokens, D_MODEL), F32), jax.ShapeDtypeStruct((tokens, D_MODEL), BF16)],
        in_specs=[pl.BlockSpec((tm, k_dim), lambda i: (i, 0)), _resident(w.shape),
                  pl.BlockSpec((tm, D_MODEL), lambda i: (i, 0)),
                  pl.BlockSpec((1, 1, D_MODEL), lambda i: (i // per_seq, 0, 0))],
        out_specs=[pl.BlockSpec((tm, D_MODEL), lambda i: (i, 0))] * 2,
        operands=(src, w, x, gate), parallel=True, comm=comm)
    return x_out, f, got


def _loss_tile(xv, target, gn):
    r = _rms(xv)
    xn = xv * r
    err = xn * gn - target
    loss = (0.5 / D_MODEL) * jnp.sum(err * err, axis=0, keepdims=True)
    dyv = err * (1.0 / D_MODEL)
    dg = jnp.sum(dyv * xn, axis=0, keepdims=True)
    dxn = dyv * gn
    dx = r * (dxn - xn * jnp.mean(dxn * xn, axis=-1, keepdims=True))
    return loss, dx, dg


def _ffn_forward(x, gnorm, scale1p, shift, w_gu_t, w_down, gate, scale, seq, name, loss_head=None, comm=None):
    tokens, width = x.shape[0], w_down.shape[0]
    tm = _tile_rows(seq)
    per_seq = seq // tm
    n_chunks = width // MXU_N

    def body(x_ref, g_ref, sc_ref, sh_ref, wgu_ref, wd_ref, gate_ref, *rest):
        if loss_head:
            t_ref, gf_ref, h_ref, act_ref, silu_ref, dact_ref, xo_ref, f_ref, dgf_ref, loss_ref = rest
        else:
            h_ref, act_ref, silu_ref, dact_ref, xo_ref, f_ref = rest
        xv = x_ref[...]
        h = (xv * _rms(xv) * g_ref[...]) * sc_ref[0] + sh_ref[0]
        h_ref[...] = h.astype(BF16)
        for ck in range(n_chunks):
            cs = slice(ck * MXU_N, (ck + 1) * MXU_N)
            g = _dot_nt(h_ref[...], wgu_ref[cs, :])
            u = _dot_nt(h_ref[...], wgu_ref[width + ck * MXU_N:width + (ck + 1) * MXU_N, :])
            sig = _sigmoid(g)
            silu = g * sig
            act_ref[:, cs] = (silu * u).astype(BF16)
            silu_ref[:, cs] = silu.astype(BF16)
            dact_ref[:, cs] = (u * (sig + silu * (1.0 - sig))).astype(BF16)
        f = _dot(act_ref[...], wd_ref[...])
        f_ref[...] = f.astype(BF16)
        x_out = xv + (scale * gate_ref[0]) * f
        if loss_head:
            i = pl.program_id(0)
            loss, dx, dg = _loss_tile(x_out, t_ref[...], gf_ref[...])
            xo_ref[...] = dx

            @pl.when(i == 0)
            def _():
                dgf_ref[...] = dg
                loss_ref[...] = loss

            @pl.when(i != 0)
            def _():
                dgf_ref[...] = dgf_ref[...] + dg
                loss_ref[...] = loss_ref[...] + loss
        else:
            xo_ref[...] = x_out

    row = lambda i: (i, 0)
    per_batch = pl.BlockSpec((1, 1, D_MODEL), lambda i: (i // per_seq, 0, 0))
    tile = lambda cols: pl.BlockSpec((tm, cols), row)
    wide = jax.ShapeDtypeStruct((tokens, width), BF16)
    fixed = pl.BlockSpec((1, D_MODEL), lambda i: (0, 0))
    vec = jax.ShapeDtypeStruct((1, D_MODEL), F32)
    outs, got = _call(
        body, name=name, grid=(tokens // tm,),
        out_shape=[jax.ShapeDtypeStruct((tokens, D_MODEL), BF16), wide, wide, wide,
                   jax.ShapeDtypeStruct((tokens, D_MODEL), F32), jax.ShapeDtypeStruct((tokens, D_MODEL), BF16)]
        + ([vec, vec] if loss_head else []),
        in_specs=[tile(D_MODEL), _resident((1, D_MODEL)), per_batch, per_batch, _resident(w_gu_t.shape),
                  _resident(w_down.shape), per_batch] + ([tile(D_MODEL), _resident((1, D_MODEL))] if loss_head else []),
        out_specs=[tile(D_MODEL), tile(width), tile(width), tile(width), tile(D_MODEL), tile(D_MODEL)]
        + ([fixed, fixed] if loss_head else []),
        operands=(x, gnorm, scale1p, shift, w_gu_t, w_down, gate) + (tuple(loss_head) if loss_head else ()),
        parallel=not loss_head, comm=comm)
    return (*outs, got)


def _residual_backward(dy, gate, f, w, scale, silu, dact, seq, name, comm=None):
    tokens, k_dim = dy.shape[0], w.shape[0]
    batch = tokens // seq
    tm = _tile_rows(seq)
    per_seq = seq // tm
    n_chunks = k_dim // MXU_N

    def body(dy_ref, gate_ref, f_ref, silu_ref, dact_ref, w_ref, df_ref, dgate_ref, dgu_ref):
        i = pl.program_id(0)
        dy_v = dy_ref[...]
        df_ref[...] = ((scale * gate_ref[0]) * dy_v).astype(BF16)
        part = scale * jnp.sum(dy_v * f_ref[...].astype(F32), axis=0, keepdims=True)
        for ck in range(n_chunks):
            cs = slice(ck * MXU_N, (ck + 1) * MXU_N)
            cu = slice(k_dim + ck * MXU_N, k_dim + (ck + 1) * MXU_N)
            da = _dot_nt(df_ref[...], w_ref[cs, :])
            dgu_ref[:, cs] = (da * dact_ref[:, cs].astype(F32)).astype(BF16)
            dgu_ref[:, cu] = (da * silu_ref[:, cs].astype(F32)).astype(BF16)

        @pl.when(i % per_seq == 0)
        def _():
            dgate_ref[0] = part

        @pl.when(i % per_seq != 0)
        def _():
            dgate_ref[0] = dgate_ref[0] + part

    row = lambda i: (i, 0)
    per_batch = pl.BlockSpec((1, 1, D_MODEL), lambda i: (i // per_seq, 0, 0))
    tile = lambda cols: pl.BlockSpec((tm, cols), row)
    outs, got = _call(
        body, name=name, grid=(tokens // tm,),
        out_shape=[jax.ShapeDtypeStruct((tokens, D_MODEL), BF16), jax.ShapeDtypeStruct((batch, 1, D_MODEL), F32),
                   jax.ShapeDtypeStruct((tokens, 2 * k_dim), BF16)],
        in_specs=[tile(D_MODEL), per_batch, tile(D_MODEL), tile(k_dim), tile(k_dim), _resident(w.shape)],
        out_specs=[tile(D_MODEL), per_batch, tile(2 * k_dim)],
        operands=(dy, gate, f, silu, dact, w), comm=comm)
    return (*outs, got)


def _matmul_normmod_backward(dsrc, w_t, x, dy, gnorm, scale1p, seq, name, comm=None):
    tokens, k_dim = dsrc.shape
    batch = tokens // seq
    tm = _tile_rows(seq)
    per_seq = seq // tm

    def body(ds_ref, w_ref, x_ref, dy_ref, g_ref, sc_ref, dx_ref, dsh_ref, dsc_ref, dg_ref):
        i = pl.program_id(0)
        dh = _dot(ds_ref[...], w_ref[...])
        xv = x_ref[...]
        r = _rms(xv)
        xn = xv * r
        gn = g_ref[...]
        dsh = jnp.sum(dh, axis=0, keepdims=True)
        dsc = jnp.sum(dh * (xn * gn), axis=0, keepdims=True)
        dhn = dh * sc_ref[0]
        dg = jnp.sum(dhn * xn, axis=0, keepdims=True)
        dxn = dhn * gn
        dx_ref[...] = dy_ref[...] + r * (dxn - xn * jnp.mean(dxn * xn, axis=-1, keepdims=True))

        @pl.when(i % per_seq == 0)
        def _():
            dsh_ref[0] = dsh
            dsc_ref[0] = dsc

        @pl.when(i % per_seq != 0)
        def _():
            dsh_ref[0] = dsh_ref[0] + dsh
            dsc_ref[0] = dsc_ref[0] + dsc

        @pl.when(i == 0)
        def _():
            dg_ref[...] = dg

        @pl.when(i != 0)
        def _():
            dg_ref[...] = dg_ref[...] + dg

    row = lambda i: (i, 0)
    per_batch = pl.BlockSpec((1, 1, D_MODEL), lambda i: (i // per_seq, 0, 0))
    outs, got = _call(
        body, name=name, grid=(tokens // tm,),
        out_shape=[jax.ShapeDtypeStruct((tokens, D_MODEL), F32), jax.ShapeDtypeStruct((batch, 1, D_MODEL), F32),
                   jax.ShapeDtypeStruct((batch, 1, D_MODEL), F32), jax.ShapeDtypeStruct((1, D_MODEL), F32)],
        in_specs=[pl.BlockSpec((tm, k_dim), row), _resident(w_t.shape), pl.BlockSpec((tm, D_MODEL), row),
                  pl.BlockSpec((tm, D_MODEL), row), _resident((1, D_MODEL)), per_batch],
        out_specs=[pl.BlockSpec((tm, D_MODEL), row), per_batch, per_batch, pl.BlockSpec((1, D_MODEL), lambda i: (0, 0))],
        operands=(dsrc, w_t, x, dy, gnorm, scale1p), comm=comm)
    return (*outs, got)


def _ffn_backward(dy, gate, f, silu, dact, w_down, w_gu_t, x, gnorm, scale1p, scale, seq, name, comm=None):
    tokens, k_dim = dy.shape[0], w_down.shape[0]
    batch = tokens // seq
    tm = min(256, seq)
    per_seq = seq // tm
    n_chunks = k_dim // MXU_N

    def body(dy_ref, gate_ref, f_ref, silu_ref, dact_ref, wd_ref, wgu_ref, x_ref, g_ref, sc_ref,
             df_ref, dgate_ref, dgu_ref, dx_ref, dsh_ref, dsc_ref, dg_ref):
        i = pl.program_id(0)
        dy_v = dy_ref[...]
        df_ref[...] = ((scale * gate_ref[0]) * dy_v).astype(BF16)
        dgate = scale * jnp.sum(dy_v * f_ref[...].astype(F32), axis=0, keepdims=True)
        for ck in range(n_chunks):
            cs = slice(ck * MXU_N, (ck + 1) * MXU_N)
            cu = slice(k_dim + ck * MXU_N, k_dim + (ck + 1) * MXU_N)
            da = _dot_nt(df_ref[...], wd_ref[cs, :])
            dgu_ref[:, cs] = (da * dact_ref[:, cs].astype(F32)).astype(BF16)
            dgu_ref[:, cu] = (da * silu_ref[:, cs].astype(F32)).astype(BF16)
        dh = _dot(dgu_ref[...], wgu_ref[...])
        xv = x_ref[...]
        r = _rms(xv)
        xn = xv * r
        gn = g_ref[...]
        dsh = jnp.sum(dh, axis=0, keepdims=True)
        dsc = jnp.sum(dh * (xn * gn), axis=0, keepdims=True)
        dhn = dh * sc_ref[0]
        dg = jnp.sum(dhn * xn, axis=0, keepdims=True)
        dxn = dhn * gn
        dx_ref[...] = dy_v + r * (dxn - xn * jnp.mean(dxn * xn, axis=-1, keepdims=True))

        @pl.when(i % per_seq == 0)
        def _():
            dgate_ref[0] = dgate
            dsh_ref[0] = dsh
            dsc_ref[0] = dsc

        @pl.when(i % per_seq != 0)
        def _():
            dgate_ref[0] = dgate_ref[0] + dgate
            dsh_ref[0] = dsh_ref[0] + dsh
            dsc_ref[0] = dsc_ref[0] + dsc

        @pl.when(i == 0)
        def _():
            dg_ref[...] = dg

        @pl.when(i != 0)
        def _():
            dg_ref[...] = dg_ref[...] + dg

    row = lambda i: (i, 0)
    per_batch = pl.BlockSpec((1, 1, D_MODEL), lambda i: (i // per_seq, 0, 0))
    tile = lambda width: pl.BlockSpec((tm, width), row)
    vec = jax.ShapeDtypeStruct((batch, 1, D_MODEL), F32)
    outs, got = _call(
        body, name=name, grid=(tokens // tm,),
        out_shape=[jax.ShapeDtypeStruct((tokens, D_MODEL), BF16), vec, jax.ShapeDtypeStruct((tokens, 2 * k_dim), BF16),
                   jax.ShapeDtypeStruct((tokens, D_MODEL), F32), vec, vec, jax.ShapeDtypeStruct((1, D_MODEL), F32)],
        in_specs=[tile(D_MODEL), per_batch, tile(D_MODEL), tile(k_dim), tile(k_dim), _resident(w_down.shape),
                  _resident(w_gu_t.shape), tile(D_MODEL), _resident((1, D_MODEL)), per_batch],
        out_specs=[tile(D_MODEL), per_batch, tile(2 * k_dim), tile(D_MODEL), per_batch, per_batch,
                   pl.BlockSpec((1, D_MODEL), lambda i: (0, 0))],
        operands=(dy, gate, f, silu, dact, w_down, w_gu_t, x, gnorm, scale1p), comm=comm)
    return (*outs, got)


def _weight_grad(a, b, seq, name, comm=None):
    tokens, n_out = a.shape
    tn = MXU_N

    def body(a_ref, b_ref, o_ref):
        o_ref[...] = _dot_tn(a_ref[...], b_ref[...]).astype(BF16)

    (out,), got = _call(
        body, name=name, grid=(n_out // tn,),
        out_shape=[jax.ShapeDtypeStruct((n_out, D_MODEL), BF16)],
        in_specs=[pl.BlockSpec((tokens, tn), lambda j: (0, j)), _resident((tokens, D_MODEL))],
        out_specs=[pl.BlockSpec((tn, D_MODEL), lambda j: (j, 0))],
        operands=(a, b), comm=comm)
    return out, got


def _group_mean(v, bd):
    hi = v.astype(BF16)
    lo = (v - hi.astype(F32)).astype(BF16)
    blocks = [slice(k * MXU_N, (k + 1) * MXU_N) for k in range(v.shape[1] // MXU_N)]
    return jnp.concatenate([_dot(hi[:, b], bd) + _dot(lo[:, b], bd) for b in blocks], axis=1)


def _sgu_forward(pm_ref, wm_ref, bias_ref, lng_ref, lnb_ref, bd_ref, mixed_scr, n_sub):
    ua = pm_ref[:, 0:D_A].astype(F32)
    va = pm_ref[:, D_A:2 * D_A].astype(F32)
    u_act = _gelu(ua)
    v_act = _gelu(va)
    bd = bd_ref[...]
    vc = v_act - _group_mean(v_act, bd)
    rstd = lax.rsqrt(_group_mean(vc * vc, bd) + EPS)
    vhat = vc * rstd
    vln = vhat * lng_ref[...] + lnb_ref[...]
    left = lax.broadcasted_iota(jnp.int32, (CHUNK, LANES), 1) < HEAD_DIM
    for q in range(n_sub):
        rows = slice(q * CHUNK, (q + 1) * CHUNK)
        for p in range(N_HEADS // 2):
            cols = slice(p * LANES, (p + 1) * LANES)
            vp = vln[rows, cols]
            stacked = jnp.concatenate([jnp.where(left, vp, 0.0), jnp.where(left, 0.0, vp)], axis=0).astype(BF16)
            mixed_scr[rows, cols] = _dot(wm_ref[p], stacked) + bias_ref[:, cols]
    return ua, va, u_act, vhat, rstd, vln


def _halo_specs(tm, tokens, width):
    prev = pl.BlockSpec((HALO, width), lambda i: (jnp.maximum(i * (tm // HALO) - 1, 0), 0))
    nxt = pl.BlockSpec((HALO, width), lambda i: (jnp.minimum((i + 1) * (tm // HALO), tokens // HALO - 1), 0))
    return prev, nxt


def _mixer_forward(x, gnorm, scale1p, shift, w_in_t, gate, w_out, wm, bias_full, lng, lnb, convw, og, bd, seq, name, comm=None):
    tokens = x.shape[0]
    tm = _tile_rows(seq)
    per_seq = seq // tm
    n_sub = tm // CHUNK

    def body(x_ref, xp_ref, g_ref, sc_ref, sh_ref, win_ref, gate_ref, wo_ref, wm_ref, bias_ref, lng_ref, lnb_ref, cw_ref,
             og_ref, bd_ref, h_ref, pm_ref, y_ref, xo_ref, o_ref, mixed_scr):
        i = pl.program_id(0)
        first = (i % per_seq) == 0
        xv = x_ref[...]
        h_ref[...] = ((xv * _rms(xv) * g_ref[...]) * sc_ref[0] + sh_ref[0]).astype(BF16)
        for ck in range(D_PROJ // MXU_N):
            cs = slice(ck * MXU_N, (ck + 1) * MXU_N)
            pm_ref[:, cs] = _dot_nt(h_ref[...], win_ref[cs, :]).astype(BF16)
        xp = xp_ref[...]
        hp = ((xp * _rms(xp) * g_ref[...]) * sc_ref[0] + sh_ref[0]).astype(BF16)
        gates_prev = _dot_nt(hp, win_ref[3 * D_A:5 * D_A, :]).astype(BF16).astype(F32)

        _, _, u_act, _, _, _ = _sgu_forward(pm_ref, wm_ref, bias_ref, lng_ref, lnb_ref, bd_ref, mixed_scr, n_sub)
        ya = u_act * mixed_scr[...]
        y_ref[:, 0:D_A] = (ya * _rms(ya) * og_ref[:, 0:D_A]).astype(BF16)

        bg = pm_ref[:, 2 * D_A:3 * D_A].astype(F32)
        z = pm_ref[:, 3 * D_A:4 * D_A].astype(F32) * pm_ref[:, 4 * D_A:5 * D_A].astype(F32)
        zp = jnp.where(first, 0.0, gates_prev[:, 0:D_A] * gates_prev[:, D_A:2 * D_A])
        zext = jnp.concatenate([zp, z], axis=0)
        z1 = pltpu.roll(zext, 1, 0)[HALO:]
        z2 = pltpu.roll(zext, 2, 0)[HALO:]
        conv = cw_ref[0:1, :] * z2 + cw_ref[1:2, :] * z1 + cw_ref[2:3, :] * z
        yb = bg * conv
        y_ref[:, D_A:2 * D_A] = (yb * _rms(yb) * og_ref[:, D_A:2 * D_A]).astype(BF16)

        f = _dot(y_ref[...], wo_ref[...])
        o_ref[...] = f.astype(BF16)
        xo_ref[...] = xv + gate_ref[0] * f

    prev, _ = _halo_specs(tm, tokens, D_MODEL)
    tile = pl.BlockSpec((tm, D_MODEL), lambda i: (i, 0))
    per_batch = pl.BlockSpec((1, 1, D_MODEL), lambda i: (i // per_seq, 0, 0))
    bf = lambda cols: jax.ShapeDtypeStruct((tokens, cols), BF16)
    outs, got = _call(
        body, name=name, grid=(tokens // tm,),
        out_shape=[bf(D_MODEL), bf(D_PROJ), bf(D_MODEL), jax.ShapeDtypeStruct((tokens, D_MODEL), F32), bf(D_MODEL)],
        in_specs=[tile, prev, _resident((1, D_MODEL)), per_batch, per_batch, _resident(w_in_t.shape), per_batch,
                  _resident(w_out.shape), _resident(wm.shape), _resident(bias_full.shape), _resident(lng.shape),
                  _resident(lnb.shape), _resident(convw.shape), _resident(og.shape), _resident(bd.shape)],
        out_specs=[tile, pl.BlockSpec((tm, D_PROJ), lambda i: (i, 0)), tile, tile, tile],
        scratch_shapes=[pltpu.VMEM((tm, D_A), F32)],
        operands=(x, x, gnorm, scale1p, shift, w_in_t, gate, w_out, wm, bias_full, lng, lnb, convw, og, bd),
        parallel=True, comm=comm)
    return (*outs, got)


def _mixer_backward(proj, dx, gate, o, w_out, x, gnorm, scale1p, w_in_t, wm, bias_full, lng, lnb, convw, og, bd, wm_rows,
                    causal, seq, name, comm=None):
    tokens = proj.shape[0]
    batch = tokens // seq
    tm = _tile_rows(seq)
    per_seq = seq // tm
    n_sub = tm // CHUNK
    ext = tm + 2 * HALO

    def body(pm_ref, pp_ref, pn_ref, dx_ref, dxn_ref, gate_ref, o_ref, wo_ref, x_ref, g_ref, sc_ref, win_ref, wm_ref, bias_ref,
             lng_ref, lnb_ref, cw_ref, og_ref, bd_ref, wmr_ref, causal_ref, do_ref, dgate_ref, dp_ref, dxo_ref, dsh_ref, dsc_ref,
             dgn_ref,
             dog_ref, dcw_ref, dlng_ref, dlnb_ref, dbias_ref, dwm_ref, mixed_scr, dvln_scr, dy_scr):
        i = pl.program_id(0)
        first = (i % per_seq) == 0
        last = (i % per_seq) == per_seq - 1

        @pl.when(i == 0)
        def _():
            dog_ref[...] = jnp.zeros_like(dog_ref)
            dcw_ref[...] = jnp.zeros_like(dcw_ref)
            dlng_ref[...] = jnp.zeros_like(dlng_ref)
            dlnb_ref[...] = jnp.zeros_like(dlnb_ref)
            dbias_ref[...] = jnp.zeros_like(dbias_ref)
            dwm_ref[...] = jnp.zeros_like(dwm_ref)

        dx_v = dx_ref[...]
        do_ref[...] = (gate_ref[0] * dx_v).astype(BF16)
        dgate = jnp.sum(dx_v * o_ref[...].astype(F32), axis=0, keepdims=True)
        dy_scr[...] = _dot_nt(do_ref[...], wo_ref[...])
        dyn_conv = _dot_nt((gate_ref[0] * dxn_ref[...]).astype(BF16), wo_ref[D_A:2 * D_A, :])

        ua, va, u_act, vhat, rstd, vln = _sgu_forward(pm_ref, wm_ref, bias_ref, lng_ref, lnb_ref, bd_ref, mixed_scr, n_sub)
        mixed = mixed_scr[...]
        ya = u_act * mixed
        ra = _rms(ya)
        yhat = ya * ra
        dya_in = dy_scr[:, 0:D_A]
        dog_ref[:, 0:D_A] = dog_ref[:, 0:D_A] + jnp.sum(dya_in * yhat, axis=0, keepdims=True)
        dyh = dya_in * og_ref[:, 0:D_A]
        dya = ra * (dyh - yhat * jnp.mean(dyh * yhat, axis=-1, keepdims=True))
        d_u = dya * mixed
        d_mixed = dya * u_act
        left = lax.broadcasted_iota(jnp.int32, (CHUNK, LANES), 1) < HEAD_DIM
        dbias = jnp.zeros((CHUNK, D_A), F32)
        for q in range(n_sub):
            rows = slice(q * CHUNK, (q + 1) * CHUNK)
            dbias = dbias + d_mixed[rows, :]
            for p in range(N_HEADS // 2):
                cols = slice(p * LANES, (p + 1) * LANES)
                dm = d_mixed[rows, cols]
                stacked = jnp.concatenate([jnp.where(left, dm, 0.0), jnp.where(left, 0.0, dm)], axis=0).astype(BF16)
                dw = _dot_nt(stacked, vln[rows, cols].astype(BF16))
                dwm_ref[2 * p] = dwm_ref[2 * p] + causal_ref[...] * dw[0:CHUNK]
                dwm_ref[2 * p + 1] = dwm_ref[2 * p + 1] + causal_ref[...] * dw[CHUNK:2 * CHUNK]
                dvln_scr[rows, cols] = _dot_tn(wmr_ref[p], stacked)
        dbias_ref[...] = dbias_ref[...] + dbias
        dvln = dvln_scr[...]
        dlng_ref[...] = dlng_ref[...] + jnp.sum(dvln * vhat, axis=0, keepdims=True)
        dlnb_ref[...] = dlnb_ref[...] + jnp.sum(dvln, axis=0, keepdims=True)
        dvh = dvln * lng_ref[...]
        bd = bd_ref[...]
        d_v = rstd * (dvh - _group_mean(dvh, bd) - vhat * _group_mean(dvh * vhat, bd))
        dp_ref[:, 0:D_A] = (d_u * _gelu_grad(ua)).astype(BF16)
        dp_ref[:, D_A:2 * D_A] = (d_v * _gelu_grad(va)).astype(BF16)
        dh_a = _dot(dp_ref[:, 0:2 * D_A], win_ref[0:2 * D_A, :])

        def ext_cols(lo):
            cs = slice(lo, lo + D_A)
            return jnp.concatenate([pp_ref[:, cs], pm_ref[:, cs], pn_ref[:, cs]], axis=0).astype(F32)

        bg, cg, xb = ext_cols(2 * D_A), ext_cols(3 * D_A), ext_cols(4 * D_A)
        row = lax.broadcasted_iota(jnp.int32, (ext, D_A), 0)
        z = jnp.where(jnp.logical_and(first, row < HALO), 0.0, cg * xb)
        z1 = pltpu.roll(z, 1, 0)
        z2 = pltpu.roll(z, 2, 0)
        w0, w1, w2 = cw_ref[0:1, :], cw_ref[1:2, :], cw_ref[2:3, :]
        conv = w0 * z2 + w1 * z1 + w2 * z
        yb = bg * conv
        rb = _rms(yb)
        yhb = yb * rb
        dyn = jnp.where(last, 0.0, dyn_conv)
        dyb_in = jnp.concatenate([jnp.zeros((HALO, D_A), F32), dy_scr[:, D_A:2 * D_A], dyn], axis=0)
        dyhb = dyb_in * og_ref[:, D_A:2 * D_A]
        dyb = rb * (dyhb - yhb * jnp.mean(dyhb * yhb, axis=-1, keepdims=True))
        d_conv = dyb * bg
        dz = w2 * d_conv + w1 * pltpu.roll(d_conv, ext - 1, 0) + w0 * pltpu.roll(d_conv, ext - 2, 0)
        main = slice(HALO, HALO + tm)
        dp_ref[:, 2 * D_A:3 * D_A] = (dyb * conv)[main].astype(BF16)
        dp_ref[:, 3 * D_A:4 * D_A] = (dz * xb)[main].astype(BF16)
        dp_ref[:, 4 * D_A:5 * D_A] = (dz * cg)[main].astype(BF16)
        dog_ref[:, D_A:2 * D_A] = dog_ref[:, D_A:2 * D_A] + jnp.sum((dyb_in * yhb)[main], axis=0, keepdims=True)
        dcm = d_conv[main]
        dcw_ref[0:1, :] = dcw_ref[0:1, :] + jnp.sum(dcm * z2[main], axis=0, keepdims=True)
        dcw_ref[1:2, :] = dcw_ref[1:2, :] + jnp.sum(dcm * z1[main], axis=0, keepdims=True)
        dcw_ref[2:3, :] = dcw_ref[2:3, :] + jnp.sum(dcm * z[main], axis=0, keepdims=True)

        dh = dh_a + _dot(dp_ref[:, 2 * D_A:5 * D_A], win_ref[2 * D_A:5 * D_A, :])
        xv = x_ref[...]
        r = _rms(xv)
        xn = xv * r
        gn = g_ref[...]
        dsh = jnp.sum(dh, axis=0, keepdims=True)
        dsc = jnp.sum(dh * (xn * gn), axis=0, keepdims=True)
        dhn = dh * sc_ref[0]
        dgn = jnp.sum(dhn * xn, axis=0, keepdims=True)
        dxn = dhn * gn
        dxo_ref[...] = dx_v + r * (dxn - xn * jnp.mean(dxn * xn, axis=-1, keepdims=True))

        @pl.when(first)
        def _():
            dgate_ref[0] = dgate
            dsh_ref[0] = dsh
            dsc_ref[0] = dsc

        @pl.when(jnp.logical_not(first))
        def _():
            dgate_ref[0] = dgate_ref[0] + dgate
            dsh_ref[0] = dsh_ref[0] + dsh
            dsc_ref[0] = dsc_ref[0] + dsc

        @pl.when(i == 0)
        def _():
            dgn_ref[...] = dgn

        @pl.when(i != 0)
        def _():
            dgn_ref[...] = dgn_ref[...] + dgn

    prev_p, next_p = _halo_specs(tm, tokens, D_PROJ)
    _, next_d = _halo_specs(tm, tokens, D_MODEL)
    fixed2 = lambda shape: pl.BlockSpec(shape, lambda i: (0, 0))
    tile = pl.BlockSpec((tm, D_MODEL), lambda i: (i, 0))
    per_batch = pl.BlockSpec((1, 1, D_MODEL), lambda i: (i // per_seq, 0, 0))
    vec = jax.ShapeDtypeStruct((batch, 1, D_MODEL), F32)
    outs, got = _call(
        body, name=name, grid=(tokens // tm,),
        out_shape=[jax.ShapeDtypeStruct((tokens, D_MODEL), BF16), vec, jax.ShapeDtypeStruct((tokens, D_PROJ), BF16),
                   jax.ShapeDtypeStruct((tokens, D_MODEL), F32), vec, vec, jax.ShapeDtypeStruct((1, D_MODEL), F32),
                   jax.ShapeDtypeStruct((1, D_MODEL), F32), jax.ShapeDtypeStruct((8, D_A), F32),
                   jax.ShapeDtypeStruct((1, D_A), F32), jax.ShapeDtypeStruct((1, D_A), F32),
                   jax.ShapeDtypeStruct((CHUNK, D_A), F32), jax.ShapeDtypeStruct((N_HEADS, CHUNK, CHUNK), F32)],
        in_specs=[pl.BlockSpec((tm, D_PROJ), lambda i: (i, 0)), prev_p, next_p, tile, next_d, per_batch, tile,
                  _resident(w_out.shape), tile, _resident((1, D_MODEL)), per_batch, _resident(w_in_t.shape),
                  _resident(wm.shape), _resident(bias_full.shape), _resident(lng.shape), _resident(lnb.shape),
                  _resident(convw.shape), _resident(og.shape), _resident(bd.shape), _resident(wm_rows.shape),
                  _resident(causal.shape)],
        out_specs=[tile, per_batch, pl.BlockSpec((tm, D_PROJ), lambda i: (i, 0)), tile, per_batch, per_batch,
                   fixed2((1, D_MODEL)), fixed2((1, D_MODEL)), fixed2((8, D_A)), fixed2((1, D_A)), fixed2((1, D_A)),
                   fixed2((CHUNK, D_A)), pl.BlockSpec((N_HEADS, CHUNK, CHUNK), lambda i: (0, 0, 0))],
        scratch_shapes=[pltpu.VMEM((tm, D_A), F32), pltpu.VMEM((tm, D_A), F32), pltpu.VMEM((tm, D_MODEL), F32)],
        operands=(proj, proj, proj, dx, dx, gate, o, w_out, x, gnorm, scale1p, w_in_t, wm, bias_full, lng, lnb, convw, og, bd,
                  wm_rows, causal), comm=comm)
    return (*outs, got)


def _adamw_update(wv, gv, mv, vv):
    nm = ADAM_B1 * mv + (1.0 - ADAM_B1) * gv
    nv = ADAM_B2 * vv + (1.0 - ADAM_B2) * (gv * gv)
    m_hat = nm / (1.0 - ADAM_B1 ** ADAM_STEP)
    v_hat = nv / (1.0 - ADAM_B2 ** ADAM_STEP)
    return -ADAM_LR * (m_hat / (jnp.sqrt(v_hat) + ADAM_EPS) + ADAM_WD * wv), nm, nv


def _adamw_rows(recv, w, m, v, name):
    depth, rows, cols = w.shape
    tr = rows // 2
    last = rows // tr - 1

    def body(*refs):
        r_refs, (w_ref, m_ref, v_ref, g_ref, d_ref, nm_ref, nv_ref) = refs[:depth], refs[depth:]
        for l in range(depth):
            @pl.when(pl.program_id(0) == l)
            def _(r_ref=r_refs[l]):
                acc = r_ref[0].astype(F32)
                for d in range(1, N_DEV):
                    acc = acc + r_ref[d].astype(F32)
                g_ref[0] = acc
                d_ref[0], nm_ref[0], nv_ref[0] = _adamw_update(w_ref[0], acc, m_ref[0], v_ref[0])

    def slots(l):
        return pl.BlockSpec((N_DEV, tr, cols), lambda ll, i: (0, jnp.where(ll == l, i, jnp.where(ll < l, 0, last)), 0))

    spec = pl.BlockSpec((1, tr, cols), lambda ll, i: (ll, i, 0))
    return pl.pallas_call(
        body, name=name, grid=(depth, rows // tr),
        out_shape=[jax.ShapeDtypeStruct((depth, rows, cols), F32)] * 4,
        in_specs=[slots(l) for l in range(depth)] + [spec] * 3, out_specs=[spec] * 4,
        compiler_params=_params(2),
    )(*recv, w, m, v)


def _adamw(w, g, m, v, name):
    rows, cols = w.shape
    tr = max(t for t in range(8, 513, 8) if rows % t == 0)

    def body(w_ref, g_ref, m_ref, v_ref, d_ref, nm_ref, nv_ref):
        d_ref[...], nm_ref[...], nv_ref[...] = _adamw_update(w_ref[...], g_ref[...], m_ref[...], v_ref[...])

    spec = pl.BlockSpec((tr, cols), lambda i: (i, 0))
    return pl.pallas_call(
        body, name=name, grid=(rows // tr,),
        out_shape=[jax.ShapeDtypeStruct((rows, cols), F32)] * 3,
        in_specs=[spec] * 4, out_specs=[spec] * 3,
        compiler_params=_params(parallel=True),
    )(w, g, m, v)


def _adamw_many(ws, gs, ms, vs, name):
    n = len(ws)
    two_d = lambda a: a.reshape(-1, a.shape[-1])

    def body(*refs):
        w_refs, g_refs, m_refs, v_refs = refs[:n], refs[n:2 * n], refs[2 * n:3 * n], refs[3 * n:4 * n]
        d_refs, nm_refs, nv_refs = refs[4 * n:5 * n], refs[5 * n:6 * n], refs[6 * n:]
        for k in range(n):
            d_refs[k][...], nm_refs[k][...], nv_refs[k][...] = _adamw_update(
                w_refs[k][...], g_refs[k][...], m_refs[k][...], v_refs[k][...])

    flat = [two_d(a) for a in ws]
    outs = pl.pallas_call(
        body, name=name, out_shape=[jax.ShapeDtypeStruct(a.shape, F32) for a in flat] * 3,
        in_specs=[pl.BlockSpec(memory_space=pltpu.VMEM)] * (4 * n),
        out_specs=[pl.BlockSpec(memory_space=pltpu.VMEM)] * (3 * n),
        compiler_params=pltpu.CompilerParams(vmem_limit_bytes=VMEM_LIMIT),
    )(*flat, *[two_d(a) for a in gs], *[two_d(a) for a in ms], *[two_d(a) for a in vs])
    shaped = [o.reshape(ws[k % n].shape) for k, o in enumerate(outs)]
    return shaped[:n], shaped[n:2 * n], shaped[2 * n:]


def _adamw_nd(w, g, m, v, name):
    shape = w.shape
    two_d = (-1, shape[-1])
    d, nm, nv = _adamw(w.reshape(two_d), g.reshape(two_d), m.reshape(two_d), v.reshape(two_d), name)
    return d.reshape(shape), nm.reshape(shape), nv.reshape(shape)


def kernel(x, c, ada_w, ada_b, norm_ffn1_g, ffn1_w_gu, ffn1_w_down, norm_mix_g, mix_w_in, sgu_ln_g, sgu_ln_b, sgu_w_s, sgu_b, conv_w, out_norm_g, mix_w_out, norm_ffn2_g, ffn2_w_gu, ffn2_w_down, final_norm_g, loss_target, m_ada_w, m_ada_b, m_norm_ffn1_g, m_ffn1_w_gu, m_ffn1_w_down, m_norm_mix_g, m_mix_w_in, m_sgu_ln_g, m_sgu_ln_b, m_sgu_w_s, m_sgu_b, m_conv_w, m_out_norm_g, m_mix_w_out, m_norm_ffn2_g, m_ffn2_w_gu, m_ffn2_w_down, m_final_norm_g, v_ada_w, v_ada_b, v_norm_ffn1_g, v_ffn1_w_gu, v_ffn1_w_down, v_norm_mix_g, v_mix_w_in, v_sgu_ln_g, v_sgu_ln_b, v_sgu_w_s, v_sgu_b, v_conv_w, v_out_norm_g, v_mix_w_out, v_norm_ffn2_g, v_ffn2_w_gu, v_ffn2_w_down, v_final_norm_g):
    batch, seq, _ = x.shape
    tokens = batch * seq
    me = 4 * lax.axis_index("x") + 2 * lax.axis_index("y") + lax.axis_index("c")
    weights = dict(ada_w=ada_w, ada_b=ada_b, norm_ffn1_g=norm_ffn1_g, ffn1_w_gu=ffn1_w_gu, ffn1_w_down=ffn1_w_down,
                   norm_mix_g=norm_mix_g, mix_w_in=mix_w_in, sgu_ln_g=sgu_ln_g, sgu_ln_b=sgu_ln_b, sgu_w_s=sgu_w_s,
                   sgu_b=sgu_b, conv_w=conv_w, out_norm_g=out_norm_g, mix_w_out=mix_w_out, norm_ffn2_g=norm_ffn2_g,
                   ffn2_w_gu=ffn2_w_gu, ffn2_w_down=ffn2_w_down, final_norm_g=final_norm_g)
    mom1 = dict(ada_w=m_ada_w, ada_b=m_ada_b, norm_ffn1_g=m_norm_ffn1_g, ffn1_w_gu=m_ffn1_w_gu,
                ffn1_w_down=m_ffn1_w_down, norm_mix_g=m_norm_mix_g, mix_w_in=m_mix_w_in, sgu_ln_g=m_sgu_ln_g,
                sgu_ln_b=m_sgu_ln_b, sgu_w_s=m_sgu_w_s, sgu_b=m_sgu_b, conv_w=m_conv_w, out_norm_g=m_out_norm_g,
                mix_w_out=m_mix_w_out, norm_ffn2_g=m_norm_ffn2_g, ffn2_w_gu=m_ffn2_w_gu, ffn2_w_down=m_ffn2_w_down,
                final_norm_g=m_final_norm_g)
    mom2 = dict(ada_w=v_ada_w, ada_b=v_ada_b, norm_ffn1_g=v_norm_ffn1_g, ffn1_w_gu=v_ffn1_w_gu,
                ffn1_w_down=v_ffn1_w_down, norm_mix_g=v_norm_mix_g, mix_w_in=v_mix_w_in, sgu_ln_g=v_sgu_ln_g,
                sgu_ln_b=v_sgu_ln_b, sgu_w_s=v_sgu_w_s, sgu_b=v_sgu_b, conv_w=v_conv_w, out_norm_g=v_out_norm_g,
                mix_w_out=v_mix_w_out, norm_ffn2_g=v_norm_ffn2_g, ffn2_w_gu=v_ffn2_w_gu, ffn2_w_down=v_ffn2_w_down,
                final_norm_g=v_final_norm_g)

    big = ("ffn1_w_gu", "ffn1_w_down", "mix_w_in", "mix_w_out", "ffn2_w_gu", "ffn2_w_down")
    transposed = ("ffn1_w_gu", "mix_w_in", "ffn2_w_gu")
    as_rows = lambda nm, a: jnp.swapaxes(a, 1, 2) if nm in transposed else a
    shard = {(l, nm): as_rows(nm, weights[nm])[l].astype(BF16) for l in range(DEPTH) for nm in big}
    full_w = {}

    def gather_of(keys):
        return keys, _GatherRows([shard[k] for k in keys])

    def landed(plan, got):
        full_w.update(zip(plan[0], got))

    ada_cols = ada_w.shape[2]
    ada_b_cols = lax.dynamic_slice_in_dim(ada_b, me * ada_cols, ada_cols, axis=1).reshape(DEPTH, 1, ada_cols)
    plan = gather_of([(0, "ffn1_w_gu")])
    c_dev, convw_dev, ada_recv, got = _prologue(
        jnp.pad(c, ((0, 8 - batch), (0, 0))), jnp.pad(conv_w.reshape(-1), (0, 8 * LANES - conv_w.size)).reshape(8, LANES),
        ada_w, ada_b_cols, plan[1])
    landed(plan, got)
    c_all = c_dev[:, :batch].reshape(N_DEV * batch, D_MODEL)
    convw_all = convw_dev.reshape(N_DEV, -1)[:, :conv_w.size].reshape((N_DEV,) + conv_w.shape)
    convw_full = jnp.transpose(convw_all, (1, 2, 0, 3)).reshape(DEPTH, 3, D_A)
    ada_mine = jnp.transpose(ada_recv[:, :, :batch, :], (1, 2, 0, 3)).reshape(DEPTH, batch, N_MOD * D_MODEL)
    mod = ada_mine.reshape(DEPTH, batch, N_MOD, 1, D_MODEL)

    causal = jnp.tril(jnp.ones((CHUNK, CHUNK), F32))
    bd = jnp.kron(jnp.eye(MXU_N // HEAD_DIM, dtype=F32), jnp.full((HEAD_DIM, HEAD_DIM), 1.0 / HEAD_DIM, F32)).astype(BF16)
    row_vec = lambda a: a.reshape(1, -1)

    hosted_gathers = {
        (0, "ffn1"): [(0, "ffn1_w_down"), (0, "mix_w_in"), (0, "mix_w_out")],
        (0, "ffn_down1"): [(0, "ffn2_w_gu")],
        (0, "mix_in"): [(0, "ffn2_w_down")],
        (0, "ffn2"): [(1, "ffn1_w_gu"), (1, "ffn1_w_down"), (1, "mix_w_in"), (1, "mix_w_out")],
        (1, "ffn1"): [(1, "ffn2_w_gu"), (1, "ffn2_w_down")],
    }

    def hosting(l, site):
        keys = hosted_gathers.get((l, site))
        return gather_of(keys) if keys else (None, None)

    xs = x.reshape(tokens, D_MODEL)
    saved = []
    for l in range(DEPTH):
        sh1, sc1, g1, sh2, sc2, g2, sh3, sc3, g3 = [mod[l, :, k] for k in range(N_MOD)]
        wm_masked = (sgu_w_s[l] * causal[None]).astype(BF16)
        mixer_consts = dict(
            wm=jnp.concatenate([wm_masked[0::2], wm_masked[1::2]], axis=2),
            bias_full=jnp.repeat(sgu_b[l].T, HEAD_DIM, axis=1),
            lng=row_vec(jnp.tile(sgu_ln_g[l], N_HEADS)), lnb=row_vec(jnp.tile(sgu_ln_b[l], N_HEADS)),
            convw=jnp.pad(convw_full[l], ((0, 5), (0, 0))), og=row_vec(out_norm_g[l]), bd=bd)
        x0 = xs
        plan = hosting(l, "ffn1")
        if l == 0:
            h1, a1, s1, w1, got = _normmod_matmul(x0, row_vec(norm_ffn1_g[l]), 1.0 + sc1, sh1, full_w[l, "ffn1_w_gu"], seq, "ffn_up", plan[1])
            landed(plan, got)
            plan = hosting(l, "ffn_down1")
            x1, f1, got = _matmul_residual(a1, full_w[l, "ffn1_w_down"], x0, g1, 0.5, seq, "ffn_down", plan[1])
        else:
            h1, a1, s1, w1, x1, f1, got = _ffn_forward(
                x0, row_vec(norm_ffn1_g[l]), 1.0 + sc1, sh1, full_w[l, "ffn1_w_gu"], full_w[l, "ffn1_w_down"], g1, 0.5, seq, "ffn_fwd",
                comm=plan[1])
        if got:
            landed(plan, got)
        plan = hosting(l, "mix_in")
        h2, proj, ymix, x2, o2, got = _mixer_forward(
            x1, row_vec(norm_mix_g[l]), 1.0 + sc2, sh2, full_w[l, "mix_w_in"], g2, full_w[l, "mix_w_out"], seq=seq,
            name="mixer_forward", comm=plan[1], **mixer_consts)
        if got:
            landed(plan, got)
        plan = hosting(l, "ffn2")
        if l + 1 < DEPTH:
            h3, a3, s3, w3, x3, f3, got = _ffn_forward(
                x2, row_vec(norm_ffn2_g[l]), 1.0 + sc3, sh3, full_w[l, "ffn2_w_gu"], full_w[l, "ffn2_w_down"], g3, 0.5, seq, "ffn_fwd",
                comm=plan[1])
        else:
            head = (loss_target.reshape(tokens, D_MODEL), row_vec(final_norm_g))
            h3, a3, s3, w3, x3, f3, d_final_g, loss_cols, got = _ffn_forward(
                x2, row_vec(norm_ffn2_g[l]), 1.0 + sc3, sh3, full_w[l, "ffn2_w_gu"], full_w[l, "ffn2_w_down"], g3, 0.5, seq, "ffn_fwd_loss",
                loss_head=head, comm=plan[1])
        if got:
            landed(plan, got)
        saved.append(dict(x0=x0, x1=x1, x2=x2, h1=h1, h2=h2, h3=h3, a1=a1, s1=s1, w1=w1, a3=a3, s3=s3, w3=w3, f1=f1, f3=f3, o2=o2, proj=proj,
                          ymix=ymix, mixer_consts=mixer_consts, wm_rows=wm_masked.reshape(N_HEADS // 2, 2 * CHUNK, CHUNK), sc=(1.0 + sc1, 1.0 + sc2, 1.0 + sc3), gates=(g1, g2, g3)))
        xs = x3

    dx = xs

    recv = {}
    small_grads = [None] * DEPTH
    d_mod = [None] * DEPTH

    mix_names = ("out_norm_g", "sgu_ln_g", "sgu_ln_b", "sgu_w_s", "sgu_b", "conv_w")
    late_names = ("norm_ffn1_g", "norm_mix_g", "norm_ffn2_g")

    def mix_parts(l):
        return [small_grads[l][nm] for nm in mix_names]

    def late_parts(l):
        return [small_grads[l][nm] for nm in late_names] + [d_mod[l]]

    pending = []

    def scatter_later(l, nm, grad):
        pending.append(((l, nm), _ScatterRows([grad])))

    def host():
        keys, parts = [k for k, _ in pending], [p for _, p in pending]
        pending.clear()
        return keys, (_Exchanges(parts) if parts else None)

    def hosted(keys, got):
        if got:
            recv.update(zip(keys, got))

    for l in reversed(range(DEPTH)):
        sv = saved[l]
        mc = sv["mixer_consts"]
        if l + 1 < DEPTH:
            pending.append((("late", l + 1), _GatherRows([_pack_small(late_parts(l + 1))])))
        keys, comm = host()
        df3, dg3, dgu3, dx2, dsh3, dsc3, dn3, got = _ffn_backward(
            dx, sv["gates"][2], sv["f3"], sv["s3"], sv["w3"], full_w[l, "ffn2_w_down"], full_w[l, "ffn2_w_gu"], sv["x2"],
            row_vec(norm_ffn2_g[l]), sv["sc"][2], 0.5, seq, "ffn_bwd", comm)
        hosted(keys, got)
        gw_down2, _ = _weight_grad(sv["a3"], df3, seq, "grad_w_down")
        scatter_later(l, "ffn2_w_down", gw_down2)
        keys, comm = host()
        gw_gu2, got = _weight_grad(dgu3, sv["h3"], seq, "grad_w_gu", comm)
        hosted(keys, got)
        scatter_later(l, "ffn2_w_gu", gw_gu2)
        keys, comm = host()
        do2, dg2, dproj, dx1, dsh2, dsc2, dn2, d_og, d_cw, d_lng, d_lnb, d_bias, d_wm, got = _mixer_backward(
            sv["proj"], dx2, sv["gates"][1], sv["o2"], full_w[l, "mix_w_out"], sv["x1"], row_vec(norm_mix_g[l]), sv["sc"][1],
            full_w[l, "mix_w_in"], wm_rows=sv["wm_rows"], causal=causal, seq=seq, name="mixer_backward", comm=comm, **mc)
        hosted(keys, got)
        small_grads[l] = dict(
            out_norm_g=d_og, sgu_ln_g=d_lng.reshape(N_HEADS, HEAD_DIM).sum(0), sgu_ln_b=d_lnb.reshape(N_HEADS, HEAD_DIM).sum(0),
            sgu_w_s=d_wm, sgu_b=d_bias.reshape(CHUNK, N_HEADS, HEAD_DIM).sum(-1).T, conv_w=d_cw[0:3])
        gw_out, _ = _weight_grad(sv["ymix"], do2, seq, "grad_w_out")
        scatter_later(l, "mix_w_out", gw_out)
        keys, comm = host()
        gw_in, got = _weight_grad(dproj, sv["h2"], seq, "grad_w_in", comm)
        hosted(keys, got)
        scatter_later(l, "mix_w_in", gw_in)
        keys, comm = host()
        pending.append((("mix", l), _GatherRows([_pack_small(mix_parts(l))])))
        if l > 0:
            df1, dg1, dgu1, dx0, dsh1, dsc1, dn1, got = _ffn_backward(
                dx1, sv["gates"][0], sv["f1"], sv["s1"], sv["w1"], full_w[l, "ffn1_w_down"], full_w[l, "ffn1_w_gu"], sv["x0"],
                row_vec(norm_ffn1_g[l]), sv["sc"][0], 0.5, seq, "ffn_bwd", comm)
        else:
            df1, dg1, dgu1, got = _residual_backward(dx1, sv["gates"][0], sv["f1"], full_w[l, "ffn1_w_down"], 0.5, sv["s1"], sv["w1"], seq, "ffn_down_bwd", comm)
        hosted(keys, got)
        gw_down1, _ = _weight_grad(sv["a1"], df1, seq, "grad_w_down")
        scatter_later(l, "ffn1_w_down", gw_down1)
        keys, comm = host()
        gw_gu1, got = _weight_grad(dgu1, sv["h1"], seq, "grad_w_gu", comm)
        hosted(keys, got)
        scatter_later(l, "ffn1_w_gu", gw_gu1)
        if l == 0:
            keys, comm = host()
            dx0, dsh1, dsc1, dn1, got = _matmul_normmod_backward(dgu1, full_w[l, "ffn1_w_gu"], sv["x0"], dx1, row_vec(norm_ffn1_g[l]), sv["sc"][0], seq, "ffn_up_bwd", comm)
            hosted(keys, got)
        dx = dx0
        small_grads[l].update(norm_ffn1_g=dn1, norm_mix_g=dn2, norm_ffn2_g=dn3)
        d_mod[l] = jnp.concatenate([dsh1, dsc1, dg1, dsh2, dsc2, dg2, dsh3, dsc3, dg3], axis=1)
    grad_x = dx.reshape(batch, seq, D_MODEL)

    grad_big, delta, new_m, new_v = {}, {}, {}, {}
    for nm in big:
        results = _adamw_rows([recv[l, nm] for l in range(DEPTH)], as_rows(nm, weights[nm]), as_rows(nm, mom1[nm]),
                              as_rows(nm, mom2[nm]), "adamw_" + nm)
        grad_big[nm], delta[nm], new_m[nm], new_v[nm] = [as_rows(nm, r) for r in results]

    last_parts = late_parts(0) + [d_final_g, loss_cols]
    last_shapes = [p.shape for p in last_parts]
    packed_all, packed_sum = _all_gather_small(_pack_small(last_parts), "reduce_small")
    late_sum = {0: _unpack_small(packed_sum, last_shapes)}
    d_mod_dev = {0: _unpack_small(packed_all, last_shapes, lead=(N_DEV,))[len(late_names)]}
    mix_sum = {}
    for l in range(DEPTH):
        gathered = recv["mix", l].reshape(N_DEV, -1, LANES)
        mix_sum[l] = _unpack_small(_sum_gathered(gathered, "sum_mix"), [p.shape for p in mix_parts(l)])
        if l > 0:
            shapes_l = [p.shape for p in late_parts(l)]
            gathered = recv["late", l].reshape(N_DEV, -1, LANES)
            late_sum[l] = _unpack_small(_sum_gathered(gathered, "sum_late"), shapes_l)
            d_mod_dev[l] = _unpack_small(gathered, shapes_l, lead=(N_DEV,))[len(late_names)]
    grad_small = {}
    for group, names in ((mix_sum, mix_names), (late_sum, late_names)):
        for k, nm in enumerate(names):
            grad_small[nm] = jnp.stack([group[l][k] for l in range(DEPTH)]).reshape(
                (DEPTH, 3, D_A) if nm == "conv_w" else weights[nm].shape)
    grad_small["conv_w"] = lax.dynamic_slice_in_dim(grad_small["conv_w"], me * conv_w.shape[2], conv_w.shape[2], axis=2)
    grad_small["final_norm_g"] = late_sum[0][len(late_names) + 1].reshape(final_norm_g.shape)
    loss = jnp.sum(late_sum[0][len(late_names) + 2])
    d_ada_all = jnp.stack([d_mod_dev[l] for l in range(DEPTH)]).reshape(DEPTH, N_DEV * batch, N_MOD * D_MODEL)
    d_ada_cols = lax.dynamic_slice_in_dim(d_ada_all, me * ada_cols, ada_cols, axis=2)
    g_ada_w, g_ada_b = _ada_backward(c_all, d_ada_cols, d_ada_all)

    grads = dict(grad_big)
    grads.update(grad_small)
    grads["ada_w"] = g_ada_w
    grads["ada_b"] = g_ada_b.reshape(ada_b.shape)

    names = ("ada_w", "ada_b", "norm_ffn1_g", "ffn1_w_gu", "ffn1_w_down", "norm_mix_g", "mix_w_in", "sgu_ln_g",
             "sgu_ln_b", "sgu_w_s", "sgu_b", "conv_w", "out_norm_g", "mix_w_out", "norm_ffn2_g", "ffn2_w_gu",
             "ffn2_w_down", "final_norm_g")
    delta["ada_w"], new_m["ada_w"], new_v["ada_w"] = _adamw_nd(ada_w, grads["ada_w"], m_ada_w, v_ada_w, "adamw_ada_w")
    rest = [nm for nm in names if nm not in big and nm != "ada_w"]
    pick = lambda src: [src[nm] for nm in rest]
    for nm, d_k, m_k, v_k in zip(rest, *_adamw_many(pick(weights), pick(grads), pick(mom1), pick(mom2), "adamw_small")):
        delta[nm], new_m[nm], new_v[nm] = d_k, m_k, v_k

    return (loss, grad_x, *[grads[nm] for nm in names], *[delta[nm] for nm in names],
            *[new_m[nm] for nm in names], *[new_v[nm] for nm in names])
```

```python
import math

import jax
import jax.numpy as jnp
from jax import lax
from jax.experimental import pallas as pl
from jax.experimental.pallas import tpu as pltpu

F32 = jnp.float32
BF16 = jnp.bfloat16

D_MODEL = 1024
D_A = 512
D_PROJ = 2560
N_HEADS = 8
HEAD_DIM = 64
CHUNK = 128
N_MOD = 9
DEPTH = 2
EPS = 1e-6
N_DEV = 8
LANES = 128
MXU_N = 256
HALO = 16
VMEM_LIMIT = 56 * 1024 * 1024
FORWARD_STEPS = 4

ADAM_LR = 0.001
ADAM_B1 = 0.9
ADAM_B2 = 0.999
ADAM_EPS = 1e-08
ADAM_WD = 0.01
ADAM_STEP = 10

MESH = pl.DeviceIdType.MESH


def _dot(a, b):
    return jnp.dot(a, b, preferred_element_type=F32)


def _dot_nt(a, b):
    return lax.dot_general(a, b, (((1,), (1,)), ((), ())), preferred_element_type=F32)


def _dot_tn(a, b):
    return lax.dot_general(a, b, (((0,), (0,)), ((), ())), preferred_element_type=F32)


def _sigmoid(x):
    return 0.5 * jnp.tanh(0.5 * x) + 0.5


def _gelu(x):
    return 0.5 * x * (1.0 + lax.erf(x * (1.0 / math.sqrt(2.0))))


def _gelu_grad(x):
    cdf = 0.5 * (1.0 + lax.erf(x * (1.0 / math.sqrt(2.0))))
    return cdf + x * jnp.exp(-0.5 * x * x) * (1.0 / math.sqrt(2.0 * math.pi))


def _params(n_axes=1, parallel=False):
    sem = ("parallel" if parallel else "arbitrary",) * n_axes
    return pltpu.CompilerParams(dimension_semantics=sem, vmem_limit_bytes=VMEM_LIMIT)


def _resident(shape):
    nd = len(shape)
    return pl.BlockSpec(shape, lambda *_: (0,) * nd, pipeline_mode=pl.Buffered(1))


def _tile_rows(seq):
    return min(512, seq)


def _my_position():
    x, y, c = lax.axis_index("x"), lax.axis_index("y"), lax.axis_index("c")
    return x, y, c, 4 * x + 2 * y + c


def _peer(x, y, c, p):
    return (x ^ ((p >> 2) & 1), y ^ ((p >> 1) & 1), c ^ (p & 1))


class _GatherRows:
    def __init__(self, shards):
        self.operands = list(shards)
        n = len(shards)
        self.out_shape = [jax.ShapeDtypeStruct((N_DEV * s.shape[0], s.shape[1]), s.dtype) for s in shards]
        self.scratch = [pltpu.SemaphoreType.DMA((n, N_DEV - 1)), pltpu.SemaphoreType.DMA((n, N_DEV - 1)),
                        pltpu.SemaphoreType.DMA((n,))]

    def _plan(self, src, dst, send, recv, loc):
        x, y, c, _ = _my_position()
        me, sib = (x, y, c), (x, y, 1 - c)
        chips = [(1 - x, y), (x, 1 - y), (1 - x, 1 - y)]
        plans = []
        for k, shard in enumerate(self.operands):
            rows = shard.shape[0]

            def blk(pos, k=k, rows=rows):
                return dst[k].at[pl.ds((4 * pos[0] + 2 * pos[1] + pos[2]) * rows, rows), :]

            def rc(s, block, to, source=None, k=k, blk=blk):
                return pltpu.make_async_remote_copy(
                    src_ref=blk(block) if source is None else source, dst_ref=blk(block),
                    send_sem=send.at[k, s], recv_sem=recv.at[k, s], device_id=to, device_id_type=MESH)

            plans.append(dict(
                local=pltpu.make_async_copy(src[k], blk(me), loc.at[k]),
                first=[rc(0, me, sib, src[k])] + [rc(1 + j, me, (*chip, c), src[k]) for j, chip in enumerate(chips)],
                landed=[rc(1 + j, (*chip, c), me) for j, chip in enumerate(chips)],
                passed=[rc(4 + j, (*chip, c), sib) for j, chip in enumerate(chips)],
                from_sib=[rc(0, sib, me)] + [rc(4 + j, (*chip, 1 - c), me) for j, chip in enumerate(chips)]))
        return plans

    def start(self, src, dst, send, recv, loc):
        for plan in self._plan(src, dst, send, recv, loc):
            plan["local"].start()
            for cp in plan["first"]:
                cp.start()

    def forward(self, src, dst, send, recv, loc):
        for plan in self._plan(src, dst, send, recv, loc):
            for landed, passed in zip(plan["landed"], plan["passed"]):
                landed.wait_recv()
                passed.start()

    def finish(self, src, dst, send, recv, loc):
        for plan in self._plan(src, dst, send, recv, loc):
            for cp in plan["from_sib"]:
                cp.wait_recv()
            for cp in plan["first"] + plan["passed"]:
                cp.wait_send()
            plan["local"].wait()


class _ScatterRows:
    def __init__(self, grads):
        self.operands = list(grads)
        n = len(grads)
        self.out_shape = [jax.ShapeDtypeStruct((N_DEV, g.shape[0] // N_DEV, g.shape[1]), g.dtype) for g in grads]
        self.scratch = [pltpu.SemaphoreType.DMA((n, N_DEV - 1)), pltpu.SemaphoreType.DMA((n, N_DEV - 1)),
                        pltpu.SemaphoreType.DMA((n,))]

    def _plan(self, src, dst, send, recv, loc):
        x, y, c, me = _my_position()
        copies = []
        for k, grad in enumerate(self.operands):
            rows = grad.shape[0] // N_DEV
            copies.append(pltpu.make_async_copy(src[k].at[pl.ds(me * rows, rows), :], dst[k].at[me], loc.at[k]))
            for p in range(1, N_DEV):
                px, py, pc = _peer(x, y, c, p)
                copies.append(pltpu.make_async_remote_copy(
                    src_ref=src[k].at[pl.ds((4 * px + 2 * py + pc) * rows, rows), :], dst_ref=dst[k].at[me],
                    send_sem=send.at[k, p - 1], recv_sem=recv.at[k, p - 1], device_id=(px, py, pc), device_id_type=MESH))
        return copies

    def start(self, src, dst, send, recv, loc):
        for cp in self._plan(src, dst, send, recv, loc):
            cp.start()

    def forward(self, src, dst, send, recv, loc):
        pass

    def finish(self, src, dst, send, recv, loc):
        for cp in self._plan(src, dst, send, recv, loc):
            cp.wait()


class _Exchanges:
    def __init__(self, parts):
        self.parts = list(parts)
        self.operands = [op for part in self.parts for op in part.operands]
        self.out_shape = [shp for part in self.parts for shp in part.out_shape]
        self.scratch = [scr for part in self.parts for scr in part.scratch]

    def _each(self, src, dst, sems):
        at, sem_at = 0, 0
        for part in self.parts:
            n, n_sem = len(part.operands), len(part.scratch)
            yield part, src[at:at + n], dst[at:at + n], sems[sem_at:sem_at + n_sem]
            at, sem_at = at + n, sem_at + n_sem

    def start(self, src, dst, *sems):
        for part, part_src, part_dst, part_sems in self._each(src, dst, sems):
            part.start(part_src, part_dst, *part_sems)

    def forward(self, src, dst, *sems):
        for part, part_src, part_dst, part_sems in self._each(src, dst, sems):
            part.forward(part_src, part_dst, *part_sems)

    def finish(self, src, dst, *sems):
        for part, part_src, part_dst, part_sems in self._each(src, dst, sems):
            part.finish(part_src, part_dst, *part_sems)


_ANY = pl.BlockSpec(memory_space=pl.ANY)


def _call(body, *, name, grid, in_specs, out_specs, out_shape, operands, scratch_shapes=(), parallel=False, comm=None):
    n_axes = len(grid)
    if comm is None:
        outs = pl.pallas_call(body, name=name, grid=grid, out_shape=list(out_shape), in_specs=list(in_specs),
                              out_specs=list(out_specs), scratch_shapes=list(scratch_shapes),
                              compiler_params=_params(n_axes, parallel))(*operands)
        return list(outs), None
    n_in, n_out, n_scr, n_c = len(in_specs), len(out_specs), len(scratch_shapes), len(comm.operands)
    total = math.prod(grid)

    def hosted(*refs):
        ins, c_src = refs[:n_in], refs[n_in:n_in + n_c]
        outs, c_dst = refs[n_in + n_c:n_in + n_c + n_out], refs[n_in + n_c + n_out:n_in + 2 * n_c + n_out]
        scr, sems = refs[n_in + 2 * n_c + n_out:n_in + 2 * n_c + n_out + n_scr], refs[n_in + 2 * n_c + n_out + n_scr:]
        step = pl.program_id(0)
        for axis in range(1, n_axes):
            step = step * grid[axis] + pl.program_id(axis)

        @pl.when(step == 0)
        def _():
            comm.start(c_src, c_dst, *sems)

        @pl.when(step == max(total - FORWARD_STEPS, 0))
        def _():
            comm.forward(c_src, c_dst, *sems)

        body(*ins, *outs, *scr)

        @pl.when(step == total - 1)
        def _():
            comm.finish(c_src, c_dst, *sems)

    res = pl.pallas_call(hosted, name=name, grid=grid, out_shape=list(out_shape) + comm.out_shape,
                         in_specs=list(in_specs) + [_ANY] * n_c, out_specs=list(out_specs) + [_ANY] * n_c,
                         scratch_shapes=list(scratch_shapes) + comm.scratch,
                         compiler_params=_params(n_axes, False))(*operands, *comm.operands)
    return list(res[:n_out]), list(res[n_out:])


def _all_gather_small(v, name):
    rows = v.shape[0]

    def body(v_ref, all_ref, sum_ref, send_sems, recv_sems):
        x, y, c, me = _my_position()
        all_ref[me] = v_ref[...]
        copies = []
        for p in range(1, N_DEV):
            cp = pltpu.make_async_remote_copy(
                src_ref=v_ref, dst_ref=all_ref.at[me], send_sem=send_sems.at[p - 1], recv_sem=recv_sems.at[p - 1],
                device_id=_peer(x, y, c, p), device_id_type=MESH)
            cp.start()
            copies.append(cp)
        for cp in copies:
            cp.wait()
        acc = all_ref[0]
        for d in range(1, N_DEV):
            acc = acc + all_ref[d]
        sum_ref[...] = acc

    return pl.pallas_call(
        body, name=name,
        out_shape=[jax.ShapeDtypeStruct((N_DEV, rows, LANES), F32), jax.ShapeDtypeStruct((rows, LANES), F32)],
        in_specs=[pl.BlockSpec(memory_space=pltpu.VMEM)],
        out_specs=[pl.BlockSpec(memory_space=pltpu.VMEM)] * 2,
        scratch_shapes=[pltpu.SemaphoreType.DMA((N_DEV - 1,)), pltpu.SemaphoreType.DMA((N_DEV - 1,))],
        compiler_params=pltpu.CompilerParams(vmem_limit_bytes=VMEM_LIMIT),
    )(v)


def _sum_gathered(gathered, name):
    rows = gathered.shape[1]

    def body(g_ref, o_ref):
        acc = g_ref[0]
        for d in range(1, N_DEV):
            acc = acc + g_ref[d]
        o_ref[...] = acc

    return pl.pallas_call(
        body, name=name, out_shape=jax.ShapeDtypeStruct((rows, LANES), F32),
        in_specs=[pl.BlockSpec(memory_space=pltpu.VMEM)], out_specs=pl.BlockSpec(memory_space=pltpu.VMEM),
        compiler_params=pltpu.CompilerParams(vmem_limit_bytes=VMEM_LIMIT),
    )(gathered)


def _pack_small(parts):
    flat = jnp.concatenate([p.reshape(-1).astype(F32) for p in parts])
    total = flat.shape[0]
    padded = -(-total // (8 * LANES)) * (8 * LANES)
    flat = jnp.pad(flat, (0, padded - total))
    return flat.reshape(padded // LANES, LANES)


def _unpack_small(packed, shapes, lead=()):
    flat = packed.reshape(lead + (-1,))
    out, off = [], 0
    for shp in shapes:
        size = math.prod(shp)
        out.append(flat[..., off:off + size].reshape(lead + tuple(shp)))
        off += size
    return out


def _prologue(c_rows, convw_rows, ada_w, ada_b_cols, gather):
    depth, _, cols = ada_w.shape
    n_c = len(gather.operands)
    sub = 8

    def body(c_ref, cw_ref, b_ref, w_hbm, *rest):
        g_src, (c_all_ref, cw_all_ref, ada_ref), g_dst = rest[:n_c], rest[n_c:n_c + 3], rest[n_c + 3:2 * n_c + 3]
        ada_local, w_ref, w_sem, send_sems, recv_sems = rest[2 * n_c + 3:2 * n_c + 8]
        g_sems = rest[2 * n_c + 8:]
        x, y, c, me = _my_position()
        gather.start(g_src, g_dst, *g_sems)
        load_w = pltpu.make_async_copy(w_hbm, w_ref, w_sem)
        load_w.start()

        def to_all(k, src_ref, dst_ref):
            copies = []
            for p in range(1, N_DEV):
                copies.append(pltpu.make_async_remote_copy(
                    src_ref=src_ref, dst_ref=dst_ref.at[me], send_sem=send_sems.at[k, p - 1], recv_sem=recv_sems.at[k, p - 1],
                    device_id=_peer(x, y, c, p), device_id_type=MESH))
            return copies

        first = to_all(0, c_ref, c_all_ref) + to_all(1, cw_ref, cw_all_ref)
        c_all_ref[me] = c_ref[...]
        cw_all_ref[me] = cw_ref[...]
        for cp in first:
            cp.start()
        for cp in first:
            cp.wait()
        cv = c_all_ref[...].reshape(N_DEV * sub, D_MODEL)
        act = (cv * _sigmoid(cv)).astype(BF16)
        load_w.wait()
        for l in range(depth):
            ada_local[l] = _dot(act, w_ref[l].astype(BF16)) + b_ref[l]
        ada_ref[me] = ada_local[:, pl.ds(pl.multiple_of(me * sub, sub), sub), :]
        rows_out = []
        for p in range(1, N_DEV):
            px, py, pc = _peer(x, y, c, p)
            rows = pl.ds(pl.multiple_of((4 * px + 2 * py + pc) * sub, sub), sub)
            rows_out.append(pltpu.make_async_remote_copy(
                src_ref=ada_local.at[:, rows, :], dst_ref=ada_ref.at[me], send_sem=send_sems.at[2, p - 1],
                recv_sem=recv_sems.at[2, p - 1], device_id=(px, py, pc), device_id_type=MESH))
        for cp in rows_out:
            cp.start()
        for cp in rows_out:
            cp.wait()
        gather.forward(g_src, g_dst, *g_sems)
        gather.finish(g_src, g_dst, *g_sems)

    vmem = pl.BlockSpec(memory_space=pltpu.VMEM)
    outs = pl.pallas_call(
        body, name="prologue",
        out_shape=[jax.ShapeDtypeStruct((N_DEV, sub, D_MODEL), F32), jax.ShapeDtypeStruct((N_DEV, sub, LANES), F32),
                   jax.ShapeDtypeStruct((N_DEV, depth, sub, cols), F32)] + gather.out_shape,
        in_specs=[vmem] * 3 + [_ANY] * (1 + n_c), out_specs=[vmem] * 3 + [_ANY] * n_c,
        scratch_shapes=[pltpu.VMEM((depth, N_DEV * sub, cols), F32), pltpu.VMEM(ada_w.shape, F32), pltpu.SemaphoreType.DMA,
                        pltpu.SemaphoreType.DMA((3, N_DEV - 1)), pltpu.SemaphoreType.DMA((3, N_DEV - 1))] + gather.scratch,
        compiler_params=pltpu.CompilerParams(vmem_limit_bytes=VMEM_LIMIT),
    )(c_rows, convw_rows, ada_b_cols, ada_w, *gather.operands)
    return outs[0], outs[1], outs[2], list(outs[3:])


def _ada_backward(c_all, d_ada_cols, d_ada_all):
    nb = c_all.shape[0]
    cols = d_ada_cols.shape[2]
    full = d_ada_all.shape[2]

    def body(c_ref, dc_ref, da_ref, gw_ref, gb_ref):
        cv = c_ref[...]
        act = (cv * _sigmoid(cv)).astype(BF16)
        gw_ref[0] = _dot_tn(act, dc_ref[0].astype(BF16))
        gb_ref[0] = jnp.sum(da_ref[0], axis=0, keepdims=True)

    return pl.pallas_call(
        body, name="ada_backward", grid=(DEPTH,),
        out_shape=[jax.ShapeDtypeStruct((DEPTH, D_MODEL, cols), F32), jax.ShapeDtypeStruct((DEPTH, 1, full), F32)],
        in_specs=[pl.BlockSpec((nb, D_MODEL), lambda l: (0, 0)),
                  pl.BlockSpec((1, nb, cols), lambda l: (l, 0, 0)),
                  pl.BlockSpec((1, nb, full), lambda l: (l, 0, 0))],
        out_specs=[pl.BlockSpec((1, D_MODEL, cols), lambda l: (l, 0, 0)),
                   pl.BlockSpec((1, 1, full), lambda l: (l, 0, 0))],
        compiler_params=_params(),
    )(c_all, d_ada_cols, d_ada_all)


def _rms(xv):
    return lax.rsqrt(jnp.mean(xv * xv, axis=-1, keepdims=True) + EPS)


def _normmod_matmul(x, gnorm, scale1p, shift, w_t, seq, name, comm=None):
    tokens, width = x.shape[0], w_t.shape[0] // 2
    tm = _tile_rows(seq)
    per_seq = seq // tm
    n_chunks = width // MXU_N

    def body(x_ref, g_ref, sc_ref, sh_ref, w_ref, h_ref, act_ref, silu_ref, dact_ref):
        xv = x_ref[...]
        h = (xv * _rms(xv) * g_ref[...]) * sc_ref[0] + sh_ref[0]
        h_ref[...] = h.astype(BF16)
        for ck in range(n_chunks):
            cs = slice(ck * MXU_N, (ck + 1) * MXU_N)
            g = _dot_nt(h_ref[...], w_ref[cs, :])
            u = _dot_nt(h_ref[...], w_ref[width + ck * MXU_N:width + (ck + 1) * MXU_N, :])
            sig = _sigmoid(g)
            silu = g * sig
            act_ref[:, cs] = (silu * u).astype(BF16)
            silu_ref[:, cs] = silu.astype(BF16)
            dact_ref[:, cs] = (u * (sig + silu * (1.0 - sig))).astype(BF16)

    per_batch = pl.BlockSpec((1, 1, D_MODEL), lambda i: (i // per_seq, 0, 0))
    outs, got = _call(
        body, name=name, grid=(tokens // tm,),
        out_shape=[jax.ShapeDtypeStruct((tokens, D_MODEL), BF16)] + [jax.ShapeDtypeStruct((tokens, width), BF16)] * 3,
        in_specs=[pl.BlockSpec((tm, D_MODEL), lambda i: (i, 0)), _resident((1, D_MODEL)), per_batch, per_batch,
                  _resident(w_t.shape)],
        out_specs=[pl.BlockSpec((tm, D_MODEL), lambda i: (i, 0))] + [pl.BlockSpec((tm, width), lambda i: (i, 0))] * 3,
        operands=(x, gnorm, scale1p, shift, w_t), parallel=True, comm=comm)
    return (*outs, got)


def _matmul_residual(src, w, x, gate, scale, seq, name, comm=None):
    tokens, k_dim = x.shape[0], w.shape[0]
    tm = _tile_rows(seq)
    per_seq = seq // tm

    def body(s_ref, w_ref, x_ref, gate_ref, xo_ref, f_ref):
        f = _dot(s_ref[...], w_ref[...])
        f_ref[...] = f.astype(BF16)
        xo_ref[...] = x_ref[...] + (scale * gate_ref[0]) * f

    (x_out, f), got = _call(
        body, name=name, grid=(tokens // tm,),
        out_shape=[jax.ShapeDtypeStruct((tokens, D_MODEL), F32), jax.ShapeDtypeStruct((tokens, D_MODEL), BF16)],
        in_specs=[pl.BlockSpec((tm, k_dim), lambda i: (i, 0)), _resident(w.shape),
                  pl.BlockSpec((tm, D_MODEL), lambda i: (i, 0)),
                  pl.BlockSpec((1, 1, D_MODEL), lambda i: (i // per_seq, 0, 0))],
        out_specs=[pl.BlockSpec((tm, D_MODEL), lambda i: (i, 0))] * 2,
        operands=(src, w, x, gate), parallel=True, comm=comm)
    return x_out, f, got


def _loss_tile(xv, target, gn):
    r = _rms(xv)
    xn = xv * r
    err = xn * gn - target
    loss = (0.5 / D_MODEL) * jnp.sum(err * err, axis=0, keepdims=True)
    dyv = err * (1.0 / D_MODEL)
    dg = jnp.sum(dyv * xn, axis=0, keepdims=True)
    dxn = dyv * gn
    dx = r * (dxn - xn * jnp.mean(dxn * xn, axis=-1, keepdims=True))
    return loss, dx, dg


def _ffn_forward(x, gnorm, scale1p, shift, w_gu_t, w_down, gate, scale, seq, name, loss_head=None, comm=None):
    tokens, width = x.shape[0], w_down.shape[0]
    tm = _tile_rows(seq)
    per_seq = seq // tm
    n_chunks = width // MXU_N

    def body(x_ref, g_ref, sc_ref, sh_ref, wgu_ref, wd_ref, gate_ref, *rest):
        if loss_head:
            t_ref, gf_ref, h_ref, act_ref, silu_ref, dact_ref, xo_ref, f_ref, dgf_ref, loss_ref = rest
        else:
            h_ref, act_ref, silu_ref, dact_ref, xo_ref, f_ref = rest
        xv = x_ref[...]
        h = (xv * _rms(xv) * g_ref[...]) * sc_ref[0] + sh_ref[0]
        h_ref[...] = h.astype(BF16)
        for ck in range(n_chunks):
            cs = slice(ck * MXU_N, (ck + 1) * MXU_N)
            g = _dot_nt(h_ref[...], wgu_ref[cs, :])
            u = _dot_nt(h_ref[...], wgu_ref[width + ck * MXU_N:width + (ck + 1) * MXU_N, :])
            sig = _sigmoid(g)
            silu = g * sig
            act_ref[:, cs] = (silu * u).astype(BF16)
            silu_ref[:, cs] = silu.astype(BF16)
            dact_ref[:, cs] = (u * (sig + silu * (1.0 - sig))).astype(BF16)
        f = _dot(act_ref[...], wd_ref[...])
        f_ref[...] = f.astype(BF16)
        x_out = xv + (scale * gate_ref[0]) * f
        if loss_head:
            i = pl.program_id(0)
            loss, dx, dg = _loss_tile(x_out, t_ref[...], gf_ref[...])
            xo_ref[...] = dx

            @pl.when(i == 0)
            def _():
                dgf_ref[...] = dg
                loss_ref[...] = loss

            @pl.when(i != 0)
            def _():
                dgf_ref[...] = dgf_ref[...] + dg
                loss_ref[...] = loss_ref[...] + loss
        else:
            xo_ref[...] = x_out

    row = lambda i: (i, 0)
    per_batch = pl.BlockSpec((1, 1, D_MODEL), lambda i: (i // per_seq, 0, 0))
    tile = lambda cols: pl.BlockSpec((tm, cols), row)
    wide = jax.ShapeDtypeStruct((tokens, width), BF16)
    fixed = pl.BlockSpec((1, D_MODEL), lambda i: (0, 0))
    vec = jax.ShapeDtypeStruct((1, D_MODEL), F32)
    outs, got = _call(
        body, name=name, grid=(tokens // tm,),
        out_shape=[jax.ShapeDtypeStruct((tokens, D_MODEL), BF16), wide, wide, wide,
                   jax.ShapeDtypeStruct((tokens, D_MODEL), F32), jax.ShapeDtypeStruct((tokens, D_MODEL), BF16)]
        + ([vec, vec] if loss_head else []),
        in_specs=[tile(D_MODEL), _resident((1, D_MODEL)), per_batch, per_batch, _resident(w_gu_t.shape),
                  _resident(w_down.shape), per_batch] + ([tile(D_MODEL), _resident((1, D_MODEL))] if loss_head else []),
        out_specs=[tile(D_MODEL), tile(width), tile(width), tile(width), tile(D_MODEL), tile(D_MODEL)]
        + ([fixed, fixed] if loss_head else []),
        operands=(x, gnorm, scale1p, shift, w_gu_t, w_down, gate) + (tuple(loss_head) if loss_head else ()),
        parallel=not loss_head, comm=comm)
    return (*outs, got)


def _residual_backward(dy, gate, f, w, scale, silu, dact, seq, name, comm=None):
    tokens, k_dim = dy.shape[0], w.shape[0]
    batch = tokens // seq
    tm = _tile_rows(seq)
    per_seq = seq // tm
    n_chunks = k_dim // MXU_N

    def body(dy_ref, gate_ref, f_ref, silu_ref, dact_ref, w_ref, df_ref, dgate_ref, dgu_ref):
        i = pl.program_id(0)
        dy_v = dy_ref[...]
        df_ref[...] = ((scale * gate_ref[0]) * dy_v).astype(BF16)
        part = scale * jnp.sum(dy_v * f_ref[...].astype(F32), axis=0, keepdims=True)
        for ck in range(n_chunks):
            cs = slice(ck * MXU_N, (ck + 1) * MXU_N)
            cu = slice(k_dim + ck * MXU_N, k_dim + (ck + 1) * MXU_N)
            da = _dot_nt(df_ref[...], w_ref[cs, :])
            dgu_ref[:, cs] = (da * dact_ref[:, cs].astype(F32)).astype(BF16)
            dgu_ref[:, cu] = (da * silu_ref[:, cs].astype(F32)).astype(BF16)

        @pl.when(i % per_seq == 0)
        def _():
            dgate_ref[0] = part

        @pl.when(i % per_seq != 0)
        def _():
            dgate_ref[0] = dgate_ref[0] + part

    row = lambda i: (i, 0)
    per_batch = pl.BlockSpec((1, 1, D_MODEL), lambda i: (i // per_seq, 0, 0))
    tile = lambda cols: pl.BlockSpec((tm, cols), row)
    outs, got = _call(
        body, name=name, grid=(tokens // tm,),
        out_shape=[jax.ShapeDtypeStruct((tokens, D_MODEL), BF16), jax.ShapeDtypeStruct((batch, 1, D_MODEL), F32),
                   jax.ShapeDtypeStruct((tokens, 2 * k_dim), BF16)],
        in_specs=[tile(D_MODEL), per_batch, tile(D_MODEL), tile(k_dim), tile(k_dim), _resident(w.shape)],
        out_specs=[tile(D_MODEL), per_batch, tile(2 * k_dim)],
        operands=(dy, gate, f, silu, dact, w), comm=comm)
    return (*outs, got)


def _matmul_normmod_backward(dsrc, w_t, x, dy, gnorm, scale1p, seq, name, comm=None):
    tokens, k_dim = dsrc.shape
    batch = tokens // seq
    tm = _tile_rows(seq)
    per_seq = seq // tm

    def body(ds_ref, w_ref, x_ref, dy_ref, g_ref, sc_ref, dx_ref, dsh_ref, dsc_ref, dg_ref):
        i = pl.program_id(0)
        dh = _dot(ds_ref[...], w_ref[...])
        xv = x_ref[...]
        r = _rms(xv)
        xn = xv * r
        gn = g_ref[...]
        dsh = jnp.sum(dh, axis=0, keepdims=True)
        dsc = jnp.sum(dh * (xn * gn), axis=0, keepdims=True)
        dhn = dh * sc_ref[0]
        dg = jnp.sum(dhn * xn, axis=0, keepdims=True)
        dxn = dhn * gn
        dx_ref[...] = dy_ref[...] + r * (dxn - xn * jnp.mean(dxn * xn, axis=-1, keepdims=True))

        @pl.when(i % per_seq == 0)
        def _():
            dsh_ref[0] = dsh
            dsc_ref[0] = dsc

        @pl.when(i % per_seq != 0)
        def _():
            dsh_ref[0] = dsh_ref[0] + dsh
            dsc_ref[0] = dsc_ref[0] + dsc

        @pl.when(i == 0)
        def _():
            dg_ref[...] = dg

        @pl.when(i != 0)
        def _():
            dg_ref[...] = dg_ref[...] + dg

    row = lambda i: (i, 0)
    per_batch = pl.BlockSpec((1, 1, D_MODEL), lambda i: (i // per_seq, 0, 0))
    outs, got = _call(
        body, name=name, grid=(tokens // tm,),
        out_shape=[jax.ShapeDtypeStruct((tokens, D_MODEL), F32), jax.ShapeDtypeStruct((batch, 1, D_MODEL), F32),
                   jax.ShapeDtypeStruct((batch, 1, D_MODEL), F32), jax.ShapeDtypeStruct((1, D_MODEL), F32)],
        in_specs=[pl.BlockSpec((tm, k_dim), row), _resident(w_t.shape), pl.BlockSpec((tm, D_MODEL), row),
                  pl.BlockSpec((tm, D_MODEL), row), _resident((1, D_MODEL)), per_batch],
        out_specs=[pl.BlockSpec((tm, D_MODEL), row), per_batch, per_batch, pl.BlockSpec((1, D_MODEL), lambda i: (0, 0))],
        operands=(dsrc, w_t, x, dy, gnorm, scale1p), comm=comm)
    return (*outs, got)


def _ffn_backward(dy, gate, f, silu, dact, w_down, w_gu_t, x, gnorm, scale1p, scale, seq, name, comm=None):
    tokens, k_dim = dy.shape[0], w_down.shape[0]
    batch = tokens // seq
    tm = min(256, seq)
    per_seq = seq // tm
    n_chunks = k_dim // MXU_N

    def body(dy_ref, gate_ref, f_ref, silu_ref, dact_ref, wd_ref, wgu_ref, x_ref, g_ref, sc_ref,
             df_ref, dgate_ref, dgu_ref, dx_ref, dsh_ref, dsc_ref, dg_ref):
        i = pl.program_id(0)
        dy_v = dy_ref[...]
        df_ref[...] = ((scale * gate_ref[0]) * dy_v).astype(BF16)
        dgate = scale * jnp.sum(dy_v * f_ref[...].astype(F32), axis=0, keepdims=True)
        for ck in range(n_chunks):
            cs = slice(ck * MXU_N, (ck + 1) * MXU_N)
            cu = slice(k_dim + ck * MXU_N, k_dim + (ck + 1) * MXU_N)
            da = _dot_nt(df_ref[...], wd_ref[cs, :])
            dgu_ref[:, cs] = (da * dact_ref[:, cs].astype(F32)).astype(BF16)
            dgu_ref[:, cu] = (da * silu_ref[:, cs].astype(F32)).astype(BF16)
        dh = _dot(dgu_ref[...], wgu_ref[...])
        xv = x_ref[...]
        r = _rms(xv)
        xn = xv * r
        gn = g_ref[...]
        dsh = jnp.sum(dh, axis=0, keepdims=True)
        dsc = jnp.sum(dh * (xn * gn), axis=0, keepdims=True)
        dhn = dh * sc_ref[0]
        dg = jnp.sum(dhn * xn, axis=0, keepdims=True)
        dxn = dhn * gn
        dx_ref[...] = dy_v + r * (dxn - xn * jnp.mean(dxn * xn, axis=-1, keepdims=True))

        @pl.when(i % per_seq == 0)
        def _():
            dgate_ref[0] = dgate
            dsh_ref[0] = dsh
            dsc_ref[0] = dsc

        @pl.when(i % per_seq != 0)
        def _():
            dgate_ref[0] = dgate_ref[0] + dgate
            dsh_ref[0] = dsh_ref[0] + dsh
            dsc_ref[0] = dsc_ref[0] + dsc

        @pl.when(i == 0)
        def _():
            dg_ref[...] = dg

        @pl.when(i != 0)
        def _():
            dg_ref[...] = dg_ref[...] + dg

    row = lambda i: (i, 0)
    per_batch = pl.BlockSpec((1, 1, D_MODEL), lambda i: (i // per_seq, 0, 0))
    tile = lambda width: pl.BlockSpec((tm, width), row)
    vec = jax.ShapeDtypeStruct((batch, 1, D_MODEL), F32)
    outs, got = _call(
        body, name=name, grid=(tokens // tm,),
        out_shape=[jax.ShapeDtypeStruct((tokens, D_MODEL), BF16), vec, jax.ShapeDtypeStruct((tokens, 2 * k_dim), BF16),
                   jax.ShapeDtypeStruct((tokens, D_MODEL), F32), vec, vec, jax.ShapeDtypeStruct((1, D_MODEL), F32)],
        in_specs=[tile(D_MODEL), per_batch, tile(D_MODEL), tile(k_dim), tile(k_dim), _resident(w_down.shape),
                  _resident(w_gu_t.shape), tile(D_MODEL), _resident((1, D_MODEL)), per_batch],
        out_specs=[tile(D_MODEL), per_batch, tile(2 * k_dim), tile(D_MODEL), per_batch, per_batch,
                   pl.BlockSpec((1, D_MODEL), lambda i: (0, 0))],
        operands=(dy, gate, f, silu, dact, w_down, w_gu_t, x, gnorm, scale1p), comm=comm)
    return (*outs, got)


def _weight_grad(a, b, seq, name, comm=None):
    tokens, n_out = a.shape
    tn = MXU_N

    def body(a_ref, b_ref, o_ref):
        o_ref[...] = _dot_tn(a_ref[...], b_ref[...]).astype(BF16)

    (out,), got = _call(
        body, name=name, grid=(n_out // tn,),
        out_shape=[jax.ShapeDtypeStruct((n_out, D_MODEL), BF16)],
        in_specs=[pl.BlockSpec((tokens, tn), lambda j: (0, j)), _resident((tokens, D_MODEL))],
        out_specs=[pl.BlockSpec((tn, D_MODEL), lambda j: (j, 0))],
        operands=(a, b), comm=comm)
    return out, got


def _group_mean(v, bd):
    hi = v.astype(BF16)
    lo = (v - hi.astype(F32)).astype(BF16)
    blocks = [slice(k * MXU_N, (k + 1) * MXU_N) for k in range(v.shape[1] // MXU_N)]
    return jnp.concatenate([_dot(hi[:, b], bd) + _dot(lo[:, b], bd) for b in blocks], axis=1)


def _sgu_forward(pm_ref, wm_ref, bias_ref, lng_ref, lnb_ref, bd_ref, mixed_scr, n_sub):
    ua = pm_ref[:, 0:D_A].astype(F32)
    va = pm_ref[:, D_A:2 * D_A].astype(F32)
    u_act = _gelu(ua)
    v_act = _gelu(va)
    bd = bd_ref[...]
    vc = v_act - _group_mean(v_act, bd)
    rstd = lax.rsqrt(_group_mean(vc * vc, bd) + EPS)
    vhat = vc * rstd
    vln = vhat * lng_ref[...] + lnb_ref[...]
    left = lax.broadcasted_iota(jnp.int32, (CHUNK, LANES), 1) < HEAD_DIM
    for q in range(n_sub):
        rows = slice(q * CHUNK, (q + 1) * CHUNK)
        for p in range(N_HEADS // 2):
            cols = slice(p * LANES, (p + 1) * LANES)
            vp = vln[rows, cols]
            stacked = jnp.concatenate([jnp.where(left, vp, 0.0), jnp.where(left, 0.0, vp)], axis=0).astype(BF16)
            mixed_scr[rows, cols] = _dot(wm_ref[p], stacked) + bias_ref[:, cols]
    return ua, va, u_act, vhat, rstd, vln


def _halo_specs(tm, tokens, width):
    prev = pl.BlockSpec((HALO, width), lambda i: (jnp.maximum(i * (tm // HALO) - 1, 0), 0))
    nxt = pl.BlockSpec((HALO, width), lambda i: (jnp.minimum((i + 1) * (tm // HALO), tokens // HALO - 1), 0))
    return prev, nxt


def _mixer_forward(x, gnorm, scale1p, shift, w_in_t, gate, w_out, wm, bias_full, lng, lnb, convw, og, bd, seq, name, comm=None):
    tokens = x.shape[0]
    tm = _tile_rows(seq)
    per_seq = seq // tm
    n_sub = tm // CHUNK

    def body(x_ref, xp_ref, g_ref, sc_ref, sh_ref, win_ref, gate_ref, wo_ref, wm_ref, bias_ref, lng_ref, lnb_ref, cw_ref,
             og_ref, bd_ref, h_ref, pm_ref, y_ref, xo_ref, o_ref, mixed_scr):
        i = pl.program_id(0)
        first = (i % per_seq) == 0
        xv = x_ref[...]
        h_ref[...] = ((xv * _rms(xv) * g_ref[...]) * sc_ref[0] + sh_ref[0]).astype(BF16)
        for ck in range(D_PROJ // MXU_N):
            cs = slice(ck * MXU_N, (ck + 1) * MXU_N)
            pm_ref[:, cs] = _dot_nt(h_ref[...], win_ref[cs, :]).astype(BF16)
        xp = xp_ref[...]
        hp = ((xp * _rms(xp) * g_ref[...]) * sc_ref[0] + sh_ref[0]).astype(BF16)
        gates_prev = _dot_nt(hp, win_ref[3 * D_A:5 * D_A, :]).astype(BF16).astype(F32)

        _, _, u_act, _, _, _ = _sgu_forward(pm_ref, wm_ref, bias_ref, lng_ref, lnb_ref, bd_ref, mixed_scr, n_sub)
        ya = u_act * mixed_scr[...]
        y_ref[:, 0:D_A] = (ya * _rms(ya) * og_ref[:, 0:D_A]).astype(BF16)

        bg = pm_ref[:, 2 * D_A:3 * D_A].astype(F32)
        z = pm_ref[:, 3 * D_A:4 * D_A].astype(F32) * pm_ref[:, 4 * D_A:5 * D_A].astype(F32)
        zp = jnp.where(first, 0.0, gates_prev[:, 0:D_A] * gates_prev[:, D_A:2 * D_A])
        zext = jnp.concatenate([zp, z], axis=0)
        z1 = pltpu.roll(zext, 1, 0)[HALO:]
        z2 = pltpu.roll(zext, 2, 0)[HALO:]
        conv = cw_ref[0:1, :] * z2 + cw_ref[1:2, :] * z1 + cw_ref[2:3, :] * z
        yb = bg * conv
        y_ref[:, D_A:2 * D_A] = (yb * _rms(yb) * og_ref[:, D_A:2 * D_A]).astype(BF16)

        f = _dot(y_ref[...], wo_ref[...])
        o_ref[...] = f.astype(BF16)
        xo_ref[...] = xv + gate_ref[0] * f

    prev, _ = _halo_specs(tm, tokens, D_MODEL)
    tile = pl.BlockSpec((tm, D_MODEL), lambda i: (i, 0))
    per_batch = pl.BlockSpec((1, 1, D_MODEL), lambda i: (i // per_seq, 0, 0))
    bf = lambda cols: jax.ShapeDtypeStruct((tokens, cols), BF16)
    outs, got = _call(
        body, name=name, grid=(tokens // tm,),
        out_shape=[bf(D_MODEL), bf(D_PROJ), bf(D_MODEL), jax.ShapeDtypeStruct((tokens, D_MODEL), F32), bf(D_MODEL)],
        in_specs=[tile, prev, _resident((1, D_MODEL)), per_batch, per_batch, _resident(w_in_t.shape), per_batch,
                  _resident(w_out.shape), _resident(wm.shape), _resident(bias_full.shape), _resident(lng.shape),
                  _resident(lnb.shape), _resident(convw.shape), _resident(og.shape), _resident(bd.shape)],
        out_specs=[tile, pl.BlockSpec((tm, D_PROJ), lambda i: (i, 0)), tile, tile, tile],
        scratch_shapes=[pltpu.VMEM((tm, D_A), F32)],
        operands=(x, x, gnorm, scale1p, shift, w_in_t, gate, w_out, wm, bias_full, lng, lnb, convw, og, bd),
        parallel=True, comm=comm)
    return (*outs, got)


def _mixer_backward(proj, dx, gate, o, w_out, x, gnorm, scale1p, w_in_t, wm, bias_full, lng, lnb, convw, og, bd, wm_rows,
                    causal, seq, name, comm=None):
    tokens = proj.shape[0]
    batch = tokens // seq
    tm = _tile_rows(seq)
    per_seq = seq // tm
    n_sub = tm // CHUNK
    ext = tm + 2 * HALO

    def body(pm_ref, pp_ref, pn_ref, dx_ref, dxn_ref, gate_ref, o_ref, wo_ref, x_ref, g_ref, sc_ref, win_ref, wm_ref, bias_ref,
             lng_ref, lnb_ref, cw_ref, og_ref, bd_ref, wmr_ref, causal_ref, do_ref, dgate_ref, dp_ref, dxo_ref, dsh_ref, dsc_ref,
             dgn_ref,
             dog_ref, dcw_ref, dlng_ref, dlnb_ref, dbias_ref, dwm_ref, mixed_scr, dvln_scr, dy_scr):
        i = pl.program_id(0)
        first = (i % per_seq) == 0
        last = (i % per_seq) == per_seq - 1

        @pl.when(i == 0)
        def _():
            dog_ref[...] = jnp.zeros_like(dog_ref)
            dcw_ref[...] = jnp.zeros_like(dcw_ref)
            dlng_ref[...] = jnp.zeros_like(dlng_ref)
            dlnb_ref[...] = jnp.zeros_like(dlnb_ref)
            dbias_ref[...] = jnp.zeros_like(dbias_ref)
            dwm_ref[...] = jnp.zeros_like(dwm_ref)

        dx_v = dx_ref[...]
        do_ref[...] = (gate_ref[0] * dx_v).astype(BF16)
        dgate = jnp.sum(dx_v * o_ref[...].astype(F32), axis=0, keepdims=True)
        dy_scr[...] = _dot_nt(do_ref[...], wo_ref[...])
        dyn_conv = _dot_nt((gate_ref[0] * dxn_ref[...]).astype(BF16), wo_ref[D_A:2 * D_A, :])

        ua, va, u_act, vhat, rstd, vln = _sgu_forward(pm_ref, wm_ref, bias_ref, lng_ref, lnb_ref, bd_ref, mixed_scr, n_sub)
        mixed = mixed_scr[...]
        ya = u_act * mixed
        ra = _rms(ya)
        yhat = ya * ra
        dya_in = dy_scr[:, 0:D_A]
        dog_ref[:, 0:D_A] = dog_ref[:, 0:D_A] + jnp.sum(dya_in * yhat, axis=0, keepdims=True)
        dyh = dya_in * og_ref[:, 0:D_A]
        dya = ra * (dyh - yhat * jnp.mean(dyh * yhat, axis=-1, keepdims=True))
        d_u = dya * mixed
        d_mixed = dya * u_act
        left = lax.broadcasted_iota(jnp.int32, (CHUNK, LANES), 1) < HEAD_DIM
        dbias = jnp.zeros((CHUNK, D_A), F32)
        for q in range(n_sub):
            rows = slice(q * CHUNK, (q + 1) * CHUNK)
            dbias = dbias + d_mixed[rows, :]
            for p in range(N_HEADS // 2):
                cols = slice(p * LANES, (p + 1) * LANES)
                dm = d_mixed[rows, cols]
                stacked = jnp.concatenate([jnp.where(left, dm, 0.0), jnp.where(left, 0.0, dm)], axis=0).astype(BF16)
                dw = _dot_nt(stacked, vln[rows, cols].astype(BF16))
                dwm_ref[2 * p] = dwm_ref[2 * p] + causal_ref[...] * dw[0:CHUNK]
                dwm_ref[2 * p + 1] = dwm_ref[2 * p + 1] + causal_ref[...] * dw[CHUNK:2 * CHUNK]
                dvln_scr[rows, cols] = _dot_tn(wmr_ref[p], stacked)
        dbias_ref[...] = dbias_ref[...] + dbias
        dvln = dvln_scr[...]
        dlng_ref[...] = dlng_ref[...] + jnp.sum(dvln * vhat, axis=0, keepdims=True)
        dlnb_ref[...] = dlnb_ref[...] + jnp.sum(dvln, axis=0, keepdims=True)
        dvh = dvln * lng_ref[...]
        bd = bd_ref[...]
        d_v = rstd * (dvh - _group_mean(dvh, bd) - vhat * _group_mean(dvh * vhat, bd))
        dp_ref[:, 0:D_A] = (d_u * _gelu_grad(ua)).astype(BF16)
        dp_ref[:, D_A:2 * D_A] = (d_v * _gelu_grad(va)).astype(BF16)
        dh_a = _dot(dp_ref[:, 0:2 * D_A], win_ref[0:2 * D_A, :])

        def ext_cols(lo):
            cs = slice(lo, lo + D_A)
            return jnp.concatenate([pp_ref[:, cs], pm_ref[:, cs], pn_ref[:, cs]], axis=0).astype(F32)

        bg, cg, xb = ext_cols(2 * D_A), ext_cols(3 * D_A), ext_cols(4 * D_A)
        row = lax.broadcasted_iota(jnp.int32, (ext, D_A), 0)
        z = jnp.where(jnp.logical_and(first, row < HALO), 0.0, cg * xb)
        z1 = pltpu.roll(z, 1, 0)
        z2 = pltpu.roll(z, 2, 0)
        w0, w1, w2 = cw_ref[0:1, :], cw_ref[1:2, :], cw_ref[2:3, :]
        conv = w0 * z2 + w1 * z1 + w2 * z
        yb = bg * conv
        rb = _rms(yb)
        yhb = yb * rb
        dyn = jnp.where(last, 0.0, dyn_conv)
        dyb_in = jnp.concatenate([jnp.zeros((HALO, D_A), F32), dy_scr[:, D_A:2 * D_A], dyn], axis=0)
        dyhb = dyb_in * og_ref[:, D_A:2 * D_A]
        dyb = rb * (dyhb - yhb * jnp.mean(dyhb * yhb, axis=-1, keepdims=True))
        d_conv = dyb * bg
        dz = w2 * d_conv + w1 * pltpu.roll(d_conv, ext - 1, 0) + w0 * pltpu.roll(d_conv, ext - 2, 0)
        main = slice(HALO, HALO + tm)
        dp_ref[:, 2 * D_A:3 * D_A] = (dyb * conv)[main].astype(BF16)
        dp_ref[:, 3 * D_A:4 * D_A] = (dz * xb)[main].astype(BF16)
        dp_ref[:, 4 * D_A:5 * D_A] = (dz * cg)[main].astype(BF16)
        dog_ref[:, D_A:2 * D_A] = dog_ref[:, D_A:2 * D_A] + jnp.sum((dyb_in * yhb)[main], axis=0, keepdims=True)
        dcm = d_conv[main]
        dcw_ref[0:1, :] = dcw_ref[0:1, :] + jnp.sum(dcm * z2[main], axis=0, keepdims=True)
        dcw_ref[1:2, :] = dcw_ref[1:2, :] + jnp.sum(dcm * z1[main], axis=0, keepdims=True)
        dcw_ref[2:3, :] = dcw_ref[2:3, :] + jnp.sum(dcm * z[main], axis=0, keepdims=True)

        dh = dh_a + _dot(dp_ref[:, 2 * D_A:5 * D_A], win_ref[2 * D_A:5 * D_A, :])
        xv = x_ref[...]
        r = _rms(xv)
        xn = xv * r
        gn = g_ref[...]
        dsh = jnp.sum(dh, axis=0, keepdims=True)
        dsc = jnp.sum(dh * (xn * gn), axis=0, keepdims=True)
        dhn = dh * sc_ref[0]
        dgn = jnp.sum(dhn * xn, axis=0, keepdims=True)
        dxn = dhn * gn
        dxo_ref[...] = dx_v + r * (dxn - xn * jnp.mean(dxn * xn, axis=-1, keepdims=True))

        @pl.when(first)
        def _():
            dgate_ref[0] = dgate
            dsh_ref[0] = dsh
            dsc_ref[0] = dsc

        @pl.when(jnp.logical_not(first))
        def _():
            dgate_ref[0] = dgate_ref[0] + dgate
            dsh_ref[0] = dsh_ref[0] + dsh
            dsc_ref[0] = dsc_ref[0] + dsc

        @pl.when(i == 0)
        def _():
            dgn_ref[...] = dgn

        @pl.when(i != 0)
        def _():
            dgn_ref[...] = dgn_ref[...] + dgn

    prev_p, next_p = _halo_specs(tm, tokens, D_PROJ)
    _, next_d = _halo_specs(tm, tokens, D_MODEL)
    fixed2 = lambda shape: pl.BlockSpec(shape, lambda i: (0, 0))
    tile = pl.BlockSpec((tm, D_MODEL), lambda i: (i, 0))
    per_batch = pl.BlockSpec((1, 1, D_MODEL), lambda i: (i // per_seq, 0, 0))
    vec = jax.ShapeDtypeStruct((batch, 1, D_MODEL), F32)
    outs, got = _call(
        body, name=name, grid=(tokens // tm,),
        out_shape=[jax.ShapeDtypeStruct((tokens, D_MODEL), BF16), vec, jax.ShapeDtypeStruct((tokens, D_PROJ), BF16),
                   jax.ShapeDtypeStruct((tokens, D_MODEL), F32), vec, vec, jax.ShapeDtypeStruct((1, D_MODEL), F32),
                   jax.ShapeDtypeStruct((1, D_MODEL), F32), jax.ShapeDtypeStruct((8, D_A), F32),
                   jax.ShapeDtypeStruct((1, D_A), F32), jax.ShapeDtypeStruct((1, D_A), F32),
                   jax.ShapeDtypeStruct((CHUNK, D_A), F32), jax.ShapeDtypeStruct((N_HEADS, CHUNK, CHUNK), F32)],
        in_specs=[pl.BlockSpec((tm, D_PROJ), lambda i: (i, 0)), prev_p, next_p, tile, next_d, per_batch, tile,
                  _resident(w_out.shape), tile, _resident((1, D_MODEL)), per_batch, _resident(w_in_t.shape),
                  _resident(wm.shape), _resident(bias_full.shape), _resident(lng.shape), _resident(lnb.shape),
                  _resident(convw.shape), _resident(og.shape), _resident(bd.shape), _resident(wm_rows.shape),
                  _resident(causal.shape)],
        out_specs=[tile, per_batch, pl.BlockSpec((tm, D_PROJ), lambda i: (i, 0)), tile, per_batch, per_batch,
                   fixed2((1, D_MODEL)), fixed2((1, D_MODEL)), fixed2((8, D_A)), fixed2((1, D_A)), fixed2((1, D_A)),
                   fixed2((CHUNK, D_A)), pl.BlockSpec((N_HEADS, CHUNK, CHUNK), lambda i: (0, 0, 0))],
        scratch_shapes=[pltpu.VMEM((tm, D_A), F32), pltpu.VMEM((tm, D_A), F32), pltpu.VMEM((tm, D_MODEL), F32)],
        operands=(proj, proj, proj, dx, dx, gate, o, w_out, x, gnorm, scale1p, w_in_t, wm, bias_full, lng, lnb, convw, og, bd,
                  wm_rows, causal), comm=comm)
    return (*outs, got)


def _adamw_update(wv, gv, mv, vv):
    nm = ADAM_B1 * mv + (1.0 - ADAM_B1) * gv
    nv = ADAM_B2 * vv + (1.0 - ADAM_B2) * (gv * gv)
    m_hat = nm / (1.0 - ADAM_B1 ** ADAM_STEP)
    v_hat = nv / (1.0 - ADAM_B2 ** ADAM_STEP)
    return -ADAM_LR * (m_hat / (jnp.sqrt(v_hat) + ADAM_EPS) + ADAM_WD * wv), nm, nv


def _adamw_rows(recv, w, m, v, name):
    depth, rows, cols = w.shape
    tr = rows // 2
    last = rows // tr - 1

    def body(*refs):
        r_refs, (w_ref, m_ref, v_ref, g_ref, d_ref, nm_ref, nv_ref) = refs[:depth], refs[depth:]
        for l in range(depth):
            @pl.when(pl.program_id(0) == l)
            def _(r_ref=r_refs[l]):
                acc = r_ref[0].astype(F32)
                for d in range(1, N_DEV):
                    acc = acc + r_ref[d].astype(F32)
                g_ref[0] = acc
                d_ref[0], nm_ref[0], nv_ref[0] = _adamw_update(w_ref[0], acc, m_ref[0], v_ref[0])

    def slots(l):
        return pl.BlockSpec((N_DEV, tr, cols), lambda ll, i: (0, jnp.where(ll == l, i, jnp.where(ll < l, 0, last)), 0))

    spec = pl.BlockSpec((1, tr, cols), lambda ll, i: (ll, i, 0))
    return pl.pallas_call(
        body, name=name, grid=(depth, rows // tr),
        out_shape=[jax.ShapeDtypeStruct((depth, rows, cols), F32)] * 4,
        in_specs=[slots(l) for l in range(depth)] + [spec] * 3, out_specs=[spec] * 4,
        compiler_params=_params(2),
    )(*recv, w, m, v)


def _adamw(w, g, m, v, name):
    rows, cols = w.shape
    tr = max(t for t in range(8, 513, 8) if rows % t == 0)

    def body(w_ref, g_ref, m_ref, v_ref, d_ref, nm_ref, nv_ref):
        d_ref[...], nm_ref[...], nv_ref[...] = _adamw_update(w_ref[...], g_ref[...], m_ref[...], v_ref[...])

    spec = pl.BlockSpec((tr, cols), lambda i: (i, 0))
    return pl.pallas_call(
        body, name=name, grid=(rows // tr,),
        out_shape=[jax.ShapeDtypeStruct((rows, cols), F32)] * 3,
        in_specs=[spec] * 4, out_specs=[spec] * 3,
        compiler_params=_params(parallel=True),
    )(w, g, m, v)


def _adamw_many(ws, gs, ms, vs, name):
    n = len(ws)
    two_d = lambda a: a.reshape(-1, a.shape[-1])

    def body(*refs):
        w_refs, g_refs, m_refs, v_refs = refs[:n], refs[n:2 * n], refs[2 * n:3 * n], refs[3 * n:4 * n]
        d_refs, nm_refs, nv_refs = refs[4 * n:5 * n], refs[5 * n:6 * n], refs[6 * n:]
        for k in range(n):
            d_refs[k][...], nm_refs[k][...], nv_refs[k][...] = _adamw_update(
                w_refs[k][...], g_refs[k][...], m_refs[k][...], v_refs[k][...])

    flat = [two_d(a) for a in ws]
    outs = pl.pallas_call(
        body, name=name, out_shape=[jax.ShapeDtypeStruct(a.shape, F32) for a in flat] * 3,
        in_specs=[pl.BlockSpec(memory_space=pltpu.VMEM)] * (4 * n),
        out_specs=[pl.BlockSpec(memory_space=pltpu.VMEM)] * (3 * n),
        compiler_params=pltpu.CompilerParams(vmem_limit_bytes=VMEM_LIMIT),
    )(*flat, *[two_d(a) for a in gs], *[two_d(a) for a in ms], *[two_d(a) for a in vs])
    shaped = [o.reshape(ws[k % n].shape) for k, o in enumerate(outs)]
    return shaped[:n], shaped[n:2 * n], shaped[2 * n:]


def _adamw_nd(w, g, m, v, name):
    shape = w.shape
    two_d = (-1, shape[-1])
    d, nm, nv = _adamw(w.reshape(two_d), g.reshape(two_d), m.reshape(two_d), v.reshape(two_d), name)
    return d.reshape(shape), nm.reshape(shape), nv.reshape(shape)


def kernel(x, c, ada_w, ada_b, norm_ffn1_g, ffn1_w_gu, ffn1_w_down, norm_mix_g, mix_w_in, sgu_ln_g, sgu_ln_b, sgu_w_s, sgu_b, conv_w, out_norm_g, mix_w_out, norm_ffn2_g, ffn2_w_gu, ffn2_w_down, final_norm_g, loss_target, m_ada_w, m_ada_b, m_norm_ffn1_g, m_ffn1_w_gu, m_ffn1_w_down, m_norm_mix_g, m_mix_w_in, m_sgu_ln_g, m_sgu_ln_b, m_sgu_w_s, m_sgu_b, m_conv_w, m_out_norm_g, m_mix_w_out, m_norm_ffn2_g, m_ffn2_w_gu, m_ffn2_w_down, m_final_norm_g, v_ada_w, v_ada_b, v_norm_ffn1_g, v_ffn1_w_gu, v_ffn1_w_down, v_norm_mix_g, v_mix_w_in, v_sgu_ln_g, v_sgu_ln_b, v_sgu_w_s, v_sgu_b, v_conv_w, v_out_norm_g, v_mix_w_out, v_norm_ffn2_g, v_ffn2_w_gu, v_ffn2_w_down, v_final_norm_g):
    batch, seq, _ = x.shape
    tokens = batch * seq
    me = 4 * lax.axis_index("x") + 2 * lax.axis_index("y") + lax.axis_index("c")
    weights = dict(ada_w=ada_w, ada_b=ada_b, norm_ffn1_g=norm_ffn1_g, ffn1_w_gu=ffn1_w_gu, ffn1_w_down=ffn1_w_down,
                   norm_mix_g=norm_mix_g, mix_w_in=mix_w_in, sgu_ln_g=sgu_ln_g, sgu_ln_b=sgu_ln_b, sgu_w_s=sgu_w_s,
                   sgu_b=sgu_b, conv_w=conv_w, out_norm_g=out_norm_g, mix_w_out=mix_w_out, norm_ffn2_g=norm_ffn2_g,
                   ffn2_w_gu=ffn2_w_gu, ffn2_w_down=ffn2_w_down, final_norm_g=final_norm_g)
    mom1 = dict(ada_w=m_ada_w, ada_b=m_ada_b, norm_ffn1_g=m_norm_ffn1_g, ffn1_w_gu=m_ffn1_w_gu,
                ffn1_w_down=m_ffn1_w_down, norm_mix_g=m_norm_mix_g, mix_w_in=m_mix_w_in, sgu_ln_g=m_sgu_ln_g,
                sgu_ln_b=m_sgu_ln_b, sgu_w_s=m_sgu_w_s, sgu_b=m_sgu_b, conv_w=m_conv_w, out_norm_g=m_out_norm_g,
                mix_w_out=m_mix_w_out, norm_ffn2_g=m_norm_ffn2_g, ffn2_w_gu=m_ffn2_w_gu, ffn2_w_down=m_ffn2_w_down,
                final_norm_g=m_final_norm_g)
    mom2 = dict(ada_w=v_ada_w, ada_b=v_ada_b, norm_ffn1_g=v_norm_ffn1_g, ffn1_w_gu=v_ffn1_w_gu,
                ffn1_w_down=v_ffn1_w_down, norm_mix_g=v_norm_mix_g, mix_w_in=v_mix_w_in, sgu_ln_g=v_sgu_ln_g,
                sgu_ln_b=v_sgu_ln_b, sgu_w_s=v_sgu_w_s, sgu_b=v_sgu_b, conv_w=v_conv_w, out_norm_g=v_out_norm_g,
                mix_w_out=v_mix_w_out, norm_ffn2_g=v_norm_ffn2_g, ffn2_w_gu=v_ffn2_w_gu, ffn2_w_down=v_ffn2_w_down,
                final_norm_g=v_final_norm_g)

    big = ("ffn1_w_gu", "ffn1_w_down", "mix_w_in", "mix_w_out", "ffn2_w_gu", "ffn2_w_down")
    transposed = ("ffn1_w_gu", "mix_w_in", "ffn2_w_gu")
    as_rows = lambda nm, a: jnp.swapaxes(a, 1, 2) if nm in transposed else a
    shard = {(l, nm): as_rows(nm, weights[nm])[l].astype(BF16) for l in range(DEPTH) for nm in big}
    full_w = {}

    def gather_of(keys):
        return keys, _GatherRows([shard[k] for k in keys])

    def landed(plan, got):
        full_w.update(zip(plan[0], got))

    ada_cols = ada_w.shape[2]
    ada_b_cols = lax.dynamic_slice_in_dim(ada_b, me * ada_cols, ada_cols, axis=1).reshape(DEPTH, 1, ada_cols)
    plan = gather_of([(0, "ffn1_w_gu")])
    c_dev, convw_dev, ada_recv, got = _prologue(
        jnp.pad(c, ((0, 8 - batch), (0, 0))), jnp.pad(conv_w.reshape(-1), (0, 8 * LANES - conv_w.size)).reshape(8, LANES),
        ada_w, ada_b_cols, plan[1])
    landed(plan, got)
    c_all = c_dev[:, :batch].reshape(N_DEV * batch, D_MODEL)
    convw_all = convw_dev.reshape(N_DEV, -1)[:, :conv_w.size].reshape((N_DEV,) + conv_w.shape)
    convw_full = jnp.transpose(convw_all, (1, 2, 0, 3)).reshape(DEPTH, 3, D_A)
    ada_mine = jnp.transpose(ada_recv[:, :, :batch, :], (1, 2, 0, 3)).reshape(DEPTH, batch, N_MOD * D_MODEL)
    mod = ada_mine.reshape(DEPTH, batch, N_MOD, 1, D_MODEL)

    causal = jnp.tril(jnp.ones((CHUNK, CHUNK), F32))
    bd = jnp.kron(jnp.eye(MXU_N // HEAD_DIM, dtype=F32), jnp.full((HEAD_DIM, HEAD_DIM), 1.0 / HEAD_DIM, F32)).astype(BF16)
    row_vec = lambda a: a.reshape(1, -1)

    hosted_gathers = {
        (0, "ffn1"): [(0, "ffn1_w_down"), (0, "mix_w_in"), (0, "mix_w_out")],
        (0, "ffn_down1"): [(0, "ffn2_w_down")],
        (0, "mix_in"): [(0, "ffn2_w_gu")],
        (0, "ffn2"): [(1, "ffn1_w_gu"), (1, "ffn1_w_down"), (1, "mix_w_in"), (1, "mix_w_out")],
        (1, "ffn1"): [(1, "ffn2_w_gu"), (1, "ffn2_w_down")],
    }

    def hosting(l, site):
        keys = hosted_gathers.get((l, site))
        return gather_of(keys) if keys else (None, None)

    xs = x.reshape(tokens, D_MODEL)
    saved = []
    for l in range(DEPTH):
        sh1, sc1, g1, sh2, sc2, g2, sh3, sc3, g3 = [mod[l, :, k] for k in range(N_MOD)]
        wm_masked = (sgu_w_s[l] * causal[None]).astype(BF16)
        mixer_consts = dict(
            wm=jnp.concatenate([wm_masked[0::2], wm_masked[1::2]], axis=2),
            bias_full=jnp.repeat(sgu_b[l].T, HEAD_DIM, axis=1),
            lng=row_vec(jnp.tile(sgu_ln_g[l], N_HEADS)), lnb=row_vec(jnp.tile(sgu_ln_b[l], N_HEADS)),
            convw=jnp.pad(convw_full[l], ((0, 5), (0, 0))), og=row_vec(out_norm_g[l]), bd=bd)
        x0 = xs
        plan = hosting(l, "ffn1")
        if l == 0:
            h1, a1, s1, w1, got = _normmod_matmul(x0, row_vec(norm_ffn1_g[l]), 1.0 + sc1, sh1, full_w[l, "ffn1_w_gu"], seq, "ffn_up", plan[1])
            landed(plan, got)
            plan = hosting(l, "ffn_down1")
            x1, f1, got = _matmul_residual(a1, full_w[l, "ffn1_w_down"], x0, g1, 0.5, seq, "ffn_down", plan[1])
        else:
            h1, a1, s1, w1, x1, f1, got = _ffn_forward(
                x0, row_vec(norm_ffn1_g[l]), 1.0 + sc1, sh1, full_w[l, "ffn1_w_gu"], full_w[l, "ffn1_w_down"], g1, 0.5, seq, "ffn_fwd",
                comm=plan[1])
        if got:
            landed(plan, got)
        plan = hosting(l, "mix_in")
        h2, proj, ymix, x2, o2, got = _mixer_forward(
            x1, row_vec(norm_mix_g[l]), 1.0 + sc2, sh2, full_w[l, "mix_w_in"], g2, full_w[l, "mix_w_out"], seq=seq,
            name="mixer_forward", comm=plan[1], **mixer_consts)
        if got:
            landed(plan, got)
        plan = hosting(l, "ffn2")
        if l + 1 < DEPTH:
            h3, a3, s3, w3, x3, f3, got = _ffn_forward(
                x2, row_vec(norm_ffn2_g[l]), 1.0 + sc3, sh3, full_w[l, "ffn2_w_gu"], full_w[l, "ffn2_w_down"], g3, 0.5, seq, "ffn_fwd",
                comm=plan[1])
        else:
            head = (loss_target.reshape(tokens, D_MODEL), row_vec(final_norm_g))
            h3, a3, s3, w3, x3, f3, d_final_g, loss_cols, got = _ffn_forward(
                x2, row_vec(norm_ffn2_g[l]), 1.0 + sc3, sh3, full_w[l, "ffn2_w_gu"], full_w[l, "ffn2_w_down"], g3, 0.5, seq, "ffn_fwd_loss",
                loss_head=head, comm=plan[1])
        if got:
            landed(plan, got)
        saved.append(dict(x0=x0, x1=x1, x2=x2, h1=h1, h2=h2, h3=h3, a1=a1, s1=s1, w1=w1, a3=a3, s3=s3, w3=w3, f1=f1, f3=f3, o2=o2, proj=proj,
                          ymix=ymix, mixer_consts=mixer_consts, wm_rows=wm_masked.reshape(N_HEADS // 2, 2 * CHUNK, CHUNK), sc=(1.0 + sc1, 1.0 + sc2, 1.0 + sc3), gates=(g1, g2, g3)))
        xs = x3

    dx = xs

    recv = {}
    small_grads = [None] * DEPTH
    d_mod = [None] * DEPTH

    mix_names = ("out_norm_g", "sgu_ln_g", "sgu_ln_b", "sgu_w_s", "sgu_b", "conv_w")
    late_names = ("norm_ffn1_g", "norm_mix_g", "norm_ffn2_g")

    def mix_parts(l):
        return [small_grads[l][nm] for nm in mix_names]

    def late_parts(l):
        return [small_grads[l][nm] for nm in late_names] + [d_mod[l]]

    pending = []

    def scatter_later(l, nm, grad):
        pending.append(((l, nm), _ScatterRows([grad])))

    def host():
        keys, parts = [k for k, _ in pending], [p for _, p in pending]
        pending.clear()
        return keys, (_Exchanges(parts) if parts else None)

    def hosted(keys, got):
        if got:
            recv.update(zip(keys, got))

    for l in reversed(range(DEPTH)):
        sv = saved[l]
        mc = sv["mixer_consts"]
        if l + 1 < DEPTH:
            pending.append((("late", l + 1), _GatherRows([_pack_small(late_parts(l + 1))])))
        keys, comm = host()
        df3, dg3, dgu3, dx2, dsh3, dsc3, dn3, got = _ffn_backward(
            dx, sv["gates"][2], sv["f3"], sv["s3"], sv["w3"], full_w[l, "ffn2_w_down"], full_w[l, "ffn2_w_gu"], sv["x2"],
            row_vec(norm_ffn2_g[l]), sv["sc"][2], 0.5, seq, "ffn_bwd", comm)
        hosted(keys, got)
        gw_down2, _ = _weight_grad(sv["a3"], df3, seq, "grad_w_down")
        scatter_later(l, "ffn2_w_down", gw_down2)
        keys, comm = host()
        gw_gu2, got = _weight_grad(dgu3, sv["h3"], seq, "grad_w_gu", comm)
        hosted(keys, got)
        scatter_later(l, "ffn2_w_gu", gw_gu2)
        keys, comm = host()
        do2, dg2, dproj, dx1, dsh2, dsc2, dn2, d_og, d_cw, d_lng, d_lnb, d_bias, d_wm, got = _mixer_backward(
            sv["proj"], dx2, sv["gates"][1], sv["o2"], full_w[l, "mix_w_out"], sv["x1"], row_vec(norm_mix_g[l]), sv["sc"][1],
            full_w[l, "mix_w_in"], wm_rows=sv["wm_rows"], causal=causal, seq=seq, name="mixer_backward", comm=comm, **mc)
        hosted(keys, got)
        small_grads[l] = dict(
            out_norm_g=d_og, sgu_ln_g=d_lng.reshape(N_HEADS, HEAD_DIM).sum(0), sgu_ln_b=d_lnb.reshape(N_HEADS, HEAD_DIM).sum(0),
            sgu_w_s=d_wm, sgu_b=d_bias.reshape(CHUNK, N_HEADS, HEAD_DIM).sum(-1).T, conv_w=d_cw[0:3])
        gw_out, _ = _weight_grad(sv["ymix"], do2, seq, "grad_w_out")
        scatter_later(l, "mix_w_out", gw_out)
        keys, comm = host()
        gw_in, got = _weight_grad(dproj, sv["h2"], seq, "grad_w_in", comm)
        hosted(keys, got)
        scatter_later(l, "mix_w_in", gw_in)
        keys, comm = host()
        pending.append((("mix", l), _GatherRows([_pack_small(mix_parts(l))])))
        if l > 0:
            df1, dg1, dgu1, dx0, dsh1, dsc1, dn1, got = _ffn_backward(
                dx1, sv["gates"][0], sv["f1"], sv["s1"], sv["w1"], full_w[l, "ffn1_w_down"], full_w[l, "ffn1_w_gu"], sv["x0"],
                row_vec(norm_ffn1_g[l]), sv["sc"][0], 0.5, seq, "ffn_bwd", comm)
        else:
            df1, dg1, dgu1, got = _residual_backward(dx1, sv["gates"][0], sv["f1"], full_w[l, "ffn1_w_down"], 0.5, sv["s1"], sv["w1"], seq, "ffn_down_bwd", comm)
        hosted(keys, got)
        gw_down1, _ = _weight_grad(sv["a1"], df1, seq, "grad_w_down")
        scatter_later(l, "ffn1_w_down", gw_down1)
        keys, comm = host()
        gw_gu1, got = _weight_grad(dgu1, sv["h1"], seq, "grad_w_gu", comm)
        hosted(keys, got)
        scatter_later(l, "ffn1_w_gu", gw_gu1)
        if l == 0:
            keys, comm = host()
            dx0, dsh1, dsc1, dn1, got = _matmul_normmod_backward(dgu1, full_w[l, "ffn1_w_gu"], sv["x0"], dx1, row_vec(norm_ffn1_g[l]), sv["sc"][0], seq, "ffn_up_bwd", comm)
            hosted(keys, got)
        dx = dx0
        small_grads[l].update(norm_ffn1_g=dn1, norm_mix_g=dn2, norm_ffn2_g=dn3)
        d_mod[l] = jnp.concatenate([dsh1, dsc1, dg1, dsh2, dsc2, dg2, dsh3, dsc3, dg3], axis=1)
    grad_x = dx.reshape(batch, seq, D_MODEL)

    grad_big, delta, new_m, new_v = {}, {}, {}, {}
    for nm in big:
        results = _adamw_rows([recv[l, nm] for l in range(DEPTH)], as_rows(nm, weights[nm]), as_rows(nm, mom1[nm]),
                              as_rows(nm, mom2[nm]), "adamw_" + nm)
        grad_big[nm], delta[nm], new_m[nm], new_v[nm] = [as_rows(nm, r) for r in results]

    last_parts = late_parts(0) + [d_final_g, loss_cols]
    last_shapes = [p.shape for p in last_parts]
    packed_all, packed_sum = _all_gather_small(_pack_small(last_parts), "reduce_small")
    late_sum = {0: _unpack_small(packed_sum, last_shapes)}
    d_mod_dev = {0: _unpack_small(packed_all, last_shapes, lead=(N_DEV,))[len(late_names)]}
    mix_sum = {}
    for l in range(DEPTH):
        gathered = recv["mix", l].reshape(N_DEV, -1, LANES)
        mix_sum[l] = _unpack_small(_sum_gathered(gathered, "sum_mix"), [p.shape for p in mix_parts(l)])
        if l > 0:
            shapes_l = [p.shape for p in late_parts(l)]
            gathered = recv["late", l].reshape(N_DEV, -1, LANES)
            late_sum[l] = _unpack_small(_sum_gathered(gathered, "sum_late"), shapes_l)
            d_mod_dev[l] = _unpack_small(gathered, shapes_l, lead=(N_DEV,))[len(late_names)]
    grad_small = {}
    for group, names in ((mix_sum, mix_names), (late_sum, late_names)):
        for k, nm in enumerate(names):
            grad_small[nm] = jnp.stack([group[l][k] for l in range(DEPTH)]).reshape(
                (DEPTH, 3, D_A) if nm == "conv_w" else weights[nm].shape)
    grad_small["conv_w"] = lax.dynamic_slice_in_dim(grad_small["conv_w"], me * conv_w.shape[2], conv_w.shape[2], axis=2)
    grad_small["final_norm_g"] = late_sum[0][len(late_names) + 1].reshape(final_norm_g.shape)
    loss = jnp.sum(late_sum[0][len(late_names) + 2])
    d_ada_all = jnp.stack([d_mod_dev[l] for l in range(DEPTH)]).reshape(DEPTH, N_DEV * batch, N_MOD * D_MODEL)
    d_ada_cols = lax.dynamic_slice_in_dim(d_ada_all, me * ada_cols, ada_cols, axis=2)
    g_ada_w, g_ada_b = _ada_backward(c_all, d_ada_cols, d_ada_all)

    grads = dict(grad_big)
    grads.update(grad_small)
    grads["ada_w"] = g_ada_w
    grads["ada_b"] = g_ada_b.reshape(ada_b.shape)

    names = ("ada_w", "ada_b", "norm_ffn1_g", "ffn1_w_gu", "ffn1_w_down", "norm_mix_g", "mix_w_in", "sgu_ln_g",
             "sgu_ln_b", "sgu_w_s", "sgu_b", "conv_w", "out_norm_g", "mix_w_out", "norm_ffn2_g", "ffn2_w_gu",
             "ffn2_w_down", "final_norm_g")
    delta["ada_w"], new_m["ada_w"], new_v["ada_w"] = _adamw_nd(ada_w, grads["ada_w"], m_ada_w, v_ada_w, "adamw_ada_w")
    rest = [nm for nm in names if nm not in big and nm != "ada_w"]
    pick = lambda src: [src[nm] for nm in rest]
    for nm, d_k, m_k, v_k in zip(rest, *_adamw_many(pick(weights), pick(grads), pick(mom1), pick(mom2), "adamw_small")):
        delta[nm], new_m[nm], new_v[nm] = d_k, m_k, v_k

    return (loss, grad_x, *[grads[nm] for nm in names], *[delta[nm] for nm in names],
            *[new_m[nm] for nm in names], *[new_v[nm] for nm in names])
```

```python
import math

import jax
import jax.numpy as jnp
from jax import lax
from jax.experimental import pallas as pl
from jax.experimental.pallas import tpu as pltpu

F32 = jnp.float32
BF16 = jnp.bfloat16

D_MODEL = 1024
D_A = 512
D_PROJ = 2560
N_HEADS = 8
HEAD_DIM = 64
CHUNK = 128
N_MOD = 9
DEPTH = 2
EPS = 1e-6
N_DEV = 8
LANES = 128
MXU_N = 256
HALO = 16
VMEM_LIMIT = 56 * 1024 * 1024
FORWARD_STEPS = 3

ADAM_LR = 0.001
ADAM_B1 = 0.9
ADAM_B2 = 0.999
ADAM_EPS = 1e-08
ADAM_WD = 0.01
ADAM_STEP = 10

MESH = pl.DeviceIdType.MESH


def _dot(a, b):
    return jnp.dot(a, b, preferred_element_type=F32)


def _dot_nt(a, b):
    return lax.dot_general(a, b, (((1,), (1,)), ((), ())), preferred_element_type=F32)


def _dot_tn(a, b):
    return lax.dot_general(a, b, (((0,), (0,)), ((), ())), preferred_element_type=F32)


def _sigmoid(x):
    return 0.5 * jnp.tanh(0.5 * x) + 0.5


def _gelu(x):
    return 0.5 * x * (1.0 + lax.erf(x * (1.0 / math.sqrt(2.0))))


def _gelu_grad(x):
    cdf = 0.5 * (1.0 + lax.erf(x * (1.0 / math.sqrt(2.0))))
    return cdf + x * jnp.exp(-0.5 * x * x) * (1.0 / math.sqrt(2.0 * math.pi))


def _params(n_axes=1, parallel=False):
    sem = ("parallel" if parallel else "arbitrary",) * n_axes
    return pltpu.CompilerParams(dimension_semantics=sem, vmem_limit_bytes=VMEM_LIMIT)


def _resident(shape):
    nd = len(shape)
    return pl.BlockSpec(shape, lambda *_: (0,) * nd, pipeline_mode=pl.Buffered(1))


def _tile_rows(seq):
    return min(512, seq)


def _my_position():
    x, y, c = lax.axis_index("x"), lax.axis_index("y"), lax.axis_index("c")
    return x, y, c, 4 * x + 2 * y + c


def _peer(x, y, c, p):
    return (x ^ ((p >> 2) & 1), y ^ ((p >> 1) & 1), c ^ (p & 1))


class _GatherRows:
    def __init__(self, shards):
        self.operands = list(shards)
        n = len(shards)
        self.out_shape = [jax.ShapeDtypeStruct((N_DEV * s.shape[0], s.shape[1]), s.dtype) for s in shards]
        self.scratch = [pltpu.SemaphoreType.DMA((n, N_DEV - 1)), pltpu.SemaphoreType.DMA((n, N_DEV - 1)),
                        pltpu.SemaphoreType.DMA((n,))]

    def _plan(self, src, dst, send, recv, loc):
        x, y, c, _ = _my_position()
        me, sib = (x, y, c), (x, y, 1 - c)
        chips = [(1 - x, y), (x, 1 - y), (1 - x, 1 - y)]
        plans = []
        for k, shard in enumerate(self.operands):
            rows = shard.shape[0]

            def blk(pos, k=k, rows=rows):
                return dst[k].at[pl.ds((4 * pos[0] + 2 * pos[1] + pos[2]) * rows, rows), :]

            def rc(s, block, to, source=None, k=k, blk=blk):
                return pltpu.make_async_remote_copy(
                    src_ref=blk(block) if source is None else source, dst_ref=blk(block),
                    send_sem=send.at[k, s], recv_sem=recv.at[k, s], device_id=to, device_id_type=MESH)

            plans.append(dict(
                local=pltpu.make_async_copy(src[k], blk(me), loc.at[k]),
                first=[rc(0, me, sib, src[k])] + [rc(1 + j, me, (*chip, c), src[k]) for j, chip in enumerate(chips)],
                landed=[rc(1 + j, (*chip, c), me) for j, chip in enumerate(chips)],
                passed=[rc(4 + j, (*chip, c), sib) for j, chip in enumerate(chips)],
                from_sib=[rc(0, sib, me)] + [rc(4 + j, (*chip, 1 - c), me) for j, chip in enumerate(chips)]))
        return plans

    def start(self, src, dst, send, recv, loc):
        for plan in self._plan(src, dst, send, recv, loc):
            plan["local"].start()
            for cp in plan["first"]:
                cp.start()

    def forward(self, src, dst, send, recv, loc):
        for plan in self._plan(src, dst, send, recv, loc):
            for landed, passed in zip(plan["landed"], plan["passed"]):
                landed.wait_recv()
                passed.start()

    def finish(self, src, dst, send, recv, loc):
        for plan in self._plan(src, dst, send, recv, loc):
            for cp in plan["from_sib"]:
                cp.wait_recv()
            for cp in plan["first"] + plan["passed"]:
                cp.wait_send()
            plan["local"].wait()


class _ScatterRows:
    def __init__(self, grads):
        self.operands = list(grads)
        n = len(grads)
        self.out_shape = [jax.ShapeDtypeStruct((N_DEV, g.shape[0] // N_DEV, g.shape[1]), g.dtype) for g in grads]
        self.scratch = [pltpu.SemaphoreType.DMA((n, N_DEV - 1)), pltpu.SemaphoreType.DMA((n, N_DEV - 1)),
                        pltpu.SemaphoreType.DMA((n,))]

    def _plan(self, src, dst, send, recv, loc):
        x, y, c, me = _my_position()
        copies = []
        for k, grad in enumerate(self.operands):
            rows = grad.shape[0] // N_DEV
            copies.append(pltpu.make_async_copy(src[k].at[pl.ds(me * rows, rows), :], dst[k].at[me], loc.at[k]))
            for p in range(1, N_DEV):
                px, py, pc = _peer(x, y, c, p)
                copies.append(pltpu.make_async_remote_copy(
                    src_ref=src[k].at[pl.ds((4 * px + 2 * py + pc) * rows, rows), :], dst_ref=dst[k].at[me],
                    send_sem=send.at[k, p - 1], recv_sem=recv.at[k, p - 1], device_id=(px, py, pc), device_id_type=MESH))
        return copies

    def start(self, src, dst, send, recv, loc):
        for cp in self._plan(src, dst, send, recv, loc):
            cp.start()

    def forward(self, src, dst, send, recv, loc):
        pass

    def finish(self, src, dst, send, recv, loc):
        for cp in self._plan(src, dst, send, recv, loc):
            cp.wait()


class _Exchanges:
    def __init__(self, parts):
        self.parts = list(parts)
        self.operands = [op for part in self.parts for op in part.operands]
        self.out_shape = [shp for part in self.parts for shp in part.out_shape]
        self.scratch = [scr for part in self.parts for scr in part.scratch]

    def _each(self, src, dst, sems):
        at, sem_at = 0, 0
        for part in self.parts:
            n, n_sem = len(part.operands), len(part.scratch)
            yield part, src[at:at + n], dst[at:at + n], sems[sem_at:sem_at + n_sem]
            at, sem_at = at + n, sem_at + n_sem

    def start(self, src, dst, *sems):
        for part, part_src, part_dst, part_sems in self._each(src, dst, sems):
            part.start(part_src, part_dst, *part_sems)

    def forward(self, src, dst, *sems):
        for part, part_src, part_dst, part_sems in self._each(src, dst, sems):
            part.forward(part_src, part_dst, *part_sems)

    def finish(self, src, dst, *sems):
        for part, part_src, part_dst, part_sems in self._each(src, dst, sems):
            part.finish(part_src, part_dst, *part_sems)


_ANY = pl.BlockSpec(memory_space=pl.ANY)


def _call(body, *, name, grid, in_specs, out_specs, out_shape, operands, scratch_shapes=(), parallel=False, comm=None):
    n_axes = len(grid)
    if comm is None:
        outs = pl.pallas_call(body, name=name, grid=grid, out_shape=list(out_shape), in_specs=list(in_specs),
                              out_specs=list(out_specs), scratch_shapes=list(scratch_shapes),
                              compiler_params=_params(n_axes, parallel))(*operands)
        return list(outs), None
    n_in, n_out, n_scr, n_c = len(in_specs), len(out_specs), len(scratch_shapes), len(comm.operands)
    total = math.prod(grid)

    def hosted(*refs):
        ins, c_src = refs[:n_in], refs[n_in:n_in + n_c]
        outs, c_dst = refs[n_in + n_c:n_in + n_c + n_out], refs[n_in + n_c + n_out:n_in + 2 * n_c + n_out]
        scr, sems = refs[n_in + 2 * n_c + n_out:n_in + 2 * n_c + n_out + n_scr], refs[n_in + 2 * n_c + n_out + n_scr:]
        step = pl.program_id(0)
        for axis in range(1, n_axes):
            step = step * grid[axis] + pl.program_id(axis)

        @pl.when(step == 0)
        def _():
            comm.start(c_src, c_dst, *sems)

        @pl.when(step == max(total - FORWARD_STEPS, 0))
        def _():
            comm.forward(c_src, c_dst, *sems)

        body(*ins, *outs, *scr)

        @pl.when(step == total - 1)
        def _():
            comm.finish(c_src, c_dst, *sems)

    res = pl.pallas_call(hosted, name=name, grid=grid, out_shape=list(out_shape) + comm.out_shape,
                         in_specs=list(in_specs) + [_ANY] * n_c, out_specs=list(out_specs) + [_ANY] * n_c,
                         scratch_shapes=list(scratch_shapes) + comm.scratch,
                         compiler_params=_params(n_axes, False))(*operands, *comm.operands)
    return list(res[:n_out]), list(res[n_out:])


def _all_gather_small(v, name):
    rows = v.shape[0]

    def body(v_ref, all_ref, sum_ref, send_sems, recv_sems):
        x, y, c, me = _my_position()
        all_ref[me] = v_ref[...]
        copies = []
        for p in range(1, N_DEV):
            cp = pltpu.make_async_remote_copy(
                src_ref=v_ref, dst_ref=all_ref.at[me], send_sem=send_sems.at[p - 1], recv_sem=recv_sems.at[p - 1],
                device_id=_peer(x, y, c, p), device_id_type=MESH)
            cp.start()
            copies.append(cp)
        for cp in copies:
            cp.wait()
        acc = all_ref[0]
        for d in range(1, N_DEV):
            acc = acc + all_ref[d]
        sum_ref[...] = acc

    return pl.pallas_call(
        body, name=name,
        out_shape=[jax.ShapeDtypeStruct((N_DEV, rows, LANES), F32), jax.ShapeDtypeStruct((rows, LANES), F32)],
        in_specs=[pl.BlockSpec(memory_space=pltpu.VMEM)],
        out_specs=[pl.BlockSpec(memory_space=pltpu.VMEM)] * 2,
        scratch_shapes=[pltpu.SemaphoreType.DMA((N_DEV - 1,)), pltpu.SemaphoreType.DMA((N_DEV - 1,))],
        compiler_params=pltpu.CompilerParams(vmem_limit_bytes=VMEM_LIMIT),
    )(v)


def _sum_gathered(gathered, name):
    rows = gathered.shape[1]

    def body(g_ref, o_ref):
        acc = g_ref[0]
        for d in range(1, N_DEV):
            acc = acc + g_ref[d]
        o_ref[...] = acc

    return pl.pallas_call(
        body, name=name, out_shape=jax.ShapeDtypeStruct((rows, LANES), F32),
        in_specs=[pl.BlockSpec(memory_space=pltpu.VMEM)], out_specs=pl.BlockSpec(memory_space=pltpu.VMEM),
        compiler_params=pltpu.CompilerParams(vmem_limit_bytes=VMEM_LIMIT),
    )(gathered)


def _pack_small(parts):
    flat = jnp.concatenate([p.reshape(-1).astype(F32) for p in parts])
    total = flat.shape[0]
    padded = -(-total // (8 * LANES)) * (8 * LANES)
    flat = jnp.pad(flat, (0, padded - total))
    return flat.reshape(padded // LANES, LANES)


def _unpack_small(packed, shapes, lead=()):
    flat = packed.reshape(lead + (-1,))
    out, off = [], 0
    for shp in shapes:
        size = math.prod(shp)
        out.append(flat[..., off:off + size].reshape(lead + tuple(shp)))
        off += size
    return out


def _prologue(c_rows, convw_rows, ada_w, ada_b_cols, gather):
    depth, _, cols = ada_w.shape
    n_c = len(gather.operands)
    sub = 8

    def body(c_ref, cw_ref, b_ref, w_hbm, *rest):
        g_src, (c_all_ref, cw_all_ref, ada_ref), g_dst = rest[:n_c], rest[n_c:n_c + 3], rest[n_c + 3:2 * n_c + 3]
        ada_local, w_ref, w_sem, send_sems, recv_sems = rest[2 * n_c + 3:2 * n_c + 8]
        g_sems = rest[2 * n_c + 8:]
        x, y, c, me = _my_position()
        gather.start(g_src, g_dst, *g_sems)
        load_w = pltpu.make_async_copy(w_hbm, w_ref, w_sem)
        load_w.start()

        def to_all(k, src_ref, dst_ref):
            copies = []
            for p in range(1, N_DEV):
                copies.append(pltpu.make_async_remote_copy(
                    src_ref=src_ref, dst_ref=dst_ref.at[me], send_sem=send_sems.at[k, p - 1], recv_sem=recv_sems.at[k, p - 1],
                    device_id=_peer(x, y, c, p), device_id_type=MESH))
            return copies

        first = to_all(0, c_ref, c_all_ref) + to_all(1, cw_ref, cw_all_ref)
        c_all_ref[me] = c_ref[...]
        cw_all_ref[me] = cw_ref[...]
        for cp in first:
            cp.start()
        for cp in first:
            cp.wait()
        cv = c_all_ref[...].reshape(N_DEV * sub, D_MODEL)
        act = (cv * _sigmoid(cv)).astype(BF16)
        load_w.wait()
        for l in range(depth):
            ada_local[l] = _dot(act, w_ref[l].astype(BF16)) + b_ref[l]
        ada_ref[me] = ada_local[:, pl.ds(pl.multiple_of(me * sub, sub), sub), :]
        rows_out = []
        for p in range(1, N_DEV):
            px, py, pc = _peer(x, y, c, p)
            rows = pl.ds(pl.multiple_of((4 * px + 2 * py + pc) * sub, sub), sub)
            rows_out.append(pltpu.make_async_remote_copy(
                src_ref=ada_local.at[:, rows, :], dst_ref=ada_ref.at[me], send_sem=send_sems.at[2, p - 1],
                recv_sem=recv_sems.at[2, p - 1], device_id=(px, py, pc), device_id_type=MESH))
        for cp in rows_out:
            cp.start()
        for cp in rows_out:
            cp.wait()
        gather.forward(g_src, g_dst, *g_sems)
        gather.finish(g_src, g_dst, *g_sems)

    vmem = pl.BlockSpec(memory_space=pltpu.VMEM)
    outs = pl.pallas_call(
        body, name="prologue",
        out_shape=[jax.ShapeDtypeStruct((N_DEV, sub, D_MODEL), F32), jax.ShapeDtypeStruct((N_DEV, sub, LANES), F32),
                   jax.ShapeDtypeStruct((N_DEV, depth, sub, cols), F32)] + gather.out_shape,
        in_specs=[vmem] * 3 + [_ANY] * (1 + n_c), out_specs=[vmem] * 3 + [_ANY] * n_c,
        scratch_shapes=[pltpu.VMEM((depth, N_DEV * sub, cols), F32), pltpu.VMEM(ada_w.shape, F32), pltpu.SemaphoreType.DMA,
                        pltpu.SemaphoreType.DMA((3, N_DEV - 1)), pltpu.SemaphoreType.DMA((3, N_DEV - 1))] + gather.scratch,
        compiler_params=pltpu.CompilerParams(vmem_limit_bytes=VMEM_LIMIT),
    )(c_rows, convw_rows, ada_b_cols, ada_w, *gather.operands)
    return outs[0], outs[1], outs[2], list(outs[3:])


def _ada_backward(c_all, d_ada_cols, d_ada_all):
    nb = c_all.shape[0]
    cols = d_ada_cols.shape[2]
    full = d_ada_all.shape[2]

    def body(c_ref, dc_ref, da_ref, gw_ref, gb_ref):
        cv = c_ref[...]
        act = (cv * _sigmoid(cv)).astype(BF16)
        gw_ref[0] = _dot_tn(act, dc_ref[0].astype(BF16))
        gb_ref[0] = jnp.sum(da_ref[0], axis=0, keepdims=True)

    return pl.pallas_call(
        body, name="ada_backward", grid=(DEPTH,),
        out_shape=[jax.ShapeDtypeStruct((DEPTH, D_MODEL, cols), F32), jax.ShapeDtypeStruct((DEPTH, 1, full), F32)],
        in_specs=[pl.BlockSpec((nb, D_MODEL), lambda l: (0, 0)),
                  pl.BlockSpec((1, nb, cols), lambda l: (l, 0, 0)),
                  pl.BlockSpec((1, nb, full), lambda l: (l, 0, 0))],
        out_specs=[pl.BlockSpec((1, D_MODEL, cols), lambda l: (l, 0, 0)),
                   pl.BlockSpec((1, 1, full), lambda l: (l, 0, 0))],
        compiler_params=_params(),
    )(c_all, d_ada_cols, d_ada_all)


def _rms(xv):
    return lax.rsqrt(jnp.mean(xv * xv, axis=-1, keepdims=True) + EPS)


def _normmod_matmul(x, gnorm, scale1p, shift, w_t, seq, name, comm=None):
    tokens, width = x.shape[0], w_t.shape[0] // 2
    tm = _tile_rows(seq)
    per_seq = seq // tm
    n_chunks = width // MXU_N

    def body(x_ref, g_ref, sc_ref, sh_ref, w_ref, h_ref, act_ref, silu_ref, dact_ref):
        xv = x_ref[...]
        h = (xv * _rms(xv) * g_ref[...]) * sc_ref[0] + sh_ref[0]
        h_ref[...] = h.astype(BF16)
        for ck in range(n_chunks):
            cs = slice(ck * MXU_N, (ck + 1) * MXU_N)
            g = _dot_nt(h_ref[...], w_ref[cs, :])
            u = _dot_nt(h_ref[...], w_ref[width + ck * MXU_N:width + (ck + 1) * MXU_N, :])
            sig = _sigmoid(g)
            silu = g * sig
            act_ref[:, cs] = (silu * u).astype(BF16)
            silu_ref[:, cs] = silu.astype(BF16)
            dact_ref[:, cs] = (u * (sig + silu * (1.0 - sig))).astype(BF16)

    per_batch = pl.BlockSpec((1, 1, D_MODEL), lambda i: (i // per_seq, 0, 0))
    outs, got = _call(
        body, name=name, grid=(tokens // tm,),
        out_shape=[jax.ShapeDtypeStruct((tokens, D_MODEL), BF16)] + [jax.ShapeDtypeStruct((tokens, width), BF16)] * 3,
        in_specs=[pl.BlockSpec((tm, D_MODEL), lambda i: (i, 0)), _resident((1, D_MODEL)), per_batch, per_batch,
                  _resident(w_t.shape)],
        out_specs=[pl.BlockSpec((tm, D_MODEL), lambda i: (i, 0))] + [pl.BlockSpec((tm, width), lambda i: (i, 0))] * 3,
        operands=(x, gnorm, scale1p, shift, w_t), parallel=True, comm=comm)
    return (*outs, got)


def _matmul_residual(src, w, x, gate, scale, seq, name, comm=None):
    tokens, k_dim = x.shape[0], w.shape[0]
    tm = _tile_rows(seq)
    per_seq = seq // tm

    def body(s_ref, w_ref, x_ref, gate_ref, xo_ref, f_ref):
        f = _dot(s_ref[...], w_ref[...])
        f_ref[...] = f.astype(BF16)
        xo_ref[...] = x_ref[...] + (scale * gate_ref[0]) * f

    (x_out, f), got = _call(
        body, name=name, grid=(tokens // tm,),
        out_shape=[jax.ShapeDtypeStruct((tokens, D_MODEL), F32), jax.ShapeDtypeStruct((tokens, D_MODEL), BF16)],
        in_specs=[pl.BlockSpec((tm, k_dim), lambda i: (i, 0)), _resident(w.shape),
                  pl.BlockSpec((tm, D_MODEL), lambda i: (i, 0)),
                  pl.BlockSpec((1, 1, D_MODEL), lambda i: (i // per_seq, 0, 0))],
        out_specs=[pl.BlockSpec((tm, D_MODEL), lambda i: (i, 0))] * 2,
        operands=(src, w, x, gate), parallel=True, comm=comm)
    return x_out, f, got


def _loss_tile(xv, target, gn):
    r = _rms(xv)
    xn = xv * r
    err = xn * gn - target
    loss = (0.5 / D_MODEL) * jnp.sum(err * err, axis=0, keepdims=True)
    dyv = err * (1.0 / D_MODEL)
    dg = jnp.sum(dyv * xn, axis=0, keepdims=True)
    dxn = dyv * gn
    dx = r * (dxn - xn * jnp.mean(dxn * xn, axis=-1, keepdims=True))
    return loss, dx, dg


def _ffn_forward(x, gnorm, scale1p, shift, w_gu_t, w_down, gate, scale, seq, name, loss_head=None, comm=None):
    tokens, width = x.shape[0], w_down.shape[0]
    tm = _tile_rows(seq)
    per_seq = seq // tm
    n_chunks = width // MXU_N

    def body(x_ref, g_ref, sc_ref, sh_ref, wgu_ref, wd_ref, gate_ref, *rest):
        if loss_head:
            t_ref, gf_ref, h_ref, act_ref, silu_ref, dact_ref, xo_ref, f_ref, dgf_ref, loss_ref = rest
        else:
            h_ref, act_ref, silu_ref, dact_ref, xo_ref, f_ref = rest
        xv = x_ref[...]
        h = (xv * _rms(xv) * g_ref[...]) * sc_ref[0] + sh_ref[0]
        h_ref[...] = h.astype(BF16)
        for ck in range(n_chunks):
            cs = slice(ck * MXU_N, (ck + 1) * MXU_N)
            g = _dot_nt(h_ref[...], wgu_ref[cs, :])
            u = _dot_nt(h_ref[...], wgu_ref[width + ck * MXU_N:width + (ck + 1) * MXU_N, :])
            sig = _sigmoid(g)
            silu = g * sig
            act_ref[:, cs] = (silu * u).astype(BF16)
            silu_ref[:, cs] = silu.astype(BF16)
            dact_ref[:, cs] = (u * (sig + silu * (1.0 - sig))).astype(BF16)
        f = _dot(act_ref[...], wd_ref[...])
        f_ref[...] = f.astype(BF16)
        x_out = xv + (scale * gate_ref[0]) * f
        if loss_head:
            i = pl.program_id(0)
            loss, dx, dg = _loss_tile(x_out, t_ref[...], gf_ref[...])
            xo_ref[...] = dx

            @pl.when(i == 0)
            def _():
                dgf_ref[...] = dg
                loss_ref[...] = loss

            @pl.when(i != 0)
            def _():
                dgf_ref[...] = dgf_ref[...] + dg
                loss_ref[...] = loss_ref[...] + loss
        else:
            xo_ref[...] = x_out

    row = lambda i: (i, 0)
    per_batch = pl.BlockSpec((1, 1, D_MODEL), lambda i: (i // per_seq, 0, 0))
    tile = lambda cols: pl.BlockSpec((tm, cols), row)
    wide = jax.ShapeDtypeStruct((tokens, width), BF16)
    fixed = pl.BlockSpec((1, D_MODEL), lambda i: (0, 0))
    vec = jax.ShapeDtypeStruct((1, D_MODEL), F32)
    outs, got = _call(
        body, name=name, grid=(tokens // tm,),
        out_shape=[jax.ShapeDtypeStruct((tokens, D_MODEL), BF16), wide, wide, wide,
                   jax.ShapeDtypeStruct((tokens, D_MODEL), F32), jax.ShapeDtypeStruct((tokens, D_MODEL), BF16)]
        + ([vec, vec] if loss_head else []),
        in_specs=[tile(D_MODEL), _resident((1, D_MODEL)), per_batch, per_batch, _resident(w_gu_t.shape),
                  _resident(w_down.shape), per_batch] + ([tile(D_MODEL), _resident((1, D_MODEL))] if loss_head else []),
        out_specs=[tile(D_MODEL), tile(width), tile(width), tile(width), tile(D_MODEL), tile(D_MODEL)]
        + ([fixed, fixed] if loss_head else []),
        operands=(x, gnorm, scale1p, shift, w_gu_t, w_down, gate) + (tuple(loss_head) if loss_head else ()),
        parallel=not loss_head, comm=comm)
    return (*outs, got)


def _residual_backward(dy, gate, f, w, scale, silu, dact, seq, name, comm=None):
    tokens, k_dim = dy.shape[0], w.shape[0]
    batch = tokens // seq
    tm = _tile_rows(seq)
    per_seq = seq // tm
    n_chunks = k_dim // MXU_N

    def body(dy_ref, gate_ref, f_ref, silu_ref, dact_ref, w_ref, df_ref, dgate_ref, dgu_ref):
        i = pl.program_id(0)
        dy_v = dy_ref[...]
        df_ref[...] = ((scale * gate_ref[0]) * dy_v).astype(BF16)
        part = scale * jnp.sum(dy_v * f_ref[...].astype(F32), axis=0, keepdims=True)
        for ck in range(n_chunks):
            cs = slice(ck * MXU_N, (ck + 1) * MXU_N)
            cu = slice(k_dim + ck * MXU_N, k_dim + (ck + 1) * MXU_N)
            da = _dot_nt(df_ref[...], w_ref[cs, :])
            dgu_ref[:, cs] = (da * dact_ref[:, cs].astype(F32)).astype(BF16)
            dgu_ref[:, cu] = (da * silu_ref[:, cs].astype(F32)).astype(BF16)

        @pl.when(i % per_seq == 0)
        def _():
            dgate_ref[0] = part

        @pl.when(i % per_seq != 0)
        def _():
            dgate_ref[0] = dgate_ref[0] + part

    row = lambda i: (i, 0)
    per_batch = pl.BlockSpec((1, 1, D_MODEL), lambda i: (i // per_seq, 0, 0))
    tile = lambda cols: pl.BlockSpec((tm, cols), row)
    outs, got = _call(
        body, name=name, grid=(tokens // tm,),
        out_shape=[jax.ShapeDtypeStruct((tokens, D_MODEL), BF16), jax.ShapeDtypeStruct((batch, 1, D_MODEL), F32),
                   jax.ShapeDtypeStruct((tokens, 2 * k_dim), BF16)],
        in_specs=[tile(D_MODEL), per_batch, tile(D_MODEL), tile(k_dim), tile(k_dim), _resident(w.shape)],
        out_specs=[tile(D_MODEL), per_batch, tile(2 * k_dim)],
        operands=(dy, gate, f, silu, dact, w), comm=comm)
    return (*outs, got)


def _matmul_normmod_backward(dsrc, w_t, x, dy, gnorm, scale1p, seq, name, comm=None):
    tokens, k_dim = dsrc.shape
    batch = tokens // seq
    tm = _tile_rows(seq)
    per_seq = seq // tm

    def body(ds_ref, w_ref, x_ref, dy_ref, g_ref, sc_ref, dx_ref, dsh_ref, dsc_ref, dg_ref):
        i = pl.program_id(0)
        dh = _dot(ds_ref[...], w_ref[...])
        xv = x_ref[...]
        r = _rms(xv)
        xn = xv * r
        gn = g_ref[...]
        dsh = jnp.sum(dh, axis=0, keepdims=True)
        dsc = jnp.sum(dh * (xn * gn), axis=0, keepdims=True)
        dhn = dh * sc_ref[0]
        dg = jnp.sum(dhn * xn, axis=0, keepdims=True)
        dxn = dhn * gn
        dx_ref[...] = dy_ref[...] + r * (dxn - xn * jnp.mean(dxn * xn, axis=-1, keepdims=True))

        @pl.when(i % per_seq == 0)
        def _():
            dsh_ref[0] = dsh
            dsc_ref[0] = dsc

        @pl.when(i % per_seq != 0)
        def _():
            dsh_ref[0] = dsh_ref[0] + dsh
            dsc_ref[0] = dsc_ref[0] + dsc

        @pl.when(i == 0)
        def _():
            dg_ref[...] = dg

        @pl.when(i != 0)
        def _():
            dg_ref[...] = dg_ref[...] + dg

    row = lambda i: (i, 0)
    per_batch = pl.BlockSpec((1, 1, D_MODEL), lambda i: (i // per_seq, 0, 0))
    outs, got = _call(
        body, name=name, grid=(tokens // tm,),
        out_shape=[jax.ShapeDtypeStruct((tokens, D_MODEL), F32), jax.ShapeDtypeStruct((batch, 1, D_MODEL), F32),
                   jax.ShapeDtypeStruct((batch, 1, D_MODEL), F32), jax.ShapeDtypeStruct((1, D_MODEL), F32)],
        in_specs=[pl.BlockSpec((tm, k_dim), row), _resident(w_t.shape), pl.BlockSpec((tm, D_MODEL), row),
                  pl.BlockSpec((tm, D_MODEL), row), _resident((1, D_MODEL)), per_batch],
        out_specs=[pl.BlockSpec((tm, D_MODEL), row), per_batch, per_batch, pl.BlockSpec((1, D_MODEL), lambda i: (0, 0))],
        operands=(dsrc, w_t, x, dy, gnorm, scale1p), comm=comm)
    return (*outs, got)


def _ffn_backward(dy, gate, f, silu, dact, w_down, w_gu_t, x, gnorm, scale1p, scale, seq, name, comm=None):
    tokens, k_dim = dy.shape[0], w_down.shape[0]
    batch = tokens // seq
    tm = min(256, seq)
    per_seq = seq // tm
    n_chunks = k_dim // MXU_N

    def body(dy_ref, gate_ref, f_ref, silu_ref, dact_ref, wd_ref, wgu_ref, x_ref, g_ref, sc_ref,
             df_ref, dgate_ref, dgu_ref, dx_ref, dsh_ref, dsc_ref, dg_ref):
        i = pl.program_id(0)
        dy_v = dy_ref[...]
        df_ref[...] = ((scale * gate_ref[0]) * dy_v).astype(BF16)
        dgate = scale * jnp.sum(dy_v * f_ref[...].astype(F32), axis=0, keepdims=True)
        for ck in range(n_chunks):
            cs = slice(ck * MXU_N, (ck + 1) * MXU_N)
            cu = slice(k_dim + ck * MXU_N, k_dim + (ck + 1) * MXU_N)
            da = _dot_nt(df_ref[...], wd_ref[cs, :])
            dgu_ref[:, cs] = (da * dact_ref[:, cs].astype(F32)).astype(BF16)
            dgu_ref[:, cu] = (da * silu_ref[:, cs].astype(F32)).astype(BF16)
        dh = _dot(dgu_ref[...], wgu_ref[...])
        xv = x_ref[...]
        r = _rms(xv)
        xn = xv * r
        gn = g_ref[...]
        dsh = jnp.sum(dh, axis=0, keepdims=True)
        dsc = jnp.sum(dh * (xn * gn), axis=0, keepdims=True)
        dhn = dh * sc_ref[0]
        dg = jnp.sum(dhn * xn, axis=0, keepdims=True)
        dxn = dhn * gn
        dx_ref[...] = dy_v + r * (dxn - xn * jnp.mean(dxn * xn, axis=-1, keepdims=True))

        @pl.when(i % per_seq == 0)
        def _():
            dgate_ref[0] = dgate
            dsh_ref[0] = dsh
            dsc_ref[0] = dsc

        @pl.when(i % per_seq != 0)
        def _():
            dgate_ref[0] = dgate_ref[0] + dgate
            dsh_ref[0] = dsh_ref[0] + dsh
            dsc_ref[0] = dsc_ref[0] + dsc

        @pl.when(i == 0)
        def _():
            dg_ref[...] = dg

        @pl.when(i != 0)
        def _():
            dg_ref[...] = dg_ref[...] + dg

    row = lambda i: (i, 0)
    per_batch = pl.BlockSpec((1, 1, D_MODEL), lambda i: (i // per_seq, 0, 0))
    tile = lambda width: pl.BlockSpec((tm, width), row)
    vec = jax.ShapeDtypeStruct((batch, 1, D_MODEL), F32)
    outs, got = _call(
        body, name=name, grid=(tokens // tm,),
        out_shape=[jax.ShapeDtypeStruct((tokens, D_MODEL), BF16), vec, jax.ShapeDtypeStruct((tokens, 2 * k_dim), BF16),
                   jax.ShapeDtypeStruct((tokens, D_MODEL), F32), vec, vec, jax.ShapeDtypeStruct((1, D_MODEL), F32)],
        in_specs=[tile(D_MODEL), per_batch, tile(D_MODEL), tile(k_dim), tile(k_dim), _resident(w_down.shape),
                  _resident(w_gu_t.shape), tile(D_MODEL), _resident((1, D_MODEL)), per_batch],
        out_specs=[tile(D_MODEL), per_batch, tile(2 * k_dim), tile(D_MODEL), per_batch, per_batch,
                   pl.BlockSpec((1, D_MODEL), lambda i: (0, 0))],
        operands=(dy, gate, f, silu, dact, w_down, w_gu_t, x, gnorm, scale1p), comm=comm)
    return (*outs, got)


def _weight_grad(a, b, seq, name, comm=None):
    tokens, n_out = a.shape
    tn = MXU_N

    def body(a_ref, b_ref, o_ref):
        o_ref[...] = _dot_tn(a_ref[...], b_ref[...]).astype(BF16)

    (out,), got = _call(
        body, name=name, grid=(n_out // tn,),
        out_shape=[jax.ShapeDtypeStruct((n_out, D_MODEL), BF16)],
        in_specs=[pl.BlockSpec((tokens, tn), lambda j: (0, j)), _resident((tokens, D_MODEL))],
        out_specs=[pl.BlockSpec((tn, D_MODEL), lambda j: (j, 0))],
        operands=(a, b), comm=comm)
    return out, got


def _group_mean(v, bd):
    hi = v.astype(BF16)
    lo = (v - hi.astype(F32)).astype(BF16)
    blocks = [slice(k * MXU_N, (k + 1) * MXU_N) for k in range(v.shape[1] // MXU_N)]
    return jnp.concatenate([_dot(hi[:, b], bd) + _dot(lo[:, b], bd) for b in blocks], axis=1)


def _sgu_forward(pm_ref, wm_ref, bias_ref, lng_ref, lnb_ref, bd_ref, mixed_scr, n_sub):
    ua = pm_ref[:, 0:D_A].astype(F32)
    va = pm_ref[:, D_A:2 * D_A].astype(F32)
    u_act = _gelu(ua)
    v_act = _gelu(va)
    bd = bd_ref[...]
    vc = v_act - _group_mean(v_act, bd)
    rstd = lax.rsqrt(_group_mean(vc * vc, bd) + EPS)
    vhat = vc * rstd
    vln = vhat * lng_ref[...] + lnb_ref[...]
    left = lax.broadcasted_iota(jnp.int32, (CHUNK, LANES), 1) < HEAD_DIM
    for q in range(n_sub):
        rows = slice(q * CHUNK, (q + 1) * CHUNK)
        for p in range(N_HEADS // 2):
            cols = slice(p * LANES, (p + 1) * LANES)
            vp = vln[rows, cols]
            stacked = jnp.concatenate([jnp.where(left, vp, 0.0), jnp.where(left, 0.0, vp)], axis=0).astype(BF16)
            mixed_scr[rows, cols] = _dot(wm_ref[p], stacked) + bias_ref[:, cols]
    return ua, va, u_act, vhat, rstd, vln


def _halo_specs(tm, tokens, width):
    prev = pl.BlockSpec((HALO, width), lambda i: (jnp.maximum(i * (tm // HALO) - 1, 0), 0))
    nxt = pl.BlockSpec((HALO, width), lambda i: (jnp.minimum((i + 1) * (tm // HALO), tokens // HALO - 1), 0))
    return prev, nxt


def _mixer_forward(x, gnorm, scale1p, shift, w_in_t, gate, w_out, wm, bias_full, lng, lnb, convw, og, bd, seq, name, comm=None):
    tokens = x.shape[0]
    tm = _tile_rows(seq)
    per_seq = seq // tm
    n_sub = tm // CHUNK

    def body(x_ref, xp_ref, g_ref, sc_ref, sh_ref, win_ref, gate_ref, wo_ref, wm_ref, bias_ref, lng_ref, lnb_ref, cw_ref,
             og_ref, bd_ref, h_ref, pm_ref, y_ref, xo_ref, o_ref, mixed_scr):
        i = pl.program_id(0)
        first = (i % per_seq) == 0
        xv = x_ref[...]
        h_ref[...] = ((xv * _rms(xv) * g_ref[...]) * sc_ref[0] + sh_ref[0]).astype(BF16)
        for ck in range(D_PROJ // MXU_N):
            cs = slice(ck * MXU_N, (ck + 1) * MXU_N)
            pm_ref[:, cs] = _dot_nt(h_ref[...], win_ref[cs, :]).astype(BF16)
        xp = xp_ref[...]
        hp = ((xp * _rms(xp) * g_ref[...]) * sc_ref[0] + sh_ref[0]).astype(BF16)
        gates_prev = _dot_nt(hp, win_ref[3 * D_A:5 * D_A, :]).astype(BF16).astype(F32)

        _, _, u_act, _, _, _ = _sgu_forward(pm_ref, wm_ref, bias_ref, lng_ref, lnb_ref, bd_ref, mixed_scr, n_sub)
        ya = u_act * mixed_scr[...]
        y_ref[:, 0:D_A] = (ya * _rms(ya) * og_ref[:, 0:D_A]).astype(BF16)

        bg = pm_ref[:, 2 * D_A:3 * D_A].astype(F32)
        z = pm_ref[:, 3 * D_A:4 * D_A].astype(F32) * pm_ref[:, 4 * D_A:5 * D_A].astype(F32)
        zp = jnp.where(first, 0.0, gates_prev[:, 0:D_A] * gates_prev[:, D_A:2 * D_A])
        zext = jnp.concatenate([zp, z], axis=0)
        z1 = pltpu.roll(zext, 1, 0)[HALO:]
        z2 = pltpu.roll(zext, 2, 0)[HALO:]
        conv = cw_ref[0:1, :] * z2 + cw_ref[1:2, :] * z1 + cw_ref[2:3, :] * z
        yb = bg * conv
        y_ref[:, D_A:2 * D_A] = (yb * _rms(yb) * og_ref[:, D_A:2 * D_A]).astype(BF16)

        f = _dot(y_ref[...], wo_ref[...])
        o_ref[...] = f.astype(BF16)
        xo_ref[...] = xv + gate_ref[0] * f

    prev, _ = _halo_specs(tm, tokens, D_MODEL)
    tile = pl.BlockSpec((tm, D_MODEL), lambda i: (i, 0))
    per_batch = pl.BlockSpec((1, 1, D_MODEL), lambda i: (i // per_seq, 0, 0))
    bf = lambda cols: jax.ShapeDtypeStruct((tokens, cols), BF16)
    outs, got = _call(
        body, name=name, grid=(tokens // tm,),
        out_shape=[bf(D_MODEL), bf(D_PROJ), bf(D_MODEL), jax.ShapeDtypeStruct((tokens, D_MODEL), F32), bf(D_MODEL)],
        in_specs=[tile, prev, _resident((1, D_MODEL)), per_batch, per_batch, _resident(w_in_t.shape), per_batch,
                  _resident(w_out.shape), _resident(wm.shape), _resident(bias_full.shape), _resident(lng.shape),
                  _resident(lnb.shape), _resident(convw.shape), _resident(og.shape), _resident(bd.shape)],
        out_specs=[tile, pl.BlockSpec((tm, D_PROJ), lambda i: (i, 0)), tile, tile, tile],
        scratch_shapes=[pltpu.VMEM((tm, D_A), F32)],
        operands=(x, x, gnorm, scale1p, shift, w_in_t, gate, w_out, wm, bias_full, lng, lnb, convw, og, bd),
        parallel=True, comm=comm)
    return (*outs, got)


def _mixer_backward(proj, dx, gate, o, w_out, x, gnorm, scale1p, w_in_t, wm, bias_full, lng, lnb, convw, og, bd, wm_rows,
                    causal, seq, name, comm=None):
    tokens = proj.shape[0]
    batch = tokens // seq
    tm = _tile_rows(seq)
    per_seq = seq // tm
    n_sub = tm // CHUNK
    ext = tm + 2 * HALO

    def body(pm_ref, pp_ref, pn_ref, dx_ref, dxn_ref, gate_ref, o_ref, wo_ref, x_ref, g_ref, sc_ref, win_ref, wm_ref, bias_ref,
             lng_ref, lnb_ref, cw_ref, og_ref, bd_ref, wmr_ref, causal_ref, do_ref, dgate_ref, dp_ref, dxo_ref, dsh_ref, dsc_ref,
             dgn_ref,
             dog_ref, dcw_ref, dlng_ref, dlnb_ref, dbias_ref, dwm_ref, mixed_scr, dvln_scr, dy_scr):
        i = pl.program_id(0)
        first = (i % per_seq) == 0
        last = (i % per_seq) == per_seq - 1

        @pl.when(i == 0)
        def _():
            dog_ref[...] = jnp.zeros_like(dog_ref)
            dcw_ref[...] = jnp.zeros_like(dcw_ref)
            dlng_ref[...] = jnp.zeros_like(dlng_ref)
            dlnb_ref[...] = jnp.zeros_like(dlnb_ref)
            dbias_ref[...] = jnp.zeros_like(dbias_ref)
            dwm_ref[...] = jnp.zeros_like(dwm_ref)

        dx_v = dx_ref[...]
        do_ref[...] = (gate_ref[0] * dx_v).astype(BF16)
        dgate = jnp.sum(dx_v * o_ref[...].astype(F32), axis=0, keepdims=True)
        dy_scr[...] = _dot_nt(do_ref[...], wo_ref[...])
        dyn_conv = _dot_nt((gate_ref[0] * dxn_ref[...]).astype(BF16), wo_ref[D_A:2 * D_A, :])

        ua, va, u_act, vhat, rstd, vln = _sgu_forward(pm_ref, wm_ref, bias_ref, lng_ref, lnb_ref, bd_ref, mixed_scr, n_sub)
        mixed = mixed_scr[...]
        ya = u_act * mixed
        ra = _rms(ya)
        yhat = ya * ra
        dya_in = dy_scr[:, 0:D_A]
        dog_ref[:, 0:D_A] = dog_ref[:, 0:D_A] + jnp.sum(dya_in * yhat, axis=0, keepdims=True)
        dyh = dya_in * og_ref[:, 0:D_A]
        dya = ra * (dyh - yhat * jnp.mean(dyh * yhat, axis=-1, keepdims=True))
        d_u = dya * mixed
        d_mixed = dya * u_act
        left = lax.broadcasted_iota(jnp.int32, (CHUNK, LANES), 1) < HEAD_DIM
        dbias = jnp.zeros((CHUNK, D_A), F32)
        for q in range(n_sub):
            rows = slice(q * CHUNK, (q + 1) * CHUNK)
            dbias = dbias + d_mixed[rows, :]
            for p in range(N_HEADS // 2):
                cols = slice(p * LANES, (p + 1) * LANES)
                dm = d_mixed[rows, cols]
                stacked = jnp.concatenate([jnp.where(left, dm, 0.0), jnp.where(left, 0.0, dm)], axis=0).astype(BF16)
                dw = _dot_nt(stacked, vln[rows, cols].astype(BF16))
                dwm_ref[2 * p] = dwm_ref[2 * p] + causal_ref[...] * dw[0:CHUNK]
                dwm_ref[2 * p + 1] = dwm_ref[2 * p + 1] + causal_ref[...] * dw[CHUNK:2 * CHUNK]
                dvln_scr[rows, cols] = _dot_tn(wmr_ref[p], stacked)
        dbias_ref[...] = dbias_ref[...] + dbias
        dvln = dvln_scr[...]
        dlng_ref[...] = dlng_ref[...] + jnp.sum(dvln * vhat, axis=0, keepdims=True)
        dlnb_ref[...] = dlnb_ref[...] + jnp.sum(dvln, axis=0, keepdims=True)
        dvh = dvln * lng_ref[...]
        bd = bd_ref[...]
        d_v = rstd * (dvh - _group_mean(dvh, bd) - vhat * _group_mean(dvh * vhat, bd))
        dp_ref[:, 0:D_A] = (d_u * _gelu_grad(ua)).astype(BF16)
        dp_ref[:, D_A:2 * D_A] = (d_v * _gelu_grad(va)).astype(BF16)
        dh_a = _dot(dp_ref[:, 0:2 * D_A], win_ref[0:2 * D_A, :])

        def ext_cols(lo):
            cs = slice(lo, lo + D_A)
            return jnp.concatenate([pp_ref[:, cs], pm_ref[:, cs], pn_ref[:, cs]], axis=0).astype(F32)

        bg, cg, xb = ext_cols(2 * D_A), ext_cols(3 * D_A), ext_cols(4 * D_A)
        row = lax.broadcasted_iota(jnp.int32, (ext, D_A), 0)
        z = jnp.where(jnp.logical_and(first, row < HALO), 0.0, cg * xb)
        z1 = pltpu.roll(z, 1, 0)
        z2 = pltpu.roll(z, 2, 0)
        w0, w1, w2 = cw_ref[0:1, :], cw_ref[1:2, :], cw_ref[2:3, :]
        conv = w0 * z2 + w1 * z1 + w2 * z
        yb = bg * conv
        rb = _rms(yb)
        yhb = yb * rb
        dyn = jnp.where(last, 0.0, dyn_conv)
        dyb_in = jnp.concatenate([jnp.zeros((HALO, D_A), F32), dy_scr[:, D_A:2 * D_A], dyn], axis=0)
        dyhb = dyb_in * og_ref[:, D_A:2 * D_A]
        dyb = rb * (dyhb - yhb * jnp.mean(dyhb * yhb, axis=-1, keepdims=True))
        d_conv = dyb * bg
        dz = w2 * d_conv + w1 * pltpu.roll(d_conv, ext - 1, 0) + w0 * pltpu.roll(d_conv, ext - 2, 0)
        main = slice(HALO, HALO + tm)
        dp_ref[:, 2 * D_A:3 * D_A] = (dyb * conv)[main].astype(BF16)
        dp_ref[:, 3 * D_A:4 * D_A] = (dz * xb)[main].astype(BF16)
        dp_ref[:, 4 * D_A:5 * D_A] = (dz * cg)[main].astype(BF16)
        dog_ref[:, D_A:2 * D_A] = dog_ref[:, D_A:2 * D_A] + jnp.sum((dyb_in * yhb)[main], axis=0, keepdims=True)
        dcm = d_conv[main]
        dcw_ref[0:1, :] = dcw_ref[0:1, :] + jnp.sum(dcm * z2[main], axis=0, keepdims=True)
        dcw_ref[1:2, :] = dcw_ref[1:2, :] + jnp.sum(dcm * z1[main], axis=0, keepdims=True)
        dcw_ref[2:3, :] = dcw_ref[2:3, :] + jnp.sum(dcm * z[main], axis=0, keepdims=True)

        dh = dh_a + _dot(dp_ref[:, 2 * D_A:5 * D_A], win_ref[2 * D_A:5 * D_A, :])
        xv = x_ref[...]
        r = _rms(xv)
        xn = xv * r
        gn = g_ref[...]
        dsh = jnp.sum(dh, axis=0, keepdims=True)
        dsc = jnp.sum(dh * (xn * gn), axis=0, keepdims=True)
        dhn = dh * sc_ref[0]
        dgn = jnp.sum(dhn * xn, axis=0, keepdims=True)
        dxn = dhn * gn
        dxo_ref[...] = dx_v + r * (dxn - xn * jnp.mean(dxn * xn, axis=-1, keepdims=True))

        @pl.when(first)
        def _():
            dgate_ref[0] = dgate
            dsh_ref[0] = dsh
            dsc_ref[0] = dsc

        @pl.when(jnp.logical_not(first))
        def _():
            dgate_ref[0] = dgate_ref[0] + dgate
            dsh_ref[0] = dsh_ref[0] + dsh
            dsc_ref[0] = dsc_ref[0] + dsc

        @pl.when(i == 0)
        def _():
            dgn_ref[...] = dgn

        @pl.when(i != 0)
        def _():
            dgn_ref[...] = dgn_ref[...] + dgn

    prev_p, next_p = _halo_specs(tm, tokens, D_PROJ)
    _, next_d = _halo_specs(tm, tokens, D_MODEL)
    fixed2 = lambda shape: pl.BlockSpec(shape, lambda i: (0, 0))
    tile = pl.BlockSpec((tm, D_MODEL), lambda i: (i, 0))
    per_batch = pl.BlockSpec((1, 1, D_MODEL), lambda i: (i // per_seq, 0, 0))
    vec = jax.ShapeDtypeStruct((batch, 1, D_MODEL), F32)
    outs, got = _call(
        body, name=name, grid=(tokens // tm,),
        out_shape=[jax.ShapeDtypeStruct((tokens, D_MODEL), BF16), vec, jax.ShapeDtypeStruct((tokens, D_PROJ), BF16),
                   jax.ShapeDtypeStruct((tokens, D_MODEL), F32), vec, vec, jax.ShapeDtypeStruct((1, D_MODEL), F32),
                   jax.ShapeDtypeStruct((1, D_MODEL), F32), jax.ShapeDtypeStruct((8, D_A), F32),
                   jax.ShapeDtypeStruct((1, D_A), F32), jax.ShapeDtypeStruct((1, D_A), F32),
                   jax.ShapeDtypeStruct((CHUNK, D_A), F32), jax.ShapeDtypeStruct((N_HEADS, CHUNK, CHUNK), F32)],
        in_specs=[pl.BlockSpec((tm, D_PROJ), lambda i: (i, 0)), prev_p, next_p, tile, next_d, per_batch, tile,
                  _resident(w_out.shape), tile, _resident((1, D_MODEL)), per_batch, _resident(w_in_t.shape),
                  _resident(wm.shape), _resident(bias_full.shape), _resident(lng.shape), _resident(lnb.shape),
                  _resident(convw.shape), _resident(og.shape), _resident(bd.shape), _resident(wm_rows.shape),
                  _resident(causal.shape)],
        out_specs=[tile, per_batch, pl.BlockSpec((tm, D_PROJ), lambda i: (i, 0)), tile, per_batch, per_batch,
                   fixed2((1, D_MODEL)), fixed2((1, D_MODEL)), fixed2((8, D_A)), fixed2((1, D_A)), fixed2((1, D_A)),
                   fixed2((CHUNK, D_A)), pl.BlockSpec((N_HEADS, CHUNK, CHUNK), lambda i: (0, 0, 0))],
        scratch_shapes=[pltpu.VMEM((tm, D_A), F32), pltpu.VMEM((tm, D_A), F32), pltpu.VMEM((tm, D_MODEL), F32)],
        operands=(proj, proj, proj, dx, dx, gate, o, w_out, x, gnorm, scale1p, w_in_t, wm, bias_full, lng, lnb, convw, og, bd,
                  wm_rows, causal), comm=comm)
    return (*outs, got)


def _adamw_update(wv, gv, mv, vv):
    nm = ADAM_B1 * mv + (1.0 - ADAM_B1) * gv
    nv = ADAM_B2 * vv + (1.0 - ADAM_B2) * (gv * gv)
    m_hat = nm / (1.0 - ADAM_B1 ** ADAM_STEP)
    v_hat = nv / (1.0 - ADAM_B2 ** ADAM_STEP)
    return -ADAM_LR * (m_hat / (jnp.sqrt(v_hat) + ADAM_EPS) + ADAM_WD * wv), nm, nv


def _adamw_rows(recv, w, m, v, name):
    depth, rows, cols = w.shape
    tr = rows // 2
    last = rows // tr - 1

    def body(*refs):
        r_refs, (w_ref, m_ref, v_ref, g_ref, d_ref, nm_ref, nv_ref) = refs[:depth], refs[depth:]
        for l in range(depth):
            @pl.when(pl.program_id(0) == l)
            def _(r_ref=r_refs[l]):
                acc = r_ref[0].astype(F32)
                for d in range(1, N_DEV):
                    acc = acc + r_ref[d].astype(F32)
                g_ref[0] = acc
                d_ref[0], nm_ref[0], nv_ref[0] = _adamw_update(w_ref[0], acc, m_ref[0], v_ref[0])

    def slots(l):
        return pl.BlockSpec((N_DEV, tr, cols), lambda ll, i: (0, jnp.where(ll == l, i, jnp.where(ll < l, 0, last)), 0))

    spec = pl.BlockSpec((1, tr, cols), lambda ll, i: (ll, i, 0))
    return pl.pallas_call(
        body, name=name, grid=(depth, rows // tr),
        out_shape=[jax.ShapeDtypeStruct((depth, rows, cols), F32)] * 4,
        in_specs=[slots(l) for l in range(depth)] + [spec] * 3, out_specs=[spec] * 4,
        compiler_params=_params(2),
    )(*recv, w, m, v)


def _adamw(w, g, m, v, name):
    rows, cols = w.shape
    tr = max(t for t in range(8, 513, 8) if rows % t == 0)

    def body(w_ref, g_ref, m_ref, v_ref, d_ref, nm_ref, nv_ref):
        d_ref[...], nm_ref[...], nv_ref[...] = _adamw_update(w_ref[...], g_ref[...], m_ref[...], v_ref[...])

    spec = pl.BlockSpec((tr, cols), lambda i: (i, 0))
    return pl.pallas_call(
        body, name=name, grid=(rows // tr,),
        out_shape=[jax.ShapeDtypeStruct((rows, cols), F32)] * 3,
        in_specs=[spec] * 4, out_specs=[spec] * 3,
        compiler_params=_params(parallel=True),
    )(w, g, m, v)


def _adamw_many(ws, gs, ms, vs, name):
    n = len(ws)
    two_d = lambda a: a.reshape(-1, a.shape[-1])

    def body(*refs):
        w_refs, g_refs, m_refs, v_refs = refs[:n], refs[n:2 * n], refs[2 * n:3 * n], refs[3 * n:4 * n]
        d_refs, nm_refs, nv_refs = refs[4 * n:5 * n], refs[5 * n:6 * n], refs[6 * n:]
        for k in range(n):
            d_refs[k][...], nm_refs[k][...], nv_refs[k][...] = _adamw_update(
                w_refs[k][...], g_refs[k][...], m_refs[k][...], v_refs[k][...])

    flat = [two_d(a) for a in ws]
    outs = pl.pallas_call(
        body, name=name, out_shape=[jax.ShapeDtypeStruct(a.shape, F32) for a in flat] * 3,
        in_specs=[pl.BlockSpec(memory_space=pltpu.VMEM)] * (4 * n),
        out_specs=[pl.BlockSpec(memory_space=pltpu.VMEM)] * (3 * n),
        compiler_params=pltpu.CompilerParams(vmem_limit_bytes=VMEM_LIMIT),
    )(*flat, *[two_d(a) for a in gs], *[two_d(a) for a in ms], *[two_d(a) for a in vs])
    shaped = [o.reshape(ws[k % n].shape) for k, o in enumerate(outs)]
    return shaped[:n], shaped[n:2 * n], shaped[2 * n:]


def _adamw_nd(w, g, m, v, name):
    shape = w.shape
    two_d = (-1, shape[-1])
    d, nm, nv = _adamw(w.reshape(two_d), g.reshape(two_d), m.reshape(two_d), v.reshape(two_d), name)
    return d.reshape(shape), nm.reshape(shape), nv.reshape(shape)


def kernel(x, c, ada_w, ada_b, norm_ffn1_g, ffn1_w_gu, ffn1_w_down, norm_mix_g, mix_w_in, sgu_ln_g, sgu_ln_b, sgu_w_s, sgu_b, conv_w, out_norm_g, mix_w_out, norm_ffn2_g, ffn2_w_gu, ffn2_w_down, final_norm_g, loss_target, m_ada_w, m_ada_b, m_norm_ffn1_g, m_ffn1_w_gu, m_ffn1_w_down, m_norm_mix_g, m_mix_w_in, m_sgu_ln_g, m_sgu_ln_b, m_sgu_w_s, m_sgu_b, m_conv_w, m_out_norm_g, m_mix_w_out, m_norm_ffn2_g, m_ffn2_w_gu, m_ffn2_w_down, m_final_norm_g, v_ada_w, v_ada_b, v_norm_ffn1_g, v_ffn1_w_gu, v_ffn1_w_down, v_norm_mix_g, v_mix_w_in, v_sgu_ln_g, v_sgu_ln_b, v_sgu_w_s, v_sgu_b, v_conv_w, v_out_norm_g, v_mix_w_out, v_norm_ffn2_g, v_ffn2_w_gu, v_ffn2_w_down, v_final_norm_g):
    batch, seq, _ = x.shape
    tokens = batch * seq
    me = 4 * lax.axis_index("x") + 2 * lax.axis_index("y") + lax.axis_index("c")
    weights = dict(ada_w=ada_w, ada_b=ada_b, norm_ffn1_g=norm_ffn1_g, ffn1_w_gu=ffn1_w_gu, ffn1_w_down=ffn1_w_down,
                   norm_mix_g=norm_mix_g, mix_w_in=mix_w_in, sgu_ln_g=sgu_ln_g, sgu_ln_b=sgu_ln_b, sgu_w_s=sgu_w_s,
                   sgu_b=sgu_b, conv_w=conv_w, out_norm_g=out_norm_g, mix_w_out=mix_w_out, norm_ffn2_g=norm_ffn2_g,
                   ffn2_w_gu=ffn2_w_gu, ffn2_w_down=ffn2_w_down, final_norm_g=final_norm_g)
    mom1 = dict(ada_w=m_ada_w, ada_b=m_ada_b, norm_ffn1_g=m_norm_ffn1_g, ffn1_w_gu=m_ffn1_w_gu,
                ffn1_w_down=m_ffn1_w_down, norm_mix_g=m_norm_mix_g, mix_w_in=m_mix_w_in, sgu_ln_g=m_sgu_ln_g,
                sgu_ln_b=m_sgu_ln_b, sgu_w_s=m_sgu_w_s, sgu_b=m_sgu_b, conv_w=m_conv_w, out_norm_g=m_out_norm_g,
                mix_w_out=m_mix_w_out, norm_ffn2_g=m_norm_ffn2_g, ffn2_w_gu=m_ffn2_w_gu, ffn2_w_down=m_ffn2_w_down,
                final_norm_g=m_final_norm_g)
    mom2 = dict(ada_w=v_ada_w, ada_b=v_ada_b, norm_ffn1_g=v_norm_ffn1_g, ffn1_w_gu=v_ffn1_w_gu,
                ffn1_w_down=v_ffn1_w_down, norm_mix_g=v_norm_mix_g, mix_w_in=v_mix_w_in, sgu_ln_g=v_sgu_ln_g,
                sgu_ln_b=v_sgu_ln_b, sgu_w_s=v_sgu_w_s, sgu_b=v_sgu_b, conv_w=v_conv_w, out_norm_g=v_out_norm_g,
                mix_w_out=v_mix_w_out, norm_ffn2_g=v_norm_ffn2_g, ffn2_w_gu=v_ffn2_w_gu, ffn2_w_down=v_ffn2_w_down,
                final_norm_g=v_final_norm_g)

    big = ("ffn1_w_gu", "ffn1_w_down", "mix_w_in", "mix_w_out", "ffn2_w_gu", "ffn2_w_down")
    transposed = ("ffn1_w_gu", "mix_w_in", "ffn2_w_gu")
    as_rows = lambda nm, a: jnp.swapaxes(a, 1, 2) if nm in transposed else a
    shard = {(l, nm): as_rows(nm, weights[nm])[l].astype(BF16) for l in range(DEPTH) for nm in big}
    full_w = {}

    def gather_of(keys):
        return keys, _GatherRows([shard[k] for k in keys])

    def landed(plan, got):
        full_w.update(zip(plan[0], got))

    ada_cols = ada_w.shape[2]
    ada_b_cols = lax.dynamic_slice_in_dim(ada_b, me * ada_cols, ada_cols, axis=1).reshape(DEPTH, 1, ada_cols)
    plan = gather_of([(0, "ffn1_w_gu")])
    c_dev, convw_dev, ada_recv, got = _prologue(
        jnp.pad(c, ((0, 8 - batch), (0, 0))), jnp.pad(conv_w.reshape(-1), (0, 8 * LANES - conv_w.size)).reshape(8, LANES),
        ada_w, ada_b_cols, plan[1])
    landed(plan, got)
    c_all = c_dev[:, :batch].reshape(N_DEV * batch, D_MODEL)
    convw_all = convw_dev.reshape(N_DEV, -1)[:, :conv_w.size].reshape((N_DEV,) + conv_w.shape)
    convw_full = jnp.transpose(convw_all, (1, 2, 0, 3)).reshape(DEPTH, 3, D_A)
    ada_mine = jnp.transpose(ada_recv[:, :, :batch, :], (1, 2, 0, 3)).reshape(DEPTH, batch, N_MOD * D_MODEL)
    mod = ada_mine.reshape(DEPTH, batch, N_MOD, 1, D_MODEL)

    causal = jnp.tril(jnp.ones((CHUNK, CHUNK), F32))
    bd = jnp.kron(jnp.eye(MXU_N // HEAD_DIM, dtype=F32), jnp.full((HEAD_DIM, HEAD_DIM), 1.0 / HEAD_DIM, F32)).astype(BF16)
    row_vec = lambda a: a.reshape(1, -1)

    hosted_gathers = {
        (0, "ffn1"): [(0, "ffn1_w_down"), (0, "mix_w_in"), (0, "mix_w_out")],
        (0, "ffn_down1"): [(0, "ffn2_w_down")],
        (0, "mix_in"): [(0, "ffn2_w_gu")],
        (0, "ffn2"): [(1, "ffn1_w_gu"), (1, "ffn1_w_down"), (1, "mix_w_in"), (1, "mix_w_out")],
        (1, "ffn1"): [(1, "ffn2_w_gu"), (1, "ffn2_w_down")],
    }

    def hosting(l, site):
        keys = hosted_gathers.get((l, site))
        return gather_of(keys) if keys else (None, None)

    xs = x.reshape(tokens, D_MODEL)
    saved = []
    for l in range(DEPTH):
        sh1, sc1, g1, sh2, sc2, g2, sh3, sc3, g3 = [mod[l, :, k] for k in range(N_MOD)]
        wm_masked = (sgu_w_s[l] * causal[None]).astype(BF16)
        mixer_consts = dict(
            wm=jnp.concatenate([wm_masked[0::2], wm_masked[1::2]], axis=2),
            bias_full=jnp.repeat(sgu_b[l].T, HEAD_DIM, axis=1),
            lng=row_vec(jnp.tile(sgu_ln_g[l], N_HEADS)), lnb=row_vec(jnp.tile(sgu_ln_b[l], N_HEADS)),
            convw=jnp.pad(convw_full[l], ((0, 5), (0, 0))), og=row_vec(out_norm_g[l]), bd=bd)
        x0 = xs
        plan = hosting(l, "ffn1")
        if l == 0:
            h1, a1, s1, w1, got = _normmod_matmul(x0, row_vec(norm_ffn1_g[l]), 1.0 + sc1, sh1, full_w[l, "ffn1_w_gu"], seq, "ffn_up", plan[1])
            landed(plan, got)
            plan = hosting(l, "ffn_down1")
            x1, f1, got = _matmul_residual(a1, full_w[l, "ffn1_w_down"], x0, g1, 0.5, seq, "ffn_down", plan[1])
        else:
            h1, a1, s1, w1, x1, f1, got = _ffn_forward(
                x0, row_vec(norm_ffn1_g[l]), 1.0 + sc1, sh1, full_w[l, "ffn1_w_gu"], full_w[l, "ffn1_w_down"], g1, 0.5, seq, "ffn_fwd",
                comm=plan[1])
        if got:
            landed(plan, got)
        plan = hosting(l, "mix_in")
        h2, proj, ymix, x2, o2, got = _mixer_forward(
            x1, row_vec(norm_mix_g[l]), 1.0 + sc2, sh2, full_w[l, "mix_w_in"], g2, full_w[l, "mix_w_out"], seq=seq,
            name="mixer_forward", comm=plan[1], **mixer_consts)
        if got:
            landed(plan, got)
        plan = hosting(l, "ffn2")
        if l + 1 < DEPTH:
            h3, a3, s3, w3, x3, f3, got = _ffn_forward(
                x2, row_vec(norm_ffn2_g[l]), 1.0 + sc3, sh3, full_w[l, "ffn2_w_gu"], full_w[l, "ffn2_w_down"], g3, 0.5, seq, "ffn_fwd",
                comm=plan[1])
        else:
            head = (loss_target.reshape(tokens, D_MODEL), row_vec(final_norm_g))
            h3, a3, s3, w3, x3, f3, d_final_g, loss_cols, got = _ffn_forward(
                x2, row_vec(norm_ffn2_g[l]), 1.0 + sc3, sh3, full_w[l, "ffn2_w_gu"], full_w[l, "ffn2_w_down"], g3, 0.5, seq, "ffn_fwd_loss",
                loss_head=head, comm=plan[1])
        if got:
            landed(plan, got)
        saved.append(dict(x0=x0, x1=x1, x2=x2, h1=h1, h2=h2, h3=h3, a1=a1, s1=s1, w1=w1, a3=a3, s3=s3, w3=w3, f1=f1, f3=f3, o2=o2, proj=proj,
                          ymix=ymix, mixer_consts=mixer_consts, wm_rows=wm_masked.reshape(N_HEADS // 2, 2 * CHUNK, CHUNK), sc=(1.0 + sc1, 1.0 + sc2, 1.0 + sc3), gates=(g1, g2, g3)))
        xs = x3

    dx = xs

    recv = {}
    small_grads = [None] * DEPTH
    d_mod = [None] * DEPTH

    mix_names = ("out_norm_g", "sgu_ln_g", "sgu_ln_b", "sgu_w_s", "sgu_b", "conv_w")
    late_names = ("norm_ffn1_g", "norm_mix_g", "norm_ffn2_g")

    def mix_parts(l):
        return [small_grads[l][nm] for nm in mix_names]

    def late_parts(l):
        return [small_grads[l][nm] for nm in late_names] + [d_mod[l]]

    pending = []

    def scatter_later(l, nm, grad):
        pending.append(((l, nm), _ScatterRows([grad])))

    def host():
        keys, parts = [k for k, _ in pending], [p for _, p in pending]
        pending.clear()
        return keys, (_Exchanges(parts) if parts else None)

    def hosted(keys, got):
        if got:
            recv.update(zip(keys, got))

    for l in reversed(range(DEPTH)):
        sv = saved[l]
        mc = sv["mixer_consts"]
        if l + 1 < DEPTH:
            pending.append((("late", l + 1), _GatherRows([_pack_small(late_parts(l + 1))])))
        keys, comm = host()
        df3, dg3, dgu3, dx2, dsh3, dsc3, dn3, got = _ffn_backward(
            dx, sv["gates"][2], sv["f3"], sv["s3"], sv["w3"], full_w[l, "ffn2_w_down"], full_w[l, "ffn2_w_gu"], sv["x2"],
            row_vec(norm_ffn2_g[l]), sv["sc"][2], 0.5, seq, "ffn_bwd", comm)
        hosted(keys, got)
        gw_down2, _ = _weight_grad(sv["a3"], df3, seq, "grad_w_down")
        scatter_later(l, "ffn2_w_down", gw_down2)
        keys, comm = host()
        gw_gu2, got = _weight_grad(dgu3, sv["h3"], seq, "grad_w_gu", comm)
        hosted(keys, got)
        scatter_later(l, "ffn2_w_gu", gw_gu2)
        keys, comm = host()
        do2, dg2, dproj, dx1, dsh2, dsc2, dn2, d_og, d_cw, d_lng, d_lnb, d_bias, d_wm, got = _mixer_backward(
            sv["proj"], dx2, sv["gates"][1], sv["o2"], full_w[l, "mix_w_out"], sv["x1"], row_vec(norm_mix_g[l]), sv["sc"][1],
            full_w[l, "mix_w_in"], wm_rows=sv["wm_rows"], causal=causal, seq=seq, name="mixer_backward", comm=comm, **mc)
        hosted(keys, got)
        small_grads[l] = dict(
            out_norm_g=d_og, sgu_ln_g=d_lng.reshape(N_HEADS, HEAD_DIM).sum(0), sgu_ln_b=d_lnb.reshape(N_HEADS, HEAD_DIM).sum(0),
            sgu_w_s=d_wm, sgu_b=d_bias.reshape(CHUNK, N_HEADS, HEAD_DIM).sum(-1).T, conv_w=d_cw[0:3])
        gw_out, _ = _weight_grad(sv["ymix"], do2, seq, "grad_w_out")
        scatter_later(l, "mix_w_out", gw_out)
        keys, comm = host()
        gw_in, got = _weight_grad(dproj, sv["h2"], seq, "grad_w_in", comm)
        hosted(keys, got)
        scatter_later(l, "mix_w_in", gw_in)
        keys, comm = host()
        pending.append((("mix", l), _GatherRows([_pack_small(mix_parts(l))])))
        if l > 0:
            df1, dg1, dgu1, dx0, dsh1, dsc1, dn1, got = _ffn_backward(
                dx1, sv["gates"][0], sv["f1"], sv["s1"], sv["w1"], full_w[l, "ffn1_w_down"], full_w[l, "ffn1_w_gu"], sv["x0"],
                row_vec(norm_ffn1_g[l]), sv["sc"][0], 0.5, seq, "ffn_bwd", comm)
        else:
            df1, dg1, dgu1, got = _residual_backward(dx1, sv["gates"][0], sv["f1"], full_w[l, "ffn1_w_down"], 0.5, sv["s1"], sv["w1"], seq, "ffn_down_bwd", comm)
        hosted(keys, got)
        gw_down1, _ = _weight_grad(sv["a1"], df1, seq, "grad_w_down")
        scatter_later(l, "ffn1_w_down", gw_down1)
        keys, comm = host()
        gw_gu1, got = _weight_grad(dgu1, sv["h1"], seq, "grad_w_gu", comm)
        hosted(keys, got)
        scatter_later(l, "ffn1_w_gu", gw_gu1)
        if l == 0:
            keys, comm = host()
            dx0, dsh1, dsc1, dn1, got = _matmul_normmod_backward(dgu1, full_w[l, "ffn1_w_gu"], sv["x0"], dx1, row_vec(norm_ffn1_g[l]), sv["sc"][0], seq, "ffn_up_bwd", comm)
            hosted(keys, got)
        dx = dx0
        small_grads[l].update(norm_ffn1_g=dn1, norm_mix_g=dn2, norm_ffn2_g=dn3)
        d_mod[l] = jnp.concatenate([dsh1, dsc1, dg1, dsh2, dsc2, dg2, dsh3, dsc3, dg3], axis=1)
    grad_x = dx.reshape(batch, seq, D_MODEL)

    grad_big, delta, new_m, new_v = {}, {}, {}, {}
    for nm in big:
        results = _adamw_rows([recv[l, nm] for l in range(DEPTH)], as_rows(nm, weights[nm]), as_rows(nm, mom1[nm]),
                              as_rows(nm, mom2[nm]), "adamw_" + nm)
        grad_big[nm], delta[nm], new_m[nm], new_v[nm] = [as_rows(nm, r) for r in results]

    last_parts = late_parts(0) + [d_final_g, loss_cols]
    last_shapes = [p.shape for p in last_parts]
    packed_all, packed_sum = _all_gather_small(_pack_small(last_parts), "reduce_small")
    late_sum = {0: _unpack_small(packed_sum, last_shapes)}
    d_mod_dev = {0: _unpack_small(packed_all, last_shapes, lead=(N_DEV,))[len(late_names)]}
    mix_sum = {}
    for l in range(DEPTH):
        gathered = recv["mix", l].reshape(N_DEV, -1, LANES)
        mix_sum[l] = _unpack_small(_sum_gathered(gathered, "sum_mix"), [p.shape for p in mix_parts(l)])
        if l > 0:
            shapes_l = [p.shape for p in late_parts(l)]
            gathered = recv["late", l].reshape(N_DEV, -1, LANES)
            late_sum[l] = _unpack_small(_sum_gathered(gathered, "sum_late"), shapes_l)
            d_mod_dev[l] = _unpack_small(gathered, shapes_l, lead=(N_DEV,))[len(late_names)]
    grad_small = {}
    for group, names in ((mix_sum, mix_names), (late_sum, late_names)):
        for k, nm in enumerate(names):
            grad_small[nm] = jnp.stack([group[l][k] for l in range(DEPTH)]).reshape(
                (DEPTH, 3, D_A) if nm == "conv_w" else weights[nm].shape)
    grad_small["conv_w"] = lax.dynamic_slice_in_dim(grad_small["conv_w"], me * conv_w.shape[2], conv_w.shape[2], axis=2)
    grad_small["final_norm_g"] = late_sum[0][len(late_names) + 1].reshape(final_norm_g.shape)
    loss = jnp.sum(late_sum[0][len(late_names) + 2])
    d_ada_all = jnp.stack([d_mod_dev[l] for l in range(DEPTH)]).reshape(DEPTH, N_DEV * batch, N_MOD * D_MODEL)
    d_ada_cols = lax.dynamic_slice_in_dim(d_ada_all, me * ada_cols, ada_cols, axis=2)
    g_ada_w, g_ada_b = _ada_backward(c_all, d_ada_cols, d_ada_all)

    grads = dict(grad_big)
    grads.update(grad_small)
    grads["ada_w"] = g_ada_w
    grads["ada_b"] = g_ada_b.reshape(ada_b.shape)

    names = ("ada_w", "ada_b", "norm_ffn1_g", "ffn1_w_gu", "ffn1_w_down", "norm_mix_g", "mix_w_in", "sgu_ln_g",
             "sgu_ln_b", "sgu_w_s", "sgu_b", "conv_w", "out_norm_g", "mix_w_out", "norm_ffn2_g", "ffn2_w_gu",
             "ffn2_w_down", "final_norm_g")
    delta["ada_w"], new_m["ada_w"], new_v["ada_w"] = _adamw_nd(ada_w, grads["ada_w"], m_ada_w, v_ada_w, "adamw_ada_w")
    rest = [nm for nm in names if nm not in big and nm != "ada_w"]
    pick = lambda src: [src[nm] for nm in rest]
    for nm, d_k, m_k, v_k in zip(rest, *_adamw_many(pick(weights), pick(grads), pick(mom1), pick(mom2), "adamw_small")):
        delta[nm], new_m[nm], new_v[nm] = d_k, m_k, v_k

    return (loss, grad_x, *[grads[nm] for nm in names], *[delta[nm] for nm in names],
            *[new_m[nm] for nm in names], *[new_v[nm] for nm in names])
```

```python
import math

import jax
import jax.numpy as jnp
from jax import lax
from jax.experimental import pallas as pl
from jax.experimental.pallas import tpu as pltpu

F32 = jnp.float32
BF16 = jnp.bfloat16

D_MODEL = 1024
D_A = 512
D_PROJ = 2560
N_HEADS = 8
HEAD_DIM = 64
CHUNK = 128
N_MOD = 9
DEPTH = 2
EPS = 1e-6
N_DEV = 8
LANES = 128
MXU_N = 256
HALO = 16
VMEM_LIMIT = 56 * 1024 * 1024
FORWARD_STEPS = 2

ADAM_LR = 0.001
ADAM_B1 = 0.9
ADAM_B2 = 0.999
ADAM_EPS = 1e-08
ADAM_WD = 0.01
ADAM_STEP = 10

MESH = pl.DeviceIdType.MESH


def _dot(a, b):
    return jnp.dot(a, b, preferred_element_type=F32)


def _dot_nt(a, b):
    return lax.dot_general(a, b, (((1,), (1,)), ((), ())), preferred_element_type=F32)


def _dot_tn(a, b):
    return lax.dot_general(a, b, (((0,), (0,)), ((), ())), preferred_element_type=F32)


def _sigmoid(x):
    return 0.5 * jnp.tanh(0.5 * x) + 0.5


def _gelu(x):
    return 0.5 * x * (1.0 + lax.erf(x * (1.0 / math.sqrt(2.0))))


def _gelu_grad(x):
    cdf = 0.5 * (1.0 + lax.erf(x * (1.0 / math.sqrt(2.0))))
    return cdf + x * jnp.exp(-0.5 * x * x) * (1.0 / math.sqrt(2.0 * math.pi))


def _params(n_axes=1, parallel=False):
    sem = ("parallel" if parallel else "arbitrary",) * n_axes
    return pltpu.CompilerParams(dimension_semantics=sem, vmem_limit_bytes=VMEM_LIMIT)


def _resident(shape):
    nd = len(shape)
    return pl.BlockSpec(shape, lambda *_: (0,) * nd, pipeline_mode=pl.Buffered(1))


def _tile_rows(seq):
    return min(512, seq)


def _my_position():
    x, y, c = lax.axis_index("x"), lax.axis_index("y"), lax.axis_index("c")
    return x, y, c, 4 * x + 2 * y + c


def _peer(x, y, c, p):
    return (x ^ ((p >> 2) & 1), y ^ ((p >> 1) & 1), c ^ (p & 1))


class _GatherRows:
    def __init__(self, shards):
        self.operands = list(shards)
        n = len(shards)
        self.out_shape = [jax.ShapeDtypeStruct((N_DEV * s.shape[0], s.shape[1]), s.dtype) for s in shards]
        self.scratch = [pltpu.SemaphoreType.DMA((n, N_DEV - 1)), pltpu.SemaphoreType.DMA((n, N_DEV - 1)),
                        pltpu.SemaphoreType.DMA((n,))]

    def _plan(self, src, dst, send, recv, loc):
        x, y, c, _ = _my_position()
        me, sib = (x, y, c), (x, y, 1 - c)
        chips = [(1 - x, y), (x, 1 - y), (1 - x, 1 - y)]
        plans = []
        for k, shard in enumerate(self.operands):
            rows = shard.shape[0]

            def blk(pos, k=k, rows=rows):
                return dst[k].at[pl.ds((4 * pos[0] + 2 * pos[1] + pos[2]) * rows, rows), :]

            def rc(s, block, to, source=None, k=k, blk=blk):
                return pltpu.make_async_remote_copy(
                    src_ref=blk(block) if source is None else source, dst_ref=blk(block),
                    send_sem=send.at[k, s], recv_sem=recv.at[k, s], device_id=to, device_id_type=MESH)

            plans.append(dict(
                local=pltpu.make_async_copy(src[k], blk(me), loc.at[k]),
                first=[rc(0, me, sib, src[k])] + [rc(1 + j, me, (*chip, c), src[k]) for j, chip in enumerate(chips)],
                landed=[rc(1 + j, (*chip, c), me) for j, chip in enumerate(chips)],
                passed=[rc(4 + j, (*chip, c), sib) for j, chip in enumerate(chips)],
                from_sib=[rc(0, sib, me)] + [rc(4 + j, (*chip, 1 - c), me) for j, chip in enumerate(chips)]))
        return plans

    def start(self, src, dst, send, recv, loc):
        for plan in self._plan(src, dst, send, recv, loc):
            plan["local"].start()
            for cp in plan["first"]:
                cp.start()

    def forward(self, src, dst, send, recv, loc):
        for plan in self._plan(src, dst, send, recv, loc):
            for landed, passed in zip(plan["landed"], plan["passed"]):
                landed.wait_recv()
                passed.start()

    def finish(self, src, dst, send, recv, loc):
        for plan in self._plan(src, dst, send, recv, loc):
            for cp in plan["from_sib"]:
                cp.wait_recv()
            for cp in plan["first"] + plan["passed"]:
                cp.wait_send()
            plan["local"].wait()


class _ScatterRows:
    def __init__(self, grads):
        self.operands = list(grads)
        n = len(grads)
        self.out_shape = [jax.ShapeDtypeStruct((N_DEV, g.shape[0] // N_DEV, g.shape[1]), g.dtype) for g in grads]
        self.scratch = [pltpu.SemaphoreType.DMA((n, N_DEV - 1)), pltpu.SemaphoreType.DMA((n, N_DEV - 1)),
                        pltpu.SemaphoreType.DMA((n,))]

    def _plan(self, src, dst, send, recv, loc):
        x, y, c, me = _my_position()
        copies = []
        for k, grad in enumerate(self.operands):
            rows = grad.shape[0] // N_DEV
            copies.append(pltpu.make_async_copy(src[k].at[pl.ds(me * rows, rows), :], dst[k].at[me], loc.at[k]))
            for p in range(1, N_DEV):
                px, py, pc = _peer(x, y, c, p)
                copies.append(pltpu.make_async_remote_copy(
                    src_ref=src[k].at[pl.ds((4 * px + 2 * py + pc) * rows, rows), :], dst_ref=dst[k].at[me],
                    send_sem=send.at[k, p - 1], recv_sem=recv.at[k, p - 1], device_id=(px, py, pc), device_id_type=MESH))
        return copies

    def start(self, src, dst, send, recv, loc):
        for cp in self._plan(src, dst, send, recv, loc):
            cp.start()

    def forward(self, src, dst, send, recv, loc):
        pass

    def finish(self, src, dst, send, recv, loc):
        for cp in self._plan(src, dst, send, recv, loc):
            cp.wait()


class _Exchanges:
    def __init__(self, parts):
        self.parts = list(parts)
        self.operands = [op for part in self.parts for op in part.operands]
        self.out_shape = [shp for part in self.parts for shp in part.out_shape]
        self.scratch = [scr for part in self.parts for scr in part.scratch]

    def _each(self, src, dst, sems):
        at, sem_at = 0, 0
        for part in self.parts:
            n, n_sem = len(part.operands), len(part.scratch)
            yield part, src[at:at + n], dst[at:at + n], sems[sem_at:sem_at + n_sem]
            at, sem_at = at + n, sem_at + n_sem

    def start(self, src, dst, *sems):
        for part, part_src, part_dst, part_sems in self._each(src, dst, sems):
            part.start(part_src, part_dst, *part_sems)

    def forward(self, src, dst, *sems):
        for part, part_src, part_dst, part_sems in self._each(src, dst, sems):
            part.forward(part_src, part_dst, *part_sems)

    def finish(self, src, dst, *sems):
        for part, part_src, part_dst, part_sems in self._each(src, dst, sems):
            part.finish(part_src, part_dst, *part_sems)


_ANY = pl.BlockSpec(memory_space=pl.ANY)


def _call(body, *, name, grid, in_specs, out_specs, out_shape, operands, scratch_shapes=(), parallel=False, comm=None):
    n_axes = len(grid)
    if comm is None:
        outs = pl.pallas_call(body, name=name, grid=grid, out_shape=list(out_shape), in_specs=list(in_specs),
                              out_specs=list(out_specs), scratch_shapes=list(scratch_shapes),
                              compiler_params=_params(n_axes, parallel))(*operands)
        return list(outs), None
    n_in, n_out, n_scr, n_c = len(in_specs), len(out_specs), len(scratch_shapes), len(comm.operands)
    total = math.prod(grid)

    def hosted(*refs):
        ins, c_src = refs[:n_in], refs[n_in:n_in + n_c]
        outs, c_dst = refs[n_in + n_c:n_in + n_c + n_out], refs[n_in + n_c + n_out:n_in + 2 * n_c + n_out]
        scr, sems = refs[n_in + 2 * n_c + n_out:n_in + 2 * n_c + n_out + n_scr], refs[n_in + 2 * n_c + n_out + n_scr:]
        step = pl.program_id(0)
        for axis in range(1, n_axes):
            step = step * grid[axis] + pl.program_id(axis)

        @pl.when(step == 0)
        def _():
            comm.start(c_src, c_dst, *sems)

        @pl.when(step == max(total - FORWARD_STEPS, 0))
        def _():
            comm.forward(c_src, c_dst, *sems)

        body(*ins, *outs, *scr)

        @pl.when(step == total - 1)
        def _():
            comm.finish(c_src, c_dst, *sems)

    res = pl.pallas_call(hosted, name=name, grid=grid, out_shape=list(out_shape) + comm.out_shape,
                         in_specs=list(in_specs) + [_ANY] * n_c, out_specs=list(out_specs) + [_ANY] * n_c,
                         scratch_shapes=list(scratch_shapes) + comm.scratch,
                         compiler_params=_params(n_axes, False))(*operands, *comm.operands)
    return list(res[:n_out]), list(res[n_out:])


def _all_gather_small(v, name):
    rows = v.shape[0]

    def body(v_ref, all_ref, sum_ref, send_sems, recv_sems):
        x, y, c, me = _my_position()
        all_ref[me] = v_ref[...]
        copies = []
        for p in range(1, N_DEV):
            cp = pltpu.make_async_remote_copy(
                src_ref=v_ref, dst_ref=all_ref.at[me], send_sem=send_sems.at[p - 1], recv_sem=recv_sems.at[p - 1],
                device_id=_peer(x, y, c, p), device_id_type=MESH)
            cp.start()
            copies.append(cp)
        for cp in copies:
            cp.wait()
        acc = all_ref[0]
        for d in range(1, N_DEV):
            acc = acc + all_ref[d]
        sum_ref[...] = acc

    return pl.pallas_call(
        body, name=name,
        out_shape=[jax.ShapeDtypeStruct((N_DEV, rows, LANES), F32), jax.ShapeDtypeStruct((rows, LANES), F32)],
        in_specs=[pl.BlockSpec(memory_space=pltpu.VMEM)],
        out_specs=[pl.BlockSpec(memory_space=pltpu.VMEM)] * 2,
        scratch_shapes=[pltpu.SemaphoreType.DMA((N_DEV - 1,)), pltpu.SemaphoreType.DMA((N_DEV - 1,))],
        compiler_params=pltpu.CompilerParams(vmem_limit_bytes=VMEM_LIMIT),
    )(v)


def _sum_gathered(gathered, name):
    rows = gathered.shape[1]

    def body(g_ref, o_ref):
        acc = g_ref[0]
        for d in range(1, N_DEV):
            acc = acc + g_ref[d]
        o_ref[...] = acc

    return pl.pallas_call(
        body, name=name, out_shape=jax.ShapeDtypeStruct((rows, LANES), F32),
        in_specs=[pl.BlockSpec(memory_space=pltpu.VMEM)], out_specs=pl.BlockSpec(memory_space=pltpu.VMEM),
        compiler_params=pltpu.CompilerParams(vmem_limit_bytes=VMEM_LIMIT),
    )(gathered)


def _pack_small(parts):
    flat = jnp.concatenate([p.reshape(-1).astype(F32) for p in parts])
    total = flat.shape[0]
    padded = -(-total // (8 * LANES)) * (8 * LANES)
    flat = jnp.pad(flat, (0, padded - total))
    return flat.reshape(padded // LANES, LANES)


def _unpack_small(packed, shapes, lead=()):
    flat = packed.reshape(lead + (-1,))
    out, off = [], 0
    for shp in shapes:
        size = math.prod(shp)
        out.append(flat[..., off:off + size].reshape(lead + tuple(shp)))
        off += size
    return out


def _prologue(c_rows, convw_rows, ada_w, ada_b_cols, gather):
    depth, _, cols = ada_w.shape
    n_c = len(gather.operands)
    sub = 8

    def body(c_ref, cw_ref, b_ref, w_hbm, *rest):
        g_src, (c_all_ref, cw_all_ref, ada_ref), g_dst = rest[:n_c], rest[n_c:n_c + 3], rest[n_c + 3:2 * n_c + 3]
        ada_local, w_ref, w_sem, send_sems, recv_sems = rest[2 * n_c + 3:2 * n_c + 8]
        g_sems = rest[2 * n_c + 8:]
        x, y, c, me = _my_position()
        gather.start(g_src, g_dst, *g_sems)
        load_w = pltpu.make_async_copy(w_hbm, w_ref, w_sem)
        load_w.start()

        def to_all(k, src_ref, dst_ref):
            copies = []
            for p in range(1, N_DEV):
                copies.append(pltpu.make_async_remote_copy(
                    src_ref=src_ref, dst_ref=dst_ref.at[me], send_sem=send_sems.at[k, p - 1], recv_sem=recv_sems.at[k, p - 1],
                    device_id=_peer(x, y, c, p), device_id_type=MESH))
            return copies

        first = to_all(0, c_ref, c_all_ref) + to_all(1, cw_ref, cw_all_ref)
        c_all_ref[me] = c_ref[...]
        cw_all_ref[me] = cw_ref[...]
        for cp in first:
            cp.start()
        for cp in first:
            cp.wait()
        cv = c_all_ref[...].reshape(N_DEV * sub, D_MODEL)
        act = (cv * _sigmoid(cv)).astype(BF16)
        load_w.wait()
        for l in range(depth):
            ada_local[l] = _dot(act, w_ref[l].astype(BF16)) + b_ref[l]
        ada_ref[me] = ada_local[:, pl.ds(pl.multiple_of(me * sub, sub), sub), :]
        rows_out = []
        for p in range(1, N_DEV):
            px, py, pc = _peer(x, y, c, p)
            rows = pl.ds(pl.multiple_of((4 * px + 2 * py + pc) * sub, sub), sub)
            rows_out.append(pltpu.make_async_remote_copy(
                src_ref=ada_local.at[:, rows, :], dst_ref=ada_ref.at[me], send_sem=send_sems.at[2, p - 1],
                recv_sem=recv_sems.at[2, p - 1], device_id=(px, py, pc), device_id_type=MESH))
        for cp in rows_out:
            cp.start()
        for cp in rows_out:
            cp.wait()
        gather.forward(g_src, g_dst, *g_sems)
        gather.finish(g_src, g_dst, *g_sems)

    vmem = pl.BlockSpec(memory_space=pltpu.VMEM)
    outs = pl.pallas_call(
        body, name="prologue",
        out_shape=[jax.ShapeDtypeStruct((N_DEV, sub, D_MODEL), F32), jax.ShapeDtypeStruct((N_DEV, sub, LANES), F32),
                   jax.ShapeDtypeStruct((N_DEV, depth, sub, cols), F32)] + gather.out_shape,
        in_specs=[vmem] * 3 + [_ANY] * (1 + n_c), out_specs=[vmem] * 3 + [_ANY] * n_c,
        scratch_shapes=[pltpu.VMEM((depth, N_DEV * sub, cols), F32), pltpu.VMEM(ada_w.shape, F32), pltpu.SemaphoreType.DMA,
                        pltpu.SemaphoreType.DMA((3, N_DEV - 1)), pltpu.SemaphoreType.DMA((3, N_DEV - 1))] + gather.scratch,
        compiler_params=pltpu.CompilerParams(vmem_limit_bytes=VMEM_LIMIT),
    )(c_rows, convw_rows, ada_b_cols, ada_w, *gather.operands)
    return outs[0], outs[1], outs[2], list(outs[3:])


def _ada_backward(c_all, d_ada_cols, d_ada_all):
    nb = c_all.shape[0]
    cols = d_ada_cols.shape[2]
    full = d_ada_all.shape[2]

    def body(c_ref, dc_ref, da_ref, gw_ref, gb_ref):
        cv = c_ref[...]
        act = (cv * _sigmoid(cv)).astype(BF16)
        gw_ref[0] = _dot_tn(act, dc_ref[0].astype(BF16))
        gb_ref[0] = jnp.sum(da_ref[0], axis=0, keepdims=True)

    return pl.pallas_call(
        body, name="ada_backward", grid=(DEPTH,),
        out_shape=[jax.ShapeDtypeStruct((DEPTH, D_MODEL, cols), F32), jax.ShapeDtypeStruct((DEPTH, 1, full), F32)],
        in_specs=[pl.BlockSpec((nb, D_MODEL), lambda l: (0, 0)),
                  pl.BlockSpec((1, nb, cols), lambda l: (l, 0, 0)),
                  pl.BlockSpec((1, nb, full), lambda l: (l, 0, 0))],
        out_specs=[pl.BlockSpec((1, D_MODEL, cols), lambda l: (l, 0, 0)),
                   pl.BlockSpec((1, 1, full), lambda l: (l, 0, 0))],
        compiler_params=_params(),
    )(c_all, d_ada_cols, d_ada_all)


def _rms(xv):
    return lax.rsqrt(jnp.mean(xv * xv, axis=-1, keepdims=True) + EPS)


def _normmod_matmul(x, gnorm, scale1p, shift, w_t, seq, name, comm=None):
    tokens, width = x.shape[0], w_t.shape[0] // 2
    tm = _tile_rows(seq)
    per_seq = seq // tm
    n_chunks = width // MXU_N

    def body(x_ref, g_ref, sc_ref, sh_ref, w_ref, h_ref, act_ref, silu_ref, dact_ref):
        xv = x_ref[...]
        h = (xv * _rms(xv) * g_ref[...]) * sc_ref[0] + sh_ref[0]
        h_ref[...] = h.astype(BF16)
        for ck in range(n_chunks):
            cs = slice(ck * MXU_N, (ck + 1) * MXU_N)
            g = _dot_nt(h_ref[...], w_ref[cs, :])
            u = _dot_nt(h_ref[...], w_ref[width + ck * MXU_N:width + (ck + 1) * MXU_N, :])
            sig = _sigmoid(g)
            silu = g * sig
            act_ref[:, cs] = (silu * u).astype(BF16)
            silu_ref[:, cs] = silu.astype(BF16)
            dact_ref[:, cs] = (u * (sig + silu * (1.0 - sig))).astype(BF16)

    per_batch = pl.BlockSpec((1, 1, D_MODEL), lambda i: (i // per_seq, 0, 0))
    outs, got = _call(
        body, name=name, grid=(tokens // tm,),
        out_shape=[jax.ShapeDtypeStruct((tokens, D_MODEL), BF16)] + [jax.ShapeDtypeStruct((tokens, width), BF16)] * 3,
        in_specs=[pl.BlockSpec((tm, D_MODEL), lambda i: (i, 0)), _resident((1, D_MODEL)), per_batch, per_batch,
                  _resident(w_t.shape)],
        out_specs=[pl.BlockSpec((tm, D_MODEL), lambda i: (i, 0))] + [pl.BlockSpec((tm, width), lambda i: (i, 0))] * 3,
        operands=(x, gnorm, scale1p, shift, w_t), parallel=True, comm=comm)
    return (*outs, got)


def _matmul_residual(src, w, x, gate, scale, seq, name, comm=None):
    tokens, k_dim = x.shape[0], w.shape[0]
    tm = _tile_rows(seq)
    per_seq = seq // tm

    def body(s_ref, w_ref, x_ref, gate_ref, xo_ref, f_ref):
        f = _dot(s_ref[...], w_ref[...])
        f_ref[...] = f.astype(BF16)
        xo_ref[...] = x_ref[...] + (scale * gate_ref[0]) * f

    (x_out, f), got = _call(
        body, name=name, grid=(tokens // tm,),
        out_shape=[jax.ShapeDtypeStruct((tokens, D_MODEL), F32), jax.ShapeDtypeStruct((tokens, D_MODEL), BF16)],
        in_specs=[pl.BlockSpec((tm, k_dim), lambda i: (i, 0)), _resident(w.shape),
                  pl.BlockSpec((tm, D_MODEL), lambda i: (i, 0)),
                  pl.BlockSpec((1, 1, D_MODEL), lambda i: (i // per_seq, 0, 0))],
        out_specs=[pl.BlockSpec((tm, D_MODEL), lambda i: (i, 0))] * 2,
        operands=(src, w, x, gate), parallel=True, comm=comm)
    return x_out, f, got


def _loss_tile(xv, target, gn):
    r = _rms(xv)
    xn = xv * r
    err = xn * gn - target
    loss = (0.5 / D_MODEL) * jnp.sum(err * err, axis=0, keepdims=True)
    dyv = err * (1.0 / D_MODEL)
    dg = jnp.sum(dyv * xn, axis=0, keepdims=True)
    dxn = dyv * gn
    dx = r * (dxn - xn * jnp.mean(dxn * xn, axis=-1, keepdims=True))
    return loss, dx, dg


def _ffn_forward(x, gnorm, scale1p, shift, w_gu_t, w_down, gate, scale, seq, name, loss_head=None, comm=None):
    tokens, width = x.shape[0], w_down.shape[0]
    tm = _tile_rows(seq)
    per_seq = seq // tm
    n_chunks = width // MXU_N

    def body(x_ref, g_ref, sc_ref, sh_ref, wgu_ref, wd_ref, gate_ref, *rest):
        if loss_head:
            t_ref, gf_ref, h_ref, act_ref, silu_ref, dact_ref, xo_ref, f_ref, dgf_ref, loss_ref = rest
        else:
            h_ref, act_ref, silu_ref, dact_ref, xo_ref, f_ref = rest
        xv = x_ref[...]
        h = (xv * _rms(xv) * g_ref[...]) * sc_ref[0] + sh_ref[0]
        h_ref[...] = h.astype(BF16)
        for ck in range(n_chunks):
            cs = slice(ck * MXU_N, (ck + 1) * MXU_N)
            g = _dot_nt(h_ref[...], wgu_ref[cs, :])
            u = _dot_nt(h_ref[...], wgu_ref[width + ck * MXU_N:width + (ck + 1) * MXU_N, :])
            sig = _sigmoid(g)
            silu = g * sig
            act_ref[:, cs] = (silu * u).astype(BF16)
            silu_ref[:, cs] = silu.astype(BF16)
            dact_ref[:, cs] = (u * (sig + silu * (1.0 - sig))).astype(BF16)
        f = _dot(act_ref[...], wd_ref[...])
        f_ref[...] = f.astype(BF16)
        x_out = xv + (scale * gate_ref[0]) * f
        if loss_head:
            i = pl.program_id(0)
            loss, dx, dg = _loss_tile(x_out, t_ref[...], gf_ref[...])
            xo_ref[...] = dx

            @pl.when(i == 0)
            def _():
                dgf_ref[...] = dg
                loss_ref[...] = loss

            @pl.when(i != 0)
            def _():
                dgf_ref[...] = dgf_ref[...] + dg
                loss_ref[...] = loss_ref[...] + loss
        else:
            xo_ref[...] = x_out

    row = lambda i: (i, 0)
    per_batch = pl.BlockSpec((1, 1, D_MODEL), lambda i: (i // per_seq, 0, 0))
    tile = lambda cols: pl.BlockSpec((tm, cols), row)
    wide = jax.ShapeDtypeStruct((tokens, width), BF16)
    fixed = pl.BlockSpec((1, D_MODEL), lambda i: (0, 0))
    vec = jax.ShapeDtypeStruct((1, D_MODEL), F32)
    outs, got = _call(
        body, name=name, grid=(tokens // tm,),
        out_shape=[jax.ShapeDtypeStruct((tokens, D_MODEL), BF16), wide, wide, wide,
                   jax.ShapeDtypeStruct((tokens, D_MODEL), F32), jax.ShapeDtypeStruct((tokens, D_MODEL), BF16)]
        + ([vec, vec] if loss_head else []),
        in_specs=[tile(D_MODEL), _resident((1, D_MODEL)), per_batch, per_batch, _resident(w_gu_t.shape),
                  _resident(w_down.shape), per_batch] + ([tile(D_MODEL), _resident((1, D_MODEL))] if loss_head else []),
        out_specs=[tile(D_MODEL), tile(width), tile(width), tile(width), tile(D_MODEL), tile(D_MODEL)]
        + ([fixed, fixed] if loss_head else []),
        operands=(x, gnorm, scale1p, shift, w_gu_t, w_down, gate) + (tuple(loss_head) if loss_head else ()),
        parallel=not loss_head, comm=comm)
    return (*outs, got)


def _residual_backward(dy, gate, f, w, scale, silu, dact, seq, name, comm=None):
    tokens, k_dim = dy.shape[0], w.shape[0]
    batch = tokens // seq
    tm = _tile_rows(seq)
    per_seq = seq // tm
    n_chunks = k_dim // MXU_N

    def body(dy_ref, gate_ref, f_ref, silu_ref, dact_ref, w_ref, df_ref, dgate_ref, dgu_ref):
        i = pl.program_id(0)
        dy_v = dy_ref[...]
        df_ref[...] = ((scale * gate_ref[0]) * dy_v).astype(BF16)
        part = scale * jnp.sum(dy_v * f_ref[...].astype(F32), axis=0, keepdims=True)
        for ck in range(n_chunks):
            cs = slice(ck * MXU_N, (ck + 1) * MXU_N)
            cu = slice(k_dim + ck * MXU_N, k_dim + (ck + 1) * MXU_N)
            da = _dot_nt(df_ref[...], w_ref[cs, :])
            dgu_ref[:, cs] = (da * dact_ref[:, cs].astype(F32)).astype(BF16)
            dgu_ref[:, cu] = (da * silu_ref[:, cs].astype(F32)).astype(BF16)

        @pl.when(i % per_seq == 0)
        def _():
            dgate_ref[0] = part

        @pl.when(i % per_seq != 0)
        def _():
            dgate_ref[0] = dgate_ref[0] + part

    row = lambda i: (i, 0)
    per_batch = pl.BlockSpec((1, 1, D_MODEL), lambda i: (i // per_seq, 0, 0))
    tile = lambda cols: pl.BlockSpec((tm, cols), row)
    outs, got = _call(
        body, name=name, grid=(tokens // tm,),
        out_shape=[jax.ShapeDtypeStruct((tokens, D_MODEL), BF16), jax.ShapeDtypeStruct((batch, 1, D_MODEL), F32),
                   jax.ShapeDtypeStruct((tokens, 2 * k_dim), BF16)],
        in_specs=[tile(D_MODEL), per_batch, tile(D_MODEL), tile(k_dim), tile(k_dim), _resident(w.shape)],
        out_specs=[tile(D_MODEL), per_batch, tile(2 * k_dim)],
        operands=(dy, gate, f, silu, dact, w), comm=comm)
    return (*outs, got)


def _matmul_normmod_backward(dsrc, w_t, x, dy, gnorm, scale1p, seq, name, comm=None):
    tokens, k_dim = dsrc.shape
    batch = tokens // seq
    tm = _tile_rows(seq)
    per_seq = seq // tm

    def body(ds_ref, w_ref, x_ref, dy_ref, g_ref, sc_ref, dx_ref, dsh_ref, dsc_ref, dg_ref):
        i = pl.program_id(0)
        dh = _dot(ds_ref[...], w_ref[...])
        xv = x_ref[...]
        r = _rms(xv)
        xn = xv * r
        gn = g_ref[...]
        dsh = jnp.sum(dh, axis=0, keepdims=True)
        dsc = jnp.sum(dh * (xn * gn), axis=0, keepdims=True)
        dhn = dh * sc_ref[0]
        dg = jnp.sum(dhn * xn, axis=0, keepdims=True)
        dxn = dhn * gn
        dx_ref[...] = dy_ref[...] + r * (dxn - xn * jnp.mean(dxn * xn, axis=-1, keepdims=True))

        @pl.when(i % per_seq == 0)
        def _():
            dsh_ref[0] = dsh
            dsc_ref[0] = dsc

        @pl.when(i % per_seq != 0)
        def _():
            dsh_ref[0] = dsh_ref[0] + dsh
            dsc_ref[0] = dsc_ref[0] + dsc

        @pl.when(i == 0)
        def _():
            dg_ref[...] = dg

        @pl.when(i != 0)
        def _():
            dg_ref[...] = dg_ref[...] + dg

    row = lambda i: (i, 0)
    per_batch = pl.BlockSpec((1, 1, D_MODEL), lambda i: (i // per_seq, 0, 0))
    outs, got = _call(
        body, name=name, grid=(tokens // tm,),
        out_shape=[jax.ShapeDtypeStruct((tokens, D_MODEL), F32), jax.ShapeDtypeStruct((batch, 1, D_MODEL), F32),
                   jax.ShapeDtypeStruct((batch, 1, D_MODEL), F32), jax.ShapeDtypeStruct((1, D_MODEL), F32)],
        in_specs=[pl.BlockSpec((tm, k_dim), row), _resident(w_t.shape), pl.BlockSpec((tm, D_MODEL), row),
                  pl.BlockSpec((tm, D_MODEL), row), _resident((1, D_MODEL)), per_batch],
        out_specs=[pl.BlockSpec((tm, D_MODEL), row), per_batch, per_batch, pl.BlockSpec((1, D_MODEL), lambda i: (0, 0))],
        operands=(dsrc, w_t, x, dy, gnorm, scale1p), comm=comm)
    return (*outs, got)


def _ffn_backward(dy, gate, f, silu, dact, w_down, w_gu_t, x, gnorm, scale1p, scale, seq, name, comm=None):
    tokens, k_dim = dy.shape[0], w_down.shape[0]
    batch = tokens // seq
    tm = min(256, seq)
    per_seq = seq // tm
    n_chunks = k_dim // MXU_N

    def body(dy_ref, gate_ref, f_ref, silu_ref, dact_ref, wd_ref, wgu_ref, x_ref, g_ref, sc_ref,
             df_ref, dgate_ref, dgu_ref, dx_ref, dsh_ref, dsc_ref, dg_ref):
        i = pl.program_id(0)
        dy_v = dy_ref[...]
        df_ref[...] = ((scale * gate_ref[0]) * dy_v).astype(BF16)
        dgate = scale * jnp.sum(dy_v * f_ref[...].astype(F32), axis=0, keepdims=True)
        for ck in range(n_chunks):
            cs = slice(ck * MXU_N, (ck + 1) * MXU_N)
            cu = slice(k_dim + ck * MXU_N, k_dim + (ck + 1) * MXU_N)
            da = _dot_nt(df_ref[...], wd_ref[cs, :])
            dgu_ref[:, cs] = (da * dact_ref[:, cs].astype(F32)).astype(BF16)
            dgu_ref[:, cu] = (da * silu_ref[:, cs].astype(F32)).astype(BF16)
        dh = _dot(dgu_ref[...], wgu_ref[...])
        xv = x_ref[...]
        r = _rms(xv)
        xn = xv * r
        gn = g_ref[...]
        dsh = jnp.sum(dh, axis=0, keepdims=True)
        dsc = jnp.sum(dh * (xn * gn), axis=0, keepdims=True)
        dhn = dh * sc_ref[0]
        dg = jnp.sum(dhn * xn, axis=0, keepdims=True)
        dxn = dhn * gn
        dx_ref[...] = dy_v + r * (dxn - xn * jnp.mean(dxn * xn, axis=-1, keepdims=True))

        @pl.when(i % per_seq == 0)
        def _():
            dgate_ref[0] = dgate
            dsh_ref[0] = dsh
            dsc_ref[0] = dsc

        @pl.when(i % per_seq != 0)
        def _():
            dgate_ref[0] = dgate_ref[0] + dgate
            dsh_ref[0] = dsh_ref[0] + dsh
            dsc_ref[0] = dsc_ref[0] + dsc

        @pl.when(i == 0)
        def _():
            dg_ref[...] = dg

        @pl.when(i != 0)
        def _():
            dg_ref[...] = dg_ref[...] + dg

    row = lambda i: (i, 0)
    per_batch = pl.BlockSpec((1, 1, D_MODEL), lambda i: (i // per_seq, 0, 0))
    tile = lambda width: pl.BlockSpec((tm, width), row)
    vec = jax.ShapeDtypeStruct((batch, 1, D_MODEL), F32)
    outs, got = _call(
        body, name=name, grid=(tokens // tm,),
        out_shape=[jax.ShapeDtypeStruct((tokens, D_MODEL), BF16), vec, jax.ShapeDtypeStruct((tokens, 2 * k_dim), BF16),
                   jax.ShapeDtypeStruct((tokens, D_MODEL), F32), vec, vec, jax.ShapeDtypeStruct((1, D_MODEL), F32)],
        in_specs=[tile(D_MODEL), per_batch, tile(D_MODEL), tile(k_dim), tile(k_dim), _resident(w_down.shape),
                  _resident(w_gu_t.shape), tile(D_MODEL), _resident((1, D_MODEL)), per_batch],
        out_specs=[tile(D_MODEL), per_batch, tile(2 * k_dim), tile(D_MODEL), per_batch, per_batch,
                   pl.BlockSpec((1, D_MODEL), lambda i: (0, 0))],
        operands=(dy, gate, f, silu, dact, w_down, w_gu_t, x, gnorm, scale1p), comm=comm)
    return (*outs, got)


def _weight_grad(a, b, seq, name, comm=None):
    tokens, n_out = a.shape
    tn = MXU_N

    def body(a_ref, b_ref, o_ref):
        o_ref[...] = _dot_tn(a_ref[...], b_ref[...]).astype(BF16)

    (out,), got = _call(
        body, name=name, grid=(n_out // tn,),
        out_shape=[jax.ShapeDtypeStruct((n_out, D_MODEL), BF16)],
        in_specs=[pl.BlockSpec((tokens, tn), lambda j: (0, j)), _resident((tokens, D_MODEL))],
        out_specs=[pl.BlockSpec((tn, D_MODEL), lambda j: (j, 0))],
        operands=(a, b), comm=comm)
    return out, got


def _group_mean(v, bd):
    hi = v.astype(BF16)
    lo = (v - hi.astype(F32)).astype(BF16)
    blocks = [slice(k * MXU_N, (k + 1) * MXU_N) for k in range(v.shape[1] // MXU_N)]
    return jnp.concatenate([_dot(hi[:, b], bd) + _dot(lo[:, b], bd) for b in blocks], axis=1)


def _sgu_forward(pm_ref, wm_ref, bias_ref, lng_ref, lnb_ref, bd_ref, mixed_scr, n_sub):
    ua = pm_ref[:, 0:D_A].astype(F32)
    va = pm_ref[:, D_A:2 * D_A].astype(F32)
    u_act = _gelu(ua)
    v_act = _gelu(va)
    bd = bd_ref[...]
    vc = v_act - _group_mean(v_act, bd)
    rstd = lax.rsqrt(_group_mean(vc * vc, bd) + EPS)
    vhat = vc * rstd
    vln = vhat * lng_ref[...] + lnb_ref[...]
    left = lax.broadcasted_iota(jnp.int32, (CHUNK, LANES), 1) < HEAD_DIM
    for q in range(n_sub):
        rows = slice(q * CHUNK, (q + 1) * CHUNK)
        for p in range(N_HEADS // 2):
            cols = slice(p * LANES, (p + 1) * LANES)
            vp = vln[rows, cols]
            stacked = jnp.concatenate([jnp.where(left, vp, 0.0), jnp.where(left, 0.0, vp)], axis=0).astype(BF16)
            mixed_scr[rows, cols] = _dot(wm_ref[p], stacked) + bias_ref[:, cols]
    return ua, va, u_act, vhat, rstd, vln


def _halo_specs(tm, tokens, width):
    prev = pl.BlockSpec((HALO, width), lambda i: (jnp.maximum(i * (tm // HALO) - 1, 0), 0))
    nxt = pl.BlockSpec((HALO, width), lambda i: (jnp.minimum((i + 1) * (tm // HALO), tokens // HALO - 1), 0))
    return prev, nxt


def _mixer_forward(x, gnorm, scale1p, shift, w_in_t, gate, w_out, wm, bias_full, lng, lnb, convw, og, bd, seq, name, comm=None):
    tokens = x.shape[0]
    tm = _tile_rows(seq)
    per_seq = seq // tm
    n_sub = tm // CHUNK

    def body(x_ref, xp_ref, g_ref, sc_ref, sh_ref, win_ref, gate_ref, wo_ref, wm_ref, bias_ref, lng_ref, lnb_ref, cw_ref,
             og_ref, bd_ref, h_ref, pm_ref, y_ref, xo_ref, o_ref, mixed_scr):
        i = pl.program_id(0)
        first = (i % per_seq) == 0
        xv = x_ref[...]
        h_ref[...] = ((xv * _rms(xv) * g_ref[...]) * sc_ref[0] + sh_ref[0]).astype(BF16)
        for ck in range(D_PROJ // MXU_N):
            cs = slice(ck * MXU_N, (ck + 1) * MXU_N)
            pm_ref[:, cs] = _dot_nt(h_ref[...], win_ref[cs, :]).astype(BF16)
        xp = xp_ref[...]
        hp = ((xp * _rms(xp) * g_ref[...]) * sc_ref[0] + sh_ref[0]).astype(BF16)
        gates_prev = _dot_nt(hp, win_ref[3 * D_A:5 * D_A, :]).astype(BF16).astype(F32)

        _, _, u_act, _, _, _ = _sgu_forward(pm_ref, wm_ref, bias_ref, lng_ref, lnb_ref, bd_ref, mixed_scr, n_sub)
        ya = u_act * mixed_scr[...]
        y_ref[:, 0:D_A] = (ya * _rms(ya) * og_ref[:, 0:D_A]).astype(BF16)

        bg = pm_ref[:, 2 * D_A:3 * D_A].astype(F32)
        z = pm_ref[:, 3 * D_A:4 * D_A].astype(F32) * pm_ref[:, 4 * D_A:5 * D_A].astype(F32)
        zp = jnp.where(first, 0.0, gates_prev[:, 0:D_A] * gates_prev[:, D_A:2 * D_A])
        zext = jnp.concatenate([zp, z], axis=0)
        z1 = pltpu.roll(zext, 1, 0)[HALO:]
        z2 = pltpu.roll(zext, 2, 0)[HALO:]
        conv = cw_ref[0:1, :] * z2 + cw_ref[1:2, :] * z1 + cw_ref[2:3, :] * z
        yb = bg * conv
        y_ref[:, D_A:2 * D_A] = (yb * _rms(yb) * og_ref[:, D_A:2 * D_A]).astype(BF16)

        f = _dot(y_ref[...], wo_ref[...])
        o_ref[...] = f.astype(BF16)
        xo_ref[...] = xv + gate_ref[0] * f

    prev, _ = _halo_specs(tm, tokens, D_MODEL)
    tile = pl.BlockSpec((tm, D_MODEL), lambda i: (i, 0))
    per_batch = pl.BlockSpec((1, 1, D_MODEL), lambda i: (i // per_seq, 0, 0))
    bf = lambda cols: jax.ShapeDtypeStruct((tokens, cols), BF16)
    outs, got = _call(
        body, name=name, grid=(tokens // tm,),
        out_shape=[bf(D_MODEL), bf(D_PROJ), bf(D_MODEL), jax.ShapeDtypeStruct((tokens, D_MODEL), F32), bf(D_MODEL)],
        in_specs=[tile, prev, _resident((1, D_MODEL)), per_batch, per_batch, _resident(w_in_t.shape), per_batch,
                  _resident(w_out.shape), _resident(wm.shape), _resident(bias_full.shape), _resident(lng.shape),
                  _resident(lnb.shape), _resident(convw.shape), _resident(og.shape), _resident(bd.shape)],
        out_specs=[tile, pl.BlockSpec((tm, D_PROJ), lambda i: (i, 0)), tile, tile, tile],
        scratch_shapes=[pltpu.VMEM((tm, D_A), F32)],
        operands=(x, x, gnorm, scale1p, shift, w_in_t, gate, w_out, wm, bias_full, lng, lnb, convw, og, bd),
        parallel=True, comm=comm)
    return (*outs, got)


def _mixer_backward(proj, dx, gate, o, w_out, x, gnorm, scale1p, w_in_t, wm, bias_full, lng, lnb, convw, og, bd, wm_rows,
                    causal, seq, name, comm=None):
    tokens = proj.shape[0]
    batch = tokens // seq
    tm = _tile_rows(seq)
    per_seq = seq // tm
    n_sub = tm // CHUNK
    ext = tm + 2 * HALO

    def body(pm_ref, pp_ref, pn_ref, dx_ref, dxn_ref, gate_ref, o_ref, wo_ref, x_ref, g_ref, sc_ref, win_ref, wm_ref, bias_ref,
             lng_ref, lnb_ref, cw_ref, og_ref, bd_ref, wmr_ref, causal_ref, do_ref, dgate_ref, dp_ref, dxo_ref, dsh_ref, dsc_ref,
             dgn_ref,
             dog_ref, dcw_ref, dlng_ref, dlnb_ref, dbias_ref, dwm_ref, mixed_scr, dvln_scr, dy_scr):
        i = pl.program_id(0)
        first = (i % per_seq) == 0
        last = (i % per_seq) == per_seq - 1

        @pl.when(i == 0)
        def _():
            dog_ref[...] = jnp.zeros_like(dog_ref)
            dcw_ref[...] = jnp.zeros_like(dcw_ref)
            dlng_ref[...] = jnp.zeros_like(dlng_ref)
            dlnb_ref[...] = jnp.zeros_like(dlnb_ref)
            dbias_ref[...] = jnp.zeros_like(dbias_ref)
            dwm_ref[...] = jnp.zeros_like(dwm_ref)

        dx_v = dx_ref[...]
        do_ref[...] = (gate_ref[0] * dx_v).astype(BF16)
        dgate = jnp.sum(dx_v * o_ref[...].astype(F32), axis=0, keepdims=True)
        dy_scr[...] = _dot_nt(do_ref[...], wo_ref[...])
        dyn_conv = _dot_nt((gate_ref[0] * dxn_ref[...]).astype(BF16), wo_ref[D_A:2 * D_A, :])

        ua, va, u_act, vhat, rstd, vln = _sgu_forward(pm_ref, wm_ref, bias_ref, lng_ref, lnb_ref, bd_ref, mixed_scr, n_sub)
        mixed = mixed_scr[...]
        ya = u_act * mixed
        ra = _rms(ya)
        yhat = ya * ra
        dya_in = dy_scr[:, 0:D_A]
        dog_ref[:, 0:D_A] = dog_ref[:, 0:D_A] + jnp.sum(dya_in * yhat, axis=0, keepdims=True)
        dyh = dya_in * og_ref[:, 0:D_A]
        dya = ra * (dyh - yhat * jnp.mean(dyh * yhat, axis=-1, keepdims=True))
        d_u = dya * mixed
        d_mixed = dya * u_act
        left = lax.broadcasted_iota(jnp.int32, (CHUNK, LANES), 1) < HEAD_DIM
        dbias = jnp.zeros((CHUNK, D_A), F32)
        for q in range(n_sub):
            rows = slice(q * CHUNK, (q + 1) * CHUNK)
            dbias = dbias + d_mixed[rows, :]
            for p in range(N_HEADS // 2):
                cols = slice(p * LANES, (p + 1) * LANES)
                dm = d_mixed[rows, cols]
                stacked = jnp.concatenate([jnp.where(left, dm, 0.0), jnp.where(left, 0.0, dm)], axis=0).astype(BF16)
                dw = _dot_nt(stacked, vln[rows, cols].astype(BF16))
                dwm_ref[2 * p] = dwm_ref[2 * p] + causal_ref[...] * dw[0:CHUNK]
                dwm_ref[2 * p + 1] = dwm_ref[2 * p + 1] + causal_ref[...] * dw[CHUNK:2 * CHUNK]
                dvln_scr[rows, cols] = _dot_tn(wmr_ref[p], stacked)
        dbias_ref[...] = dbias_ref[...] + dbias
        dvln = dvln_scr[...]
        dlng_ref[...] = dlng_ref[...] + jnp.sum(dvln * vhat, axis=0, keepdims=True)
        dlnb_ref[...] = dlnb_ref[...] + jnp.sum(dvln, axis=0, keepdims=True)
        dvh = dvln * lng_ref[...]
        bd = bd_ref[...]
        d_v = rstd * (dvh - _group_mean(dvh, bd) - vhat * _group_mean(dvh * vhat, bd))
        dp_ref[:, 0:D_A] = (d_u * _gelu_grad(ua)).astype(BF16)
        dp_ref[:, D_A:2 * D_A] = (d_v * _gelu_grad(va)).astype(BF16)
        dh_a = _dot(dp_ref[:, 0:2 * D_A], win_ref[0:2 * D_A, :])

        def ext_cols(lo):
            cs = slice(lo, lo + D_A)
            return jnp.concatenate([pp_ref[:, cs], pm_ref[:, cs], pn_ref[:, cs]], axis=0).astype(F32)

        bg, cg, xb = ext_cols(2 * D_A), ext_cols(3 * D_A), ext_cols(4 * D_A)
        row = lax.broadcasted_iota(jnp.int32, (ext, D_A), 0)
        z = jnp.where(jnp.logical_and(first, row < HALO), 0.0, cg * xb)
        z1 = pltpu.roll(z, 1, 0)
        z2 = pltpu.roll(z, 2, 0)
        w0, w1, w2 = cw_ref[0:1, :], cw_ref[1:2, :], cw_ref[2:3, :]
        conv = w0 * z2 + w1 * z1 + w2 * z
        yb = bg * conv
        rb = _rms(yb)
        yhb = yb * rb
        dyn = jnp.where(last, 0.0, dyn_conv)
        dyb_in = jnp.concatenate([jnp.zeros((HALO, D_A), F32), dy_scr[:, D_A:2 * D_A], dyn], axis=0)
        dyhb = dyb_in * og_ref[:, D_A:2 * D_A]
        dyb = rb * (dyhb - yhb * jnp.mean(dyhb * yhb, axis=-1, keepdims=True))
        d_conv = dyb * bg
        dz = w2 * d_conv + w1 * pltpu.roll(d_conv, ext - 1, 0) + w0 * pltpu.roll(d_conv, ext - 2, 0)
        main = slice(HALO, HALO + tm)
        dp_ref[:, 2 * D_A:3 * D_A] = (dyb * conv)[main].astype(BF16)
        dp_ref[:, 3 * D_A:4 * D_A] = (dz * xb)[main].astype(BF16)
        dp_ref[:, 4 * D_A:5 * D_A] = (dz * cg)[main].astype(BF16)
        dog_ref[:, D_A:2 * D_A] = dog_ref[:, D_A:2 * D_A] + jnp.sum((dyb_in * yhb)[main], axis=0, keepdims=True)
        dcm = d_conv[main]
        dcw_ref[0:1, :] = dcw_ref[0:1, :] + jnp.sum(dcm * z2[main], axis=0, keepdims=True)
        dcw_ref[1:2, :] = dcw_ref[1:2, :] + jnp.sum(dcm * z1[main], axis=0, keepdims=True)
        dcw_ref[2:3, :] = dcw_ref[2:3, :] + jnp.sum(dcm * z[main], axis=0, keepdims=True)

        dh = dh_a + _dot(dp_ref[:, 2 * D_A:5 * D_A], win_ref[2 * D_A:5 * D_A, :])
        xv = x_ref[...]
        r = _rms(xv)
        xn = xv * r
        gn = g_ref[...]
        dsh = jnp.sum(dh, axis=0, keepdims=True)
        dsc = jnp.sum(dh * (xn * gn), axis=0, keepdims=True)
        dhn = dh * sc_ref[0]
        dgn = jnp.sum(dhn * xn, axis=0, keepdims=True)
        dxn = dhn * gn
        dxo_ref[...] = dx_v + r * (dxn - xn * jnp.mean(dxn * xn, axis=-1, keepdims=True))

        @pl.when(first)
        def _():
            dgate_ref[0] = dgate
            dsh_ref[0] = dsh
            dsc_ref[0] = dsc

        @pl.when(jnp.logical_not(first))
        def _():
            dgate_ref[0] = dgate_ref[0] + dgate
            dsh_ref[0] = dsh_ref[0] + dsh
            dsc_ref[0] = dsc_ref[0] + dsc

        @pl.when(i == 0)
        def _():
            dgn_ref[...] = dgn

        @pl.when(i != 0)
        def _():
            dgn_ref[...] = dgn_ref[...] + dgn

    prev_p, next_p = _halo_specs(tm, tokens, D_PROJ)
    _, next_d = _halo_specs(tm, tokens, D_MODEL)
    fixed2 = lambda shape: pl.BlockSpec(shape, lambda i: (0, 0))
    tile = pl.BlockSpec((tm, D_MODEL), lambda i: (i, 0))
    per_batch = pl.BlockSpec((1, 1, D_MODEL), lambda i: (i // per_seq, 0, 0))
    vec = jax.ShapeDtypeStruct((batch, 1, D_MODEL), F32)
    outs, got = _call(
        body, name=name, grid=(tokens // tm,),
        out_shape=[jax.ShapeDtypeStruct((tokens, D_MODEL), BF16), vec, jax.ShapeDtypeStruct((tokens, D_PROJ), BF16),
                   jax.ShapeDtypeStruct((tokens, D_MODEL), F32), vec, vec, jax.ShapeDtypeStruct((1, D_MODEL), F32),
                   jax.ShapeDtypeStruct((1, D_MODEL), F32), jax.ShapeDtypeStruct((8, D_A), F32),
                   jax.ShapeDtypeStruct((1, D_A), F32), jax.ShapeDtypeStruct((1, D_A), F32),
                   jax.ShapeDtypeStruct((CHUNK, D_A), F32), jax.ShapeDtypeStruct((N_HEADS, CHUNK, CHUNK), F32)],
        in_specs=[pl.BlockSpec((tm, D_PROJ), lambda i: (i, 0)), prev_p, next_p, tile, next_d, per_batch, tile,
                  _resident(w_out.shape), tile, _resident((1, D_MODEL)), per_batch, _resident(w_in_t.shape),
                  _resident(wm.shape), _resident(bias_full.shape), _resident(lng.shape), _resident(lnb.shape),
                  _resident(convw.shape), _resident(og.shape), _resident(bd.shape), _resident(wm_rows.shape),
                  _resident(causal.shape)],
        out_specs=[tile, per_batch, pl.BlockSpec((tm, D_PROJ), lambda i: (i, 0)), tile, per_batch, per_batch,
                   fixed2((1, D_MODEL)), fixed2((1, D_MODEL)), fixed2((8, D_A)), fixed2((1, D_A)), fixed2((1, D_A)),
                   fixed2((CHUNK, D_A)), pl.BlockSpec((N_HEADS, CHUNK, CHUNK), lambda i: (0, 0, 0))],
        scratch_shapes=[pltpu.VMEM((tm, D_A), F32), pltpu.VMEM((tm, D_A), F32), pltpu.VMEM((tm, D_MODEL), F32)],
        operands=(proj, proj, proj, dx, dx, gate, o, w_out, x, gnorm, scale1p, w_in_t, wm, bias_full, lng, lnb, convw, og, bd,
                  wm_rows, causal), comm=comm)
    return (*outs, got)


def _adamw_update(wv, gv, mv, vv):
    nm = ADAM_B1 * mv + (1.0 - ADAM_B1) * gv
    nv = ADAM_B2 * vv + (1.0 - ADAM_B2) * (gv * gv)
    m_hat = nm / (1.0 - ADAM_B1 ** ADAM_STEP)
    v_hat = nv / (1.0 - ADAM_B2 ** ADAM_STEP)
    return -ADAM_LR * (m_hat / (jnp.sqrt(v_hat) + ADAM_EPS) + ADAM_WD * wv), nm, nv


def _adamw_rows(recv, w, m, v, name):
    depth, rows, cols = w.shape
    tr = rows // 2
    last = rows // tr - 1

    def body(*refs):
        r_refs, (w_ref, m_ref, v_ref, g_ref, d_ref, nm_ref, nv_ref) = refs[:depth], refs[depth:]
        for l in range(depth):
            @pl.when(pl.program_id(0) == l)
            def _(r_ref=r_refs[l]):
                acc = r_ref[0].astype(F32)
                for d in range(1, N_DEV):
                    acc = acc + r_ref[d].astype(F32)
                g_ref[0] = acc
                d_ref[0], nm_ref[0], nv_ref[0] = _adamw_update(w_ref[0], acc, m_ref[0], v_ref[0])

    def slots(l):
        return pl.BlockSpec((N_DEV, tr, cols), lambda ll, i: (0, jnp.where(ll == l, i, jnp.where(ll < l, 0, last)), 0))

    spec = pl.BlockSpec((1, tr, cols), lambda ll, i: (ll, i, 0))
    return pl.pallas_call(
        body, name=name, grid=(depth, rows // tr),
        out_shape=[jax.ShapeDtypeStruct((depth, rows, cols), F32)] * 4,
        in_specs=[slots(l) for l in range(depth)] + [spec] * 3, out_specs=[spec] * 4,
        compiler_params=_params(2),
    )(*recv, w, m, v)


def _adamw(w, g, m, v, name):
    rows, cols = w.shape
    tr = max(t for t in range(8, 513, 8) if rows % t == 0)

    def body(w_ref, g_ref, m_ref, v_ref, d_ref, nm_ref, nv_ref):
        d_ref[...], nm_ref[...], nv_ref[...] = _adamw_update(w_ref[...], g_ref[...], m_ref[...], v_ref[...])

    spec = pl.BlockSpec((tr, cols), lambda i: (i, 0))
    return pl.pallas_call(
        body, name=name, grid=(rows // tr,),
        out_shape=[jax.ShapeDtypeStruct((rows, cols), F32)] * 3,
        in_specs=[spec] * 4, out_specs=[spec] * 3,
        compiler_params=_params(parallel=True),
    )(w, g, m, v)


def _adamw_many(ws, gs, ms, vs, name):
    n = len(ws)
    two_d = lambda a: a.reshape(-1, a.shape[-1])

    def body(*refs):
        w_refs, g_refs, m_refs, v_refs = refs[:n], refs[n:2 * n], refs[2 * n:3 * n], refs[3 * n:4 * n]
        d_refs, nm_refs, nv_refs = refs[4 * n:5 * n], refs[5 * n:6 * n], refs[6 * n:]
        for k in range(n):
            d_refs[k][...], nm_refs[k][...], nv_refs[k][...] = _adamw_update(
                w_refs[k][...], g_refs[k][...], m_refs[k][...], v_refs[k][...])

    flat = [two_d(a) for a in ws]
    outs = pl.pallas_call(
        body, name=name, out_shape=[jax.ShapeDtypeStruct(a.shape, F32) for a in flat] * 3,
        in_specs=[pl.BlockSpec(memory_space=pltpu.VMEM)] * (4 * n),
        out_specs=[pl.BlockSpec(memory_space=pltpu.VMEM)] * (3 * n),
        compiler_params=pltpu.CompilerParams(vmem_limit_bytes=VMEM_LIMIT),
    )(*flat, *[two_d(a) for a in gs], *[two_d(a) for a in ms], *[two_d(a) for a in vs])
    shaped = [o.reshape(ws[k % n].shape) for k, o in enumerate(outs)]
    return shaped[:n], shaped[n:2 * n], shaped[2 * n:]


def _adamw_nd(w, g, m, v, name):
    shape = w.shape
    two_d = (-1, shape[-1])
    d, nm, nv = _adamw(w.reshape(two_d), g.reshape(two_d), m.reshape(two_d), v.reshape(two_d), name)
    return d.reshape(shape), nm.reshape(shape), nv.reshape(shape)


def kernel(x, c, ada_w, ada_b, norm_ffn1_g, ffn1_w_gu, ffn1_w_down, norm_mix_g, mix_w_in, sgu_ln_g, sgu_ln_b, sgu_w_s, sgu_b, conv_w, out_norm_g, mix_w_out, norm_ffn2_g, ffn2_w_gu, ffn2_w_down, final_norm_g, loss_target, m_ada_w, m_ada_b, m_norm_ffn1_g, m_ffn1_w_gu, m_ffn1_w_down, m_norm_mix_g, m_mix_w_in, m_sgu_ln_g, m_sgu_ln_b, m_sgu_w_s, m_sgu_b, m_conv_w, m_out_norm_g, m_mix_w_out, m_norm_ffn2_g, m_ffn2_w_gu, m_ffn2_w_down, m_final_norm_g, v_ada_w, v_ada_b, v_norm_ffn1_g, v_ffn1_w_gu, v_ffn1_w_down, v_norm_mix_g, v_mix_w_in, v_sgu_ln_g, v_sgu_ln_b, v_sgu_w_s, v_sgu_b, v_conv_w, v_out_norm_g, v_mix_w_out, v_norm_ffn2_g, v_ffn2_w_gu, v_ffn2_w_down, v_final_norm_g):
    batch, seq, _ = x.shape
    tokens = batch * seq
    me = 4 * lax.axis_index("x") + 2 * lax.axis_index("y") + lax.axis_index("c")
    weights = dict(ada_w=ada_w, ada_b=ada_b, norm_ffn1_g=norm_ffn1_g, ffn1_w_gu=ffn1_w_gu, ffn1_w_down=ffn1_w_down,
                   norm_mix_g=norm_mix_g, mix_w_in=mix_w_in, sgu_ln_g=sgu_ln_g, sgu_ln_b=sgu_ln_b, sgu_w_s=sgu_w_s,
                   sgu_b=sgu_b, conv_w=conv_w, out_norm_g=out_norm_g, mix_w_out=mix_w_out, norm_ffn2_g=norm_ffn2_g,
                   ffn2_w_gu=ffn2_w_gu, ffn2_w_down=ffn2_w_down, final_norm_g=final_norm_g)
    mom1 = dict(ada_w=m_ada_w, ada_b=m_ada_b, norm_ffn1_g=m_norm_ffn1_g, ffn1_w_gu=m_ffn1_w_gu,
                ffn1_w_down=m_ffn1_w_down, norm_mix_g=m_norm_mix_g, mix_w_in=m_mix_w_in, sgu_ln_g=m_sgu_ln_g,
                sgu_ln_b=m_sgu_ln_b, sgu_w_s=m_sgu_w_s, sgu_b=m_sgu_b, conv_w=m_conv_w, out_norm_g=m_out_norm_g,
                mix_w_out=m_mix_w_out, norm_ffn2_g=m_norm_ffn2_g, ffn2_w_gu=m_ffn2_w_gu, ffn2_w_down=m_ffn2_w_down,
                final_norm_g=m_final_norm_g)
    mom2 = dict(ada_w=v_ada_w, ada_b=v_ada_b, norm_ffn1_g=v_norm_ffn1_g, ffn1_w_gu=v_ffn1_w_gu,
                ffn1_w_down=v_ffn1_w_down, norm_mix_g=v_norm_mix_g, mix_w_in=v_mix_w_in, sgu_ln_g=v_sgu_ln_g,
                sgu_ln_b=v_sgu_ln_b, sgu_w_s=v_sgu_w_s, sgu_b=v_sgu_b, conv_w=v_conv_w, out_norm_g=v_out_norm_g,
                mix_w_out=v_mix_w_out, norm_ffn2_g=v_norm_ffn2_g, ffn2_w_gu=v_ffn2_w_gu, ffn2_w_down=v_ffn2_w_down,
                final_norm_g=v_final_norm_g)

    big = ("ffn1_w_gu", "ffn1_w_down", "mix_w_in", "mix_w_out", "ffn2_w_gu", "ffn2_w_down")
    transposed = ("ffn1_w_gu", "mix_w_in", "ffn2_w_gu")
    as_rows = lambda nm, a: jnp.swapaxes(a, 1, 2) if nm in transposed else a
    shard = {(l, nm): as_rows(nm, weights[nm])[l].astype(BF16) for l in range(DEPTH) for nm in big}
    full_w = {}

    def gather_of(keys):
        return keys, _GatherRows([shard[k] for k in keys])

    def landed(plan, got):
        full_w.update(zip(plan[0], got))

    ada_cols = ada_w.shape[2]
    ada_b_cols = lax.dynamic_slice_in_dim(ada_b, me * ada_cols, ada_cols, axis=1).reshape(DEPTH, 1, ada_cols)
    plan = gather_of([(0, "ffn1_w_gu")])
    c_dev, convw_dev, ada_recv, got = _prologue(
        jnp.pad(c, ((0, 8 - batch), (0, 0))), jnp.pad(conv_w.reshape(-1), (0, 8 * LANES - conv_w.size)).reshape(8, LANES),
        ada_w, ada_b_cols, plan[1])
    landed(plan, got)
    c_all = c_dev[:, :batch].reshape(N_DEV * batch, D_MODEL)
    convw_all = convw_dev.reshape(N_DEV, -1)[:, :conv_w.size].reshape((N_DEV,) + conv_w.shape)
    convw_full = jnp.transpose(convw_all, (1, 2, 0, 3)).reshape(DEPTH, 3, D_A)
    ada_mine = jnp.transpose(ada_recv[:, :, :batch, :], (1, 2, 0, 3)).reshape(DEPTH, batch, N_MOD * D_MODEL)
    mod = ada_mine.reshape(DEPTH, batch, N_MOD, 1, D_MODEL)

    causal = jnp.tril(jnp.ones((CHUNK, CHUNK), F32))
    bd = jnp.kron(jnp.eye(MXU_N // HEAD_DIM, dtype=F32), jnp.full((HEAD_DIM, HEAD_DIM), 1.0 / HEAD_DIM, F32)).astype(BF16)
    row_vec = lambda a: a.reshape(1, -1)

    hosted_gathers = {
        (0, "ffn1"): [(0, "ffn1_w_down"), (0, "mix_w_in"), (0, "mix_w_out")],
        (0, "ffn_down1"): [(0, "ffn2_w_down")],
        (0, "mix_in"): [(0, "ffn2_w_gu")],
        (0, "ffn2"): [(1, "ffn1_w_gu"), (1, "ffn1_w_down"), (1, "mix_w_in"), (1, "mix_w_out")],
        (1, "ffn1"): [(1, "ffn2_w_gu"), (1, "ffn2_w_down")],
    }

    def hosting(l, site):
        keys = hosted_gathers.get((l, site))
        return gather_of(keys) if keys else (None, None)

    xs = x.reshape(tokens, D_MODEL)
    saved = []
    for l in range(DEPTH):
        sh1, sc1, g1, sh2, sc2, g2, sh3, sc3, g3 = [mod[l, :, k] for k in range(N_MOD)]
        wm_masked = (sgu_w_s[l] * causal[None]).astype(BF16)
        mixer_consts = dict(
            wm=jnp.concatenate([wm_masked[0::2], wm_masked[1::2]], axis=2),
            bias_full=jnp.repeat(sgu_b[l].T, HEAD_DIM, axis=1),
            lng=row_vec(jnp.tile(sgu_ln_g[l], N_HEADS)), lnb=row_vec(jnp.tile(sgu_ln_b[l], N_HEADS)),
            convw=jnp.pad(convw_full[l], ((0, 5), (0, 0))), og=row_vec(out_norm_g[l]), bd=bd)
        x0 = xs
        plan = hosting(l, "ffn1")
        if l == 0:
            h1, a1, s1, w1, got = _normmod_matmul(x0, row_vec(norm_ffn1_g[l]), 1.0 + sc1, sh1, full_w[l, "ffn1_w_gu"], seq, "ffn_up", plan[1])
            landed(plan, got)
            plan = hosting(l, "ffn_down1")
            x1, f1, got = _matmul_residual(a1, full_w[l, "ffn1_w_down"], x0, g1, 0.5, seq, "ffn_down", plan[1])
        else:
            h1, a1, s1, w1, x1, f1, got = _ffn_forward(
                x0, row_vec(norm_ffn1_g[l]), 1.0 + sc1, sh1, full_w[l, "ffn1_w_gu"], full_w[l, "ffn1_w_down"], g1, 0.5, seq, "ffn_fwd",
                comm=plan[1])
        if got:
            landed(plan, got)
        plan = hosting(l, "mix_in")
        h2, proj, ymix, x2, o2, got = _mixer_forward(
            x1, row_vec(norm_mix_g[l]), 1.0 + sc2, sh2, full_w[l, "mix_w_in"], g2, full_w[l, "mix_w_out"], seq=seq,
            name="mixer_forward", comm=plan[1], **mixer_consts)
        if got:
            landed(plan, got)
        plan = hosting(l, "ffn2")
        if l + 1 < DEPTH:
            h3, a3, s3, w3, x3, f3, got = _ffn_forward(
                x2, row_vec(norm_ffn2_g[l]), 1.0 + sc3, sh3, full_w[l, "ffn2_w_gu"], full_w[l, "ffn2_w_down"], g3, 0.5, seq, "ffn_fwd",
                comm=plan[1])
        else:
            head = (loss_target.reshape(tokens, D_MODEL), row_vec(final_norm_g))
            h3, a3, s3, w3, x3, f3, d_final_g, loss_cols, got = _ffn_forward(
                x2, row_vec(norm_ffn2_g[l]), 1.0 + sc3, sh3, full_w[l, "ffn2_w_gu"], full_w[l, "ffn2_w_down"], g3, 0.5, seq, "ffn_fwd_loss",
                loss_head=head, comm=plan[1])
        if got:
            landed(plan, got)
        saved.append(dict(x0=x0, x1=x1, x2=x2, h1=h1, h2=h2, h3=h3, a1=a1, s1=s1, w1=w1, a3=a3, s3=s3, w3=w3, f1=f1, f3=f3, o2=o2, proj=proj,
                          ymix=ymix, mixer_consts=mixer_consts, wm_rows=wm_masked.reshape(N_HEADS // 2, 2 * CHUNK, CHUNK), sc=(1.0 + sc1, 1.0 + sc2, 1.0 + sc3), gates=(g1, g2, g3)))
        xs = x3

    dx = xs

    recv = {}
    small_grads = [None] * DEPTH
    d_mod = [None] * DEPTH

    mix_names = ("out_norm_g", "sgu_ln_g", "sgu_ln_b", "sgu_w_s", "sgu_b", "conv_w")
    late_names = ("norm_ffn1_g", "norm_mix_g", "norm_ffn2_g")

    def mix_parts(l):
        return [small_grads[l][nm] for nm in mix_names]

    def late_parts(l):
        return [small_grads[l][nm] for nm in late_names] + [d_mod[l]]

    pending = []

    def scatter_later(l, nm, grad):
        pending.append(((l, nm), _ScatterRows([grad])))

    def host():
        keys, parts = [k for k, _ in pending], [p for _, p in pending]
        pending.clear()
        return keys, (_Exchanges(parts) if parts else None)

    def hosted(keys, got):
        if got:
            recv.update(zip(keys, got))

    for l in reversed(range(DEPTH)):
        sv = saved[l]
        mc = sv["mixer_consts"]
        if l + 1 < DEPTH:
            pending.append((("late", l + 1), _GatherRows([_pack_small(late_parts(l + 1))])))
        keys, comm = host()
        df3, dg3, dgu3, dx2, dsh3, dsc3, dn3, got = _ffn_backward(
            dx, sv["gates"][2], sv["f3"], sv["s3"], sv["w3"], full_w[l, "ffn2_w_down"], full_w[l, "ffn2_w_gu"], sv["x2"],
            row_vec(norm_ffn2_g[l]), sv["sc"][2], 0.5, seq, "ffn_bwd", comm)
        hosted(keys, got)
        gw_down2, _ = _weight_grad(sv["a3"], df3, seq, "grad_w_down")
        scatter_later(l, "ffn2_w_down", gw_down2)
        keys, comm = host()
        gw_gu2, got = _weight_grad(dgu3, sv["h3"], seq, "grad_w_gu", comm)
        hosted(keys, got)
        scatter_later(l, "ffn2_w_gu", gw_gu2)
        keys, comm = host()
        do2, dg2, dproj, dx1, dsh2, dsc2, dn2, d_og, d_cw, d_lng, d_lnb, d_bias, d_wm, got = _mixer_backward(
            sv["proj"], dx2, sv["gates"][1], sv["o2"], full_w[l, "mix_w_out"], sv["x1"], row_vec(norm_mix_g[l]), sv["sc"][1],
            full_w[l, "mix_w_in"], wm_rows=sv["wm_rows"], causal=causal, seq=seq, name="mixer_backward", comm=comm, **mc)
        hosted(keys, got)
        small_grads[l] = dict(
            out_norm_g=d_og, sgu_ln_g=d_lng.reshape(N_HEADS, HEAD_DIM).sum(0), sgu_ln_b=d_lnb.reshape(N_HEADS, HEAD_DIM).sum(0),
            sgu_w_s=d_wm, sgu_b=d_bias.reshape(CHUNK, N_HEADS, HEAD_DIM).sum(-1).T, conv_w=d_cw[0:3])
        gw_out, _ = _weight_grad(sv["ymix"], do2, seq, "grad_w_out")
        scatter_later(l, "mix_w_out", gw_out)
        keys, comm = host()
        gw_in, got = _weight_grad(dproj, sv["h2"], seq, "grad_w_in", comm)
        hosted(keys, got)
        scatter_later(l, "mix_w_in", gw_in)
        keys, comm = host()
        pending.append((("mix", l), _GatherRows([_pack_small(mix_parts(l))])))
        if l > 0:
            df1, dg1, dgu1, dx0, dsh1, dsc1, dn1, got = _ffn_backward(
                dx1, sv["gates"][0], sv["f1"], sv["s1"], sv["w1"], full_w[l, "ffn1_w_down"], full_w[l, "ffn1_w_gu"], sv["x0"],
                row_vec(norm_ffn1_g[l]), sv["sc"][0], 0.5, seq, "ffn_bwd", comm)
        else:
            df1, dg1, dgu1, got = _residual_backward(dx1, sv["gates"][0], sv["f1"], full_w[l, "ffn1_w_down"], 0.5, sv["s1"], sv["w1"], seq, "ffn_down_bwd", comm)
        hosted(keys, got)
        gw_down1, _ = _weight_grad(sv["a1"], df1, seq, "grad_w_down")
        scatter_later(l, "ffn1_w_down", gw_down1)
        keys, comm = host()
        gw_gu1, got = _weight_grad(dgu1, sv["h1"], seq, "grad_w_gu", comm)
        hosted(keys, got)
        scatter_later(l, "ffn1_w_gu", gw_gu1)
        if l == 0:
            keys, comm = host()
            dx0, dsh1, dsc1, dn1, got = _matmul_normmod_backward(dgu1, full_w[l, "ffn1_w_gu"], sv["x0"], dx1, row_vec(norm_ffn1_g[l]), sv["sc"][0], seq, "ffn_up_bwd", comm)
            hosted(keys, got)
        dx = dx0
        small_grads[l].update(norm_ffn1_g=dn1, norm_mix_g=dn2, norm_ffn2_g=dn3)
        d_mod[l] = jnp.concatenate([dsh1, dsc1, dg1, dsh2, dsc2, dg2, dsh3, dsc3, dg3], axis=1)
    grad_x = dx.reshape(batch, seq, D_MODEL)

    grad_big, delta, new_m, new_v = {}, {}, {}, {}
    for nm in big:
        results = _adamw_rows([recv[l, nm] for l in range(DEPTH)], as_rows(nm, weights[nm]), as_rows(nm, mom1[nm]),
                              as_rows(nm, mom2[nm]), "adamw_" + nm)
        grad_big[nm], delta[nm], new_m[nm], new_v[nm] = [as_rows(nm, r) for r in results]

    last_parts = late_parts(0) + [d_final_g, loss_cols]
    last_shapes = [p.shape for p in last_parts]
    packed_all, packed_sum = _all_gather_small(_pack_small(last_parts), "reduce_small")
    late_sum = {0: _unpack_small(packed_sum, last_shapes)}
    d_mod_dev = {0: _unpack_small(packed_all, last_shapes, lead=(N_DEV,))[len(late_names)]}
    mix_sum = {}
    for l in range(DEPTH):
        gathered = recv["mix", l].reshape(N_DEV, -1, LANES)
        mix_sum[l] = _unpack_small(_sum_gathered(gathered, "sum_mix"), [p.shape for p in mix_parts(l)])
        if l > 0:
            shapes_l = [p.shape for p in late_parts(l)]
            gathered = recv["late", l].reshape(N_DEV, -1, LANES)
            late_sum[l] = _unpack_small(_sum_gathered(gathered, "sum_late"), shapes_l)
            d_mod_dev[l] = _unpack_small(gathered, shapes_l, lead=(N_DEV,))[len(late_names)]
    grad_small = {}
    for group, names in ((mix_sum, mix_names), (late_sum, late_names)):
        for k, nm in enumerate(names):
            grad_small[nm] = jnp.stack([group[l][k] for l in range(DEPTH)]).reshape(
                (DEPTH, 3, D_A) if nm == "conv_w" else weights[nm].shape)
    grad_small["conv_w"] = lax.dynamic_slice_in_dim(grad_small["conv_w"], me * conv_w.shape[2], conv_w.shape[2], axis=2)
    grad_small["final_norm_g"] = late_sum[0][len(late_names) + 1].reshape(final_norm_g.shape)
    loss = jnp.sum(late_sum[0][len(late_names) + 2])
    d_ada_all = jnp.stack([d_mod_dev[l] for l in range(DEPTH)]).reshape(DEPTH, N_DEV * batch, N_MOD * D_MODEL)
    d_ada_cols = lax.dynamic_slice_in_dim(d_ada_all, me * ada_cols, ada_cols, axis=2)
    g_ada_w, g_ada_b = _ada_backward(c_all, d_ada_cols, d_ada_all)

    grads = dict(grad_big)
    grads.update(grad_small)
    grads["ada_w"] = g_ada_w
    grads["ada_b"] = g_ada_b.reshape(ada_b.shape)

    names = ("ada_w", "ada_b", "norm_ffn1_g", "ffn1_w_gu", "ffn1_w_down", "norm_mix_g", "mix_w_in", "sgu_ln_g",
             "sgu_ln_b", "sgu_w_s", "sgu_b", "conv_w", "out_norm_g", "mix_w_out", "norm_ffn2_g", "ffn2_w_gu",
             "ffn2_w_down", "final_norm_g")
    delta["ada_w"], new_m["ada_w"], new_v["ada_w"] = _adamw_nd(ada_w, grads["ada_w"], m_ada_w, v_ada_w, "adamw_ada_w")
    rest = [nm for nm in names if nm not in big and nm != "ada_w"]
    pick = lambda src: [src[nm] for nm in rest]
    for nm, d_k, m_k, v_k in zip(rest, *_adamw_many(pick(weights), pick(grads), pick(mom1), pick(mom2), "adamw_small")):
        delta[nm], new_m[nm], new_v[nm] = d_k, m_k, v_k

    return (loss, grad_x, *[grads[nm] for nm in names], *[delta[nm] for nm in names],
            *[new_m[nm] for nm in names], *[new_v[nm] for nm in names])
```

```python
import math

import jax
import jax.numpy as jnp
from jax import lax
from jax.experimental import pallas as pl
from jax.experimental.pallas import tpu as pltpu

F32 = jnp.float32
BF16 = jnp.bfloat16

D_MODEL = 1024
D_A = 512
D_PROJ = 2560
N_HEADS = 8
HEAD_DIM = 64
CHUNK = 128
N_MOD = 9
DEPTH = 2
EPS = 1e-6
N_DEV = 8
LANES = 128
MXU_N = 256
HALO = 16
VMEM_LIMIT = 62 * 1024 * 1024
FORWARD_STEPS = 3

ADAM_LR = 0.001
ADAM_B1 = 0.9
ADAM_B2 = 0.999
ADAM_EPS = 1e-08
ADAM_WD = 0.01
ADAM_STEP = 10

MESH = pl.DeviceIdType.MESH


def _dot(a, b):
    return jnp.dot(a, b, preferred_element_type=F32)


def _dot_nt(a, b):
    return lax.dot_general(a, b, (((1,), (1,)), ((), ())), preferred_element_type=F32)


def _dot_tn(a, b):
    return lax.dot_general(a, b, (((0,), (0,)), ((), ())), preferred_element_type=F32)


def _sigmoid(x):
    return 0.5 * jnp.tanh(0.5 * x) + 0.5


def _gelu(x):
    return 0.5 * x * (1.0 + lax.erf(x * (1.0 / math.sqrt(2.0))))


def _gelu_grad(x):
    cdf = 0.5 * (1.0 + lax.erf(x * (1.0 / math.sqrt(2.0))))
    return cdf + x * jnp.exp(-0.5 * x * x) * (1.0 / math.sqrt(2.0 * math.pi))


def _params(n_axes=1, parallel=False):
    sem = ("parallel" if parallel else "arbitrary",) * n_axes
    return pltpu.CompilerParams(dimension_semantics=sem, vmem_limit_bytes=VMEM_LIMIT)


def _resident(shape):
    nd = len(shape)
    return pl.BlockSpec(shape, lambda *_: (0,) * nd, pipeline_mode=pl.Buffered(1))


def _tile_rows(seq):
    return min(512, seq)


def _my_position():
    x, y, c = lax.axis_index("x"), lax.axis_index("y"), lax.axis_index("c")
    return x, y, c, 4 * x + 2 * y + c


def _peer(x, y, c, p):
    return (x ^ ((p >> 2) & 1), y ^ ((p >> 1) & 1), c ^ (p & 1))


class _GatherRows:
    def __init__(self, shards):
        self.operands = list(shards)
        n = len(shards)
        self.out_shape = [jax.ShapeDtypeStruct((N_DEV * s.shape[0], s.shape[1]), s.dtype) for s in shards]
        self.scratch = [pltpu.SemaphoreType.DMA((n, N_DEV - 1)), pltpu.SemaphoreType.DMA((n, N_DEV - 1)),
                        pltpu.SemaphoreType.DMA((n,))]

    def _plan(self, src, dst, send, recv, loc):
        x, y, c, _ = _my_position()
        me, sib = (x, y, c), (x, y, 1 - c)
        chips = [(1 - x, y), (x, 1 - y), (1 - x, 1 - y)]
        plans = []
        for k, shard in enumerate(self.operands):
            rows = shard.shape[0]

            def blk(pos, k=k, rows=rows):
                return dst[k].at[pl.ds((4 * pos[0] + 2 * pos[1] + pos[2]) * rows, rows), :]

            def rc(s, block, to, source=None, k=k, blk=blk):
                return pltpu.make_async_remote_copy(
                    src_ref=blk(block) if source is None else source, dst_ref=blk(block),
                    send_sem=send.at[k, s], recv_sem=recv.at[k, s], device_id=to, device_id_type=MESH)

            plans.append(dict(
                local=pltpu.make_async_copy(src[k], blk(me), loc.at[k]),
                first=[rc(0, me, sib, src[k])] + [rc(1 + j, me, (*chip, c), src[k]) for j, chip in enumerate(chips)],
                landed=[rc(1 + j, (*chip, c), me) for j, chip in enumerate(chips)],
                passed=[rc(4 + j, (*chip, c), sib) for j, chip in enumerate(chips)],
                from_sib=[rc(0, sib, me)] + [rc(4 + j, (*chip, 1 - c), me) for j, chip in enumerate(chips)]))
        return plans

    def start(self, src, dst, send, recv, loc):
        for plan in self._plan(src, dst, send, recv, loc):
            plan["local"].start()
            for cp in plan["first"]:
                cp.start()

    def forward(self, src, dst, send, recv, loc):
        for plan in self._plan(src, dst, send, recv, loc):
            for landed, passed in zip(plan["landed"], plan["passed"]):
                landed.wait_recv()
                passed.start()

    def finish(self, src, dst, send, recv, loc):
        for plan in self._plan(src, dst, send, recv, loc):
            for cp in plan["from_sib"]:
                cp.wait_recv()
            for cp in plan["first"] + plan["passed"]:
                cp.wait_send()
            plan["local"].wait()


class _ScatterRows:
    def __init__(self, grads):
        self.operands = list(grads)
        n = len(grads)
        self.out_shape = [jax.ShapeDtypeStruct((N_DEV, g.shape[0] // N_DEV, g.shape[1]), g.dtype) for g in grads]
        self.scratch = [pltpu.SemaphoreType.DMA((n, N_DEV - 1)), pltpu.SemaphoreType.DMA((n, N_DEV - 1)),
                        pltpu.SemaphoreType.DMA((n,))]

    def _plan(self, src, dst, send, recv, loc):
        x, y, c, me = _my_position()
        copies = []
        for k, grad in enumerate(self.operands):
            rows = grad.shape[0] // N_DEV
            copies.append(pltpu.make_async_copy(src[k].at[pl.ds(me * rows, rows), :], dst[k].at[me], loc.at[k]))
            for p in range(1, N_DEV):
                px, py, pc = _peer(x, y, c, p)
                copies.append(pltpu.make_async_remote_copy(
                    src_ref=src[k].at[pl.ds((4 * px + 2 * py + pc) * rows, rows), :], dst_ref=dst[k].at[me],
                    send_sem=send.at[k, p - 1], recv_sem=recv.at[k, p - 1], device_id=(px, py, pc), device_id_type=MESH))
        return copies

    def start(self, src, dst, send, recv, loc):
        for cp in self._plan(src, dst, send, recv, loc):
            cp.start()

    def forward(self, src, dst, send, recv, loc):
        pass

    def finish(self, src, dst, send, recv, loc):
        for cp in self._plan(src, dst, send, recv, loc):
            cp.wait()


class _Exchanges:
    def __init__(self, parts):
        self.parts = list(parts)
        self.operands = [op for part in self.parts for op in part.operands]
        self.out_shape = [shp for part in self.parts for shp in part.out_shape]
        self.scratch = [scr for part in self.parts for scr in part.scratch]

    def _each(self, src, dst, sems):
        at, sem_at = 0, 0
        for part in self.parts:
            n, n_sem = len(part.operands), len(part.scratch)
            yield part, src[at:at + n], dst[at:at + n], sems[sem_at:sem_at + n_sem]
            at, sem_at = at + n, sem_at + n_sem

    def start(self, src, dst, *sems):
        for part, part_src, part_dst, part_sems in self._each(src, dst, sems):
            part.start(part_src, part_dst, *part_sems)

    def forward(self, src, dst, *sems):
        for part, part_src, part_dst, part_sems in self._each(src, dst, sems):
            part.forward(part_src, part_dst, *part_sems)

    def finish(self, src, dst, *sems):
        for part, part_src, part_dst, part_sems in self._each(src, dst, sems):
            part.finish(part_src, part_dst, *part_sems)


_ANY = pl.BlockSpec(memory_space=pl.ANY)


def _call(body, *, name, grid, in_specs, out_specs, out_shape, operands, scratch_shapes=(), parallel=False, comm=None):
    n_axes = len(grid)
    if comm is None:
        outs = pl.pallas_call(body, name=name, grid=grid, out_shape=list(out_shape), in_specs=list(in_specs),
                              out_specs=list(out_specs), scratch_shapes=list(scratch_shapes),
                              compiler_params=_params(n_axes, parallel))(*operands)
        return list(outs), None
    n_in, n_out, n_scr, n_c = len(in_specs), len(out_specs), len(scratch_shapes), len(comm.operands)
    total = math.prod(grid)

    def hosted(*refs):
        ins, c_src = refs[:n_in], refs[n_in:n_in + n_c]
        outs, c_dst = refs[n_in + n_c:n_in + n_c + n_out], refs[n_in + n_c + n_out:n_in + 2 * n_c + n_out]
        scr, sems = refs[n_in + 2 * n_c + n_out:n_in + 2 * n_c + n_out + n_scr], refs[n_in + 2 * n_c + n_out + n_scr:]
        step = pl.program_id(0)
        for axis in range(1, n_axes):
            step = step * grid[axis] + pl.program_id(axis)

        @pl.when(step == 0)
        def _():
            comm.start(c_src, c_dst, *sems)

        @pl.when(step == max(total - FORWARD_STEPS, 0))
        def _():
            comm.forward(c_src, c_dst, *sems)

        body(*ins, *outs, *scr)

        @pl.when(step == total - 1)
        def _():
            comm.finish(c_src, c_dst, *sems)

    res = pl.pallas_call(hosted, name=name, grid=grid, out_shape=list(out_shape) + comm.out_shape,
                         in_specs=list(in_specs) + [_ANY] * n_c, out_specs=list(out_specs) + [_ANY] * n_c,
                         scratch_shapes=list(scratch_shapes) + comm.scratch,
                         compiler_params=_params(n_axes, False))(*operands, *comm.operands)
    return list(res[:n_out]), list(res[n_out:])


def _all_gather_small(v, name):
    rows = v.shape[0]

    def body(v_ref, all_ref, sum_ref, send_sems, recv_sems):
        x, y, c, me = _my_position()
        all_ref[me] = v_ref[...]
        copies = []
        for p in range(1, N_DEV):
            cp = pltpu.make_async_remote_copy(
                src_ref=v_ref, dst_ref=all_ref.at[me], send_sem=send_sems.at[p - 1], recv_sem=recv_sems.at[p - 1],
                device_id=_peer(x, y, c, p), device_id_type=MESH)
            cp.start()
            copies.append(cp)
        for cp in copies:
            cp.wait()
        acc = all_ref[0]
        for d in range(1, N_DEV):
            acc = acc + all_ref[d]
        sum_ref[...] = acc

    return pl.pallas_call(
        body, name=name,
        out_shape=[jax.ShapeDtypeStruct((N_DEV, rows, LANES), F32), jax.ShapeDtypeStruct((rows, LANES), F32)],
        in_specs=[pl.BlockSpec(memory_space=pltpu.VMEM)],
        out_specs=[pl.BlockSpec(memory_space=pltpu.VMEM)] * 2,
        scratch_shapes=[pltpu.SemaphoreType.DMA((N_DEV - 1,)), pltpu.SemaphoreType.DMA((N_DEV - 1,))],
        compiler_params=pltpu.CompilerParams(vmem_limit_bytes=VMEM_LIMIT),
    )(v)


def _sum_gathered(gathered, name):
    rows = gathered.shape[1]

    def body(g_ref, o_ref):
        acc = g_ref[0]
        for d in range(1, N_DEV):
            acc = acc + g_ref[d]
        o_ref[...] = acc

    return pl.pallas_call(
        body, name=name, out_shape=jax.ShapeDtypeStruct((rows, LANES), F32),
        in_specs=[pl.BlockSpec(memory_space=pltpu.VMEM)], out_specs=pl.BlockSpec(memory_space=pltpu.VMEM),
        compiler_params=pltpu.CompilerParams(vmem_limit_bytes=VMEM_LIMIT),
    )(gathered)


def _pack_small(parts):
    flat = jnp.concatenate([p.reshape(-1).astype(F32) for p in parts])
    total = flat.shape[0]
    padded = -(-total // (8 * LANES)) * (8 * LANES)
    flat = jnp.pad(flat, (0, padded - total))
    return flat.reshape(padded // LANES, LANES)


def _unpack_small(packed, shapes, lead=()):
    flat = packed.reshape(lead + (-1,))
    out, off = [], 0
    for shp in shapes:
        size = math.prod(shp)
        out.append(flat[..., off:off + size].reshape(lead + tuple(shp)))
        off += size
    return out


def _prologue(c_rows, convw_rows, ada_w, ada_b_cols, gather):
    depth, _, cols = ada_w.shape
    n_c = len(gather.operands)
    sub = 8

    def body(c_ref, cw_ref, b_ref, w_hbm, *rest):
        g_src, (c_all_ref, cw_all_ref, ada_ref), g_dst = rest[:n_c], rest[n_c:n_c + 3], rest[n_c + 3:2 * n_c + 3]
        ada_local, w_ref, w_sem, send_sems, recv_sems = rest[2 * n_c + 3:2 * n_c + 8]
        g_sems = rest[2 * n_c + 8:]
        x, y, c, me = _my_position()
        gather.start(g_src, g_dst, *g_sems)
        load_w = pltpu.make_async_copy(w_hbm, w_ref, w_sem)
        load_w.start()

        def to_all(k, src_ref, dst_ref):
            copies = []
            for p in range(1, N_DEV):
                copies.append(pltpu.make_async_remote_copy(
                    src_ref=src_ref, dst_ref=dst_ref.at[me], send_sem=send_sems.at[k, p - 1], recv_sem=recv_sems.at[k, p - 1],
                    device_id=_peer(x, y, c, p), device_id_type=MESH))
            return copies

        first = to_all(0, c_ref, c_all_ref) + to_all(1, cw_ref, cw_all_ref)
        c_all_ref[me] = c_ref[...]
        cw_all_ref[me] = cw_ref[...]
        for cp in first:
            cp.start()
        for cp in first:
            cp.wait()
        cv = c_all_ref[...].reshape(N_DEV * sub, D_MODEL)
        act = (cv * _sigmoid(cv)).astype(BF16)
        load_w.wait()
        for l in range(depth):
            ada_local[l] = _dot(act, w_ref[l].astype(BF16)) + b_ref[l]
        ada_ref[me] = ada_local[:, pl.ds(pl.multiple_of(me * sub, sub), sub), :]
        rows_out = []
        for p in range(1, N_DEV):
            px, py, pc = _peer(x, y, c, p)
            rows = pl.ds(pl.multiple_of((4 * px + 2 * py + pc) * sub, sub), sub)
            rows_out.append(pltpu.make_async_remote_copy(
                src_ref=ada_local.at[:, rows, :], dst_ref=ada_ref.at[me], send_sem=send_sems.at[2, p - 1],
                recv_sem=recv_sems.at[2, p - 1], device_id=(px, py, pc), device_id_type=MESH))
        for cp in rows_out:
            cp.start()
        for cp in rows_out:
            cp.wait()
        gather.forward(g_src, g_dst, *g_sems)
        gather.finish(g_src, g_dst, *g_sems)

    vmem = pl.BlockSpec(memory_space=pltpu.VMEM)
    outs = pl.pallas_call(
        body, name="prologue",
        out_shape=[jax.ShapeDtypeStruct((N_DEV, sub, D_MODEL), F32), jax.ShapeDtypeStruct((N_DEV, sub, LANES), F32),
                   jax.ShapeDtypeStruct((N_DEV, depth, sub, cols), F32)] + gather.out_shape,
        in_specs=[vmem] * 3 + [_ANY] * (1 + n_c), out_specs=[vmem] * 3 + [_ANY] * n_c,
        scratch_shapes=[pltpu.VMEM((depth, N_DEV * sub, cols), F32), pltpu.VMEM(ada_w.shape, F32), pltpu.SemaphoreType.DMA,
                        pltpu.SemaphoreType.DMA((3, N_DEV - 1)), pltpu.SemaphoreType.DMA((3, N_DEV - 1))] + gather.scratch,
        compiler_params=pltpu.CompilerParams(vmem_limit_bytes=VMEM_LIMIT),
    )(c_rows, convw_rows, ada_b_cols, ada_w, *gather.operands)
    return outs[0], outs[1], outs[2], list(outs[3:])


def _ada_backward(c_all, d_ada_cols, d_ada_all):
    nb = c_all.shape[0]
    cols = d_ada_cols.shape[2]
    full = d_ada_all.shape[2]

    def body(c_ref, dc_ref, da_ref, gw_ref, gb_ref):
        cv = c_ref[...]
        act = (cv * _sigmoid(cv)).astype(BF16)
        gw_ref[0] = _dot_tn(act, dc_ref[0].astype(BF16))
        gb_ref[0] = jnp.sum(da_ref[0], axis=0, keepdims=True)

    return pl.pallas_call(
        body, name="ada_backward", grid=(DEPTH,),
        out_shape=[jax.ShapeDtypeStruct((DEPTH, D_MODEL, cols), F32), jax.ShapeDtypeStruct((DEPTH, 1, full), F32)],
        in_specs=[pl.BlockSpec((nb, D_MODEL), lambda l: (0, 0)),
                  pl.BlockSpec((1, nb, cols), lambda l: (l, 0, 0)),
                  pl.BlockSpec((1, nb, full), lambda l: (l, 0, 0))],
        out_specs=[pl.BlockSpec((1, D_MODEL, cols), lambda l: (l, 0, 0)),
                   pl.BlockSpec((1, 1, full), lambda l: (l, 0, 0))],
        compiler_params=_params(),
    )(c_all, d_ada_cols, d_ada_all)


def _rms(xv):
    return lax.rsqrt(jnp.mean(xv * xv, axis=-1, keepdims=True) + EPS)


def _normmod_matmul(x, gnorm, scale1p, shift, w_t, seq, name, comm=None):
    tokens, width = x.shape[0], w_t.shape[0] // 2
    tm = _tile_rows(seq)
    per_seq = seq // tm
    n_chunks = width // MXU_N

    def body(x_ref, g_ref, sc_ref, sh_ref, w_ref, h_ref, act_ref, silu_ref, dact_ref):
        xv = x_ref[...]
        h = (xv * _rms(xv) * g_ref[...]) * sc_ref[0] + sh_ref[0]
        h_ref[...] = h.astype(BF16)
        for ck in range(n_chunks):
            cs = slice(ck * MXU_N, (ck + 1) * MXU_N)
            g = _dot_nt(h_ref[...], w_ref[cs, :])
            u = _dot_nt(h_ref[...], w_ref[width + ck * MXU_N:width + (ck + 1) * MXU_N, :])
            sig = _sigmoid(g)
            silu = g * sig
            act_ref[:, cs] = (silu * u).astype(BF16)
            silu_ref[:, cs] = silu.astype(BF16)
            dact_ref[:, cs] = (u * (sig + silu * (1.0 - sig))).astype(BF16)

    per_batch = pl.BlockSpec((1, 1, D_MODEL), lambda i: (i // per_seq, 0, 0))
    outs, got = _call(
        body, name=name, grid=(tokens // tm,),
        out_shape=[jax.ShapeDtypeStruct((tokens, D_MODEL), BF16)] + [jax.ShapeDtypeStruct((tokens, width), BF16)] * 3,
        in_specs=[pl.BlockSpec((tm, D_MODEL), lambda i: (i, 0)), _resident((1, D_MODEL)), per_batch, per_batch,
                  _resident(w_t.shape)],
        out_specs=[pl.BlockSpec((tm, D_MODEL), lambda i: (i, 0))] + [pl.BlockSpec((tm, width), lambda i: (i, 0))] * 3,
        operands=(x, gnorm, scale1p, shift, w_t), parallel=True, comm=comm)
    return (*outs, got)


def _matmul_residual(src, w, x, gate, scale, seq, name, comm=None):
    tokens, k_dim = x.shape[0], w.shape[0]
    tm = _tile_rows(seq)
    per_seq = seq // tm

    def body(s_ref, w_ref, x_ref, gate_ref, xo_ref, f_ref):
        f = _dot(s_ref[...], w_ref[...])
        f_ref[...] = f.astype(BF16)
        xo_ref[...] = x_ref[...] + (scale * gate_ref[0]) * f

    (x_out, f), got = _call(
        body, name=name, grid=(tokens // tm,),
        out_shape=[jax.ShapeDtypeStruct((tokens, D_MODEL), F32), jax.ShapeDtypeStruct((tokens, D_MODEL), BF16)],
        in_specs=[pl.BlockSpec((tm, k_dim), lambda i: (i, 0)), _resident(w.shape),
                  pl.BlockSpec((tm, D_MODEL), lambda i: (i, 0)),
                  pl.BlockSpec((1, 1, D_MODEL), lambda i: (i // per_seq, 0, 0))],
        out_specs=[pl.BlockSpec((tm, D_MODEL), lambda i: (i, 0))] * 2,
        operands=(src, w, x, gate), parallel=True, comm=comm)
    return x_out, f, got


def _loss_tile(xv, target, gn):
    r = _rms(xv)
    xn = xv * r
    err = xn * gn - target
    loss = (0.5 / D_MODEL) * jnp.sum(err * err, axis=0, keepdims=True)
    dyv = err * (1.0 / D_MODEL)
    dg = jnp.sum(dyv * xn, axis=0, keepdims=True)
    dxn = dyv * gn
    dx = r * (dxn - xn * jnp.mean(dxn * xn, axis=-1, keepdims=True))
    return loss, dx, dg


def _ffn_forward(x, gnorm, scale1p, shift, w_gu_t, w_down, gate, scale, seq, name, loss_head=None, comm=None):
    tokens, width = x.shape[0], w_down.shape[0]
    tm = _tile_rows(seq)
    per_seq = seq // tm
    n_chunks = width // MXU_N

    def body(x_ref, g_ref, sc_ref, sh_ref, wgu_ref, wd_ref, gate_ref, *rest):
        if loss_head:
            t_ref, gf_ref, h_ref, act_ref, silu_ref, dact_ref, xo_ref, f_ref, dgf_ref, loss_ref = rest
        else:
            h_ref, act_ref, silu_ref, dact_ref, xo_ref, f_ref = rest
        xv = x_ref[...]
        h = (xv * _rms(xv) * g_ref[...]) * sc_ref[0] + sh_ref[0]
        h_ref[...] = h.astype(BF16)
        for ck in range(n_chunks):
            cs = slice(ck * MXU_N, (ck + 1) * MXU_N)
            g = _dot_nt(h_ref[...], wgu_ref[cs, :])
            u = _dot_nt(h_ref[...], wgu_ref[width + ck * MXU_N:width + (ck + 1) * MXU_N, :])
            sig = _sigmoid(g)
            silu = g * sig
            act_ref[:, cs] = (silu * u).astype(BF16)
            silu_ref[:, cs] = silu.astype(BF16)
            dact_ref[:, cs] = (u * (sig + silu * (1.0 - sig))).astype(BF16)
        f = _dot(act_ref[...], wd_ref[...])
        f_ref[...] = f.astype(BF16)
        x_out = xv + (scale * gate_ref[0]) * f
        if loss_head:
            i = pl.program_id(0)
            loss, dx, dg = _loss_tile(x_out, t_ref[...], gf_ref[...])
            xo_ref[...] = dx

            @pl.when(i == 0)
            def _():
                dgf_ref[...] = dg
                loss_ref[...] = loss

            @pl.when(i != 0)
            def _():
                dgf_ref[...] = dgf_ref[...] + dg
                loss_ref[...] = loss_ref[...] + loss
        else:
            xo_ref[...] = x_out

    row = lambda i: (i, 0)
    per_batch = pl.BlockSpec((1, 1, D_MODEL), lambda i: (i // per_seq, 0, 0))
    tile = lambda cols: pl.BlockSpec((tm, cols), row)
    wide = jax.ShapeDtypeStruct((tokens, width), BF16)
    fixed = pl.BlockSpec((1, D_MODEL), lambda i: (0, 0))
    vec = jax.ShapeDtypeStruct((1, D_MODEL), F32)
    outs, got = _call(
        body, name=name, grid=(tokens // tm,),
        out_shape=[jax.ShapeDtypeStruct((tokens, D_MODEL), BF16), wide, wide, wide,
                   jax.ShapeDtypeStruct((tokens, D_MODEL), F32), jax.ShapeDtypeStruct((tokens, D_MODEL), BF16)]
        + ([vec, vec] if loss_head else []),
        in_specs=[tile(D_MODEL), _resident((1, D_MODEL)), per_batch, per_batch, _resident(w_gu_t.shape),
                  _resident(w_down.shape), per_batch] + ([tile(D_MODEL), _resident((1, D_MODEL))] if loss_head else []),
        out_specs=[tile(D_MODEL), tile(width), tile(width), tile(width), tile(D_MODEL), tile(D_MODEL)]
        + ([fixed, fixed] if loss_head else []),
        operands=(x, gnorm, scale1p, shift, w_gu_t, w_down, gate) + (tuple(loss_head) if loss_head else ()),
        parallel=not loss_head, comm=comm)
    return (*outs, got)


def _residual_backward(dy, gate, f, w, scale, silu, dact, seq, name, comm=None):
    tokens, k_dim = dy.shape[0], w.shape[0]
    batch = tokens // seq
    tm = _tile_rows(seq)
    per_seq = seq // tm
    n_chunks = k_dim // MXU_N

    def body(dy_ref, gate_ref, f_ref, silu_ref, dact_ref, w_ref, df_ref, dgate_ref, dgu_ref):
        i = pl.program_id(0)
        dy_v = dy_ref[...]
        df_ref[...] = ((scale * gate_ref[0]) * dy_v).astype(BF16)
        part = scale * jnp.sum(dy_v * f_ref[...].astype(F32), axis=0, keepdims=True)
        for ck in range(n_chunks):
            cs = slice(ck * MXU_N, (ck + 1) * MXU_N)
            cu = slice(k_dim + ck * MXU_N, k_dim + (ck + 1) * MXU_N)
            da = _dot_nt(df_ref[...], w_ref[cs, :])
            dgu_ref[:, cs] = (da * dact_ref[:, cs].astype(F32)).astype(BF16)
            dgu_ref[:, cu] = (da * silu_ref[:, cs].astype(F32)).astype(BF16)

        @pl.when(i % per_seq == 0)
        def _():
            dgate_ref[0] = part

        @pl.when(i % per_seq != 0)
        def _():
            dgate_ref[0] = dgate_ref[0] + part

    row = lambda i: (i, 0)
    per_batch = pl.BlockSpec((1, 1, D_MODEL), lambda i: (i // per_seq, 0, 0))
    tile = lambda cols: pl.BlockSpec((tm, cols), row)
    outs, got = _call(
        body, name=name, grid=(tokens // tm,),
        out_shape=[jax.ShapeDtypeStruct((tokens, D_MODEL), BF16), jax.ShapeDtypeStruct((batch, 1, D_MODEL), F32),
                   jax.ShapeDtypeStruct((tokens, 2 * k_dim), BF16)],
        in_specs=[tile(D_MODEL), per_batch, tile(D_MODEL), tile(k_dim), tile(k_dim), _resident(w.shape)],
        out_specs=[tile(D_MODEL), per_batch, tile(2 * k_dim)],
        operands=(dy, gate, f, silu, dact, w), comm=comm)
    return (*outs, got)


def _matmul_normmod_backward(dsrc, w_t, x, dy, gnorm, scale1p, seq, name, comm=None):
    tokens, k_dim = dsrc.shape
    batch = tokens // seq
    tm = _tile_rows(seq)
    per_seq = seq // tm

    def body(ds_ref, w_ref, x_ref, dy_ref, g_ref, sc_ref, dx_ref, dsh_ref, dsc_ref, dg_ref):
        i = pl.program_id(0)
        dh = _dot(ds_ref[...], w_ref[...])
        xv = x_ref[...]
        r = _rms(xv)
        xn = xv * r
        gn = g_ref[...]
        dsh = jnp.sum(dh, axis=0, keepdims=True)
        dsc = jnp.sum(dh * (xn * gn), axis=0, keepdims=True)
        dhn = dh * sc_ref[0]
        dg = jnp.sum(dhn * xn, axis=0, keepdims=True)
        dxn = dhn * gn
        dx_ref[...] = dy_ref[...] + r * (dxn - xn * jnp.mean(dxn * xn, axis=-1, keepdims=True))

        @pl.when(i % per_seq == 0)
        def _():
            dsh_ref[0] = dsh
            dsc_ref[0] = dsc

        @pl.when(i % per_seq != 0)
        def _():
            dsh_ref[0] = dsh_ref[0] + dsh
            dsc_ref[0] = dsc_ref[0] + dsc

        @pl.when(i == 0)
        def _():
            dg_ref[...] = dg

        @pl.when(i != 0)
        def _():
            dg_ref[...] = dg_ref[...] + dg

    row = lambda i: (i, 0)
    per_batch = pl.BlockSpec((1, 1, D_MODEL), lambda i: (i // per_seq, 0, 0))
    outs, got = _call(
        body, name=name, grid=(tokens // tm,),
        out_shape=[jax.ShapeDtypeStruct((tokens, D_MODEL), F32), jax.ShapeDtypeStruct((batch, 1, D_MODEL), F32),
                   jax.ShapeDtypeStruct((batch, 1, D_MODEL), F32), jax.ShapeDtypeStruct((1, D_MODEL), F32)],
        in_specs=[pl.BlockSpec((tm, k_dim), row), _resident(w_t.shape), pl.BlockSpec((tm, D_MODEL), row),
                  pl.BlockSpec((tm, D_MODEL), row), _resident((1, D_MODEL)), per_batch],
        out_specs=[pl.BlockSpec((tm, D_MODEL), row), per_batch, per_batch, pl.BlockSpec((1, D_MODEL), lambda i: (0, 0))],
        operands=(dsrc, w_t, x, dy, gnorm, scale1p), comm=comm)
    return (*outs, got)


def _ffn_backward(dy, gate, f, silu, dact, w_down, w_gu_t, x, gnorm, scale1p, scale, seq, name, comm=None):
    tokens, k_dim = dy.shape[0], w_down.shape[0]
    batch = tokens // seq
    tm = _tile_rows(seq)
    per_seq = seq // tm
    n_chunks = k_dim // MXU_N

    def body(dy_ref, gate_ref, f_ref, silu_ref, dact_ref, wd_ref, wgu_ref, x_ref, g_ref, sc_ref,
             df_ref, dgate_ref, dgu_ref, dx_ref, dsh_ref, dsc_ref, dg_ref):
        i = pl.program_id(0)
        dy_v = dy_ref[...]
        df_ref[...] = ((scale * gate_ref[0]) * dy_v).astype(BF16)
        dgate = scale * jnp.sum(dy_v * f_ref[...].astype(F32), axis=0, keepdims=True)
        for ck in range(n_chunks):
            cs = slice(ck * MXU_N, (ck + 1) * MXU_N)
            cu = slice(k_dim + ck * MXU_N, k_dim + (ck + 1) * MXU_N)
            da = _dot_nt(df_ref[...], wd_ref[cs, :])
            dgu_ref[:, cs] = (da * dact_ref[:, cs].astype(F32)).astype(BF16)
            dgu_ref[:, cu] = (da * silu_ref[:, cs].astype(F32)).astype(BF16)
        dh = _dot(dgu_ref[...], wgu_ref[...])
        xv = x_ref[...]
        r = _rms(xv)
        xn = xv * r
        gn = g_ref[...]
        dsh = jnp.sum(dh, axis=0, keepdims=True)
        dsc = jnp.sum(dh * (xn * gn), axis=0, keepdims=True)
        dhn = dh * sc_ref[0]
        dg = jnp.sum(dhn * xn, axis=0, keepdims=True)
        dxn = dhn * gn
        dx_ref[...] = dy_v + r * (dxn - xn * jnp.mean(dxn * xn, axis=-1, keepdims=True))

        @pl.when(i % per_seq == 0)
        def _():
            dgate_ref[0] = dgate
            dsh_ref[0] = dsh
            dsc_ref[0] = dsc

        @pl.when(i % per_seq != 0)
        def _():
            dgate_ref[0] = dgate_ref[0] + dgate
            dsh_ref[0] = dsh_ref[0] + dsh
            dsc_ref[0] = dsc_ref[0] + dsc

        @pl.when(i == 0)
        def _():
            dg_ref[...] = dg

        @pl.when(i != 0)
        def _():
            dg_ref[...] = dg_ref[...] + dg

    row = lambda i: (i, 0)
    per_batch = pl.BlockSpec((1, 1, D_MODEL), lambda i: (i // per_seq, 0, 0))
    tile = lambda width: pl.BlockSpec((tm, width), row)
    vec = jax.ShapeDtypeStruct((batch, 1, D_MODEL), F32)
    outs, got = _call(
        body, name=name, grid=(tokens // tm,),
        out_shape=[jax.ShapeDtypeStruct((tokens, D_MODEL), BF16), vec, jax.ShapeDtypeStruct((tokens, 2 * k_dim), BF16),
                   jax.ShapeDtypeStruct((tokens, D_MODEL), F32), vec, vec, jax.ShapeDtypeStruct((1, D_MODEL), F32)],
        in_specs=[tile(D_MODEL), per_batch, tile(D_MODEL), tile(k_dim), tile(k_dim), _resident(w_down.shape),
                  _resident(w_gu_t.shape), tile(D_MODEL), _resident((1, D_MODEL)), per_batch],
        out_specs=[tile(D_MODEL), per_batch, tile(2 * k_dim), tile(D_MODEL), per_batch, per_batch,
                   pl.BlockSpec((1, D_MODEL), lambda i: (0, 0))],
        operands=(dy, gate, f, silu, dact, w_down, w_gu_t, x, gnorm, scale1p), comm=comm)
    return (*outs, got)


def _weight_grad(a, b, seq, name, comm=None):
    tokens, n_out = a.shape
    tn = MXU_N

    def body(a_ref, b_ref, o_ref):
        o_ref[...] = _dot_tn(a_ref[...], b_ref[...]).astype(BF16)

    (out,), got = _call(
        body, name=name, grid=(n_out // tn,),
        out_shape=[jax.ShapeDtypeStruct((n_out, D_MODEL), BF16)],
        in_specs=[pl.BlockSpec((tokens, tn), lambda j: (0, j)), _resident((tokens, D_MODEL))],
        out_specs=[pl.BlockSpec((tn, D_MODEL), lambda j: (j, 0))],
        operands=(a, b), comm=comm)
    return out, got


def _group_mean(v, bd):
    hi = v.astype(BF16)
    lo = (v - hi.astype(F32)).astype(BF16)
    blocks = [slice(k * MXU_N, (k + 1) * MXU_N) for k in range(v.shape[1] // MXU_N)]
    return jnp.concatenate([_dot(hi[:, b], bd) + _dot(lo[:, b], bd) for b in blocks], axis=1)


def _sgu_forward(pm_ref, wm_ref, bias_ref, lng_ref, lnb_ref, bd_ref, mixed_scr, n_sub):
    ua = pm_ref[:, 0:D_A].astype(F32)
    va = pm_ref[:, D_A:2 * D_A].astype(F32)
    u_act = _gelu(ua)
    v_act = _gelu(va)
    bd = bd_ref[...]
    vc = v_act - _group_mean(v_act, bd)
    rstd = lax.rsqrt(_group_mean(vc * vc, bd) + EPS)
    vhat = vc * rstd
    vln = vhat * lng_ref[...] + lnb_ref[...]
    left = lax.broadcasted_iota(jnp.int32, (CHUNK, LANES), 1) < HEAD_DIM
    for q in range(n_sub):
        rows = slice(q * CHUNK, (q + 1) * CHUNK)
        for p in range(N_HEADS // 2):
            cols = slice(p * LANES, (p + 1) * LANES)
            vp = vln[rows, cols]
            stacked = jnp.concatenate([jnp.where(left, vp, 0.0), jnp.where(left, 0.0, vp)], axis=0).astype(BF16)
            mixed_scr[rows, cols] = _dot(wm_ref[p], stacked) + bias_ref[:, cols]
    return ua, va, u_act, vhat, rstd, vln


def _halo_specs(tm, tokens, width):
    prev = pl.BlockSpec((HALO, width), lambda i: (jnp.maximum(i * (tm // HALO) - 1, 0), 0))
    nxt = pl.BlockSpec((HALO, width), lambda i: (jnp.minimum((i + 1) * (tm // HALO), tokens // HALO - 1), 0))
    return prev, nxt


def _mixer_forward(x, gnorm, scale1p, shift, w_in_t, gate, w_out, wm, bias_full, lng, lnb, convw, og, bd, seq, name, comm=None):
    tokens = x.shape[0]
    tm = _tile_rows(seq)
    per_seq = seq // tm
    n_sub = tm // CHUNK

    def body(x_ref, xp_ref, g_ref, sc_ref, sh_ref, win_ref, gate_ref, wo_ref, wm_ref, bias_ref, lng_ref, lnb_ref, cw_ref,
             og_ref, bd_ref, h_ref, pm_ref, y_ref, xo_ref, o_ref, mixed_scr):
        i = pl.program_id(0)
        first = (i % per_seq) == 0
        xv = x_ref[...]
        h_ref[...] = ((xv * _rms(xv) * g_ref[...]) * sc_ref[0] + sh_ref[0]).astype(BF16)
        for ck in range(D_PROJ // MXU_N):
            cs = slice(ck * MXU_N, (ck + 1) * MXU_N)
            pm_ref[:, cs] = _dot_nt(h_ref[...], win_ref[cs, :]).astype(BF16)
        xp = xp_ref[...]
        hp = ((xp * _rms(xp) * g_ref[...]) * sc_ref[0] + sh_ref[0]).astype(BF16)
        gates_prev = _dot_nt(hp, win_ref[3 * D_A:5 * D_A, :]).astype(BF16).astype(F32)

        _, _, u_act, _, _, _ = _sgu_forward(pm_ref, wm_ref, bias_ref, lng_ref, lnb_ref, bd_ref, mixed_scr, n_sub)
        ya = u_act * mixed_scr[...]
        y_ref[:, 0:D_A] = (ya * _rms(ya) * og_ref[:, 0:D_A]).astype(BF16)

        bg = pm_ref[:, 2 * D_A:3 * D_A].astype(F32)
        z = pm_ref[:, 3 * D_A:4 * D_A].astype(F32) * pm_ref[:, 4 * D_A:5 * D_A].astype(F32)
        zp = jnp.where(first, 0.0, gates_prev[:, 0:D_A] * gates_prev[:, D_A:2 * D_A])
        zext = jnp.concatenate([zp, z], axis=0)
        z1 = pltpu.roll(zext, 1, 0)[HALO:]
        z2 = pltpu.roll(zext, 2, 0)[HALO:]
        conv = cw_ref[0:1, :] * z2 + cw_ref[1:2, :] * z1 + cw_ref[2:3, :] * z
        yb = bg * conv
        y_ref[:, D_A:2 * D_A] = (yb * _rms(yb) * og_ref[:, D_A:2 * D_A]).astype(BF16)

        f = _dot(y_ref[...], wo_ref[...])
        o_ref[...] = f.astype(BF16)
        xo_ref[...] = xv + gate_ref[0] * f

    prev, _ = _halo_specs(tm, tokens, D_MODEL)
    tile = pl.BlockSpec((tm, D_MODEL), lambda i: (i, 0))
    per_batch = pl.BlockSpec((1, 1, D_MODEL), lambda i: (i // per_seq, 0, 0))
    bf = lambda cols: jax.ShapeDtypeStruct((tokens, cols), BF16)
    outs, got = _call(
        body, name=name, grid=(tokens // tm,),
        out_shape=[bf(D_MODEL), bf(D_PROJ), bf(D_MODEL), jax.ShapeDtypeStruct((tokens, D_MODEL), F32), bf(D_MODEL)],
        in_specs=[tile, prev, _resident((1, D_MODEL)), per_batch, per_batch, _resident(w_in_t.shape), per_batch,
                  _resident(w_out.shape), _resident(wm.shape), _resident(bias_full.shape), _resident(lng.shape),
                  _resident(lnb.shape), _resident(convw.shape), _resident(og.shape), _resident(bd.shape)],
        out_specs=[tile, pl.BlockSpec((tm, D_PROJ), lambda i: (i, 0)), tile, tile, tile],
        scratch_shapes=[pltpu.VMEM((tm, D_A), F32)],
        operands=(x, x, gnorm, scale1p, shift, w_in_t, gate, w_out, wm, bias_full, lng, lnb, convw, og, bd),
        parallel=True, comm=comm)
    return (*outs, got)


def _mixer_backward(proj, dx, gate, o, w_out, x, gnorm, scale1p, w_in_t, wm, bias_full, lng, lnb, convw, og, bd, wm_rows,
                    causal, seq, name, comm=None):
    tokens = proj.shape[0]
    batch = tokens // seq
    tm = _tile_rows(seq)
    per_seq = seq // tm
    n_sub = tm // CHUNK
    ext = tm + 2 * HALO

    def body(pm_ref, pp_ref, pn_ref, dx_ref, dxn_ref, gate_ref, o_ref, wo_ref, x_ref, g_ref, sc_ref, win_ref, wm_ref, bias_ref,
             lng_ref, lnb_ref, cw_ref, og_ref, bd_ref, wmr_ref, causal_ref, do_ref, dgate_ref, dp_ref, dxo_ref, dsh_ref, dsc_ref,
             dgn_ref,
             dog_ref, dcw_ref, dlng_ref, dlnb_ref, dbias_ref, dwm_ref, mixed_scr, dvln_scr, dy_scr):
        i = pl.program_id(0)
        first = (i % per_seq) == 0
        last = (i % per_seq) == per_seq - 1

        @pl.when(i == 0)
        def _():
            dog_ref[...] = jnp.zeros_like(dog_ref)
            dcw_ref[...] = jnp.zeros_like(dcw_ref)
            dlng_ref[...] = jnp.zeros_like(dlng_ref)
            dlnb_ref[...] = jnp.zeros_like(dlnb_ref)
            dbias_ref[...] = jnp.zeros_like(dbias_ref)
            dwm_ref[...] = jnp.zeros_like(dwm_ref)

        dx_v = dx_ref[...]
        do_ref[...] = (gate_ref[0] * dx_v).astype(BF16)
        dgate = jnp.sum(dx_v * o_ref[...].astype(F32), axis=0, keepdims=True)
        dy_scr[...] = _dot_nt(do_ref[...], wo_ref[...])
        dyn_conv = _dot_nt((gate_ref[0] * dxn_ref[...]).astype(BF16), wo_ref[D_A:2 * D_A, :])

        ua, va, u_act, vhat, rstd, vln = _sgu_forward(pm_ref, wm_ref, bias_ref, lng_ref, lnb_ref, bd_ref, mixed_scr, n_sub)
        mixed = mixed_scr[...]
        ya = u_act * mixed
        ra = _rms(ya)
        yhat = ya * ra
        dya_in = dy_scr[:, 0:D_A]
        dog_ref[:, 0:D_A] = dog_ref[:, 0:D_A] + jnp.sum(dya_in * yhat, axis=0, keepdims=True)
        dyh = dya_in * og_ref[:, 0:D_A]
        dya = ra * (dyh - yhat * jnp.mean(dyh * yhat, axis=-1, keepdims=True))
        d_u = dya * mixed
        d_mixed = dya * u_act
        left = lax.broadcasted_iota(jnp.int32, (CHUNK, LANES), 1) < HEAD_DIM
        dbias = jnp.zeros((CHUNK, D_A), F32)
        for q in range(n_sub):
            rows = slice(q * CHUNK, (q + 1) * CHUNK)
            dbias = dbias + d_mixed[rows, :]
            for p in range(N_HEADS // 2):
                cols = slice(p * LANES, (p + 1) * LANES)
                dm = d_mixed[rows, cols]
                stacked = jnp.concatenate([jnp.where(left, dm, 0.0), jnp.where(left, 0.0, dm)], axis=0).astype(BF16)
                dw = _dot_nt(stacked, vln[rows, cols].astype(BF16))
                dwm_ref[2 * p] = dwm_ref[2 * p] + causal_ref[...] * dw[0:CHUNK]
                dwm_ref[2 * p + 1] = dwm_ref[2 * p + 1] + causal_ref[...] * dw[CHUNK:2 * CHUNK]
                dvln_scr[rows, cols] = _dot_tn(wmr_ref[p], stacked)
        dbias_ref[...] = dbias_ref[...] + dbias
        dvln = dvln_scr[...]
        dlng_ref[...] = dlng_ref[...] + jnp.sum(dvln * vhat, axis=0, keepdims=True)
        dlnb_ref[...] = dlnb_ref[...] + jnp.sum(dvln, axis=0, keepdims=True)
        dvh = dvln * lng_ref[...]
        bd = bd_ref[...]
        d_v = rstd * (dvh - _group_mean(dvh, bd) - vhat * _group_mean(dvh * vhat, bd))
        dp_ref[:, 0:D_A] = (d_u * _gelu_grad(ua)).astype(BF16)
        dp_ref[:, D_A:2 * D_A] = (d_v * _gelu_grad(va)).astype(BF16)
        dh_a = _dot(dp_ref[:, 0:2 * D_A], win_ref[0:2 * D_A, :])

        def ext_cols(lo):
            cs = slice(lo, lo + D_A)
            return jnp.concatenate([pp_ref[:, cs], pm_ref[:, cs], pn_ref[:, cs]], axis=0).astype(F32)

        bg, cg, xb = ext_cols(2 * D_A), ext_cols(3 * D_A), ext_cols(4 * D_A)
        row = lax.broadcasted_iota(jnp.int32, (ext, D_A), 0)
        z = jnp.where(jnp.logical_and(first, row < HALO), 0.0, cg * xb)
        z1 = pltpu.roll(z, 1, 0)
        z2 = pltpu.roll(z, 2, 0)
        w0, w1, w2 = cw_ref[0:1, :], cw_ref[1:2, :], cw_ref[2:3, :]
        conv = w0 * z2 + w1 * z1 + w2 * z
        yb = bg * conv
        rb = _rms(yb)
        yhb = yb * rb
        dyn = jnp.where(last, 0.0, dyn_conv)
        dyb_in = jnp.concatenate([jnp.zeros((HALO, D_A), F32), dy_scr[:, D_A:2 * D_A], dyn], axis=0)
        dyhb = dyb_in * og_ref[:, D_A:2 * D_A]
        dyb = rb * (dyhb - yhb * jnp.mean(dyhb * yhb, axis=-1, keepdims=True))
        d_conv = dyb * bg
        dz = w2 * d_conv + w1 * pltpu.roll(d_conv, ext - 1, 0) + w0 * pltpu.roll(d_conv, ext - 2, 0)
        main = slice(HALO, HALO + tm)
        dp_ref[:, 2 * D_A:3 * D_A] = (dyb * conv)[main].astype(BF16)
        dp_ref[:, 3 * D_A:4 * D_A] = (dz * xb)[main].astype(BF16)
        dp_ref[:, 4 * D_A:5 * D_A] = (dz * cg)[main].astype(BF16)
        dog_ref[:, D_A:2 * D_A] = dog_ref[:, D_A:2 * D_A] + jnp.sum((dyb_in * yhb)[main], axis=0, keepdims=True)
        dcm = d_conv[main]
        dcw_ref[0:1, :] = dcw_ref[0:1, :] + jnp.sum(dcm * z2[main], axis=0, keepdims=True)
        dcw_ref[1:2, :] = dcw_ref[1:2, :] + jnp.sum(dcm * z1[main], axis=0, keepdims=True)
        dcw_ref[2:3, :] = dcw_ref[2:3, :] + jnp.sum(dcm * z[main], axis=0, keepdims=True)

        dh = dh_a + _dot(dp_ref[:, 2 * D_A:5 * D_A], win_ref[2 * D_A:5 * D_A, :])
        xv = x_ref[...]
        r = _rms(xv)
        xn = xv * r
        gn = g_ref[...]
        dsh = jnp.sum(dh, axis=0, keepdims=True)
        dsc = jnp.sum(dh * (xn * gn), axis=0, keepdims=True)
        dhn = dh * sc_ref[0]
        dgn = jnp.sum(dhn * xn, axis=0, keepdims=True)
        dxn = dhn * gn
        dxo_ref[...] = dx_v + r * (dxn - xn * jnp.mean(dxn * xn, axis=-1, keepdims=True))

        @pl.when(first)
        def _():
            dgate_ref[0] = dgate
            dsh_ref[0] = dsh
            dsc_ref[0] = dsc

        @pl.when(jnp.logical_not(first))
        def _():
            dgate_ref[0] = dgate_ref[0] + dgate
            dsh_ref[0] = dsh_ref[0] + dsh
            dsc_ref[0] = dsc_ref[0] + dsc

        @pl.when(i == 0)
        def _():
            dgn_ref[...] = dgn

        @pl.when(i != 0)
        def _():
            dgn_ref[...] = dgn_ref[...] + dgn

    prev_p, next_p = _halo_specs(tm, tokens, D_PROJ)
    _, next_d = _halo_specs(tm, tokens, D_MODEL)
    fixed2 = lambda shape: pl.BlockSpec(shape, lambda i: (0, 0))
    tile = pl.BlockSpec((tm, D_MODEL), lambda i: (i, 0))
    per_batch = pl.BlockSpec((1, 1, D_MODEL), lambda i: (i // per_seq, 0, 0))
    vec = jax.ShapeDtypeStruct((batch, 1, D_MODEL), F32)
    outs, got = _call(
        body, name=name, grid=(tokens // tm,),
        out_shape=[jax.ShapeDtypeStruct((tokens, D_MODEL), BF16), vec, jax.ShapeDtypeStruct((tokens, D_PROJ), BF16),
                   jax.ShapeDtypeStruct((tokens, D_MODEL), F32), vec, vec, jax.ShapeDtypeStruct((1, D_MODEL), F32),
                   jax.ShapeDtypeStruct((1, D_MODEL), F32), jax.ShapeDtypeStruct((8, D_A), F32),
                   jax.ShapeDtypeStruct((1, D_A), F32), jax.ShapeDtypeStruct((1, D_A), F32),
                   jax.ShapeDtypeStruct((CHUNK, D_A), F32), jax.ShapeDtypeStruct((N_HEADS, CHUNK, CHUNK), F32)],
        in_specs=[pl.BlockSpec((tm, D_PROJ), lambda i: (i, 0)), prev_p, next_p, tile, next_d, per_batch, tile,
                  _resident(w_out.shape), tile, _resident((1, D_MODEL)), per_batch, _resident(w_in_t.shape),
                  _resident(wm.shape), _resident(bias_full.shape), _resident(lng.shape), _resident(lnb.shape),
                  _resident(convw.shape), _resident(og.shape), _resident(bd.shape), _resident(wm_rows.shape),
                  _resident(causal.shape)],
        out_specs=[tile, per_batch, pl.BlockSpec((tm, D_PROJ), lambda i: (i, 0)), tile, per_batch, per_batch,
                   fixed2((1, D_MODEL)), fixed2((1, D_MODEL)), fixed2((8, D_A)), fixed2((1, D_A)), fixed2((1, D_A)),
                   fixed2((CHUNK, D_A)), pl.BlockSpec((N_HEADS, CHUNK, CHUNK), lambda i: (0, 0, 0))],
        scratch_shapes=[pltpu.VMEM((tm, D_A), F32), pltpu.VMEM((tm, D_A), F32), pltpu.VMEM((tm, D_MODEL), F32)],
        operands=(proj, proj, proj, dx, dx, gate, o, w_out, x, gnorm, scale1p, w_in_t, wm, bias_full, lng, lnb, convw, og, bd,
                  wm_rows, causal), comm=comm)
    return (*outs, got)


def _adamw_update(wv, gv, mv, vv):
    nm = ADAM_B1 * mv + (1.0 - ADAM_B1) * gv
    nv = ADAM_B2 * vv + (1.0 - ADAM_B2) * (gv * gv)
    m_hat = nm / (1.0 - ADAM_B1 ** ADAM_STEP)
    v_hat = nv / (1.0 - ADAM_B2 ** ADAM_STEP)
    return -ADAM_LR * (m_hat / (jnp.sqrt(v_hat) + ADAM_EPS) + ADAM_WD * wv), nm, nv


def _adamw_rows(recv, w, m, v, name):
    depth, rows, cols = w.shape
    tr = rows // 2
    last = rows // tr - 1

    def body(*refs):
        r_refs, (w_ref, m_ref, v_ref, g_ref, d_ref, nm_ref, nv_ref) = refs[:depth], refs[depth:]
        for l in range(depth):
            @pl.when(pl.program_id(0) == l)
            def _(r_ref=r_refs[l]):
                acc = r_ref[0].astype(F32)
                for d in range(1, N_DEV):
                    acc = acc + r_ref[d].astype(F32)
                g_ref[0] = acc
                d_ref[0], nm_ref[0], nv_ref[0] = _adamw_update(w_ref[0], acc, m_ref[0], v_ref[0])

    def slots(l):
        return pl.BlockSpec((N_DEV, tr, cols), lambda ll, i: (0, jnp.where(ll == l, i, jnp.where(ll < l, 0, last)), 0))

    spec = pl.BlockSpec((1, tr, cols), lambda ll, i: (ll, i, 0))
    return pl.pallas_call(
        body, name=name, grid=(depth, rows // tr),
        out_shape=[jax.ShapeDtypeStruct((depth, rows, cols), F32)] * 4,
        in_specs=[slots(l) for l in range(depth)] + [spec] * 3, out_specs=[spec] * 4,
        compiler_params=_params(2),
    )(*recv, w, m, v)


def _adamw(w, g, m, v, name):
    rows, cols = w.shape
    tr = max(t for t in range(8, 513, 8) if rows % t == 0)

    def body(w_ref, g_ref, m_ref, v_ref, d_ref, nm_ref, nv_ref):
        d_ref[...], nm_ref[...], nv_ref[...] = _adamw_update(w_ref[...], g_ref[...], m_ref[...], v_ref[...])

    spec = pl.BlockSpec((tr, cols), lambda i: (i, 0))
    return pl.pallas_call(
        body, name=name, grid=(rows // tr,),
        out_shape=[jax.ShapeDtypeStruct((rows, cols), F32)] * 3,
        in_specs=[spec] * 4, out_specs=[spec] * 3,
        compiler_params=_params(parallel=True),
    )(w, g, m, v)


def _adamw_many(ws, gs, ms, vs, name):
    n = len(ws)
    two_d = lambda a: a.reshape(-1, a.shape[-1])

    def body(*refs):
        w_refs, g_refs, m_refs, v_refs = refs[:n], refs[n:2 * n], refs[2 * n:3 * n], refs[3 * n:4 * n]
        d_refs, nm_refs, nv_refs = refs[4 * n:5 * n], refs[5 * n:6 * n], refs[6 * n:]
        for k in range(n):
            d_refs[k][...], nm_refs[k][...], nv_refs[k][...] = _adamw_update(
                w_refs[k][...], g_refs[k][...], m_refs[k][...], v_refs[k][...])

    flat = [two_d(a) for a in ws]
    outs = pl.pallas_call(
        body, name=name, out_shape=[jax.ShapeDtypeStruct(a.shape, F32) for a in flat] * 3,
        in_specs=[pl.BlockSpec(memory_space=pltpu.VMEM)] * (4 * n),
        out_specs=[pl.BlockSpec(memory_space=pltpu.VMEM)] * (3 * n),
        compiler_params=pltpu.CompilerParams(vmem_limit_bytes=VMEM_LIMIT),
    )(*flat, *[two_d(a) for a in gs], *[two_d(a) for a in ms], *[two_d(a) for a in vs])
    shaped = [o.reshape(ws[k % n].shape) for k, o in enumerate(outs)]
    return shaped[:n], shaped[n:2 * n], shaped[2 * n:]


def _adamw_nd(w, g, m, v, name):
    shape = w.shape
    two_d = (-1, shape[-1])
    d, nm, nv = _adamw(w.reshape(two_d), g.reshape(two_d), m.reshape(two_d), v.reshape(two_d), name)
    return d.reshape(shape), nm.reshape(shape), nv.reshape(shape)


def kernel(x, c, ada_w, ada_b, norm_ffn1_g, ffn1_w_gu, ffn1_w_down, norm_mix_g, mix_w_in, sgu_ln_g, sgu_ln_b, sgu_w_s, sgu_b, conv_w, out_norm_g, mix_w_out, norm_ffn2_g, ffn2_w_gu, ffn2_w_down, final_norm_g, loss_target, m_ada_w, m_ada_b, m_norm_ffn1_g, m_ffn1_w_gu, m_ffn1_w_down, m_norm_mix_g, m_mix_w_in, m_sgu_ln_g, m_sgu_ln_b, m_sgu_w_s, m_sgu_b, m_conv_w, m_out_norm_g, m_mix_w_out, m_norm_ffn2_g, m_ffn2_w_gu, m_ffn2_w_down, m_final_norm_g, v_ada_w, v_ada_b, v_norm_ffn1_g, v_ffn1_w_gu, v_ffn1_w_down, v_norm_mix_g, v_mix_w_in, v_sgu_ln_g, v_sgu_ln_b, v_sgu_w_s, v_sgu_b, v_conv_w, v_out_norm_g, v_mix_w_out, v_norm_ffn2_g, v_ffn2_w_gu, v_ffn2_w_down, v_final_norm_g):
    batch, seq, _ = x.shape
    tokens = batch * seq
    me = 4 * lax.axis_index("x") + 2 * lax.axis_index("y") + lax.axis_index("c")
    weights = dict(ada_w=ada_w, ada_b=ada_b, norm_ffn1_g=norm_ffn1_g, ffn1_w_gu=ffn1_w_gu, ffn1_w_down=ffn1_w_down,
                   norm_mix_g=norm_mix_g, mix_w_in=mix_w_in, sgu_ln_g=sgu_ln_g, sgu_ln_b=sgu_ln_b, sgu_w_s=sgu_w_s,
                   sgu_b=sgu_b, conv_w=conv_w, out_norm_g=out_norm_g, mix_w_out=mix_w_out, norm_ffn2_g=norm_ffn2_g,
                   ffn2_w_gu=ffn2_w_gu, ffn2_w_down=ffn2_w_down, final_norm_g=final_norm_g)
    mom1 = dict(ada_w=m_ada_w, ada_b=m_ada_b, norm_ffn1_g=m_norm_ffn1_g, ffn1_w_gu=m_ffn1_w_gu,
                ffn1_w_down=m_ffn1_w_down, norm_mix_g=m_norm_mix_g, mix_w_in=m_mix_w_in, sgu_ln_g=m_sgu_ln_g,
                sgu_ln_b=m_sgu_ln_b, sgu_w_s=m_sgu_w_s, sgu_b=m_sgu_b, conv_w=m_conv_w, out_norm_g=m_out_norm_g,
                mix_w_out=m_mix_w_out, norm_ffn2_g=m_norm_ffn2_g, ffn2_w_gu=m_ffn2_w_gu, ffn2_w_down=m_ffn2_w_down,
                final_norm_g=m_final_norm_g)
    mom2 = dict(ada_w=v_ada_w, ada_b=v_ada_b, norm_ffn1_g=v_norm_ffn1_g, ffn1_w_gu=v_ffn1_w_gu,
                ffn1_w_down=v_ffn1_w_down, norm_mix_g=v_norm_mix_g, mix_w_in=v_mix_w_in, sgu_ln_g=v_sgu_ln_g,
                sgu_ln_b=v_sgu_ln_b, sgu_w_s=v_sgu_w_s, sgu_b=v_sgu_b, conv_w=v_conv_w, out_norm_g=v_out_norm_g,
                mix_w_out=v_mix_w_out, norm_ffn2_g=v_norm_ffn2_g, ffn2_w_gu=v_ffn2_w_gu, ffn2_w_down=v_ffn2_w_down,
                final_norm_g=v_final_norm_g)

    big = ("ffn1_w_gu", "ffn1_w_down", "mix_w_in", "mix_w_out", "ffn2_w_gu", "ffn2_w_down")
    transposed = ("ffn1_w_gu", "mix_w_in", "ffn2_w_gu")
    as_rows = lambda nm, a: jnp.swapaxes(a, 1, 2) if nm in transposed else a
    shard = {(l, nm): as_rows(nm, weights[nm])[l].astype(BF16) for l in range(DEPTH) for nm in big}
    full_w = {}

    def gather_of(keys):
        return keys, _GatherRows([shard[k] for k in keys])

    def landed(plan, got):
        full_w.update(zip(plan[0], got))

    ada_cols = ada_w.shape[2]
    ada_b_cols = lax.dynamic_slice_in_dim(ada_b, me * ada_cols, ada_cols, axis=1).reshape(DEPTH, 1, ada_cols)
    plan = gather_of([(0, "ffn1_w_gu")])
    c_dev, convw_dev, ada_recv, got = _prologue(
        jnp.pad(c, ((0, 8 - batch), (0, 0))), jnp.pad(conv_w.reshape(-1), (0, 8 * LANES - conv_w.size)).reshape(8, LANES),
        ada_w, ada_b_cols, plan[1])
    landed(plan, got)
    c_all = c_dev[:, :batch].reshape(N_DEV * batch, D_MODEL)
    convw_all = convw_dev.reshape(N_DEV, -1)[:, :conv_w.size].reshape((N_DEV,) + conv_w.shape)
    convw_full = jnp.transpose(convw_all, (1, 2, 0, 3)).reshape(DEPTH, 3, D_A)
    ada_mine = jnp.transpose(ada_recv[:, :, :batch, :], (1, 2, 0, 3)).reshape(DEPTH, batch, N_MOD * D_MODEL)
    mod = ada_mine.reshape(DEPTH, batch, N_MOD, 1, D_MODEL)

    causal = jnp.tril(jnp.ones((CHUNK, CHUNK), F32))
    bd = jnp.kron(jnp.eye(MXU_N // HEAD_DIM, dtype=F32), jnp.full((HEAD_DIM, HEAD_DIM), 1.0 / HEAD_DIM, F32)).astype(BF16)
    row_vec = lambda a: a.reshape(1, -1)

    hosted_gathers = {
        (0, "ffn1"): [(0, "ffn1_w_down"), (0, "mix_w_in"), (0, "mix_w_out")],
        (0, "ffn_down1"): [(0, "ffn2_w_down")],
        (0, "mix_in"): [(0, "ffn2_w_gu")],
        (0, "ffn2"): [(1, "ffn1_w_gu"), (1, "ffn1_w_down"), (1, "mix_w_in"), (1, "mix_w_out")],
        (1, "ffn1"): [(1, "ffn2_w_gu"), (1, "ffn2_w_down")],
    }

    def hosting(l, site):
        keys = hosted_gathers.get((l, site))
        return gather_of(keys) if keys else (None, None)

    xs = x.reshape(tokens, D_MODEL)
    saved = []
    for l in range(DEPTH):
        sh1, sc1, g1, sh2, sc2, g2, sh3, sc3, g3 = [mod[l, :, k] for k in range(N_MOD)]
        wm_masked = (sgu_w_s[l] * causal[None]).astype(BF16)
        mixer_consts = dict(
            wm=jnp.concatenate([wm_masked[0::2], wm_masked[1::2]], axis=2),
            bias_full=jnp.repeat(sgu_b[l].T, HEAD_DIM, axis=1),
            lng=row_vec(jnp.tile(sgu_ln_g[l], N_HEADS)), lnb=row_vec(jnp.tile(sgu_ln_b[l], N_HEADS)),
            convw=jnp.pad(convw_full[l], ((0, 5), (0, 0))), og=row_vec(out_norm_g[l]), bd=bd)
        x0 = xs
        plan = hosting(l, "ffn1")
        if l == 0:
            h1, a1, s1, w1, got = _normmod_matmul(x0, row_vec(norm_ffn1_g[l]), 1.0 + sc1, sh1, full_w[l, "ffn1_w_gu"], seq, "ffn_up", plan[1])
            landed(plan, got)
            plan = hosting(l, "ffn_down1")
            x1, f1, got = _matmul_residual(a1, full_w[l, "ffn1_w_down"], x0, g1, 0.5, seq, "ffn_down", plan[1])
        else:
            h1, a1, s1, w1, x1, f1, got = _ffn_forward(
                x0, row_vec(norm_ffn1_g[l]), 1.0 + sc1, sh1, full_w[l, "ffn1_w_gu"], full_w[l, "ffn1_w_down"], g1, 0.5, seq, "ffn_fwd",
                comm=plan[1])
        if got:
            landed(plan, got)
        plan = hosting(l, "mix_in")
        h2, proj, ymix, x2, o2, got = _mixer_forward(
            x1, row_vec(norm_mix_g[l]), 1.0 + sc2, sh2, full_w[l, "mix_w_in"], g2, full_w[l, "mix_w_out"], seq=seq,
            name="mixer_forward", comm=plan[1], **mixer_consts)
        if got:
            landed(plan, got)
        plan = hosting(l, "ffn2")
        if l + 1 < DEPTH:
            h3, a3, s3, w3, x3, f3, got = _ffn_forward(
                x2, row_vec(norm_ffn2_g[l]), 1.0 + sc3, sh3, full_w[l, "ffn2_w_gu"], full_w[l, "ffn2_w_down"], g3, 0.5, seq, "ffn_fwd",
                comm=plan[1])
        else:
            head = (loss_target.reshape(tokens, D_MODEL), row_vec(final_norm_g))
            h3, a3, s3, w3, x3, f3, d_final_g, loss_cols, got = _ffn_forward(
                x2, row_vec(norm_ffn2_g[l]), 1.0 + sc3, sh3, full_w[l, "ffn2_w_gu"], full_w[l, "ffn2_w_down"], g3, 0.5, seq, "ffn_fwd_loss",
                loss_head=head, comm=plan[1])
        if got:
            landed(plan, got)
        saved.append(dict(x0=x0, x1=x1, x2=x2, h1=h1, h2=h2, h3=h3, a1=a1, s1=s1, w1=w1, a3=a3, s3=s3, w3=w3, f1=f1, f3=f3, o2=o2, proj=proj,
                          ymix=ymix, mixer_consts=mixer_consts, wm_rows=wm_masked.reshape(N_HEADS // 2, 2 * CHUNK, CHUNK), sc=(1.0 + sc1, 1.0 + sc2, 1.0 + sc3), gates=(g1, g2, g3)))
        xs = x3

    dx = xs

    recv = {}
    small_grads = [None] * DEPTH
    d_mod = [None] * DEPTH

    mix_names = ("out_norm_g", "sgu_ln_g", "sgu_ln_b", "sgu_w_s", "sgu_b", "conv_w")
    late_names = ("norm_ffn1_g", "norm_mix_g", "norm_ffn2_g")

    def mix_parts(l):
        return [small_grads[l][nm] for nm in mix_names]

    def late_parts(l):
        return [small_grads[l][nm] for nm in late_names] + [d_mod[l]]

    pending = []

    def scatter_later(l, nm, grad):
        pending.append(((l, nm), _ScatterRows([grad])))

    def host():
        keys, parts = [k for k, _ in pending], [p for _, p in pending]
        pending.clear()
        return keys, (_Exchanges(parts) if parts else None)

    def hosted(keys, got):
        if got:
            recv.update(zip(keys, got))

    for l in reversed(range(DEPTH)):
        sv = saved[l]
        mc = sv["mixer_consts"]
        if l + 1 < DEPTH:
            pending.append((("late", l + 1), _GatherRows([_pack_small(late_parts(l + 1))])))
        keys, comm = host()
        df3, dg3, dgu3, dx2, dsh3, dsc3, dn3, got = _ffn_backward(
            dx, sv["gates"][2], sv["f3"], sv["s3"], sv["w3"], full_w[l, "ffn2_w_down"], full_w[l, "ffn2_w_gu"], sv["x2"],
            row_vec(norm_ffn2_g[l]), sv["sc"][2], 0.5, seq, "ffn_bwd", comm)
        hosted(keys, got)
        gw_down2, _ = _weight_grad(sv["a3"], df3, seq, "grad_w_down")
        scatter_later(l, "ffn2_w_down", gw_down2)
        keys, comm = host()
        gw_gu2, got = _weight_grad(dgu3, sv["h3"], seq, "grad_w_gu", comm)
        hosted(keys, got)
        scatter_later(l, "ffn2_w_gu", gw_gu2)
        keys, comm = host()
        do2, dg2, dproj, dx1, dsh2, dsc2, dn2, d_og, d_cw, d_lng, d_lnb, d_bias, d_wm, got = _mixer_backward(
            sv["proj"], dx2, sv["gates"][1], sv["o2"], full_w[l, "mix_w_out"], sv["x1"], row_vec(norm_mix_g[l]), sv["sc"][1],
            full_w[l, "mix_w_in"], wm_rows=sv["wm_rows"], causal=causal, seq=seq, name="mixer_backward", comm=comm, **mc)
        hosted(keys, got)
        small_grads[l] = dict(
            out_norm_g=d_og, sgu_ln_g=d_lng.reshape(N_HEADS, HEAD_DIM).sum(0), sgu_ln_b=d_lnb.reshape(N_HEADS, HEAD_DIM).sum(0),
            sgu_w_s=d_wm, sgu_b=d_bias.reshape(CHUNK, N_HEADS, HEAD_DIM).sum(-1).T, conv_w=d_cw[0:3])
        gw_out, _ = _weight_grad(sv["ymix"], do2, seq, "grad_w_out")
        scatter_later(l, "mix_w_out", gw_out)
        keys, comm = host()
        gw_in, got = _weight_grad(dproj, sv["h2"], seq, "grad_w_in", comm)
        hosted(keys, got)
        scatter_later(l, "mix_w_in", gw_in)
        keys, comm = host()
        pending.append((("mix", l), _GatherRows([_pack_small(mix_parts(l))])))
        if l > 0:
            df1, dg1, dgu1, dx0, dsh1, dsc1, dn1, got = _ffn_backward(
                dx1, sv["gates"][0], sv["f1"], sv["s1"], sv["w1"], full_w[l, "ffn1_w_down"], full_w[l, "ffn1_w_gu"], sv["x0"],
                row_vec(norm_ffn1_g[l]), sv["sc"][0], 0.5, seq, "ffn_bwd", comm)
        else:
            df1, dg1, dgu1, got = _residual_backward(dx1, sv["gates"][0], sv["f1"], full_w[l, "ffn1_w_down"], 0.5, sv["s1"], sv["w1"], seq, "ffn_down_bwd", comm)
        hosted(keys, got)
        gw_down1, _ = _weight_grad(sv["a1"], df1, seq, "grad_w_down")
        scatter_later(l, "ffn1_w_down", gw_down1)
        keys, comm = host()
        gw_gu1, got = _weight_grad(dgu1, sv["h1"], seq, "grad_w_gu", comm)
        hosted(keys, got)
        scatter_later(l, "ffn1_w_gu", gw_gu1)
        if l == 0:
            keys, comm = host()
            dx0, dsh1, dsc1, dn1, got = _matmul_normmod_backward(dgu1, full_w[l, "ffn1_w_gu"], sv["x0"], dx1, row_vec(norm_ffn1_g[l]), sv["sc"][0], seq, "ffn_up_bwd", comm)
            hosted(keys, got)
        dx = dx0
        small_grads[l].update(norm_ffn1_g=dn1, norm_mix_g=dn2, norm_ffn2_g=dn3)
        d_mod[l] = jnp.concatenate([dsh1, dsc1, dg1, dsh2, dsc2, dg2, dsh3, dsc3, dg3], axis=1)
    grad_x = dx.reshape(batch, seq, D_MODEL)

    grad_big, delta, new_m, new_v = {}, {}, {}, {}
    for nm in big:
        results = _adamw_rows([recv[l, nm] for l in range(DEPTH)], as_rows(nm, weights[nm]), as_rows(nm, mom1[nm]),
                              as_rows(nm, mom2[nm]), "adamw_" + nm)
        grad_big[nm], delta[nm], new_m[nm], new_v[nm] = [as_rows(nm, r) for r in results]

    last_parts = late_parts(0) + [d_final_g, loss_cols]
    last_shapes = [p.shape for p in last_parts]
    packed_all, packed_sum = _all_gather_small(_pack_small(last_parts), "reduce_small")
    late_sum = {0: _unpack_small(packed_sum, last_shapes)}
    d_mod_dev = {0: _unpack_small(packed_all, last_shapes, lead=(N_DEV,))[len(late_names)]}
    mix_sum = {}
    for l in range(DEPTH):
        gathered = recv["mix", l].reshape(N_DEV, -1, LANES)
        mix_sum[l] = _unpack_small(_sum_gathered(gathered, "sum_mix"), [p.shape for p in mix_parts(l)])
        if l > 0:
            shapes_l = [p.shape for p in late_parts(l)]
            gathered = recv["late", l].reshape(N_DEV, -1, LANES)
            late_sum[l] = _unpack_small(_sum_gathered(gathered, "sum_late"), shapes_l)
            d_mod_dev[l] = _unpack_small(gathered, shapes_l, lead=(N_DEV,))[len(late_names)]
    grad_small = {}
    for group, names in ((mix_sum, mix_names), (late_sum, late_names)):
        for k, nm in enumerate(names):
            grad_small[nm] = jnp.stack([group[l][k] for l in range(DEPTH)]).reshape(
                (DEPTH, 3, D_A) if nm == "conv_w" else weights[nm].shape)
    grad_small["conv_w"] = lax.dynamic_slice_in_dim(grad_small["conv_w"], me * conv_w.shape[2], conv_w.shape[2], axis=2)
    grad_small["final_norm_g"] = late_sum[0][len(late_names) + 1].reshape(final_norm_g.shape)
    loss = jnp.sum(late_sum[0][len(late_names) + 2])
    d_ada_all = jnp.stack([d_mod_dev[l] for l in range(DEPTH)]).reshape(DEPTH, N_DEV * batch, N_MOD * D_MODEL)
    d_ada_cols = lax.dynamic_slice_in_dim(d_ada_all, me * ada_cols, ada_cols, axis=2)
    g_ada_w, g_ada_b = _ada_backward(c_all, d_ada_cols, d_ada_all)

    grads = dict(grad_big)
    grads.update(grad_small)
    grads["ada_w"] = g_ada_w
    grads["ada_b"] = g_ada_b.reshape(ada_b.shape)

    names = ("ada_w", "ada_b", "norm_ffn1_g", "ffn1_w_gu", "ffn1_w_down", "norm_mix_g", "mix_w_in", "sgu_ln_g",
             "sgu_ln_b", "sgu_w_s", "sgu_b", "conv_w", "out_norm_g", "mix_w_out", "norm_ffn2_g", "ffn2_w_gu",
             "ffn2_w_down", "final_norm_g")
    delta["ada_w"], new_m["ada_w"], new_v["ada_w"] = _adamw_nd(ada_w, grads["ada_w"], m_ada_w, v_ada_w, "adamw_ada_w")
    rest = [nm for nm in names if nm not in big and nm != "ada_w"]
    pick = lambda src: [src[nm] for nm in rest]
    for nm, d_k, m_k, v_k in zip(rest, *_adamw_many(pick(weights), pick(grads), pick(mom1), pick(mom2), "adamw_small")):
        delta[nm], new_m[nm], new_v[nm] = d_k, m_k, v_k

    return (loss, grad_x, *[grads[nm] for nm in names], *[delta[nm] for nm in names],
            *[new_m[nm] for nm in names], *[new_v[nm] for nm in names])
```

```python
import math

import jax
import jax.numpy as jnp
from jax import lax
from jax.experimental import pallas as pl
from jax.experimental.pallas import tpu as pltpu

F32 = jnp.float32
BF16 = jnp.bfloat16

D_MODEL = 1024
D_A = 512
D_PROJ = 2560
N_HEADS = 8
HEAD_DIM = 64
CHUNK = 128
N_MOD = 9
DEPTH = 2
EPS = 1e-6
N_DEV = 8
LANES = 128
MXU_N = 256
HALO = 16
VMEM_LIMIT = 62 * 1024 * 1024
FORWARD_STEPS = 3

ADAM_LR = 0.001
ADAM_B1 = 0.9
ADAM_B2 = 0.999
ADAM_EPS = 1e-08
ADAM_WD = 0.01
ADAM_STEP = 10

MESH = pl.DeviceIdType.MESH


def _dot(a, b):
    return jnp.dot(a, b, preferred_element_type=F32)


def _dot_nt(a, b):
    return lax.dot_general(a, b, (((1,), (1,)), ((), ())), preferred_element_type=F32)


def _dot_tn(a, b):
    return lax.dot_general(a, b, (((0,), (0,)), ((), ())), preferred_element_type=F32)


def _sigmoid(x):
    return 0.5 * jnp.tanh(0.5 * x) + 0.5


def _gelu(x):
    return 0.5 * x * (1.0 + lax.erf(x * (1.0 / math.sqrt(2.0))))


def _gelu_grad(x):
    cdf = 0.5 * (1.0 + lax.erf(x * (1.0 / math.sqrt(2.0))))
    return cdf + x * jnp.exp(-0.5 * x * x) * (1.0 / math.sqrt(2.0 * math.pi))


def _params(n_axes=1, parallel=False):
    sem = ("parallel" if parallel else "arbitrary",) * n_axes
    return pltpu.CompilerParams(dimension_semantics=sem, vmem_limit_bytes=VMEM_LIMIT)


def _resident(shape):
    nd = len(shape)
    return pl.BlockSpec(shape, lambda *_: (0,) * nd, pipeline_mode=pl.Buffered(1))


def _tile_rows(seq):
    return min(512, seq)


def _my_position():
    x, y, c = lax.axis_index("x"), lax.axis_index("y"), lax.axis_index("c")
    return x, y, c, 4 * x + 2 * y + c


def _peer(x, y, c, p):
    return (x ^ ((p >> 2) & 1), y ^ ((p >> 1) & 1), c ^ (p & 1))


class _GatherRows:
    def __init__(self, shards):
        self.operands = list(shards)
        n = len(shards)
        self.out_shape = [jax.ShapeDtypeStruct((N_DEV * s.shape[0], s.shape[1]), s.dtype) for s in shards]
        self.scratch = [pltpu.SemaphoreType.DMA((n, N_DEV - 1)), pltpu.SemaphoreType.DMA((n, N_DEV - 1)),
                        pltpu.SemaphoreType.DMA((n,))]

    def _plan(self, src, dst, send, recv, loc):
        x, y, c, _ = _my_position()
        me, sib = (x, y, c), (x, y, 1 - c)
        chips = [(1 - x, y), (x, 1 - y), (1 - x, 1 - y)]
        plans = []
        for k, shard in enumerate(self.operands):
            rows = shard.shape[0]

            def blk(pos, k=k, rows=rows):
                return dst[k].at[pl.ds((4 * pos[0] + 2 * pos[1] + pos[2]) * rows, rows), :]

            def rc(s, block, to, source=None, k=k, blk=blk):
                return pltpu.make_async_remote_copy(
                    src_ref=blk(block) if source is None else source, dst_ref=blk(block),
                    send_sem=send.at[k, s], recv_sem=recv.at[k, s], device_id=to, device_id_type=MESH)

            plans.append(dict(
                local=pltpu.make_async_copy(src[k], blk(me), loc.at[k]),
                first=[rc(0, me, sib, src[k])] + [rc(1 + j, me, (*chip, c), src[k]) for j, chip in enumerate(chips)],
                landed=[rc(1 + j, (*chip, c), me) for j, chip in enumerate(chips)],
                passed=[rc(4 + j, (*chip, c), sib) for j, chip in enumerate(chips)],
                from_sib=[rc(0, sib, me)] + [rc(4 + j, (*chip, 1 - c), me) for j, chip in enumerate(chips)]))
        return plans

    def start(self, src, dst, send, recv, loc):
        for plan in self._plan(src, dst, send, recv, loc):
            plan["local"].start()
            for cp in plan["first"]:
                cp.start()

    def forward(self, src, dst, send, recv, loc):
        for plan in self._plan(src, dst, send, recv, loc):
            for landed, passed in zip(plan["landed"], plan["passed"]):
                landed.wait_recv()
                passed.start()

    def finish(self, src, dst, send, recv, loc):
        for plan in self._plan(src, dst, send, recv, loc):
            for cp in plan["from_sib"]:
                cp.wait_recv()
            for cp in plan["first"] + plan["passed"]:
                cp.wait_send()
            plan["local"].wait()


class _ScatterRows:
    def __init__(self, grads):
        self.operands = list(grads)
        n = len(grads)
        self.out_shape = [jax.ShapeDtypeStruct((N_DEV, g.shape[0] // N_DEV, g.shape[1]), g.dtype) for g in grads]
        self.scratch = [pltpu.SemaphoreType.DMA((n, N_DEV - 1)), pltpu.SemaphoreType.DMA((n, N_DEV - 1)),
                        pltpu.SemaphoreType.DMA((n,))]

    def _plan(self, src, dst, send, recv, loc):
        x, y, c, me = _my_position()
        copies = []
        for k, grad in enumerate(self.operands):
            rows = grad.shape[0] // N_DEV
            copies.append(pltpu.make_async_copy(src[k].at[pl.ds(me * rows, rows), :], dst[k].at[me], loc.at[k]))
            for p in range(1, N_DEV):
                px, py, pc = _peer(x, y, c, p)
                copies.append(pltpu.make_async_remote_copy(
                    src_ref=src[k].at[pl.ds((4 * px + 2 * py + pc) * rows, rows), :], dst_ref=dst[k].at[me],
                    send_sem=send.at[k, p - 1], recv_sem=recv.at[k, p - 1], device_id=(px, py, pc), device_id_type=MESH))
        return copies

    def start(self, src, dst, send, recv, loc):
        for cp in self._plan(src, dst, send, recv, loc):
            cp.start()

    def forward(self, src, dst, send, recv, loc):
        pass

    def finish(self, src, dst, send, recv, loc):
        for cp in self._plan(src, dst, send, recv, loc):
            cp.wait()


class _Exchanges:
    def __init__(self, parts):
        self.parts = list(parts)
        self.operands = [op for part in self.parts for op in part.operands]
        self.out_shape = [shp for part in self.parts for shp in part.out_shape]
        self.scratch = [scr for part in self.parts for scr in part.scratch]

    def _each(self, src, dst, sems):
        at, sem_at = 0, 0
        for part in self.parts:
            n, n_sem = len(part.operands), len(part.scratch)
            yield part, src[at:at + n], dst[at:at + n], sems[sem_at:sem_at + n_sem]
            at, sem_at = at + n, sem_at + n_sem

    def start(self, src, dst, *sems):
        for part, part_src, part_dst, part_sems in self._each(src, dst, sems):
            part.start(part_src, part_dst, *part_sems)

    def forward(self, src, dst, *sems):
        for part, part_src, part_dst, part_sems in self._each(src, dst, sems):
            part.forward(part_src, part_dst, *part_sems)

    def finish(self, src, dst, *sems):
        for part, part_src, part_dst, part_sems in self._each(src, dst, sems):
            part.finish(part_src, part_dst, *part_sems)


_ANY = pl.BlockSpec(memory_space=pl.ANY)


def _call(body, *, name, grid, in_specs, out_specs, out_shape, operands, scratch_shapes=(), parallel=False, comm=None):
    n_axes = len(grid)
    if comm is None:
        outs = pl.pallas_call(body, name=name, grid=grid, out_shape=list(out_shape), in_specs=list(in_specs),
                              out_specs=list(out_specs), scratch_shapes=list(scratch_shapes),
                              compiler_params=_params(n_axes, parallel))(*operands)
        return list(outs), None
    n_in, n_out, n_scr, n_c = len(in_specs), len(out_specs), len(scratch_shapes), len(comm.operands)
    total = math.prod(grid)

    def hosted(*refs):
        ins, c_src = refs[:n_in], refs[n_in:n_in + n_c]
        outs, c_dst = refs[n_in + n_c:n_in + n_c + n_out], refs[n_in + n_c + n_out:n_in + 2 * n_c + n_out]
        scr, sems = refs[n_in + 2 * n_c + n_out:n_in + 2 * n_c + n_out + n_scr], refs[n_in + 2 * n_c + n_out + n_scr:]
        step = pl.program_id(0)
        for axis in range(1, n_axes):
            step = step * grid[axis] + pl.program_id(axis)

        @pl.when(step == 0)
        def _():
            comm.start(c_src, c_dst, *sems)

        @pl.when(step == max(total - FORWARD_STEPS, 0))
        def _():
            comm.forward(c_src, c_dst, *sems)

        body(*ins, *outs, *scr)

        @pl.when(step == total - 1)
        def _():
            comm.finish(c_src, c_dst, *sems)

    res = pl.pallas_call(hosted, name=name, grid=grid, out_shape=list(out_shape) + comm.out_shape,
                         in_specs=list(in_specs) + [_ANY] * n_c, out_specs=list(out_specs) + [_ANY] * n_c,
                         scratch_shapes=list(scratch_shapes) + comm.scratch,
                         compiler_params=_params(n_axes, False))(*operands, *comm.operands)
    return list(res[:n_out]), list(res[n_out:])


def _all_gather_small(v, name):
    rows = v.shape[0]

    def body(v_ref, all_ref, sum_ref, send_sems, recv_sems):
        x, y, c, me = _my_position()
        all_ref[me] = v_ref[...]
        copies = []
        for p in range(1, N_DEV):
            cp = pltpu.make_async_remote_copy(
                src_ref=v_ref, dst_ref=all_ref.at[me], send_sem=send_sems.at[p - 1], recv_sem=recv_sems.at[p - 1],
                device_id=_peer(x, y, c, p), device_id_type=MESH)
            cp.start()
            copies.append(cp)
        for cp in copies:
            cp.wait()
        acc = all_ref[0]
        for d in range(1, N_DEV):
            acc = acc + all_ref[d]
        sum_ref[...] = acc

    return pl.pallas_call(
        body, name=name,
        out_shape=[jax.ShapeDtypeStruct((N_DEV, rows, LANES), F32), jax.ShapeDtypeStruct((rows, LANES), F32)],
        in_specs=[pl.BlockSpec(memory_space=pltpu.VMEM)],
        out_specs=[pl.BlockSpec(memory_space=pltpu.VMEM)] * 2,
        scratch_shapes=[pltpu.SemaphoreType.DMA((N_DEV - 1,)), pltpu.SemaphoreType.DMA((N_DEV - 1,))],
        compiler_params=pltpu.CompilerParams(vmem_limit_bytes=VMEM_LIMIT),
    )(v)


def _sum_gathered(gathered, name):
    rows = gathered.shape[1]

    def body(g_ref, o_ref):
        acc = g_ref[0]
        for d in range(1, N_DEV):
            acc = acc + g_ref[d]
        o_ref[...] = acc

    return pl.pallas_call(
        body, name=name, out_shape=jax.ShapeDtypeStruct((rows, LANES), F32),
        in_specs=[pl.BlockSpec(memory_space=pltpu.VMEM)], out_specs=pl.BlockSpec(memory_space=pltpu.VMEM),
        compiler_params=pltpu.CompilerParams(vmem_limit_bytes=VMEM_LIMIT),
    )(gathered)


def _pack_small(parts):
    flat = jnp.concatenate([p.reshape(-1).astype(F32) for p in parts])
    total = flat.shape[0]
    padded = -(-total // (8 * LANES)) * (8 * LANES)
    flat = jnp.pad(flat, (0, padded - total))
    return flat.reshape(padded // LANES, LANES)


def _unpack_small(packed, shapes, lead=()):
    flat = packed.reshape(lead + (-1,))
    out, off = [], 0
    for shp in shapes:
        size = math.prod(shp)
        out.append(flat[..., off:off + size].reshape(lead + tuple(shp)))
        off += size
    return out


def _prologue(c_rows, convw_rows, ada_w, ada_b_cols, gather):
    depth, _, cols = ada_w.shape
    n_c = len(gather.operands)
    sub = 8

    def body(c_ref, cw_ref, b_ref, w_hbm, *rest):
        g_src, (c_all_ref, cw_all_ref, ada_ref), g_dst = rest[:n_c], rest[n_c:n_c + 3], rest[n_c + 3:2 * n_c + 3]
        ada_local, w_ref, w_sem, send_sems, recv_sems = rest[2 * n_c + 3:2 * n_c + 8]
        g_sems = rest[2 * n_c + 8:]
        x, y, c, me = _my_position()
        gather.start(g_src, g_dst, *g_sems)
        load_w = pltpu.make_async_copy(w_hbm, w_ref, w_sem)
        load_w.start()

        def to_all(k, src_ref, dst_ref):
            copies = []
            for p in range(1, N_DEV):
                copies.append(pltpu.make_async_remote_copy(
                    src_ref=src_ref, dst_ref=dst_ref.at[me], send_sem=send_sems.at[k, p - 1], recv_sem=recv_sems.at[k, p - 1],
                    device_id=_peer(x, y, c, p), device_id_type=MESH))
            return copies

        first = to_all(0, c_ref, c_all_ref) + to_all(1, cw_ref, cw_all_ref)
        c_all_ref[me] = c_ref[...]
        cw_all_ref[me] = cw_ref[...]
        for cp in first:
            cp.start()
        for cp in first:
            cp.wait()
        cv = c_all_ref[...].reshape(N_DEV * sub, D_MODEL)
        act = (cv * _sigmoid(cv)).astype(BF16)
        load_w.wait()
        for l in range(depth):
            ada_local[l] = _dot(act, w_ref[l].astype(BF16)) + b_ref[l]
        ada_ref[me] = ada_local[:, pl.ds(pl.multiple_of(me * sub, sub), sub), :]
        rows_out = []
        for p in range(1, N_DEV):
            px, py, pc = _peer(x, y, c, p)
            rows = pl.ds(pl.multiple_of((4 * px + 2 * py + pc) * sub, sub), sub)
            rows_out.append(pltpu.make_async_remote_copy(
                src_ref=ada_local.at[:, rows, :], dst_ref=ada_ref.at[me], send_sem=send_sems.at[2, p - 1],
                recv_sem=recv_sems.at[2, p - 1], device_id=(px, py, pc), device_id_type=MESH))
        for cp in rows_out:
            cp.start()
        for cp in rows_out:
            cp.wait()
        gather.forward(g_src, g_dst, *g_sems)
        gather.finish(g_src, g_dst, *g_sems)

    vmem = pl.BlockSpec(memory_space=pltpu.VMEM)
    outs = pl.pallas_call(
        body, name="prologue",
        out_shape=[jax.ShapeDtypeStruct((N_DEV, sub, D_MODEL), F32), jax.ShapeDtypeStruct((N_DEV, sub, LANES), F32),
                   jax.ShapeDtypeStruct((N_DEV, depth, sub, cols), F32)] + gather.out_shape,
        in_specs=[vmem] * 3 + [_ANY] * (1 + n_c), out_specs=[vmem] * 3 + [_ANY] * n_c,
        scratch_shapes=[pltpu.VMEM((depth, N_DEV * sub, cols), F32), pltpu.VMEM(ada_w.shape, F32), pltpu.SemaphoreType.DMA,
                        pltpu.SemaphoreType.DMA((3, N_DEV - 1)), pltpu.SemaphoreType.DMA((3, N_DEV - 1))] + gather.scratch,
        compiler_params=pltpu.CompilerParams(vmem_limit_bytes=VMEM_LIMIT),
    )(c_rows, convw_rows, ada_b_cols, ada_w, *gather.operands)
    return outs[0], outs[1], outs[2], list(outs[3:])


def _ada_backward(c_all, d_ada_cols, d_ada_all):
    nb = c_all.shape[0]
    cols = d_ada_cols.shape[2]
    full = d_ada_all.shape[2]

    def body(c_ref, dc_ref, da_ref, gw_ref, gb_ref):
        cv = c_ref[...]
        act = (cv * _sigmoid(cv)).astype(BF16)
        gw_ref[0] = _dot_tn(act, dc_ref[0].astype(BF16))
        gb_ref[0] = jnp.sum(da_ref[0], axis=0, keepdims=True)

    return pl.pallas_call(
        body, name="ada_backward", grid=(DEPTH,),
        out_shape=[jax.ShapeDtypeStruct((DEPTH, D_MODEL, cols), F32), jax.ShapeDtypeStruct((DEPTH, 1, full), F32)],
        in_specs=[pl.BlockSpec((nb, D_MODEL), lambda l: (0, 0)),
                  pl.BlockSpec((1, nb, cols), lambda l: (l, 0, 0)),
                  pl.BlockSpec((1, nb, full), lambda l: (l, 0, 0))],
        out_specs=[pl.BlockSpec((1, D_MODEL, cols), lambda l: (l, 0, 0)),
                   pl.BlockSpec((1, 1, full), lambda l: (l, 0, 0))],
        compiler_params=_params(),
    )(c_all, d_ada_cols, d_ada_all)


def _rms(xv):
    return lax.rsqrt(jnp.mean(xv * xv, axis=-1, keepdims=True) + EPS)


def _normmod_matmul(x, gnorm, scale1p, shift, w_t, seq, name, comm=None):
    tokens, width = x.shape[0], w_t.shape[0] // 2
    tm = _tile_rows(seq)
    per_seq = seq // tm
    n_chunks = width // MXU_N

    def body(x_ref, g_ref, sc_ref, sh_ref, w_ref, h_ref, act_ref, silu_ref, dact_ref):
        xv = x_ref[...]
        h = (xv * _rms(xv) * g_ref[...]) * sc_ref[0] + sh_ref[0]
        h_ref[...] = h.astype(BF16)
        for ck in range(n_chunks):
            cs = slice(ck * MXU_N, (ck + 1) * MXU_N)
            g = _dot_nt(h_ref[...], w_ref[cs, :])
            u = _dot_nt(h_ref[...], w_ref[width + ck * MXU_N:width + (ck + 1) * MXU_N, :])
            sig = _sigmoid(g)
            silu = g * sig
            act_ref[:, cs] = (silu * u).astype(BF16)
            silu_ref[:, cs] = silu.astype(BF16)
            dact_ref[:, cs] = (u * (sig + silu * (1.0 - sig))).astype(BF16)

    per_batch = pl.BlockSpec((1, 1, D_MODEL), lambda i: (i // per_seq, 0, 0))
    outs, got = _call(
        body, name=name, grid=(tokens // tm,),
        out_shape=[jax.ShapeDtypeStruct((tokens, D_MODEL), BF16)] + [jax.ShapeDtypeStruct((tokens, width), BF16)] * 3,
        in_specs=[pl.BlockSpec((tm, D_MODEL), lambda i: (i, 0)), _resident((1, D_MODEL)), per_batch, per_batch,
                  _resident(w_t.shape)],
        out_specs=[pl.BlockSpec((tm, D_MODEL), lambda i: (i, 0))] + [pl.BlockSpec((tm, width), lambda i: (i, 0))] * 3,
        operands=(x, gnorm, scale1p, shift, w_t), parallel=True, comm=comm)
    return (*outs, got)


def _matmul_residual(src, w, x, gate, scale, seq, name, comm=None):
    tokens, k_dim = x.shape[0], w.shape[0]
    tm = _tile_rows(seq)
    per_seq = seq // tm

    def body(s_ref, w_ref, x_ref, gate_ref, xo_ref, f_ref):
        f = _dot(s_ref[...], w_ref[...])
        f_ref[...] = f.astype(BF16)
        xo_ref[...] = x_ref[...] + (scale * gate_ref[0]) * f

    (x_out, f), got = _call(
        body, name=name, grid=(tokens // tm,),
        out_shape=[jax.ShapeDtypeStruct((tokens, D_MODEL), F32), jax.ShapeDtypeStruct((tokens, D_MODEL), BF16)],
        in_specs=[pl.BlockSpec((tm, k_dim), lambda i: (i, 0)), _resident(w.shape),
                  pl.BlockSpec((tm, D_MODEL), lambda i: (i, 0)),
                  pl.BlockSpec((1, 1, D_MODEL), lambda i: (i // per_seq, 0, 0))],
        out_specs=[pl.BlockSpec((tm, D_MODEL), lambda i: (i, 0))] * 2,
        operands=(src, w, x, gate), parallel=True, comm=comm)
    return x_out, f, got


def _loss_tile(xv, target, gn):
    r = _rms(xv)
    xn = xv * r
    err = xn * gn - target
    loss = (0.5 / D_MODEL) * jnp.sum(err * err, axis=0, keepdims=True)
    dyv = err * (1.0 / D_MODEL)
    dg = jnp.sum(dyv * xn, axis=0, keepdims=True)
    dxn = dyv * gn
    dx = r * (dxn - xn * jnp.mean(dxn * xn, axis=-1, keepdims=True))
    return loss, dx, dg


def _ffn_forward(x, gnorm, scale1p, shift, w_gu_t, w_down, gate, scale, seq, name, loss_head=None, comm=None):
    tokens, width = x.shape[0], w_down.shape[0]
    tm = _tile_rows(seq)
    per_seq = seq // tm
    n_chunks = width // MXU_N

    def body(x_ref, g_ref, sc_ref, sh_ref, wgu_ref, wd_ref, gate_ref, *rest):
        if loss_head:
            t_ref, gf_ref, h_ref, act_ref, silu_ref, dact_ref, xo_ref, f_ref, dgf_ref, loss_ref = rest
        else:
            h_ref, act_ref, silu_ref, dact_ref, xo_ref, f_ref = rest
        xv = x_ref[...]
        h = (xv * _rms(xv) * g_ref[...]) * sc_ref[0] + sh_ref[0]
        h_ref[...] = h.astype(BF16)
        for ck in range(n_chunks):
            cs = slice(ck * MXU_N, (ck + 1) * MXU_N)
            g = _dot_nt(h_ref[...], wgu_ref[cs, :])
            u = _dot_nt(h_ref[...], wgu_ref[width + ck * MXU_N:width + (ck + 1) * MXU_N, :])
            sig = _sigmoid(g)
            silu = g * sig
            act_ref[:, cs] = (silu * u).astype(BF16)
            silu_ref[:, cs] = silu.astype(BF16)
            dact_ref[:, cs] = (u * (sig + silu * (1.0 - sig))).astype(BF16)
        f = _dot(act_ref[...], wd_ref[...])
        f_ref[...] = f.astype(BF16)
        x_out = xv + (scale * gate_ref[0]) * f
        if loss_head:
            i = pl.program_id(0)
            loss, dx, dg = _loss_tile(x_out, t_ref[...], gf_ref[...])
            xo_ref[...] = dx

            @pl.when(i == 0)
            def _():
                dgf_ref[...] = dg
                loss_ref[...] = loss

            @pl.when(i != 0)
            def _():
                dgf_ref[...] = dgf_ref[...] + dg
                loss_ref[...] = loss_ref[...] + loss
        else:
            xo_ref[...] = x_out

    row = lambda i: (i, 0)
    per_batch = pl.BlockSpec((1, 1, D_MODEL), lambda i: (i // per_seq, 0, 0))
    tile = lambda cols: pl.BlockSpec((tm, cols), row)
    wide = jax.ShapeDtypeStruct((tokens, width), BF16)
    fixed = pl.BlockSpec((1, D_MODEL), lambda i: (0, 0))
    vec = jax.ShapeDtypeStruct((1, D_MODEL), F32)
    outs, got = _call(
        body, name=name, grid=(tokens // tm,),
        out_shape=[jax.ShapeDtypeStruct((tokens, D_MODEL), BF16), wide, wide, wide,
                   jax.ShapeDtypeStruct((tokens, D_MODEL), F32), jax.ShapeDtypeStruct((tokens, D_MODEL), BF16)]
        + ([vec, vec] if loss_head else []),
        in_specs=[tile(D_MODEL), _resident((1, D_MODEL)), per_batch, per_batch, _resident(w_gu_t.shape),
                  _resident(w_down.shape), per_batch] + ([tile(D_MODEL), _resident((1, D_MODEL))] if loss_head else []),
        out_specs=[tile(D_MODEL), tile(width), tile(width), tile(width), tile(D_MODEL), tile(D_MODEL)]
        + ([fixed, fixed] if loss_head else []),
        operands=(x, gnorm, scale1p, shift, w_gu_t, w_down, gate) + (tuple(loss_head) if loss_head else ()),
        parallel=not loss_head, comm=comm)
    return (*outs, got)


def _residual_backward(dy, gate, f, w, scale, silu, dact, seq, name, comm=None):
    tokens, k_dim = dy.shape[0], w.shape[0]
    batch = tokens // seq
    tm = _tile_rows(seq)
    per_seq = seq // tm
    n_chunks = k_dim // MXU_N

    def body(dy_ref, gate_ref, f_ref, silu_ref, dact_ref, w_ref, df_ref, dgate_ref, dgu_ref):
        i = pl.program_id(0)
        dy_v = dy_ref[...]
        df_ref[...] = ((scale * gate_ref[0]) * dy_v).astype(BF16)
        part = scale * jnp.sum(dy_v * f_ref[...].astype(F32), axis=0, keepdims=True)
        for ck in range(n_chunks):
            cs = slice(ck * MXU_N, (ck + 1) * MXU_N)
            cu = slice(k_dim + ck * MXU_N, k_dim + (ck + 1) * MXU_N)
            da = _dot_nt(df_ref[...], w_ref[cs, :])
            dgu_ref[:, cs] = (da * dact_ref[:, cs].astype(F32)).astype(BF16)
            dgu_ref[:, cu] = (da * silu_ref[:, cs].astype(F32)).astype(BF16)

        @pl.when(i % per_seq == 0)
        def _():
            dgate_ref[0] = part

        @pl.when(i % per_seq != 0)
        def _():
            dgate_ref[0] = dgate_ref[0] + part

    row = lambda i: (i, 0)
    per_batch = pl.BlockSpec((1, 1, D_MODEL), lambda i: (i // per_seq, 0, 0))
    tile = lambda cols: pl.BlockSpec((tm, cols), row)
    outs, got = _call(
        body, name=name, grid=(tokens // tm,),
        out_shape=[jax.ShapeDtypeStruct((tokens, D_MODEL), BF16), jax.ShapeDtypeStruct((batch, 1, D_MODEL), F32),
                   jax.ShapeDtypeStruct((tokens, 2 * k_dim), BF16)],
        in_specs=[tile(D_MODEL), per_batch, tile(D_MODEL), tile(k_dim), tile(k_dim), _resident(w.shape)],
        out_specs=[tile(D_MODEL), per_batch, tile(2 * k_dim)],
        operands=(dy, gate, f, silu, dact, w), comm=comm)
    return (*outs, got)


def _matmul_normmod_backward(dsrc, w_t, x, dy, gnorm, scale1p, seq, name, comm=None):
    tokens, k_dim = dsrc.shape
    batch = tokens // seq
    tm = _tile_rows(seq)
    per_seq = seq // tm

    def body(ds_ref, w_ref, x_ref, dy_ref, g_ref, sc_ref, dx_ref, dsh_ref, dsc_ref, dg_ref):
        i = pl.program_id(0)
        dh = _dot(ds_ref[...], w_ref[...])
        xv = x_ref[...]
        r = _rms(xv)
        xn = xv * r
        gn = g_ref[...]
        dsh = jnp.sum(dh, axis=0, keepdims=True)
        dsc = jnp.sum(dh * (xn * gn), axis=0, keepdims=True)
        dhn = dh * sc_ref[0]
        dg = jnp.sum(dhn * xn, axis=0, keepdims=True)
        dxn = dhn * gn
        dx_ref[...] = dy_ref[...] + r * (dxn - xn * jnp.mean(dxn * xn, axis=-1, keepdims=True))

        @pl.when(i % per_seq == 0)
        def _():
            dsh_ref[0] = dsh
            dsc_ref[0] = dsc

        @pl.when(i % per_seq != 0)
        def _():
            dsh_ref[0] = dsh_ref[0] + dsh
            dsc_ref[0] = dsc_ref[0] + dsc

        @pl.when(i == 0)
        def _():
            dg_ref[...] = dg

        @pl.when(i != 0)
        def _():
            dg_ref[...] = dg_ref[...] + dg

    row = lambda i: (i, 0)
    per_batch = pl.BlockSpec((1, 1, D_MODEL), lambda i: (i // per_seq, 0, 0))
    outs, got = _call(
        body, name=name, grid=(tokens // tm,),
        out_shape=[jax.ShapeDtypeStruct((tokens, D_MODEL), F32), jax.ShapeDtypeStruct((batch, 1, D_MODEL), F32),
                   jax.ShapeDtypeStruct((batch, 1, D_MODEL), F32), jax.ShapeDtypeStruct((1, D_MODEL), F32)],
        in_specs=[pl.BlockSpec((tm, k_dim), row), _resident(w_t.shape), pl.BlockSpec((tm, D_MODEL), row),
                  pl.BlockSpec((tm, D_MODEL), row), _resident((1, D_MODEL)), per_batch],
        out_specs=[pl.BlockSpec((tm, D_MODEL), row), per_batch, per_batch, pl.BlockSpec((1, D_MODEL), lambda i: (0, 0))],
        operands=(dsrc, w_t, x, dy, gnorm, scale1p), comm=comm)
    return (*outs, got)


def _ffn_backward(dy, gate, f, silu, dact, w_down, w_gu_t, x, gnorm, scale1p, scale, seq, name, comm=None):
    tokens, k_dim = dy.shape[0], w_down.shape[0]
    batch = tokens // seq
    tm = _tile_rows(seq)
    per_seq = seq // tm
    n_chunks = k_dim // MXU_N

    def body(dy_ref, gate_ref, f_ref, silu_ref, dact_ref, wd_ref, wgu_ref, x_ref, g_ref, sc_ref,
             df_ref, dgate_ref, dgu_ref, dx_ref, dsh_ref, dsc_ref, dg_ref):
        i = pl.program_id(0)
        dy_v = dy_ref[...]
        df_ref[...] = ((scale * gate_ref[0]) * dy_v).astype(BF16)
        dgate = scale * jnp.sum(dy_v * f_ref[...].astype(F32), axis=0, keepdims=True)
        for ck in range(n_chunks):
            cs = slice(ck * MXU_N, (ck + 1) * MXU_N)
            cu = slice(k_dim + ck * MXU_N, k_dim + (ck + 1) * MXU_N)
            da = _dot_nt(df_ref[...], wd_ref[cs, :])
            dgu_ref[:, cs] = (da * dact_ref[:, cs].astype(F32)).astype(BF16)
            dgu_ref[:, cu] = (da * silu_ref[:, cs].astype(F32)).astype(BF16)
        dh = _dot(dgu_ref[...], wgu_ref[...])
        xv = x_ref[...]
        r = _rms(xv)
        xn = xv * r
        gn = g_ref[...]
        dsh = jnp.sum(dh, axis=0, keepdims=True)
        dsc = jnp.sum(dh * (xn * gn), axis=0, keepdims=True)
        dhn = dh * sc_ref[0]
        dg = jnp.sum(dhn * xn, axis=0, keepdims=True)
        dxn = dhn * gn
        dx_ref[...] = dy_v + r * (dxn - xn * jnp.mean(dxn * xn, axis=-1, keepdims=True))

        @pl.when(i % per_seq == 0)
        def _():
            dgate_ref[0] = dgate
            dsh_ref[0] = dsh
            dsc_ref[0] = dsc

        @pl.when(i % per_seq != 0)
        def _():
            dgate_ref[0] = dgate_ref[0] + dgate
            dsh_ref[0] = dsh_ref[0] + dsh
            dsc_ref[0] = dsc_ref[0] + dsc

        @pl.when(i == 0)
        def _():
            dg_ref[...] = dg

        @pl.when(i != 0)
        def _():
            dg_ref[...] = dg_ref[...] + dg

    row = lambda i: (i, 0)
    per_batch = pl.BlockSpec((1, 1, D_MODEL), lambda i: (i // per_seq, 0, 0))
    tile = lambda width: pl.BlockSpec((tm, width), row)
    vec = jax.ShapeDtypeStruct((batch, 1, D_MODEL), F32)
    outs, got = _call(
        body, name=name, grid=(tokens // tm,),
        out_shape=[jax.ShapeDtypeStruct((tokens, D_MODEL), BF16), vec, jax.ShapeDtypeStruct((tokens, 2 * k_dim), BF16),
                   jax.ShapeDtypeStruct((tokens, D_MODEL), F32), vec, vec, jax.ShapeDtypeStruct((1, D_MODEL), F32)],
        in_specs=[tile(D_MODEL), per_batch, tile(D_MODEL), tile(k_dim), tile(k_dim), _resident(w_down.shape),
                  _resident(w_gu_t.shape), tile(D_MODEL), _resident((1, D_MODEL)), per_batch],
        out_specs=[tile(D_MODEL), per_batch, tile(2 * k_dim), tile(D_MODEL), per_batch, per_batch,
                   pl.BlockSpec((1, D_MODEL), lambda i: (0, 0))],
        operands=(dy, gate, f, silu, dact, w_down, w_gu_t, x, gnorm, scale1p), comm=comm)
    return (*outs, got)


def _weight_grad(a, b, seq, name, comm=None):
    tokens, n_out = a.shape
    tn = MXU_N
    parts = 4
    rows = tokens // parts

    def body(a_ref, b_hbm, o_ref, b_ref, sems):
        first = pl.program_id(0) == 0
        copies = [pltpu.make_async_copy(b_hbm.at[pl.ds(k * rows, rows), :], b_ref.at[pl.ds(k * rows, rows), :], sems.at[k])
                  for k in range(parts)]

        @pl.when(first)
        def _():
            for cp in copies:
                cp.start()

        acc = None
        for k in range(parts):
            @pl.when(first)
            def _(cp=copies[k]):
                cp.wait()

            part = _dot_tn(a_ref[k * rows:(k + 1) * rows, :], b_ref[k * rows:(k + 1) * rows, :])
            acc = part if acc is None else acc + part
        o_ref[...] = acc.astype(BF16)

    (out,), got = _call(
        body, name=name, grid=(n_out // tn,),
        out_shape=[jax.ShapeDtypeStruct((n_out, D_MODEL), BF16)],
        in_specs=[pl.BlockSpec((tokens, tn), lambda j: (0, j)), _ANY],
        out_specs=[pl.BlockSpec((tn, D_MODEL), lambda j: (j, 0))],
        scratch_shapes=[pltpu.VMEM((tokens, D_MODEL), BF16), pltpu.SemaphoreType.DMA((parts,))],
        operands=(a, b), comm=comm)
    return out, got


def _group_mean(v, bd):
    hi = v.astype(BF16)
    lo = (v - hi.astype(F32)).astype(BF16)
    blocks = [slice(k * MXU_N, (k + 1) * MXU_N) for k in range(v.shape[1] // MXU_N)]
    return jnp.concatenate([_dot(hi[:, b], bd) + _dot(lo[:, b], bd) for b in blocks], axis=1)


def _sgu_forward(pm_ref, wm_ref, bias_ref, lng_ref, lnb_ref, bd_ref, mixed_scr, n_sub):
    ua = pm_ref[:, 0:D_A].astype(F32)
    va = pm_ref[:, D_A:2 * D_A].astype(F32)
    u_act = _gelu(ua)
    v_act = _gelu(va)
    bd = bd_ref[...]
    vc = v_act - _group_mean(v_act, bd)
    rstd = lax.rsqrt(_group_mean(vc * vc, bd) + EPS)
    vhat = vc * rstd
    vln = vhat * lng_ref[...] + lnb_ref[...]
    left = lax.broadcasted_iota(jnp.int32, (CHUNK, LANES), 1) < HEAD_DIM
    for q in range(n_sub):
        rows = slice(q * CHUNK, (q + 1) * CHUNK)
        for p in range(N_HEADS // 2):
            cols = slice(p * LANES, (p + 1) * LANES)
            vp = vln[rows, cols]
            stacked = jnp.concatenate([jnp.where(left, vp, 0.0), jnp.where(left, 0.0, vp)], axis=0).astype(BF16)
            mixed_scr[rows, cols] = _dot(wm_ref[p], stacked) + bias_ref[:, cols]
    return ua, va, u_act, vhat, rstd, vln


def _halo_specs(tm, tokens, width):
    prev = pl.BlockSpec((HALO, width), lambda i: (jnp.maximum(i * (tm // HALO) - 1, 0), 0))
    nxt = pl.BlockSpec((HALO, width), lambda i: (jnp.minimum((i + 1) * (tm // HALO), tokens // HALO - 1), 0))
    return prev, nxt


def _mixer_forward(x, gnorm, scale1p, shift, w_in_t, gate, w_out, wm, bias_full, lng, lnb, convw, og, bd, seq, name, comm=None):
    tokens = x.shape[0]
    tm = _tile_rows(seq)
    per_seq = seq // tm
    n_sub = tm // CHUNK

    def body(x_ref, xp_ref, g_ref, sc_ref, sh_ref, win_ref, gate_ref, wo_ref, wm_ref, bias_ref, lng_ref, lnb_ref, cw_ref,
             og_ref, bd_ref, h_ref, pm_ref, y_ref, xo_ref, o_ref, mixed_scr):
        i = pl.program_id(0)
        first = (i % per_seq) == 0
        xv = x_ref[...]
        h_ref[...] = ((xv * _rms(xv) * g_ref[...]) * sc_ref[0] + sh_ref[0]).astype(BF16)
        for ck in range(D_PROJ // MXU_N):
            cs = slice(ck * MXU_N, (ck + 1) * MXU_N)
            pm_ref[:, cs] = _dot_nt(h_ref[...], win_ref[cs, :]).astype(BF16)
        xp = xp_ref[...]
        hp = ((xp * _rms(xp) * g_ref[...]) * sc_ref[0] + sh_ref[0]).astype(BF16)
        gates_prev = _dot_nt(hp, win_ref[3 * D_A:5 * D_A, :]).astype(BF16).astype(F32)

        _, _, u_act, _, _, _ = _sgu_forward(pm_ref, wm_ref, bias_ref, lng_ref, lnb_ref, bd_ref, mixed_scr, n_sub)
        ya = u_act * mixed_scr[...]
        y_ref[:, 0:D_A] = (ya * _rms(ya) * og_ref[:, 0:D_A]).astype(BF16)

        bg = pm_ref[:, 2 * D_A:3 * D_A].astype(F32)
        z = pm_ref[:, 3 * D_A:4 * D_A].astype(F32) * pm_ref[:, 4 * D_A:5 * D_A].astype(F32)
        zp = jnp.where(first, 0.0, gates_prev[:, 0:D_A] * gates_prev[:, D_A:2 * D_A])
        zext = jnp.concatenate([zp, z], axis=0)
        z1 = pltpu.roll(zext, 1, 0)[HALO:]
        z2 = pltpu.roll(zext, 2, 0)[HALO:]
        conv = cw_ref[0:1, :] * z2 + cw_ref[1:2, :] * z1 + cw_ref[2:3, :] * z
        yb = bg * conv
        y_ref[:, D_A:2 * D_A] = (yb * _rms(yb) * og_ref[:, D_A:2 * D_A]).astype(BF16)

        f = _dot(y_ref[...], wo_ref[...])
        o_ref[...] = f.astype(BF16)
        xo_ref[...] = xv + gate_ref[0] * f

    prev, _ = _halo_specs(tm, tokens, D_MODEL)
    tile = pl.BlockSpec((tm, D_MODEL), lambda i: (i, 0))
    per_batch = pl.BlockSpec((1, 1, D_MODEL), lambda i: (i // per_seq, 0, 0))
    bf = lambda cols: jax.ShapeDtypeStruct((tokens, cols), BF16)
    outs, got = _call(
        body, name=name, grid=(tokens // tm,),
        out_shape=[bf(D_MODEL), bf(D_PROJ), bf(D_MODEL), jax.ShapeDtypeStruct((tokens, D_MODEL), F32), bf(D_MODEL)],
        in_specs=[tile, prev, _resident((1, D_MODEL)), per_batch, per_batch, _resident(w_in_t.shape), per_batch,
                  _resident(w_out.shape), _resident(wm.shape), _resident(bias_full.shape), _resident(lng.shape),
                  _resident(lnb.shape), _resident(convw.shape), _resident(og.shape), _resident(bd.shape)],
        out_specs=[tile, pl.BlockSpec((tm, D_PROJ), lambda i: (i, 0)), tile, tile, tile],
        scratch_shapes=[pltpu.VMEM((tm, D_A), F32)],
        operands=(x, x, gnorm, scale1p, shift, w_in_t, gate, w_out, wm, bias_full, lng, lnb, convw, og, bd),
        parallel=True, comm=comm)
    return (*outs, got)


def _mixer_backward(proj, dx, gate, o, w_out, x, gnorm, scale1p, w_in_t, wm, bias_full, lng, lnb, convw, og, bd, wm_rows,
                    causal, seq, name, comm=None):
    tokens = proj.shape[0]
    batch = tokens // seq
    tm = _tile_rows(seq)
    per_seq = seq // tm
    n_sub = tm // CHUNK
    ext = tm + 2 * HALO

    def body(pm_ref, pp_ref, pn_ref, dx_ref, dxn_ref, gate_ref, o_ref, wo_ref, x_ref, g_ref, sc_ref, win_ref, wm_ref, bias_ref,
             lng_ref, lnb_ref, cw_ref, og_ref, bd_ref, wmr_ref, causal_ref, do_ref, dgate_ref, dp_ref, dxo_ref, dsh_ref, dsc_ref,
             dgn_ref,
             dog_ref, dcw_ref, dlng_ref, dlnb_ref, dbias_ref, dwm_ref, mixed_scr, dvln_scr, dy_scr):
        i = pl.program_id(0)
        first = (i % per_seq) == 0
        last = (i % per_seq) == per_seq - 1

        @pl.when(i == 0)
        def _():
            dog_ref[...] = jnp.zeros_like(dog_ref)
            dcw_ref[...] = jnp.zeros_like(dcw_ref)
            dlng_ref[...] = jnp.zeros_like(dlng_ref)
            dlnb_ref[...] = jnp.zeros_like(dlnb_ref)
            dbias_ref[...] = jnp.zeros_like(dbias_ref)
            dwm_ref[...] = jnp.zeros_like(dwm_ref)

        dx_v = dx_ref[...]
        do_ref[...] = (gate_ref[0] * dx_v).astype(BF16)
        dgate = jnp.sum(dx_v * o_ref[...].astype(F32), axis=0, keepdims=True)
        dy_scr[...] = _dot_nt(do_ref[...], wo_ref[...])
        dyn_conv = _dot_nt((gate_ref[0] * dxn_ref[...]).astype(BF16), wo_ref[D_A:2 * D_A, :])

        ua, va, u_act, vhat, rstd, vln = _sgu_forward(pm_ref, wm_ref, bias_ref, lng_ref, lnb_ref, bd_ref, mixed_scr, n_sub)
        mixed = mixed_scr[...]
        ya = u_act * mixed
        ra = _rms(ya)
        yhat = ya * ra
        dya_in = dy_scr[:, 0:D_A]
        dog_ref[:, 0:D_A] = dog_ref[:, 0:D_A] + jnp.sum(dya_in * yhat, axis=0, keepdims=True)
        dyh = dya_in * og_ref[:, 0:D_A]
        dya = ra * (dyh - yhat * jnp.mean(dyh * yhat, axis=-1, keepdims=True))
        d_u = dya * mixed
        d_mixed = dya * u_act
        left = lax.broadcasted_iota(jnp.int32, (CHUNK, LANES), 1) < HEAD_DIM
        dbias = jnp.zeros((CHUNK, D_A), F32)
        for q in range(n_sub):
            rows = slice(q * CHUNK, (q + 1) * CHUNK)
            dbias = dbias + d_mixed[rows, :]
            for p in range(N_HEADS // 2):
                cols = slice(p * LANES, (p + 1) * LANES)
                dm = d_mixed[rows, cols]
                stacked = jnp.concatenate([jnp.where(left, dm, 0.0), jnp.where(left, 0.0, dm)], axis=0).astype(BF16)
                dw = _dot_nt(stacked, vln[rows, cols].astype(BF16))
                dwm_ref[2 * p] = dwm_ref[2 * p] + causal_ref[...] * dw[0:CHUNK]
                dwm_ref[2 * p + 1] = dwm_ref[2 * p + 1] + causal_ref[...] * dw[CHUNK:2 * CHUNK]
                dvln_scr[rows, cols] = _dot_tn(wmr_ref[p], stacked)
        dbias_ref[...] = dbias_ref[...] + dbias
        dvln = dvln_scr[...]
        dlng_ref[...] = dlng_ref[...] + jnp.sum(dvln * vhat, axis=0, keepdims=True)
        dlnb_ref[...] = dlnb_ref[...] + jnp.sum(dvln, axis=0, keepdims=True)
        dvh = dvln * lng_ref[...]
        bd = bd_ref[...]
        d_v = rstd * (dvh - _group_mean(dvh, bd) - vhat * _group_mean(dvh * vhat, bd))
        dp_ref[:, 0:D_A] = (d_u * _gelu_grad(ua)).astype(BF16)
        dp_ref[:, D_A:2 * D_A] = (d_v * _gelu_grad(va)).astype(BF16)
        dh_a = _dot(dp_ref[:, 0:2 * D_A], win_ref[0:2 * D_A, :])

        def ext_cols(lo):
            cs = slice(lo, lo + D_A)
            return jnp.concatenate([pp_ref[:, cs], pm_ref[:, cs], pn_ref[:, cs]], axis=0).astype(F32)

        bg, cg, xb = ext_cols(2 * D_A), ext_cols(3 * D_A), ext_cols(4 * D_A)
        row = lax.broadcasted_iota(jnp.int32, (ext, D_A), 0)
        z = jnp.where(jnp.logical_and(first, row < HALO), 0.0, cg * xb)
        z1 = pltpu.roll(z, 1, 0)
        z2 = pltpu.roll(z, 2, 0)
        w0, w1, w2 = cw_ref[0:1, :], cw_ref[1:2, :], cw_ref[2:3, :]
        conv = w0 * z2 + w1 * z1 + w2 * z
        yb = bg * conv
        rb = _rms(yb)
        yhb = yb * rb
        dyn = jnp.where(last, 0.0, dyn_conv)
        dyb_in = jnp.concatenate([jnp.zeros((HALO, D_A), F32), dy_scr[:, D_A:2 * D_A], dyn], axis=0)
        dyhb = dyb_in * og_ref[:, D_A:2 * D_A]
        dyb = rb * (dyhb - yhb * jnp.mean(dyhb * yhb, axis=-1, keepdims=True))
        d_conv = dyb * bg
        dz = w2 * d_conv + w1 * pltpu.roll(d_conv, ext - 1, 0) + w0 * pltpu.roll(d_conv, ext - 2, 0)
        main = slice(HALO, HALO + tm)
        dp_ref[:, 2 * D_A:3 * D_A] = (dyb * conv)[main].astype(BF16)
        dp_ref[:, 3 * D_A:4 * D_A] = (dz * xb)[main].astype(BF16)
        dp_ref[:, 4 * D_A:5 * D_A] = (dz * cg)[main].astype(BF16)
        dog_ref[:, D_A:2 * D_A] = dog_ref[:, D_A:2 * D_A] + jnp.sum((dyb_in * yhb)[main], axis=0, keepdims=True)
        dcm = d_conv[main]
        dcw_ref[0:1, :] = dcw_ref[0:1, :] + jnp.sum(dcm * z2[main], axis=0, keepdims=True)
        dcw_ref[1:2, :] = dcw_ref[1:2, :] + jnp.sum(dcm * z1[main], axis=0, keepdims=True)
        dcw_ref[2:3, :] = dcw_ref[2:3, :] + jnp.sum(dcm * z[main], axis=0, keepdims=True)

        dh = dh_a + _dot(dp_ref[:, 2 * D_A:5 * D_A], win_ref[2 * D_A:5 * D_A, :])
        xv = x_ref[...]
        r = _rms(xv)
        xn = xv * r
        gn = g_ref[...]
        dsh = jnp.sum(dh, axis=0, keepdims=True)
        dsc = jnp.sum(dh * (xn * gn), axis=0, keepdims=True)
        dhn = dh * sc_ref[0]
        dgn = jnp.sum(dhn * xn, axis=0, keepdims=True)
        dxn = dhn * gn
        dxo_ref[...] = dx_v + r * (dxn - xn * jnp.mean(dxn * xn, axis=-1, keepdims=True))

        @pl.when(first)
        def _():
            dgate_ref[0] = dgate
            dsh_ref[0] = dsh
            dsc_ref[0] = dsc

        @pl.when(jnp.logical_not(first))
        def _():
            dgate_ref[0] = dgate_ref[0] + dgate
            dsh_ref[0] = dsh_ref[0] + dsh
            dsc_ref[0] = dsc_ref[0] + dsc

        @pl.when(i == 0)
        def _():
            dgn_ref[...] = dgn

        @pl.when(i != 0)
        def _():
            dgn_ref[...] = dgn_ref[...] + dgn

    prev_p, next_p = _halo_specs(tm, tokens, D_PROJ)
    _, next_d = _halo_specs(tm, tokens, D_MODEL)
    fixed2 = lambda shape: pl.BlockSpec(shape, lambda i: (0, 0))
    tile = pl.BlockSpec((tm, D_MODEL), lambda i: (i, 0))
    per_batch = pl.BlockSpec((1, 1, D_MODEL), lambda i: (i // per_seq, 0, 0))
    vec = jax.ShapeDtypeStruct((batch, 1, D_MODEL), F32)
    outs, got = _call(
        body, name=name, grid=(tokens // tm,),
        out_shape=[jax.ShapeDtypeStruct((tokens, D_MODEL), BF16), vec, jax.ShapeDtypeStruct((tokens, D_PROJ), BF16),
                   jax.ShapeDtypeStruct((tokens, D_MODEL), F32), vec, vec, jax.ShapeDtypeStruct((1, D_MODEL), F32),
                   jax.ShapeDtypeStruct((1, D_MODEL), F32), jax.ShapeDtypeStruct((8, D_A), F32),
                   jax.ShapeDtypeStruct((1, D_A), F32), jax.ShapeDtypeStruct((1, D_A), F32),
                   jax.ShapeDtypeStruct((CHUNK, D_A), F32), jax.ShapeDtypeStruct((N_HEADS, CHUNK, CHUNK), F32)],
        in_specs=[pl.BlockSpec((tm, D_PROJ), lambda i: (i, 0)), prev_p, next_p, tile, next_d, per_batch, tile,
                  _resident(w_out.shape), tile, _resident((1, D_MODEL)), per_batch, _resident(w_in_t.shape),
                  _resident(wm.shape), _resident(bias_full.shape), _resident(lng.shape), _resident(lnb.shape),
                  _resident(convw.shape), _resident(og.shape), _resident(bd.shape), _resident(wm_rows.shape),
                  _resident(causal.shape)],
        out_specs=[tile, per_batch, pl.BlockSpec((tm, D_PROJ), lambda i: (i, 0)), tile, per_batch, per_batch,
                   fixed2((1, D_MODEL)), fixed2((1, D_MODEL)), fixed2((8, D_A)), fixed2((1, D_A)), fixed2((1, D_A)),
                   fixed2((CHUNK, D_A)), pl.BlockSpec((N_HEADS, CHUNK, CHUNK), lambda i: (0, 0, 0))],
        scratch_shapes=[pltpu.VMEM((tm, D_A), F32), pltpu.VMEM((tm, D_A), F32), pltpu.VMEM((tm, D_MODEL), F32)],
        operands=(proj, proj, proj, dx, dx, gate, o, w_out, x, gnorm, scale1p, w_in_t, wm, bias_full, lng, lnb, convw, og, bd,
                  wm_rows, causal), comm=comm)
    return (*outs, got)


def _adamw_update(wv, gv, mv, vv):
    nm = ADAM_B1 * mv + (1.0 - ADAM_B1) * gv
    nv = ADAM_B2 * vv + (1.0 - ADAM_B2) * (gv * gv)
    m_hat = nm / (1.0 - ADAM_B1 ** ADAM_STEP)
    v_hat = nv / (1.0 - ADAM_B2 ** ADAM_STEP)
    return -ADAM_LR * (m_hat / (jnp.sqrt(v_hat) + ADAM_EPS) + ADAM_WD * wv), nm, nv


def _adamw_rows(recv, w, m, v, name):
    depth, rows, cols = w.shape
    tr = rows // 2
    last = rows // tr - 1

    def body(*refs):
        r_refs, (w_ref, m_ref, v_ref, g_ref, d_ref, nm_ref, nv_ref) = refs[:depth], refs[depth:]
        for l in range(depth):
            @pl.when(pl.program_id(0) == l)
            def _(r_ref=r_refs[l]):
                acc = r_ref[0].astype(F32)
                for d in range(1, N_DEV):
                    acc = acc + r_ref[d].astype(F32)
                g_ref[0] = acc
                d_ref[0], nm_ref[0], nv_ref[0] = _adamw_update(w_ref[0], acc, m_ref[0], v_ref[0])

    def slots(l):
        return pl.BlockSpec((N_DEV, tr, cols), lambda ll, i: (0, jnp.where(ll == l, i, jnp.where(ll < l, 0, last)), 0))

    spec = pl.BlockSpec((1, tr, cols), lambda ll, i: (ll, i, 0))
    return pl.pallas_call(
        body, name=name, grid=(depth, rows // tr),
        out_shape=[jax.ShapeDtypeStruct((depth, rows, cols), F32)] * 4,
        in_specs=[slots(l) for l in range(depth)] + [spec] * 3, out_specs=[spec] * 4,
        compiler_params=_params(2),
    )(*recv, w, m, v)


def _adamw(w, g, m, v, name):
    rows, cols = w.shape
    tr = max(t for t in range(8, 513, 8) if rows % t == 0)

    def body(w_ref, g_ref, m_ref, v_ref, d_ref, nm_ref, nv_ref):
        d_ref[...], nm_ref[...], nv_ref[...] = _adamw_update(w_ref[...], g_ref[...], m_ref[...], v_ref[...])

    spec = pl.BlockSpec((tr, cols), lambda i: (i, 0))
    return pl.pallas_call(
        body, name=name, grid=(rows // tr,),
        out_shape=[jax.ShapeDtypeStruct((rows, cols), F32)] * 3,
        in_specs=[spec] * 4, out_specs=[spec] * 3,
        compiler_params=_params(parallel=True),
    )(w, g, m, v)


def _adamw_many(ws, gs, ms, vs, name):
    n = len(ws)
    two_d = lambda a: a.reshape(-1, a.shape[-1])

    def body(*refs):
        w_refs, g_refs, m_refs, v_refs = refs[:n], refs[n:2 * n], refs[2 * n:3 * n], refs[3 * n:4 * n]
        d_refs, nm_refs, nv_refs = refs[4 * n:5 * n], refs[5 * n:6 * n], refs[6 * n:]
        for k in range(n):
            d_refs[k][...], nm_refs[k][...], nv_refs[k][...] = _adamw_update(
                w_refs[k][...], g_refs[k][...], m_refs[k][...], v_refs[k][...])

    flat = [two_d(a) for a in ws]
    outs = pl.pallas_call(
        body, name=name, out_shape=[jax.ShapeDtypeStruct(a.shape, F32) for a in flat] * 3,
        in_specs=[pl.BlockSpec(memory_space=pltpu.VMEM)] * (4 * n),
        out_specs=[pl.BlockSpec(memory_space=pltpu.VMEM)] * (3 * n),
        compiler_params=pltpu.CompilerParams(vmem_limit_bytes=VMEM_LIMIT),
    )(*flat, *[two_d(a) for a in gs], *[two_d(a) for a in ms], *[two_d(a) for a in vs])
    shaped = [o.reshape(ws[k % n].shape) for k, o in enumerate(outs)]
    return shaped[:n], shaped[n:2 * n], shaped[2 * n:]


def _adamw_nd(w, g, m, v, name):
    shape = w.shape
    two_d = (-1, shape[-1])
    d, nm, nv = _adamw(w.reshape(two_d), g.reshape(two_d), m.reshape(two_d), v.reshape(two_d), name)
    return d.reshape(shape), nm.reshape(shape), nv.reshape(shape)


def kernel(x, c, ada_w, ada_b, norm_ffn1_g, ffn1_w_gu, ffn1_w_down, norm_mix_g, mix_w_in, sgu_ln_g, sgu_ln_b, sgu_w_s, sgu_b, conv_w, out_norm_g, mix_w_out, norm_ffn2_g, ffn2_w_gu, ffn2_w_down, final_norm_g, loss_target, m_ada_w, m_ada_b, m_norm_ffn1_g, m_ffn1_w_gu, m_ffn1_w_down, m_norm_mix_g, m_mix_w_in, m_sgu_ln_g, m_sgu_ln_b, m_sgu_w_s, m_sgu_b, m_conv_w, m_out_norm_g, m_mix_w_out, m_norm_ffn2_g, m_ffn2_w_gu, m_ffn2_w_down, m_final_norm_g, v_ada_w, v_ada_b, v_norm_ffn1_g, v_ffn1_w_gu, v_ffn1_w_down, v_norm_mix_g, v_mix_w_in, v_sgu_ln_g, v_sgu_ln_b, v_sgu_w_s, v_sgu_b, v_conv_w, v_out_norm_g, v_mix_w_out, v_norm_ffn2_g, v_ffn2_w_gu, v_ffn2_w_down, v_final_norm_g):
    batch, seq, _ = x.shape
    tokens = batch * seq
    me = 4 * lax.axis_index("x") + 2 * lax.axis_index("y") + lax.axis_index("c")
    weights = dict(ada_w=ada_w, ada_b=ada_b, norm_ffn1_g=norm_ffn1_g, ffn1_w_gu=ffn1_w_gu, ffn1_w_down=ffn1_w_down,
                   norm_mix_g=norm_mix_g, mix_w_in=mix_w_in, sgu_ln_g=sgu_ln_g, sgu_ln_b=sgu_ln_b, sgu_w_s=sgu_w_s,
                   sgu_b=sgu_b, conv_w=conv_w, out_norm_g=out_norm_g, mix_w_out=mix_w_out, norm_ffn2_g=norm_ffn2_g,
                   ffn2_w_gu=ffn2_w_gu, ffn2_w_down=ffn2_w_down, final_norm_g=final_norm_g)
    mom1 = dict(ada_w=m_ada_w, ada_b=m_ada_b, norm_ffn1_g=m_norm_ffn1_g, ffn1_w_gu=m_ffn1_w_gu,
                ffn1_w_down=m_ffn1_w_down, norm_mix_g=m_norm_mix_g, mix_w_in=m_mix_w_in, sgu_ln_g=m_sgu_ln_g,
                sgu_ln_b=m_sgu_ln_b, sgu_w_s=m_sgu_w_s, sgu_b=m_sgu_b, conv_w=m_conv_w, out_norm_g=m_out_norm_g,
                mix_w_out=m_mix_w_out, norm_ffn2_g=m_norm_ffn2_g, ffn2_w_gu=m_ffn2_w_gu, ffn2_w_down=m_ffn2_w_down,
                final_norm_g=m_final_norm_g)
    mom2 = dict(ada_w=v_ada_w, ada_b=v_ada_b, norm_ffn1_g=v_norm_ffn1_g, ffn1_w_gu=v_ffn1_w_gu,
                ffn1_w_down=v_ffn1_w_down, norm_mix_g=v_norm_mix_g, mix_w_in=v_mix_w_in, sgu_ln_g=v_sgu_ln_g,
                sgu_ln_b=v_sgu_ln_b, sgu_w_s=v_sgu_w_s, sgu_b=v_sgu_b, conv_w=v_conv_w, out_norm_g=v_out_norm_g,
                mix_w_out=v_mix_w_out, norm_ffn2_g=v_norm_ffn2_g, ffn2_w_gu=v_ffn2_w_gu, ffn2_w_down=v_ffn2_w_down,
                final_norm_g=v_final_norm_g)

    big = ("ffn1_w_gu", "ffn1_w_down", "mix_w_in", "mix_w_out", "ffn2_w_gu", "ffn2_w_down")
    transposed = ("ffn1_w_gu", "mix_w_in", "ffn2_w_gu")
    as_rows = lambda nm, a: jnp.swapaxes(a, 1, 2) if nm in transposed else a
    shard = {(l, nm): as_rows(nm, weights[nm])[l].astype(BF16) for l in range(DEPTH) for nm in big}
    full_w = {}

    def gather_of(keys):
        return keys, _GatherRows([shard[k] for k in keys])

    def landed(plan, got):
        full_w.update(zip(plan[0], got))

    ada_cols = ada_w.shape[2]
    ada_b_cols = lax.dynamic_slice_in_dim(ada_b, me * ada_cols, ada_cols, axis=1).reshape(DEPTH, 1, ada_cols)
    plan = gather_of([(0, "ffn1_w_gu")])
    c_dev, convw_dev, ada_recv, got = _prologue(
        jnp.pad(c, ((0, 8 - batch), (0, 0))), jnp.pad(conv_w.reshape(-1), (0, 8 * LANES - conv_w.size)).reshape(8, LANES),
        ada_w, ada_b_cols, plan[1])
    landed(plan, got)
    c_all = c_dev[:, :batch].reshape(N_DEV * batch, D_MODEL)
    convw_all = convw_dev.reshape(N_DEV, -1)[:, :conv_w.size].reshape((N_DEV,) + conv_w.shape)
    convw_full = jnp.transpose(convw_all, (1, 2, 0, 3)).reshape(DEPTH, 3, D_A)
    ada_mine = jnp.transpose(ada_recv[:, :, :batch, :], (1, 2, 0, 3)).reshape(DEPTH, batch, N_MOD * D_MODEL)
    mod = ada_mine.reshape(DEPTH, batch, N_MOD, 1, D_MODEL)

    causal = jnp.tril(jnp.ones((CHUNK, CHUNK), F32))
    bd = jnp.kron(jnp.eye(MXU_N // HEAD_DIM, dtype=F32), jnp.full((HEAD_DIM, HEAD_DIM), 1.0 / HEAD_DIM, F32)).astype(BF16)
    row_vec = lambda a: a.reshape(1, -1)

    hosted_gathers = {
        (0, "ffn1"): [(0, "ffn1_w_down"), (0, "mix_w_in"), (0, "mix_w_out")],
        (0, "ffn_down1"): [(0, "ffn2_w_down")],
        (0, "mix_in"): [(0, "ffn2_w_gu")],
        (0, "ffn2"): [(1, "ffn1_w_gu"), (1, "ffn1_w_down"), (1, "mix_w_in"), (1, "mix_w_out")],
        (1, "ffn1"): [(1, "ffn2_w_gu"), (1, "ffn2_w_down")],
    }

    def hosting(l, site):
        keys = hosted_gathers.get((l, site))
        return gather_of(keys) if keys else (None, None)

    xs = x.reshape(tokens, D_MODEL)
    saved = []
    for l in range(DEPTH):
        sh1, sc1, g1, sh2, sc2, g2, sh3, sc3, g3 = [mod[l, :, k] for k in range(N_MOD)]
        wm_masked = (sgu_w_s[l] * causal[None]).astype(BF16)
        mixer_consts = dict(
            wm=jnp.concatenate([wm_masked[0::2], wm_masked[1::2]], axis=2),
            bias_full=jnp.repeat(sgu_b[l].T, HEAD_DIM, axis=1),
            lng=row_vec(jnp.tile(sgu_ln_g[l], N_HEADS)), lnb=row_vec(jnp.tile(sgu_ln_b[l], N_HEADS)),
            convw=jnp.pad(convw_full[l], ((0, 5), (0, 0))), og=row_vec(out_norm_g[l]), bd=bd)
        x0 = xs
        plan = hosting(l, "ffn1")
        if l == 0:
            h1, a1, s1, w1, got = _normmod_matmul(x0, row_vec(norm_ffn1_g[l]), 1.0 + sc1, sh1, full_w[l, "ffn1_w_gu"], seq, "ffn_up", plan[1])
            landed(plan, got)
            plan = hosting(l, "ffn_down1")
            x1, f1, got = _matmul_residual(a1, full_w[l, "ffn1_w_down"], x0, g1, 0.5, seq, "ffn_down", plan[1])
        else:
            h1, a1, s1, w1, x1, f1, got = _ffn_forward(
                x0, row_vec(norm_ffn1_g[l]), 1.0 + sc1, sh1, full_w[l, "ffn1_w_gu"], full_w[l, "ffn1_w_down"], g1, 0.5, seq, "ffn_fwd",
                comm=plan[1])
        if got:
            landed(plan, got)
        plan = hosting(l, "mix_in")
        h2, proj, ymix, x2, o2, got = _mixer_forward(
            x1, row_vec(norm_mix_g[l]), 1.0 + sc2, sh2, full_w[l, "mix_w_in"], g2, full_w[l, "mix_w_out"], seq=seq,
            name="mixer_forward", comm=plan[1], **mixer_consts)
        if got:
            landed(plan, got)
        plan = hosting(l, "ffn2")
        if l + 1 < DEPTH:
            h3, a3, s3, w3, x3, f3, got = _ffn_forward(
                x2, row_vec(norm_ffn2_g[l]), 1.0 + sc3, sh3, full_w[l, "ffn2_w_gu"], full_w[l, "ffn2_w_down"], g3, 0.5, seq, "ffn_fwd",
                comm=plan[1])
        else:
            head = (loss_target.reshape(tokens, D_MODEL), row_vec(final_norm_g))
            h3, a3, s3, w3, x3, f3, d_final_g, loss_cols, got = _ffn_forward(
                x2, row_vec(norm_ffn2_g[l]), 1.0 + sc3, sh3, full_w[l, "ffn2_w_gu"], full_w[l, "ffn2_w_down"], g3, 0.5, seq, "ffn_fwd_loss",
                loss_head=head, comm=plan[1])
        if got:
            landed(plan, got)
        saved.append(dict(x0=x0, x1=x1, x2=x2, h1=h1, h2=h2, h3=h3, a1=a1, s1=s1, w1=w1, a3=a3, s3=s3, w3=w3, f1=f1, f3=f3, o2=o2, proj=proj,
                          ymix=ymix, mixer_consts=mixer_consts, wm_rows=wm_masked.reshape(N_HEADS // 2, 2 * CHUNK, CHUNK), sc=(1.0 + sc1, 1.0 + sc2, 1.0 + sc3), gates=(g1, g2, g3)))
        xs = x3

    dx = xs

    recv = {}
    small_grads = [None] * DEPTH
    d_mod = [None] * DEPTH

    mix_names = ("out_norm_g", "sgu_ln_g", "sgu_ln_b", "sgu_w_s", "sgu_b", "conv_w")
    late_names = ("norm_ffn1_g", "norm_mix_g", "norm_ffn2_g")

    def mix_parts(l):
        return [small_grads[l][nm] for nm in mix_names]

    def late_parts(l):
        return [small_grads[l][nm] for nm in late_names] + [d_mod[l]]

    pending = []

    def scatter_later(l, nm, grad):
        pending.append(((l, nm), _ScatterRows([grad])))

    def host():
        keys, parts = [k for k, _ in pending], [p for _, p in pending]
        pending.clear()
        return keys, (_Exchanges(parts) if parts else None)

    def hosted(keys, got):
        if got:
            recv.update(zip(keys, got))

    for l in reversed(range(DEPTH)):
        sv = saved[l]
        mc = sv["mixer_consts"]
        if l + 1 < DEPTH:
            pending.append((("late", l + 1), _GatherRows([_pack_small(late_parts(l + 1))])))
        keys, comm = host()
        df3, dg3, dgu3, dx2, dsh3, dsc3, dn3, got = _ffn_backward(
            dx, sv["gates"][2], sv["f3"], sv["s3"], sv["w3"], full_w[l, "ffn2_w_down"], full_w[l, "ffn2_w_gu"], sv["x2"],
            row_vec(norm_ffn2_g[l]), sv["sc"][2], 0.5, seq, "ffn_bwd", comm)
        hosted(keys, got)
        gw_down2, _ = _weight_grad(sv["a3"], df3, seq, "grad_w_down")
        scatter_later(l, "ffn2_w_down", gw_down2)
        keys, comm = host()
        gw_gu2, got = _weight_grad(dgu3, sv["h3"], seq, "grad_w_gu", comm)
        hosted(keys, got)
        scatter_later(l, "ffn2_w_gu", gw_gu2)
        keys, comm = host()
        do2, dg2, dproj, dx1, dsh2, dsc2, dn2, d_og, d_cw, d_lng, d_lnb, d_bias, d_wm, got = _mixer_backward(
            sv["proj"], dx2, sv["gates"][1], sv["o2"], full_w[l, "mix_w_out"], sv["x1"], row_vec(norm_mix_g[l]), sv["sc"][1],
            full_w[l, "mix_w_in"], wm_rows=sv["wm_rows"], causal=causal, seq=seq, name="mixer_backward", comm=comm, **mc)
        hosted(keys, got)
        small_grads[l] = dict(
            out_norm_g=d_og, sgu_ln_g=d_lng.reshape(N_HEADS, HEAD_DIM).sum(0), sgu_ln_b=d_lnb.reshape(N_HEADS, HEAD_DIM).sum(0),
            sgu_w_s=d_wm, sgu_b=d_bias.reshape(CHUNK, N_HEADS, HEAD_DIM).sum(-1).T, conv_w=d_cw[0:3])
        gw_out, _ = _weight_grad(sv["ymix"], do2, seq, "grad_w_out")
        scatter_later(l, "mix_w_out", gw_out)
        keys, comm = host()
        gw_in, got = _weight_grad(dproj, sv["h2"], seq, "grad_w_in", comm)
        hosted(keys, got)
        scatter_later(l, "mix_w_in", gw_in)
        keys, comm = host()
        pending.append((("mix", l), _GatherRows([_pack_small(mix_parts(l))])))
        if l > 0:
            df1, dg1, dgu1, dx0, dsh1, dsc1, dn1, got = _ffn_backward(
                dx1, sv["gates"][0], sv["f1"], sv["s1"], sv["w1"], full_w[l, "ffn1_w_down"], full_w[l, "ffn1_w_gu"], sv["x0"],
                row_vec(norm_ffn1_g[l]), sv["sc"][0], 0.5, seq, "ffn_bwd", comm)
        else:
            df1, dg1, dgu1, got = _residual_backward(dx1, sv["gates"][0], sv["f1"], full_w[l, "ffn1_w_down"], 0.5, sv["s1"], sv["w1"], seq, "ffn_down_bwd", comm)
        hosted(keys, got)
        gw_down1, _ = _weight_grad(sv["a1"], df1, seq, "grad_w_down")
        scatter_later(l, "ffn1_w_down", gw_down1)
        keys, comm = host()
        gw_gu1, got = _weight_grad(dgu1, sv["h1"], seq, "grad_w_gu", comm)
        hosted(keys, got)
        scatter_later(l, "ffn1_w_gu", gw_gu1)
        if l == 0:
            keys, comm = host()
            dx0, dsh1, dsc1, dn1, got = _matmul_normmod_backward(dgu1, full_w[l, "ffn1_w_gu"], sv["x0"], dx1, row_vec(norm_ffn1_g[l]), sv["sc"][0], seq, "ffn_up_bwd", comm)
            hosted(keys, got)
        dx = dx0
        small_grads[l].update(norm_ffn1_g=dn1, norm_mix_g=dn2, norm_ffn2_g=dn3)
        d_mod[l] = jnp.concatenate([dsh1, dsc1, dg1, dsh2, dsc2, dg2, dsh3, dsc3, dg3], axis=1)
    grad_x = dx.reshape(batch, seq, D_MODEL)

    grad_big, delta, new_m, new_v = {}, {}, {}, {}
    for nm in big:
        results = _adamw_rows([recv[l, nm] for l in range(DEPTH)], as_rows(nm, weights[nm]), as_rows(nm, mom1[nm]),
                              as_rows(nm, mom2[nm]), "adamw_" + nm)
        grad_big[nm], delta[nm], new_m[nm], new_v[nm] = [as_rows(nm, r) for r in results]

    last_parts = late_parts(0) + [d_final_g, loss_cols]
    last_shapes = [p.shape for p in last_parts]
    packed_all, packed_sum = _all_gather_small(_pack_small(last_parts), "reduce_small")
    late_sum = {0: _unpack_small(packed_sum, last_shapes)}
    d_mod_dev = {0: _unpack_small(packed_all, last_shapes, lead=(N_DEV,))[len(late_names)]}
    mix_sum = {}
    for l in range(DEPTH):
        gathered = recv["mix", l].reshape(N_DEV, -1, LANES)
        mix_sum[l] = _unpack_small(_sum_gathered(gathered, "sum_mix"), [p.shape for p in mix_parts(l)])
        if l > 0:
            shapes_l = [p.shape for p in late_parts(l)]
            gathered = recv["late", l].reshape(N_DEV, -1, LANES)
            late_sum[l] = _unpack_small(_sum_gathered(gathered, "sum_late"), shapes_l)
            d_mod_dev[l] = _unpack_small(gathered, shapes_l, lead=(N_DEV,))[len(late_names)]
    grad_small = {}
    for group, names in ((mix_sum, mix_names), (late_sum, late_names)):
        for k, nm in enumerate(names):
            grad_small[nm] = jnp.stack([group[l][k] for l in range(DEPTH)]).reshape(
                (DEPTH, 3, D_A) if nm == "conv_w" else weights[nm].shape)
    grad_small["conv_w"] = lax.dynamic_slice_in_dim(grad_small["conv_w"], me * conv_w.shape[2], conv_w.shape[2], axis=2)
    grad_small["final_norm_g"] = late_sum[0][len(late_names) + 1].reshape(final_norm_g.shape)
    loss = jnp.sum(late_sum[0][len(late_names) + 2])
    d_ada_all = jnp.stack([d_mod_dev[l] for l in range(DEPTH)]).reshape(DEPTH, N_DEV * batch, N_MOD * D_MODEL)
    d_ada_cols = lax.dynamic_slice_in_dim(d_ada_all, me * ada_cols, ada_cols, axis=2)
    g_ada_w, g_ada_b = _ada_backward(c_all, d_ada_cols, d_ada_all)

    grads = dict(grad_big)
    grads.update(grad_small)
    grads["ada_w"] = g_ada_w
    grads["ada_b"] = g_ada_b.reshape(ada_b.shape)

    names = ("ada_w", "ada_b", "norm_ffn1_g", "ffn1_w_gu", "ffn1_w_down", "norm_mix_g", "mix_w_in", "sgu_ln_g",
             "sgu_ln_b", "sgu_w_s", "sgu_b", "conv_w", "out_norm_g", "mix_w_out", "norm_ffn2_g", "ffn2_w_gu",
             "ffn2_w_down", "final_norm_g")
    delta["ada_w"], new_m["ada_w"], new_v["ada_w"] = _adamw_nd(ada_w, grads["ada_w"], m_ada_w, v_ada_w, "adamw_ada_w")
    rest = [nm for nm in names if nm not in big and nm != "ada_w"]
    pick = lambda src: [src[nm] for nm in rest]
    for nm, d_k, m_k, v_k in zip(rest, *_adamw_many(pick(weights), pick(grads), pick(mom1), pick(mom2), "adamw_small")):
        delta[nm], new_m[nm], new_v[nm] = d_k, m_k, v_k

    return (loss, grad_x, *[grads[nm] for nm in names], *[delta[nm] for nm in names],
            *[new_m[nm] for nm in names], *[new_v[nm] for nm in names])
```
